```python
import jax, jax.numpy as jnp
from jax import lax
import numpy as np

D_MODEL = 2048
BATCH = 8
SEQ = 4096
DEPTH = 2

POOL_WINDOWS = (2, 4, 8, 16)
POOL_GROUPS = 4
POOL_GROUP_DIM = D_MODEL // 16
POOL_WIDTH = POOL_GROUPS * POOL_GROUP_DIM
GDN_HEAD_DIM = 128
GDN_HEADS = (3 * D_MODEL // 8) // GDN_HEAD_DIM
GDN_WIDTH = GDN_HEADS * GDN_HEAD_DIM
GDN_CONV = 4
GDN_CHUNK = 64
LRU_BLOCK_DIM = 128
LRU_WIDTH = D_MODEL - POOL_WIDTH - GDN_WIDTH
LRU_BLOCKS = LRU_WIDTH // LRU_BLOCK_DIM
LRU_CONV = 4
LRU_C = 8.0
IN_SIZES = (POOL_WIDTH, GDN_WIDTH, GDN_WIDTH, GDN_WIDTH, GDN_WIDTH, GDN_HEADS, GDN_HEADS, LRU_WIDTH, LRU_WIDTH)
IN_COLS = POOL_WIDTH + 4 * GDN_WIDTH + 2 * GDN_HEADS + 2 * LRU_WIDTH
D_FF = 3 * D_MODEL
FFN_CONV = 3
EPS = 1e-6

kernel_name = 'hymba_pool_gdn_rglru_convffn'


def rmsnorm(x, w):
    xf = x.astype(jnp.float32)
    y = xf * lax.rsqrt(jnp.mean(xf * xf, axis=-1, keepdims=True) + EPS)
    return (y * w.astype(jnp.float32)).astype(x.dtype)


def l2norm(t):
    return t * lax.rsqrt(jnp.sum(t * t, axis=-1, keepdims=True) + EPS)


def causal_dwconv(x, w):
    k = w.shape[0]
    return lax.conv_general_dilated(
        x, w[:, None, :].astype(x.dtype), window_strides=(1,), padding=[(k - 1, 0)],
        dimension_numbers=('NWC', 'WIO', 'NWC'), feature_group_count=x.shape[-1])


def split_points():
    return np.cumsum(np.array(IN_SIZES))[:-1].tolist()


def pool_mixer(u, w, b, scale):
    bsz, s, _ = u.shape
    uf = u.astype(jnp.float32).reshape(bsz, s, POOL_GROUPS, POOL_GROUP_DIM)
    cs = jnp.cumsum(uf, axis=1)
    pos = jnp.arange(s)
    outs = []
    for g, win in enumerate(POOL_WINDOWS):
        c = cs[:, :, g]
        lag = jnp.pad(c, ((0, 0), (win, 0), (0, 0)))[:, :s]
        cnt = jnp.minimum(pos + 1, win).astype(jnp.float32)[None, :, None]
        outs.append((c - lag) / cnt - uf[:, :, g])
    d = jnp.stack(outs, axis=2).astype(u.dtype)
    y = jnp.einsum('bsgc,gcd->bsgd', d, w) + b
    return y.reshape(bsz, s, POOL_WIDTH) * scale


def gated_deltanet(q, k, v, z, a, bt, conv_w, a_log, dt_bias, norm_w):
    bsz, s, _ = q.shape
    H, Dh, C = GDN_HEADS, GDN_HEAD_DIM, GDN_CHUNK
    N = s // C
    out_dtype = z.dtype
    qkv = jax.nn.silu(causal_dwconv(jnp.concatenate([q, k, v], axis=-1), conv_w)).astype(jnp.float32)
    q, k, v = jnp.split(qkv, 3, axis=-1)

    def heads(t):
        return t.reshape(bsz, N, C, H, Dh).transpose(0, 3, 1, 2, 4)

    def per_head(t):
        return t.reshape(bsz, N, C, H).transpose(0, 3, 1, 2)

    q = l2norm(heads(q)) * (Dh ** -0.5)
    k = l2norm(heads(k))
    v = heads(v)
    beta = jax.nn.sigmoid(per_head(bt.astype(jnp.float32)))
    g = -jnp.exp(a_log.astype(jnp.float32)) * jax.nn.softplus(a.astype(jnp.float32) + dt_bias.astype(jnp.float32))
    g = jnp.cumsum(per_head(g), axis=-1)

    causal = jnp.tril(jnp.ones((C, C), dtype=bool))
    strict = jnp.tril(jnp.ones((C, C), dtype=bool), -1)
    decay = jnp.exp(jnp.where(causal, g[..., :, None] - g[..., None, :], -jnp.inf))
    kk = jnp.einsum('bhncd,bhnsd->bhncs', k, k)
    m = jnp.where(strict, beta[..., None] * kk * decay, 0.0) + jnp.eye(C, dtype=jnp.float32)
    rhs = jnp.concatenate([k * (beta * jnp.exp(g))[..., None], v * beta[..., None]], axis=-1)
    wu = lax.linalg.triangular_solve(m, rhs, left_side=True, lower=True)
    w_c, u_c = wu[..., :Dh], wu[..., Dh:]
    attn = jnp.einsum('bhncd,bhnsd->bhncs', q, k) * decay
    g_last = g[..., -1]
    q_dec = q * jnp.exp(g)[..., None]
    k_dec = k * jnp.exp(g_last[..., None] - g)[..., None]

    def step(state, xs):
        qd, kd, wc, uc, at, gl = xs
        v_new = uc - jnp.einsum('bhcd,bhde->bhce', wc, state)
        o = jnp.einsum('bhcd,bhde->bhce', qd, state) + jnp.einsum('bhcs,bhse->bhce', at, v_new)
        state = state * jnp.exp(gl)[..., None, None] + jnp.einsum('bhcd,bhce->bhde', kd, v_new)
        return state, o

    xs = tuple(jnp.moveaxis(t, 2, 0) for t in (q_dec, k_dec, w_c, u_c, attn, g_last))
    s0 = jnp.zeros((bsz, H, Dh, Dh), jnp.float32)
    _, o = lax.scan(step, s0, xs)
    o = o.transpose(1, 0, 3, 2, 4).reshape(bsz, s, H, Dh)
    zf = z.astype(jnp.float32).reshape(bsz, s, H, Dh)
    o = o * lax.rsqrt(jnp.mean(o * o, axis=-1, keepdims=True) + EPS) * norm_w.astype(jnp.float32) * jax.nn.silu(zf)
    return o.reshape(bsz, s, GDN_WIDTH).astype(out_dtype)


def rglru_mixer(xb, gate, conv_w, conv_b, wa, ba, wx, bx, lam):
    bsz, s, _ = xb.shape
    xc = causal_dwconv(xb, conv_w) + conv_b
    xh = xc.reshape(bsz, s, LRU_BLOCKS, LRU_BLOCK_DIM)
    r = jax.nn.sigmoid((jnp.einsum('bshc,hcd->bshd', xh, wa).reshape(bsz, s, LRU_WIDTH) + ba).astype(jnp.float32))
    i = jax.nn.sigmoid((jnp.einsum('bshc,hcd->bshd', xh, wx).reshape(bsz, s, LRU_WIDTH) + bx).astype(jnp.float32))
    log_a = -LRU_C * r * jax.nn.softplus(-lam.astype(jnp.float32))
    a = jnp.exp(log_a)
    mult = jnp.sqrt(-jnp.expm1(2.0 * log_a))
    pos = jnp.arange(s)[None, :, None]
    mult = jnp.where(pos == 0, 1.0, mult)
    b_in = mult * i * xc.astype(jnp.float32)

    def combine(lhs, rhs):
        return (lhs[0] * rhs[0], rhs[0] * lhs[1] + rhs[1])

    _, h = lax.associative_scan(combine, (a, b_in), axis=1)
    y = h * jax.nn.gelu(gate.astype(jnp.float32))
    return y.astype(xb.dtype)


def conv_ffn(h, w_up, conv_w, w_down):
    up = h @ w_up
    gate, val = jnp.split(up, 2, axis=-1)
    gate = causal_dwconv(gate, conv_w)
    return (jax.nn.gelu(gate) * val) @ w_down


def _fwd_setup_inputs(seed: int = 0) -> dict:
    key = jax.random.key(seed)
    ks = jax.random.split(key, 24)
    f32 = jnp.float32
    L = DEPTH
    nrm = lambda k, shape, sc: jax.random.normal(k, shape, f32) * sc
    gain = lambda k, shape: 1.0 + 0.02 * jax.random.normal(k, shape, f32)
    dt = jnp.exp(jax.random.uniform(ks[8], (L, GDN_HEADS), f32, np.log(1e-3), np.log(1e-1)))
    a0 = jax.random.uniform(ks[17], (L, LRU_WIDTH), f32, 0.9, 0.999) ** (1.0 / LRU_C)
    return {
        'x': nrm(ks[0], (BATCH, SEQ, D_MODEL), 1.0),
        'norm1_w': gain(ks[1], (L, D_MODEL)),
        'w_in': nrm(ks[2], (L, D_MODEL, IN_COLS), D_MODEL ** -0.5),
        'pool_w': nrm(ks[3], (L, POOL_GROUPS, POOL_GROUP_DIM, POOL_GROUP_DIM), POOL_GROUP_DIM ** -0.5),
        'pool_b': nrm(ks[4], (L, POOL_GROUPS, POOL_GROUP_DIM), 0.01),
        'pool_scale': 0.5 + 0.05 * jax.random.normal(ks[5], (L, POOL_WIDTH), f32),
        'gdn_conv_w': nrm(ks[6], (L, GDN_CONV, 3 * GDN_WIDTH), GDN_CONV ** -0.5),
        'gdn_a_log': jnp.log(jax.random.uniform(ks[7], (L, GDN_HEADS), f32, 1.0, 16.0)),
        'gdn_dt_bias': dt + jnp.log(-jnp.expm1(-dt)),
        'gdn_norm_w': gain(ks[9], (L, GDN_HEAD_DIM)),
        'lru_conv_w': nrm(ks[10], (L, LRU_CONV, LRU_WIDTH), LRU_CONV ** -0.5),
        'lru_conv_b': nrm(ks[11], (L, LRU_WIDTH), 0.01),
        'lru_wa': nrm(ks[12], (L, LRU_BLOCKS, LRU_BLOCK_DIM, LRU_BLOCK_DIM), LRU_BLOCK_DIM ** -0.5),
        'lru_ba': nrm(ks[13], (L, LRU_WIDTH), 0.01),
        'lru_wx': nrm(ks[14], (L, LRU_BLOCKS, LRU_BLOCK_DIM, LRU_BLOCK_DIM), LRU_BLOCK_DIM ** -0.5),
        'lru_bx': nrm(ks[15], (L, LRU_WIDTH), 0.01),
        'lru_lambda': jnp.log(a0 / (1.0 - a0)),
        'w_out': nrm(ks[16], (L, D_MODEL, D_MODEL), D_MODEL ** -0.5),
        'norm2_w': gain(ks[18], (L, D_MODEL)),
        'ffn_up': nrm(ks[19], (L, D_MODEL, 2 * D_FF), D_MODEL ** -0.5),
        'ffn_conv_w': nrm(ks[20], (L, FFN_CONV, D_FF), FFN_CONV ** -0.5),
        'ffn_down': nrm(ks[21], (L, D_FF, D_MODEL), D_FF ** -0.5),
        'final_norm_w': gain(ks[22], (D_MODEL,)),
    }


def _fwd_reference(x, norm1_w, w_in, pool_w, pool_b, pool_scale, gdn_conv_w, gdn_a_log, gdn_dt_bias,
              gdn_norm_w, lru_conv_w, lru_conv_b, lru_wa, lru_ba, lru_wx, lru_bx, lru_lambda,
              w_out, norm2_w, ffn_up, ffn_conv_w, ffn_down, final_norm_w):
    cuts = split_points()
    for l in range(DEPTH):
        h = rmsnorm(x, norm1_w[l])
        proj = h @ w_in[l]
        u_pool, q, k, v, z, a, bt, xr, gr = jnp.split(proj, cuts, axis=-1)
        y_pool = pool_mixer(u_pool, pool_w[l], pool_b[l], pool_scale[l])
        y_gdn = gated_deltanet(q, k, v, z, a, bt, gdn_conv_w[l], gdn_a_log[l], gdn_dt_bias[l], gdn_norm_w[l])
        y_lru = rglru_mixer(xr, gr, lru_conv_w[l], lru_conv_b[l], lru_wa[l], lru_ba[l],
                            lru_wx[l], lru_bx[l], lru_lambda[l])
        mixed = jnp.concatenate([y_pool.astype(x.dtype), y_gdn.astype(x.dtype), y_lru.astype(x.dtype)], axis=-1)
        x = x + mixed @ w_out[l]
        x = x + conv_ffn(rmsnorm(x, norm2_w[l]), ffn_up[l], ffn_conv_w[l], ffn_down[l])
    return rmsnorm(x, final_norm_w)


import jax as _jax
import jax.numpy as _jnp

TWIN_FORMAT = 'train_step'
FWD_PARAMS = ['x', 'norm1_w', 'w_in', 'pool_w', 'pool_b', 'pool_scale', 'gdn_conv_w', 'gdn_a_log', 'gdn_dt_bias', 'gdn_norm_w', 'lru_conv_w', 'lru_conv_b', 'lru_wa', 'lru_ba', 'lru_wx', 'lru_bx', 'lru_lambda', 'w_out', 'norm2_w', 'ffn_up', 'ffn_conv_w', 'ffn_down', 'final_norm_w']
TWIN_WEIGHTS = ['norm1_w', 'w_in', 'pool_w', 'pool_b', 'pool_scale', 'gdn_conv_w', 'gdn_a_log', 'gdn_dt_bias', 'gdn_norm_w', 'lru_conv_w', 'lru_conv_b', 'lru_wa', 'lru_ba', 'lru_wx', 'lru_bx', 'lru_lambda', 'w_out', 'norm2_w', 'ffn_up', 'ffn_conv_w', 'ffn_down', 'final_norm_w']
TWIN_DIFF_INPUT = 'x'
TWIN_INPUTS = ['x', 'norm1_w', 'w_in', 'pool_w', 'pool_b', 'pool_scale', 'gdn_conv_w', 'gdn_a_log', 'gdn_dt_bias', 'gdn_norm_w', 'lru_conv_w', 'lru_conv_b', 'lru_wa', 'lru_ba', 'lru_wx', 'lru_bx', 'lru_lambda', 'w_out', 'norm2_w', 'ffn_up', 'ffn_conv_w', 'ffn_down', 'final_norm_w', 'loss_target', 'm_norm1_w', 'm_w_in', 'm_pool_w', 'm_pool_b', 'm_pool_scale', 'm_gdn_conv_w', 'm_gdn_a_log', 'm_gdn_dt_bias', 'm_gdn_norm_w', 'm_lru_conv_w', 'm_lru_conv_b', 'm_lru_wa', 'm_lru_ba', 'm_lru_wx', 'm_lru_bx', 'm_lru_lambda', 'm_w_out', 'm_norm2_w', 'm_ffn_up', 'm_ffn_conv_w', 'm_ffn_down', 'm_final_norm_w', 'v_norm1_w', 'v_w_in', 'v_pool_w', 'v_pool_b', 'v_pool_scale', 'v_gdn_conv_w', 'v_gdn_a_log', 'v_gdn_dt_bias', 'v_gdn_norm_w', 'v_lru_conv_w', 'v_lru_conv_b', 'v_lru_wa', 'v_lru_ba', 'v_lru_wx', 'v_lru_bx', 'v_lru_lambda', 'v_w_out', 'v_norm2_w', 'v_ffn_up', 'v_ffn_conv_w', 'v_ffn_down', 'v_final_norm_w']
TWIN_OUTPUTS = ['loss', 'grad_x', 'grad_norm1_w', 'grad_w_in', 'grad_pool_w', 'grad_pool_b', 'grad_pool_scale', 'grad_gdn_conv_w', 'grad_gdn_a_log', 'grad_gdn_dt_bias', 'grad_gdn_norm_w', 'grad_lru_conv_w', 'grad_lru_conv_b', 'grad_lru_wa', 'grad_lru_ba', 'grad_lru_wx', 'grad_lru_bx', 'grad_lru_lambda', 'grad_w_out', 'grad_norm2_w', 'grad_ffn_up', 'grad_ffn_conv_w', 'grad_ffn_down', 'grad_final_norm_w', 'delta_norm1_w', 'delta_w_in', 'delta_pool_w', 'delta_pool_b', 'delta_pool_scale', 'delta_gdn_conv_w', 'delta_gdn_a_log', 'delta_gdn_dt_bias', 'delta_gdn_norm_w', 'delta_lru_conv_w', 'delta_lru_conv_b', 'delta_lru_wa', 'delta_lru_ba', 'delta_lru_wx', 'delta_lru_bx', 'delta_lru_lambda', 'delta_w_out', 'delta_norm2_w', 'delta_ffn_up', 'delta_ffn_conv_w', 'delta_ffn_down', 'delta_final_norm_w', 'new_m_norm1_w', 'new_m_w_in', 'new_m_pool_w', 'new_m_pool_b', 'new_m_pool_scale', 'new_m_gdn_conv_w', 'new_m_gdn_a_log', 'new_m_gdn_dt_bias', 'new_m_gdn_norm_w', 'new_m_lru_conv_w', 'new_m_lru_conv_b', 'new_m_lru_wa', 'new_m_lru_ba', 'new_m_lru_wx', 'new_m_lru_bx', 'new_m_lru_lambda', 'new_m_w_out', 'new_m_norm2_w', 'new_m_ffn_up', 'new_m_ffn_conv_w', 'new_m_ffn_down', 'new_m_final_norm_w', 'new_v_norm1_w', 'new_v_w_in', 'new_v_pool_w', 'new_v_pool_b', 'new_v_pool_scale', 'new_v_gdn_conv_w', 'new_v_gdn_a_log', 'new_v_gdn_dt_bias', 'new_v_gdn_norm_w', 'new_v_lru_conv_w', 'new_v_lru_conv_b', 'new_v_lru_wa', 'new_v_lru_ba', 'new_v_lru_wx', 'new_v_lru_bx', 'new_v_lru_lambda', 'new_v_w_out', 'new_v_norm2_w', 'new_v_ffn_up', 'new_v_ffn_conv_w', 'new_v_ffn_down', 'new_v_final_norm_w']
TWIN_LEAF_KINDS = {'loss': 'loss', 'grad_x': 'grad_x', 'grad_norm1_w': 'grad_w', 'grad_w_in': 'grad_w', 'grad_pool_w': 'grad_w', 'grad_pool_b': 'grad_w', 'grad_pool_scale': 'grad_w', 'grad_gdn_conv_w': 'grad_w', 'grad_gdn_a_log': 'grad_w', 'grad_gdn_dt_bias': 'grad_w', 'grad_gdn_norm_w': 'grad_w', 'grad_lru_conv_w': 'grad_w', 'grad_lru_conv_b': 'grad_w', 'grad_lru_wa': 'grad_w', 'grad_lru_ba': 'grad_w', 'grad_lru_wx': 'grad_w', 'grad_lru_bx': 'grad_w', 'grad_lru_lambda': 'grad_w', 'grad_w_out': 'grad_w', 'grad_norm2_w': 'grad_w', 'grad_ffn_up': 'grad_w', 'grad_ffn_conv_w': 'grad_w', 'grad_ffn_down': 'grad_w', 'grad_final_norm_w': 'grad_w', 'delta_norm1_w': 'delta_w', 'delta_w_in': 'delta_w', 'delta_pool_w': 'delta_w', 'delta_pool_b': 'delta_w', 'delta_pool_scale': 'delta_w', 'delta_gdn_conv_w': 'delta_w', 'delta_gdn_a_log': 'delta_w', 'delta_gdn_dt_bias': 'delta_w', 'delta_gdn_norm_w': 'delta_w', 'delta_lru_conv_w': 'delta_w', 'delta_lru_conv_b': 'delta_w', 'delta_lru_wa': 'delta_w', 'delta_lru_ba': 'delta_w', 'delta_lru_wx': 'delta_w', 'delta_lru_bx': 'delta_w', 'delta_lru_lambda': 'delta_w', 'delta_w_out': 'delta_w', 'delta_norm2_w': 'delta_w', 'delta_ffn_up': 'delta_w', 'delta_ffn_conv_w': 'delta_w', 'delta_ffn_down': 'delta_w', 'delta_final_norm_w': 'delta_w', 'new_m_norm1_w': 'new_m', 'new_m_w_in': 'new_m', 'new_m_pool_w': 'new_m', 'new_m_pool_b': 'new_m', 'new_m_pool_scale': 'new_m', 'new_m_gdn_conv_w': 'new_m', 'new_m_gdn_a_log': 'new_m', 'new_m_gdn_dt_bias': 'new_m', 'new_m_gdn_norm_w': 'new_m', 'new_m_lru_conv_w': 'new_m', 'new_m_lru_conv_b': 'new_m', 'new_m_lru_wa': 'new_m', 'new_m_lru_ba': 'new_m', 'new_m_lru_wx': 'new_m', 'new_m_lru_bx': 'new_m', 'new_m_lru_lambda': 'new_m', 'new_m_w_out': 'new_m', 'new_m_norm2_w': 'new_m', 'new_m_ffn_up': 'new_m', 'new_m_ffn_conv_w': 'new_m', 'new_m_ffn_down': 'new_m', 'new_m_final_norm_w': 'new_m', 'new_v_norm1_w': 'new_v', 'new_v_w_in': 'new_v', 'new_v_pool_w': 'new_v', 'new_v_pool_b': 'new_v', 'new_v_pool_scale': 'new_v', 'new_v_gdn_conv_w': 'new_v', 'new_v_gdn_a_log': 'new_v', 'new_v_gdn_dt_bias': 'new_v', 'new_v_gdn_norm_w': 'new_v', 'new_v_lru_conv_w': 'new_v', 'new_v_lru_conv_b': 'new_v', 'new_v_lru_wa': 'new_v', 'new_v_lru_ba': 'new_v', 'new_v_lru_wx': 'new_v', 'new_v_lru_bx': 'new_v', 'new_v_lru_lambda': 'new_v', 'new_v_w_out': 'new_v', 'new_v_norm2_w': 'new_v', 'new_v_ffn_up': 'new_v', 'new_v_ffn_conv_w': 'new_v', 'new_v_ffn_down': 'new_v', 'new_v_final_norm_w': 'new_v'}


def _forward(args):
    return _fwd_reference(*[args[k] for k in FWD_PARAMS])


def _output_shape():
    def fwd():
        inp = _fwd_setup_inputs(0)
        return _fwd_reference(*[inp[k] for k in FWD_PARAMS])
    out = _jax.eval_shape(fwd)
    return out.shape, out.dtype

N_MICROBATCH = 1
ADAM_LR = 0.001
ADAM_B1 = 0.9
ADAM_B2 = 0.999
ADAM_EPS = 1e-08
ADAM_WD = 0.01
ADAM_STEP = 10
PER_EXAMPLE_BATCH_AXIS = {'x': 0, 'loss_target': 0}
SHARED_INPUTS = []
_WEIGHT_DTYPES = {'norm1_w': _jnp.float32, 'w_in': _jnp.float32, 'pool_w': _jnp.float32, 'pool_b': _jnp.float32, 'pool_scale': _jnp.float32, 'gdn_conv_w': _jnp.float32, 'gdn_a_log': _jnp.float32, 'gdn_dt_bias': _jnp.float32, 'gdn_norm_w': _jnp.float32, 'lru_conv_w': _jnp.float32, 'lru_conv_b': _jnp.float32, 'lru_wa': _jnp.float32, 'lru_ba': _jnp.float32, 'lru_wx': _jnp.float32, 'lru_bx': _jnp.float32, 'lru_lambda': _jnp.float32, 'w_out': _jnp.float32, 'norm2_w': _jnp.float32, 'ffn_up': _jnp.float32, 'ffn_conv_w': _jnp.float32, 'ffn_down': _jnp.float32, 'final_norm_w': _jnp.float32}
MOMENT_SCALE = {'norm1_w': 6.938092e-02, 'w_in': 4.288033e-02, 'pool_w': 3.919392e-02, 'pool_b': 6.448573e-02, 'pool_scale': 7.913117e-02, 'gdn_conv_w': 4.177818e-02, 'gdn_a_log': 1.777420e-01, 'gdn_dt_bias': 1.742042e-01, 'gdn_norm_w': 1.443786e-01, 'lru_conv_w': 4.537601e-02, 'lru_conv_b': 4.564790e-01, 'lru_wa': 1.316029e-02, 'lru_ba': 1.095716e-02, 'lru_wx': 2.383377e-02, 'lru_bx': 1.644819e-02, 'lru_lambda': 2.116389e-02, 'w_out': 4.679996e-02, 'norm2_w': 6.671191e-02, 'ffn_up': 2.700489e-02, 'ffn_conv_w': 2.765467e-02, 'ffn_down': 4.601608e-02, 'final_norm_w': 1.598782e+01}


def _to_microbatches(a, axis):
    t = _jnp.moveaxis(a, axis, 0)
    t = t.reshape((N_MICROBATCH, t.shape[0] // N_MICROBATCH) + t.shape[1:])
    return _jnp.moveaxis(t, 1, axis + 1)


def setup_inputs(seed: int = 0) -> dict:
    inp = _fwd_setup_inputs(seed)
    key = _jax.random.fold_in(_jax.random.key(seed), 7919)
    shape, _ = _output_shape()
    out = dict(inp)
    out["loss_target"] = _jax.random.normal(_jax.random.fold_in(key, 0), shape, _jnp.float32)
    for i, name in enumerate(TWIN_WEIGHTS):
        w = inp[name].astype(_jnp.float32)
        if MOMENT_SCALE is None:
            s = _jnp.sqrt(_jnp.mean(_jnp.square(w)) + 1e-30)
        else:
            s = MOMENT_SCALE[name]
        km, kv = _jax.random.split(_jax.random.fold_in(key, i + 1))
        out[name] = w
        out["m_" + name] = s * _jax.random.normal(km, w.shape, _jnp.float32)
        out["v_" + name] = (s * s) * _jax.random.uniform(kv, w.shape, _jnp.float32, 0.5, 1.5)
    if N_MICROBATCH > 1:
        for name, axis in PER_EXAMPLE_BATCH_AXIS.items():
            out[name] = _to_microbatches(out[name], axis)
    return {'x': out['x'], 'norm1_w': out['norm1_w'], 'w_in': out['w_in'], 'pool_w': out['pool_w'], 'pool_b': out['pool_b'], 'pool_scale': out['pool_scale'], 'gdn_conv_w': out['gdn_conv_w'], 'gdn_a_log': out['gdn_a_log'], 'gdn_dt_bias': out['gdn_dt_bias'], 'gdn_norm_w': out['gdn_norm_w'], 'lru_conv_w': out['lru_conv_w'], 'lru_conv_b': out['lru_conv_b'], 'lru_wa': out['lru_wa'], 'lru_ba': out['lru_ba'], 'lru_wx': out['lru_wx'], 'lru_bx': out['lru_bx'], 'lru_lambda': out['lru_lambda'], 'w_out': out['w_out'], 'norm2_w': out['norm2_w'], 'ffn_up': out['ffn_up'], 'ffn_conv_w': out['ffn_conv_w'], 'ffn_down': out['ffn_down'], 'final_norm_w': out['final_norm_w'], 'loss_target': out['loss_target'], 'm_norm1_w': out['m_norm1_w'], 'm_w_in': out['m_w_in'], 'm_pool_w': out['m_pool_w'], 'm_pool_b': out['m_pool_b'], 'm_pool_scale': out['m_pool_scale'], 'm_gdn_conv_w': out['m_gdn_conv_w'], 'm_gdn_a_log': out['m_gdn_a_log'], 'm_gdn_dt_bias': out['m_gdn_dt_bias'], 'm_gdn_norm_w': out['m_gdn_norm_w'], 'm_lru_conv_w': out['m_lru_conv_w'], 'm_lru_conv_b': out['m_lru_conv_b'], 'm_lru_wa': out['m_lru_wa'], 'm_lru_ba': out['m_lru_ba'], 'm_lru_wx': out['m_lru_wx'], 'm_lru_bx': out['m_lru_bx'], 'm_lru_lambda': out['m_lru_lambda'], 'm_w_out': out['m_w_out'], 'm_norm2_w': out['m_norm2_w'], 'm_ffn_up': out['m_ffn_up'], 'm_ffn_conv_w': out['m_ffn_conv_w'], 'm_ffn_down': out['m_ffn_down'], 'm_final_norm_w': out['m_final_norm_w'], 'v_norm1_w': out['v_norm1_w'], 'v_w_in': out['v_w_in'], 'v_pool_w': out['v_pool_w'], 'v_pool_b': out['v_pool_b'], 'v_pool_scale': out['v_pool_scale'], 'v_gdn_conv_w': out['v_gdn_conv_w'], 'v_gdn_a_log': out['v_gdn_a_log'], 'v_gdn_dt_bias': out['v_gdn_dt_bias'], 'v_gdn_norm_w': out['v_gdn_norm_w'], 'v_lru_conv_w': out['v_lru_conv_w'], 'v_lru_conv_b': out['v_lru_conv_b'], 'v_lru_wa': out['v_lru_wa'], 'v_lru_ba': out['v_lru_ba'], 'v_lru_wx': out['v_lru_wx'], 'v_lru_bx': out['v_lru_bx'], 'v_lru_lambda': out['v_lru_lambda'], 'v_w_out': out['v_w_out'], 'v_norm2_w': out['v_norm2_w'], 'v_ffn_up': out['v_ffn_up'], 'v_ffn_conv_w': out['v_ffn_conv_w'], 'v_ffn_down': out['v_ffn_down'], 'v_final_norm_w': out['v_final_norm_w']}


def _loss(weights, diff, rest, loss_target):
    with _jax.named_scope("forward"):
        args = {**rest, TWIN_DIFF_INPUT: diff, **{k: w.astype(_WEIGHT_DTYPES[k]) for k, w in weights.items()}}
        y = _forward(args)
    with _jax.named_scope("loss_head"):
        err = _jnp.square(y.astype(_jnp.float32) - loss_target)
        return 0.5 * _jnp.sum(_jnp.mean(err, axis=-1)) if err.ndim else 0.5 * err


def _adamw(w, g, m, v):
    m = ADAM_B1 * m + (1.0 - ADAM_B1) * g
    v = ADAM_B2 * v + (1.0 - ADAM_B2) * _jnp.square(g)
    m_hat = m / (1.0 - ADAM_B1 ** ADAM_STEP)
    v_hat = v / (1.0 - ADAM_B2 ** ADAM_STEP)
    delta = -ADAM_LR * (m_hat / (_jnp.sqrt(v_hat) + ADAM_EPS) + ADAM_WD * w)
    return delta, m, v


def reference(x, norm1_w, w_in, pool_w, pool_b, pool_scale, gdn_conv_w, gdn_a_log, gdn_dt_bias, gdn_norm_w, lru_conv_w, lru_conv_b, lru_wa, lru_ba, lru_wx, lru_bx, lru_lambda, w_out, norm2_w, ffn_up, ffn_conv_w, ffn_down, final_norm_w, loss_target, m_norm1_w, m_w_in, m_pool_w, m_pool_b, m_pool_scale, m_gdn_conv_w, m_gdn_a_log, m_gdn_dt_bias, m_gdn_norm_w, m_lru_conv_w, m_lru_conv_b, m_lru_wa, m_lru_ba, m_lru_wx, m_lru_bx, m_lru_lambda, m_w_out, m_norm2_w, m_ffn_up, m_ffn_conv_w, m_ffn_down, m_final_norm_w, v_norm1_w, v_w_in, v_pool_w, v_pool_b, v_pool_scale, v_gdn_conv_w, v_gdn_a_log, v_gdn_dt_bias, v_gdn_norm_w, v_lru_conv_w, v_lru_conv_b, v_lru_wa, v_lru_ba, v_lru_wx, v_lru_bx, v_lru_lambda, v_w_out, v_norm2_w, v_ffn_up, v_ffn_conv_w, v_ffn_down, v_final_norm_w):
    given = dict(x=x, norm1_w=norm1_w, w_in=w_in, pool_w=pool_w, pool_b=pool_b, pool_scale=pool_scale, gdn_conv_w=gdn_conv_w, gdn_a_log=gdn_a_log, gdn_dt_bias=gdn_dt_bias, gdn_norm_w=gdn_norm_w, lru_conv_w=lru_conv_w, lru_conv_b=lru_conv_b, lru_wa=lru_wa, lru_ba=lru_ba, lru_wx=lru_wx, lru_bx=lru_bx, lru_lambda=lru_lambda, w_out=w_out, norm2_w=norm2_w, ffn_up=ffn_up, ffn_conv_w=ffn_conv_w, ffn_down=ffn_down, final_norm_w=final_norm_w, loss_target=loss_target, m_norm1_w=m_norm1_w, m_w_in=m_w_in, m_pool_w=m_pool_w, m_pool_b=m_pool_b, m_pool_scale=m_pool_scale, m_gdn_conv_w=m_gdn_conv_w, m_gdn_a_log=m_gdn_a_log, m_gdn_dt_bias=m_gdn_dt_bias, m_gdn_norm_w=m_gdn_norm_w, m_lru_conv_w=m_lru_conv_w, m_lru_conv_b=m_lru_conv_b, m_lru_wa=m_lru_wa, m_lru_ba=m_lru_ba, m_lru_wx=m_lru_wx, m_lru_bx=m_lru_bx, m_lru_lambda=m_lru_lambda, m_w_out=m_w_out, m_norm2_w=m_norm2_w, m_ffn_up=m_ffn_up, m_ffn_conv_w=m_ffn_conv_w, m_ffn_down=m_ffn_down, m_final_norm_w=m_final_norm_w, v_norm1_w=v_norm1_w, v_w_in=v_w_in, v_pool_w=v_pool_w, v_pool_b=v_pool_b, v_pool_scale=v_pool_scale, v_gdn_conv_w=v_gdn_conv_w, v_gdn_a_log=v_gdn_a_log, v_gdn_dt_bias=v_gdn_dt_bias, v_gdn_norm_w=v_gdn_norm_w, v_lru_conv_w=v_lru_conv_w, v_lru_conv_b=v_lru_conv_b, v_lru_wa=v_lru_wa, v_lru_ba=v_lru_ba, v_lru_wx=v_lru_wx, v_lru_bx=v_lru_bx, v_lru_lambda=v_lru_lambda, v_w_out=v_w_out, v_norm2_w=v_norm2_w, v_ffn_up=v_ffn_up, v_ffn_conv_w=v_ffn_conv_w, v_ffn_down=v_ffn_down, v_final_norm_w=v_final_norm_w)
    weights = {n: given[n] for n in TWIN_WEIGHTS}
    shared = {n: given[n] for n in SHARED_INPUTS}
    per_example = {n: given[n] for n in ['x']}
    grad_fn = _jax.value_and_grad(_loss, argnums=(0, 1))

    def one_microbatch(ex, loss_target):
        ex = dict(ex)
        diff = ex.pop(TWIN_DIFF_INPUT)
        return grad_fn(weights, diff, {**shared, **ex}, loss_target)

    if N_MICROBATCH == 1:
        loss, (grad_w, grad_x) = one_microbatch(per_example, given["loss_target"])
    else:
        def body(carry, xs):
            loss_sum, grad_sum = carry
            l_k, (gw_k, gx_k) = one_microbatch(xs[0], xs[1])
            with _jax.named_scope("update"):
                return (loss_sum + l_k, _jax.tree.map(_jnp.add, grad_sum, gw_k)), gx_k

        init = (_jnp.zeros((), _jnp.float32), _jax.tree.map(_jnp.zeros_like, weights))
        (loss, grad_w), grad_x = _jax.lax.scan(body, init, (per_example, given["loss_target"]))
    with _jax.named_scope("update"):
        delta_w, new_m, new_v = {}, {}, {}
        for n in TWIN_WEIGHTS:
            delta_w[n], new_m[n], new_v[n] = _adamw(weights[n], grad_w[n], given["m_" + n], given["v_" + n])
    return (loss, grad_x, *[grad_w[n] for n in TWIN_WEIGHTS], *[delta_w[n] for n in TWIN_WEIGHTS],
            *[new_m[n] for n in TWIN_WEIGHTS], *[new_v[n] for n in TWIN_WEIGHTS])
```

```python
import functools

import jax
import jax.numpy as jnp
from jax import lax
from jax.experimental import pallas as pl
from jax.experimental.pallas import tpu as pltpu

F32 = jnp.float32
BF16 = jnp.bfloat16
_MXU = jnp.bfloat16

D_MODEL = 2048
N_LAYERS = 2
POOL_W = 512
POOL_G = 4
POOL_GD = 128
POOL_WINDOWS = (2, 4, 8, 16)
POOL_HALO = 16
GDN_W = 768
GDN_H = 6
GDN_DH = 128
GDN_C = 64
LRU_W = 768
LRU_NB = 6
LRU_BD = 128
LRU_C = 8.0
D_FF = 6144
EPS = 1e-6
IN_COLS = 5132
HALO = 8

PQ, PK, PV, PZ, PXR, PGR, PPOOL, PAB, PCOLS = 0, 768, 1536, 2304, 3072, 3840, 4608, 5120, 5376
CB = 768

ADAM_LR = 0.001
ADAM_B1 = 0.9
ADAM_B2 = 0.999
ADAM_EPS = 1e-08
ADAM_WD = 0.01
ADAM_STEP = 10

VMEM_LIMIT = 56 * 1024 * 1024
MESH = pl.DeviceIdType.MESH
HBM = pl.BlockSpec(memory_space=pltpu.HBM)


def _cp(*sem):
    return pltpu.CompilerParams(dimension_semantics=sem, vmem_limit_bytes=VMEM_LIMIT)


def _dg(a, b, ta, tb):
    dims = (((0 if ta else 1,), (1 if tb else 0,)), ((), ()))
    return lax.dot_general(a, b, dims, preferred_element_type=F32)


def _split2(a):
    hi = a.astype(BF16)
    lo = (a - hi.astype(F32)).astype(BF16)
    return hi, lo


def _mm_raw(a, b, ta, tb, hi):
    if _MXU == F32:
        return _dg(a, b, ta, tb)
    if not hi:
        return _dg(a.astype(_MXU), b.astype(_MXU), ta, tb)
    a1, a2 = _split2(a)
    b1, b2 = _split2(b)
    return _dg(a1, b1, ta, tb) + (_dg(a1, b2, ta, tb) + _dg(a2, b1, ta, tb))


@functools.partial(jax.custom_vjp, nondiff_argnums=(2, 3, 4))
def _mm(a, b, ta=False, tb=False, hi=False):
    return _mm_raw(a, b, ta, tb, hi)


def _mm_fwd(a, b, ta, tb, hi):
    return _mm_raw(a, b, ta, tb, hi), (a, b)


def _mm_bwd(ta, tb, hi, res, dc):
    a, b = res
    da = _mm(b, dc, tb, True, hi) if ta else _mm(dc, b, False, not tb, hi)
    db = _mm(dc, a, True, ta, hi) if tb else _mm(a, dc, not ta, False, hi)
    return da, db


_mm.defvjp(_mm_fwd, _mm_bwd)


def _mm01(m01, x):
    if _MXU == F32:
        return _dg(m01, x, False, False)
    m = m01.astype(BF16)
    x1 = x.astype(BF16)
    r = x - x1.astype(F32)
    x2 = r.astype(BF16)
    x3 = (r - x2.astype(F32)).astype(BF16)
    return _dg(m, x1, False, False) + (_dg(m, x2, False, False) + _dg(m, x3, False, False))


def _down(x, k):
    return x if k == 0 else pltpu.roll(x, k, 0)


def _up(x, k):
    return x if k == 0 else pltpu.roll(x, x.shape[0] - k, 0)


def _rows(shape):
    return lax.broadcasted_iota(jnp.int32, shape, 0)


def _lanes(shape):
    return lax.broadcasted_iota(jnp.int32, shape, 1)


def _matmul(a, b, mode, *, name, res=None, tm=1024, tn=1024, tk=2048, b_split=False, o_split=0):
    ta, tb = mode == "tn", mode == "nt"
    if ta:
        K, M = a.shape
    else:
        M, K = a.shape
    if b_split:
        ns = b.shape[0]
        N = b.shape[1] if tb else ns * b.shape[2]
    else:
        N = b.shape[0] if tb else b.shape[1]
    tm, tn, tk = min(tm, M), min(tn, N), min(tk, K)
    if b_split:
        per = b.shape[2]
        if tb:
            tk = min(tk, per)
        else:
            tn = min(tn, per)
    if o_split:
        tn = min(tn, N // o_split)
    assert M % tm == 0 and N % tn == 0 and K % tk == 0, (name, M, N, K, tm, tn, tk)
    nk = K // tk
    a_spec = pl.BlockSpec((tk, tm), lambda i, j, k: (k, i)) if ta else pl.BlockSpec((tm, tk), lambda i, j, k: (i, k))
    if not b_split:
        b_spec = pl.BlockSpec((tn, tk), lambda i, j, k: (j, k)) if tb else pl.BlockSpec((tk, tn), lambda i, j, k: (k, j))
    elif tb:
        kb = per // tk
        b_spec = pl.BlockSpec((None, tn, tk), lambda i, j, k: (k // kb, j, k % kb))
    else:
        nb = per // tn
        b_spec = pl.BlockSpec((None, tk, tn), lambda i, j, k: (j // nb, k, j % nb))
    if o_split:
        ob = (N // o_split) // tn
        out_shape = jax.ShapeDtypeStruct((o_split, M, N // o_split), F32)
        o_spec = pl.BlockSpec((None, tm, tn), lambda i, j, k: (j // ob, i, j % ob))
    else:
        out_shape = jax.ShapeDtypeStruct((M, N), F32)
        o_spec = pl.BlockSpec((tm, tn), lambda i, j, k: (i, j))
    in_specs = [a_spec, b_spec]
    args = [a, b]
    if res is not None:
        in_specs.append(pl.BlockSpec((tm, tn), lambda i, j, k: (i, j)))
        args.append(res)

    def body(*refs):
        a_ref, b_ref = refs[0], refs[1]
        o_ref = refs[-1]
        p = _dg(a_ref[...].astype(_MXU), b_ref[...].astype(_MXU), ta, tb)
        first = p + refs[2][...] if res is not None else p
        if nk == 1:
            o_ref[...] = first
        else:
            k = pl.program_id(2)

            @pl.when(k == 0)
            def _():
                o_ref[...] = first

            @pl.when(k > 0)
            def _():
                o_ref[...] += p

    return pl.pallas_call(
        body, name=name, grid=(M // tm, N // tn, nk), in_specs=in_specs, out_specs=o_spec, out_shape=out_shape,
        compiler_params=_cp("parallel", "parallel", "arbitrary"),
    )(*args)


def _rms(x, w):
    return x * lax.rsqrt(jnp.mean(x * x, axis=-1, keepdims=True) + EPS) * w


def _row_tile(S, t=512):
    t = min(t, S)
    assert S % t == 0
    return t


def _rms_fwd(x, w, name):
    S, D = x.shape
    T = _row_tile(S)

    def body(x_ref, w_ref, o_ref):
        o_ref[...] = _rms(x_ref[...], w_ref[...]).astype(o_ref.dtype)

    return pl.pallas_call(
        body, name=name, grid=(S // T,),
        in_specs=[pl.BlockSpec((T, D), lambda i: (i, 0)), pl.BlockSpec((1, D), lambda i: (0, 0))],
        out_specs=pl.BlockSpec((T, D), lambda i: (i, 0)), out_shape=jax.ShapeDtypeStruct((S, D), BF16),
        compiler_params=_cp("parallel"),
    )(x, w.reshape(1, D))


def _rms_bwd(x, w, dh, dres, name):
    S, D = x.shape
    T = _row_tile(S)

    def body(x_ref, w_ref, dh_ref, dr_ref, dx_ref, gw_ref):
        _, vjp = jax.vjp(_rms, x_ref[...], w_ref[...])
        dx, dw = vjp(dh_ref[...])
        dx_ref[...] = dr_ref[...] + dx

        @pl.when(pl.program_id(0) == 0)
        def _():
            gw_ref[...] = jnp.zeros_like(gw_ref)

        gw_ref[...] += dw

    row = pl.BlockSpec((T, D), lambda i: (i, 0))
    vec = pl.BlockSpec((1, D), lambda i: (0, 0))
    return pl.pallas_call(
        body, name=name, grid=(S // T,), in_specs=[row, vec, row, row], out_specs=[row, vec],
        out_shape=[jax.ShapeDtypeStruct((S, D), F32), jax.ShapeDtypeStruct((1, D), F32)],
        compiler_params=_cp("arbitrary"),
    )(x, w.reshape(1, D), dh, dres)


def _loss_head(x, w, tgt, name):
    S, D = x.shape
    T = _row_tile(S)

    def body(x_ref, w_ref, t_ref, l_ref, dx_ref, gw_ref):
        y, vjp = jax.vjp(_rms, x_ref[...], w_ref[...])
        err = y - t_ref[...]
        part = 0.5 * jnp.sum(jnp.mean(err * err, axis=-1, keepdims=True), axis=0, keepdims=True)
        dx, dw = vjp(err * (1.0 / D))
        dx_ref[...] = dx

        @pl.when(pl.program_id(0) == 0)
        def _():
            gw_ref[...] = jnp.zeros_like(gw_ref)
            l_ref[...] = jnp.zeros_like(l_ref)

        gw_ref[...] += dw
        l_ref[...] += jnp.broadcast_to(part, l_ref.shape)

    row = pl.BlockSpec((T, D), lambda i: (i, 0))
    vec = pl.BlockSpec((1, D), lambda i: (0, 0))
    return pl.pallas_call(
        body, name=name, grid=(S // T,), in_specs=[row, vec, row],
        out_specs=[pl.BlockSpec((8, 128), lambda i: (0, 0)), row, vec],
        out_shape=[jax.ShapeDtypeStruct((8, 128), F32), jax.ShapeDtypeStruct((S, D), F32), jax.ShapeDtypeStruct((1, D), F32)],
        compiler_params=_cp("arbitrary"),
    )(x, w.reshape(1, D), tgt)


def _by_group(shape, vals):
    g = _lanes(shape) // POOL_GD
    out = vals[-1]
    for k in range(len(vals) - 2, -1, -1):
        out = jnp.where(g == k, vals[k], out)
    return out


def _pool_d(prev, u, t0):
    ext = jnp.concatenate([prev, u], axis=0)
    s2 = ext + _down(ext, 1)
    s4 = s2 + _down(s2, 2)
    s8 = s4 + _down(s4, 4)
    s16 = s8 + _down(s8, 8)
    ssel = _by_group(ext.shape, [s2, s4, s8, s16])[POOL_HALO:]
    win = _by_group(u.shape, [jnp.int32(w) for w in POOL_WINDOWS])
    cnt = jnp.minimum(t0 + _rows(u.shape) + 1, win).astype(F32)
    return ssel / cnt - u


def _pool_lin(d, w_ref, b):
    ys = [_mm(d[:, g * POOL_GD:(g + 1) * POOL_GD], w_ref[g]) for g in range(POOL_G)]
    return jnp.concatenate(ys, axis=1) + b


def _pool_fwd(proj, w, b, scale, name):
    S = proj.shape[0]
    T = _row_tile(S)
    r = T // POOL_HALO
    cb = PPOOL // POOL_W

    def body(u_ref, up_ref, w_ref, b_ref, sc_ref, y_ref):
        i = pl.program_id(0)
        prev = jnp.where(i > 0, up_ref[...], 0.0)
        d = _pool_d(prev, u_ref[...], i * T)
        y_ref[...] = _pool_lin(d, w_ref, b_ref[...]) * sc_ref[...]

    vec = pl.BlockSpec((1, POOL_W), lambda i: (0, 0))
    return pl.pallas_call(
        body, name=name, grid=(S // T,),
        in_specs=[pl.BlockSpec((T, POOL_W), lambda i: (i, cb)),
                  pl.BlockSpec((POOL_HALO, POOL_W), lambda i: (jnp.maximum(i * r - 1, 0), cb)),
                  pl.BlockSpec((POOL_G, POOL_GD, POOL_GD), lambda i: (0, 0, 0)), vec, vec],
        out_specs=pl.BlockSpec((T, POOL_W), lambda i: (i, 0)), out_shape=jax.ShapeDtypeStruct((S, POOL_W), F32),
        compiler_params=_cp("parallel"),
    )(proj, proj, w, b.reshape(1, POOL_W), scale.reshape(1, POOL_W))


def _pool_bwd(proj, dmixed, w, b, scale, name):
    S = proj.shape[0]
    T = _row_tile(S)
    n = S // T
    r = T // POOL_HALO
    cb = PPOOL // POOL_W
    mb = 1536 // POOL_W

    def body(u_ref, up_ref, dy_ref, dyn_ref, w_ref, b_ref, sc_ref, du_ref, gw_ref, gb_ref, gs_ref):
        i = pl.program_id(0)
        sc = sc_ref[...]
        dy = dy_ref[...]
        dy_ext = jnp.concatenate([dy, jnp.where(i < n - 1, dyn_ref[...], 0.0)], axis=0)
        dyl = dy_ext * sc
        dd = jnp.concatenate(
            [_mm(dyl[:, g * POOL_GD:(g + 1) * POOL_GD], w_ref[g], False, True) for g in range(POOL_G)], axis=1)
        t_ext = i * T + _rows(dd.shape)
        win = _by_group(dd.shape, [jnp.int32(v) for v in POOL_WINDOWS])
        cnt = jnp.minimum(t_ext + 1, win).astype(F32)
        e = jnp.where(t_ext < S, dd / cnt, 0.0)
        f2 = e + _up(e, 1)
        f4 = f2 + _up(f2, 2)
        f8 = f4 + _up(f4, 4)
        f16 = f8 + _up(f8, 8)
        du = (_by_group(dd.shape, [f2, f4, f8, f16]) - dd)[:T]
        du_ref[...] = du.astype(du_ref.dtype)

        prev = jnp.where(i > 0, up_ref[...], 0.0)
        d = _pool_d(prev, u_ref[...], i * T)
        ylin = _pool_lin(d, w_ref, b_ref[...])
        dyl_m = dy * sc

        @pl.when(i == 0)
        def _():
            gw_ref[...] = jnp.zeros_like(gw_ref)
            gb_ref[...] = jnp.zeros_like(gb_ref)
            gs_ref[...] = jnp.zeros_like(gs_ref)

        gs_ref[...] += jnp.sum(dy * ylin, axis=0, keepdims=True)
        gb_ref[...] += jnp.sum(dyl_m, axis=0, keepdims=True)
        for g in range(POOL_G):
            sl = slice(g * POOL_GD, (g + 1) * POOL_GD)
            gw_ref[g] += _mm(d[:, sl], dyl_m[:, sl], True, False)

    vec = pl.BlockSpec((1, POOL_W), lambda i: (0, 0))
    wsp = pl.BlockSpec((POOL_G, POOL_GD, POOL_GD), lambda i: (0, 0, 0))
    nh = S // POOL_HALO
    return pl.pallas_call(
        body, name=name, grid=(n,),
        in_specs=[pl.BlockSpec((T, POOL_W), lambda i: (i, cb)),
                  pl.BlockSpec((POOL_HALO, POOL_W), lambda i: (jnp.maximum(i * r - 1, 0), cb)),
                  pl.BlockSpec((T, POOL_W), lambda i: (i, mb)),
                  pl.BlockSpec((POOL_HALO, POOL_W), lambda i: (jnp.minimum((i + 1) * r, nh - 1), mb)),
                  wsp, vec, vec],
        out_specs=[pl.BlockSpec((T, POOL_W), lambda i: (i, 0)), wsp, vec, vec],
        out_shape=[jax.ShapeDtypeStruct((S, POOL_W), BF16), jax.ShapeDtypeStruct((POOL_G, POOL_GD, POOL_GD), F32),
                   jax.ShapeDtypeStruct((1, POOL_W), F32), jax.ShapeDtypeStruct((1, POOL_W), F32)],
        compiler_params=_cp("arbitrary"),
    )(proj, proj, dmixed, dmixed, w, b.reshape(1, POOL_W), scale.reshape(1, POOL_W))


def _conv_rows(ext, w_ref, taps):
    acc = w_ref[taps - 1:taps, :] * ext
    for k in range(1, taps):
        acc = acc + w_ref[taps - 1 - k:taps - k, :] * _down(ext, k)
    return acc


def _conv_t_rows(dc, w_ref, taps):
    acc = w_ref[taps - 1:taps, :] * dc
    for k in range(1, taps):
        acc = acc + w_ref[taps - 1 - k:taps - k, :] * _up(dc, k)
    return acc


def _conv_specs(T, S, ncb0, with_next):
    r = T // HALO
    nh = S // HALO
    main = pl.BlockSpec((T, CB), lambda j, i: (i, j + ncb0))
    prev = pl.BlockSpec((HALO, CB), lambda j, i: (jnp.maximum(i * r - 1, 0), j + ncb0))
    nxt = pl.BlockSpec((HALO, CB), lambda j, i: (jnp.minimum((i + 1) * r, nh - 1), j + ncb0))
    return (main, prev, nxt) if with_next else (main, prev)


def _gdn_conv_fwd(proj, w, name):
    S = proj.shape[0]
    T = _row_tile(S)
    taps = w.shape[0]
    ncb = 3 * GDN_W // CB

    def body(x_ref, xp_ref, w_ref, o_ref):
        i = pl.program_id(1)
        ext = jnp.concatenate([jnp.where(i > 0, xp_ref[...], 0.0), x_ref[...]], axis=0)
        o_ref[...] = jax.nn.silu(_conv_rows(ext, w_ref, taps)[HALO:])

    main, prev = _conv_specs(T, S, PQ // CB, False)
    return pl.pallas_call(
        body, name=name, grid=(ncb, S // T),
        in_specs=[main, prev, pl.BlockSpec((taps, CB), lambda j, i: (0, j))],
        out_specs=pl.BlockSpec((T, CB), lambda j, i: (i, j)), out_shape=jax.ShapeDtypeStruct((S, 3 * GDN_W), F32),
        compiler_params=_cp("parallel", "parallel"),
    )(proj, proj, w)


def _gdn_conv_bwd(proj, dact, w, name):
    S = proj.shape[0]
    T = _row_tile(S)
    n = S // T
    taps = w.shape[0]
    ncb = 3 * GDN_W // CB

    def body(x_ref, xp_ref, xn_ref, d_ref, dn_ref, w_ref, dx_ref, gw_ref):
        i = pl.program_id(1)
        last = i == n - 1
        ext = jnp.concatenate([jnp.where(i > 0, xp_ref[...], 0.0), x_ref[...], jnp.where(last, 0.0, xn_ref[...])], axis=0)
        c = _conv_rows(ext, w_ref, taps)[HALO:]
        d_ext = jnp.concatenate([d_ref[...], jnp.where(last, 0.0, dn_ref[...])], axis=0)
        _, vjp = jax.vjp(jax.nn.silu, c)
        dc = vjp(d_ext)[0]
        dx_ref[...] = _conv_t_rows(dc, w_ref, taps)[:T].astype(dx_ref.dtype)

        @pl.when(i == 0)
        def _():
            gw_ref[...] = jnp.zeros_like(gw_ref)

        dcm = dc[:T]
        for k in range(taps):
            gw_ref[taps - 1 - k:taps - k, :] += jnp.sum(dcm * _down(ext, k)[HALO:HALO + T], axis=0, keepdims=True)

    main, prev, nxt = _conv_specs(T, S, PQ // CB, True)
    dmain, _, dnxt = _conv_specs(T, S, 0, True)
    wsp = pl.BlockSpec((taps, CB), lambda j, i: (0, j))
    return pl.pallas_call(
        body, name=name, grid=(ncb, n), in_specs=[main, prev, nxt, dmain, dnxt, wsp],
        out_specs=[pl.BlockSpec((T, CB), lambda j, i: (i, j)), wsp],
        out_shape=[jax.ShapeDtypeStruct((S, 3 * GDN_W), BF16), jax.ShapeDtypeStruct((taps, 3 * GDN_W), F32)],
        compiler_params=_cp("parallel", "arbitrary"),
    )(proj, proj, proj, dact, dact, w)


def _ffn_act_fwd(up, w, name):
    S = up.shape[0]
    T = _row_tile(S)
    taps = w.shape[0]
    ncb = D_FF // CB

    def body(g_ref, gp_ref, v_ref, w_ref, o_ref):
        i = pl.program_id(1)
        ext = jnp.concatenate([jnp.where(i > 0, gp_ref[...], 0.0), g_ref[...]], axis=0)
        c = _conv_rows(ext, w_ref, taps)[HALO:]
        o_ref[...] = (jax.nn.gelu(c) * v_ref[...]).astype(o_ref.dtype)

    main, prev = _conv_specs(T, S, 0, False)
    val = pl.BlockSpec((T, CB), lambda j, i: (i, j + ncb))
    return pl.pallas_call(
        body, name=name, grid=(ncb, S // T),
        in_specs=[main, prev, val, pl.BlockSpec((taps, CB), lambda j, i: (0, j))],
        out_specs=pl.BlockSpec((T, CB), lambda j, i: (i, j)), out_shape=jax.ShapeDtypeStruct((S, D_FF), BF16),
        compiler_params=_cp("parallel", "parallel"),
    )(up, up, up, w)


def _ffn_act_bwd(up, dact, w, name):
    S = up.shape[0]
    T = _row_tile(S)
    n = S // T
    taps = w.shape[0]
    ncb = D_FF // CB

    def body(g_ref, gp_ref, gn_ref, v_ref, vn_ref, d_ref, dn_ref, w_ref, dg_ref, dv_ref, gw_ref):
        i = pl.program_id(1)
        last = i == n - 1
        ext = jnp.concatenate([jnp.where(i > 0, gp_ref[...], 0.0), g_ref[...], jnp.where(last, 0.0, gn_ref[...])], axis=0)
        c = _conv_rows(ext, w_ref, taps)[HALO:]
        v_ext = jnp.concatenate([v_ref[...], jnp.where(last, 0.0, vn_ref[...])], axis=0)
        d_ext = jnp.concatenate([d_ref[...], jnp.where(last, 0.0, dn_ref[...])], axis=0)
        gl, vjp = jax.vjp(jax.nn.gelu, c)
        dv_ref[...] = (d_ext * gl)[:T].astype(dv_ref.dtype)
        dc = vjp(d_ext * v_ext)[0]
        dg_ref[...] = _conv_t_rows(dc, w_ref, taps)[:T].astype(dg_ref.dtype)

        @pl.when(i == 0)
        def _():
            gw_ref[...] = jnp.zeros_like(gw_ref)

        dcm = dc[:T]
        for k in range(taps):
            gw_ref[taps - 1 - k:taps - k, :] += jnp.sum(dcm * _down(ext, k)[HALO:HALO + T], axis=0, keepdims=True)

    main, prev, nxt = _conv_specs(T, S, 0, True)
    vmain, _, vnxt = _conv_specs(T, S, ncb, True)
    wsp = pl.BlockSpec((taps, CB), lambda j, i: (0, j))
    osp = pl.BlockSpec((T, CB), lambda j, i: (i, j))
    return pl.pallas_call(
        body, name=name, grid=(ncb, n), in_specs=[main, prev, nxt, vmain, vnxt, main, nxt, wsp],
        out_specs=[osp, osp, wsp],
        out_shape=[jax.ShapeDtypeStruct((S, D_FF), BF16), jax.ShapeDtypeStruct((S, D_FF), BF16),
                   jax.ShapeDtypeStruct((taps, D_FF), F32)],
        compiler_params=_cp("parallel", "arbitrary"),
    )(up, up, up, up, up, dact, dact, w)


def _tri_masks():
    r = _rows((GDN_C, GDN_C))
    c = _lanes((GDN_C, GDN_C))
    return r >= c, r > c


def _tri_inv_raw(low):
    r = _rows(low.shape)
    c = _lanes(low.shape)
    p = jnp.where(r == c, 1.0, 0.0) - low
    lp = low
    for _ in range(5):
        lp = _mm(lp, lp, False, False, True)
        p = p + _mm(p, lp, False, False, True)
    return p


@jax.custom_vjp
def _tri_inv(low):
    return _tri_inv_raw(low)


def _tri_inv_fwd(low):
    t = _tri_inv_raw(low)
    return t, t


def _tri_inv_bwd(t, dt):
    return (-_mm(_mm(t, dt, True, False, True), t, False, True, True),)


_tri_inv.defvjp(_tri_inv_fwd, _tri_inv_bwd)


def _gdn_glog(a_col, alog, dtb):
    return -jnp.exp(alog) * jax.nn.softplus(a_col + dtb)


def _decay_operand():
    r = _rows((GDN_C, 2 * GDN_C))
    c = _lanes((GDN_C, 2 * GDN_C))
    return jnp.where((c >= GDN_C) | (r > c), 1.0, 0.0)


def _gdn_decay(glog):
    causal, _ = _tri_masks()
    res = _mm01(jnp.where(causal, 1.0, 0.0), glog * _decay_operand())
    return res[:, GDN_C:GDN_C + 1], res[:, :GDN_C]


def _gdn_decay_bwd(dgcol, dd):
    r = _rows((GDN_C, GDN_C))
    c = _lanes((GDN_C, GDN_C))
    dres = jnp.concatenate([dd, jnp.where(c == 0, dgcol, 0.0)], axis=1)
    dx = _mm01(jnp.where(r <= c, 1.0, 0.0), dres)
    return jnp.sum(dx * _decay_operand(), axis=1, keepdims=True)


def _gdn_chunk(qa, ka, va, bt_col, gcol, dmat):
    causal, strict = _tri_masks()
    qn = qa * lax.rsqrt(jnp.sum(qa * qa, axis=-1, keepdims=True) + EPS) * (GDN_DH ** -0.5)
    kn = ka * lax.rsqrt(jnp.sum(ka * ka, axis=-1, keepdims=True) + EPS)
    beta = jax.nn.sigmoid(bt_col)
    eg = jnp.exp(gcol)
    decay = jnp.where(causal, jnp.exp(dmat), 0.0)
    low = jnp.where(strict, beta * _mm(kn, kn, False, True) * decay, 0.0)
    t = _tri_inv(low)
    w = _mm(t, kn * (beta * eg), False, False, True)
    u = _mm(t, va * beta, False, False, True)
    attn = _mm(qn, kn, False, True) * decay
    g_last = jnp.sum(jnp.where(_rows(gcol.shape) == GDN_C - 1, gcol, 0.0), axis=0, keepdims=True)
    return w, u, qn * eg, kn * jnp.exp(g_last - gcol), attn


def _gdn_step(state, w, u, qd, kd, attn, egl):
    v_new = u - _mm(w, state)
    o = _mm(qd, state) + _mm(attn, v_new)
    return o, state * egl + _mm(kd, v_new, True, False)


def _gated_norm(o, z, nw):
    return o * lax.rsqrt(jnp.mean(o * o, axis=-1, keepdims=True) + EPS) * nw * jax.nn.silu(z)


def _hsl(h):
    return slice(h * GDN_DH, (h + 1) * GDN_DH)


def _pad_lanes(a, width=GDN_DH):
    return jnp.concatenate([a, jnp.zeros((a.shape[0], width - a.shape[1]), a.dtype)], axis=1)


def _gdn_prep(qkv, proj, alog, dtb, name):
    S = qkv.shape[0]
    N = S // GDN_C

    def body(qkv_ref, ab_ref, al_ref, dt_ref, w_ref, u_ref, qd_ref, kd_ref, at_ref, gc_ref):
        ab = ab_ref[...]
        gc = jnp.zeros((GDN_C, 128), F32)
        for h in range(GDN_H):
            glog = _gdn_glog(ab[:, h:h + 1], al_ref[:, h:h + 1], dt_ref[:, h:h + 1])
            gcol, dmat = _gdn_decay(glog)
            w, u, qd, kd, attn = _gdn_chunk(
                qkv_ref[:, _hsl(h)], qkv_ref[:, _hsl(GDN_H + h)], qkv_ref[:, _hsl(2 * GDN_H + h)],
                ab[:, GDN_H + h:GDN_H + h + 1], gcol, dmat)
            w_ref[:, _hsl(h)] = w
            u_ref[:, _hsl(h)] = u
            qd_ref[:, _hsl(h)] = qd
            kd_ref[:, _hsl(h)] = kd
            at_ref[:, _hsl(h)] = _pad_lanes(attn)
            gc = jnp.where(_lanes(gc.shape) == h, gcol, gc)
        gc_ref[...] = gc

    vec = pl.BlockSpec((1, 128), lambda i: (0, 0))
    hsp = pl.BlockSpec((GDN_C, GDN_W), lambda i: (i, 0))
    hshape = jax.ShapeDtypeStruct((S, GDN_W), F32)
    return pl.pallas_call(
        body, name=name, grid=(N,),
        in_specs=[pl.BlockSpec((GDN_C, 3 * GDN_W), lambda i: (i, 0)), pl.BlockSpec((GDN_C, 128), lambda i: (i, PAB // 128)), vec, vec],
        out_specs=[hsp] * 5 + [pl.BlockSpec((GDN_C, 128), lambda i: (i, 0))],
        out_shape=[hshape] * 5 + [jax.ShapeDtypeStruct((S, 128), F32)],
        compiler_params=_cp("parallel"),
    )(qkv, proj, alog, dtb)


def _gdn_scan(w, u, qd, kd, attn, gc, name):
    S = w.shape[0]
    N = S // GDN_C

    def body(w_ref, u_ref, qd_ref, kd_ref, at_ref, gc_ref, o_ref, st_ref, s_ref):
        @pl.when(pl.program_id(0) == 0)
        def _():
            s_ref[...] = jnp.zeros_like(s_ref)

        for h in range(GDN_H):
            state = s_ref[_hsl(h), :]
            st_ref[_hsl(h), :] = state
            egl = jnp.exp(gc_ref[GDN_C - 1:GDN_C, h:h + 1])
            o, new = _gdn_step(state, w_ref[:, _hsl(h)], u_ref[:, _hsl(h)], qd_ref[:, _hsl(h)], kd_ref[:, _hsl(h)],
                               at_ref[:, h * GDN_DH:h * GDN_DH + GDN_C], egl)
            o_ref[:, _hsl(h)] = o
            s_ref[_hsl(h), :] = new

    hsp = pl.BlockSpec((GDN_C, GDN_W), lambda i: (i, 0))
    return pl.pallas_call(
        body, name=name, grid=(N,),
        in_specs=[hsp] * 5 + [pl.BlockSpec((GDN_C, 128), lambda i: (i, 0))],
        out_specs=[hsp, pl.BlockSpec((None, GDN_W, GDN_DH), lambda i: (i, 0, 0))],
        out_shape=[jax.ShapeDtypeStruct((S, GDN_W), F32), jax.ShapeDtypeStruct((N, GDN_W, GDN_DH), F32)],
        scratch_shapes=[pltpu.VMEM((GDN_W, GDN_DH), F32)],
        compiler_params=_cp("arbitrary"),
    )(w, u, qd, kd, attn, gc)


def _gdn_scan_bwd(w, u, qd, kd, attn, gc, states, o, proj, dmixed, nw, name):
    S = w.shape[0]
    N = S // GDN_C

    def body(w_ref, u_ref, qd_ref, kd_ref, at_ref, gc_ref, st_ref, o_ref, z_ref, dm_ref, nw_ref,
             dw_ref, du_ref, dqd_ref, dkd_ref, dat_ref, dgl_ref, dz_ref, gnw_ref, ds_ref):
        @pl.when(pl.program_id(0) == 0)
        def _():
            ds_ref[...] = jnp.zeros_like(ds_ref)
            gnw_ref[...] = jnp.zeros_like(gnw_ref)

        dgl = jnp.zeros((8, 128), F32)
        gnw = jnp.zeros((1, GDN_DH), F32)
        for h in range(GDN_H):
            _, vjp_n = jax.vjp(_gated_norm, o_ref[:, _hsl(h)], z_ref[:, _hsl(h)], nw_ref[...])
            do, dz, dnw = vjp_n(dm_ref[:, _hsl(h)])
            dz_ref[:, _hsl(h)] = dz.astype(dz_ref.dtype)
            gnw = gnw + dnw
            egl = jnp.exp(gc_ref[GDN_C - 1:GDN_C, h:h + 1])
            _, vjp_s = jax.vjp(_gdn_step, st_ref[_hsl(h), :], w_ref[:, _hsl(h)], u_ref[:, _hsl(h)], qd_ref[:, _hsl(h)],
                               kd_ref[:, _hsl(h)], at_ref[:, h * GDN_DH:h * GDN_DH + GDN_C], egl)
            ds, dw, du, dqd, dkd, dat, degl = vjp_s((do, ds_ref[_hsl(h), :]))
            ds_ref[_hsl(h), :] = ds
            dw_ref[:, _hsl(h)] = dw
            du_ref[:, _hsl(h)] = du
            dqd_ref[:, _hsl(h)] = dqd
            dkd_ref[:, _hsl(h)] = dkd
            dat_ref[:, _hsl(h)] = _pad_lanes(dat)
            dgl = jnp.where(_lanes(dgl.shape) == h, degl * egl, dgl)
        dgl_ref[...] = dgl
        gnw_ref[...] += gnw

    rev = lambda i: (N - 1 - i, 0)
    hsp = pl.BlockSpec((GDN_C, GDN_W), rev)
    gsp = pl.BlockSpec((GDN_C, 128), rev)
    vec = pl.BlockSpec((1, GDN_DH), lambda i: (0, 0))
    hshape = jax.ShapeDtypeStruct((S, GDN_W), F32)
    return pl.pallas_call(
        body, name=name, grid=(N,),
        in_specs=[hsp] * 5 + [gsp, pl.BlockSpec((None, GDN_W, GDN_DH), lambda i: (N - 1 - i, 0, 0)), hsp,
                              pl.BlockSpec((GDN_C, GDN_W), lambda i: (N - 1 - i, PZ // GDN_W)), hsp, vec],
        out_specs=[hsp] * 5 + [pl.BlockSpec((8, 128), rev), hsp, vec],
        out_shape=[hshape] * 5 + [jax.ShapeDtypeStruct((N * 8, 128), F32), jax.ShapeDtypeStruct((S, GDN_W), BF16),
                                  jax.ShapeDtypeStruct((1, GDN_DH), F32)],
        scratch_shapes=[pltpu.VMEM((GDN_W, GDN_DH), F32)],
        compiler_params=_cp("arbitrary"),
    )(w, u, qd, kd, attn, gc, states, o, proj, dmixed, nw)


def _gdn_prep_bwd(qkv, proj, alog, dtb, dw, du, dqd, dkd, dat, dgl, name):
    S = qkv.shape[0]
    N = S // GDN_C

    def body(qkv_ref, ab_ref, al_ref, dt_ref, dw_ref, du_ref, dqd_ref, dkd_ref, dat_ref, dgl_ref,
             dqkv_ref, dab_ref, gal_ref, gdt_ref):
        @pl.when(pl.program_id(0) == 0)
        def _():
            gal_ref[...] = jnp.zeros_like(gal_ref)
            gdt_ref[...] = jnp.zeros_like(gdt_ref)

        ab = ab_ref[...]
        dab = jnp.zeros((GDN_C, 128), F32)
        gal = jnp.zeros((1, 128), F32)
        gdt = jnp.zeros((1, 128), F32)
        for h in range(GDN_H):
            a_col = ab[:, h:h + 1]
            glog, vjp_g = jax.vjp(_gdn_glog, a_col, al_ref[:, h:h + 1], dt_ref[:, h:h + 1])
            gcol, dmat = _gdn_decay(glog)
            _, vjp_c = jax.vjp(_gdn_chunk, qkv_ref[:, _hsl(h)], qkv_ref[:, _hsl(GDN_H + h)], qkv_ref[:, _hsl(2 * GDN_H + h)],
                               ab[:, GDN_H + h:GDN_H + h + 1], gcol, dmat)
            dqa, dka, dva, dbt, dgcol, dd = vjp_c((dw_ref[:, _hsl(h)], du_ref[:, _hsl(h)], dqd_ref[:, _hsl(h)],
                                                    dkd_ref[:, _hsl(h)], dat_ref[:, h * GDN_DH:h * GDN_DH + GDN_C]))
            dgcol = dgcol + jnp.where(_rows(dgcol.shape) == GDN_C - 1, dgl_ref[0:1, h:h + 1], 0.0)
            da_col, dal, ddt = vjp_g(_gdn_decay_bwd(dgcol, dd))
            dqkv_ref[:, _hsl(h)] = dqa
            dqkv_ref[:, _hsl(GDN_H + h)] = dka
            dqkv_ref[:, _hsl(2 * GDN_H + h)] = dva
            ln = _lanes(dab.shape)
            dab = dab + jnp.where(ln == h, da_col, 0.0) + jnp.where(ln == GDN_H + h, dbt, 0.0)
            l1 = _lanes(gal.shape)
            gal = gal + jnp.where(l1 == h, dal, 0.0)
            gdt = gdt + jnp.where(l1 == h, ddt, 0.0)
        dab_ref[...] = dab.astype(dab_ref.dtype)
        gal_ref[...] += gal
        gdt_ref[...] += gdt

    vec = pl.BlockSpec((1, 128), lambda i: (0, 0))
    hsp = pl.BlockSpec((GDN_C, GDN_W), lambda i: (i, 0))
    qsp = pl.BlockSpec((GDN_C, 3 * GDN_W), lambda i: (i, 0))
    return pl.pallas_call(
        body, name=name, grid=(N,),
        in_specs=[qsp, pl.BlockSpec((GDN_C, 128), lambda i: (i, PAB // 128)), vec, vec] + [hsp] * 5
        + [pl.BlockSpec((8, 128), lambda i: (i, 0))],
        out_specs=[qsp, pl.BlockSpec((GDN_C, 128), lambda i: (i, 0)), vec, vec],
        out_shape=[jax.ShapeDtypeStruct((S, 3 * GDN_W), F32), jax.ShapeDtypeStruct((S, 128), BF16),
                   jax.ShapeDtypeStruct((1, 128), F32), jax.ShapeDtypeStruct((1, 128), F32)],
        compiler_params=_cp("arbitrary"),
    )(qkv, proj, alog, dtb, dw, du, dqd, dkd, dat, dgl)


@jax.custom_vjp
def _expm1(x):
    u = jnp.exp(x)
    lu = jnp.log(u)
    small = (u - 1.0) * x / jnp.where(u == 1.0, 1.0, lu)
    small = jnp.where(u == 1.0, x, small)
    return jnp.where(jnp.abs(x) < 0.5, small, u - 1.0)


def _expm1_fwd(x):
    return _expm1(x), jnp.exp(x)


def _expm1_bwd(ex, g):
    return (g * ex,)


_expm1.defvjp(_expm1_fwd, _expm1_bwd)


def _lru_gates(xc, wa, ba, wx, bx, lam, first):
    r = jax.nn.sigmoid(_mm(xc, wa) + ba)
    i = jax.nn.sigmoid(_mm(xc, wx) + bx)
    log_a = -LRU_C * r * jax.nn.softplus(-lam)
    mult = jnp.sqrt(-_expm1(2.0 * log_a))
    mult = jnp.where(first, 1.0, mult)
    return jnp.exp(log_a), mult * i * xc


def _scan_fwd(a, b):
    T = a.shape[0]
    rows = _rows(a.shape)
    s = 1
    while s < T:
        ok = rows >= s
        b = a * jnp.where(ok, _down(b, s), 0.0) + b
        a = a * jnp.where(ok, _down(a, s), 1.0)
        s *= 2
    return a, b


def _scan_rev(a, b):
    T = a.shape[0]
    rows = _rows(a.shape)
    s = 1
    while s < T:
        ok = rows + s < T
        b = a * jnp.where(ok, _up(b, s), 0.0) + b
        a = a * jnp.where(ok, _up(a, s), 1.0)
        s *= 2
    return b


def _bsl(j):
    return slice(j * LRU_BD, (j + 1) * LRU_BD)


def _lru_tile(S):
    return _row_tile(S, 256)


def _lru_fwd(proj, conv_w, conv_b, wa, ba, wx, bx, lam, name):
    S = proj.shape[0]
    T = _lru_tile(S)
    taps = conv_w.shape[0]
    r = T // HALO

    def body(x_ref, xp_ref, cw_ref, cb_ref, wa_ref, ba_ref, wx_ref, bx_ref, lam_ref, h_ref, carry_ref):
        i = pl.program_id(0)

        @pl.when(i == 0)
        def _():
            carry_ref[...] = jnp.zeros_like(carry_ref)

        ext = jnp.concatenate([jnp.where(i > 0, xp_ref[...], 0.0), x_ref[...]], axis=0)
        xc = _conv_rows(ext, cw_ref, taps)[HALO:] + cb_ref[...]
        first = (i * T + _rows((T, LRU_BD))) == 0
        for j in range(LRU_NB):
            a, b = _lru_gates(xc[:, _bsl(j)], wa_ref[j], ba_ref[:, _bsl(j)], wx_ref[j], bx_ref[:, _bsl(j)],
                              lam_ref[:, _bsl(j)], first=first)
            pa, hb = _scan_fwd(a, b)
            h_ref[:, _bsl(j)] = pa * carry_ref[0:1, _bsl(j)] + hb
            carry_ref[0:1, _bsl(j)] = h_ref[T - 1:T, _bsl(j)]

    vec = pl.BlockSpec((1, LRU_W), lambda i: (0, 0))
    wsp = pl.BlockSpec((LRU_NB, LRU_BD, LRU_BD), lambda i: (0, 0, 0))
    return pl.pallas_call(
        body, name=name, grid=(S // T,),
        in_specs=[pl.BlockSpec((T, LRU_W), lambda i: (i, PXR // LRU_W)),
                  pl.BlockSpec((HALO, LRU_W), lambda i: (jnp.maximum(i * r - 1, 0), PXR // LRU_W)),
                  pl.BlockSpec((taps, LRU_W), lambda i: (0, 0)), vec, wsp, vec, wsp, vec, vec],
        out_specs=pl.BlockSpec((T, LRU_W), lambda i: (i, 0)), out_shape=jax.ShapeDtypeStruct((S, LRU_W), F32),
        scratch_shapes=[pltpu.VMEM((8, LRU_W), F32)],
        compiler_params=_cp("arbitrary"),
    )(proj, proj, conv_w, conv_b.reshape(1, LRU_W), wa, ba.reshape(1, LRU_W), wx, bx.reshape(1, LRU_W), lam.reshape(1, LRU_W))


def _lru_bwd(proj, hl, dmixed, conv_w, conv_b, wa, ba, wx, bx, lam, name):
    S = proj.shape[0]
    T = _lru_tile(S)
    n = S // T
    taps = conv_w.shape[0]
    r = T // HALO
    mb = 768 // LRU_W

    def body(x_ref, xp_ref, g_ref, h_ref, hp_ref, dy_ref, cw_ref, cb_ref, wa_ref, ba_ref, wx_ref, bx_ref, lam_ref,
             dx_ref, dg_ref, gcw_ref, gcb_ref, gwa_ref, gba_ref, gwx_ref, gbx_ref, glam_ref, carry_ref, dxc_ref, nxt_ref):
        s = pl.program_id(0)
        i = n - 1 - s

        @pl.when(s == 0)
        def _():
            carry_ref[...] = jnp.zeros_like(carry_ref)
            nxt_ref[...] = jnp.zeros_like(nxt_ref)
            for ref in (gcw_ref, gcb_ref, gwa_ref, gba_ref, gwx_ref, gbx_ref, glam_ref):
                ref[...] = jnp.zeros_like(ref)

        ext = jnp.concatenate([jnp.where(i > 0, xp_ref[...], 0.0), x_ref[...]], axis=0)
        xc = _conv_rows(ext, cw_ref, taps)[HALO:] + cb_ref[...]
        rows = _rows((T, LRU_BD))
        first = (i * T + rows) == 0
        h_before = jnp.where(i > 0, hp_ref[HALO - 1:HALO, :], 0.0)
        for j in range(LRU_NB):
            sl = _bsl(j)
            (a, _), vjp_g = jax.vjp(functools.partial(_lru_gates, first=first), xc[:, sl], wa_ref[j], ba_ref[:, sl],
                                    wx_ref[j], bx_ref[:, sl], lam_ref[:, sl])
            gelu_g, vjp_a = jax.vjp(jax.nn.gelu, g_ref[:, sl])
            h = h_ref[:, sl]
            dy = dy_ref[:, sl]
            dg_ref[:, sl] = vjp_a(dy * h)[0].astype(dg_ref.dtype)
            b_rev = dy * gelu_g + jnp.where(rows == T - 1, carry_ref[0:1, sl], 0.0)
            a_rev = jnp.where(rows == T - 1, 0.0, _up(a, 1))
            dh = _scan_rev(a_rev, b_rev)
            carry_ref[:, sl] = (a * dh)[:HALO]
            h_prev = jnp.where(rows == 0, h_before[:, sl], _down(h, 1))
            dxc, dwa, dba, dwx, dbx, dlam = vjp_g((dh * h_prev, dh))
            dxc_ref[:, sl] = dxc
            gwa_ref[j] += dwa
            gwx_ref[j] += dwx
            gba_ref[:, sl] += dba
            gbx_ref[:, sl] += dbx
            glam_ref[:, sl] += dlam
        dxc = dxc_ref[...]
        d_ext = jnp.concatenate([dxc, nxt_ref[...]], axis=0)
        dx_ref[...] = _conv_t_rows(d_ext, cw_ref, taps)[:T].astype(dx_ref.dtype)
        nxt_ref[...] = dxc[:HALO]
        gcb_ref[...] += jnp.sum(dxc, axis=0, keepdims=True)
        for k in range(taps):
            gcw_ref[taps - 1 - k:taps - k, :] += jnp.sum(dxc * _down(ext, k)[HALO:], axis=0, keepdims=True)

    vec = pl.BlockSpec((1, LRU_W), lambda s: (0, 0))
    wsp = pl.BlockSpec((LRU_NB, LRU_BD, LRU_BD), lambda s: (0, 0, 0))
    cwsp = pl.BlockSpec((taps, LRU_W), lambda s: (0, 0))

    def main(cb):
        return pl.BlockSpec((T, LRU_W), lambda s: (n - 1 - s, cb))

    def prev(cb):
        return pl.BlockSpec((HALO, LRU_W), lambda s: (jnp.maximum((n - 1 - s) * r - 1, 0), cb))

    vshape = jax.ShapeDtypeStruct((1, LRU_W), F32)
    wshape = jax.ShapeDtypeStruct((LRU_NB, LRU_BD, LRU_BD), F32)
    return pl.pallas_call(
        body, name=name, grid=(n,),
        in_specs=[main(PXR // LRU_W), prev(PXR // LRU_W), main(PGR // LRU_W), main(0), prev(0), main(mb),
                  cwsp, vec, wsp, vec, wsp, vec, vec],
        out_specs=[main(0), main(0), cwsp, vec, wsp, vec, wsp, vec, vec],
        out_shape=[jax.ShapeDtypeStruct((S, LRU_W), BF16), jax.ShapeDtypeStruct((S, LRU_W), BF16),
                   jax.ShapeDtypeStruct((taps, LRU_W), F32), vshape, wshape, vshape, wshape, vshape, vshape],
        scratch_shapes=[pltpu.VMEM((8, LRU_W), F32), pltpu.VMEM((T, LRU_W), F32), pltpu.VMEM((HALO, LRU_W), F32)],
        compiler_params=_cp("arbitrary"),
    )(proj, proj, proj, hl, hl, dmixed, conv_w, conv_b.reshape(1, LRU_W), wa, ba.reshape(1, LRU_W), wx,
      bx.reshape(1, LRU_W), lam.reshape(1, LRU_W))


def _mix_out(o, proj, hl, y_pool, nw, name):
    S = o.shape[0]
    T = _row_tile(S)

    def body(o_ref, z_ref, h_ref, g_ref, p_ref, nw_ref, m_ref):
        for h in range(GDN_H):
            m_ref[:, _hsl(h)] = _gated_norm(o_ref[:, _hsl(h)], z_ref[:, _hsl(h)], nw_ref[...]).astype(m_ref.dtype)
        m_ref[:, GDN_W:GDN_W + LRU_W] = (h_ref[...] * jax.nn.gelu(g_ref[...])).astype(m_ref.dtype)
        m_ref[:, GDN_W + LRU_W:] = p_ref[...].astype(m_ref.dtype)

    row = pl.BlockSpec((T, GDN_W), lambda i: (i, 0))
    return pl.pallas_call(
        body, name=name, grid=(S // T,),
        in_specs=[row, pl.BlockSpec((T, GDN_W), lambda i: (i, PZ // GDN_W)), row,
                  pl.BlockSpec((T, LRU_W), lambda i: (i, PGR // LRU_W)), pl.BlockSpec((T, POOL_W), lambda i: (i, 0)),
                  pl.BlockSpec((1, GDN_DH), lambda i: (0, 0))],
        out_specs=pl.BlockSpec((T, D_MODEL), lambda i: (i, 0)), out_shape=jax.ShapeDtypeStruct((S, D_MODEL), BF16),
        compiler_params=_cp("parallel"),
    )(o, proj, hl, proj, y_pool, nw)


def _as2d(a):
    return a.reshape(-1, a.shape[-1])


def _ew_rows(rows, cols):
    t = rows
    while t * cols * 4 > (2 << 20) and t % 16 == 0:
        t //= 2
    return t


def _add_sel(src, r, p, name):
    shape = r.shape
    src2, r2 = src.reshape(2, -1, shape[-1]), _as2d(r)
    rows, cols = r2.shape
    t = _ew_rows(rows, cols)

    def body(p_ref, s_ref, r_ref, o_ref):
        o_ref[...] = s_ref[...] + r_ref[...]

    out = pl.pallas_call(
        body, name=name,
        grid_spec=pltpu.PrefetchScalarGridSpec(
            num_scalar_prefetch=1, grid=(rows // t,),
            in_specs=[pl.BlockSpec((None, t, cols), lambda i, p_ref: (p_ref[0], i, 0)),
                      pl.BlockSpec((t, cols), lambda i, p_ref: (i, 0))],
            out_specs=pl.BlockSpec((t, cols), lambda i, p_ref: (i, 0))),
        out_shape=jax.ShapeDtypeStruct((rows, cols), F32), compiler_params=_cp("parallel"),
    )(jnp.reshape(p, (1,)).astype(jnp.int32), src2, r2)
    return out.reshape(shape)


def _add(a, b, name):
    shape = a.shape
    a2, b2 = _as2d(a), _as2d(b)
    rows, cols = a2.shape
    t = _ew_rows(rows, cols)

    def body(a_ref, b_ref, o_ref):
        o_ref[...] = a_ref[...] + b_ref[...]

    sp = pl.BlockSpec((t, cols), lambda i: (i, 0))
    return pl.pallas_call(body, name=name, grid=(rows // t,), in_specs=[sp, sp], out_specs=sp,
                          out_shape=jax.ShapeDtypeStruct((rows, cols), F32), compiler_params=_cp("parallel"))(a2, b2).reshape(shape)


def _adamw(w, g, m, v, name):
    shape = w.shape
    w2, g2, m2, v2 = _as2d(w), _as2d(g), _as2d(m), _as2d(v)
    rows, cols = w2.shape
    t = _ew_rows(rows, cols)

    def body(w_ref, g_ref, m_ref, v_ref, d_ref, nm_ref, nv_ref):
        gr = g_ref[...]
        nm = ADAM_B1 * m_ref[...] + (1.0 - ADAM_B1) * gr
        nv = ADAM_B2 * v_ref[...] + (1.0 - ADAM_B2) * (gr * gr)
        m_hat = nm / (1.0 - ADAM_B1 ** ADAM_STEP)
        v_hat = nv / (1.0 - ADAM_B2 ** ADAM_STEP)
        d_ref[...] = -ADAM_LR * (m_hat / (jnp.sqrt(v_hat) + ADAM_EPS) + ADAM_WD * w_ref[...])
        nm_ref[...] = nm
        nv_ref[...] = nv

    sp = pl.BlockSpec((t, cols), lambda i: (i, 0))
    sh = jax.ShapeDtypeStruct((rows, cols), F32)
    d, nm, nv = pl.pallas_call(body, name=name, grid=(rows // t,), in_specs=[sp] * 4, out_specs=[sp] * 3,
                               out_shape=[sh] * 3, compiler_params=_cp("parallel"))(w2, g2, m2, v2)
    return d.reshape(shape), nm.reshape(shape), nv.reshape(shape)


def _place():
    return lax.axis_index("x"), lax.axis_index("y"), lax.axis_index("c")


def _gather_weights(arrs, name):
    n = len(arrs)

    def body(*refs):
        srcs, outs = refs[:n], refs[n:2 * n]
        send, recv, loc = refs[2 * n:]
        x, y, c = _place()
        s_me, s_x, s_y, s_d = 2 * x + y, 2 * (1 - x) + y, 2 * x + (1 - y), 2 * (1 - x) + (1 - y)
        xpeer, ypeer, sib = (1 - x, y, c), (x, 1 - y, c), (x, y, 1 - c)

        def rc(k, t, src, dst, to):
            return pltpu.make_async_remote_copy(src_ref=src, dst_ref=dst, send_sem=send.at[k, t], recv_sem=recv.at[k, t],
                                                device_id=to, device_id_type=MESH)

        local = []
        for k in range(n):
            for l in range(2):
                cp = pltpu.make_async_copy(srcs[k].at[l], outs[k].at[l, s_me], loc.at[k, l])
                cp.start()
                local.append(cp)
        sent = []
        for k in range(n):
            for t, to in ((0, xpeer), (1, ypeer)):
                cp = rc(k, t, srcs[k].at[c], outs[k].at[c, s_me], to)
                cp.start()
                sent.append(cp)
        for k in range(n):
            got = outs[k].at[c, s_x]
            rc(k, 0, got, got, xpeer).wait_recv()
            for t, to in ((2, ypeer), (3, sib)):
                cp = rc(k, t, got, got, to)
                cp.start()
                sent.append(cp)
        for k in range(n):
            got = outs[k].at[c, s_y]
            rc(k, 1, got, got, ypeer).wait_recv()
            cp = rc(k, 4, got, got, sib)
            cp.start()
            sent.append(cp)
        for k in range(n):
            got = outs[k].at[c, s_d]
            rc(k, 2, got, got, ypeer).wait_recv()
            cp = rc(k, 5, got, got, sib)
            cp.start()
            sent.append(cp)
        for k in range(n):
            for t, s in ((3, s_x), (4, s_y), (5, s_d)):
                got = outs[k].at[1 - c, s]
                rc(k, t, got, got, sib).wait_recv()
        for cp in sent:
            cp.wait_send()
        for cp in local:
            cp.wait()

    return pl.pallas_call(
        body, name=name, in_specs=[HBM] * n, out_specs=[HBM] * n,
        out_shape=[jax.ShapeDtypeStruct((2, 4) + a.shape[1:], a.dtype) for a in arrs],
        scratch_shapes=[pltpu.SemaphoreType.DMA((n, 6)), pltpu.SemaphoreType.DMA((n, 6)), pltpu.SemaphoreType.DMA((n, 2))],
    )(*arrs)


def _exchange(arrs, axis, name, half=True):
    n = len(arrs)

    def body(*refs):
        srcs, outs = refs[:n], refs[n:2 * n]
        send, recv = refs[2 * n:]
        x, y, c = _place()
        p = {"x": x, "y": y, "c": c}[axis]
        peer = {"x": (1 - x, y, c), "y": (x, 1 - y, c), "c": (x, y, 1 - c)}[axis]
        cps = []
        for k in range(n):
            cp = pltpu.make_async_remote_copy(src_ref=srcs[k].at[1 - p] if half else srcs[k], dst_ref=outs[k],
                                              send_sem=send.at[k], recv_sem=recv.at[k], device_id=peer, device_id_type=MESH)
            cp.start()
            cps.append(cp)
        for cp in cps:
            cp.wait()

    return pl.pallas_call(
        body, name=name, in_specs=[HBM] * n, out_specs=[HBM] * n,
        out_shape=[jax.ShapeDtypeStruct(a.shape[1:] if half else a.shape, a.dtype) for a in arrs],
        scratch_shapes=[pltpu.SemaphoreType.DMA((n,)), pltpu.SemaphoreType.DMA((n,))],
    )(*arrs)


def _share_layers(arrs, name):
    n = len(arrs)

    def body(*refs):
        srcs, outs = refs[:n], refs[n:2 * n]
        send, recv, loc = refs[2 * n:]
        x, y, c = _place()
        cps, lcs = [], []
        for k in range(n):
            lc = pltpu.make_async_copy(srcs[k], outs[k].at[c], loc.at[k])
            lc.start()
            lcs.append(lc)
            cp = pltpu.make_async_remote_copy(src_ref=srcs[k], dst_ref=outs[k].at[c], send_sem=send.at[k], recv_sem=recv.at[k],
                                              device_id=(x, y, 1 - c), device_id_type=MESH)
            cp.start()
            cps.append(cp)
        for k in range(n):
            got = outs[k].at[1 - c]
            pltpu.make_async_remote_copy(src_ref=got, dst_ref=got, send_sem=send.at[k], recv_sem=recv.at[k],
                                         device_id=(x, y, 1 - c), device_id_type=MESH).wait_recv()
        for cp in cps:
            cp.wait_send()
        for lc in lcs:
            lc.wait()

    return pl.pallas_call(
        body, name=name, in_specs=[HBM] * n, out_specs=[HBM] * n,
        out_shape=[jax.ShapeDtypeStruct((2,) + a.shape, a.dtype) for a in arrs],
        scratch_shapes=[pltpu.SemaphoreType.DMA((n,)), pltpu.SemaphoreType.DMA((n,)), pltpu.SemaphoreType.DMA((n,))],
    )(*arrs)


def _reduce_scatter(grads):
    x, y, c = _place()
    r1 = _exchange(grads, "c", "rs_c")
    a1 = [_add_sel(g, r, c, f"rs_add_c{k}") for k, (g, r) in enumerate(zip(grads, r1))]
    a1 = [a.reshape((2, 2) + a.shape[1:]) for a in a1]
    r2 = _exchange(a1, "x", "rs_x")
    a2 = [_add_sel(a, r, x, f"rs_add_x{k}") for k, (a, r) in enumerate(zip(a1, r2))]
    r3 = _exchange(a2, "y", "rs_y")
    a3 = [_add_sel(a, r, y, f"rs_add_y{k}") for k, (a, r) in enumerate(zip(a2, r3))]
    return _share_layers(a3, "rs_share")


def _all_reduce(buf):
    for axis in ("c", "x", "y"):
        (other,) = _exchange([buf], axis, f"ar_{axis}", half=False)
        buf = _add(buf, other, f"ar_add_{axis}")
    return buf


def _pad128(v):
    return jnp.zeros((1, 128), F32).at[0, :v.shape[0]].set(v)


def _layer_fwd(l, x, p):
    h1 = _rms_fwd(x, p["norm1_w"], f"rms1_{l}")
    proj = _matmul(h1, p["w_in"], "nn", name=f"mm_in_{l}", tn=768)
    y_pool = _pool_fwd(proj, p["pool_w"], p["pool_b"], p["pool_scale"], f"pool_{l}")
    qkv = _gdn_conv_fwd(proj, p["gdn_conv_w"], f"gconv_{l}")
    alog, dtb = _pad128(p["gdn_a_log"]), _pad128(p["gdn_dt_bias"])
    gw, gu, gqd, gkd, gat, gc = _gdn_prep(qkv, proj, alog, dtb, f"gprep_{l}")
    o, states = _gdn_scan(gw, gu, gqd, gkd, gat, gc, f"gscan_{l}")
    hl = _lru_fwd(proj, p["lru_conv_w"], p["lru_conv_b"], p["lru_wa"], p["lru_ba"], p["lru_wx"], p["lru_bx"],
                  p["lru_lambda"], f"lru_{l}")
    mixed = _mix_out(o, proj, hl, y_pool, p["gdn_norm_w"].reshape(1, GDN_DH), f"mix_{l}")
    x2 = _matmul(mixed, p["w_out"], "nn", name=f"mm_out_{l}", res=x)
    h2 = _rms_fwd(x2, p["norm2_w"], f"rms2_{l}")
    up = _matmul(h2, p["ffn_up"], "nn", name=f"mm_up_{l}", b_split=True)
    act = _ffn_act_fwd(up, p["ffn_conv_w"], f"ffn_{l}")
    x3 = _matmul(act, p["ffn_down"], "nn", name=f"mm_down_{l}", res=x2)
    saved = dict(x=x, h1=h1, proj=proj, qkv=qkv, gdn=(gw, gu, gqd, gkd, gat, gc), states=states, o=o, hl=hl, mixed=mixed,
                 x2=x2, h2=h2, up=up, act=act, alog=alog, dtb=dtb)
    return x3, saved


def _layer_bwd(l, dx3, p, s):
    g = {}
    dact = _matmul(dx3, p["ffn_down"], "nt", name=f"mm_ddown_{l}")
    g["ffn_down"] = _matmul(s["act"], dx3, "tn", name=f"mm_gdown_{l}")
    dgate, dval, g["ffn_conv_w"] = _ffn_act_bwd(s["up"], dact, p["ffn_conv_w"], f"ffn_b_{l}")
    dup = jnp.concatenate([dgate, dval], axis=1)
    dh2 = _matmul(dup, p["ffn_up"], "nt", name=f"mm_dup_{l}", b_split=True, tk=3072)
    g["ffn_up"] = _matmul(s["h2"], dup, "tn", name=f"mm_gup_{l}", o_split=4, tk=4096)
    dx2, g["norm2_w"] = _rms_bwd(s["x2"], p["norm2_w"], dh2, dx3, f"rms2_b_{l}")
    dmixed = _matmul(dx2, p["w_out"], "nt", name=f"mm_dout_{l}")
    g["w_out"] = _matmul(s["mixed"], dx2, "tn", name=f"mm_gout_{l}")
    proj = s["proj"]
    du_pool, g["pool_w"], g["pool_b"], g["pool_scale"] = _pool_bwd(proj, dmixed, p["pool_w"], p["pool_b"], p["pool_scale"], f"pool_b_{l}")
    gw, gu, gqd, gkd, gat, gc = s["gdn"]
    dw, du, dqd, dkd, dat, dgl, dz, g["gdn_norm_w"] = _gdn_scan_bwd(
        gw, gu, gqd, gkd, gat, gc, s["states"], s["o"], proj, dmixed, p["gdn_norm_w"].reshape(1, GDN_DH), f"gscan_b_{l}")
    dqkv, dab, gal, gdt = _gdn_prep_bwd(s["qkv"], proj, s["alog"], s["dtb"], dw, du, dqd, dkd, dat, dgl, f"gprep_b_{l}")
    g["gdn_a_log"], g["gdn_dt_bias"] = gal[0, :GDN_H], gdt[0, :GDN_H]
    dpre, g["gdn_conv_w"] = _gdn_conv_bwd(proj, dqkv, p["gdn_conv_w"], f"gconv_b_{l}")
    (dxr, dgr, g["lru_conv_w"], g["lru_conv_b"], g["lru_wa"], g["lru_ba"], g["lru_wx"], g["lru_bx"], g["lru_lambda"]) = _lru_bwd(
        proj, s["hl"], dmixed, p["lru_conv_w"], p["lru_conv_b"], p["lru_wa"], p["lru_ba"], p["lru_wx"], p["lru_bx"],
        p["lru_lambda"], f"lru_b_{l}")
    S = proj.shape[0]
    dproj = jnp.concatenate([dpre, dz, dxr, dgr, du_pool, dab, jnp.zeros((S, PCOLS - PAB - 128), BF16)], axis=1)
    dh1 = _matmul(dproj, p["w_in"], "nt", name=f"mm_din_{l}", tk=1792)
    g["w_in"] = _matmul(s["h1"], dproj, "tn", name=f"mm_gin_{l}", tn=768, tk=4096)
    dx, g["norm1_w"] = _rms_bwd(s["x"], p["norm1_w"], dh1, dx2, f"rms1_b_{l}")
    return dx, g


_IN_PERM = ((512, 3584), (3596, 5132), (0, 512), (3584, 3596))


def _w_in_to_proj(w):
    parts = [w[:, a:b] for a, b in _IN_PERM]
    return jnp.concatenate(parts + [jnp.zeros((w.shape[0], PCOLS - IN_COLS), w.dtype)], axis=1)


def _proj_to_w_in(g):
    return jnp.concatenate([g[:, PPOOL:PPOOL + 512], g[:, 0:3072], g[:, PAB:PAB + 12], g[:, 3072:PPOOL]], axis=1)


def _rows_to_mixed(w):
    return jnp.concatenate([w[512:], w[:512]], axis=0)


def _mixed_to_rows(g):
    return jnp.concatenate([g[1536:], g[:1536]], axis=0)


SMALL_SHARDED = ("gdn_conv_w", "lru_conv_w", "ffn_conv_w")
BIG = ("w_in", "w_out", "ffn_up", "ffn_down")
SMALL_REPLICATED = ("norm1_w", "pool_w", "pool_b", "pool_scale", "gdn_a_log", "gdn_dt_bias", "gdn_norm_w", "lru_conv_b",
                    "lru_wa", "lru_ba", "lru_wx", "lru_bx", "lru_lambda", "norm2_w")
WEIGHTS = ("norm1_w", "w_in", "pool_w", "pool_b", "pool_scale", "gdn_conv_w", "gdn_a_log", "gdn_dt_bias", "gdn_norm_w",
           "lru_conv_w", "lru_conv_b", "lru_wa", "lru_ba", "lru_wx", "lru_bx", "lru_lambda", "w_out", "norm2_w", "ffn_up",
           "ffn_conv_w", "ffn_down", "final_norm_w")
FLAT_COLS = 1024


def _pack(arrs):
    flat = jnp.concatenate([a.reshape(-1) for a in arrs])
    rows = -(-flat.shape[0] // (8 * FLAT_COLS)) * 8
    return jnp.pad(flat, (0, rows * FLAT_COLS - flat.shape[0])).reshape(rows, FLAT_COLS)


def _unpack(buf, like):
    flat = buf.reshape(-1)
    out, off = [], 0
    for a in like:
        size = 1
        for d in a.shape:
            size *= d
        out.append(flat[off:off + size].reshape(a.shape))
        off += size
    return out


def kernel(x, norm1_w, w_in, pool_w, pool_b, pool_scale, gdn_conv_w, gdn_a_log, gdn_dt_bias, gdn_norm_w, lru_conv_w, lru_conv_b, lru_wa, lru_ba, lru_wx, lru_bx, lru_lambda, w_out, norm2_w, ffn_up, ffn_conv_w, ffn_down, final_norm_w, loss_target, m_norm1_w, m_w_in, m_pool_w, m_pool_b, m_pool_scale, m_gdn_conv_w, m_gdn_a_log, m_gdn_dt_bias, m_gdn_norm_w, m_lru_conv_w, m_lru_conv_b, m_lru_wa, m_lru_ba, m_lru_wx, m_lru_bx, m_lru_lambda, m_w_out, m_norm2_w, m_ffn_up, m_ffn_conv_w, m_ffn_down, m_final_norm_w, v_norm1_w, v_w_in, v_pool_w, v_pool_b, v_pool_scale, v_gdn_conv_w, v_gdn_a_log, v_gdn_dt_bias, v_gdn_norm_w, v_lru_conv_w, v_lru_conv_b, v_lru_wa, v_lru_ba, v_lru_wx, v_lru_bx, v_lru_lambda, v_w_out, v_norm2_w, v_ffn_up, v_ffn_conv_w, v_ffn_down, v_final_norm_w):
    W = dict(norm1_w=norm1_w, w_in=w_in, pool_w=pool_w, pool_b=pool_b, pool_scale=pool_scale, gdn_conv_w=gdn_conv_w,
             gdn_a_log=gdn_a_log, gdn_dt_bias=gdn_dt_bias, gdn_norm_w=gdn_norm_w, lru_conv_w=lru_conv_w, lru_conv_b=lru_conv_b,
             lru_wa=lru_wa, lru_ba=lru_ba, lru_wx=lru_wx, lru_bx=lru_bx, lru_lambda=lru_lambda, w_out=w_out, norm2_w=norm2_w,
             ffn_up=ffn_up, ffn_conv_w=ffn_conv_w, ffn_down=ffn_down, final_norm_w=final_norm_w)
    M = dict(norm1_w=m_norm1_w, w_in=m_w_in, pool_w=m_pool_w, pool_b=m_pool_b, pool_scale=m_pool_scale, gdn_conv_w=m_gdn_conv_w,
             gdn_a_log=m_gdn_a_log, gdn_dt_bias=m_gdn_dt_bias, gdn_norm_w=m_gdn_norm_w, lru_conv_w=m_lru_conv_w,
             lru_conv_b=m_lru_conv_b, lru_wa=m_lru_wa, lru_ba=m_lru_ba, lru_wx=m_lru_wx, lru_bx=m_lru_bx, lru_lambda=m_lru_lambda,
             w_out=m_w_out, norm2_w=m_norm2_w, ffn_up=m_ffn_up, ffn_conv_w=m_ffn_conv_w, ffn_down=m_ffn_down,
             final_norm_w=m_final_norm_w)
    V = dict(norm1_w=v_norm1_w, w_in=v_w_in, pool_w=v_pool_w, pool_b=v_pool_b, pool_scale=v_pool_scale, gdn_conv_w=v_gdn_conv_w,
             gdn_a_log=v_gdn_a_log, gdn_dt_bias=v_gdn_dt_bias, gdn_norm_w=v_gdn_norm_w, lru_conv_w=v_lru_conv_w,
             lru_conv_b=v_lru_conv_b, lru_wa=v_lru_wa, lru_ba=v_lru_ba, lru_wx=v_lru_wx, lru_bx=v_lru_bx, lru_lambda=v_lru_lambda,
             w_out=v_w_out, norm2_w=v_norm2_w, ffn_up=v_ffn_up, ffn_conv_w=v_ffn_conv_w, ffn_down=v_ffn_down,
             final_norm_w=v_final_norm_w)
    S = x.shape[1]
    xs = x.reshape(S, D_MODEL)
    tgt = loss_target.reshape(S, D_MODEL)
    mx, my, mc = _place()
    shard = 2 * mx + my

    small_sh = jnp.concatenate([W[k].reshape(N_LAYERS, -1) for k in SMALL_SHARDED], axis=1)
    n_small = small_sh.shape[1]
    pad = -n_small % 1024
    small_sh = jnp.pad(small_sh, ((0, 0), (0, pad))).reshape(N_LAYERS, -1, 1024)
    gathered = _gather_weights([W[k].astype(BF16) for k in BIG] + [small_sh], "gather_weights")
    g_in, g_out, g_up, g_down, g_small = gathered
    g_small = g_small.reshape(N_LAYERS, 4, -1)[:, :, :n_small]

    layers = []
    for l in range(N_LAYERS):
        p = {k: W[k][l] for k in SMALL_REPLICATED}
        rows = g_in.shape[2]
        p["w_in"] = _w_in_to_proj(jnp.transpose(g_in[l], (1, 0, 2)).reshape(rows, IN_COLS))
        p["w_out"] = _rows_to_mixed(g_out[l].reshape(D_MODEL, D_MODEL))
        p["ffn_up"] = g_up[l]
        p["ffn_down"] = g_down[l].reshape(D_FF, D_MODEL)
        off = 0
        for k in SMALL_SHARDED:
            taps, width = W[k].shape[1], W[k].shape[2]
            piece = g_small[l, :, off:off + taps * width].reshape(4, taps, width)
            p[k] = jnp.transpose(piece, (1, 0, 2)).reshape(taps, 4 * width)
            off += taps * width
        layers.append(p)

    saved = []
    h = xs
    for l in range(N_LAYERS):
        h, s = _layer_fwd(l, h, layers[l])
        saved.append(s)
    loss_part, dh, g_final = _loss_head(h, final_norm_w, tgt, "loss_head")

    grads = [None] * N_LAYERS
    for l in reversed(range(N_LAYERS)):
        dh, grads[l] = _layer_bwd(l, dh, layers[l], saved[l])
    grad_x = dh.reshape(x.shape)

    def big_partial(k):
        per = []
        for l in range(N_LAYERS):
            g = grads[l][k]
            if k == "w_in":
                g = _proj_to_w_in(g)
                g = jnp.transpose(g.reshape(g.shape[0], 4, IN_COLS // 4), (1, 0, 2))
            elif k == "w_out":
                g = _mixed_to_rows(g).reshape(4, D_MODEL // 4, D_MODEL)
            elif k == "ffn_down":
                g = g.reshape(4, D_FF // 4, D_MODEL)
            per.append(g)
        return jnp.stack(per)

    big_g = dict(zip(BIG, _reduce_scatter([big_partial(k) for k in BIG])))

    small_names = SMALL_REPLICATED + SMALL_SHARDED
    small_list = [jnp.stack([grads[l][k].reshape(W[k].shape[1:]) if k in SMALL_REPLICATED else grads[l][k] for l in range(N_LAYERS)])
                  for k in small_names]
    small_list += [g_final.reshape(D_MODEL), loss_part[0, 0:1]]
    reduced = _unpack(_all_reduce(_pack(small_list)), small_list)
    small_g = dict(zip(small_names, reduced[:len(small_names)]))
    small_g["final_norm_w"] = reduced[-2]
    loss = reduced[-1][0]
    for k in SMALL_SHARDED:
        width = W[k].shape[2]
        small_g[k] = lax.dynamic_slice_in_dim(small_g[k], shard * width, width, axis=2)

    G, DELTA, NM, NV = {}, {}, {}, {}
    for k in BIG:
        G[k] = big_g[k]
        DELTA[k], NM[k], NV[k] = _adamw(W[k], G[k], M[k], V[k], f"adam_{k}")
    small_all = small_names + ("final_norm_w",)
    dl, nm, nv = _adamw(_pack([W[k] for k in small_all]), _pack([small_g[k] for k in small_all]),
                        _pack([M[k] for k in small_all]), _pack([V[k] for k in small_all]), "adam_small")
    like = [W[k] for k in small_all]
    for k, d_, m_, v_ in zip(small_all, _unpack(dl, like), _unpack(nm, like), _unpack(nv, like)):
        G[k], DELTA[k], NM[k], NV[k] = small_g[k], d_, m_, v_

    return (loss, grad_x, *[G[k] for k in WEIGHTS], *[DELTA[k] for k in WEIGHTS], *[NM[k] for k in WEIGHTS],
            *[NV[k] for k in WEIGHTS])
```

```python
import functools

import jax
import jax.numpy as jnp
from jax import lax
from jax.experimental import pallas as pl
from jax.experimental.pallas import tpu as pltpu

F32 = jnp.float32
BF16 = jnp.bfloat16
_MXU = jnp.bfloat16

D_MODEL = 2048
N_LAYERS = 2
POOL_W = 512
POOL_G = 4
POOL_GD = 128
POOL_WINDOWS = (2, 4, 8, 16)
POOL_HALO = 16
GDN_W = 768
GDN_H = 6
GDN_DH = 128
GDN_C = 64
LRU_W = 768
LRU_NB = 6
LRU_BD = 128
LRU_C = 8.0
D_FF = 6144
EPS = 1e-6
IN_COLS = 5132
HALO = 8

PQ, PK, PV, PZ, PXR, PGR, PPOOL, PAB, PCOLS = 0, 768, 1536, 2304, 3072, 3840, 4608, 5120, 5376
CB = 768

ADAM_LR = 0.001
ADAM_B1 = 0.9
ADAM_B2 = 0.999
ADAM_EPS = 1e-08
ADAM_WD = 0.01
ADAM_STEP = 10

VMEM_LIMIT = 56 * 1024 * 1024
MESH = pl.DeviceIdType.MESH
HBM = pl.BlockSpec(memory_space=pltpu.HBM)


def _cp(*sem):
    return pltpu.CompilerParams(dimension_semantics=sem, vmem_limit_bytes=VMEM_LIMIT)


def _dg(a, b, ta, tb):
    dims = (((0 if ta else 1,), (1 if tb else 0,)), ((), ()))
    return lax.dot_general(a, b, dims, preferred_element_type=F32)


def _split2(a):
    hi = a.astype(BF16)
    lo = (a - hi.astype(F32)).astype(BF16)
    return hi, lo


def _mm_raw(a, b, ta, tb, hi):
    if _MXU == F32:
        return _dg(a, b, ta, tb)
    if not hi:
        return _dg(a.astype(_MXU), b.astype(_MXU), ta, tb)
    a1, a2 = _split2(a)
    b1, b2 = _split2(b)
    return _dg(a1, b1, ta, tb) + (_dg(a1, b2, ta, tb) + _dg(a2, b1, ta, tb))


@functools.partial(jax.custom_vjp, nondiff_argnums=(2, 3, 4))
def _mm(a, b, ta=False, tb=False, hi=False):
    return _mm_raw(a, b, ta, tb, hi)


def _mm_fwd(a, b, ta, tb, hi):
    return _mm_raw(a, b, ta, tb, hi), (a, b)


def _mm_bwd(ta, tb, hi, res, dc):
    a, b = res
    da = _mm(b, dc, tb, True, hi) if ta else _mm(dc, b, False, not tb, hi)
    db = _mm(dc, a, True, ta, hi) if tb else _mm(a, dc, not ta, False, hi)
    return da, db


_mm.defvjp(_mm_fwd, _mm_bwd)


def _mm01(m01, x):
    if _MXU == F32:
        return _dg(m01, x, False, False)
    m = m01.astype(BF16)
    x1 = x.astype(BF16)
    r = x - x1.astype(F32)
    x2 = r.astype(BF16)
    x3 = (r - x2.astype(F32)).astype(BF16)
    return _dg(m, x1, False, False) + (_dg(m, x2, False, False) + _dg(m, x3, False, False))


def _down(x, k):
    return x if k == 0 else pltpu.roll(x, k, 0)


def _up(x, k):
    return x if k == 0 else pltpu.roll(x, x.shape[0] - k, 0)


def _rows(shape):
    return lax.broadcasted_iota(jnp.int32, shape, 0)


def _lanes(shape):
    return lax.broadcasted_iota(jnp.int32, shape, 1)


def _matmul(a, b, mode, *, name, res=None, tm=1024, tn=1024, tk=2048, b_split=False, o_split=0, out_dtype=F32):
    ta, tb = mode == "tn", mode == "nt"
    a_split = a.ndim == 3
    if a_split:
        assert not ta
        M, K = a.shape[1], a.shape[0] * a.shape[2]
        tk = min(tk, a.shape[2])
    elif ta:
        K, M = a.shape
    else:
        M, K = a.shape
    if b_split:
        ns = b.shape[0]
        N = b.shape[1] if tb else ns * b.shape[2]
    else:
        N = b.shape[0] if tb else b.shape[1]
    tm, tn, tk = min(tm, M), min(tn, N), min(tk, K)
    if b_split:
        per = b.shape[2]
        if tb:
            tk = min(tk, per)
        else:
            tn = min(tn, per)
    if o_split:
        tn = min(tn, N // o_split)
    assert M % tm == 0 and N % tn == 0 and K % tk == 0, (name, M, N, K, tm, tn, tk)
    nk = K // tk
    if a_split:
        ka = a.shape[2] // tk
        a_spec = pl.BlockSpec((None, tm, tk), lambda i, j, k: (k // ka, i, k % ka))
    else:
        a_spec = pl.BlockSpec((tk, tm), lambda i, j, k: (k, i)) if ta else pl.BlockSpec((tm, tk), lambda i, j, k: (i, k))
    if not b_split:
        b_spec = pl.BlockSpec((tn, tk), lambda i, j, k: (j, k)) if tb else pl.BlockSpec((tk, tn), lambda i, j, k: (k, j))
    elif tb:
        kb = per // tk
        b_spec = pl.BlockSpec((None, tn, tk), lambda i, j, k: (k // kb, j, k % kb))
    else:
        nb = per // tn
        b_spec = pl.BlockSpec((None, tk, tn), lambda i, j, k: (j // nb, k, j % nb))
    if o_split:
        ob = (N // o_split) // tn
        out_shape = jax.ShapeDtypeStruct((o_split, M, N // o_split), out_dtype)
        o_spec = pl.BlockSpec((None, tm, tn), lambda i, j, k: (j // ob, i, j % ob))
    else:
        out_shape = jax.ShapeDtypeStruct((M, N), out_dtype)
        o_spec = pl.BlockSpec((tm, tn), lambda i, j, k: (i, j))
    in_specs = [a_spec, b_spec]
    args = [a, b]
    if res is not None:
        in_specs.append(pl.BlockSpec((tm, tn), lambda i, j, k: (i, j)))
        args.append(res)
    use_acc = nk > 1 and out_dtype != F32

    def body(*refs):
        a_ref, b_ref = refs[0], refs[1]
        o_ref = refs[2 + (res is not None)]
        acc_ref = refs[-1] if use_acc else o_ref
        p = _dg(a_ref[...].astype(_MXU), b_ref[...].astype(_MXU), ta, tb)
        first = p + refs[2][...] if res is not None else p
        if nk == 1:
            o_ref[...] = first.astype(o_ref.dtype)
        else:
            k = pl.program_id(2)

            @pl.when(k == 0)
            def _():
                acc_ref[...] = first

            @pl.when(k > 0)
            def _():
                acc_ref[...] += p

            if use_acc:
                @pl.when(k == nk - 1)
                def _():
                    o_ref[...] = acc_ref[...].astype(o_ref.dtype)

    return pl.pallas_call(
        body, name=name, grid=(M // tm, N // tn, nk), in_specs=in_specs, out_specs=o_spec, out_shape=out_shape,
        scratch_shapes=[pltpu.VMEM((tm, tn), F32)] if use_acc else [],
        compiler_params=_cp("parallel", "parallel", "arbitrary"),
    )(*args)


def _rms(x, w):
    return x * lax.rsqrt(jnp.mean(x * x, axis=-1, keepdims=True) + EPS) * w


def _row_tile(S, t=512):
    t = min(t, S)
    assert S % t == 0
    return t


def _rms_fwd(x, w, name):
    S, D = x.shape
    T = _row_tile(S)

    def body(x_ref, w_ref, o_ref):
        o_ref[...] = _rms(x_ref[...], w_ref[...]).astype(o_ref.dtype)

    return pl.pallas_call(
        body, name=name, grid=(S // T,),
        in_specs=[pl.BlockSpec((T, D), lambda i: (i, 0)), pl.BlockSpec((1, D), lambda i: (0, 0))],
        out_specs=pl.BlockSpec((T, D), lambda i: (i, 0)), out_shape=jax.ShapeDtypeStruct((S, D), BF16),
        compiler_params=_cp("parallel"),
    )(x, w.reshape(1, D))


def _rms_bwd(x, w, dh, dres, name):
    S, D = x.shape
    T = _row_tile(S)

    def body(x_ref, w_ref, dh_ref, dr_ref, dx_ref, gw_ref):
        _, vjp = jax.vjp(_rms, x_ref[...], w_ref[...])
        dx, dw = vjp(dh_ref[...])
        dx_ref[...] = dr_ref[...] + dx

        @pl.when(pl.program_id(0) == 0)
        def _():
            gw_ref[...] = jnp.zeros_like(gw_ref)

        gw_ref[...] += dw

    row = pl.BlockSpec((T, D), lambda i: (i, 0))
    vec = pl.BlockSpec((1, D), lambda i: (0, 0))
    return pl.pallas_call(
        body, name=name, grid=(S // T,), in_specs=[row, vec, row, row], out_specs=[row, vec],
        out_shape=[jax.ShapeDtypeStruct((S, D), F32), jax.ShapeDtypeStruct((1, D), F32)],
        compiler_params=_cp("arbitrary"),
    )(x, w.reshape(1, D), dh, dres)


def _loss_head(x, w, tgt, name):
    S, D = x.shape
    T = _row_tile(S)

    def body(x_ref, w_ref, t_ref, l_ref, dx_ref, gw_ref):
        y, vjp = jax.vjp(_rms, x_ref[...], w_ref[...])
        err = y - t_ref[...]
        part = 0.5 * jnp.sum(jnp.mean(err * err, axis=-1, keepdims=True), axis=0, keepdims=True)
        dx, dw = vjp(err * (1.0 / D))
        dx_ref[...] = dx

        @pl.when(pl.program_id(0) == 0)
        def _():
            gw_ref[...] = jnp.zeros_like(gw_ref)
            l_ref[...] = jnp.zeros_like(l_ref)

        gw_ref[...] += dw
        l_ref[...] += jnp.broadcast_to(part, l_ref.shape)

    row = pl.BlockSpec((T, D), lambda i: (i, 0))
    vec = pl.BlockSpec((1, D), lambda i: (0, 0))
    return pl.pallas_call(
        body, name=name, grid=(S // T,), in_specs=[row, vec, row],
        out_specs=[pl.BlockSpec((8, 128), lambda i: (0, 0)), row, vec],
        out_shape=[jax.ShapeDtypeStruct((8, 128), F32), jax.ShapeDtypeStruct((S, D), F32), jax.ShapeDtypeStruct((1, D), F32)],
        compiler_params=_cp("arbitrary"),
    )(x, w.reshape(1, D), tgt)


def _by_group(shape, vals):
    g = _lanes(shape) // POOL_GD
    out = vals[-1]
    for k in range(len(vals) - 2, -1, -1):
        out = jnp.where(g == k, vals[k], out)
    return out


def _pool_d(prev, u, t0):
    ext = jnp.concatenate([prev, u], axis=0)
    s2 = ext + _down(ext, 1)
    s4 = s2 + _down(s2, 2)
    s8 = s4 + _down(s4, 4)
    s16 = s8 + _down(s8, 8)
    ssel = _by_group(ext.shape, [s2, s4, s8, s16])[POOL_HALO:]
    win = _by_group(u.shape, [jnp.int32(w) for w in POOL_WINDOWS])
    cnt = jnp.minimum(t0 + _rows(u.shape) + 1, win).astype(F32)
    return ssel / cnt - u


def _pool_lin(d, w_ref, b):
    ys = [_mm(d[:, g * POOL_GD:(g + 1) * POOL_GD], w_ref[g]) for g in range(POOL_G)]
    return jnp.concatenate(ys, axis=1) + b


def _pool_fwd(proj, w, b, scale, name):
    S = proj.shape[0]
    T = _row_tile(S)
    r = T // POOL_HALO
    cb = PPOOL // POOL_W

    def body(u_ref, up_ref, w_ref, b_ref, sc_ref, y_ref):
        i = pl.program_id(0)
        prev = jnp.where(i > 0, up_ref[...], 0.0)
        d = _pool_d(prev, u_ref[...], i * T)
        y_ref[...] = _pool_lin(d, w_ref, b_ref[...]) * sc_ref[...]

    vec = pl.BlockSpec((1, POOL_W), lambda i: (0, 0))
    return pl.pallas_call(
        body, name=name, grid=(S // T,),
        in_specs=[pl.BlockSpec((T, POOL_W), lambda i: (i, cb)),
                  pl.BlockSpec((POOL_HALO, POOL_W), lambda i: (jnp.maximum(i * r - 1, 0), cb)),
                  pl.BlockSpec((POOL_G, POOL_GD, POOL_GD), lambda i: (0, 0, 0)), vec, vec],
        out_specs=pl.BlockSpec((T, POOL_W), lambda i: (i, 0)), out_shape=jax.ShapeDtypeStruct((S, POOL_W), F32),
        compiler_params=_cp("parallel"),
    )(proj, proj, w, b.reshape(1, POOL_W), scale.reshape(1, POOL_W))


def _pool_bwd(proj, dmixed, w, b, scale, name):
    S = proj.shape[0]
    T = _row_tile(S)
    n = S // T
    r = T // POOL_HALO
    cb = PPOOL // POOL_W
    mb = 1536 // POOL_W

    def body(u_ref, up_ref, dy_ref, dyn_ref, w_ref, b_ref, sc_ref, du_ref, gw_ref, gb_ref, gs_ref):
        i = pl.program_id(0)
        sc = sc_ref[...]
        dy = dy_ref[...]
        dy_ext = jnp.concatenate([dy, jnp.where(i < n - 1, dyn_ref[...], 0.0)], axis=0)
        dyl = dy_ext * sc
        dd = jnp.concatenate(
            [_mm(dyl[:, g * POOL_GD:(g + 1) * POOL_GD], w_ref[g], False, True) for g in range(POOL_G)], axis=1)
        t_ext = i * T + _rows(dd.shape)
        win = _by_group(dd.shape, [jnp.int32(v) for v in POOL_WINDOWS])
        cnt = jnp.minimum(t_ext + 1, win).astype(F32)
        e = jnp.where(t_ext < S, dd / cnt, 0.0)
        f2 = e + _up(e, 1)
        f4 = f2 + _up(f2, 2)
        f8 = f4 + _up(f4, 4)
        f16 = f8 + _up(f8, 8)
        du = (_by_group(dd.shape, [f2, f4, f8, f16]) - dd)[:T]
        du_ref[...] = du.astype(du_ref.dtype)

        prev = jnp.where(i > 0, up_ref[...], 0.0)
        d = _pool_d(prev, u_ref[...], i * T)
        ylin = _pool_lin(d, w_ref, b_ref[...])
        dyl_m = dy * sc

        @pl.when(i == 0)
        def _():
            gw_ref[...] = jnp.zeros_like(gw_ref)
            gb_ref[...] = jnp.zeros_like(gb_ref)
            gs_ref[...] = jnp.zeros_like(gs_ref)

        gs_ref[...] += jnp.sum(dy * ylin, axis=0, keepdims=True)
        gb_ref[...] += jnp.sum(dyl_m, axis=0, keepdims=True)
        for g in range(POOL_G):
            sl = slice(g * POOL_GD, (g + 1) * POOL_GD)
            gw_ref[g] += _mm(d[:, sl], dyl_m[:, sl], True, False)

    vec = pl.BlockSpec((1, POOL_W), lambda i: (0, 0))
    wsp = pl.BlockSpec((POOL_G, POOL_GD, POOL_GD), lambda i: (0, 0, 0))
    nh = S // POOL_HALO
    return pl.pallas_call(
        body, name=name, grid=(n,),
        in_specs=[pl.BlockSpec((T, POOL_W), lambda i: (i, cb)),
                  pl.BlockSpec((POOL_HALO, POOL_W), lambda i: (jnp.maximum(i * r - 1, 0), cb)),
                  pl.BlockSpec((T, POOL_W), lambda i: (i, mb)),
                  pl.BlockSpec((POOL_HALO, POOL_W), lambda i: (jnp.minimum((i + 1) * r, nh - 1), mb)),
                  wsp, vec, vec],
        out_specs=[pl.BlockSpec((T, POOL_W), lambda i: (i, 0)), wsp, vec, vec],
        out_shape=[jax.ShapeDtypeStruct((S, POOL_W), BF16), jax.ShapeDtypeStruct((POOL_G, POOL_GD, POOL_GD), F32),
                   jax.ShapeDtypeStruct((1, POOL_W), F32), jax.ShapeDtypeStruct((1, POOL_W), F32)],
        compiler_params=_cp("arbitrary"),
    )(proj, proj, dmixed, dmixed, w, b.reshape(1, POOL_W), scale.reshape(1, POOL_W))


def _conv_rows(ext, w_ref, taps):
    acc = w_ref[taps - 1:taps, :] * ext
    for k in range(1, taps):
        acc = acc + w_ref[taps - 1 - k:taps - k, :] * _down(ext, k)
    return acc


def _conv_t_rows(dc, w_ref, taps):
    acc = w_ref[taps - 1:taps, :] * dc
    for k in range(1, taps):
        acc = acc + w_ref[taps - 1 - k:taps - k, :] * _up(dc, k)
    return acc


def _conv_specs(T, S, ncb0, with_next):
    r = T // HALO
    nh = S // HALO
    main = pl.BlockSpec((T, CB), lambda j, i: (i, j + ncb0))
    prev = pl.BlockSpec((HALO, CB), lambda j, i: (jnp.maximum(i * r - 1, 0), j + ncb0))
    nxt = pl.BlockSpec((HALO, CB), lambda j, i: (jnp.minimum((i + 1) * r, nh - 1), j + ncb0))
    return (main, prev, nxt) if with_next else (main, prev)


def _gdn_conv_fwd(proj, w, name):
    S = proj.shape[0]
    T = _row_tile(S)
    taps = w.shape[0]
    ncb = 3 * GDN_W // CB

    def body(x_ref, xp_ref, w_ref, o_ref):
        i = pl.program_id(1)
        ext = jnp.concatenate([jnp.where(i > 0, xp_ref[...], 0.0), x_ref[...]], axis=0)
        o_ref[...] = jax.nn.silu(_conv_rows(ext, w_ref, taps)[HALO:])

    main, prev = _conv_specs(T, S, PQ // CB, False)
    return pl.pallas_call(
        body, name=name, grid=(ncb, S // T),
        in_specs=[main, prev, pl.BlockSpec((taps, CB), lambda j, i: (0, j))],
        out_specs=pl.BlockSpec((T, CB), lambda j, i: (i, j)), out_shape=jax.ShapeDtypeStruct((S, 3 * GDN_W), F32),
        compiler_params=_cp("parallel", "parallel"),
    )(proj, proj, w)


def _gdn_conv_bwd(proj, dact, w, name):
    S = proj.shape[0]
    T = _row_tile(S)
    n = S // T
    taps = w.shape[0]
    ncb = 3 * GDN_W // CB

    def body(x_ref, xp_ref, xn_ref, d_ref, dn_ref, w_ref, dx_ref, gw_ref):
        i = pl.program_id(1)
        last = i == n - 1
        ext = jnp.concatenate([jnp.where(i > 0, xp_ref[...], 0.0), x_ref[...], jnp.where(last, 0.0, xn_ref[...])], axis=0)
        c = _conv_rows(ext, w_ref, taps)[HALO:]
        d_ext = jnp.concatenate([d_ref[...], jnp.where(last, 0.0, dn_ref[...])], axis=0)
        _, vjp = jax.vjp(jax.nn.silu, c)
        dc = vjp(d_ext)[0]
        dx_ref[...] = _conv_t_rows(dc, w_ref, taps)[:T].astype(dx_ref.dtype)

        @pl.when(i == 0)
        def _():
            gw_ref[...] = jnp.zeros_like(gw_ref)

        dcm = dc[:T]
        for k in range(taps):
            gw_ref[taps - 1 - k:taps - k, :] += jnp.sum(dcm * _down(ext, k)[HALO:HALO + T], axis=0, keepdims=True)

    main, prev, nxt = _conv_specs(T, S, PQ // CB, True)
    dmain, _, dnxt = _conv_specs(T, S, 0, True)
    wsp = pl.BlockSpec((taps, CB), lambda j, i: (0, j))
    return pl.pallas_call(
        body, name=name, grid=(ncb, n), in_specs=[main, prev, nxt, dmain, dnxt, wsp],
        out_specs=[pl.BlockSpec((T, CB), lambda j, i: (i, j)), wsp],
        out_shape=[jax.ShapeDtypeStruct((S, 3 * GDN_W), BF16), jax.ShapeDtypeStruct((taps, 3 * GDN_W), F32)],
        compiler_params=_cp("parallel", "arbitrary"),
    )(proj, proj, proj, dact, dact, w)


def _ffn_act_fwd(up, w, name):
    S = up.shape[0]
    T = _row_tile(S)
    taps = w.shape[0]
    ncb = D_FF // CB

    def body(g_ref, gp_ref, v_ref, w_ref, o_ref):
        i = pl.program_id(1)
        ext = jnp.concatenate([jnp.where(i > 0, gp_ref[...], 0.0), g_ref[...]], axis=0)
        c = _conv_rows(ext, w_ref, taps)[HALO:]
        o_ref[...] = (jax.nn.gelu(c) * v_ref[...]).astype(o_ref.dtype)

    main, prev = _conv_specs(T, S, 0, False)
    val = pl.BlockSpec((T, CB), lambda j, i: (i, j + ncb))
    return pl.pallas_call(
        body, name=name, grid=(ncb, S // T),
        in_specs=[main, prev, val, pl.BlockSpec((taps, CB), lambda j, i: (0, j))],
        out_specs=pl.BlockSpec((T, CB), lambda j, i: (i, j)), out_shape=jax.ShapeDtypeStruct((S, D_FF), BF16),
        compiler_params=_cp("parallel", "parallel"),
    )(up, up, up, w)


def _ffn_act_bwd(up, dact, w, name):
    S = up.shape[0]
    T = _row_tile(S)
    n = S // T
    taps = w.shape[0]
    ncb = D_FF // CB

    def body(g_ref, gp_ref, gn_ref, v_ref, vn_ref, d_ref, dn_ref, w_ref, dup_ref, gw_ref):
        i = pl.program_id(1)
        last = i == n - 1
        ext = jnp.concatenate([jnp.where(i > 0, gp_ref[...], 0.0), g_ref[...], jnp.where(last, 0.0, gn_ref[...])], axis=0)
        c = _conv_rows(ext, w_ref, taps)[HALO:]
        v_ext = jnp.concatenate([v_ref[...], jnp.where(last, 0.0, vn_ref[...])], axis=0)
        d_ext = jnp.concatenate([d_ref[...], jnp.where(last, 0.0, dn_ref[...])], axis=0)
        gl, vjp = jax.vjp(jax.nn.gelu, c)
        dup_ref[1] = (d_ext * gl)[:T].astype(dup_ref.dtype)
        dc = vjp(d_ext * v_ext)[0]
        dup_ref[0] = _conv_t_rows(dc, w_ref, taps)[:T].astype(dup_ref.dtype)

        @pl.when(i == 0)
        def _():
            gw_ref[...] = jnp.zeros_like(gw_ref)

        dcm = dc[:T]
        for k in range(taps):
            gw_ref[taps - 1 - k:taps - k, :] += jnp.sum(dcm * _down(ext, k)[HALO:HALO + T], axis=0, keepdims=True)

    main, prev, nxt = _conv_specs(T, S, 0, True)
    vmain, _, vnxt = _conv_specs(T, S, ncb, True)
    wsp = pl.BlockSpec((taps, CB), lambda j, i: (0, j))
    osp = pl.BlockSpec((2, T, CB), lambda j, i: (0, i, j))
    return pl.pallas_call(
        body, name=name, grid=(ncb, n), in_specs=[main, prev, nxt, vmain, vnxt, main, nxt, wsp],
        out_specs=[osp, wsp],
        out_shape=[jax.ShapeDtypeStruct((2, S, D_FF), BF16), jax.ShapeDtypeStruct((taps, D_FF), F32)],
        compiler_params=_cp("parallel", "arbitrary"),
    )(up, up, up, up, up, dact, dact, w)


def _tri_masks():
    r = _rows((GDN_C, GDN_C))
    c = _lanes((GDN_C, GDN_C))
    return r >= c, r > c


def _each(fn, *cols):
    return tuple(fn(*args) for args in zip(*cols))


def _tri_inv_raw(lows):
    r = _rows(lows[0].shape)
    c = _lanes(lows[0].shape)
    eye = jnp.where(r == c, 1.0, 0.0)
    ps = _each(lambda low: eye - low, lows)
    lps = lows
    for _ in range(5):
        lps = _each(lambda lp: _mm(lp, lp, False, False, True), lps)
        ps = _each(lambda p, lp: p + _mm(p, lp, False, False, True), ps, lps)
    return ps


@jax.custom_vjp
def _tri_inv(lows):
    return _tri_inv_raw(lows)


def _tri_inv_fwd(lows):
    ts = _tri_inv_raw(lows)
    return ts, ts


def _tri_inv_bwd(ts, dts):
    inner = _each(lambda t, dt: _mm(t, dt, True, False, True), ts, dts)
    return (_each(lambda m, t: -_mm(m, t, False, True, True), inner, ts),)


_tri_inv.defvjp(_tri_inv_fwd, _tri_inv_bwd)


def _gdn_glog(a_col, alog, dtb):
    return -jnp.exp(alog) * jax.nn.softplus(a_col + dtb)


def _decay_operand():
    r = _rows((GDN_C, 2 * GDN_C))
    c = _lanes((GDN_C, 2 * GDN_C))
    return jnp.where((c >= GDN_C) | (r > c), 1.0, 0.0)


def _gdn_decay(glog):
    causal, _ = _tri_masks()
    res = _mm01(jnp.where(causal, 1.0, 0.0), glog * _decay_operand())
    return res[:, GDN_C:GDN_C + 1], res[:, :GDN_C]


def _gdn_decay_bwd(dgcol, dd):
    r = _rows((GDN_C, GDN_C))
    c = _lanes((GDN_C, GDN_C))
    dres = jnp.concatenate([dd, jnp.where(c == 0, dgcol, 0.0)], axis=1)
    dx = _mm01(jnp.where(r <= c, 1.0, 0.0), dres)
    return jnp.sum(dx * _decay_operand(), axis=1, keepdims=True)


def _gdn_chunk(qa, ka, va, bt_col, gcol, dmat):
    causal, strict = _tri_masks()
    qn = _each(lambda q: q * lax.rsqrt(jnp.sum(q * q, axis=-1, keepdims=True) + EPS) * (GDN_DH ** -0.5), qa)
    kn = _each(lambda k: k * lax.rsqrt(jnp.sum(k * k, axis=-1, keepdims=True) + EPS), ka)
    beta = _each(jax.nn.sigmoid, bt_col)
    eg = _each(jnp.exp, gcol)
    decay = _each(lambda d: jnp.where(causal, jnp.exp(d), 0.0), dmat)
    kk = _each(lambda k: _mm(k, k, False, True), kn)
    low = _each(lambda b, m, d: jnp.where(strict, b * m * d, 0.0), beta, kk, decay)
    t = _tri_inv(low)
    w = _each(lambda t_, k, b, e: _mm(t_, k * (b * e), False, False, True), t, kn, beta, eg)
    u = _each(lambda t_, v, b: _mm(t_, v * b, False, False, True), t, va, beta)
    attn = _each(lambda q, k, d: _mm(q, k, False, True) * d, qn, kn, decay)
    last = _rows(gcol[0].shape) == GDN_C - 1
    g_last = _each(lambda g: jnp.sum(jnp.where(last, g, 0.0), axis=0, keepdims=True), gcol)
    qd = _each(lambda q, e: q * e, qn, eg)
    kd = _each(lambda k, gl, g: k * jnp.exp(gl - g), kn, g_last, gcol)
    return w, u, qd, kd, attn


def _gdn_step(state, w, u, qd, kd, attn, egl):
    v_new = _each(lambda u_, w_, s: u_ - _mm(w_, s), u, w, state)
    o_state = _each(_mm, qd, state)
    o = _each(lambda os, a, v: os + _mm(a, v), o_state, attn, v_new)
    new = _each(lambda s, e, k, v: s * e + _mm(k, v, True, False), state, egl, kd, v_new)
    return o, new


def _heads(ref, base=0, width=GDN_DH):
    return tuple(ref[:, (base + h) * GDN_DH:(base + h) * GDN_DH + width] for h in range(GDN_H))


def _cols(a, base):
    return tuple(a[:, base + h:base + h + 1] for h in range(GDN_H))


def _gated_norm(o, z, nw):
    return o * lax.rsqrt(jnp.mean(o * o, axis=-1, keepdims=True) + EPS) * nw * jax.nn.silu(z)


def _hsl(h):
    return slice(h * GDN_DH, (h + 1) * GDN_DH)


def _pad_lanes(a, width=GDN_DH):
    return jnp.concatenate([a, jnp.zeros((a.shape[0], width - a.shape[1]), a.dtype)], axis=1)


def _gdn_prep(qkv, proj, alog, dtb, name):
    S = qkv.shape[0]
    N = S // GDN_C

    def body(qkv_ref, ab_ref, al_ref, dt_ref, w_ref, u_ref, qd_ref, kd_ref, at_ref, gc_ref):
        ab = ab_ref[...]
        glog = _each(_gdn_glog, _cols(ab, 0), _cols(al_ref[...], 0), _cols(dt_ref[...], 0))
        dec = _each(_gdn_decay, glog)
        gcol, dmat = _each(lambda d: d[0], dec), _each(lambda d: d[1], dec)
        w, u, qd, kd, attn = _gdn_chunk(_heads(qkv_ref), _heads(qkv_ref, GDN_H), _heads(qkv_ref, 2 * GDN_H),
                                        _cols(ab, GDN_H), gcol, dmat)
        gc = jnp.zeros((GDN_C, 128), F32)
        for h in range(GDN_H):
            w_ref[:, _hsl(h)] = w[h]
            u_ref[:, _hsl(h)] = u[h]
            qd_ref[:, _hsl(h)] = qd[h]
            kd_ref[:, _hsl(h)] = kd[h]
            at_ref[:, _hsl(h)] = _pad_lanes(attn[h])
            gc = jnp.where(_lanes(gc.shape) == h, gcol[h], gc)
        gc_ref[...] = gc

    vec = pl.BlockSpec((1, 128), lambda i: (0, 0))
    hsp = pl.BlockSpec((GDN_C, GDN_W), lambda i: (i, 0))
    hshape = jax.ShapeDtypeStruct((S, GDN_W), F32)
    return pl.pallas_call(
        body, name=name, grid=(N,),
        in_specs=[pl.BlockSpec((GDN_C, 3 * GDN_W), lambda i: (i, 0)), pl.BlockSpec((GDN_C, 128), lambda i: (i, PAB // 128)), vec, vec],
        out_specs=[hsp] * 5 + [pl.BlockSpec((GDN_C, 128), lambda i: (i, 0))],
        out_shape=[hshape] * 5 + [jax.ShapeDtypeStruct((S, 128), F32)],
        compiler_params=_cp("parallel"),
    )(qkv, proj, alog, dtb)


def _gdn_scan(w, u, qd, kd, attn, gc, name):
    S = w.shape[0]
    N = S // GDN_C

    def body(w_ref, u_ref, qd_ref, kd_ref, at_ref, gc_ref, o_ref, st_ref, s_ref):
        @pl.when(pl.program_id(0) == 0)
        def _():
            s_ref[...] = jnp.zeros_like(s_ref)

        state = tuple(s_ref[_hsl(h), :] for h in range(GDN_H))
        egl = _each(jnp.exp, _cols(gc_ref[GDN_C - 1:GDN_C, :], 0))
        o, new = _gdn_step(state, _heads(w_ref), _heads(u_ref), _heads(qd_ref), _heads(kd_ref),
                           _heads(at_ref, width=GDN_C), egl)
        for h in range(GDN_H):
            st_ref[_hsl(h), :] = state[h]
            o_ref[:, _hsl(h)] = o[h]
            s_ref[_hsl(h), :] = new[h]

    hsp = pl.BlockSpec((GDN_C, GDN_W), lambda i: (i, 0))
    return pl.pallas_call(
        body, name=name, grid=(N,),
        in_specs=[hsp] * 5 + [pl.BlockSpec((GDN_C, 128), lambda i: (i, 0))],
        out_specs=[hsp, pl.BlockSpec((None, GDN_W, GDN_DH), lambda i: (i, 0, 0))],
        out_shape=[jax.ShapeDtypeStruct((S, GDN_W), F32), jax.ShapeDtypeStruct((N, GDN_W, GDN_DH), F32)],
        scratch_shapes=[pltpu.VMEM((GDN_W, GDN_DH), F32)],
        compiler_params=_cp("arbitrary"),
    )(w, u, qd, kd, attn, gc)


def _gdn_scan_bwd(w, u, qd, kd, attn, gc, states, o, proj, dmixed, nw, name):
    S = w.shape[0]
    N = S // GDN_C

    def body(w_ref, u_ref, qd_ref, kd_ref, at_ref, gc_ref, st_ref, o_ref, z_ref, dm_ref, nw_ref,
             dw_ref, du_ref, dqd_ref, dkd_ref, dat_ref, dgl_ref, dz_ref, gnw_ref, ds_ref):
        @pl.when(pl.program_id(0) == 0)
        def _():
            ds_ref[...] = jnp.zeros_like(ds_ref)
            gnw_ref[...] = jnp.zeros_like(gnw_ref)

        nw = nw_ref[...]
        _, vjp_n = jax.vjp(lambda o, z, w_: _each(lambda a, b: _gated_norm(a, b, w_), o, z), _heads(o_ref), _heads(z_ref), nw)
        do, dz, dnw = vjp_n(_heads(dm_ref))
        state = tuple(st_ref[_hsl(h), :] for h in range(GDN_H))
        egl = _each(jnp.exp, _cols(gc_ref[GDN_C - 1:GDN_C, :], 0))
        _, vjp_s = jax.vjp(_gdn_step, state, _heads(w_ref), _heads(u_ref), _heads(qd_ref), _heads(kd_ref),
                           _heads(at_ref, width=GDN_C), egl)
        ds, dw, du, dqd, dkd, dat, degl = vjp_s((do, tuple(ds_ref[_hsl(h), :] for h in range(GDN_H))))
        dgl = jnp.zeros((8, 128), F32)
        for h in range(GDN_H):
            dz_ref[:, _hsl(h)] = dz[h].astype(dz_ref.dtype)
            ds_ref[_hsl(h), :] = ds[h]
            dw_ref[:, _hsl(h)] = dw[h]
            du_ref[:, _hsl(h)] = du[h]
            dqd_ref[:, _hsl(h)] = dqd[h]
            dkd_ref[:, _hsl(h)] = dkd[h]
            dat_ref[:, _hsl(h)] = _pad_lanes(dat[h])
            dgl = jnp.where(_lanes(dgl.shape) == h, degl[h] * egl[h], dgl)
        dgl_ref[...] = dgl
        gnw_ref[...] += dnw

    rev = lambda i: (N - 1 - i, 0)
    hsp = pl.BlockSpec((GDN_C, GDN_W), rev)
    gsp = pl.BlockSpec((GDN_C, 128), rev)
    vec = pl.BlockSpec((1, GDN_DH), lambda i: (0, 0))
    hshape = jax.ShapeDtypeStruct((S, GDN_W), F32)
    return pl.pallas_call(
        body, name=name, grid=(N,),
        in_specs=[hsp] * 5 + [gsp, pl.BlockSpec((None, GDN_W, GDN_DH), lambda i: (N - 1 - i, 0, 0)), hsp,
                              pl.BlockSpec((GDN_C, GDN_W), lambda i: (N - 1 - i, PZ // GDN_W)), hsp, vec],
        out_specs=[hsp] * 5 + [pl.BlockSpec((8, 128), rev), hsp, vec],
        out_shape=[hshape] * 5 + [jax.ShapeDtypeStruct((N * 8, 128), F32), jax.ShapeDtypeStruct((S, GDN_W), BF16),
                                  jax.ShapeDtypeStruct((1, GDN_DH), F32)],
        scratch_shapes=[pltpu.VMEM((GDN_W, GDN_DH), F32)],
        compiler_params=_cp("arbitrary"),
    )(w, u, qd, kd, attn, gc, states, o, proj, dmixed, nw)


def _gdn_prep_bwd(qkv, proj, alog, dtb, dw, du, dqd, dkd, dat, dgl, name):
    S = qkv.shape[0]
    N = S // GDN_C

    def body(qkv_ref, ab_ref, al_ref, dt_ref, dw_ref, du_ref, dqd_ref, dkd_ref, dat_ref, dgl_ref,
             dqkv_ref, dab_ref, gal_ref, gdt_ref):
        @pl.when(pl.program_id(0) == 0)
        def _():
            gal_ref[...] = jnp.zeros_like(gal_ref)
            gdt_ref[...] = jnp.zeros_like(gdt_ref)

        ab = ab_ref[...]
        glog, vjp_g = jax.vjp(lambda a, al, dt: _each(_gdn_glog, a, al, dt), _cols(ab, 0), _cols(al_ref[...], 0),
                              _cols(dt_ref[...], 0))
        dec = _each(_gdn_decay, glog)
        gcol, dmat = _each(lambda d: d[0], dec), _each(lambda d: d[1], dec)
        _, vjp_c = jax.vjp(_gdn_chunk, _heads(qkv_ref), _heads(qkv_ref, GDN_H), _heads(qkv_ref, 2 * GDN_H),
                           _cols(ab, GDN_H), gcol, dmat)
        dqa, dka, dva, dbt, dgcol, dd = vjp_c((_heads(dw_ref), _heads(du_ref), _heads(dqd_ref), _heads(dkd_ref),
                                               _heads(dat_ref, width=GDN_C)))
        last = _rows(dgcol[0].shape) == GDN_C - 1
        dgcol = _each(lambda d, g: d + jnp.where(last, g, 0.0), dgcol, _cols(dgl_ref[0:1, :], 0))
        da_col, dal, ddt = vjp_g(_each(_gdn_decay_bwd, dgcol, dd))
        dab = jnp.zeros((GDN_C, 128), F32)
        gal = jnp.zeros((1, 128), F32)
        gdt = jnp.zeros((1, 128), F32)
        for h in range(GDN_H):
            dqkv_ref[:, _hsl(h)] = dqa[h]
            dqkv_ref[:, _hsl(GDN_H + h)] = dka[h]
            dqkv_ref[:, _hsl(2 * GDN_H + h)] = dva[h]
            ln = _lanes(dab.shape)
            dab = dab + jnp.where(ln == h, da_col[h], 0.0) + jnp.where(ln == GDN_H + h, dbt[h], 0.0)
            l1 = _lanes(gal.shape)
            gal = gal + jnp.where(l1 == h, dal[h], 0.0)
            gdt = gdt + jnp.where(l1 == h, ddt[h], 0.0)
        dab_ref[...] = dab.astype(dab_ref.dtype)
        gal_ref[...] += gal
        gdt_ref[...] += gdt

    vec = pl.BlockSpec((1, 128), lambda i: (0, 0))
    hsp = pl.BlockSpec((GDN_C, GDN_W), lambda i: (i, 0))
    qsp = pl.BlockSpec((GDN_C, 3 * GDN_W), lambda i: (i, 0))
    return pl.pallas_call(
        body, name=name, grid=(N,),
        in_specs=[qsp, pl.BlockSpec((GDN_C, 128), lambda i: (i, PAB // 128)), vec, vec] + [hsp] * 5
        + [pl.BlockSpec((8, 128), lambda i: (i, 0))],
        out_specs=[qsp, pl.BlockSpec((GDN_C, 128), lambda i: (i, 0)), vec, vec],
        out_shape=[jax.ShapeDtypeStruct((S, 3 * GDN_W), F32), jax.ShapeDtypeStruct((S, 128), BF16),
                   jax.ShapeDtypeStruct((1, 128), F32), jax.ShapeDtypeStruct((1, 128), F32)],
        compiler_params=_cp("arbitrary"),
    )(qkv, proj, alog, dtb, dw, du, dqd, dkd, dat, dgl)


@jax.custom_vjp
def _expm1(x):
    u = jnp.exp(x)
    lu = jnp.log(u)
    small = (u - 1.0) * x / jnp.where(u == 1.0, 1.0, lu)
    small = jnp.where(u == 1.0, x, small)
    return jnp.where(jnp.abs(x) < 0.5, small, u - 1.0)


def _expm1_fwd(x):
    return _expm1(x), jnp.exp(x)


def _expm1_bwd(ex, g):
    return (g * ex,)


_expm1.defvjp(_expm1_fwd, _expm1_bwd)


def _lru_gates(xc, wa, ba, wx, bx, lam, first):
    r = jax.nn.sigmoid(_mm(xc, wa) + ba)
    i = jax.nn.sigmoid(_mm(xc, wx) + bx)
    log_a = -LRU_C * r * jax.nn.softplus(-lam)
    mult = jnp.sqrt(-_expm1(2.0 * log_a))
    mult = jnp.where(first, 1.0, mult)
    return jnp.exp(log_a), mult * i * xc


def _scan_fwd(a, b):
    T = a.shape[0]
    rows = _rows(a.shape)
    s = 1
    while s < T:
        ok = rows >= s
        b = a * jnp.where(ok, _down(b, s), 0.0) + b
        a = a * jnp.where(ok, _down(a, s), 1.0)
        s *= 2
    return a, b


def _scan_rev(a, b):
    T = a.shape[0]
    rows = _rows(a.shape)
    s = 1
    while s < T:
        ok = rows + s < T
        b = a * jnp.where(ok, _up(b, s), 0.0) + b
        a = a * jnp.where(ok, _up(a, s), 1.0)
        s *= 2
    return b


def _bsl(j):
    return slice(j * LRU_BD, (j + 1) * LRU_BD)


def _lru_tile(S):
    return _row_tile(S, 256)


def _lru_fwd(proj, conv_w, conv_b, wa, ba, wx, bx, lam, name):
    S = proj.shape[0]
    T = _lru_tile(S)
    taps = conv_w.shape[0]
    r = T // HALO

    def body(x_ref, xp_ref, cw_ref, cb_ref, wa_ref, ba_ref, wx_ref, bx_ref, lam_ref, h_ref, carry_ref):
        i = pl.program_id(0)

        @pl.when(i == 0)
        def _():
            carry_ref[...] = jnp.zeros_like(carry_ref)

        ext = jnp.concatenate([jnp.where(i > 0, xp_ref[...], 0.0), x_ref[...]], axis=0)
        xc = _conv_rows(ext, cw_ref, taps)[HALO:] + cb_ref[...]
        first = (i * T + _rows((T, LRU_BD))) == 0
        for j in range(LRU_NB):
            a, b = _lru_gates(xc[:, _bsl(j)], wa_ref[j], ba_ref[:, _bsl(j)], wx_ref[j], bx_ref[:, _bsl(j)],
                              lam_ref[:, _bsl(j)], first=first)
            pa, hb = _scan_fwd(a, b)
            h_ref[:, _bsl(j)] = pa * carry_ref[0:1, _bsl(j)] + hb
            carry_ref[0:1, _bsl(j)] = h_ref[T - 1:T, _bsl(j)]

    vec = pl.BlockSpec((1, LRU_W), lambda i: (0, 0))
    wsp = pl.BlockSpec((LRU_NB, LRU_BD, LRU_BD), lambda i: (0, 0, 0))
    return pl.pallas_call(
        body, name=name, grid=(S // T,),
        in_specs=[pl.BlockSpec((T, LRU_W), lambda i: (i, PXR // LRU_W)),
                  pl.BlockSpec((HALO, LRU_W), lambda i: (jnp.maximum(i * r - 1, 0), PXR // LRU_W)),
                  pl.BlockSpec((taps, LRU_W), lambda i: (0, 0)), vec, wsp, vec, wsp, vec, vec],
        out_specs=pl.BlockSpec((T, LRU_W), lambda i: (i, 0)), out_shape=jax.ShapeDtypeStruct((S, LRU_W), F32),
        scratch_shapes=[pltpu.VMEM((8, LRU_W), F32)],
        compiler_params=_cp("arbitrary"),
    )(proj, proj, conv_w, conv_b.reshape(1, LRU_W), wa, ba.reshape(1, LRU_W), wx, bx.reshape(1, LRU_W), lam.reshape(1, LRU_W))


def _lru_bwd(proj, hl, dmixed, conv_w, conv_b, wa, ba, wx, bx, lam, name):
    S = proj.shape[0]
    T = _lru_tile(S)
    n = S // T
    taps = conv_w.shape[0]
    r = T // HALO
    mb = 768 // LRU_W

    def body(x_ref, xp_ref, g_ref, h_ref, hp_ref, dy_ref, cw_ref, cb_ref, wa_ref, ba_ref, wx_ref, bx_ref, lam_ref,
             dx_ref, dg_ref, gcw_ref, gcb_ref, gwa_ref, gba_ref, gwx_ref, gbx_ref, glam_ref, carry_ref, dxc_ref, nxt_ref):
        s = pl.program_id(0)
        i = n - 1 - s

        @pl.when(s == 0)
        def _():
            carry_ref[...] = jnp.zeros_like(carry_ref)
            nxt_ref[...] = jnp.zeros_like(nxt_ref)
            for ref in (gcw_ref, gcb_ref, gwa_ref, gba_ref, gwx_ref, gbx_ref, glam_ref):
                ref[...] = jnp.zeros_like(ref)

        ext = jnp.concatenate([jnp.where(i > 0, xp_ref[...], 0.0), x_ref[...]], axis=0)
        xc = _conv_rows(ext, cw_ref, taps)[HALO:] + cb_ref[...]
        rows = _rows((T, LRU_BD))
        first = (i * T + rows) == 0
        h_before = jnp.where(i > 0, hp_ref[HALO - 1:HALO, :], 0.0)
        for j in range(LRU_NB):
            sl = _bsl(j)
            (a, _), vjp_g = jax.vjp(functools.partial(_lru_gates, first=first), xc[:, sl], wa_ref[j], ba_ref[:, sl],
                                    wx_ref[j], bx_ref[:, sl], lam_ref[:, sl])
            gelu_g, vjp_a = jax.vjp(jax.nn.gelu, g_ref[:, sl])
            h = h_ref[:, sl]
            dy = dy_ref[:, sl]
            dg_ref[:, sl] = vjp_a(dy * h)[0].astype(dg_ref.dtype)
            b_rev = dy * gelu_g + jnp.where(rows == T - 1, carry_ref[0:1, sl], 0.0)
            a_rev = jnp.where(rows == T - 1, 0.0, _up(a, 1))
            dh = _scan_rev(a_rev, b_rev)
            carry_ref[:, sl] = (a * dh)[:HALO]
            h_prev = jnp.where(rows == 0, h_before[:, sl], _down(h, 1))
            dxc, dwa, dba, dwx, dbx, dlam = vjp_g((dh * h_prev, dh))
            dxc_ref[:, sl] = dxc
            gwa_ref[j] += dwa
            gwx_ref[j] += dwx
            gba_ref[:, sl] += dba
            gbx_ref[:, sl] += dbx
            glam_ref[:, sl] += dlam
        dxc = dxc_ref[...]
        d_ext = jnp.concatenate([dxc, nxt_ref[...]], axis=0)
        dx_ref[...] = _conv_t_rows(d_ext, cw_ref, taps)[:T].astype(dx_ref.dtype)
        nxt_ref[...] = dxc[:HALO]
        gcb_ref[...] += jnp.sum(dxc, axis=0, keepdims=True)
        for k in range(taps):
            gcw_ref[taps - 1 - k:taps - k, :] += jnp.sum(dxc * _down(ext, k)[HALO:], axis=0, keepdims=True)

    vec = pl.BlockSpec((1, LRU_W), lambda s: (0, 0))
    wsp = pl.BlockSpec((LRU_NB, LRU_BD, LRU_BD), lambda s: (0, 0, 0))
    cwsp = pl.BlockSpec((taps, LRU_W), lambda s: (0, 0))

    def main(cb):
        return pl.BlockSpec((T, LRU_W), lambda s: (n - 1 - s, cb))

    def prev(cb):
        return pl.BlockSpec((HALO, LRU_W), lambda s: (jnp.maximum((n - 1 - s) * r - 1, 0), cb))

    vshape = jax.ShapeDtypeStruct((1, LRU_W), F32)
    wshape = jax.ShapeDtypeStruct((LRU_NB, LRU_BD, LRU_BD), F32)
    return pl.pallas_call(
        body, name=name, grid=(n,),
        in_specs=[main(PXR // LRU_W), prev(PXR // LRU_W), main(PGR // LRU_W), main(0), prev(0), main(mb),
                  cwsp, vec, wsp, vec, wsp, vec, vec],
        out_specs=[main(0), main(0), cwsp, vec, wsp, vec, wsp, vec, vec],
        out_shape=[jax.ShapeDtypeStruct((S, LRU_W), BF16), jax.ShapeDtypeStruct((S, LRU_W), BF16),
                   jax.ShapeDtypeStruct((taps, LRU_W), F32), vshape, wshape, vshape, wshape, vshape, vshape],
        scratch_shapes=[pltpu.VMEM((8, LRU_W), F32), pltpu.VMEM((T, LRU_W), F32), pltpu.VMEM((HALO, LRU_W), F32)],
        compiler_params=_cp("arbitrary"),
    )(proj, proj, proj, hl, hl, dmixed, conv_w, conv_b.reshape(1, LRU_W), wa, ba.reshape(1, LRU_W), wx,
      bx.reshape(1, LRU_W), lam.reshape(1, LRU_W))


def _mix_out(o, proj, hl, y_pool, nw, name):
    S = o.shape[0]
    T = _row_tile(S)

    def body(o_ref, z_ref, h_ref, g_ref, p_ref, nw_ref, m_ref):
        for h in range(GDN_H):
            m_ref[:, _hsl(h)] = _gated_norm(o_ref[:, _hsl(h)], z_ref[:, _hsl(h)], nw_ref[...]).astype(m_ref.dtype)
        m_ref[:, GDN_W:GDN_W + LRU_W] = (h_ref[...] * jax.nn.gelu(g_ref[...])).astype(m_ref.dtype)
        m_ref[:, GDN_W + LRU_W:] = p_ref[...].astype(m_ref.dtype)

    row = pl.BlockSpec((T, GDN_W), lambda i: (i, 0))
    return pl.pallas_call(
        body, name=name, grid=(S // T,),
        in_specs=[row, pl.BlockSpec((T, GDN_W), lambda i: (i, PZ // GDN_W)), row,
                  pl.BlockSpec((T, LRU_W), lambda i: (i, PGR // LRU_W)), pl.BlockSpec((T, POOL_W), lambda i: (i, 0)),
                  pl.BlockSpec((1, GDN_DH), lambda i: (0, 0))],
        out_specs=pl.BlockSpec((T, D_MODEL), lambda i: (i, 0)), out_shape=jax.ShapeDtypeStruct((S, D_MODEL), BF16),
        compiler_params=_cp("parallel"),
    )(o, proj, hl, proj, y_pool, nw)


def _as2d(a):
    return a.reshape(-1, a.shape[-1])


def _ew_rows(rows, cols):
    t = rows
    while t * cols * 4 > (2 << 20) and t % 16 == 0:
        t //= 2
    return t


def _rs_rows(rows, cols):
    t = rows
    while t * cols * 4 > (2 << 20) and t % 32 == 0:
        t //= 2
    return t


def _rs_add(src, recv, src_index, grid_lead, out_dtype, name):
    rows, cols = recv.shape[-2:]
    t = _rs_rows(rows, cols)
    nl = len(grid_lead)
    lead_none = (None,) * (src.ndim - 2)

    def body(p_ref, s_ref, r_ref, o_ref):
        o_ref[...] = (s_ref[...].astype(F32) + r_ref[...].astype(F32)).astype(o_ref.dtype)

    def src_map(*a):
        return (*src_index(*a[:nl], a[-1]), a[nl], 0)

    def own_map(*a):
        return (*a[:nl], a[nl], 0)

    x, y, c = _place()
    place = jnp.stack([x, y, c]).astype(jnp.int32)
    return pl.pallas_call(
        body, name=name,
        grid_spec=pltpu.PrefetchScalarGridSpec(
            num_scalar_prefetch=1, grid=(*grid_lead, rows // t),
            in_specs=[pl.BlockSpec((*lead_none, t, cols), src_map), pl.BlockSpec(((None,) * nl) + (t, cols), own_map)],
            out_specs=pl.BlockSpec(((None,) * nl) + (t, cols), own_map)),
        out_shape=jax.ShapeDtypeStruct(recv.shape, out_dtype),
        compiler_params=pltpu.CompilerParams(dimension_semantics=("parallel",) * (nl + 1), vmem_limit_bytes=VMEM_LIMIT),
    )(place, src, recv)


def _add(a, b, name):
    shape = a.shape
    a2, b2 = _as2d(a), _as2d(b)
    rows, cols = a2.shape
    t = _ew_rows(rows, cols)

    def body(a_ref, b_ref, o_ref):
        o_ref[...] = a_ref[...] + b_ref[...]

    sp = pl.BlockSpec((t, cols), lambda i: (i, 0))
    return pl.pallas_call(body, name=name, grid=(rows // t,), in_specs=[sp, sp], out_specs=sp,
                          out_shape=jax.ShapeDtypeStruct((rows, cols), F32), compiler_params=_cp("parallel"))(a2, b2).reshape(shape)


def _adamw(w, g, m, v, name):
    shape = w.shape
    w2, g2, m2, v2 = _as2d(w), _as2d(g), _as2d(m), _as2d(v)
    rows, cols = w2.shape
    t = _ew_rows(rows, cols)

    def body(w_ref, g_ref, m_ref, v_ref, d_ref, nm_ref, nv_ref):
        gr = g_ref[...]
        nm = ADAM_B1 * m_ref[...] + (1.0 - ADAM_B1) * gr
        nv = ADAM_B2 * v_ref[...] + (1.0 - ADAM_B2) * (gr * gr)
        m_hat = nm / (1.0 - ADAM_B1 ** ADAM_STEP)
        v_hat = nv / (1.0 - ADAM_B2 ** ADAM_STEP)
        d_ref[...] = -ADAM_LR * (m_hat / (jnp.sqrt(v_hat) + ADAM_EPS) + ADAM_WD * w_ref[...])
        nm_ref[...] = nm
        nv_ref[...] = nv

    sp = pl.BlockSpec((t, cols), lambda i: (i, 0))
    sh = jax.ShapeDtypeStruct((rows, cols), F32)
    d, nm, nv = pl.pallas_call(body, name=name, grid=(rows // t,), in_specs=[sp] * 4, out_specs=[sp] * 3,
                               out_shape=[sh] * 3, compiler_params=_cp("parallel"))(w2, g2, m2, v2)
    return d.reshape(shape), nm.reshape(shape), nv.reshape(shape)


def _place():
    return lax.axis_index("x"), lax.axis_index("y"), lax.axis_index("c")


def _gather_weights(arrs, name):
    n = len(arrs)

    def body(*refs):
        srcs, outs = refs[:n], refs[n:2 * n]
        send, recv, loc = refs[2 * n:]
        x, y, c = _place()
        s_me, s_x, s_y, s_d = 2 * x + y, 2 * (1 - x) + y, 2 * x + (1 - y), 2 * (1 - x) + (1 - y)
        xpeer, ypeer, sib = (1 - x, y, c), (x, 1 - y, c), (x, y, 1 - c)

        def rc(k, t, src, dst, to):
            return pltpu.make_async_remote_copy(src_ref=src, dst_ref=dst, send_sem=send.at[k, t], recv_sem=recv.at[k, t],
                                                device_id=to, device_id_type=MESH)

        def half(k, s, top):
            rh = outs[k].shape[2] // 2
            return outs[k].at[c, s, pl.ds(0 if top else rh, rh)]

        local = []
        for k in range(n):
            for l in range(2):
                cp = pltpu.make_async_copy(srcs[k].at[l], outs[k].at[l, s_me], loc.at[k, l])
                cp.start()
                local.append(cp)
        sent = []

        def start(k, t, ref, to):
            cp = rc(k, t, ref, ref, to)
            cp.start()
            sent.append(cp)

        for k in range(n):
            for t, to in ((0, xpeer), (1, ypeer)):
                cp = rc(k, t, srcs[k].at[c], outs[k].at[c, s_me], to)
                cp.start()
                sent.append(cp)
        for k in range(n):
            got = outs[k].at[c, s_x]
            rc(k, 0, got, got, xpeer).wait_recv()
            start(k, 2, half(k, s_x, True), ypeer)
            start(k, 3, got, sib)
        for k in range(n):
            got = outs[k].at[c, s_y]
            rc(k, 1, got, got, ypeer).wait_recv()
            start(k, 6, half(k, s_y, False), xpeer)
            start(k, 4, got, sib)
        for k in range(n):
            top, bottom = half(k, s_d, True), half(k, s_d, False)
            rc(k, 2, top, top, ypeer).wait_recv()
            rc(k, 6, bottom, bottom, xpeer).wait_recv()
            start(k, 5, outs[k].at[c, s_d], sib)
        for k in range(n):
            for t, s in ((3, s_x), (4, s_y), (5, s_d)):
                got = outs[k].at[1 - c, s]
                rc(k, t, got, got, sib).wait_recv()
        for cp in sent:
            cp.wait_send()
        for cp in local:
            cp.wait()

    return pl.pallas_call(
        body, name=name, in_specs=[HBM] * n, out_specs=[HBM] * n,
        out_shape=[jax.ShapeDtypeStruct((2, 4) + a.shape[1:], a.dtype) for a in arrs],
        scratch_shapes=[pltpu.SemaphoreType.DMA((n, 7)), pltpu.SemaphoreType.DMA((n, 7)), pltpu.SemaphoreType.DMA((n, 2))],
    )(*arrs)


def _exchange(arrs, axis, name, half=True):
    n = len(arrs)

    def body(*refs):
        srcs, outs = refs[:n], refs[n:2 * n]
        send, recv = refs[2 * n:]
        x, y, c = _place()
        p = {"x": x, "y": y, "c": c}[axis]
        peer = {"x": (1 - x, y, c), "y": (x, 1 - y, c), "c": (x, y, 1 - c)}[axis]
        cps = []
        for k in range(n):
            cp = pltpu.make_async_remote_copy(src_ref=srcs[k].at[1 - p] if half else srcs[k], dst_ref=outs[k],
                                              send_sem=send.at[k], recv_sem=recv.at[k], device_id=peer, device_id_type=MESH)
            cp.start()
            cps.append(cp)
        for cp in cps:
            cp.wait()

    return pl.pallas_call(
        body, name=name, in_specs=[HBM] * n, out_specs=[HBM] * n,
        out_shape=[jax.ShapeDtypeStruct(a.shape[1:] if half else a.shape, a.dtype) for a in arrs],
        scratch_shapes=[pltpu.SemaphoreType.DMA((n,)), pltpu.SemaphoreType.DMA((n,))],
    )(*arrs)


def _share_layers(arrs, name):
    n = len(arrs)

    def body(*refs):
        srcs, outs = refs[:n], refs[n:2 * n]
        send, recv, loc = refs[2 * n:]
        x, y, c = _place()
        cps, lcs = [], []
        for k in range(n):
            lc = pltpu.make_async_copy(srcs[k], outs[k].at[c], loc.at[k])
            lc.start()
            lcs.append(lc)
            cp = pltpu.make_async_remote_copy(src_ref=srcs[k], dst_ref=outs[k].at[c], send_sem=send.at[k], recv_sem=recv.at[k],
                                              device_id=(x, y, 1 - c), device_id_type=MESH)
            cp.start()
            cps.append(cp)
        for k in range(n):
            got = outs[k].at[1 - c]
            pltpu.make_async_remote_copy(src_ref=got, dst_ref=got, send_sem=send.at[k], recv_sem=recv.at[k],
                                         device_id=(x, y, 1 - c), device_id_type=MESH).wait_recv()
        for cp in cps:
            cp.wait_send()
        for lc in lcs:
            lc.wait()

    return pl.pallas_call(
        body, name=name, in_specs=[HBM] * n, out_specs=[HBM] * n,
        out_shape=[jax.ShapeDtypeStruct((2,) + a.shape, a.dtype) for a in arrs],
        scratch_shapes=[pltpu.SemaphoreType.DMA((n,)), pltpu.SemaphoreType.DMA((n,)), pltpu.SemaphoreType.DMA((n,))],
    )(*arrs)


def _rs_exchange(arrs, phase, name):
    n = len(arrs)

    def body(*refs):
        srcs, outs = refs[:n], refs[n:2 * n]
        send, recv = refs[2 * n:]
        x, y, c = _place()
        xpeer, ypeer = (1 - x, y, c), (x, 1 - y, c)
        cps = []
        for k in range(n):
            if phase == 2:
                parts = ((srcs[k].at[1 - x, :, 0], xpeer), (srcs[k].at[:, 1 - y, 1], ypeer))
            else:
                parts = ((srcs[k].at[0, 1 - y], ypeer), (srcs[k].at[1, 1 - x], xpeer))
            for h, (src, to) in enumerate(parts):
                cp = pltpu.make_async_remote_copy(src_ref=src, dst_ref=outs[k].at[h], send_sem=send.at[k, h],
                                                  recv_sem=recv.at[k, h], device_id=to, device_id_type=MESH)
                cp.start()
                cps.append(cp)
        for cp in cps:
            cp.wait()

    def out_shape(a):
        return (2, 2) + a.shape[3:] if phase == 2 else (2,) + a.shape[2:]

    return pl.pallas_call(
        body, name=name, in_specs=[HBM] * n, out_specs=[HBM] * n,
        out_shape=[jax.ShapeDtypeStruct(out_shape(a), a.dtype) for a in arrs],
        scratch_shapes=[pltpu.SemaphoreType.DMA((n, 2)), pltpu.SemaphoreType.DMA((n, 2))],
    )(*arrs)


def _reduce_scatter(grads):
    r1 = _exchange(grads, "c", "rs_c")
    a1 = [_rs_add(g, r, lambda s, p: (p[2], s), (4,), BF16, f"rs_add1_{k}") for k, (g, r) in enumerate(zip(grads, r1))]
    a1 = [a.reshape(2, 2, 2, a.shape[1] // 2, a.shape[2]) for a in a1]
    r2 = _rs_exchange(a1, 2, "rs_p2")
    a2 = [_rs_add(a, r, lambda h, j, p: (p[0] * (1 - h) + j * h, j * (1 - h) + p[1] * h, h), (2, 2), BF16, f"rs_add2_{k}")
          for k, (a, r) in enumerate(zip(a1, r2))]
    r3 = _rs_exchange(a2, 3, "rs_p3")
    a3 = [_rs_add(a, r, lambda h, p: (h, p[1] * (1 - h) + p[0] * h), (2,), F32, f"rs_add3_{k}")
          for k, (a, r) in enumerate(zip(a2, r3))]
    a3 = [a.reshape(2 * a.shape[1], a.shape[2]) for a in a3]
    return _share_layers(a3, "rs_share")


def _all_reduce(buf):
    for axis in ("c", "x", "y"):
        (other,) = _exchange([buf], axis, f"ar_{axis}", half=False)
        buf = _add(buf, other, f"ar_add_{axis}")
    return buf


def _pad128(v):
    return jnp.zeros((1, 128), F32).at[0, :v.shape[0]].set(v)


def _layer_fwd(l, x, p):
    h1 = _rms_fwd(x, p["norm1_w"], f"rms1_{l}")
    proj = _matmul(h1, p["w_in"], "nn", name=f"mm_in_{l}", tn=768)
    y_pool = _pool_fwd(proj, p["pool_w"], p["pool_b"], p["pool_scale"], f"pool_{l}")
    qkv = _gdn_conv_fwd(proj, p["gdn_conv_w"], f"gconv_{l}")
    alog, dtb = _pad128(p["gdn_a_log"]), _pad128(p["gdn_dt_bias"])
    gw, gu, gqd, gkd, gat, gc = _gdn_prep(qkv, proj, alog, dtb, f"gprep_{l}")
    o, states = _gdn_scan(gw, gu, gqd, gkd, gat, gc, f"gscan_{l}")
    hl = _lru_fwd(proj, p["lru_conv_w"], p["lru_conv_b"], p["lru_wa"], p["lru_ba"], p["lru_wx"], p["lru_bx"],
                  p["lru_lambda"], f"lru_{l}")
    mixed = _mix_out(o, proj, hl, y_pool, p["gdn_norm_w"].reshape(1, GDN_DH), f"mix_{l}")
    x2 = _matmul(mixed, p["w_out"], "nn", name=f"mm_out_{l}", res=x)
    h2 = _rms_fwd(x2, p["norm2_w"], f"rms2_{l}")
    up = _matmul(h2, p["ffn_up"], "nn", name=f"mm_up_{l}", b_split=True)
    act = _ffn_act_fwd(up, p["ffn_conv_w"], f"ffn_{l}")
    x3 = _matmul(act, p["ffn_down"], "nn", name=f"mm_down_{l}", res=x2)
    saved = dict(x=x, h1=h1, proj=proj, qkv=qkv, gdn=(gw, gu, gqd, gkd, gat, gc), states=states, o=o, hl=hl, mixed=mixed,
                 x2=x2, h2=h2, up=up, act=act, alog=alog, dtb=dtb)
    return x3, saved


def _layer_bwd(l, dx3, p, s):
    g = {}
    dact = _matmul(dx3, p["ffn_down"], "nt", name=f"mm_ddown_{l}")
    g["ffn_down"] = _matmul(s["act"], dx3, "tn", name=f"mm_gdown_{l}", out_dtype=BF16)
    dup, g["ffn_conv_w"] = _ffn_act_bwd(s["up"], dact, p["ffn_conv_w"], f"ffn_b_{l}")
    dh2 = _matmul(dup, p["ffn_up"], "nt", name=f"mm_dup_{l}", b_split=True, tk=3072)
    g["ffn_up"] = _matmul(s["h2"], dup, "tn", name=f"mm_gup_{l}", b_split=True, o_split=4, tk=4096, out_dtype=BF16)
    dx2, g["norm2_w"] = _rms_bwd(s["x2"], p["norm2_w"], dh2, dx3, f"rms2_b_{l}")
    dmixed = _matmul(dx2, p["w_out"], "nt", name=f"mm_dout_{l}")
    g["w_out"] = _matmul(s["mixed"], dx2, "tn", name=f"mm_gout_{l}", out_dtype=BF16)
    proj = s["proj"]
    du_pool, g["pool_w"], g["pool_b"], g["pool_scale"] = _pool_bwd(proj, dmixed, p["pool_w"], p["pool_b"], p["pool_scale"], f"pool_b_{l}")
    gw, gu, gqd, gkd, gat, gc = s["gdn"]
    dw, du, dqd, dkd, dat, dgl, dz, g["gdn_norm_w"] = _gdn_scan_bwd(
        gw, gu, gqd, gkd, gat, gc, s["states"], s["o"], proj, dmixed, p["gdn_norm_w"].reshape(1, GDN_DH), f"gscan_b_{l}")
    dqkv, dab, gal, gdt = _gdn_prep_bwd(s["qkv"], proj, s["alog"], s["dtb"], dw, du, dqd, dkd, dat, dgl, f"gprep_b_{l}")
    g["gdn_a_log"], g["gdn_dt_bias"] = gal[0, :GDN_H], gdt[0, :GDN_H]
    dpre, g["gdn_conv_w"] = _gdn_conv_bwd(proj, dqkv, p["gdn_conv_w"], f"gconv_b_{l}")
    (dxr, dgr, g["lru_conv_w"], g["lru_conv_b"], g["lru_wa"], g["lru_ba"], g["lru_wx"], g["lru_bx"], g["lru_lambda"]) = _lru_bwd(
        proj, s["hl"], dmixed, p["lru_conv_w"], p["lru_conv_b"], p["lru_wa"], p["lru_ba"], p["lru_wx"], p["lru_bx"],
        p["lru_lambda"], f"lru_b_{l}")
    S = proj.shape[0]
    dproj = jnp.concatenate([dpre, dz, dxr, dgr, du_pool, dab, jnp.zeros((S, PCOLS - PAB - 128), BF16)], axis=1)
    dh1 = _matmul(dproj, p["w_in"], "nt", name=f"mm_din_{l}", tk=1792)
    g["w_in"] = _matmul(s["h1"], dproj, "tn", name=f"mm_gin_{l}", tn=768, tk=4096, out_dtype=BF16)
    dx, g["norm1_w"] = _rms_bwd(s["x"], p["norm1_w"], dh1, dx2, f"rms1_b_{l}")
    return dx, g


_IN_PERM = ((512, 3584), (3596, 5132), (0, 512), (3584, 3596))


def _w_in_to_proj(w):
    parts = [w[:, a:b] for a, b in _IN_PERM]
    return jnp.concatenate(parts + [jnp.zeros((w.shape[0], PCOLS - IN_COLS), w.dtype)], axis=1)


def _proj_to_w_in(g):
    return jnp.concatenate([g[:, PPOOL:PPOOL + 512], g[:, 0:3072], g[:, PAB:PAB + 12], g[:, 3072:PPOOL]], axis=1)


def _rows_to_mixed(w):
    return jnp.concatenate([w[512:], w[:512]], axis=0)


def _mixed_to_rows(g):
    return jnp.concatenate([g[1536:], g[:1536]], axis=0)


SMALL_SHARDED = ("gdn_conv_w", "lru_conv_w", "ffn_conv_w")
BIG = ("w_in", "w_out", "ffn_up", "ffn_down")
SMALL_REPLICATED = ("norm1_w", "pool_w", "pool_b", "pool_scale", "gdn_a_log", "gdn_dt_bias", "gdn_norm_w", "lru_conv_b",
                    "lru_wa", "lru_ba", "lru_wx", "lru_bx", "lru_lambda", "norm2_w")
WEIGHTS = ("norm1_w", "w_in", "pool_w", "pool_b", "pool_scale", "gdn_conv_w", "gdn_a_log", "gdn_dt_bias", "gdn_norm_w",
           "lru_conv_w", "lru_conv_b", "lru_wa", "lru_ba", "lru_wx", "lru_bx", "lru_lambda", "w_out", "norm2_w", "ffn_up",
           "ffn_conv_w", "ffn_down", "final_norm_w")
FLAT_COLS = 1024


def _pack(arrs):
    flat = jnp.concatenate([a.reshape(-1) for a in arrs])
    rows = -(-flat.shape[0] // (8 * FLAT_COLS)) * 8
    return jnp.pad(flat, (0, rows * FLAT_COLS - flat.shape[0])).reshape(rows, FLAT_COLS)


def _unpack(buf, like):
    flat = buf.reshape(-1)
    out, off = [], 0
    for a in like:
        size = 1
        for d in a.shape:
            size *= d
        out.append(flat[off:off + size].reshape(a.shape))
        off += size
    return out


def kernel(x, norm1_w, w_in, pool_w, pool_b, pool_scale, gdn_conv_w, gdn_a_log, gdn_dt_bias, gdn_norm_w, lru_conv_w, lru_conv_b, lru_wa, lru_ba, lru_wx, lru_bx, lru_lambda, w_out, norm2_w, ffn_up, ffn_conv_w, ffn_down, final_norm_w, loss_target, m_norm1_w, m_w_in, m_pool_w, m_pool_b, m_pool_scale, m_gdn_conv_w, m_gdn_a_log, m_gdn_dt_bias, m_gdn_norm_w, m_lru_conv_w, m_lru_conv_b, m_lru_wa, m_lru_ba, m_lru_wx, m_lru_bx, m_lru_lambda, m_w_out, m_norm2_w, m_ffn_up, m_ffn_conv_w, m_ffn_down, m_final_norm_w, v_norm1_w, v_w_in, v_pool_w, v_pool_b, v_pool_scale, v_gdn_conv_w, v_gdn_a_log, v_gdn_dt_bias, v_gdn_norm_w, v_lru_conv_w, v_lru_conv_b, v_lru_wa, v_lru_ba, v_lru_wx, v_lru_bx, v_lru_lambda, v_w_out, v_norm2_w, v_ffn_up, v_ffn_conv_w, v_ffn_down, v_final_norm_w):
    W = dict(norm1_w=norm1_w, w_in=w_in, pool_w=pool_w, pool_b=pool_b, pool_scale=pool_scale, gdn_conv_w=gdn_conv_w,
             gdn_a_log=gdn_a_log, gdn_dt_bias=gdn_dt_bias, gdn_norm_w=gdn_norm_w, lru_conv_w=lru_conv_w, lru_conv_b=lru_conv_b,
             lru_wa=lru_wa, lru_ba=lru_ba, lru_wx=lru_wx, lru_bx=lru_bx, lru_lambda=lru_lambda, w_out=w_out, norm2_w=norm2_w,
             ffn_up=ffn_up, ffn_conv_w=ffn_conv_w, ffn_down=ffn_down, final_norm_w=final_norm_w)
    M = dict(norm1_w=m_norm1_w, w_in=m_w_in, pool_w=m_pool_w, pool_b=m_pool_b, pool_scale=m_pool_scale, gdn_conv_w=m_gdn_conv_w,
             gdn_a_log=m_gdn_a_log, gdn_dt_bias=m_gdn_dt_bias, gdn_norm_w=m_gdn_norm_w, lru_conv_w=m_lru_conv_w,
             lru_conv_b=m_lru_conv_b, lru_wa=m_lru_wa, lru_ba=m_lru_ba, lru_wx=m_lru_wx, lru_bx=m_lru_bx, lru_lambda=m_lru_lambda,
             w_out=m_w_out, norm2_w=m_norm2_w, ffn_up=m_ffn_up, ffn_conv_w=m_ffn_conv_w, ffn_down=m_ffn_down,
             final_norm_w=m_final_norm_w)
    V = dict(norm1_w=v_norm1_w, w_in=v_w_in, pool_w=v_pool_w, pool_b=v_pool_b, pool_scale=v_pool_scale, gdn_conv_w=v_gdn_conv_w,
             gdn_a_log=v_gdn_a_log, gdn_dt_bias=v_gdn_dt_bias, gdn_norm_w=v_gdn_norm_w, lru_conv_w=v_lru_conv_w,
             lru_conv_b=v_lru_conv_b, lru_wa=v_lru_wa, lru_ba=v_lru_ba, lru_wx=v_lru_wx, lru_bx=v_lru_bx, lru_lambda=v_lru_lambda,
             w_out=v_w_out, norm2_w=v_norm2_w, ffn_up=v_ffn_up, ffn_conv_w=v_ffn_conv_w, ffn_down=v_ffn_down,
             final_norm_w=v_final_norm_w)
    S = x.shape[1]
    xs = x.reshape(S, D_MODEL)
    tgt = loss_target.reshape(S, D_MODEL)
    mx, my, mc = _place()
    shard = 2 * mx + my

    small_sh = jnp.concatenate([W[k].reshape(N_LAYERS, -1) for k in SMALL_SHARDED], axis=1)
    n_small = small_sh.shape[1]
    pad = -n_small % 1024
    small_sh = jnp.pad(small_sh, ((0, 0), (0, pad))).reshape(N_LAYERS, -1, 1024)
    gathered = _gather_weights([W[k].astype(BF16) for k in BIG] + [small_sh], "gather_weights")
    g_in, g_out, g_up, g_down, g_small = gathered
    g_small = g_small.reshape(N_LAYERS, 4, -1)[:, :, :n_small]

    layers = []
    for l in range(N_LAYERS):
        p = {k: W[k][l] for k in SMALL_REPLICATED}
        rows = g_in.shape[2]
        p["w_in"] = _w_in_to_proj(jnp.transpose(g_in[l], (1, 0, 2)).reshape(rows, IN_COLS))
        p["w_out"] = _rows_to_mixed(g_out[l].reshape(D_MODEL, D_MODEL))
        p["ffn_up"] = g_up[l]
        p["ffn_down"] = g_down[l].reshape(D_FF, D_MODEL)
        off = 0
        for k in SMALL_SHARDED:
            taps, width = W[k].shape[1], W[k].shape[2]
            piece = g_small[l, :, off:off + taps * width].reshape(4, taps, width)
            p[k] = jnp.transpose(piece, (1, 0, 2)).reshape(taps, 4 * width)
            off += taps * width
        layers.append(p)

    saved = []
    h = xs
    for l in range(N_LAYERS):
        h, s = _layer_fwd(l, h, layers[l])
        saved.append(s)
    loss_part, dh, g_final = _loss_head(h, final_norm_w, tgt, "loss_head")

    grads = [None] * N_LAYERS
    for l in reversed(range(N_LAYERS)):
        dh, grads[l] = _layer_bwd(l, dh, layers[l], saved[l])
    grad_x = dh.reshape(x.shape)

    def big_partial(k):
        per = []
        for l in range(N_LAYERS):
            g = grads[l][k]
            if k == "w_in":
                g = _proj_to_w_in(g)
                g = jnp.transpose(g.reshape(g.shape[0], 4, IN_COLS // 4), (1, 0, 2))
            elif k == "w_out":
                g = _mixed_to_rows(g).reshape(4, D_MODEL // 4, D_MODEL)
            elif k == "ffn_down":
                g = g.reshape(4, D_FF // 4, D_MODEL)
            per.append(g)
        return jnp.stack(per)

    big_g = dict(zip(BIG, _reduce_scatter([big_partial(k) for k in BIG])))

    small_names = SMALL_REPLICATED + SMALL_SHARDED
    small_list = [jnp.stack([grads[l][k].reshape(W[k].shape[1:]) if k in SMALL_REPLICATED else grads[l][k] for l in range(N_LAYERS)])
                  for k in small_names]
    small_list += [g_final.reshape(D_MODEL), loss_part[0, 0:1]]
    reduced = _unpack(_all_reduce(_pack(small_list)), small_list)
    small_g = dict(zip(small_names, reduced[:len(small_names)]))
    small_g["final_norm_w"] = reduced[-2]
    loss = reduced[-1][0]
    for k in SMALL_SHARDED:
        width = W[k].shape[2]
        small_g[k] = lax.dynamic_slice_in_dim(small_g[k], shard * width, width, axis=2)

    G, DELTA, NM, NV = {}, {}, {}, {}
    for k in BIG:
        G[k] = big_g[k]
        DELTA[k], NM[k], NV[k] = _adamw(W[k], G[k], M[k], V[k], f"adam_{k}")
    small_all = small_names + ("final_norm_w",)
    dl, nm, nv = _adamw(_pack([W[k] for k in small_all]), _pack([small_g[k] for k in small_all]),
                        _pack([M[k] for k in small_all]), _pack([V[k] for k in small_all]), "adam_small")
    like = [W[k] for k in small_all]
    for k, d_, m_, v_ in zip(small_all, _unpack(dl, like), _unpack(nm, like), _unpack(nv, like)):
        G[k], DELTA[k], NM[k], NV[k] = small_g[k], d_, m_, v_

    return (loss, grad_x, *[G[k] for k in WEIGHTS], *[DELTA[k] for k in WEIGHTS], *[NM[k] for k in WEIGHTS],
            *[NV[k] for k in WEIGHTS])
```

```python
import functools

import jax
import jax.numpy as jnp
from jax import lax
from jax.experimental import pallas as pl
from jax.experimental.pallas import tpu as pltpu

F32 = jnp.float32
BF16 = jnp.bfloat16
_MXU = jnp.bfloat16

D_MODEL = 2048
N_LAYERS = 2
POOL_W = 512
POOL_G = 4
POOL_GD = 128
POOL_WINDOWS = (2, 4, 8, 16)
POOL_HALO = 16
GDN_W = 768
GDN_H = 6
GDN_DH = 128
GDN_C = 64
LRU_W = 768
LRU_NB = 6
LRU_BD = 128
LRU_C = 8.0
D_FF = 6144
EPS = 1e-6
IN_COLS = 5132
HALO = 8

PQ, PK, PV, PZ, PXR, PGR, PPOOL, PAB, PCOLS = 0, 768, 1536, 2304, 3072, 3840, 4608, 5120, 5376
CB = 768

ADAM_LR = 0.001
ADAM_B1 = 0.9
ADAM_B2 = 0.999
ADAM_EPS = 1e-08
ADAM_WD = 0.01
ADAM_STEP = 10

VMEM_LIMIT = 56 * 1024 * 1024
MESH = pl.DeviceIdType.MESH
HBM = pl.BlockSpec(memory_space=pltpu.HBM)


def _cp(*sem):
    return pltpu.CompilerParams(dimension_semantics=sem, vmem_limit_bytes=VMEM_LIMIT)


def _dg(a, b, ta, tb):
    dims = (((0 if ta else 1,), (1 if tb else 0,)), ((), ()))
    return lax.dot_general(a, b, dims, preferred_element_type=F32)


def _split2(a):
    hi = a.astype(BF16)
    lo = (a - hi.astype(F32)).astype(BF16)
    return hi, lo


def _mm_raw(a, b, ta, tb, hi):
    if _MXU == F32:
        return _dg(a, b, ta, tb)
    if not hi:
        return _dg(a.astype(_MXU), b.astype(_MXU), ta, tb)
    a1, a2 = _split2(a)
    b1, b2 = _split2(b)
    return _dg(a1, b1, ta, tb) + (_dg(a1, b2, ta, tb) + _dg(a2, b1, ta, tb))


@functools.partial(jax.custom_vjp, nondiff_argnums=(2, 3, 4))
def _mm(a, b, ta=False, tb=False, hi=False):
    return _mm_raw(a, b, ta, tb, hi)


def _mm_fwd(a, b, ta, tb, hi):
    return _mm_raw(a, b, ta, tb, hi), (a, b)


def _mm_bwd(ta, tb, hi, res, dc):
    a, b = res
    da = _mm(b, dc, tb, True, hi) if ta else _mm(dc, b, False, not tb, hi)
    db = _mm(dc, a, True, ta, hi) if tb else _mm(a, dc, not ta, False, hi)
    return da, db


_mm.defvjp(_mm_fwd, _mm_bwd)


def _mm01(m01, x):
    if _MXU == F32:
        return _dg(m01, x, False, False)
    m = m01.astype(BF16)
    x1 = x.astype(BF16)
    r = x - x1.astype(F32)
    x2 = r.astype(BF16)
    x3 = (r - x2.astype(F32)).astype(BF16)
    return _dg(m, x1, False, False) + (_dg(m, x2, False, False) + _dg(m, x3, False, False))


def _down(x, k):
    return x if k == 0 else pltpu.roll(x, k, 0)


def _up(x, k):
    return x if k == 0 else pltpu.roll(x, x.shape[0] - k, 0)


def _rows(shape):
    return lax.broadcasted_iota(jnp.int32, shape, 0)


def _lanes(shape):
    return lax.broadcasted_iota(jnp.int32, shape, 1)


def _matmul(a, b, mode, *, name, res=None, tm=1024, tn=1024, tk=2048, b_split=False, o_split=0, out_dtype=F32):
    ta, tb = mode == "tn", mode == "nt"
    a_split = a.ndim == 3
    if a_split:
        assert not ta
        M, K = a.shape[1], a.shape[0] * a.shape[2]
        tk = min(tk, a.shape[2])
    elif ta:
        K, M = a.shape
    else:
        M, K = a.shape
    if b_split:
        ns = b.shape[0]
        N = b.shape[1] if tb else ns * b.shape[2]
    else:
        N = b.shape[0] if tb else b.shape[1]
    tm, tn, tk = min(tm, M), min(tn, N), min(tk, K)
    if b_split:
        per = b.shape[2]
        if tb:
            tk = min(tk, per)
        else:
            tn = min(tn, per)
    if o_split:
        tn = min(tn, N // o_split)
    assert M % tm == 0 and N % tn == 0 and K % tk == 0, (name, M, N, K, tm, tn, tk)
    nk = K // tk
    if a_split:
        ka = a.shape[2] // tk
        a_spec = pl.BlockSpec((None, tm, tk), lambda i, j, k: (k // ka, i, k % ka))
    else:
        a_spec = pl.BlockSpec((tk, tm), lambda i, j, k: (k, i)) if ta else pl.BlockSpec((tm, tk), lambda i, j, k: (i, k))
    if not b_split:
        b_spec = pl.BlockSpec((tn, tk), lambda i, j, k: (j, k)) if tb else pl.BlockSpec((tk, tn), lambda i, j, k: (k, j))
    elif tb:
        kb = per // tk
        b_spec = pl.BlockSpec((None, tn, tk), lambda i, j, k: (k // kb, j, k % kb))
    else:
        nb = per // tn
        b_spec = pl.BlockSpec((None, tk, tn), lambda i, j, k: (j // nb, k, j % nb))
    if o_split:
        ob = (N // o_split) // tn
        out_shape = jax.ShapeDtypeStruct((o_split, M, N // o_split), out_dtype)
        o_spec = pl.BlockSpec((None, tm, tn), lambda i, j, k: (j // ob, i, j % ob))
    else:
        out_shape = jax.ShapeDtypeStruct((M, N), out_dtype)
        o_spec = pl.BlockSpec((tm, tn), lambda i, j, k: (i, j))
    in_specs = [a_spec, b_spec]
    args = [a, b]
    if res is not None:
        in_specs.append(pl.BlockSpec((tm, tn), lambda i, j, k: (i, j)))
        args.append(res)
    use_acc = nk > 1 and out_dtype != F32

    def body(*refs):
        a_ref, b_ref = refs[0], refs[1]
        o_ref = refs[2 + (res is not None)]
        acc_ref = refs[-1] if use_acc else o_ref
        p = _dg(a_ref[...].astype(_MXU), b_ref[...].astype(_MXU), ta, tb)
        first = p + refs[2][...] if res is not None else p
        if nk == 1:
            o_ref[...] = first.astype(o_ref.dtype)
        else:
            k = pl.program_id(2)

            @pl.when(k == 0)
            def _():
                acc_ref[...] = first

            @pl.when(k > 0)
            def _():
                acc_ref[...] += p

            if use_acc:
                @pl.when(k == nk - 1)
                def _():
                    o_ref[...] = acc_ref[...].astype(o_ref.dtype)

    return pl.pallas_call(
        body, name=name, grid=(M // tm, N // tn, nk), in_specs=in_specs, out_specs=o_spec, out_shape=out_shape,
        scratch_shapes=[pltpu.VMEM((tm, tn), F32)] if use_acc else [],
        compiler_params=_cp("parallel", "parallel", "arbitrary"),
    )(*args)


def _rms(x, w):
    return x * lax.rsqrt(jnp.mean(x * x, axis=-1, keepdims=True) + EPS) * w


def _row_tile(S, t=512):
    t = min(t, S)
    assert S % t == 0
    return t


def _rms_fwd(x, w, name):
    S, D = x.shape
    T = _row_tile(S)

    def body(x_ref, w_ref, o_ref):
        o_ref[...] = _rms(x_ref[...], w_ref[...]).astype(o_ref.dtype)

    return pl.pallas_call(
        body, name=name, grid=(S // T,),
        in_specs=[pl.BlockSpec((T, D), lambda i: (i, 0)), pl.BlockSpec((1, D), lambda i: (0, 0))],
        out_specs=pl.BlockSpec((T, D), lambda i: (i, 0)), out_shape=jax.ShapeDtypeStruct((S, D), BF16),
        compiler_params=_cp("parallel"),
    )(x, w.reshape(1, D))


def _rms_bwd(x, w, dh, dres, name):
    S, D = x.shape
    T = _row_tile(S)

    def body(x_ref, w_ref, dh_ref, dr_ref, dx_ref, gw_ref):
        _, vjp = jax.vjp(_rms, x_ref[...], w_ref[...])
        dx, dw = vjp(dh_ref[...])
        dx_ref[...] = dr_ref[...] + dx

        @pl.when(pl.program_id(0) == 0)
        def _():
            gw_ref[...] = jnp.zeros_like(gw_ref)

        gw_ref[...] += dw

    row = pl.BlockSpec((T, D), lambda i: (i, 0))
    vec = pl.BlockSpec((1, D), lambda i: (0, 0))
    return pl.pallas_call(
        body, name=name, grid=(S // T,), in_specs=[row, vec, row, row], out_specs=[row, vec],
        out_shape=[jax.ShapeDtypeStruct((S, D), F32), jax.ShapeDtypeStruct((1, D), F32)],
        compiler_params=_cp("arbitrary"),
    )(x, w.reshape(1, D), dh, dres)


def _loss_head(x, w, tgt, name):
    S, D = x.shape
    T = _row_tile(S)

    def body(x_ref, w_ref, t_ref, l_ref, dx_ref, gw_ref):
        y, vjp = jax.vjp(_rms, x_ref[...], w_ref[...])
        err = y - t_ref[...]
        part = 0.5 * jnp.sum(jnp.mean(err * err, axis=-1, keepdims=True), axis=0, keepdims=True)
        dx, dw = vjp(err * (1.0 / D))
        dx_ref[...] = dx

        @pl.when(pl.program_id(0) == 0)
        def _():
            gw_ref[...] = jnp.zeros_like(gw_ref)
            l_ref[...] = jnp.zeros_like(l_ref)

        gw_ref[...] += dw
        l_ref[...] += jnp.broadcast_to(part, l_ref.shape)

    row = pl.BlockSpec((T, D), lambda i: (i, 0))
    vec = pl.BlockSpec((1, D), lambda i: (0, 0))
    return pl.pallas_call(
        body, name=name, grid=(S // T,), in_specs=[row, vec, row],
        out_specs=[pl.BlockSpec((8, 128), lambda i: (0, 0)), row, vec],
        out_shape=[jax.ShapeDtypeStruct((8, 128), F32), jax.ShapeDtypeStruct((S, D), F32), jax.ShapeDtypeStruct((1, D), F32)],
        compiler_params=_cp("arbitrary"),
    )(x, w.reshape(1, D), tgt)


def _by_group(shape, vals):
    g = _lanes(shape) // POOL_GD
    out = vals[-1]
    for k in range(len(vals) - 2, -1, -1):
        out = jnp.where(g == k, vals[k], out)
    return out


def _pool_d(prev, u, t0):
    ext = jnp.concatenate([prev, u], axis=0)
    s2 = ext + _down(ext, 1)
    s4 = s2 + _down(s2, 2)
    s8 = s4 + _down(s4, 4)
    s16 = s8 + _down(s8, 8)
    ssel = _by_group(ext.shape, [s2, s4, s8, s16])[POOL_HALO:]
    win = _by_group(u.shape, [jnp.int32(w) for w in POOL_WINDOWS])
    cnt = jnp.minimum(t0 + _rows(u.shape) + 1, win).astype(F32)
    return ssel / cnt - u


def _pool_lin(d, w_ref, b):
    ys = [_mm(d[:, g * POOL_GD:(g + 1) * POOL_GD], w_ref[g]) for g in range(POOL_G)]
    return jnp.concatenate(ys, axis=1) + b


def _pool_fwd(proj, w, b, scale, name):
    S = proj.shape[0]
    T = _row_tile(S)
    r = T // POOL_HALO
    cb = PPOOL // POOL_W

    def body(u_ref, up_ref, w_ref, b_ref, sc_ref, y_ref):
        i = pl.program_id(0)
        prev = jnp.where(i > 0, up_ref[...], 0.0)
        d = _pool_d(prev, u_ref[...], i * T)
        y_ref[...] = _pool_lin(d, w_ref, b_ref[...]) * sc_ref[...]

    vec = pl.BlockSpec((1, POOL_W), lambda i: (0, 0))
    return pl.pallas_call(
        body, name=name, grid=(S // T,),
        in_specs=[pl.BlockSpec((T, POOL_W), lambda i: (i, cb)),
                  pl.BlockSpec((POOL_HALO, POOL_W), lambda i: (jnp.maximum(i * r - 1, 0), cb)),
                  pl.BlockSpec((POOL_G, POOL_GD, POOL_GD), lambda i: (0, 0, 0)), vec, vec],
        out_specs=pl.BlockSpec((T, POOL_W), lambda i: (i, 0)), out_shape=jax.ShapeDtypeStruct((S, POOL_W), F32),
        compiler_params=_cp("parallel"),
    )(proj, proj, w, b.reshape(1, POOL_W), scale.reshape(1, POOL_W))


def _pool_bwd(proj, dmixed, w, b, scale, name):
    S = proj.shape[0]
    T = _row_tile(S)
    n = S // T
    r = T // POOL_HALO
    cb = PPOOL // POOL_W
    mb = 1536 // POOL_W

    def body(u_ref, up_ref, dy_ref, dyn_ref, w_ref, b_ref, sc_ref, du_ref, gw_ref, gb_ref, gs_ref):
        i = pl.program_id(0)
        sc = sc_ref[...]
        dy = dy_ref[...]
        dy_ext = jnp.concatenate([dy, jnp.where(i < n - 1, dyn_ref[...], 0.0)], axis=0)
        dyl = dy_ext * sc
        dd = jnp.concatenate(
            [_mm(dyl[:, g * POOL_GD:(g + 1) * POOL_GD], w_ref[g], False, True) for g in range(POOL_G)], axis=1)
        t_ext = i * T + _rows(dd.shape)
        win = _by_group(dd.shape, [jnp.int32(v) for v in POOL_WINDOWS])
        cnt = jnp.minimum(t_ext + 1, win).astype(F32)
        e = jnp.where(t_ext < S, dd / cnt, 0.0)
        f2 = e + _up(e, 1)
        f4 = f2 + _up(f2, 2)
        f8 = f4 + _up(f4, 4)
        f16 = f8 + _up(f8, 8)
        du = (_by_group(dd.shape, [f2, f4, f8, f16]) - dd)[:T]
        du_ref[...] = du.astype(du_ref.dtype)

        prev = jnp.where(i > 0, up_ref[...], 0.0)
        d = _pool_d(prev, u_ref[...], i * T)
        ylin = _pool_lin(d, w_ref, b_ref[...])
        dyl_m = dy * sc

        @pl.when(i == 0)
        def _():
            gw_ref[...] = jnp.zeros_like(gw_ref)
            gb_ref[...] = jnp.zeros_like(gb_ref)
            gs_ref[...] = jnp.zeros_like(gs_ref)

        gs_ref[...] += jnp.sum(dy * ylin, axis=0, keepdims=True)
        gb_ref[...] += jnp.sum(dyl_m, axis=0, keepdims=True)
        for g in range(POOL_G):
            sl = slice(g * POOL_GD, (g + 1) * POOL_GD)
            gw_ref[g] += _mm(d[:, sl], dyl_m[:, sl], True, False)

    vec = pl.BlockSpec((1, POOL_W), lambda i: (0, 0))
    wsp = pl.BlockSpec((POOL_G, POOL_GD, POOL_GD), lambda i: (0, 0, 0))
    nh = S // POOL_HALO
    return pl.pallas_call(
        body, name=name, grid=(n,),
        in_specs=[pl.BlockSpec((T, POOL_W), lambda i: (i, cb)),
                  pl.BlockSpec((POOL_HALO, POOL_W), lambda i: (jnp.maximum(i * r - 1, 0), cb)),
                  pl.BlockSpec((T, POOL_W), lambda i: (i, mb)),
                  pl.BlockSpec((POOL_HALO, POOL_W), lambda i: (jnp.minimum((i + 1) * r, nh - 1), mb)),
                  wsp, vec, vec],
        out_specs=[pl.BlockSpec((T, POOL_W), lambda i: (i, 0)), wsp, vec, vec],
        out_shape=[jax.ShapeDtypeStruct((S, POOL_W), BF16), jax.ShapeDtypeStruct((POOL_G, POOL_GD, POOL_GD), F32),
                   jax.ShapeDtypeStruct((1, POOL_W), F32), jax.ShapeDtypeStruct((1, POOL_W), F32)],
        compiler_params=_cp("arbitrary"),
    )(proj, proj, dmixed, dmixed, w, b.reshape(1, POOL_W), scale.reshape(1, POOL_W))


def _conv_rows(ext, w_ref, taps):
    acc = w_ref[taps - 1:taps, :] * ext
    for k in range(1, taps):
        acc = acc + w_ref[taps - 1 - k:taps - k, :] * _down(ext, k)
    return acc


def _conv_t_rows(dc, w_ref, taps):
    acc = w_ref[taps - 1:taps, :] * dc
    for k in range(1, taps):
        acc = acc + w_ref[taps - 1 - k:taps - k, :] * _up(dc, k)
    return acc


def _conv_specs(T, S, ncb0, with_next):
    r = T // HALO
    nh = S // HALO
    main = pl.BlockSpec((T, CB), lambda j, i: (i, j + ncb0))
    prev = pl.BlockSpec((HALO, CB), lambda j, i: (jnp.maximum(i * r - 1, 0), j + ncb0))
    nxt = pl.BlockSpec((HALO, CB), lambda j, i: (jnp.minimum((i + 1) * r, nh - 1), j + ncb0))
    return (main, prev, nxt) if with_next else (main, prev)


def _gdn_conv_fwd(proj, w, name):
    S = proj.shape[0]
    T = _row_tile(S)
    taps = w.shape[0]
    ncb = 3 * GDN_W // CB

    def body(x_ref, xp_ref, w_ref, o_ref):
        i = pl.program_id(1)
        ext = jnp.concatenate([jnp.where(i > 0, xp_ref[...], 0.0), x_ref[...]], axis=0)
        o_ref[...] = jax.nn.silu(_conv_rows(ext, w_ref, taps)[HALO:])

    main, prev = _conv_specs(T, S, PQ // CB, False)
    return pl.pallas_call(
        body, name=name, grid=(ncb, S // T),
        in_specs=[main, prev, pl.BlockSpec((taps, CB), lambda j, i: (0, j))],
        out_specs=pl.BlockSpec((T, CB), lambda j, i: (i, j)), out_shape=jax.ShapeDtypeStruct((S, 3 * GDN_W), F32),
        compiler_params=_cp("parallel", "parallel"),
    )(proj, proj, w)


def _gdn_conv_bwd(proj, dact, w, name):
    S = proj.shape[0]
    T = _row_tile(S)
    n = S // T
    taps = w.shape[0]
    ncb = 3 * GDN_W // CB

    def body(x_ref, xp_ref, xn_ref, d_ref, dn_ref, w_ref, dx_ref, gw_ref):
        i = pl.program_id(1)
        last = i == n - 1
        ext = jnp.concatenate([jnp.where(i > 0, xp_ref[...], 0.0), x_ref[...], jnp.where(last, 0.0, xn_ref[...])], axis=0)
        c = _conv_rows(ext, w_ref, taps)[HALO:]
        d_ext = jnp.concatenate([d_ref[...], jnp.where(last, 0.0, dn_ref[...])], axis=0)
        _, vjp = jax.vjp(jax.nn.silu, c)
        dc = vjp(d_ext)[0]
        dx_ref[...] = _conv_t_rows(dc, w_ref, taps)[:T].astype(dx_ref.dtype)

        @pl.when(i == 0)
        def _():
            gw_ref[...] = jnp.zeros_like(gw_ref)

        dcm = dc[:T]
        for k in range(taps):
            gw_ref[taps - 1 - k:taps - k, :] += jnp.sum(dcm * _down(ext, k)[HALO:HALO + T], axis=0, keepdims=True)

    main, prev, nxt = _conv_specs(T, S, PQ // CB, True)
    dmain, _, dnxt = _conv_specs(T, S, 0, True)
    wsp = pl.BlockSpec((taps, CB), lambda j, i: (0, j))
    return pl.pallas_call(
        body, name=name, grid=(ncb, n), in_specs=[main, prev, nxt, dmain, dnxt, wsp],
        out_specs=[pl.BlockSpec((T, CB), lambda j, i: (i, j)), wsp],
        out_shape=[jax.ShapeDtypeStruct((S, 3 * GDN_W), BF16), jax.ShapeDtypeStruct((taps, 3 * GDN_W), F32)],
        compiler_params=_cp("parallel", "arbitrary"),
    )(proj, proj, proj, dact, dact, w)


def _ffn_act_fwd(up, w, name):
    S = up.shape[0]
    T = _row_tile(S)
    taps = w.shape[0]
    ncb = D_FF // CB

    def body(g_ref, gp_ref, v_ref, w_ref, o_ref):
        i = pl.program_id(1)
        ext = jnp.concatenate([jnp.where(i > 0, gp_ref[...], 0.0), g_ref[...]], axis=0)
        c = _conv_rows(ext, w_ref, taps)[HALO:]
        o_ref[...] = (jax.nn.gelu(c) * v_ref[...]).astype(o_ref.dtype)

    main, prev = _conv_specs(T, S, 0, False)
    val = pl.BlockSpec((T, CB), lambda j, i: (i, j + ncb))
    return pl.pallas_call(
        body, name=name, grid=(ncb, S // T),
        in_specs=[main, prev, val, pl.BlockSpec((taps, CB), lambda j, i: (0, j))],
        out_specs=pl.BlockSpec((T, CB), lambda j, i: (i, j)), out_shape=jax.ShapeDtypeStruct((S, D_FF), BF16),
        compiler_params=_cp("parallel", "parallel"),
    )(up, up, up, w)


def _ffn_act_bwd(up, dact, w, name):
    S = up.shape[0]
    T = _row_tile(S)
    n = S // T
    taps = w.shape[0]
    ncb = D_FF // CB

    def body(g_ref, gp_ref, gn_ref, v_ref, vn_ref, d_ref, dn_ref, w_ref, dup_ref, gw_ref):
        i = pl.program_id(1)
        last = i == n - 1
        ext = jnp.concatenate([jnp.where(i > 0, gp_ref[...], 0.0), g_ref[...], jnp.where(last, 0.0, gn_ref[...])], axis=0)
        c = _conv_rows(ext, w_ref, taps)[HALO:]
        v_ext = jnp.concatenate([v_ref[...], jnp.where(last, 0.0, vn_ref[...])], axis=0)
        d_ext = jnp.concatenate([d_ref[...], jnp.where(last, 0.0, dn_ref[...])], axis=0)
        gl, vjp = jax.vjp(jax.nn.gelu, c)
        dup_ref[1] = (d_ext * gl)[:T].astype(dup_ref.dtype)
        dc = vjp(d_ext * v_ext)[0]
        dup_ref[0] = _conv_t_rows(dc, w_ref, taps)[:T].astype(dup_ref.dtype)

        @pl.when(i == 0)
        def _():
            gw_ref[...] = jnp.zeros_like(gw_ref)

        dcm = dc[:T]
        for k in range(taps):
            gw_ref[taps - 1 - k:taps - k, :] += jnp.sum(dcm * _down(ext, k)[HALO:HALO + T], axis=0, keepdims=True)

    main, prev, nxt = _conv_specs(T, S, 0, True)
    vmain, _, vnxt = _conv_specs(T, S, ncb, True)
    wsp = pl.BlockSpec((taps, CB), lambda j, i: (0, j))
    osp = pl.BlockSpec((2, T, CB), lambda j, i: (0, i, j))
    return pl.pallas_call(
        body, name=name, grid=(ncb, n), in_specs=[main, prev, nxt, vmain, vnxt, main, nxt, wsp],
        out_specs=[osp, wsp],
        out_shape=[jax.ShapeDtypeStruct((2, S, D_FF), BF16), jax.ShapeDtypeStruct((taps, D_FF), F32)],
        compiler_params=_cp("parallel", "arbitrary"),
    )(up, up, up, up, up, dact, dact, w)


def _tri_masks():
    r = _rows((GDN_C, GDN_C))
    c = _lanes((GDN_C, GDN_C))
    return r >= c, r > c


def _each(fn, *cols):
    return tuple(fn(*args) for args in zip(*cols))


def _tri_inv_raw(lows):
    r = _rows(lows[0].shape)
    c = _lanes(lows[0].shape)
    eye = jnp.where(r == c, 1.0, 0.0)
    ps = _each(lambda low: eye - low, lows)
    lps = lows
    for _ in range(5):
        lps = _each(lambda lp: _mm(lp, lp, False, False, True), lps)
        ps = _each(lambda p, lp: p + _mm(p, lp, False, False, True), ps, lps)
    return ps


@jax.custom_vjp
def _tri_inv(lows):
    return _tri_inv_raw(lows)


def _tri_inv_fwd(lows):
    ts = _tri_inv_raw(lows)
    return ts, ts


def _tri_inv_bwd(ts, dts):
    inner = _each(lambda t, dt: _mm(t, dt, True, False, True), ts, dts)
    return (_each(lambda m, t: -_mm(m, t, False, True, True), inner, ts),)


_tri_inv.defvjp(_tri_inv_fwd, _tri_inv_bwd)


def _gdn_glog(a_col, alog, dtb):
    return -jnp.exp(alog) * jax.nn.softplus(a_col + dtb)


def _decay_operand():
    r = _rows((GDN_C, 2 * GDN_C))
    c = _lanes((GDN_C, 2 * GDN_C))
    return jnp.where((c >= GDN_C) | (r > c), 1.0, 0.0)


def _gdn_decay(glog):
    causal, _ = _tri_masks()
    res = _mm01(jnp.where(causal, 1.0, 0.0), glog * _decay_operand())
    return res[:, GDN_C:GDN_C + 1], res[:, :GDN_C]


def _gdn_decay_bwd(dgcol, dd):
    r = _rows((GDN_C, GDN_C))
    c = _lanes((GDN_C, GDN_C))
    dres = jnp.concatenate([dd, jnp.where(c == 0, dgcol, 0.0)], axis=1)
    dx = _mm01(jnp.where(r <= c, 1.0, 0.0), dres)
    return jnp.sum(dx * _decay_operand(), axis=1, keepdims=True)


def _gdn_chunk(qa, ka, va, bt_col, gcol, dmat):
    causal, strict = _tri_masks()
    qn = _each(lambda q: q * lax.rsqrt(jnp.sum(q * q, axis=-1, keepdims=True) + EPS) * (GDN_DH ** -0.5), qa)
    kn = _each(lambda k: k * lax.rsqrt(jnp.sum(k * k, axis=-1, keepdims=True) + EPS), ka)
    beta = _each(jax.nn.sigmoid, bt_col)
    eg = _each(jnp.exp, gcol)
    decay = _each(lambda d: jnp.where(causal, jnp.exp(d), 0.0), dmat)
    kk = _each(lambda k: _mm(k, k, False, True), kn)
    low = _each(lambda b, m, d: jnp.where(strict, b * m * d, 0.0), beta, kk, decay)
    t = _tri_inv(low)
    w = _each(lambda t_, k, b, e: _mm(t_, k * (b * e), False, False, True), t, kn, beta, eg)
    u = _each(lambda t_, v, b: _mm(t_, v * b, False, False, True), t, va, beta)
    attn = _each(lambda q, k, d: _mm(q, k, False, True) * d, qn, kn, decay)
    last = _rows(gcol[0].shape) == GDN_C - 1
    g_last = _each(lambda g: jnp.sum(jnp.where(last, g, 0.0), axis=0, keepdims=True), gcol)
    qd = _each(lambda q, e: q * e, qn, eg)
    kd = _each(lambda k, gl, g: k * jnp.exp(gl - g), kn, g_last, gcol)
    return w, u, qd, kd, attn


def _gdn_step(state, w, u, qd, kd, attn, egl):
    v_new = _each(lambda u_, w_, s: u_ - _mm(w_, s), u, w, state)
    o_state = _each(_mm, qd, state)
    o = _each(lambda os, a, v: os + _mm(a, v), o_state, attn, v_new)
    new = _each(lambda s, e, k, v: s * e + _mm(k, v, True, False), state, egl, kd, v_new)
    return o, new


def _heads(ref, base=0, width=GDN_DH):
    return tuple(ref[:, (base + h) * GDN_DH:(base + h) * GDN_DH + width] for h in range(GDN_H))


def _cols(a, base):
    return tuple(a[:, base + h:base + h + 1] for h in range(GDN_H))


def _gated_norm(o, z, nw):
    return o * lax.rsqrt(jnp.mean(o * o, axis=-1, keepdims=True) + EPS) * nw * jax.nn.silu(z)


def _hsl(h):
    return slice(h * GDN_DH, (h + 1) * GDN_DH)


def _pad_lanes(a, width=GDN_DH):
    return jnp.concatenate([a, jnp.zeros((a.shape[0], width - a.shape[1]), a.dtype)], axis=1)


def _gdn_prep(qkv, proj, alog, dtb, name):
    S = qkv.shape[0]
    N = S // GDN_C

    def body(qkv_ref, ab_ref, al_ref, dt_ref, w_ref, u_ref, qd_ref, kd_ref, at_ref, gc_ref):
        ab = ab_ref[...]
        glog = _each(_gdn_glog, _cols(ab, 0), _cols(al_ref[...], 0), _cols(dt_ref[...], 0))
        dec = _each(_gdn_decay, glog)
        gcol, dmat = _each(lambda d: d[0], dec), _each(lambda d: d[1], dec)
        w, u, qd, kd, attn = _gdn_chunk(_heads(qkv_ref), _heads(qkv_ref, GDN_H), _heads(qkv_ref, 2 * GDN_H),
                                        _cols(ab, GDN_H), gcol, dmat)
        gc = jnp.zeros((GDN_C, 128), F32)
        for h in range(GDN_H):
            w_ref[:, _hsl(h)] = w[h]
            u_ref[:, _hsl(h)] = u[h]
            qd_ref[:, _hsl(h)] = qd[h]
            kd_ref[:, _hsl(h)] = kd[h]
            at_ref[:, _hsl(h)] = _pad_lanes(attn[h])
            gc = jnp.where(_lanes(gc.shape) == h, gcol[h], gc)
        gc_ref[...] = gc

    vec = pl.BlockSpec((1, 128), lambda i: (0, 0))
    hsp = pl.BlockSpec((GDN_C, GDN_W), lambda i: (i, 0))
    hshape = jax.ShapeDtypeStruct((S, GDN_W), F32)
    return pl.pallas_call(
        body, name=name, grid=(N,),
        in_specs=[pl.BlockSpec((GDN_C, 3 * GDN_W), lambda i: (i, 0)), pl.BlockSpec((GDN_C, 128), lambda i: (i, PAB // 128)), vec, vec],
        out_specs=[hsp] * 5 + [pl.BlockSpec((GDN_C, 128), lambda i: (i, 0))],
        out_shape=[hshape] * 5 + [jax.ShapeDtypeStruct((S, 128), F32)],
        compiler_params=_cp("parallel"),
    )(qkv, proj, alog, dtb)


def _gdn_scan(w, u, qd, kd, attn, gc, name):
    S = w.shape[0]
    N = S // GDN_C

    def body(w_ref, u_ref, qd_ref, kd_ref, at_ref, gc_ref, o_ref, st_ref, s_ref):
        @pl.when(pl.program_id(0) == 0)
        def _():
            s_ref[...] = jnp.zeros_like(s_ref)

        state = tuple(s_ref[_hsl(h), :] for h in range(GDN_H))
        egl = _each(jnp.exp, _cols(gc_ref[GDN_C - 1:GDN_C, :], 0))
        o, new = _gdn_step(state, _heads(w_ref), _heads(u_ref), _heads(qd_ref), _heads(kd_ref),
                           _heads(at_ref, width=GDN_C), egl)
        for h in range(GDN_H):
            st_ref[_hsl(h), :] = state[h]
            o_ref[:, _hsl(h)] = o[h]
            s_ref[_hsl(h), :] = new[h]

    hsp = pl.BlockSpec((GDN_C, GDN_W), lambda i: (i, 0))
    return pl.pallas_call(
        body, name=name, grid=(N,),
        in_specs=[hsp] * 5 + [pl.BlockSpec((GDN_C, 128), lambda i: (i, 0))],
        out_specs=[hsp, pl.BlockSpec((None, GDN_W, GDN_DH), lambda i: (i, 0, 0))],
        out_shape=[jax.ShapeDtypeStruct((S, GDN_W), F32), jax.ShapeDtypeStruct((N, GDN_W, GDN_DH), F32)],
        scratch_shapes=[pltpu.VMEM((GDN_W, GDN_DH), F32)],
        compiler_params=_cp("arbitrary"),
    )(w, u, qd, kd, attn, gc)


def _gdn_scan_bwd(w, u, qd, kd, attn, gc, states, o, proj, dmixed, nw, name):
    S = w.shape[0]
    N = S // GDN_C

    def body(w_ref, u_ref, qd_ref, kd_ref, at_ref, gc_ref, st_ref, o_ref, z_ref, dm_ref, nw_ref,
             dw_ref, du_ref, dqd_ref, dkd_ref, dat_ref, dgl_ref, dz_ref, gnw_ref, ds_ref):
        @pl.when(pl.program_id(0) == 0)
        def _():
            ds_ref[...] = jnp.zeros_like(ds_ref)
            gnw_ref[...] = jnp.zeros_like(gnw_ref)

        nw = nw_ref[...]
        _, vjp_n = jax.vjp(lambda o, z, w_: _each(lambda a, b: _gated_norm(a, b, w_), o, z), _heads(o_ref), _heads(z_ref), nw)
        do, dz, dnw = vjp_n(_heads(dm_ref))
        state = tuple(st_ref[_hsl(h), :] for h in range(GDN_H))
        egl = _each(jnp.exp, _cols(gc_ref[GDN_C - 1:GDN_C, :], 0))
        _, vjp_s = jax.vjp(_gdn_step, state, _heads(w_ref), _heads(u_ref), _heads(qd_ref), _heads(kd_ref),
                           _heads(at_ref, width=GDN_C), egl)
        ds, dw, du, dqd, dkd, dat, degl = vjp_s((do, tuple(ds_ref[_hsl(h), :] for h in range(GDN_H))))
        dgl = jnp.zeros((8, 128), F32)
        for h in range(GDN_H):
            dz_ref[:, _hsl(h)] = dz[h].astype(dz_ref.dtype)
            ds_ref[_hsl(h), :] = ds[h]
            dw_ref[:, _hsl(h)] = dw[h]
            du_ref[:, _hsl(h)] = du[h]
            dqd_ref[:, _hsl(h)] = dqd[h]
            dkd_ref[:, _hsl(h)] = dkd[h]
            dat_ref[:, _hsl(h)] = _pad_lanes(dat[h])
            dgl = jnp.where(_lanes(dgl.shape) == h, degl[h] * egl[h], dgl)
        dgl_ref[...] = dgl
        gnw_ref[...] += dnw

    rev = lambda i: (N - 1 - i, 0)
    hsp = pl.BlockSpec((GDN_C, GDN_W), rev)
    gsp = pl.BlockSpec((GDN_C, 128), rev)
    vec = pl.BlockSpec((1, GDN_DH), lambda i: (0, 0))
    hshape = jax.ShapeDtypeStruct((S, GDN_W), F32)
    return pl.pallas_call(
        body, name=name, grid=(N,),
        in_specs=[hsp] * 5 + [gsp, pl.BlockSpec((None, GDN_W, GDN_DH), lambda i: (N - 1 - i, 0, 0)), hsp,
                              pl.BlockSpec((GDN_C, GDN_W), lambda i: (N - 1 - i, PZ // GDN_W)), hsp, vec],
        out_specs=[hsp] * 5 + [pl.BlockSpec((8, 128), rev), hsp, vec],
        out_shape=[hshape] * 5 + [jax.ShapeDtypeStruct((N * 8, 128), F32), jax.ShapeDtypeStruct((S, GDN_W), BF16),
                                  jax.ShapeDtypeStruct((1, GDN_DH), F32)],
        scratch_shapes=[pltpu.VMEM((GDN_W, GDN_DH), F32)],
        compiler_params=_cp("arbitrary"),
    )(w, u, qd, kd, attn, gc, states, o, proj, dmixed, nw)


def _gdn_prep_bwd(qkv, proj, alog, dtb, dw, du, dqd, dkd, dat, dgl, name):
    S = qkv.shape[0]
    N = S // GDN_C

    def body(qkv_ref, ab_ref, al_ref, dt_ref, dw_ref, du_ref, dqd_ref, dkd_ref, dat_ref, dgl_ref,
             dqkv_ref, dab_ref, gal_ref, gdt_ref):
        @pl.when(pl.program_id(0) == 0)
        def _():
            gal_ref[...] = jnp.zeros_like(gal_ref)
            gdt_ref[...] = jnp.zeros_like(gdt_ref)

        ab = ab_ref[...]
        glog, vjp_g = jax.vjp(lambda a, al, dt: _each(_gdn_glog, a, al, dt), _cols(ab, 0), _cols(al_ref[...], 0),
                              _cols(dt_ref[...], 0))
        dec = _each(_gdn_decay, glog)
        gcol, dmat = _each(lambda d: d[0], dec), _each(lambda d: d[1], dec)
        _, vjp_c = jax.vjp(_gdn_chunk, _heads(qkv_ref), _heads(qkv_ref, GDN_H), _heads(qkv_ref, 2 * GDN_H),
                           _cols(ab, GDN_H), gcol, dmat)
        dqa, dka, dva, dbt, dgcol, dd = vjp_c((_heads(dw_ref), _heads(du_ref), _heads(dqd_ref), _heads(dkd_ref),
                                               _heads(dat_ref, width=GDN_C)))
        last = _rows(dgcol[0].shape) == GDN_C - 1
        dgcol = _each(lambda d, g: d + jnp.where(last, g, 0.0), dgcol, _cols(dgl_ref[0:1, :], 0))
        da_col, dal, ddt = vjp_g(_each(_gdn_decay_bwd, dgcol, dd))
        dab = jnp.zeros((GDN_C, 128), F32)
        gal = jnp.zeros((1, 128), F32)
        gdt = jnp.zeros((1, 128), F32)
        for h in range(GDN_H):
            dqkv_ref[:, _hsl(h)] = dqa[h]
            dqkv_ref[:, _hsl(GDN_H + h)] = dka[h]
            dqkv_ref[:, _hsl(2 * GDN_H + h)] = dva[h]
            ln = _lanes(dab.shape)
            dab = dab + jnp.where(ln == h, da_col[h], 0.0) + jnp.where(ln == GDN_H + h, dbt[h], 0.0)
            l1 = _lanes(gal.shape)
            gal = gal + jnp.where(l1 == h, dal[h], 0.0)
            gdt = gdt + jnp.where(l1 == h, ddt[h], 0.0)
        dab_ref[...] = dab.astype(dab_ref.dtype)
        gal_ref[...] += gal
        gdt_ref[...] += gdt

    vec = pl.BlockSpec((1, 128), lambda i: (0, 0))
    hsp = pl.BlockSpec((GDN_C, GDN_W), lambda i: (i, 0))
    qsp = pl.BlockSpec((GDN_C, 3 * GDN_W), lambda i: (i, 0))
    return pl.pallas_call(
        body, name=name, grid=(N,),
        in_specs=[qsp, pl.BlockSpec((GDN_C, 128), lambda i: (i, PAB // 128)), vec, vec] + [hsp] * 5
        + [pl.BlockSpec((8, 128), lambda i: (i, 0))],
        out_specs=[qsp, pl.BlockSpec((GDN_C, 128), lambda i: (i, 0)), vec, vec],
        out_shape=[jax.ShapeDtypeStruct((S, 3 * GDN_W), F32), jax.ShapeDtypeStruct((S, 128), BF16),
                   jax.ShapeDtypeStruct((1, 128), F32), jax.ShapeDtypeStruct((1, 128), F32)],
        compiler_params=_cp("arbitrary"),
    )(qkv, proj, alog, dtb, dw, du, dqd, dkd, dat, dgl)


@jax.custom_vjp
def _expm1(x):
    u = jnp.exp(x)
    lu = jnp.log(u)
    small = (u - 1.0) * x / jnp.where(u == 1.0, 1.0, lu)
    small = jnp.where(u == 1.0, x, small)
    return jnp.where(jnp.abs(x) < 0.5, small, u - 1.0)


def _expm1_fwd(x):
    return _expm1(x), jnp.exp(x)


def _expm1_bwd(ex, g):
    return (g * ex,)


_expm1.defvjp(_expm1_fwd, _expm1_bwd)


def _lru_gates(xc, wa, ba, wx, bx, lam, first):
    r = jax.nn.sigmoid(_mm(xc, wa) + ba)
    i = jax.nn.sigmoid(_mm(xc, wx) + bx)
    log_a = -LRU_C * r * jax.nn.softplus(-lam)
    mult = jnp.sqrt(-_expm1(2.0 * log_a))
    mult = jnp.where(first, 1.0, mult)
    return jnp.exp(log_a), mult * i * xc


def _scan_fwd(a, b):
    T = a.shape[0]
    rows = _rows(a.shape)
    s = 1
    while s < T:
        ok = rows >= s
        b = a * jnp.where(ok, _down(b, s), 0.0) + b
        a = a * jnp.where(ok, _down(a, s), 1.0)
        s *= 2
    return a, b


def _scan_rev(a, b):
    T = a.shape[0]
    rows = _rows(a.shape)
    s = 1
    while s < T:
        ok = rows + s < T
        b = a * jnp.where(ok, _up(b, s), 0.0) + b
        a = a * jnp.where(ok, _up(a, s), 1.0)
        s *= 2
    return b


def _bsl(j):
    return slice(j * LRU_BD, (j + 1) * LRU_BD)


def _lru_tile(S):
    return _row_tile(S, 256)


def _lru_fwd(proj, conv_w, conv_b, wa, ba, wx, bx, lam, name):
    S = proj.shape[0]
    T = _lru_tile(S)
    taps = conv_w.shape[0]
    r = T // HALO

    def body(x_ref, xp_ref, cw_ref, cb_ref, wa_ref, ba_ref, wx_ref, bx_ref, lam_ref, h_ref, carry_ref):
        i = pl.program_id(0)

        @pl.when(i == 0)
        def _():
            carry_ref[...] = jnp.zeros_like(carry_ref)

        ext = jnp.concatenate([jnp.where(i > 0, xp_ref[...], 0.0), x_ref[...]], axis=0)
        xc = _conv_rows(ext, cw_ref, taps)[HALO:] + cb_ref[...]
        first = (i * T + _rows((T, LRU_BD))) == 0
        for j in range(LRU_NB):
            a, b = _lru_gates(xc[:, _bsl(j)], wa_ref[j], ba_ref[:, _bsl(j)], wx_ref[j], bx_ref[:, _bsl(j)],
                              lam_ref[:, _bsl(j)], first=first)
            pa, hb = _scan_fwd(a, b)
            h_ref[:, _bsl(j)] = pa * carry_ref[0:1, _bsl(j)] + hb
            carry_ref[0:1, _bsl(j)] = h_ref[T - 1:T, _bsl(j)]

    vec = pl.BlockSpec((1, LRU_W), lambda i: (0, 0))
    wsp = pl.BlockSpec((LRU_NB, LRU_BD, LRU_BD), lambda i: (0, 0, 0))
    return pl.pallas_call(
        body, name=name, grid=(S // T,),
        in_specs=[pl.BlockSpec((T, LRU_W), lambda i: (i, PXR // LRU_W)),
                  pl.BlockSpec((HALO, LRU_W), lambda i: (jnp.maximum(i * r - 1, 0), PXR // LRU_W)),
                  pl.BlockSpec((taps, LRU_W), lambda i: (0, 0)), vec, wsp, vec, wsp, vec, vec],
        out_specs=pl.BlockSpec((T, LRU_W), lambda i: (i, 0)), out_shape=jax.ShapeDtypeStruct((S, LRU_W), F32),
        scratch_shapes=[pltpu.VMEM((8, LRU_W), F32)],
        compiler_params=_cp("arbitrary"),
    )(proj, proj, conv_w, conv_b.reshape(1, LRU_W), wa, ba.reshape(1, LRU_W), wx, bx.reshape(1, LRU_W), lam.reshape(1, LRU_W))


def _lru_bwd(proj, hl, dmixed, conv_w, conv_b, wa, ba, wx, bx, lam, name):
    S = proj.shape[0]
    T = _lru_tile(S)
    n = S // T
    taps = conv_w.shape[0]
    r = T // HALO
    mb = 768 // LRU_W

    def body(x_ref, xp_ref, g_ref, h_ref, hp_ref, dy_ref, cw_ref, cb_ref, wa_ref, ba_ref, wx_ref, bx_ref, lam_ref,
             dx_ref, dg_ref, gcw_ref, gcb_ref, gwa_ref, gba_ref, gwx_ref, gbx_ref, glam_ref, carry_ref, dxc_ref, nxt_ref):
        s = pl.program_id(0)
        i = n - 1 - s

        @pl.when(s == 0)
        def _():
            carry_ref[...] = jnp.zeros_like(carry_ref)
            nxt_ref[...] = jnp.zeros_like(nxt_ref)
            for ref in (gcw_ref, gcb_ref, gwa_ref, gba_ref, gwx_ref, gbx_ref, glam_ref):
                ref[...] = jnp.zeros_like(ref)

        ext = jnp.concatenate([jnp.where(i > 0, xp_ref[...], 0.0), x_ref[...]], axis=0)
        xc = _conv_rows(ext, cw_ref, taps)[HALO:] + cb_ref[...]
        rows = _rows((T, LRU_BD))
        first = (i * T + rows) == 0
        h_before = jnp.where(i > 0, hp_ref[HALO - 1:HALO, :], 0.0)
        for j in range(LRU_NB):
            sl = _bsl(j)
            (a, _), vjp_g = jax.vjp(functools.partial(_lru_gates, first=first), xc[:, sl], wa_ref[j], ba_ref[:, sl],
                                    wx_ref[j], bx_ref[:, sl], lam_ref[:, sl])
            gelu_g, vjp_a = jax.vjp(jax.nn.gelu, g_ref[:, sl])
            h = h_ref[:, sl]
            dy = dy_ref[:, sl]
            dg_ref[:, sl] = vjp_a(dy * h)[0].astype(dg_ref.dtype)
            b_rev = dy * gelu_g + jnp.where(rows == T - 1, carry_ref[0:1, sl], 0.0)
            a_rev = jnp.where(rows == T - 1, 0.0, _up(a, 1))
            dh = _scan_rev(a_rev, b_rev)
            carry_ref[:, sl] = (a * dh)[:HALO]
            h_prev = jnp.where(rows == 0, h_before[:, sl], _down(h, 1))
            dxc, dwa, dba, dwx, dbx, dlam = vjp_g((dh * h_prev, dh))
            dxc_ref[:, sl] = dxc
            gwa_ref[j] += dwa
            gwx_ref[j] += dwx
            gba_ref[:, sl] += dba
            gbx_ref[:, sl] += dbx
            glam_ref[:, sl] += dlam
        dxc = dxc_ref[...]
        d_ext = jnp.concatenate([dxc, nxt_ref[...]], axis=0)
        dx_ref[...] = _conv_t_rows(d_ext, cw_ref, taps)[:T].astype(dx_ref.dtype)
        nxt_ref[...] = dxc[:HALO]
        gcb_ref[...] += jnp.sum(dxc, axis=0, keepdims=True)
        for k in range(taps):
            gcw_ref[taps - 1 - k:taps - k, :] += jnp.sum(dxc * _down(ext, k)[HALO:], axis=0, keepdims=True)

    vec = pl.BlockSpec((1, LRU_W), lambda s: (0, 0))
    wsp = pl.BlockSpec((LRU_NB, LRU_BD, LRU_BD), lambda s: (0, 0, 0))
    cwsp = pl.BlockSpec((taps, LRU_W), lambda s: (0, 0))

    def main(cb):
        return pl.BlockSpec((T, LRU_W), lambda s: (n - 1 - s, cb))

    def prev(cb):
        return pl.BlockSpec((HALO, LRU_W), lambda s: (jnp.maximum((n - 1 - s) * r - 1, 0), cb))

    vshape = jax.ShapeDtypeStruct((1, LRU_W), F32)
    wshape = jax.ShapeDtypeStruct((LRU_NB, LRU_BD, LRU_BD), F32)
    return pl.pallas_call(
        body, name=name, grid=(n,),
        in_specs=[main(PXR // LRU_W), prev(PXR // LRU_W), main(PGR // LRU_W), main(0), prev(0), main(mb),
                  cwsp, vec, wsp, vec, wsp, vec, vec],
        out_specs=[main(0), main(0), cwsp, vec, wsp, vec, wsp, vec, vec],
        out_shape=[jax.ShapeDtypeStruct((S, LRU_W), BF16), jax.ShapeDtypeStruct((S, LRU_W), BF16),
                   jax.ShapeDtypeStruct((taps, LRU_W), F32), vshape, wshape, vshape, wshape, vshape, vshape],
        scratch_shapes=[pltpu.VMEM((8, LRU_W), F32), pltpu.VMEM((T, LRU_W), F32), pltpu.VMEM((HALO, LRU_W), F32)],
        compiler_params=_cp("arbitrary"),
    )(proj, proj, proj, hl, hl, dmixed, conv_w, conv_b.reshape(1, LRU_W), wa, ba.reshape(1, LRU_W), wx,
      bx.reshape(1, LRU_W), lam.reshape(1, LRU_W))


def _mix_out(o, proj, hl, y_pool, nw, name):
    S = o.shape[0]
    T = _row_tile(S)

    def body(o_ref, z_ref, h_ref, g_ref, p_ref, nw_ref, m_ref):
        for h in range(GDN_H):
            m_ref[:, _hsl(h)] = _gated_norm(o_ref[:, _hsl(h)], z_ref[:, _hsl(h)], nw_ref[...]).astype(m_ref.dtype)
        m_ref[:, GDN_W:GDN_W + LRU_W] = (h_ref[...] * jax.nn.gelu(g_ref[...])).astype(m_ref.dtype)
        m_ref[:, GDN_W + LRU_W:] = p_ref[...].astype(m_ref.dtype)

    row = pl.BlockSpec((T, GDN_W), lambda i: (i, 0))
    return pl.pallas_call(
        body, name=name, grid=(S // T,),
        in_specs=[row, pl.BlockSpec((T, GDN_W), lambda i: (i, PZ // GDN_W)), row,
                  pl.BlockSpec((T, LRU_W), lambda i: (i, PGR // LRU_W)), pl.BlockSpec((T, POOL_W), lambda i: (i, 0)),
                  pl.BlockSpec((1, GDN_DH), lambda i: (0, 0))],
        out_specs=pl.BlockSpec((T, D_MODEL), lambda i: (i, 0)), out_shape=jax.ShapeDtypeStruct((S, D_MODEL), BF16),
        compiler_params=_cp("parallel"),
    )(o, proj, hl, proj, y_pool, nw)


def _as2d(a):
    return a.reshape(-1, a.shape[-1])


def _ew_rows(rows, cols):
    t = rows
    while t * cols * 4 > (2 << 20) and t % 16 == 0:
        t //= 2
    return t


def _rs_rows(rows, cols):
    t = rows
    while t * cols * 4 > (2 << 20) and t % 32 == 0:
        t //= 2
    return t


def _rs_add(src, recv, src_index, grid_lead, out_dtype, name, by_layer=False):
    rows, cols = recv.shape[-2:]
    t = _rs_rows(rows, cols)
    nl = len(grid_lead)
    lead_none = (None,) * (src.ndim - 2)

    def body(p_ref, s_ref, r_ref, o_ref):
        o_ref[...] = (s_ref[...].astype(F32) + r_ref[...].astype(F32)).astype(o_ref.dtype)

    def src_map(*a):
        return (*src_index(*a[:nl], a[-1]), a[nl], 0)

    def own_map(*a):
        return (*a[:nl], a[nl], 0)

    def layer_map(*a):
        return (a[-1][2], *a[:nl], a[nl], 0)

    x, y, c = _place()
    place = jnp.stack([x, y, c]).astype(jnp.int32)
    own = pl.BlockSpec(((None,) * nl) + (t, cols), own_map)
    return pl.pallas_call(
        body, name=name,
        grid_spec=pltpu.PrefetchScalarGridSpec(
            num_scalar_prefetch=1, grid=(*grid_lead, rows // t),
            in_specs=[pl.BlockSpec((*lead_none, t, cols), src_map), own],
            out_specs=pl.BlockSpec(((None,) * (nl + 1)) + (t, cols), layer_map) if by_layer else own),
        out_shape=jax.ShapeDtypeStruct(((2,) if by_layer else ()) + recv.shape, out_dtype),
        compiler_params=pltpu.CompilerParams(dimension_semantics=("parallel",) * (nl + 1), vmem_limit_bytes=VMEM_LIMIT),
    )(place, src, recv)


def _add(a, b, name):
    shape = a.shape
    a2, b2 = _as2d(a), _as2d(b)
    rows, cols = a2.shape
    t = _ew_rows(rows, cols)

    def body(a_ref, b_ref, o_ref):
        o_ref[...] = a_ref[...] + b_ref[...]

    sp = pl.BlockSpec((t, cols), lambda i: (i, 0))
    return pl.pallas_call(body, name=name, grid=(rows // t,), in_specs=[sp, sp], out_specs=sp,
                          out_shape=jax.ShapeDtypeStruct((rows, cols), F32), compiler_params=_cp("parallel"))(a2, b2).reshape(shape)


def _adamw(w, g, m, v, name):
    shape = w.shape
    w2, g2, m2, v2 = _as2d(w), _as2d(g), _as2d(m), _as2d(v)
    rows, cols = w2.shape
    t = _ew_rows(rows, cols)

    def body(w_ref, g_ref, m_ref, v_ref, d_ref, nm_ref, nv_ref):
        gr = g_ref[...]
        nm = ADAM_B1 * m_ref[...] + (1.0 - ADAM_B1) * gr
        nv = ADAM_B2 * v_ref[...] + (1.0 - ADAM_B2) * (gr * gr)
        m_hat = nm / (1.0 - ADAM_B1 ** ADAM_STEP)
        v_hat = nv / (1.0 - ADAM_B2 ** ADAM_STEP)
        d_ref[...] = -ADAM_LR * (m_hat / (jnp.sqrt(v_hat) + ADAM_EPS) + ADAM_WD * w_ref[...])
        nm_ref[...] = nm
        nv_ref[...] = nv

    sp = pl.BlockSpec((t, cols), lambda i: (i, 0))
    sh = jax.ShapeDtypeStruct((rows, cols), F32)
    d, nm, nv = pl.pallas_call(body, name=name, grid=(rows // t,), in_specs=[sp] * 4, out_specs=[sp] * 3,
                               out_shape=[sh] * 3, compiler_params=_cp("parallel"))(w2, g2, m2, v2)
    return d.reshape(shape), nm.reshape(shape), nv.reshape(shape)


def _place():
    return lax.axis_index("x"), lax.axis_index("y"), lax.axis_index("c")


def _gather_weights(arrs, name):
    n = len(arrs)

    def body(*refs):
        outs = refs[n:2 * n]
        send, recv = refs[2 * n:]
        x, y, c = _place()
        s_me, s_x, s_y, s_d = 2 * x + y, 2 * (1 - x) + y, 2 * x + (1 - y), 2 * (1 - x) + (1 - y)
        xpeer, ypeer, sib = (1 - x, y, c), (x, 1 - y, c), (x, y, 1 - c)

        def rc(k, t, src, dst, to):
            return pltpu.make_async_remote_copy(src_ref=src, dst_ref=dst, send_sem=send.at[k, t], recv_sem=recv.at[k, t],
                                                device_id=to, device_id_type=MESH)

        def half(k, s, top):
            rh = outs[k].shape[2] // 2
            return outs[k].at[c, s, pl.ds(0 if top else rh, rh)]

        sent = []

        def start(k, t, ref, to):
            cp = rc(k, t, ref, ref, to)
            cp.start()
            sent.append(cp)

        for k in range(n):
            start(k, 0, outs[k].at[c, s_me], xpeer)
            start(k, 1, outs[k].at[c, s_me], ypeer)
        for k in range(n):
            got = outs[k].at[c, s_x]
            rc(k, 0, got, got, xpeer).wait_recv()
            start(k, 2, half(k, s_x, True), ypeer)
            start(k, 3, got, sib)
        for k in range(n):
            got = outs[k].at[c, s_y]
            rc(k, 1, got, got, ypeer).wait_recv()
            start(k, 6, half(k, s_y, False), xpeer)
            start(k, 4, got, sib)
        for k in range(n):
            top, bottom = half(k, s_d, True), half(k, s_d, False)
            rc(k, 2, top, top, ypeer).wait_recv()
            rc(k, 6, bottom, bottom, xpeer).wait_recv()
            start(k, 5, outs[k].at[c, s_d], sib)
        for k in range(n):
            for t, s in ((3, s_x), (4, s_y), (5, s_d)):
                got = outs[k].at[1 - c, s]
                rc(k, t, got, got, sib).wait_recv()
        for cp in sent:
            cp.wait_send()

    return pl.pallas_call(
        body, name=name, in_specs=[HBM] * n, out_specs=[HBM] * n,
        out_shape=[jax.ShapeDtypeStruct(a.shape, a.dtype) for a in arrs],
        input_output_aliases={k: k for k in range(n)},
        scratch_shapes=[pltpu.SemaphoreType.DMA((n, 7)), pltpu.SemaphoreType.DMA((n, 7))],
    )(*arrs)


def _exchange(arrs, axis, name, half=True):
    n = len(arrs)

    def body(*refs):
        srcs, outs = refs[:n], refs[n:2 * n]
        send, recv = refs[2 * n:]
        x, y, c = _place()
        p = {"x": x, "y": y, "c": c}[axis]
        peer = {"x": (1 - x, y, c), "y": (x, 1 - y, c), "c": (x, y, 1 - c)}[axis]
        cps = []
        for k in range(n):
            cp = pltpu.make_async_remote_copy(src_ref=srcs[k].at[1 - p] if half else srcs[k], dst_ref=outs[k],
                                              send_sem=send.at[k], recv_sem=recv.at[k], device_id=peer, device_id_type=MESH)
            cp.start()
            cps.append(cp)
        for cp in cps:
            cp.wait()

    return pl.pallas_call(
        body, name=name, in_specs=[HBM] * n, out_specs=[HBM] * n,
        out_shape=[jax.ShapeDtypeStruct(a.shape[1:] if half else a.shape, a.dtype) for a in arrs],
        scratch_shapes=[pltpu.SemaphoreType.DMA((n,)), pltpu.SemaphoreType.DMA((n,))],
    )(*arrs)


def _share_layers(arrs, name):
    n = len(arrs)

    def body(*refs):
        outs = refs[n:2 * n]
        send, recv = refs[2 * n:]
        x, y, c = _place()
        cps = []
        for k in range(n):
            mine = outs[k].at[c]
            cp = pltpu.make_async_remote_copy(src_ref=mine, dst_ref=mine, send_sem=send.at[k], recv_sem=recv.at[k],
                                              device_id=(x, y, 1 - c), device_id_type=MESH)
            cp.start()
            cps.append(cp)
        for k in range(n):
            got = outs[k].at[1 - c]
            pltpu.make_async_remote_copy(src_ref=got, dst_ref=got, send_sem=send.at[k], recv_sem=recv.at[k],
                                         device_id=(x, y, 1 - c), device_id_type=MESH).wait_recv()
        for cp in cps:
            cp.wait_send()

    return pl.pallas_call(
        body, name=name, in_specs=[HBM] * n, out_specs=[HBM] * n,
        out_shape=[jax.ShapeDtypeStruct(a.shape, a.dtype) for a in arrs],
        input_output_aliases={k: k for k in range(n)},
        scratch_shapes=[pltpu.SemaphoreType.DMA((n,)), pltpu.SemaphoreType.DMA((n,))],
    )(*arrs)


def _rs_exchange(arrs, phase, name):
    n = len(arrs)

    def body(*refs):
        srcs, outs = refs[:n], refs[n:2 * n]
        send, recv = refs[2 * n:]
        x, y, c = _place()
        xpeer, ypeer = (1 - x, y, c), (x, 1 - y, c)
        cps = []
        for k in range(n):
            if phase == 2:
                parts = ((srcs[k].at[1 - x, :, 0], xpeer), (srcs[k].at[:, 1 - y, 1], ypeer))
            else:
                parts = ((srcs[k].at[0, 1 - y], ypeer), (srcs[k].at[1, 1 - x], xpeer))
            for h, (src, to) in enumerate(parts):
                cp = pltpu.make_async_remote_copy(src_ref=src, dst_ref=outs[k].at[h], send_sem=send.at[k, h],
                                                  recv_sem=recv.at[k, h], device_id=to, device_id_type=MESH)
                cp.start()
                cps.append(cp)
        for cp in cps:
            cp.wait()

    def out_shape(a):
        return (2, 2) + a.shape[3:] if phase == 2 else (2,) + a.shape[2:]

    return pl.pallas_call(
        body, name=name, in_specs=[HBM] * n, out_specs=[HBM] * n,
        out_shape=[jax.ShapeDtypeStruct(out_shape(a), a.dtype) for a in arrs],
        scratch_shapes=[pltpu.SemaphoreType.DMA((n, 2)), pltpu.SemaphoreType.DMA((n, 2))],
    )(*arrs)


def _reduce_scatter(grads):
    r1 = _exchange(grads, "c", "rs_c")
    a1 = [_rs_add(g, r, lambda s, p: (p[2], s), (4,), BF16, f"rs_add1_{k}") for k, (g, r) in enumerate(zip(grads, r1))]
    a1 = [a.reshape(2, 2, 2, a.shape[1] // 2, a.shape[2]) for a in a1]
    r2 = _rs_exchange(a1, 2, "rs_p2")
    a2 = [_rs_add(a, r, lambda h, j, p: (p[0] * (1 - h) + j * h, j * (1 - h) + p[1] * h, h), (2, 2), BF16, f"rs_add2_{k}")
          for k, (a, r) in enumerate(zip(a1, r2))]
    r3 = _rs_exchange(a2, 3, "rs_p3")
    a3 = [_rs_add(a, r, lambda h, p: (h, p[1] * (1 - h) + p[0] * h), (2,), F32, f"rs_add3_{k}", by_layer=True)
          for k, (a, r) in enumerate(zip(a2, r3))]
    a3 = [a.reshape(2, 2 * a.shape[2], a.shape[3]) for a in a3]
    return _share_layers(a3, "rs_share")


def _all_reduce(buf):
    for axis in ("c", "x", "y"):
        (other,) = _exchange([buf], axis, f"ar_{axis}", half=False)
        buf = _add(buf, other, f"ar_add_{axis}")
    return buf


def _pad128(v):
    return jnp.zeros((1, 128), F32).at[0, :v.shape[0]].set(v)


def _layer_fwd(l, x, p):
    h1 = _rms_fwd(x, p["norm1_w"], f"rms1_{l}")
    proj = _matmul(h1, p["w_in"], "nn", name=f"mm_in_{l}", tn=768)
    y_pool = _pool_fwd(proj, p["pool_w"], p["pool_b"], p["pool_scale"], f"pool_{l}")
    qkv = _gdn_conv_fwd(proj, p["gdn_conv_w"], f"gconv_{l}")
    alog, dtb = _pad128(p["gdn_a_log"]), _pad128(p["gdn_dt_bias"])
    gw, gu, gqd, gkd, gat, gc = _gdn_prep(qkv, proj, alog, dtb, f"gprep_{l}")
    o, states = _gdn_scan(gw, gu, gqd, gkd, gat, gc, f"gscan_{l}")
    hl = _lru_fwd(proj, p["lru_conv_w"], p["lru_conv_b"], p["lru_wa"], p["lru_ba"], p["lru_wx"], p["lru_bx"],
                  p["lru_lambda"], f"lru_{l}")
    mixed = _mix_out(o, proj, hl, y_pool, p["gdn_norm_w"].reshape(1, GDN_DH), f"mix_{l}")
    x2 = _matmul(mixed, p["w_out"], "nn", name=f"mm_out_{l}", res=x)
    h2 = _rms_fwd(x2, p["norm2_w"], f"rms2_{l}")
    up = _matmul(h2, p["ffn_up"], "nn", name=f"mm_up_{l}", b_split=True)
    act = _ffn_act_fwd(up, p["ffn_conv_w"], f"ffn_{l}")
    x3 = _matmul(act, p["ffn_down"], "nn", name=f"mm_down_{l}", res=x2)
    saved = dict(x=x, h1=h1, proj=proj, qkv=qkv, gdn=(gw, gu, gqd, gkd, gat, gc), states=states, o=o, hl=hl, mixed=mixed,
                 x2=x2, h2=h2, up=up, act=act, alog=alog, dtb=dtb)
    return x3, saved


def _layer_bwd(l, dx3, p, s):
    g = {}
    dact = _matmul(dx3, p["ffn_down"], "nt", name=f"mm_ddown_{l}")
    g["ffn_down"] = _matmul(s["act"], dx3, "tn", name=f"mm_gdown_{l}", out_dtype=BF16)
    dup, g["ffn_conv_w"] = _ffn_act_bwd(s["up"], dact, p["ffn_conv_w"], f"ffn_b_{l}")
    dh2 = _matmul(dup, p["ffn_up"], "nt", name=f"mm_dup_{l}", b_split=True, tk=3072)
    g["ffn_up"] = _matmul(s["h2"], dup, "tn", name=f"mm_gup_{l}", b_split=True, o_split=4, tk=4096, out_dtype=BF16)
    dx2, g["norm2_w"] = _rms_bwd(s["x2"], p["norm2_w"], dh2, dx3, f"rms2_b_{l}")
    dmixed = _matmul(dx2, p["w_out"], "nt", name=f"mm_dout_{l}")
    g["w_out"] = _matmul(s["mixed"], dx2, "tn", name=f"mm_gout_{l}", out_dtype=BF16)
    proj = s["proj"]
    du_pool, g["pool_w"], g["pool_b"], g["pool_scale"] = _pool_bwd(proj, dmixed, p["pool_w"], p["pool_b"], p["pool_scale"], f"pool_b_{l}")
    gw, gu, gqd, gkd, gat, gc = s["gdn"]
    dw, du, dqd, dkd, dat, dgl, dz, g["gdn_norm_w"] = _gdn_scan_bwd(
        gw, gu, gqd, gkd, gat, gc, s["states"], s["o"], proj, dmixed, p["gdn_norm_w"].reshape(1, GDN_DH), f"gscan_b_{l}")
    dqkv, dab, gal, gdt = _gdn_prep_bwd(s["qkv"], proj, s["alog"], s["dtb"], dw, du, dqd, dkd, dat, dgl, f"gprep_b_{l}")
    g["gdn_a_log"], g["gdn_dt_bias"] = gal[0, :GDN_H], gdt[0, :GDN_H]
    dpre, g["gdn_conv_w"] = _gdn_conv_bwd(proj, dqkv, p["gdn_conv_w"], f"gconv_b_{l}")
    (dxr, dgr, g["lru_conv_w"], g["lru_conv_b"], g["lru_wa"], g["lru_ba"], g["lru_wx"], g["lru_bx"], g["lru_lambda"]) = _lru_bwd(
        proj, s["hl"], dmixed, p["lru_conv_w"], p["lru_conv_b"], p["lru_wa"], p["lru_ba"], p["lru_wx"], p["lru_bx"],
        p["lru_lambda"], f"lru_b_{l}")
    S = proj.shape[0]
    dproj = jnp.concatenate([dpre, dz, dxr, dgr, du_pool, dab, jnp.zeros((S, PCOLS - PAB - 128), BF16)], axis=1)
    dh1 = _matmul(dproj, p["w_in"], "nt", name=f"mm_din_{l}", tk=1792)
    g["w_in"] = _matmul(s["h1"], dproj, "tn", name=f"mm_gin_{l}", tn=768, tk=4096, out_dtype=BF16)
    dx, g["norm1_w"] = _rms_bwd(s["x"], p["norm1_w"], dh1, dx2, f"rms1_b_{l}")
    return dx, g


_IN_PERM = ((512, 3584), (3596, 5132), (0, 512), (3584, 3596))


def _w_in_to_proj(w):
    parts = [w[:, a:b] for a, b in _IN_PERM]
    return jnp.concatenate(parts + [jnp.zeros((w.shape[0], PCOLS - IN_COLS), w.dtype)], axis=1)


def _proj_to_w_in(g):
    return jnp.concatenate([g[:, PPOOL:PPOOL + 512], g[:, 0:3072], g[:, PAB:PAB + 12], g[:, 3072:PPOOL]], axis=1)


def _rows_to_mixed(w):
    return jnp.concatenate([w[512:], w[:512]], axis=0)


def _mixed_to_rows(g):
    return jnp.concatenate([g[1536:], g[:1536]], axis=0)


SMALL_SHARDED = ("gdn_conv_w", "lru_conv_w", "ffn_conv_w")
BIG = ("w_in", "w_out", "ffn_up", "ffn_down")
SMALL_REPLICATED = ("norm1_w", "pool_w", "pool_b", "pool_scale", "gdn_a_log", "gdn_dt_bias", "gdn_norm_w", "lru_conv_b",
                    "lru_wa", "lru_ba", "lru_wx", "lru_bx", "lru_lambda", "norm2_w")
WEIGHTS = ("norm1_w", "w_in", "pool_w", "pool_b", "pool_scale", "gdn_conv_w", "gdn_a_log", "gdn_dt_bias", "gdn_norm_w",
           "lru_conv_w", "lru_conv_b", "lru_wa", "lru_ba", "lru_wx", "lru_bx", "lru_lambda", "w_out", "norm2_w", "ffn_up",
           "ffn_conv_w", "ffn_down", "final_norm_w")
FLAT_COLS = 1024


def _pack(arrs):
    flat = jnp.concatenate([a.reshape(-1) for a in arrs])
    rows = -(-flat.shape[0] // (8 * FLAT_COLS)) * 8
    return jnp.pad(flat, (0, rows * FLAT_COLS - flat.shape[0])).reshape(rows, FLAT_COLS)


def _unpack(buf, like):
    flat = buf.reshape(-1)
    out, off = [], 0
    for a in like:
        size = 1
        for d in a.shape:
            size *= d
        out.append(flat[off:off + size].reshape(a.shape))
        off += size
    return out


def kernel(x, norm1_w, w_in, pool_w, pool_b, pool_scale, gdn_conv_w, gdn_a_log, gdn_dt_bias, gdn_norm_w, lru_conv_w, lru_conv_b, lru_wa, lru_ba, lru_wx, lru_bx, lru_lambda, w_out, norm2_w, ffn_up, ffn_conv_w, ffn_down, final_norm_w, loss_target, m_norm1_w, m_w_in, m_pool_w, m_pool_b, m_pool_scale, m_gdn_conv_w, m_gdn_a_log, m_gdn_dt_bias, m_gdn_norm_w, m_lru_conv_w, m_lru_conv_b, m_lru_wa, m_lru_ba, m_lru_wx, m_lru_bx, m_lru_lambda, m_w_out, m_norm2_w, m_ffn_up, m_ffn_conv_w, m_ffn_down, m_final_norm_w, v_norm1_w, v_w_in, v_pool_w, v_pool_b, v_pool_scale, v_gdn_conv_w, v_gdn_a_log, v_gdn_dt_bias, v_gdn_norm_w, v_lru_conv_w, v_lru_conv_b, v_lru_wa, v_lru_ba, v_lru_wx, v_lru_bx, v_lru_lambda, v_w_out, v_norm2_w, v_ffn_up, v_ffn_conv_w, v_ffn_down, v_final_norm_w):
    W = dict(norm1_w=norm1_w, w_in=w_in, pool_w=pool_w, pool_b=pool_b, pool_scale=pool_scale, gdn_conv_w=gdn_conv_w,
             gdn_a_log=gdn_a_log, gdn_dt_bias=gdn_dt_bias, gdn_norm_w=gdn_norm_w, lru_conv_w=lru_conv_w, lru_conv_b=lru_conv_b,
             lru_wa=lru_wa, lru_ba=lru_ba, lru_wx=lru_wx, lru_bx=lru_bx, lru_lambda=lru_lambda, w_out=w_out, norm2_w=norm2_w,
             ffn_up=ffn_up, ffn_conv_w=ffn_conv_w, ffn_down=ffn_down, final_norm_w=final_norm_w)
    M = dict(norm1_w=m_norm1_w, w_in=m_w_in, pool_w=m_pool_w, pool_b=m_pool_b, pool_scale=m_pool_scale, gdn_conv_w=m_gdn_conv_w,
             gdn_a_log=m_gdn_a_log, gdn_dt_bias=m_gdn_dt_bias, gdn_norm_w=m_gdn_norm_w, lru_conv_w=m_lru_conv_w,
             lru_conv_b=m_lru_conv_b, lru_wa=m_lru_wa, lru_ba=m_lru_ba, lru_wx=m_lru_wx, lru_bx=m_lru_bx, lru_lambda=m_lru_lambda,
             w_out=m_w_out, norm2_w=m_norm2_w, ffn_up=m_ffn_up, ffn_conv_w=m_ffn_conv_w, ffn_down=m_ffn_down,
             final_norm_w=m_final_norm_w)
    V = dict(norm1_w=v_norm1_w, w_in=v_w_in, pool_w=v_pool_w, pool_b=v_pool_b, pool_scale=v_pool_scale, gdn_conv_w=v_gdn_conv_w,
             gdn_a_log=v_gdn_a_log, gdn_dt_bias=v_gdn_dt_bias, gdn_norm_w=v_gdn_norm_w, lru_conv_w=v_lru_conv_w,
             lru_conv_b=v_lru_conv_b, lru_wa=v_lru_wa, lru_ba=v_lru_ba, lru_wx=v_lru_wx, lru_bx=v_lru_bx, lru_lambda=v_lru_lambda,
             w_out=v_w_out, norm2_w=v_norm2_w, ffn_up=v_ffn_up, ffn_conv_w=v_ffn_conv_w, ffn_down=v_ffn_down,
             final_norm_w=v_final_norm_w)
    S = x.shape[1]
    xs = x.reshape(S, D_MODEL)
    tgt = loss_target.reshape(S, D_MODEL)
    mx, my, mc = _place()
    shard = 2 * mx + my

    small_sh = jnp.concatenate([W[k].reshape(N_LAYERS, -1) for k in SMALL_SHARDED], axis=1)
    n_small = small_sh.shape[1]
    pad = -n_small % 1024
    small_sh = jnp.pad(small_sh, ((0, 0), (0, pad))).reshape(N_LAYERS, -1, 1024)
    def own_slot(w):
        zeros = jnp.zeros((N_LAYERS, 4) + w.shape[1:], w.dtype)
        return lax.dynamic_update_slice(zeros, w[:, None], (0, shard) + (0,) * (w.ndim - 1))

    gathered = _gather_weights([own_slot(W[k].astype(BF16)) for k in BIG] + [own_slot(small_sh)], "gather_weights")
    g_in, g_out, g_up, g_down, g_small = gathered
    g_small = g_small.reshape(N_LAYERS, 4, -1)[:, :, :n_small]

    layers = []
    for l in range(N_LAYERS):
        p = {k: W[k][l] for k in SMALL_REPLICATED}
        rows = g_in.shape[2]
        p["w_in"] = _w_in_to_proj(jnp.transpose(g_in[l], (1, 0, 2)).reshape(rows, IN_COLS))
        p["w_out"] = _rows_to_mixed(g_out[l].reshape(D_MODEL, D_MODEL))
        p["ffn_up"] = g_up[l]
        p["ffn_down"] = g_down[l].reshape(D_FF, D_MODEL)
        off = 0
        for k in SMALL_SHARDED:
            taps, width = W[k].shape[1], W[k].shape[2]
            piece = g_small[l, :, off:off + taps * width].reshape(4, taps, width)
            p[k] = jnp.transpose(piece, (1, 0, 2)).reshape(taps, 4 * width)
            off += taps * width
        layers.append(p)

    saved = []
    h = xs
    for l in range(N_LAYERS):
        h, s = _layer_fwd(l, h, layers[l])
        saved.append(s)
    loss_part, dh, g_final = _loss_head(h, final_norm_w, tgt, "loss_head")

    grads = [None] * N_LAYERS
    for l in reversed(range(N_LAYERS)):
        dh, grads[l] = _layer_bwd(l, dh, layers[l], saved[l])
    grad_x = dh.reshape(x.shape)

    def big_partial(k):
        per = []
        for l in range(N_LAYERS):
            g = grads[l][k]
            if k == "w_in":
                g = _proj_to_w_in(g)
                g = jnp.transpose(g.reshape(g.shape[0], 4, IN_COLS // 4), (1, 0, 2))
            elif k == "w_out":
                g = _mixed_to_rows(g).reshape(4, D_MODEL // 4, D_MODEL)
            elif k == "ffn_down":
                g = g.reshape(4, D_FF // 4, D_MODEL)
            per.append(g)
        return jnp.stack(per)

    big_g = dict(zip(BIG, _reduce_scatter([big_partial(k) for k in BIG])))

    small_names = SMALL_REPLICATED + SMALL_SHARDED
    small_list = [jnp.stack([grads[l][k].reshape(W[k].shape[1:]) if k in SMALL_REPLICATED else grads[l][k] for l in range(N_LAYERS)])
                  for k in small_names]
    small_list += [g_final.reshape(D_MODEL), loss_part[0, 0:1]]
    reduced = _unpack(_all_reduce(_pack(small_list)), small_list)
    small_g = dict(zip(small_names, reduced[:len(small_names)]))
    small_g["final_norm_w"] = reduced[-2]
    loss = reduced[-1][0]
    for k in SMALL_SHARDED:
        width = W[k].shape[2]
        small_g[k] = lax.dynamic_slice_in_dim(small_g[k], shard * width, width, axis=2)

    G, DELTA, NM, NV = {}, {}, {}, {}
    for k in BIG:
        G[k] = big_g[k]
        DELTA[k], NM[k], NV[k] = _adamw(W[k], G[k], M[k], V[k], f"adam_{k}")
    small_all = small_names + ("final_norm_w",)
    dl, nm, nv = _adamw(_pack([W[k] for k in small_all]), _pack([small_g[k] for k in small_all]),
                        _pack([M[k] for k in small_all]), _pack([V[k] for k in small_all]), "adam_small")
    like = [W[k] for k in small_all]
    for k, d_, m_, v_ in zip(small_all, _unpack(dl, like), _unpack(nm, like), _unpack(nv, like)):
        G[k], DELTA[k], NM[k], NV[k] = small_g[k], d_, m_, v_

    return (loss, grad_x, *[G[k] for k in WEIGHTS], *[DELTA[k] for k in WEIGHTS], *[NM[k] for k in WEIGHTS],
            *[NV[k] for k in WEIGHTS])
```

```python
import functools

import jax
import jax.numpy as jnp
from jax import lax
from jax.experimental import pallas as pl
from jax.experimental.pallas import tpu as pltpu

F32 = jnp.float32
BF16 = jnp.bfloat16
_MXU = jnp.bfloat16

D_MODEL = 2048
N_LAYERS = 2
POOL_W = 512
POOL_G = 4
POOL_GD = 128
POOL_WINDOWS = (2, 4, 8, 16)
POOL_HALO = 16
GDN_W = 768
GDN_H = 6
GDN_DH = 128
GDN_C = 64
LRU_W = 768
LRU_NB = 6
LRU_BD = 128
LRU_C = 8.0
D_FF = 6144
EPS = 1e-6
IN_COLS = 5132
HALO = 8

PQ, PK, PV, PZ, PXR, PGR, PPOOL, PAB, PCOLS = 0, 768, 1536, 2304, 3072, 3840, 4608, 5120, 5376
CB = 768

ADAM_LR = 0.001
ADAM_B1 = 0.9
ADAM_B2 = 0.999
ADAM_EPS = 1e-08
ADAM_WD = 0.01
ADAM_STEP = 10

VMEM_LIMIT = 56 * 1024 * 1024
MESH = pl.DeviceIdType.MESH
HBM = pl.BlockSpec(memory_space=pltpu.HBM)


def _cp(*sem):
    return pltpu.CompilerParams(dimension_semantics=sem, vmem_limit_bytes=VMEM_LIMIT)


def _dg(a, b, ta, tb):
    dims = (((0 if ta else 1,), (1 if tb else 0,)), ((), ()))
    return lax.dot_general(a, b, dims, preferred_element_type=F32)


def _split2(a):
    hi = a.astype(BF16)
    lo = (a - hi.astype(F32)).astype(BF16)
    return hi, lo


def _mm_raw(a, b, ta, tb, hi):
    if _MXU == F32:
        return _dg(a, b, ta, tb)
    if not hi:
        return _dg(a.astype(_MXU), b.astype(_MXU), ta, tb)
    a1, a2 = _split2(a)
    b1, b2 = _split2(b)
    return _dg(a1, b1, ta, tb) + (_dg(a1, b2, ta, tb) + _dg(a2, b1, ta, tb))


@functools.partial(jax.custom_vjp, nondiff_argnums=(2, 3, 4))
def _mm(a, b, ta=False, tb=False, hi=False):
    return _mm_raw(a, b, ta, tb, hi)


def _mm_fwd(a, b, ta, tb, hi):
    return _mm_raw(a, b, ta, tb, hi), (a, b)


def _mm_bwd(ta, tb, hi, res, dc):
    a, b = res
    da = _mm(b, dc, tb, True, hi) if ta else _mm(dc, b, False, not tb, hi)
    db = _mm(dc, a, True, ta, hi) if tb else _mm(a, dc, not ta, False, hi)
    return da, db


_mm.defvjp(_mm_fwd, _mm_bwd)


def _mm01(m01, x):
    if _MXU == F32:
        return _dg(m01, x, False, False)
    m = m01.astype(BF16)
    x1 = x.astype(BF16)
    r = x - x1.astype(F32)
    x2 = r.astype(BF16)
    x3 = (r - x2.astype(F32)).astype(BF16)
    return _dg(m, x1, False, False) + (_dg(m, x2, False, False) + _dg(m, x3, False, False))


def _down(x, k):
    return x if k == 0 else pltpu.roll(x, k, 0)


def _up(x, k):
    return x if k == 0 else pltpu.roll(x, x.shape[0] - k, 0)


def _rows(shape):
    return lax.broadcasted_iota(jnp.int32, shape, 0)


def _lanes(shape):
    return lax.broadcasted_iota(jnp.int32, shape, 1)


def _matmul(a, b, mode, *, name, res=None, tm=1024, tn=1024, tk=2048, b_split=False, o_split=0, out_dtype=F32):
    ta, tb = mode == "tn", mode == "nt"
    a_split = a.ndim == 3
    if a_split:
        assert not ta
        M, K = a.shape[1], a.shape[0] * a.shape[2]
        tk = min(tk, a.shape[2])
    elif ta:
        K, M = a.shape
    else:
        M, K = a.shape
    if b_split:
        ns = b.shape[0]
        N = b.shape[1] if tb else ns * b.shape[2]
    else:
        N = b.shape[0] if tb else b.shape[1]
    tm, tn, tk = min(tm, M), min(tn, N), min(tk, K)
    if b_split:
        per = b.shape[2]
        if tb:
            tk = min(tk, per)
        else:
            tn = min(tn, per)
    if o_split:
        tn = min(tn, N // o_split)
    assert M % tm == 0 and N % tn == 0 and K % tk == 0, (name, M, N, K, tm, tn, tk)
    nk = K // tk
    if a_split:
        ka = a.shape[2] // tk
        a_spec = pl.BlockSpec((None, tm, tk), lambda i, j, k: (k // ka, i, k % ka))
    else:
        a_spec = pl.BlockSpec((tk, tm), lambda i, j, k: (k, i)) if ta else pl.BlockSpec((tm, tk), lambda i, j, k: (i, k))
    if not b_split:
        b_spec = pl.BlockSpec((tn, tk), lambda i, j, k: (j, k)) if tb else pl.BlockSpec((tk, tn), lambda i, j, k: (k, j))
    elif tb:
        kb = per // tk
        b_spec = pl.BlockSpec((None, tn, tk), lambda i, j, k: (k // kb, j, k % kb))
    else:
        nb = per // tn
        b_spec = pl.BlockSpec((None, tk, tn), lambda i, j, k: (j // nb, k, j % nb))
    if o_split:
        ob = (N // o_split) // tn
        out_shape = jax.ShapeDtypeStruct((o_split, M, N // o_split), out_dtype)
        o_spec = pl.BlockSpec((None, tm, tn), lambda i, j, k: (j // ob, i, j % ob))
    else:
        out_shape = jax.ShapeDtypeStruct((M, N), out_dtype)
        o_spec = pl.BlockSpec((tm, tn), lambda i, j, k: (i, j))
    in_specs = [a_spec, b_spec]
    args = [a, b]
    if res is not None:
        in_specs.append(pl.BlockSpec((tm, tn), lambda i, j, k: (i, j)))
        args.append(res)
    use_acc = nk > 1 and out_dtype != F32

    def body(*refs):
        a_ref, b_ref = refs[0], refs[1]
        o_ref = refs[2 + (res is not None)]
        acc_ref = refs[-1] if use_acc else o_ref
        p = _dg(a_ref[...].astype(_MXU), b_ref[...].astype(_MXU), ta, tb)
        first = p + refs[2][...] if res is not None else p
        if nk == 1:
            o_ref[...] = first.astype(o_ref.dtype)
        else:
            k = pl.program_id(2)

            @pl.when(k == 0)
            def _():
                acc_ref[...] = first

            @pl.when(k > 0)
            def _():
                acc_ref[...] += p

            if use_acc:
                @pl.when(k == nk - 1)
                def _():
                    o_ref[...] = acc_ref[...].astype(o_ref.dtype)

    return pl.pallas_call(
        body, name=name, grid=(M // tm, N // tn, nk), in_specs=in_specs, out_specs=o_spec, out_shape=out_shape,
        scratch_shapes=[pltpu.VMEM((tm, tn), F32)] if use_acc else [],
        compiler_params=_cp("parallel", "parallel", "arbitrary"),
    )(*args)


def _rms(x, w):
    return x * lax.rsqrt(jnp.mean(x * x, axis=-1, keepdims=True) + EPS) * w


def _row_tile(S, t=512):
    t = min(t, S)
    assert S % t == 0
    return t


def _rms_fwd(x, w, name):
    S, D = x.shape
    T = _row_tile(S)

    def body(x_ref, w_ref, o_ref):
        o_ref[...] = _rms(x_ref[...], w_ref[...]).astype(o_ref.dtype)

    return pl.pallas_call(
        body, name=name, grid=(S // T,),
        in_specs=[pl.BlockSpec((T, D), lambda i: (i, 0)), pl.BlockSpec((1, D), lambda i: (0, 0))],
        out_specs=pl.BlockSpec((T, D), lambda i: (i, 0)), out_shape=jax.ShapeDtypeStruct((S, D), BF16),
        compiler_params=_cp("parallel"),
    )(x, w.reshape(1, D))


def _rms_bwd(x, w, dh, dres, name):
    S, D = x.shape
    T = _row_tile(S)

    def body(x_ref, w_ref, dh_ref, dr_ref, dx_ref, gw_ref):
        _, vjp = jax.vjp(_rms, x_ref[...], w_ref[...])
        dx, dw = vjp(dh_ref[...])
        dx_ref[...] = dr_ref[...] + dx

        @pl.when(pl.program_id(0) == 0)
        def _():
            gw_ref[...] = jnp.zeros_like(gw_ref)

        gw_ref[...] += dw

    row = pl.BlockSpec((T, D), lambda i: (i, 0))
    vec = pl.BlockSpec((1, D), lambda i: (0, 0))
    return pl.pallas_call(
        body, name=name, grid=(S // T,), in_specs=[row, vec, row, row], out_specs=[row, vec],
        out_shape=[jax.ShapeDtypeStruct((S, D), F32), jax.ShapeDtypeStruct((1, D), F32)],
        compiler_params=_cp("arbitrary"),
    )(x, w.reshape(1, D), dh, dres)


def _loss_head(x, w, tgt, name):
    S, D = x.shape
    T = _row_tile(S)

    def body(x_ref, w_ref, t_ref, l_ref, dx_ref, gw_ref):
        y, vjp = jax.vjp(_rms, x_ref[...], w_ref[...])
        err = y - t_ref[...]
        part = 0.5 * jnp.sum(jnp.mean(err * err, axis=-1, keepdims=True), axis=0, keepdims=True)
        dx, dw = vjp(err * (1.0 / D))
        dx_ref[...] = dx

        @pl.when(pl.program_id(0) == 0)
        def _():
            gw_ref[...] = jnp.zeros_like(gw_ref)
            l_ref[...] = jnp.zeros_like(l_ref)

        gw_ref[...] += dw
        l_ref[...] += jnp.broadcast_to(part, l_ref.shape)

    row = pl.BlockSpec((T, D), lambda i: (i, 0))
    vec = pl.BlockSpec((1, D), lambda i: (0, 0))
    return pl.pallas_call(
        body, name=name, grid=(S // T,), in_specs=[row, vec, row],
        out_specs=[pl.BlockSpec((8, 128), lambda i: (0, 0)), row, vec],
        out_shape=[jax.ShapeDtypeStruct((8, 128), F32), jax.ShapeDtypeStruct((S, D), F32), jax.ShapeDtypeStruct((1, D), F32)],
        compiler_params=_cp("arbitrary"),
    )(x, w.reshape(1, D), tgt)


def _by_group(shape, vals):
    g = _lanes(shape) // POOL_GD
    out = vals[-1]
    for k in range(len(vals) - 2, -1, -1):
        out = jnp.where(g == k, vals[k], out)
    return out


def _pool_d(prev, u, t0):
    ext = jnp.concatenate([prev, u], axis=0)
    s2 = ext + _down(ext, 1)
    s4 = s2 + _down(s2, 2)
    s8 = s4 + _down(s4, 4)
    s16 = s8 + _down(s8, 8)
    ssel = _by_group(ext.shape, [s2, s4, s8, s16])[POOL_HALO:]
    win = _by_group(u.shape, [jnp.int32(w) for w in POOL_WINDOWS])
    cnt = jnp.minimum(t0 + _rows(u.shape) + 1, win).astype(F32)
    return ssel / cnt - u


def _pool_lin(d, w_ref, b):
    ys = [_mm(d[:, g * POOL_GD:(g + 1) * POOL_GD], w_ref[g]) for g in range(POOL_G)]
    return jnp.concatenate(ys, axis=1) + b


def _pool_fwd(proj, w, b, scale, name):
    S = proj.shape[0]
    T = _row_tile(S)
    r = T // POOL_HALO
    cb = PPOOL // POOL_W

    def body(u_ref, up_ref, w_ref, b_ref, sc_ref, y_ref):
        i = pl.program_id(0)
        prev = jnp.where(i > 0, up_ref[...], 0.0)
        d = _pool_d(prev, u_ref[...], i * T)
        y_ref[...] = _pool_lin(d, w_ref, b_ref[...]) * sc_ref[...]

    vec = pl.BlockSpec((1, POOL_W), lambda i: (0, 0))
    return pl.pallas_call(
        body, name=name, grid=(S // T,),
        in_specs=[pl.BlockSpec((T, POOL_W), lambda i: (i, cb)),
                  pl.BlockSpec((POOL_HALO, POOL_W), lambda i: (jnp.maximum(i * r - 1, 0), cb)),
                  pl.BlockSpec((POOL_G, POOL_GD, POOL_GD), lambda i: (0, 0, 0)), vec, vec],
        out_specs=pl.BlockSpec((T, POOL_W), lambda i: (i, 0)), out_shape=jax.ShapeDtypeStruct((S, POOL_W), F32),
        compiler_params=_cp("parallel"),
    )(proj, proj, w, b.reshape(1, POOL_W), scale.reshape(1, POOL_W))


def _pool_bwd(proj, dmixed, w, b, scale, name):
    S = proj.shape[0]
    T = _row_tile(S)
    n = S // T
    r = T // POOL_HALO
    cb = PPOOL // POOL_W
    mb = 1536 // POOL_W

    def body(u_ref, up_ref, dy_ref, dyn_ref, w_ref, b_ref, sc_ref, du_ref, gw_ref, gb_ref, gs_ref):
        i = pl.program_id(0)
        sc = sc_ref[...]
        dy = dy_ref[...]
        dy_ext = jnp.concatenate([dy, jnp.where(i < n - 1, dyn_ref[...], 0.0)], axis=0)
        dyl = dy_ext * sc
        dd = jnp.concatenate(
            [_mm(dyl[:, g * POOL_GD:(g + 1) * POOL_GD], w_ref[g], False, True) for g in range(POOL_G)], axis=1)
        t_ext = i * T + _rows(dd.shape)
        win = _by_group(dd.shape, [jnp.int32(v) for v in POOL_WINDOWS])
        cnt = jnp.minimum(t_ext + 1, win).astype(F32)
        e = jnp.where(t_ext < S, dd / cnt, 0.0)
        f2 = e + _up(e, 1)
        f4 = f2 + _up(f2, 2)
        f8 = f4 + _up(f4, 4)
        f16 = f8 + _up(f8, 8)
        du = (_by_group(dd.shape, [f2, f4, f8, f16]) - dd)[:T]
        du_ref[...] = du.astype(du_ref.dtype)

        prev = jnp.where(i > 0, up_ref[...], 0.0)
        d = _pool_d(prev, u_ref[...], i * T)
        ylin = _pool_lin(d, w_ref, b_ref[...])
        dyl_m = dy * sc

        @pl.when(i == 0)
        def _():
            gw_ref[...] = jnp.zeros_like(gw_ref)
            gb_ref[...] = jnp.zeros_like(gb_ref)
            gs_ref[...] = jnp.zeros_like(gs_ref)

        gs_ref[...] += jnp.sum(dy * ylin, axis=0, keepdims=True)
        gb_ref[...] += jnp.sum(dyl_m, axis=0, keepdims=True)
        for g in range(POOL_G):
            sl = slice(g * POOL_GD, (g + 1) * POOL_GD)
            gw_ref[g] += _mm(d[:, sl], dyl_m[:, sl], True, False)

    vec = pl.BlockSpec((1, POOL_W), lambda i: (0, 0))
    wsp = pl.BlockSpec((POOL_G, POOL_GD, POOL_GD), lambda i: (0, 0, 0))
    nh = S // POOL_HALO
    return pl.pallas_call(
        body, name=name, grid=(n,),
        in_specs=[pl.BlockSpec((T, POOL_W), lambda i: (i, cb)),
                  pl.BlockSpec((POOL_HALO, POOL_W), lambda i: (jnp.maximum(i * r - 1, 0), cb)),
                  pl.BlockSpec((T, POOL_W), lambda i: (i, mb)),
                  pl.BlockSpec((POOL_HALO, POOL_W), lambda i: (jnp.minimum((i + 1) * r, nh - 1), mb)),
                  wsp, vec, vec],
        out_specs=[pl.BlockSpec((T, POOL_W), lambda i: (i, 0)), wsp, vec, vec],
        out_shape=[jax.ShapeDtypeStruct((S, POOL_W), BF16), jax.ShapeDtypeStruct((POOL_G, POOL_GD, POOL_GD), F32),
                   jax.ShapeDtypeStruct((1, POOL_W), F32), jax.ShapeDtypeStruct((1, POOL_W), F32)],
        compiler_params=_cp("arbitrary"),
    )(proj, proj, dmixed, dmixed, w, b.reshape(1, POOL_W), scale.reshape(1, POOL_W))


def _conv_rows(ext, w_ref, taps):
    acc = w_ref[taps - 1:taps, :] * ext
    for k in range(1, taps):
        acc = acc + w_ref[taps - 1 - k:taps - k, :] * _down(ext, k)
    return acc


def _conv_t_rows(dc, w_ref, taps):
    acc = w_ref[taps - 1:taps, :] * dc
    for k in range(1, taps):
        acc = acc + w_ref[taps - 1 - k:taps - k, :] * _up(dc, k)
    return acc


def _conv_specs(T, S, ncb0, with_next):
    r = T // HALO
    nh = S // HALO
    main = pl.BlockSpec((T, CB), lambda j, i: (i, j + ncb0))
    prev = pl.BlockSpec((HALO, CB), lambda j, i: (jnp.maximum(i * r - 1, 0), j + ncb0))
    nxt = pl.BlockSpec((HALO, CB), lambda j, i: (jnp.minimum((i + 1) * r, nh - 1), j + ncb0))
    return (main, prev, nxt) if with_next else (main, prev)


def _gdn_conv_fwd(proj, w, name):
    S = proj.shape[0]
    T = _row_tile(S)
    taps = w.shape[0]
    ncb = 3 * GDN_W // CB

    def body(x_ref, xp_ref, w_ref, o_ref):
        i = pl.program_id(1)
        ext = jnp.concatenate([jnp.where(i > 0, xp_ref[...], 0.0), x_ref[...]], axis=0)
        o_ref[...] = jax.nn.silu(_conv_rows(ext, w_ref, taps)[HALO:])

    main, prev = _conv_specs(T, S, PQ // CB, False)
    return pl.pallas_call(
        body, name=name, grid=(ncb, S // T),
        in_specs=[main, prev, pl.BlockSpec((taps, CB), lambda j, i: (0, j))],
        out_specs=pl.BlockSpec((T, CB), lambda j, i: (i, j)), out_shape=jax.ShapeDtypeStruct((S, 3 * GDN_W), F32),
        compiler_params=_cp("parallel", "parallel"),
    )(proj, proj, w)


def _gdn_conv_bwd(proj, dact, w, name):
    S = proj.shape[0]
    T = _row_tile(S)
    n = S // T
    taps = w.shape[0]
    ncb = 3 * GDN_W // CB

    def body(x_ref, xp_ref, xn_ref, d_ref, dn_ref, w_ref, dx_ref, gw_ref):
        i = pl.program_id(1)
        last = i == n - 1
        ext = jnp.concatenate([jnp.where(i > 0, xp_ref[...], 0.0), x_ref[...], jnp.where(last, 0.0, xn_ref[...])], axis=0)
        c = _conv_rows(ext, w_ref, taps)[HALO:]
        d_ext = jnp.concatenate([d_ref[...], jnp.where(last, 0.0, dn_ref[...])], axis=0)
        _, vjp = jax.vjp(jax.nn.silu, c)
        dc = vjp(d_ext)[0]
        dx_ref[...] = _conv_t_rows(dc, w_ref, taps)[:T].astype(dx_ref.dtype)

        @pl.when(i == 0)
        def _():
            gw_ref[...] = jnp.zeros_like(gw_ref)

        dcm = dc[:T]
        for k in range(taps):
            gw_ref[taps - 1 - k:taps - k, :] += jnp.sum(dcm * _down(ext, k)[HALO:HALO + T], axis=0, keepdims=True)

    main, prev, nxt = _conv_specs(T, S, PQ // CB, True)
    dmain, _, dnxt = _conv_specs(T, S, 0, True)
    wsp = pl.BlockSpec((taps, CB), lambda j, i: (0, j))
    return pl.pallas_call(
        body, name=name, grid=(ncb, n), in_specs=[main, prev, nxt, dmain, dnxt, wsp],
        out_specs=[pl.BlockSpec((T, CB), lambda j, i: (i, j)), wsp],
        out_shape=[jax.ShapeDtypeStruct((S, 3 * GDN_W), BF16), jax.ShapeDtypeStruct((taps, 3 * GDN_W), F32)],
        compiler_params=_cp("parallel", "arbitrary"),
    )(proj, proj, proj, dact, dact, w)


def _ffn_act_fwd(up, w, name):
    S = up.shape[0]
    T = _row_tile(S)
    taps = w.shape[0]
    ncb = D_FF // CB

    def body(g_ref, gp_ref, v_ref, w_ref, o_ref):
        i = pl.program_id(1)
        ext = jnp.concatenate([jnp.where(i > 0, gp_ref[...], 0.0), g_ref[...]], axis=0)
        c = _conv_rows(ext, w_ref, taps)[HALO:]
        o_ref[...] = (jax.nn.gelu(c) * v_ref[...]).astype(o_ref.dtype)

    main, prev = _conv_specs(T, S, 0, False)
    val = pl.BlockSpec((T, CB), lambda j, i: (i, j + ncb))
    return pl.pallas_call(
        body, name=name, grid=(ncb, S // T),
        in_specs=[main, prev, val, pl.BlockSpec((taps, CB), lambda j, i: (0, j))],
        out_specs=pl.BlockSpec((T, CB), lambda j, i: (i, j)), out_shape=jax.ShapeDtypeStruct((S, D_FF), BF16),
        compiler_params=_cp("parallel", "parallel"),
    )(up, up, up, w)


def _ffn_act_bwd(up, dact, w, name):
    S = up.shape[0]
    T = _row_tile(S)
    n = S // T
    taps = w.shape[0]
    ncb = D_FF // CB

    def body(g_ref, gp_ref, gn_ref, v_ref, vn_ref, d_ref, dn_ref, w_ref, dup_ref, gw_ref):
        i = pl.program_id(1)
        last = i == n - 1
        ext = jnp.concatenate([jnp.where(i > 0, gp_ref[...], 0.0), g_ref[...], jnp.where(last, 0.0, gn_ref[...])], axis=0)
        c = _conv_rows(ext, w_ref, taps)[HALO:]
        v_ext = jnp.concatenate([v_ref[...], jnp.where(last, 0.0, vn_ref[...])], axis=0)
        d_ext = jnp.concatenate([d_ref[...], jnp.where(last, 0.0, dn_ref[...])], axis=0)
        gl, vjp = jax.vjp(jax.nn.gelu, c)
        dup_ref[1] = (d_ext * gl)[:T].astype(dup_ref.dtype)
        dc = vjp(d_ext * v_ext)[0]
        dup_ref[0] = _conv_t_rows(dc, w_ref, taps)[:T].astype(dup_ref.dtype)

        @pl.when(i == 0)
        def _():
            gw_ref[...] = jnp.zeros_like(gw_ref)

        dcm = dc[:T]
        for k in range(taps):
            gw_ref[taps - 1 - k:taps - k, :] += jnp.sum(dcm * _down(ext, k)[HALO:HALO + T], axis=0, keepdims=True)

    main, prev, nxt = _conv_specs(T, S, 0, True)
    vmain, _, vnxt = _conv_specs(T, S, ncb, True)
    wsp = pl.BlockSpec((taps, CB), lambda j, i: (0, j))
    osp = pl.BlockSpec((2, T, CB), lambda j, i: (0, i, j))
    return pl.pallas_call(
        body, name=name, grid=(ncb, n), in_specs=[main, prev, nxt, vmain, vnxt, main, nxt, wsp],
        out_specs=[osp, wsp],
        out_shape=[jax.ShapeDtypeStruct((2, S, D_FF), BF16), jax.ShapeDtypeStruct((taps, D_FF), F32)],
        compiler_params=_cp("parallel", "arbitrary"),
    )(up, up, up, up, up, dact, dact, w)


def _tri_masks():
    r = _rows((GDN_C, GDN_C))
    c = _lanes((GDN_C, GDN_C))
    return r >= c, r > c


def _each(fn, *cols):
    return tuple(fn(*args) for args in zip(*cols))


def _tri_inv_raw(lows):
    r = _rows(lows[0].shape)
    c = _lanes(lows[0].shape)
    eye = jnp.where(r == c, 1.0, 0.0)
    ps = _each(lambda low: eye - low, lows)
    lps = lows
    for _ in range(5):
        lps = _each(lambda lp: _mm(lp, lp, False, False, True), lps)
        ps = _each(lambda p, lp: p + _mm(p, lp, False, False, True), ps, lps)
    return ps


@jax.custom_vjp
def _tri_inv(lows):
    return _tri_inv_raw(lows)


def _tri_inv_fwd(lows):
    ts = _tri_inv_raw(lows)
    return ts, ts


def _tri_inv_bwd(ts, dts):
    inner = _each(lambda t, dt: _mm(t, dt, True, False, True), ts, dts)
    return (_each(lambda m, t: -_mm(m, t, False, True, True), inner, ts),)


_tri_inv.defvjp(_tri_inv_fwd, _tri_inv_bwd)


def _gdn_glog(a_col, alog, dtb):
    return -jnp.exp(alog) * jax.nn.softplus(a_col + dtb)


def _decay_operand():
    r = _rows((GDN_C, 2 * GDN_C))
    c = _lanes((GDN_C, 2 * GDN_C))
    return jnp.where((c >= GDN_C) | (r > c), 1.0, 0.0)


def _gdn_decay(glog):
    causal, _ = _tri_masks()
    res = _mm01(jnp.where(causal, 1.0, 0.0), glog * _decay_operand())
    return res[:, GDN_C:GDN_C + 1], res[:, :GDN_C]


def _gdn_decay_bwd(dgcol, dd):
    r = _rows((GDN_C, GDN_C))
    c = _lanes((GDN_C, GDN_C))
    dres = jnp.concatenate([dd, jnp.where(c == 0, dgcol, 0.0)], axis=1)
    dx = _mm01(jnp.where(r <= c, 1.0, 0.0), dres)
    return jnp.sum(dx * _decay_operand(), axis=1, keepdims=True)


def _gdn_chunk(qa, ka, va, bt_col, gcol, dmat):
    causal, strict = _tri_masks()
    qn = _each(lambda q: q * lax.rsqrt(jnp.sum(q * q, axis=-1, keepdims=True) + EPS) * (GDN_DH ** -0.5), qa)
    kn = _each(lambda k: k * lax.rsqrt(jnp.sum(k * k, axis=-1, keepdims=True) + EPS), ka)
    beta = _each(jax.nn.sigmoid, bt_col)
    eg = _each(jnp.exp, gcol)
    decay = _each(lambda d: jnp.where(causal, jnp.exp(d), 0.0), dmat)
    kk = _each(lambda k: _mm(k, k, False, True), kn)
    low = _each(lambda b, m, d: jnp.where(strict, b * m * d, 0.0), beta, kk, decay)
    t = _tri_inv(low)
    w = _each(lambda t_, k, b, e: _mm(t_, k * (b * e), False, False, True), t, kn, beta, eg)
    u = _each(lambda t_, v, b: _mm(t_, v * b, False, False, True), t, va, beta)
    attn = _each(lambda q, k, d: _mm(q, k, False, True) * d, qn, kn, decay)
    last = _rows(gcol[0].shape) == GDN_C - 1
    g_last = _each(lambda g: jnp.sum(jnp.where(last, g, 0.0), axis=0, keepdims=True), gcol)
    qd = _each(lambda q, e: q * e, qn, eg)
    kd = _each(lambda k, gl, g: k * jnp.exp(gl - g), kn, g_last, gcol)
    return w, u, qd, kd, attn


def _gdn_step(state, w, u, qd, kd, attn, egl):
    v_new = _each(lambda u_, w_, s: u_ - _mm(w_, s), u, w, state)
    o_state = _each(_mm, qd, state)
    o = _each(lambda os, a, v: os + _mm(a, v), o_state, attn, v_new)
    new = _each(lambda s, e, k, v: s * e + _mm(k, v, True, False), state, egl, kd, v_new)
    return o, new


def _heads(ref, base=0, width=GDN_DH):
    return tuple(ref[:, (base + h) * GDN_DH:(base + h) * GDN_DH + width] for h in range(GDN_H))


def _cols(a, base):
    return tuple(a[:, base + h:base + h + 1] for h in range(GDN_H))


def _gated_norm(o, z, nw):
    return o * lax.rsqrt(jnp.mean(o * o, axis=-1, keepdims=True) + EPS) * nw * jax.nn.silu(z)


def _hsl(h):
    return slice(h * GDN_DH, (h + 1) * GDN_DH)


def _pad_lanes(a, width=GDN_DH):
    return jnp.concatenate([a, jnp.zeros((a.shape[0], width - a.shape[1]), a.dtype)], axis=1)


def _gdn_prep(qkv, proj, alog, dtb, name):
    S = qkv.shape[0]
    N = S // GDN_C

    def body(qkv_ref, ab_ref, al_ref, dt_ref, w_ref, u_ref, qd_ref, kd_ref, at_ref, gc_ref):
        ab = ab_ref[...]
        glog = _each(_gdn_glog, _cols(ab, 0), _cols(al_ref[...], 0), _cols(dt_ref[...], 0))
        dec = _each(_gdn_decay, glog)
        gcol, dmat = _each(lambda d: d[0], dec), _each(lambda d: d[1], dec)
        w, u, qd, kd, attn = _gdn_chunk(_heads(qkv_ref), _heads(qkv_ref, GDN_H), _heads(qkv_ref, 2 * GDN_H),
                                        _cols(ab, GDN_H), gcol, dmat)
        gc = jnp.zeros((GDN_C, 128), F32)
        for h in range(GDN_H):
            w_ref[:, _hsl(h)] = w[h]
            u_ref[:, _hsl(h)] = u[h]
            qd_ref[:, _hsl(h)] = qd[h]
            kd_ref[:, _hsl(h)] = kd[h]
            at_ref[:, _hsl(h)] = _pad_lanes(attn[h])
            gc = jnp.where(_lanes(gc.shape) == h, gcol[h], gc)
        gc_ref[...] = gc

    vec = pl.BlockSpec((1, 128), lambda i: (0, 0))
    hsp = pl.BlockSpec((GDN_C, GDN_W), lambda i: (i, 0))
    hshape = jax.ShapeDtypeStruct((S, GDN_W), F32)
    return pl.pallas_call(
        body, name=name, grid=(N,),
        in_specs=[pl.BlockSpec((GDN_C, 3 * GDN_W), lambda i: (i, 0)), pl.BlockSpec((GDN_C, 128), lambda i: (i, PAB // 128)), vec, vec],
        out_specs=[hsp] * 5 + [pl.BlockSpec((GDN_C, 128), lambda i: (i, 0))],
        out_shape=[hshape] * 5 + [jax.ShapeDtypeStruct((S, 128), F32)],
        compiler_params=_cp("parallel"),
    )(qkv, proj, alog, dtb)


def _gdn_scan(w, u, qd, kd, attn, gc, name):
    S = w.shape[0]
    N = S // GDN_C

    def body(w_ref, u_ref, qd_ref, kd_ref, at_ref, gc_ref, o_ref, st_ref, s_ref):
        @pl.when(pl.program_id(0) == 0)
        def _():
            s_ref[...] = jnp.zeros_like(s_ref)

        state = tuple(s_ref[_hsl(h), :] for h in range(GDN_H))
        egl = _each(jnp.exp, _cols(gc_ref[GDN_C - 1:GDN_C, :], 0))
        o, new = _gdn_step(state, _heads(w_ref), _heads(u_ref), _heads(qd_ref), _heads(kd_ref),
                           _heads(at_ref, width=GDN_C), egl)
        for h in range(GDN_H):
            st_ref[_hsl(h), :] = state[h]
            o_ref[:, _hsl(h)] = o[h]
            s_ref[_hsl(h), :] = new[h]

    hsp = pl.BlockSpec((GDN_C, GDN_W), lambda i: (i, 0))
    return pl.pallas_call(
        body, name=name, grid=(N,),
        in_specs=[hsp] * 5 + [pl.BlockSpec((GDN_C, 128), lambda i: (i, 0))],
        out_specs=[hsp, pl.BlockSpec((None, GDN_W, GDN_DH), lambda i: (i, 0, 0))],
        out_shape=[jax.ShapeDtypeStruct((S, GDN_W), F32), jax.ShapeDtypeStruct((N, GDN_W, GDN_DH), F32)],
        scratch_shapes=[pltpu.VMEM((GDN_W, GDN_DH), F32)],
        compiler_params=_cp("arbitrary"),
    )(w, u, qd, kd, attn, gc)


def _gdn_scan_bwd(w, u, qd, kd, attn, gc, states, o, proj, dmixed, nw, name):
    S = w.shape[0]
    N = S // GDN_C

    def body(w_ref, u_ref, qd_ref, kd_ref, at_ref, gc_ref, st_ref, o_ref, z_ref, dm_ref, nw_ref,
             dw_ref, du_ref, dqd_ref, dkd_ref, dat_ref, dgl_ref, dz_ref, gnw_ref, ds_ref):
        @pl.when(pl.program_id(0) == 0)
        def _():
            ds_ref[...] = jnp.zeros_like(ds_ref)
            gnw_ref[...] = jnp.zeros_like(gnw_ref)

        nw = nw_ref[...]
        _, vjp_n = jax.vjp(lambda o, z, w_: _each(lambda a, b: _gated_norm(a, b, w_), o, z), _heads(o_ref), _heads(z_ref), nw)
        do, dz, dnw = vjp_n(_heads(dm_ref))
        state = tuple(st_ref[_hsl(h), :] for h in range(GDN_H))
        egl = _each(jnp.exp, _cols(gc_ref[GDN_C - 1:GDN_C, :], 0))
        _, vjp_s = jax.vjp(_gdn_step, state, _heads(w_ref), _heads(u_ref), _heads(qd_ref), _heads(kd_ref),
                           _heads(at_ref, width=GDN_C), egl)
        ds, dw, du, dqd, dkd, dat, degl = vjp_s((do, tuple(ds_ref[_hsl(h), :] for h in range(GDN_H))))
        dgl = jnp.zeros((8, 128), F32)
        for h in range(GDN_H):
            dz_ref[:, _hsl(h)] = dz[h].astype(dz_ref.dtype)
            ds_ref[_hsl(h), :] = ds[h]
            dw_ref[:, _hsl(h)] = dw[h]
            du_ref[:, _hsl(h)] = du[h]
            dqd_ref[:, _hsl(h)] = dqd[h]
            dkd_ref[:, _hsl(h)] = dkd[h]
            dat_ref[:, _hsl(h)] = _pad_lanes(dat[h])
            dgl = jnp.where(_lanes(dgl.shape) == h, degl[h] * egl[h], dgl)
        dgl_ref[...] = dgl
        gnw_ref[...] += dnw

    rev = lambda i: (N - 1 - i, 0)
    hsp = pl.BlockSpec((GDN_C, GDN_W), rev)
    gsp = pl.BlockSpec((GDN_C, 128), rev)
    vec = pl.BlockSpec((1, GDN_DH), lambda i: (0, 0))
    hshape = jax.ShapeDtypeStruct((S, GDN_W), F32)
    return pl.pallas_call(
        body, name=name, grid=(N,),
        in_specs=[hsp] * 5 + [gsp, pl.BlockSpec((None, GDN_W, GDN_DH), lambda i: (N - 1 - i, 0, 0)), hsp,
                              pl.BlockSpec((GDN_C, GDN_W), lambda i: (N - 1 - i, PZ // GDN_W)), hsp, vec],
        out_specs=[hsp] * 5 + [pl.BlockSpec((8, 128), rev), hsp, vec],
        out_shape=[hshape] * 5 + [jax.ShapeDtypeStruct((N * 8, 128), F32), jax.ShapeDtypeStruct((S, GDN_W), BF16),
                                  jax.ShapeDtypeStruct((1, GDN_DH), F32)],
        scratch_shapes=[pltpu.VMEM((GDN_W, GDN_DH), F32)],
        compiler_params=_cp("arbitrary"),
    )(w, u, qd, kd, attn, gc, states, o, proj, dmixed, nw)


def _gdn_prep_bwd(qkv, proj, alog, dtb, dw, du, dqd, dkd, dat, dgl, name):
    S = qkv.shape[0]
    N = S // GDN_C

    def body(qkv_ref, ab_ref, al_ref, dt_ref, dw_ref, du_ref, dqd_ref, dkd_ref, dat_ref, dgl_ref,
             dqkv_ref, dab_ref, gal_ref, gdt_ref):
        @pl.when(pl.program_id(0) == 0)
        def _():
            gal_ref[...] = jnp.zeros_like(gal_ref)
            gdt_ref[...] = jnp.zeros_like(gdt_ref)

        ab = ab_ref[...]
        glog, vjp_g = jax.vjp(lambda a, al, dt: _each(_gdn_glog, a, al, dt), _cols(ab, 0), _cols(al_ref[...], 0),
                              _cols(dt_ref[...], 0))
        dec = _each(_gdn_decay, glog)
        gcol, dmat = _each(lambda d: d[0], dec), _each(lambda d: d[1], dec)
        _, vjp_c = jax.vjp(_gdn_chunk, _heads(qkv_ref), _heads(qkv_ref, GDN_H), _heads(qkv_ref, 2 * GDN_H),
                           _cols(ab, GDN_H), gcol, dmat)
        dqa, dka, dva, dbt, dgcol, dd = vjp_c((_heads(dw_ref), _heads(du_ref), _heads(dqd_ref), _heads(dkd_ref),
                                               _heads(dat_ref, width=GDN_C)))
        last = _rows(dgcol[0].shape) == GDN_C - 1
        dgcol = _each(lambda d, g: d + jnp.where(last, g, 0.0), dgcol, _cols(dgl_ref[0:1, :], 0))
        da_col, dal, ddt = vjp_g(_each(_gdn_decay_bwd, dgcol, dd))
        dab = jnp.zeros((GDN_C, 128), F32)
        gal = jnp.zeros((1, 128), F32)
        gdt = jnp.zeros((1, 128), F32)
        for h in range(GDN_H):
            dqkv_ref[:, _hsl(h)] = dqa[h]
            dqkv_ref[:, _hsl(GDN_H + h)] = dka[h]
            dqkv_ref[:, _hsl(2 * GDN_H + h)] = dva[h]
            ln = _lanes(dab.shape)
            dab = dab + jnp.where(ln == h, da_col[h], 0.0) + jnp.where(ln == GDN_H + h, dbt[h], 0.0)
            l1 = _lanes(gal.shape)
            gal = gal + jnp.where(l1 == h, dal[h], 0.0)
            gdt = gdt + jnp.where(l1 == h, ddt[h], 0.0)
        dab_ref[...] = dab.astype(dab_ref.dtype)
        gal_ref[...] += gal
        gdt_ref[...] += gdt

    vec = pl.BlockSpec((1, 128), lambda i: (0, 0))
    hsp = pl.BlockSpec((GDN_C, GDN_W), lambda i: (i, 0))
    qsp = pl.BlockSpec((GDN_C, 3 * GDN_W), lambda i: (i, 0))
    return pl.pallas_call(
        body, name=name, grid=(N,),
        in_specs=[qsp, pl.BlockSpec((GDN_C, 128), lambda i: (i, PAB // 128)), vec, vec] + [hsp] * 5
        + [pl.BlockSpec((8, 128), lambda i: (i, 0))],
        out_specs=[qsp, pl.BlockSpec((GDN_C, 128), lambda i: (i, 0)), vec, vec],
        out_shape=[jax.ShapeDtypeStruct((S, 3 * GDN_W), F32), jax.ShapeDtypeStruct((S, 128), BF16),
                   jax.ShapeDtypeStruct((1, 128), F32), jax.ShapeDtypeStruct((1, 128), F32)],
        compiler_params=_cp("arbitrary"),
    )(qkv, proj, alog, dtb, dw, du, dqd, dkd, dat, dgl)


@jax.custom_vjp
def _expm1(x):
    u = jnp.exp(x)
    lu = jnp.log(u)
    small = (u - 1.0) * x / jnp.where(u == 1.0, 1.0, lu)
    small = jnp.where(u == 1.0, x, small)
    return jnp.where(jnp.abs(x) < 0.5, small, u - 1.0)


def _expm1_fwd(x):
    return _expm1(x), jnp.exp(x)


def _expm1_bwd(ex, g):
    return (g * ex,)


_expm1.defvjp(_expm1_fwd, _expm1_bwd)


def _lru_gates(xc, wa, ba, wx, bx, lam, first):
    r = jax.nn.sigmoid(_mm(xc, wa) + ba)
    i = jax.nn.sigmoid(_mm(xc, wx) + bx)
    log_a = -LRU_C * r * jax.nn.softplus(-lam)
    mult = jnp.sqrt(-_expm1(2.0 * log_a))
    mult = jnp.where(first, 1.0, mult)
    return jnp.exp(log_a), mult * i * xc


def _scan_fwd(a, b):
    T = a.shape[0]
    rows = _rows(a.shape)
    s = 1
    while s < T:
        ok = rows >= s
        b = a * jnp.where(ok, _down(b, s), 0.0) + b
        a = a * jnp.where(ok, _down(a, s), 1.0)
        s *= 2
    return a, b


def _scan_rev(a, b):
    T = a.shape[0]
    rows = _rows(a.shape)
    s = 1
    while s < T:
        ok = rows + s < T
        b = a * jnp.where(ok, _up(b, s), 0.0) + b
        a = a * jnp.where(ok, _up(a, s), 1.0)
        s *= 2
    return b


def _bsl(j):
    return slice(j * LRU_BD, (j + 1) * LRU_BD)


def _lru_tile(S):
    return _row_tile(S, 256)


def _lru_fwd(proj, conv_w, conv_b, wa, ba, wx, bx, lam, name):
    S = proj.shape[0]
    T = _lru_tile(S)
    taps = conv_w.shape[0]
    r = T // HALO

    def body(x_ref, xp_ref, cw_ref, cb_ref, wa_ref, ba_ref, wx_ref, bx_ref, lam_ref, h_ref, carry_ref):
        i = pl.program_id(0)

        @pl.when(i == 0)
        def _():
            carry_ref[...] = jnp.zeros_like(carry_ref)

        ext = jnp.concatenate([jnp.where(i > 0, xp_ref[...], 0.0), x_ref[...]], axis=0)
        xc = _conv_rows(ext, cw_ref, taps)[HALO:] + cb_ref[...]
        first = (i * T + _rows((T, LRU_BD))) == 0
        for j in range(LRU_NB):
            a, b = _lru_gates(xc[:, _bsl(j)], wa_ref[j], ba_ref[:, _bsl(j)], wx_ref[j], bx_ref[:, _bsl(j)],
                              lam_ref[:, _bsl(j)], first=first)
            pa, hb = _scan_fwd(a, b)
            h_ref[:, _bsl(j)] = pa * carry_ref[0:1, _bsl(j)] + hb
            carry_ref[0:1, _bsl(j)] = h_ref[T - 1:T, _bsl(j)]

    vec = pl.BlockSpec((1, LRU_W), lambda i: (0, 0))
    wsp = pl.BlockSpec((LRU_NB, LRU_BD, LRU_BD), lambda i: (0, 0, 0))
    return pl.pallas_call(
        body, name=name, grid=(S // T,),
        in_specs=[pl.BlockSpec((T, LRU_W), lambda i: (i, PXR // LRU_W)),
                  pl.BlockSpec((HALO, LRU_W), lambda i: (jnp.maximum(i * r - 1, 0), PXR // LRU_W)),
                  pl.BlockSpec((taps, LRU_W), lambda i: (0, 0)), vec, wsp, vec, wsp, vec, vec],
        out_specs=pl.BlockSpec((T, LRU_W), lambda i: (i, 0)), out_shape=jax.ShapeDtypeStruct((S, LRU_W), F32),
        scratch_shapes=[pltpu.VMEM((8, LRU_W), F32)],
        compiler_params=_cp("arbitrary"),
    )(proj, proj, conv_w, conv_b.reshape(1, LRU_W), wa, ba.reshape(1, LRU_W), wx, bx.reshape(1, LRU_W), lam.reshape(1, LRU_W))


def _lru_bwd(proj, hl, dmixed, conv_w, conv_b, wa, ba, wx, bx, lam, name):
    S = proj.shape[0]
    T = _lru_tile(S)
    n = S // T
    taps = conv_w.shape[0]
    r = T // HALO
    mb = 768 // LRU_W

    def body(x_ref, xp_ref, g_ref, h_ref, hp_ref, dy_ref, cw_ref, cb_ref, wa_ref, ba_ref, wx_ref, bx_ref, lam_ref,
             dx_ref, dg_ref, gcw_ref, gcb_ref, gwa_ref, gba_ref, gwx_ref, gbx_ref, glam_ref, carry_ref, dxc_ref, nxt_ref):
        s = pl.program_id(0)
        i = n - 1 - s

        @pl.when(s == 0)
        def _():
            carry_ref[...] = jnp.zeros_like(carry_ref)
            nxt_ref[...] = jnp.zeros_like(nxt_ref)
            for ref in (gcw_ref, gcb_ref, gwa_ref, gba_ref, gwx_ref, gbx_ref, glam_ref):
                ref[...] = jnp.zeros_like(ref)

        ext = jnp.concatenate([jnp.where(i > 0, xp_ref[...], 0.0), x_ref[...]], axis=0)
        xc = _conv_rows(ext, cw_ref, taps)[HALO:] + cb_ref[...]
        rows = _rows((T, LRU_BD))
        first = (i * T + rows) == 0
        h_before = jnp.where(i > 0, hp_ref[HALO - 1:HALO, :], 0.0)
        for j in range(LRU_NB):
            sl = _bsl(j)
            (a, _), vjp_g = jax.vjp(functools.partial(_lru_gates, first=first), xc[:, sl], wa_ref[j], ba_ref[:, sl],
                                    wx_ref[j], bx_ref[:, sl], lam_ref[:, sl])
            gelu_g, vjp_a = jax.vjp(jax.nn.gelu, g_ref[:, sl])
            h = h_ref[:, sl]
            dy = dy_ref[:, sl]
            dg_ref[:, sl] = vjp_a(dy * h)[0].astype(dg_ref.dtype)
            b_rev = dy * gelu_g + jnp.where(rows == T - 1, carry_ref[0:1, sl], 0.0)
            a_rev = jnp.where(rows == T - 1, 0.0, _up(a, 1))
            dh = _scan_rev(a_rev, b_rev)
            carry_ref[:, sl] = (a * dh)[:HALO]
            h_prev = jnp.where(rows == 0, h_before[:, sl], _down(h, 1))
            dxc, dwa, dba, dwx, dbx, dlam = vjp_g((dh * h_prev, dh))
            dxc_ref[:, sl] = dxc
            gwa_ref[j] += dwa
            gwx_ref[j] += dwx
            gba_ref[:, sl] += dba
            gbx_ref[:, sl] += dbx
            glam_ref[:, sl] += dlam
        dxc = dxc_ref[...]
        d_ext = jnp.concatenate([dxc, nxt_ref[...]], axis=0)
        dx_ref[...] = _conv_t_rows(d_ext, cw_ref, taps)[:T].astype(dx_ref.dtype)
        nxt_ref[...] = dxc[:HALO]
        gcb_ref[...] += jnp.sum(dxc, axis=0, keepdims=True)
        for k in range(taps):
            gcw_ref[taps - 1 - k:taps - k, :] += jnp.sum(dxc * _down(ext, k)[HALO:], axis=0, keepdims=True)

    vec = pl.BlockSpec((1, LRU_W), lambda s: (0, 0))
    wsp = pl.BlockSpec((LRU_NB, LRU_BD, LRU_BD), lambda s: (0, 0, 0))
    cwsp = pl.BlockSpec((taps, LRU_W), lambda s: (0, 0))

    def main(cb):
        return pl.BlockSpec((T, LRU_W), lambda s: (n - 1 - s, cb))

    def prev(cb):
        return pl.BlockSpec((HALO, LRU_W), lambda s: (jnp.maximum((n - 1 - s) * r - 1, 0), cb))

    vshape = jax.ShapeDtypeStruct((1, LRU_W), F32)
    wshape = jax.ShapeDtypeStruct((LRU_NB, LRU_BD, LRU_BD), F32)
    return pl.pallas_call(
        body, name=name, grid=(n,),
        in_specs=[main(PXR // LRU_W), prev(PXR // LRU_W), main(PGR // LRU_W), main(0), prev(0), main(mb),
                  cwsp, vec, wsp, vec, wsp, vec, vec],
        out_specs=[main(0), main(0), cwsp, vec, wsp, vec, wsp, vec, vec],
        out_shape=[jax.ShapeDtypeStruct((S, LRU_W), BF16), jax.ShapeDtypeStruct((S, LRU_W), BF16),
                   jax.ShapeDtypeStruct((taps, LRU_W), F32), vshape, wshape, vshape, wshape, vshape, vshape],
        scratch_shapes=[pltpu.VMEM((8, LRU_W), F32), pltpu.VMEM((T, LRU_W), F32), pltpu.VMEM((HALO, LRU_W), F32)],
        compiler_params=_cp("arbitrary"),
    )(proj, proj, proj, hl, hl, dmixed, conv_w, conv_b.reshape(1, LRU_W), wa, ba.reshape(1, LRU_W), wx,
      bx.reshape(1, LRU_W), lam.reshape(1, LRU_W))


def _mix_out(o, proj, hl, y_pool, nw, name):
    S = o.shape[0]
    T = _row_tile(S)

    def body(o_ref, z_ref, h_ref, g_ref, p_ref, nw_ref, m_ref):
        for h in range(GDN_H):
            m_ref[:, _hsl(h)] = _gated_norm(o_ref[:, _hsl(h)], z_ref[:, _hsl(h)], nw_ref[...]).astype(m_ref.dtype)
        m_ref[:, GDN_W:GDN_W + LRU_W] = (h_ref[...] * jax.nn.gelu(g_ref[...])).astype(m_ref.dtype)
        m_ref[:, GDN_W + LRU_W:] = p_ref[...].astype(m_ref.dtype)

    row = pl.BlockSpec((T, GDN_W), lambda i: (i, 0))
    return pl.pallas_call(
        body, name=name, grid=(S // T,),
        in_specs=[row, pl.BlockSpec((T, GDN_W), lambda i: (i, PZ // GDN_W)), row,
                  pl.BlockSpec((T, LRU_W), lambda i: (i, PGR // LRU_W)), pl.BlockSpec((T, POOL_W), lambda i: (i, 0)),
                  pl.BlockSpec((1, GDN_DH), lambda i: (0, 0))],
        out_specs=pl.BlockSpec((T, D_MODEL), lambda i: (i, 0)), out_shape=jax.ShapeDtypeStruct((S, D_MODEL), BF16),
        compiler_params=_cp("parallel"),
    )(o, proj, hl, proj, y_pool, nw)


def _as2d(a):
    return a.reshape(-1, a.shape[-1])


def _ew_rows(rows, cols):
    t = rows
    while t * cols * 4 > (2 << 20) and t % 16 == 0:
        t //= 2
    return t


def _rs_rows(rows, cols):
    t = rows
    while t * cols * 4 > (2 << 20) and t % 32 == 0:
        t //= 2
    return t


def _rs_add(src, recv, src_index, grid_lead, out_dtype, name, into=None):
    rows, cols = recv.shape[-2:]
    t = _rs_rows(rows, cols)
    nl = len(grid_lead)
    lead_none = (None,) * (src.ndim - 2)

    def body(p_ref, s_ref, r_ref, *rest):
        o_ref = rest[-1]
        o_ref[...] = (s_ref[...].astype(F32) + r_ref[...].astype(F32)).astype(o_ref.dtype)

    def src_map(*a):
        return (*src_index(*a[:nl], a[-1]), a[nl], 0)

    def own_map(*a):
        return (*a[:nl], a[nl], 0)

    x, y, c = _place()
    place = jnp.stack([x, y, c]).astype(jnp.int32)
    own = pl.BlockSpec(((None,) * nl) + (t, cols), own_map)
    in_specs = [pl.BlockSpec((*lead_none, t, cols), src_map), own]
    args = [place, src, recv]
    if into is None:
        out_spec, out_shape, aliases = own, jax.ShapeDtypeStruct(recv.shape, out_dtype), {}
    else:
        buf, layer = into
        shape = buf.shape
        args.append(buf.reshape((2, 2) + recv.shape))
        in_specs.append(pl.BlockSpec(memory_space=pl.ANY))
        out_spec = pl.BlockSpec(((None,) * (nl + 2)) + (t, cols), lambda *a: (layer, a[-1][2], *a[:nl], a[nl], 0))
        out_shape, aliases = jax.ShapeDtypeStruct((2, 2) + recv.shape, out_dtype), {3: 0}
    out = pl.pallas_call(
        body, name=name,
        grid_spec=pltpu.PrefetchScalarGridSpec(
            num_scalar_prefetch=1, grid=(*grid_lead, rows // t), in_specs=in_specs, out_specs=out_spec),
        out_shape=out_shape, input_output_aliases=aliases,
        compiler_params=pltpu.CompilerParams(dimension_semantics=("parallel",) * (nl + 1), vmem_limit_bytes=VMEM_LIMIT),
    )(*args)
    return out if into is None else out.reshape(shape)


def _add(a, b, name):
    shape = a.shape
    a2, b2 = _as2d(a), _as2d(b)
    rows, cols = a2.shape
    t = _ew_rows(rows, cols)

    def body(a_ref, b_ref, o_ref):
        o_ref[...] = a_ref[...] + b_ref[...]

    sp = pl.BlockSpec((t, cols), lambda i: (i, 0))
    return pl.pallas_call(body, name=name, grid=(rows // t,), in_specs=[sp, sp], out_specs=sp,
                          out_shape=jax.ShapeDtypeStruct((rows, cols), F32), compiler_params=_cp("parallel"))(a2, b2).reshape(shape)


def _adamw(w, g, m, v, name):
    shape = w.shape
    w2, g2, m2, v2 = _as2d(w), _as2d(g), _as2d(m), _as2d(v)
    rows, cols = w2.shape
    t = _ew_rows(rows, cols)

    def body(w_ref, g_ref, m_ref, v_ref, d_ref, nm_ref, nv_ref):
        gr = g_ref[...]
        nm = ADAM_B1 * m_ref[...] + (1.0 - ADAM_B1) * gr
        nv = ADAM_B2 * v_ref[...] + (1.0 - ADAM_B2) * (gr * gr)
        m_hat = nm / (1.0 - ADAM_B1 ** ADAM_STEP)
        v_hat = nv / (1.0 - ADAM_B2 ** ADAM_STEP)
        d_ref[...] = -ADAM_LR * (m_hat / (jnp.sqrt(v_hat) + ADAM_EPS) + ADAM_WD * w_ref[...])
        nm_ref[...] = nm
        nv_ref[...] = nv

    sp = pl.BlockSpec((t, cols), lambda i: (i, 0))
    sh = jax.ShapeDtypeStruct((rows, cols), F32)
    d, nm, nv = pl.pallas_call(body, name=name, grid=(rows // t,), in_specs=[sp] * 4, out_specs=[sp] * 3,
                               out_shape=[sh] * 3, compiler_params=_cp("parallel"))(w2, g2, m2, v2)
    return d.reshape(shape), nm.reshape(shape), nv.reshape(shape)


def _place():
    return lax.axis_index("x"), lax.axis_index("y"), lax.axis_index("c")


def _gather_weights(arrs, name):
    n = len(arrs)

    def body(*refs):
        outs = refs[n:2 * n]
        send, recv = refs[2 * n:]
        x, y, c = _place()
        s_me, s_x, s_y, s_d = 2 * x + y, 2 * (1 - x) + y, 2 * x + (1 - y), 2 * (1 - x) + (1 - y)
        xpeer, ypeer, sib = (1 - x, y, c), (x, 1 - y, c), (x, y, 1 - c)

        def rc(k, t, src, dst, to):
            return pltpu.make_async_remote_copy(src_ref=src, dst_ref=dst, send_sem=send.at[k, t], recv_sem=recv.at[k, t],
                                                device_id=to, device_id_type=MESH)

        def half(k, s, top):
            rh = outs[k].shape[2] // 2
            return outs[k].at[c, s, pl.ds(0 if top else rh, rh)]

        sent = []

        def start(k, t, ref, to):
            cp = rc(k, t, ref, ref, to)
            cp.start()
            sent.append(cp)

        for k in range(n):
            start(k, 0, outs[k].at[c, s_me], xpeer)
            start(k, 1, outs[k].at[c, s_me], ypeer)
        for k in range(n):
            got = outs[k].at[c, s_x]
            rc(k, 0, got, got, xpeer).wait_recv()
            start(k, 2, half(k, s_x, True), ypeer)
            start(k, 3, got, sib)
        for k in range(n):
            got = outs[k].at[c, s_y]
            rc(k, 1, got, got, ypeer).wait_recv()
            start(k, 6, half(k, s_y, False), xpeer)
            start(k, 4, got, sib)
        for k in range(n):
            top, bottom = half(k, s_d, True), half(k, s_d, False)
            rc(k, 2, top, top, ypeer).wait_recv()
            rc(k, 6, bottom, bottom, xpeer).wait_recv()
            start(k, 5, outs[k].at[c, s_d], sib)
        for k in range(n):
            for t, s in ((3, s_x), (4, s_y), (5, s_d)):
                got = outs[k].at[1 - c, s]
                rc(k, t, got, got, sib).wait_recv()
        for cp in sent:
            cp.wait_send()

    return pl.pallas_call(
        body, name=name, in_specs=[HBM] * n, out_specs=[HBM] * n,
        out_shape=[jax.ShapeDtypeStruct(a.shape, a.dtype) for a in arrs],
        input_output_aliases={k: k for k in range(n)},
        scratch_shapes=[pltpu.SemaphoreType.DMA((n, 7)), pltpu.SemaphoreType.DMA((n, 7))],
    )(*arrs)


def _exchange(arrs, axis, name, half=0):
    n = len(arrs)

    def body(*refs):
        srcs, outs = refs[:n], refs[n:2 * n]
        send, recv = refs[2 * n:]
        x, y, c = _place()
        p = {"x": x, "y": y, "c": c}[axis]
        peer = {"x": (1 - x, y, c), "y": (x, 1 - y, c), "c": (x, y, 1 - c)}[axis]
        cps = []
        for k in range(n):
            src = (srcs[k], srcs[k].at[1 - p], srcs[k].at[:, 1 - p])[half]
            cp = pltpu.make_async_remote_copy(src_ref=src, dst_ref=outs[k], send_sem=send.at[k], recv_sem=recv.at[k],
                                              device_id=peer, device_id_type=MESH)
            cp.start()
            cps.append(cp)
        for cp in cps:
            cp.wait()

    def out_shape(a):
        return (a.shape, a.shape[1:], a.shape[:1] + a.shape[2:])[half]

    return pl.pallas_call(
        body, name=name, in_specs=[HBM] * n, out_specs=[HBM] * n,
        out_shape=[jax.ShapeDtypeStruct(out_shape(a), a.dtype) for a in arrs],
        scratch_shapes=[pltpu.SemaphoreType.DMA((n,)), pltpu.SemaphoreType.DMA((n,))],
    )(*arrs)


def _share_halves(arrs, name):
    n = len(arrs)

    def body(*refs):
        outs = refs[n:2 * n]
        send, recv = refs[2 * n:]
        x, y, c = _place()
        cps = []
        for k in range(n):
            mine = outs[k].at[:, c]
            cp = pltpu.make_async_remote_copy(src_ref=mine, dst_ref=mine, send_sem=send.at[k], recv_sem=recv.at[k],
                                              device_id=(x, y, 1 - c), device_id_type=MESH)
            cp.start()
            cps.append(cp)
        for k in range(n):
            got = outs[k].at[:, 1 - c]
            pltpu.make_async_remote_copy(src_ref=got, dst_ref=got, send_sem=send.at[k], recv_sem=recv.at[k],
                                         device_id=(x, y, 1 - c), device_id_type=MESH).wait_recv()
        for cp in cps:
            cp.wait_send()

    return pl.pallas_call(
        body, name=name, in_specs=[HBM] * n, out_specs=[HBM] * n,
        out_shape=[jax.ShapeDtypeStruct(a.shape, a.dtype) for a in arrs],
        input_output_aliases={k: k for k in range(n)},
        scratch_shapes=[pltpu.SemaphoreType.DMA((n,)), pltpu.SemaphoreType.DMA((n,))],
    )(*arrs)


def _rs_exchange(arrs, phase, name):
    n = len(arrs)

    def body(*refs):
        srcs, outs = refs[:n], refs[n:2 * n]
        send, recv = refs[2 * n:]
        x, y, c = _place()
        xpeer, ypeer = (1 - x, y, c), (x, 1 - y, c)
        cps = []
        for k in range(n):
            if phase == 2:
                parts = ((srcs[k].at[1 - x, :, 0], xpeer), (srcs[k].at[:, 1 - y, 1], ypeer))
            else:
                parts = ((srcs[k].at[0, 1 - y], ypeer), (srcs[k].at[1, 1 - x], xpeer))
            for h, (src, to) in enumerate(parts):
                cp = pltpu.make_async_remote_copy(src_ref=src, dst_ref=outs[k].at[h], send_sem=send.at[k, h],
                                                  recv_sem=recv.at[k, h], device_id=to, device_id_type=MESH)
                cp.start()
                cps.append(cp)
        for cp in cps:
            cp.wait()

    def out_shape(a):
        return (2, 2) + a.shape[3:] if phase == 2 else (2,) + a.shape[2:]

    return pl.pallas_call(
        body, name=name, in_specs=[HBM] * n, out_specs=[HBM] * n,
        out_shape=[jax.ShapeDtypeStruct(out_shape(a), a.dtype) for a in arrs],
        scratch_shapes=[pltpu.SemaphoreType.DMA((n, 2)), pltpu.SemaphoreType.DMA((n, 2))],
    )(*arrs)


_REL = tuple((dx, dy, dc) for dx in (0, 1) for dy in (0, 1) for dc in (0, 1))[1:]
SEM = pl.BlockSpec(memory_space=pltpu.SEMAPHORE)
DATAFLOW = pltpu.SideEffectType.DATAFLOW_SIDE_EFFECTING


def _flip(v, d):
    return 1 - v if d else v


def _rs_direct_copies(srcs, land, send, recv):
    x, y, c = _place()
    cps = []
    for k in range(len(srcs)):
        for r, (dx, dy, dc) in enumerate(_REL):
            px, py, pc = _flip(x, dx), _flip(y, dy), _flip(c, dc)
            cps.append(pltpu.make_async_remote_copy(
                src_ref=srcs[k].at[2 * px + py, pc], dst_ref=land[k].at[r], send_sem=send.at[k * len(_REL) + r],
                recv_sem=recv.at[k * len(_REL) + r], device_id=(px, py, pc), device_id_type=MESH))
    return cps


def _rs_direct_start(grads, thru, name):
    n = len(grads)
    lands = [pltpu.with_memory_space_constraint(lax.empty((len(_REL),) + g.shape[2:], g.dtype), pltpu.HBM) for g in grads]

    def body(*refs):
        for cp in _rs_direct_copies(refs[:n], refs[n + 1:2 * n + 1], refs[2 * n + 1], refs[2 * n + 2]):
            cp.start()

    sems = pltpu.SemaphoreType.DMA((n * len(_REL),))
    keep = [pltpu.HBM(a.shape, a.dtype) for a in (*grads, thru, *lands)]
    out = pl.pallas_call(
        body, name=name, in_specs=[HBM] * (2 * n + 1), out_specs=(SEM, SEM) + (HBM,) * (2 * n + 1),
        out_shape=(sems, sems, *keep), input_output_aliases={i: 2 + i for i in range(2 * n + 1)},
        compiler_params=pltpu.CompilerParams(has_side_effects=DATAFLOW),
    )(*[pltpu.with_memory_space_constraint(a, pltpu.HBM) for a in (*grads, thru)], *lands)
    return out[0], out[1], out[2:2 + n], out[2 + n], out[3 + n:]


def _rs_direct_wait(send, recv, grads, lands, after, name):
    n = len(grads)

    def body(*refs):
        for cp in _rs_direct_copies(refs[:n], refs[n:2 * n], refs[2 * n], refs[2 * n + 1]):
            cp.wait_send()
            cp.wait_recv()

    keep = [pltpu.HBM(a.shape, a.dtype) for a in (*grads, *lands)]
    out = pl.pallas_call(
        body, name=name, in_specs=[HBM] * (2 * n) + [SEM, SEM, pl.BlockSpec(memory_space=pl.ANY)], out_specs=(HBM,) * (2 * n),
        out_shape=tuple(keep), input_output_aliases={i: i for i in range(2 * n)},
        compiler_params=pltpu.CompilerParams(has_side_effects=DATAFLOW),
    )(*grads, *lands, send, recv, after)
    return out[:n], out[n:]


def _rs_direct_sum(grad, land, layer, name):
    _, _, rows, cols = grad.shape
    t = _rs_rows(rows, cols)
    npieces = len(_REL) + 1

    def body(p_ref, g_ref, l_ref, o_ref, acc_ref):
        j = pl.program_id(1)

        @pl.when(j == 0)
        def _():
            acc_ref[...] = g_ref[...].astype(F32)

        @pl.when(j > 0)
        def _():
            acc_ref[...] += l_ref[...].astype(F32)

        @pl.when(j == npieces - 1)
        def _():
            o_ref[...] = acc_ref[...]

    x, y, c = _place()
    place = jnp.stack([x, y, c]).astype(jnp.int32)
    return pl.pallas_call(
        body, name=name,
        grid_spec=pltpu.PrefetchScalarGridSpec(
            num_scalar_prefetch=1, grid=(rows // t, npieces),
            in_specs=[pl.BlockSpec((None, None, t, cols), lambda i, j, p: (2 * p[0] + p[1], p[2], i, 0)),
                      pl.BlockSpec((None, t, cols), lambda i, j, p: (jnp.maximum(j - 1, 0), i, 0))],
            out_specs=pl.BlockSpec((None, None, t, cols), lambda i, j, p: (layer, p[2], i, 0)),
            scratch_shapes=[pltpu.VMEM((t, cols), F32)]),
        out_shape=jax.ShapeDtypeStruct((2, 2, rows, cols), F32), compiler_params=_cp("parallel", "arbitrary"),
    )(place, grad, land)


def _rs_tree(grads, into, layer):
    r1 = _exchange(grads, "c", "rs_c", half=2)
    a1 = [_rs_add(g, r, lambda s, p: (s, p[2]), (4,), BF16, f"rs_add1_{k}") for k, (g, r) in enumerate(zip(grads, r1))]
    a1 = [a.reshape(2, 2, 2, a.shape[1] // 2, a.shape[2]) for a in a1]
    r2 = _rs_exchange(a1, 2, "rs_p2")
    a2 = [_rs_add(a, r, lambda h, j, p: (p[0] * (1 - h) + j * h, j * (1 - h) + p[1] * h, h), (2, 2), BF16, f"rs_add2_{k}")
          for k, (a, r) in enumerate(zip(a1, r2))]
    r3 = _rs_exchange(a2, 3, "rs_p3")
    return [_rs_add(a, r, lambda h, p: (h, p[1] * (1 - h) + p[0] * h), (2,), F32, f"rs_add3_{k}", into=(buf, layer))
            for k, (a, r, buf) in enumerate(zip(a2, r3, into))]


def _all_reduce(buf):
    for axis in ("c", "x", "y"):
        (other,) = _exchange([buf], axis, f"ar_{axis}")
        buf = _add(buf, other, f"ar_add_{axis}")
    return buf


def _pad128(v):
    return jnp.zeros((1, 128), F32).at[0, :v.shape[0]].set(v)


def _layer_fwd(l, x, p):
    h1 = _rms_fwd(x, p["norm1_w"], f"rms1_{l}")
    proj = _matmul(h1, p["w_in"], "nn", name=f"mm_in_{l}", tn=768)
    y_pool = _pool_fwd(proj, p["pool_w"], p["pool_b"], p["pool_scale"], f"pool_{l}")
    qkv = _gdn_conv_fwd(proj, p["gdn_conv_w"], f"gconv_{l}")
    alog, dtb = _pad128(p["gdn_a_log"]), _pad128(p["gdn_dt_bias"])
    gw, gu, gqd, gkd, gat, gc = _gdn_prep(qkv, proj, alog, dtb, f"gprep_{l}")
    o, states = _gdn_scan(gw, gu, gqd, gkd, gat, gc, f"gscan_{l}")
    hl = _lru_fwd(proj, p["lru_conv_w"], p["lru_conv_b"], p["lru_wa"], p["lru_ba"], p["lru_wx"], p["lru_bx"],
                  p["lru_lambda"], f"lru_{l}")
    mixed = _mix_out(o, proj, hl, y_pool, p["gdn_norm_w"].reshape(1, GDN_DH), f"mix_{l}")
    x2 = _matmul(mixed, p["w_out"], "nn", name=f"mm_out_{l}", res=x)
    h2 = _rms_fwd(x2, p["norm2_w"], f"rms2_{l}")
    up = _matmul(h2, p["ffn_up"], "nn", name=f"mm_up_{l}", b_split=True)
    act = _ffn_act_fwd(up, p["ffn_conv_w"], f"ffn_{l}")
    x3 = _matmul(act, p["ffn_down"], "nn", name=f"mm_down_{l}", res=x2)
    saved = dict(x=x, h1=h1, proj=proj, qkv=qkv, gdn=(gw, gu, gqd, gkd, gat, gc), states=states, o=o, hl=hl, mixed=mixed,
                 x2=x2, h2=h2, up=up, act=act, alog=alog, dtb=dtb)
    return x3, saved


def _layer_bwd(l, dx3, p, s):
    g = {}
    dact = _matmul(dx3, p["ffn_down"], "nt", name=f"mm_ddown_{l}")
    g["ffn_down"] = _matmul(s["act"], dx3, "tn", name=f"mm_gdown_{l}", out_dtype=BF16)
    dup, g["ffn_conv_w"] = _ffn_act_bwd(s["up"], dact, p["ffn_conv_w"], f"ffn_b_{l}")
    dh2 = _matmul(dup, p["ffn_up"], "nt", name=f"mm_dup_{l}", b_split=True, tk=3072)
    g["ffn_up"] = _matmul(s["h2"], dup, "tn", name=f"mm_gup_{l}", b_split=True, o_split=4, tk=4096, out_dtype=BF16)
    dx2, g["norm2_w"] = _rms_bwd(s["x2"], p["norm2_w"], dh2, dx3, f"rms2_b_{l}")
    dmixed = _matmul(dx2, p["w_out"], "nt", name=f"mm_dout_{l}")
    g["w_out"] = _matmul(s["mixed"], dx2, "tn", name=f"mm_gout_{l}", out_dtype=BF16)
    proj = s["proj"]
    du_pool, g["pool_w"], g["pool_b"], g["pool_scale"] = _pool_bwd(proj, dmixed, p["pool_w"], p["pool_b"], p["pool_scale"], f"pool_b_{l}")
    gw, gu, gqd, gkd, gat, gc = s["gdn"]
    dw, du, dqd, dkd, dat, dgl, dz, g["gdn_norm_w"] = _gdn_scan_bwd(
        gw, gu, gqd, gkd, gat, gc, s["states"], s["o"], proj, dmixed, p["gdn_norm_w"].reshape(1, GDN_DH), f"gscan_b_{l}")
    dqkv, dab, gal, gdt = _gdn_prep_bwd(s["qkv"], proj, s["alog"], s["dtb"], dw, du, dqd, dkd, dat, dgl, f"gprep_b_{l}")
    g["gdn_a_log"], g["gdn_dt_bias"] = gal[0, :GDN_H], gdt[0, :GDN_H]
    dpre, g["gdn_conv_w"] = _gdn_conv_bwd(proj, dqkv, p["gdn_conv_w"], f"gconv_b_{l}")
    (dxr, dgr, g["lru_conv_w"], g["lru_conv_b"], g["lru_wa"], g["lru_ba"], g["lru_wx"], g["lru_bx"], g["lru_lambda"]) = _lru_bwd(
        proj, s["hl"], dmixed, p["lru_conv_w"], p["lru_conv_b"], p["lru_wa"], p["lru_ba"], p["lru_wx"], p["lru_bx"],
        p["lru_lambda"], f"lru_b_{l}")
    S = proj.shape[0]
    dproj = jnp.concatenate([dpre, dz, dxr, dgr, du_pool, dab, jnp.zeros((S, PCOLS - PAB - 128), BF16)], axis=1)
    dh1 = _matmul(dproj, p["w_in"], "nt", name=f"mm_din_{l}", tk=1792)
    g["w_in"] = _matmul(s["h1"], dproj, "tn", name=f"mm_gin_{l}", tn=768, tk=4096, out_dtype=BF16)
    dx, g["norm1_w"] = _rms_bwd(s["x"], p["norm1_w"], dh1, dx2, f"rms1_b_{l}")
    return dx, g


_IN_PERM = ((512, 3584), (3596, 5132), (0, 512), (3584, 3596))


def _w_in_to_proj(w):
    parts = [w[:, a:b] for a, b in _IN_PERM]
    return jnp.concatenate(parts + [jnp.zeros((w.shape[0], PCOLS - IN_COLS), w.dtype)], axis=1)


def _proj_to_w_in(g):
    return jnp.concatenate([g[:, PPOOL:PPOOL + 512], g[:, 0:3072], g[:, PAB:PAB + 12], g[:, 3072:PPOOL]], axis=1)


def _rows_to_mixed(w):
    return jnp.concatenate([w[512:], w[:512]], axis=0)


def _mixed_to_rows(g):
    return jnp.concatenate([g[1536:], g[:1536]], axis=0)


SMALL_SHARDED = ("gdn_conv_w", "lru_conv_w", "ffn_conv_w")
BIG = ("w_in", "w_out", "ffn_up", "ffn_down")
SMALL_REPLICATED = ("norm1_w", "pool_w", "pool_b", "pool_scale", "gdn_a_log", "gdn_dt_bias", "gdn_norm_w", "lru_conv_b",
                    "lru_wa", "lru_ba", "lru_wx", "lru_bx", "lru_lambda", "norm2_w")
WEIGHTS = ("norm1_w", "w_in", "pool_w", "pool_b", "pool_scale", "gdn_conv_w", "gdn_a_log", "gdn_dt_bias", "gdn_norm_w",
           "lru_conv_w", "lru_conv_b", "lru_wa", "lru_ba", "lru_wx", "lru_bx", "lru_lambda", "w_out", "norm2_w", "ffn_up",
           "ffn_conv_w", "ffn_down", "final_norm_w")
FLAT_COLS = 1024


def _pack(arrs):
    flat = jnp.concatenate([a.reshape(-1) for a in arrs])
    rows = -(-flat.shape[0] // (8 * FLAT_COLS)) * 8
    return jnp.pad(flat, (0, rows * FLAT_COLS - flat.shape[0])).reshape(rows, FLAT_COLS)


def _unpack(buf, like):
    flat = buf.reshape(-1)
    out, off = [], 0
    for a in like:
        size = 1
        for d in a.shape:
            size *= d
        out.append(flat[off:off + size].reshape(a.shape))
        off += size
    return out


def kernel(x, norm1_w, w_in, pool_w, pool_b, pool_scale, gdn_conv_w, gdn_a_log, gdn_dt_bias, gdn_norm_w, lru_conv_w, lru_conv_b, lru_wa, lru_ba, lru_wx, lru_bx, lru_lambda, w_out, norm2_w, ffn_up, ffn_conv_w, ffn_down, final_norm_w, loss_target, m_norm1_w, m_w_in, m_pool_w, m_pool_b, m_pool_scale, m_gdn_conv_w, m_gdn_a_log, m_gdn_dt_bias, m_gdn_norm_w, m_lru_conv_w, m_lru_conv_b, m_lru_wa, m_lru_ba, m_lru_wx, m_lru_bx, m_lru_lambda, m_w_out, m_norm2_w, m_ffn_up, m_ffn_conv_w, m_ffn_down, m_final_norm_w, v_norm1_w, v_w_in, v_pool_w, v_pool_b, v_pool_scale, v_gdn_conv_w, v_gdn_a_log, v_gdn_dt_bias, v_gdn_norm_w, v_lru_conv_w, v_lru_conv_b, v_lru_wa, v_lru_ba, v_lru_wx, v_lru_bx, v_lru_lambda, v_w_out, v_norm2_w, v_ffn_up, v_ffn_conv_w, v_ffn_down, v_final_norm_w):
    W = dict(norm1_w=norm1_w, w_in=w_in, pool_w=pool_w, pool_b=pool_b, pool_scale=pool_scale, gdn_conv_w=gdn_conv_w,
             gdn_a_log=gdn_a_log, gdn_dt_bias=gdn_dt_bias, gdn_norm_w=gdn_norm_w, lru_conv_w=lru_conv_w, lru_conv_b=lru_conv_b,
             lru_wa=lru_wa, lru_ba=lru_ba, lru_wx=lru_wx, lru_bx=lru_bx, lru_lambda=lru_lambda, w_out=w_out, norm2_w=norm2_w,
             ffn_up=ffn_up, ffn_conv_w=ffn_conv_w, ffn_down=ffn_down, final_norm_w=final_norm_w)
    M = dict(norm1_w=m_norm1_w, w_in=m_w_in, pool_w=m_pool_w, pool_b=m_pool_b, pool_scale=m_pool_scale, gdn_conv_w=m_gdn_conv_w,
             gdn_a_log=m_gdn_a_log, gdn_dt_bias=m_gdn_dt_bias, gdn_norm_w=m_gdn_norm_w, lru_conv_w=m_lru_conv_w,
             lru_conv_b=m_lru_conv_b, lru_wa=m_lru_wa, lru_ba=m_lru_ba, lru_wx=m_lru_wx, lru_bx=m_lru_bx, lru_lambda=m_lru_lambda,
             w_out=m_w_out, norm2_w=m_norm2_w, ffn_up=m_ffn_up, ffn_conv_w=m_ffn_conv_w, ffn_down=m_ffn_down,
             final_norm_w=m_final_norm_w)
    V = dict(norm1_w=v_norm1_w, w_in=v_w_in, pool_w=v_pool_w, pool_b=v_pool_b, pool_scale=v_pool_scale, gdn_conv_w=v_gdn_conv_w,
             gdn_a_log=v_gdn_a_log, gdn_dt_bias=v_gdn_dt_bias, gdn_norm_w=v_gdn_norm_w, lru_conv_w=v_lru_conv_w,
             lru_conv_b=v_lru_conv_b, lru_wa=v_lru_wa, lru_ba=v_lru_ba, lru_wx=v_lru_wx, lru_bx=v_lru_bx, lru_lambda=v_lru_lambda,
             w_out=v_w_out, norm2_w=v_norm2_w, ffn_up=v_ffn_up, ffn_conv_w=v_ffn_conv_w, ffn_down=v_ffn_down,
             final_norm_w=v_final_norm_w)
    S = x.shape[1]
    xs = x.reshape(S, D_MODEL)
    tgt = loss_target.reshape(S, D_MODEL)
    mx, my, mc = _place()
    shard = 2 * mx + my

    small_sh = jnp.concatenate([W[k].reshape(N_LAYERS, -1) for k in SMALL_SHARDED], axis=1)
    n_small = small_sh.shape[1]
    pad = -n_small % 1024
    small_sh = jnp.pad(small_sh, ((0, 0), (0, pad))).reshape(N_LAYERS, -1, 1024)
    def own_slot(w):
        zeros = jnp.zeros((N_LAYERS, 4) + w.shape[1:], w.dtype)
        return lax.dynamic_update_slice(zeros, w[:, None], (0, shard) + (0,) * (w.ndim - 1))

    gathered = _gather_weights([own_slot(W[k].astype(BF16)) for k in BIG] + [own_slot(small_sh)], "gather_weights")
    g_in, g_out, g_up, g_down, g_small = gathered
    g_small = g_small.reshape(N_LAYERS, 4, -1)[:, :, :n_small]

    layers = []
    for l in range(N_LAYERS):
        p = {k: W[k][l] for k in SMALL_REPLICATED}
        rows = g_in.shape[2]
        p["w_in"] = _w_in_to_proj(jnp.transpose(g_in[l], (1, 0, 2)).reshape(rows, IN_COLS))
        p["w_out"] = _rows_to_mixed(g_out[l].reshape(D_MODEL, D_MODEL))
        p["ffn_up"] = g_up[l]
        p["ffn_down"] = g_down[l].reshape(D_FF, D_MODEL)
        off = 0
        for k in SMALL_SHARDED:
            taps, width = W[k].shape[1], W[k].shape[2]
            piece = g_small[l, :, off:off + taps * width].reshape(4, taps, width)
            p[k] = jnp.transpose(piece, (1, 0, 2)).reshape(taps, 4 * width)
            off += taps * width
        layers.append(p)

    saved = []
    h = xs
    for l in range(N_LAYERS):
        h, s = _layer_fwd(l, h, layers[l])
        saved.append(s)
    loss_part, dh, g_final = _loss_head(h, final_norm_w, tgt, "loss_head")

    def big_partials(g_layer):
        out = []
        for k in BIG:
            g = g_layer[k]
            if k == "w_in":
                g = _proj_to_w_in(g)
                g = jnp.transpose(g.reshape(g.shape[0], 4, IN_COLS // 4), (1, 0, 2))
            elif k == "w_out":
                g = _mixed_to_rows(g).reshape(4, D_MODEL // 4, D_MODEL)
            elif k == "ffn_down":
                g = g.reshape(4, D_FF // 4, D_MODEL)
            out.append(g.reshape(4, 2, g.shape[1] // 2, g.shape[2]))
        return out

    grads = [None] * N_LAYERS
    dh, grads[1] = _layer_bwd(1, dh, layers[1], saved[1])
    send, recv, part1, dh, lands = _rs_direct_start(big_partials(grads[1]), dh, "rs_direct_start")
    dh, grads[0] = _layer_bwd(0, dh, layers[0], saved[0])
    part1, lands = _rs_direct_wait(send, recv, part1, lands, dh, "rs_direct_wait")
    reduced = [_rs_direct_sum(g, ld, 1, f"rs_sum_{k}") for k, (g, ld) in enumerate(zip(part1, lands))]
    reduced = _share_halves(_rs_tree(big_partials(grads[0]), reduced, 0), "rs_share")
    big_g = {k: r.reshape(N_LAYERS, 2 * r.shape[2], r.shape[3]) for k, r in zip(BIG, reduced)}
    grad_x = dh.reshape(x.shape)

    small_names = SMALL_REPLICATED + SMALL_SHARDED
    small_list = [jnp.stack([grads[l][k].reshape(W[k].shape[1:]) if k in SMALL_REPLICATED else grads[l][k] for l in range(N_LAYERS)])
                  for k in small_names]
    small_list += [g_final.reshape(D_MODEL), loss_part[0, 0:1]]
    reduced = _unpack(_all_reduce(_pack(small_list)), small_list)
    small_g = dict(zip(small_names, reduced[:len(small_names)]))
    small_g["final_norm_w"] = reduced[-2]
    loss = reduced[-1][0]
    for k in SMALL_SHARDED:
        width = W[k].shape[2]
        small_g[k] = lax.dynamic_slice_in_dim(small_g[k], shard * width, width, axis=2)

    G, DELTA, NM, NV = {}, {}, {}, {}
    for k in BIG:
        G[k] = big_g[k]
        DELTA[k], NM[k], NV[k] = _adamw(W[k], G[k], M[k], V[k], f"adam_{k}")
    small_all = small_names + ("final_norm_w",)
    dl, nm, nv = _adamw(_pack([W[k] for k in small_all]), _pack([small_g[k] for k in small_all]),
                        _pack([M[k] for k in small_all]), _pack([V[k] for k in small_all]), "adam_small")
    like = [W[k] for k in small_all]
    for k, d_, m_, v_ in zip(small_all, _unpack(dl, like), _unpack(nm, like), _unpack(nv, like)):
        G[k], DELTA[k], NM[k], NV[k] = small_g[k], d_, m_, v_

    return (loss, grad_x, *[G[k] for k in WEIGHTS], *[DELTA[k] for k in WEIGHTS], *[NM[k] for k in WEIGHTS],
            *[NV[k] for k in WEIGHTS])
```

```python
import functools

import jax
import jax.numpy as jnp
from jax import lax
from jax.experimental import pallas as pl
from jax.experimental.pallas import tpu as pltpu

F32 = jnp.float32
BF16 = jnp.bfloat16
_MXU = jnp.bfloat16

D_MODEL = 2048
N_LAYERS = 2
POOL_W = 512
POOL_G = 4
POOL_GD = 128
POOL_WINDOWS = (2, 4, 8, 16)
POOL_HALO = 16
GDN_W = 768
GDN_H = 6
GDN_DH = 128
GDN_C = 64
LRU_W = 768
LRU_NB = 6
LRU_BD = 128
LRU_C = 8.0
D_FF = 6144
EPS = 1e-6
IN_COLS = 5132
HALO = 8

PQ, PK, PV, PZ, PXR, PGR, PPOOL, PAB, PCOLS = 0, 768, 1536, 2304, 3072, 3840, 4608, 5120, 5376
CB = 768

ADAM_LR = 0.001
ADAM_B1 = 0.9
ADAM_B2 = 0.999
ADAM_EPS = 1e-08
ADAM_WD = 0.01
ADAM_STEP = 10

VMEM_LIMIT = 56 * 1024 * 1024
MESH = pl.DeviceIdType.MESH
HBM = pl.BlockSpec(memory_space=pltpu.HBM)


def _cp(*sem):
    return pltpu.CompilerParams(dimension_semantics=sem, vmem_limit_bytes=VMEM_LIMIT)


def _dg(a, b, ta, tb):
    dims = (((0 if ta else 1,), (1 if tb else 0,)), ((), ()))
    return lax.dot_general(a, b, dims, preferred_element_type=F32)


def _split2(a):
    hi = a.astype(BF16)
    lo = (a - hi.astype(F32)).astype(BF16)
    return hi, lo


def _mm_raw(a, b, ta, tb, hi):
    if _MXU == F32:
        return _dg(a, b, ta, tb)
    if not hi:
        return _dg(a.astype(_MXU), b.astype(_MXU), ta, tb)
    a1, a2 = _split2(a)
    b1, b2 = _split2(b)
    return _dg(a1, b1, ta, tb) + (_dg(a1, b2, ta, tb) + _dg(a2, b1, ta, tb))


@functools.partial(jax.custom_vjp, nondiff_argnums=(2, 3, 4))
def _mm(a, b, ta=False, tb=False, hi=False):
    return _mm_raw(a, b, ta, tb, hi)


def _mm_fwd(a, b, ta, tb, hi):
    return _mm_raw(a, b, ta, tb, hi), (a, b)


def _mm_bwd(ta, tb, hi, res, dc):
    a, b = res
    da = _mm(b, dc, tb, True, hi) if ta else _mm(dc, b, False, not tb, hi)
    db = _mm(dc, a, True, ta, hi) if tb else _mm(a, dc, not ta, False, hi)
    return da, db


_mm.defvjp(_mm_fwd, _mm_bwd)


def _mm01(m01, x):
    if _MXU == F32:
        return _dg(m01, x, False, False)
    m = m01.astype(BF16)
    x1 = x.astype(BF16)
    r = x - x1.astype(F32)
    x2 = r.astype(BF16)
    x3 = (r - x2.astype(F32)).astype(BF16)
    return _dg(m, x1, False, False) + (_dg(m, x2, False, False) + _dg(m, x3, False, False))


def _down(x, k):
    return x if k == 0 else pltpu.roll(x, k, 0)


def _up(x, k):
    return x if k == 0 else pltpu.roll(x, x.shape[0] - k, 0)


def _rows(shape):
    return lax.broadcasted_iota(jnp.int32, shape, 0)


def _lanes(shape):
    return lax.broadcasted_iota(jnp.int32, shape, 1)


def _matmul(a, b, mode, *, name, res=None, tm=1024, tn=1024, tk=2048, b_split=False, o_split=0, out_dtype=F32):
    ta, tb = mode == "tn", mode == "nt"
    a_split = a.ndim == 3
    if a_split:
        assert not ta
        M, K = a.shape[1], a.shape[0] * a.shape[2]
        tk = min(tk, a.shape[2])
    elif ta:
        K, M = a.shape
    else:
        M, K = a.shape
    if b_split:
        ns = b.shape[0]
        N = b.shape[1] if tb else ns * b.shape[2]
    else:
        N = b.shape[0] if tb else b.shape[1]
    tm, tn, tk = min(tm, M), min(tn, N), min(tk, K)
    if b_split:
        per = b.shape[2]
        if tb:
            tk = min(tk, per)
        else:
            tn = min(tn, per)
    if o_split:
        tn = min(tn, N // o_split)
    assert M % tm == 0 and N % tn == 0 and K % tk == 0, (name, M, N, K, tm, tn, tk)
    nk = K // tk
    if a_split:
        ka = a.shape[2] // tk
        a_spec = pl.BlockSpec((None, tm, tk), lambda i, j, k: (k // ka, i, k % ka))
    else:
        a_spec = pl.BlockSpec((tk, tm), lambda i, j, k: (k, i)) if ta else pl.BlockSpec((tm, tk), lambda i, j, k: (i, k))
    if not b_split:
        b_spec = pl.BlockSpec((tn, tk), lambda i, j, k: (j, k)) if tb else pl.BlockSpec((tk, tn), lambda i, j, k: (k, j))
    elif tb:
        kb = per // tk
        b_spec = pl.BlockSpec((None, tn, tk), lambda i, j, k: (k // kb, j, k % kb))
    else:
        nb = per // tn
        b_spec = pl.BlockSpec((None, tk, tn), lambda i, j, k: (j // nb, k, j % nb))
    if o_split:
        ob = (N // o_split) // tn
        out_shape = jax.ShapeDtypeStruct((o_split, M, N // o_split), out_dtype)
        o_spec = pl.BlockSpec((None, tm, tn), lambda i, j, k: (j // ob, i, j % ob))
    else:
        out_shape = jax.ShapeDtypeStruct((M, N), out_dtype)
        o_spec = pl.BlockSpec((tm, tn), lambda i, j, k: (i, j))
    in_specs = [a_spec, b_spec]
    args = [a, b]
    if res is not None:
        in_specs.append(pl.BlockSpec((tm, tn), lambda i, j, k: (i, j)))
        args.append(res)
    use_acc = nk > 1 and out_dtype != F32

    def body(*refs):
        a_ref, b_ref = refs[0], refs[1]
        o_ref = refs[2 + (res is not None)]
        acc_ref = refs[-1] if use_acc else o_ref
        p = _dg(a_ref[...].astype(_MXU), b_ref[...].astype(_MXU), ta, tb)
        first = p + refs[2][...] if res is not None else p
        if nk == 1:
            o_ref[...] = first.astype(o_ref.dtype)
        else:
            k = pl.program_id(2)

            @pl.when(k == 0)
            def _():
                acc_ref[...] = first

            @pl.when(k > 0)
            def _():
                acc_ref[...] += p

            if use_acc:
                @pl.when(k == nk - 1)
                def _():
                    o_ref[...] = acc_ref[...].astype(o_ref.dtype)

    return pl.pallas_call(
        body, name=name, grid=(M // tm, N // tn, nk), in_specs=in_specs, out_specs=o_spec, out_shape=out_shape,
        scratch_shapes=[pltpu.VMEM((tm, tn), F32)] if use_acc else [],
        compiler_params=_cp("parallel", "parallel", "arbitrary"),
    )(*args)


def _rms(x, w):
    return x * lax.rsqrt(jnp.mean(x * x, axis=-1, keepdims=True) + EPS) * w


def _row_tile(S, t=512):
    t = min(t, S)
    assert S % t == 0
    return t


def _rms_fwd(x, w, name):
    S, D = x.shape
    T = _row_tile(S)

    def body(x_ref, w_ref, o_ref):
        o_ref[...] = _rms(x_ref[...], w_ref[...]).astype(o_ref.dtype)

    return pl.pallas_call(
        body, name=name, grid=(S // T,),
        in_specs=[pl.BlockSpec((T, D), lambda i: (i, 0)), pl.BlockSpec((1, D), lambda i: (0, 0))],
        out_specs=pl.BlockSpec((T, D), lambda i: (i, 0)), out_shape=jax.ShapeDtypeStruct((S, D), BF16),
        compiler_params=_cp("parallel"),
    )(x, w.reshape(1, D))


def _rms_bwd(x, w, dh, dres, name):
    S, D = x.shape
    T = _row_tile(S)

    def body(x_ref, w_ref, dh_ref, dr_ref, dx_ref, gw_ref):
        _, vjp = jax.vjp(_rms, x_ref[...], w_ref[...])
        dx, dw = vjp(dh_ref[...])
        dx_ref[...] = dr_ref[...] + dx

        @pl.when(pl.program_id(0) == 0)
        def _():
            gw_ref[...] = jnp.zeros_like(gw_ref)

        gw_ref[...] += dw

    row = pl.BlockSpec((T, D), lambda i: (i, 0))
    vec = pl.BlockSpec((1, D), lambda i: (0, 0))
    return pl.pallas_call(
        body, name=name, grid=(S // T,), in_specs=[row, vec, row, row], out_specs=[row, vec],
        out_shape=[jax.ShapeDtypeStruct((S, D), F32), jax.ShapeDtypeStruct((1, D), F32)],
        compiler_params=_cp("arbitrary"),
    )(x, w.reshape(1, D), dh, dres)


def _loss_head(x, w, tgt, name):
    S, D = x.shape
    T = _row_tile(S)

    def body(x_ref, w_ref, t_ref, l_ref, dx_ref, gw_ref):
        y, vjp = jax.vjp(_rms, x_ref[...], w_ref[...])
        err = y - t_ref[...]
        part = 0.5 * jnp.sum(jnp.mean(err * err, axis=-1, keepdims=True), axis=0, keepdims=True)
        dx, dw = vjp(err * (1.0 / D))
        dx_ref[...] = dx

        @pl.when(pl.program_id(0) == 0)
        def _():
            gw_ref[...] = jnp.zeros_like(gw_ref)
            l_ref[...] = jnp.zeros_like(l_ref)

        gw_ref[...] += dw
        l_ref[...] += jnp.broadcast_to(part, l_ref.shape)

    row = pl.BlockSpec((T, D), lambda i: (i, 0))
    vec = pl.BlockSpec((1, D), lambda i: (0, 0))
    return pl.pallas_call(
        body, name=name, grid=(S // T,), in_specs=[row, vec, row],
        out_specs=[pl.BlockSpec((8, 128), lambda i: (0, 0)), row, vec],
        out_shape=[jax.ShapeDtypeStruct((8, 128), F32), jax.ShapeDtypeStruct((S, D), F32), jax.ShapeDtypeStruct((1, D), F32)],
        compiler_params=_cp("arbitrary"),
    )(x, w.reshape(1, D), tgt)


def _by_group(shape, vals):
    g = _lanes(shape) // POOL_GD
    out = vals[-1]
    for k in range(len(vals) - 2, -1, -1):
        out = jnp.where(g == k, vals[k], out)
    return out


def _pool_d(prev, u, t0):
    ext = jnp.concatenate([prev, u], axis=0)
    s2 = ext + _down(ext, 1)
    s4 = s2 + _down(s2, 2)
    s8 = s4 + _down(s4, 4)
    s16 = s8 + _down(s8, 8)
    ssel = _by_group(ext.shape, [s2, s4, s8, s16])[POOL_HALO:]
    win = _by_group(u.shape, [jnp.int32(w) for w in POOL_WINDOWS])
    cnt = jnp.minimum(t0 + _rows(u.shape) + 1, win).astype(F32)
    return ssel / cnt - u


def _pool_lin(d, w_ref, b):
    ys = [_mm(d[:, g * POOL_GD:(g + 1) * POOL_GD], w_ref[g]) for g in range(POOL_G)]
    return jnp.concatenate(ys, axis=1) + b


def _pool_fwd(proj, w, b, scale, name):
    S = proj.shape[0]
    T = _row_tile(S)
    r = T // POOL_HALO
    cb = PPOOL // POOL_W

    def body(u_ref, up_ref, w_ref, b_ref, sc_ref, y_ref):
        i = pl.program_id(0)
        prev = jnp.where(i > 0, up_ref[...], 0.0)
        d = _pool_d(prev, u_ref[...], i * T)
        y_ref[...] = _pool_lin(d, w_ref, b_ref[...]) * sc_ref[...]

    vec = pl.BlockSpec((1, POOL_W), lambda i: (0, 0))
    return pl.pallas_call(
        body, name=name, grid=(S // T,),
        in_specs=[pl.BlockSpec((T, POOL_W), lambda i: (i, cb)),
                  pl.BlockSpec((POOL_HALO, POOL_W), lambda i: (jnp.maximum(i * r - 1, 0), cb)),
                  pl.BlockSpec((POOL_G, POOL_GD, POOL_GD), lambda i: (0, 0, 0)), vec, vec],
        out_specs=pl.BlockSpec((T, POOL_W), lambda i: (i, 0)), out_shape=jax.ShapeDtypeStruct((S, POOL_W), F32),
        compiler_params=_cp("parallel"),
    )(proj, proj, w, b.reshape(1, POOL_W), scale.reshape(1, POOL_W))


def _pool_bwd(proj, dmixed, w, b, scale, name):
    S = proj.shape[0]
    T = _row_tile(S)
    n = S // T
    r = T // POOL_HALO
    cb = PPOOL // POOL_W
    mb = 1536 // POOL_W

    def body(u_ref, up_ref, dy_ref, dyn_ref, w_ref, b_ref, sc_ref, du_ref, gw_ref, gb_ref, gs_ref):
        i = pl.program_id(0)
        sc = sc_ref[...]
        dy = dy_ref[...]
        dy_ext = jnp.concatenate([dy, jnp.where(i < n - 1, dyn_ref[...], 0.0)], axis=0)
        dyl = dy_ext * sc
        dd = jnp.concatenate(
            [_mm(dyl[:, g * POOL_GD:(g + 1) * POOL_GD], w_ref[g], False, True) for g in range(POOL_G)], axis=1)
        t_ext = i * T + _rows(dd.shape)
        win = _by_group(dd.shape, [jnp.int32(v) for v in POOL_WINDOWS])
        cnt = jnp.minimum(t_ext + 1, win).astype(F32)
        e = jnp.where(t_ext < S, dd / cnt, 0.0)
        f2 = e + _up(e, 1)
        f4 = f2 + _up(f2, 2)
        f8 = f4 + _up(f4, 4)
        f16 = f8 + _up(f8, 8)
        du = (_by_group(dd.shape, [f2, f4, f8, f16]) - dd)[:T]
        du_ref[...] = du.astype(du_ref.dtype)

        prev = jnp.where(i > 0, up_ref[...], 0.0)
        d = _pool_d(prev, u_ref[...], i * T)
        ylin = _pool_lin(d, w_ref, b_ref[...])
        dyl_m = dy * sc

        @pl.when(i == 0)
        def _():
            gw_ref[...] = jnp.zeros_like(gw_ref)
            gb_ref[...] = jnp.zeros_like(gb_ref)
            gs_ref[...] = jnp.zeros_like(gs_ref)

        gs_ref[...] += jnp.sum(dy * ylin, axis=0, keepdims=True)
        gb_ref[...] += jnp.sum(dyl_m, axis=0, keepdims=True)
        for g in range(POOL_G):
            sl = slice(g * POOL_GD, (g + 1) * POOL_GD)
            gw_ref[g] += _mm(d[:, sl], dyl_m[:, sl], True, False)

    vec = pl.BlockSpec((1, POOL_W), lambda i: (0, 0))
    wsp = pl.BlockSpec((POOL_G, POOL_GD, POOL_GD), lambda i: (0, 0, 0))
    nh = S // POOL_HALO
    return pl.pallas_call(
        body, name=name, grid=(n,),
        in_specs=[pl.BlockSpec((T, POOL_W), lambda i: (i, cb)),
                  pl.BlockSpec((POOL_HALO, POOL_W), lambda i: (jnp.maximum(i * r - 1, 0), cb)),
                  pl.BlockSpec((T, POOL_W), lambda i: (i, mb)),
                  pl.BlockSpec((POOL_HALO, POOL_W), lambda i: (jnp.minimum((i + 1) * r, nh - 1), mb)),
                  wsp, vec, vec],
        out_specs=[pl.BlockSpec((T, POOL_W), lambda i: (i, 0)), wsp, vec, vec],
        out_shape=[jax.ShapeDtypeStruct((S, POOL_W), BF16), jax.ShapeDtypeStruct((POOL_G, POOL_GD, POOL_GD), F32),
                   jax.ShapeDtypeStruct((1, POOL_W), F32), jax.ShapeDtypeStruct((1, POOL_W), F32)],
        compiler_params=_cp("arbitrary"),
    )(proj, proj, dmixed, dmixed, w, b.reshape(1, POOL_W), scale.reshape(1, POOL_W))


def _conv_rows(ext, w_ref, taps):
    acc = w_ref[taps - 1:taps, :] * ext
    for k in range(1, taps):
        acc = acc + w_ref[taps - 1 - k:taps - k, :] * _down(ext, k)
    return acc


def _conv_t_rows(dc, w_ref, taps):
    acc = w_ref[taps - 1:taps, :] * dc
    for k in range(1, taps):
        acc = acc + w_ref[taps - 1 - k:taps - k, :] * _up(dc, k)
    return acc


def _conv_specs(T, S, ncb0, with_next):
    r = T // HALO
    nh = S // HALO
    main = pl.BlockSpec((T, CB), lambda j, i: (i, j + ncb0))
    prev = pl.BlockSpec((HALO, CB), lambda j, i: (jnp.maximum(i * r - 1, 0), j + ncb0))
    nxt = pl.BlockSpec((HALO, CB), lambda j, i: (jnp.minimum((i + 1) * r, nh - 1), j + ncb0))
    return (main, prev, nxt) if with_next else (main, prev)


def _gdn_conv_fwd(proj, w, name):
    S = proj.shape[0]
    T = _row_tile(S)
    taps = w.shape[0]
    ncb = 3 * GDN_W // CB

    def body(x_ref, xp_ref, w_ref, o_ref):
        i = pl.program_id(1)
        ext = jnp.concatenate([jnp.where(i > 0, xp_ref[...], 0.0), x_ref[...]], axis=0)
        o_ref[...] = jax.nn.silu(_conv_rows(ext, w_ref, taps)[HALO:])

    main, prev = _conv_specs(T, S, PQ // CB, False)
    return pl.pallas_call(
        body, name=name, grid=(ncb, S // T),
        in_specs=[main, prev, pl.BlockSpec((taps, CB), lambda j, i: (0, j))],
        out_specs=pl.BlockSpec((T, CB), lambda j, i: (i, j)), out_shape=jax.ShapeDtypeStruct((S, 3 * GDN_W), F32),
        compiler_params=_cp("parallel", "parallel"),
    )(proj, proj, w)


def _gdn_conv_bwd(proj, dact, w, name):
    S = proj.shape[0]
    T = _row_tile(S)
    n = S // T
    taps = w.shape[0]
    ncb = 3 * GDN_W // CB

    def body(x_ref, xp_ref, xn_ref, d_ref, dn_ref, w_ref, dx_ref, gw_ref):
        i = pl.program_id(1)
        last = i == n - 1
        ext = jnp.concatenate([jnp.where(i > 0, xp_ref[...], 0.0), x_ref[...], jnp.where(last, 0.0, xn_ref[...])], axis=0)
        c = _conv_rows(ext, w_ref, taps)[HALO:]
        d_ext = jnp.concatenate([d_ref[...], jnp.where(last, 0.0, dn_ref[...])], axis=0)
        _, vjp = jax.vjp(jax.nn.silu, c)
        dc = vjp(d_ext)[0]
        dx_ref[...] = _conv_t_rows(dc, w_ref, taps)[:T].astype(dx_ref.dtype)

        @pl.when(i == 0)
        def _():
            gw_ref[...] = jnp.zeros_like(gw_ref)

        dcm = dc[:T]
        for k in range(taps):
            gw_ref[taps - 1 - k:taps - k, :] += jnp.sum(dcm * _down(ext, k)[HALO:HALO + T], axis=0, keepdims=True)

    main, prev, nxt = _conv_specs(T, S, PQ // CB, True)
    dmain, _, dnxt = _conv_specs(T, S, 0, True)
    wsp = pl.BlockSpec((taps, CB), lambda j, i: (0, j))
    return pl.pallas_call(
        body, name=name, grid=(ncb, n), in_specs=[main, prev, nxt, dmain, dnxt, wsp],
        out_specs=[pl.BlockSpec((T, CB), lambda j, i: (i, j)), wsp],
        out_shape=[jax.ShapeDtypeStruct((S, 3 * GDN_W), BF16), jax.ShapeDtypeStruct((taps, 3 * GDN_W), F32)],
        compiler_params=_cp("parallel", "arbitrary"),
    )(proj, proj, proj, dact, dact, w)


def _ffn_act_fwd(up, w, name):
    S = up.shape[0]
    T = _row_tile(S)
    taps = w.shape[0]
    ncb = D_FF // CB

    def body(g_ref, gp_ref, v_ref, w_ref, o_ref):
        i = pl.program_id(1)
        ext = jnp.concatenate([jnp.where(i > 0, gp_ref[...], 0.0), g_ref[...]], axis=0)
        c = _conv_rows(ext, w_ref, taps)[HALO:]
        o_ref[...] = (jax.nn.gelu(c) * v_ref[...]).astype(o_ref.dtype)

    main, prev = _conv_specs(T, S, 0, False)
    val = pl.BlockSpec((T, CB), lambda j, i: (i, j + ncb))
    return pl.pallas_call(
        body, name=name, grid=(ncb, S // T),
        in_specs=[main, prev, val, pl.BlockSpec((taps, CB), lambda j, i: (0, j))],
        out_specs=pl.BlockSpec((T, CB), lambda j, i: (i, j)), out_shape=jax.ShapeDtypeStruct((S, D_FF), BF16),
        compiler_params=_cp("parallel", "parallel"),
    )(up, up, up, w)


def _ffn_act_bwd(up, dact, w, name):
    S = up.shape[0]
    T = _row_tile(S)
    n = S // T
    taps = w.shape[0]
    ncb = D_FF // CB

    def body(g_ref, gp_ref, gn_ref, v_ref, vn_ref, d_ref, dn_ref, w_ref, dup_ref, gw_ref):
        i = pl.program_id(1)
        last = i == n - 1
        ext = jnp.concatenate([jnp.where(i > 0, gp_ref[...], 0.0), g_ref[...], jnp.where(last, 0.0, gn_ref[...])], axis=0)
        c = _conv_rows(ext, w_ref, taps)[HALO:]
        v_ext = jnp.concatenate([v_ref[...], jnp.where(last, 0.0, vn_ref[...])], axis=0)
        d_ext = jnp.concatenate([d_ref[...], jnp.where(last, 0.0, dn_ref[...])], axis=0)
        gl, vjp = jax.vjp(jax.nn.gelu, c)
        dup_ref[1] = (d_ext * gl)[:T].astype(dup_ref.dtype)
        dc = vjp(d_ext * v_ext)[0]
        dup_ref[0] = _conv_t_rows(dc, w_ref, taps)[:T].astype(dup_ref.dtype)

        @pl.when(i == 0)
        def _():
            gw_ref[...] = jnp.zeros_like(gw_ref)

        dcm = dc[:T]
        for k in range(taps):
            gw_ref[taps - 1 - k:taps - k, :] += jnp.sum(dcm * _down(ext, k)[HALO:HALO + T], axis=0, keepdims=True)

    main, prev, nxt = _conv_specs(T, S, 0, True)
    vmain, _, vnxt = _conv_specs(T, S, ncb, True)
    wsp = pl.BlockSpec((taps, CB), lambda j, i: (0, j))
    osp = pl.BlockSpec((2, T, CB), lambda j, i: (0, i, j))
    return pl.pallas_call(
        body, name=name, grid=(ncb, n), in_specs=[main, prev, nxt, vmain, vnxt, main, nxt, wsp],
        out_specs=[osp, wsp],
        out_shape=[jax.ShapeDtypeStruct((2, S, D_FF), BF16), jax.ShapeDtypeStruct((taps, D_FF), F32)],
        compiler_params=_cp("parallel", "arbitrary"),
    )(up, up, up, up, up, dact, dact, w)


def _tri_masks():
    r = _rows((GDN_C, GDN_C))
    c = _lanes((GDN_C, GDN_C))
    return r >= c, r > c


def _each(fn, *cols):
    return tuple(fn(*args) for args in zip(*cols))


def _tri_inv_raw(lows):
    r = _rows(lows[0].shape)
    c = _lanes(lows[0].shape)
    eye = jnp.where(r == c, 1.0, 0.0)
    ps = _each(lambda low: eye - low, lows)
    lps = lows
    for _ in range(5):
        lps = _each(lambda lp: _mm(lp, lp, False, False, True), lps)
        ps = _each(lambda p, lp: p + _mm(p, lp, False, False, True), ps, lps)
    return ps


@jax.custom_vjp
def _tri_inv(lows):
    return _tri_inv_raw(lows)


def _tri_inv_fwd(lows):
    ts = _tri_inv_raw(lows)
    return ts, ts


def _tri_inv_bwd(ts, dts):
    inner = _each(lambda t, dt: _mm(t, dt, True, False, True), ts, dts)
    return (_each(lambda m, t: -_mm(m, t, False, True, True), inner, ts),)


_tri_inv.defvjp(_tri_inv_fwd, _tri_inv_bwd)


def _gdn_glog(a_col, alog, dtb):
    return -jnp.exp(alog) * jax.nn.softplus(a_col + dtb)


def _decay_operand():
    r = _rows((GDN_C, 2 * GDN_C))
    c = _lanes((GDN_C, 2 * GDN_C))
    return jnp.where((c >= GDN_C) | (r > c), 1.0, 0.0)


def _gdn_decay(glog):
    causal, _ = _tri_masks()
    res = _mm01(jnp.where(causal, 1.0, 0.0), glog * _decay_operand())
    return res[:, GDN_C:GDN_C + 1], res[:, :GDN_C]


def _gdn_decay_bwd(dgcol, dd):
    r = _rows((GDN_C, GDN_C))
    c = _lanes((GDN_C, GDN_C))
    dres = jnp.concatenate([dd, jnp.where(c == 0, dgcol, 0.0)], axis=1)
    dx = _mm01(jnp.where(r <= c, 1.0, 0.0), dres)
    return jnp.sum(dx * _decay_operand(), axis=1, keepdims=True)


def _gdn_chunk(qa, ka, va, bt_col, gcol, dmat):
    causal, strict = _tri_masks()
    qn = _each(lambda q: q * lax.rsqrt(jnp.sum(q * q, axis=-1, keepdims=True) + EPS) * (GDN_DH ** -0.5), qa)
    kn = _each(lambda k: k * lax.rsqrt(jnp.sum(k * k, axis=-1, keepdims=True) + EPS), ka)
    beta = _each(jax.nn.sigmoid, bt_col)
    eg = _each(jnp.exp, gcol)
    decay = _each(lambda d: jnp.where(causal, jnp.exp(d), 0.0), dmat)
    kk = _each(lambda k: _mm(k, k, False, True), kn)
    low = _each(lambda b, m, d: jnp.where(strict, b * m * d, 0.0), beta, kk, decay)
    t = _tri_inv(low)
    w = _each(lambda t_, k, b, e: _mm(t_, k * (b * e), False, False, True), t, kn, beta, eg)
    u = _each(lambda t_, v, b: _mm(t_, v * b, False, False, True), t, va, beta)
    attn = _each(lambda q, k, d: _mm(q, k, False, True) * d, qn, kn, decay)
    last = _rows(gcol[0].shape) == GDN_C - 1
    g_last = _each(lambda g: jnp.sum(jnp.where(last, g, 0.0), axis=0, keepdims=True), gcol)
    qd = _each(lambda q, e: q * e, qn, eg)
    kd = _each(lambda k, gl, g: k * jnp.exp(gl - g), kn, g_last, gcol)
    return w, u, qd, kd, attn


def _gdn_step(state, w, u, qd, kd, attn, egl):
    v_new = _each(lambda u_, w_, s: u_ - _mm(w_, s), u, w, state)
    o_state = _each(_mm, qd, state)
    o = _each(lambda os, a, v: os + _mm(a, v), o_state, attn, v_new)
    new = _each(lambda s, e, k, v: s * e + _mm(k, v, True, False), state, egl, kd, v_new)
    return o, new


def _heads(ref, base=0, width=GDN_DH):
    return tuple(ref[:, (base + h) * GDN_DH:(base + h) * GDN_DH + width] for h in range(GDN_H))


def _cols(a, base):
    return tuple(a[:, base + h:base + h + 1] for h in range(GDN_H))


def _gated_norm(o, z, nw):
    return o * lax.rsqrt(jnp.mean(o * o, axis=-1, keepdims=True) + EPS) * nw * jax.nn.silu(z)


def _hsl(h):
    return slice(h * GDN_DH, (h + 1) * GDN_DH)


def _pad_lanes(a, width=GDN_DH):
    return jnp.concatenate([a, jnp.zeros((a.shape[0], width - a.shape[1]), a.dtype)], axis=1)


def _gdn_prep(qkv, proj, alog, dtb, name):
    S = qkv.shape[0]
    N = S // GDN_C

    def body(qkv_ref, ab_ref, al_ref, dt_ref, w_ref, u_ref, qd_ref, kd_ref, at_ref, gc_ref):
        ab = ab_ref[...]
        glog = _each(_gdn_glog, _cols(ab, 0), _cols(al_ref[...], 0), _cols(dt_ref[...], 0))
        dec = _each(_gdn_decay, glog)
        gcol, dmat = _each(lambda d: d[0], dec), _each(lambda d: d[1], dec)
        w, u, qd, kd, attn = _gdn_chunk(_heads(qkv_ref), _heads(qkv_ref, GDN_H), _heads(qkv_ref, 2 * GDN_H),
                                        _cols(ab, GDN_H), gcol, dmat)
        gc = jnp.zeros((GDN_C, 128), F32)
        for h in range(GDN_H):
            w_ref[:, _hsl(h)] = w[h]
            u_ref[:, _hsl(h)] = u[h]
            qd_ref[:, _hsl(h)] = qd[h]
            kd_ref[:, _hsl(h)] = kd[h]
            at_ref[:, _hsl(h)] = _pad_lanes(attn[h])
            gc = jnp.where(_lanes(gc.shape) == h, gcol[h], gc)
        gc_ref[...] = gc

    vec = pl.BlockSpec((1, 128), lambda i: (0, 0))
    hsp = pl.BlockSpec((GDN_C, GDN_W), lambda i: (i, 0))
    hshape = jax.ShapeDtypeStruct((S, GDN_W), F32)
    return pl.pallas_call(
        body, name=name, grid=(N,),
        in_specs=[pl.BlockSpec((GDN_C, 3 * GDN_W), lambda i: (i, 0)), pl.BlockSpec((GDN_C, 128), lambda i: (i, PAB // 128)), vec, vec],
        out_specs=[hsp] * 5 + [pl.BlockSpec((GDN_C, 128), lambda i: (i, 0))],
        out_shape=[hshape] * 5 + [jax.ShapeDtypeStruct((S, 128), F32)],
        compiler_params=_cp("parallel"),
    )(qkv, proj, alog, dtb)


def _gdn_scan(w, u, qd, kd, attn, gc, name):
    S = w.shape[0]
    N = S // GDN_C

    def body(w_ref, u_ref, qd_ref, kd_ref, at_ref, gc_ref, o_ref, st_ref, s_ref):
        @pl.when(pl.program_id(0) == 0)
        def _():
            s_ref[...] = jnp.zeros_like(s_ref)

        state = tuple(s_ref[_hsl(h), :] for h in range(GDN_H))
        egl = _each(jnp.exp, _cols(gc_ref[GDN_C - 1:GDN_C, :], 0))
        o, new = _gdn_step(state, _heads(w_ref), _heads(u_ref), _heads(qd_ref), _heads(kd_ref),
                           _heads(at_ref, width=GDN_C), egl)
        for h in range(GDN_H):
            st_ref[_hsl(h), :] = state[h]
            o_ref[:, _hsl(h)] = o[h]
            s_ref[_hsl(h), :] = new[h]

    hsp = pl.BlockSpec((GDN_C, GDN_W), lambda i: (i, 0))
    return pl.pallas_call(
        body, name=name, grid=(N,),
        in_specs=[hsp] * 5 + [pl.BlockSpec((GDN_C, 128), lambda i: (i, 0))],
        out_specs=[hsp, pl.BlockSpec((None, GDN_W, GDN_DH), lambda i: (i, 0, 0))],
        out_shape=[jax.ShapeDtypeStruct((S, GDN_W), F32), jax.ShapeDtypeStruct((N, GDN_W, GDN_DH), F32)],
        scratch_shapes=[pltpu.VMEM((GDN_W, GDN_DH), F32)],
        compiler_params=_cp("arbitrary"),
    )(w, u, qd, kd, attn, gc)


def _gdn_scan_bwd(w, u, qd, kd, attn, gc, states, o, proj, dmixed, nw, name):
    S = w.shape[0]
    N = S // GDN_C

    def body(w_ref, u_ref, qd_ref, kd_ref, at_ref, gc_ref, st_ref, o_ref, z_ref, dm_ref, nw_ref,
             dw_ref, du_ref, dqd_ref, dkd_ref, dat_ref, dgl_ref, dz_ref, gnw_ref, ds_ref):
        @pl.when(pl.program_id(0) == 0)
        def _():
            ds_ref[...] = jnp.zeros_like(ds_ref)
            gnw_ref[...] = jnp.zeros_like(gnw_ref)

        nw = nw_ref[...]
        _, vjp_n = jax.vjp(lambda o, z, w_: _each(lambda a, b: _gated_norm(a, b, w_), o, z), _heads(o_ref), _heads(z_ref), nw)
        do, dz, dnw = vjp_n(_heads(dm_ref))
        state = tuple(st_ref[_hsl(h), :] for h in range(GDN_H))
        egl = _each(jnp.exp, _cols(gc_ref[GDN_C - 1:GDN_C, :], 0))
        _, vjp_s = jax.vjp(_gdn_step, state, _heads(w_ref), _heads(u_ref), _heads(qd_ref), _heads(kd_ref),
                           _heads(at_ref, width=GDN_C), egl)
        ds, dw, du, dqd, dkd, dat, degl = vjp_s((do, tuple(ds_ref[_hsl(h), :] for h in range(GDN_H))))
        dgl = jnp.zeros((8, 128), F32)
        for h in range(GDN_H):
            dz_ref[:, _hsl(h)] = dz[h].astype(dz_ref.dtype)
            ds_ref[_hsl(h), :] = ds[h]
            dw_ref[:, _hsl(h)] = dw[h]
            du_ref[:, _hsl(h)] = du[h]
            dqd_ref[:, _hsl(h)] = dqd[h]
            dkd_ref[:, _hsl(h)] = dkd[h]
            dat_ref[:, _hsl(h)] = _pad_lanes(dat[h])
            dgl = jnp.where(_lanes(dgl.shape) == h, degl[h] * egl[h], dgl)
        dgl_ref[...] = dgl
        gnw_ref[...] += dnw

    rev = lambda i: (N - 1 - i, 0)
    hsp = pl.BlockSpec((GDN_C, GDN_W), rev)
    gsp = pl.BlockSpec((GDN_C, 128), rev)
    vec = pl.BlockSpec((1, GDN_DH), lambda i: (0, 0))
    hshape = jax.ShapeDtypeStruct((S, GDN_W), F32)
    return pl.pallas_call(
        body, name=name, grid=(N,),
        in_specs=[hsp] * 5 + [gsp, pl.BlockSpec((None, GDN_W, GDN_DH), lambda i: (N - 1 - i, 0, 0)), hsp,
                              pl.BlockSpec((GDN_C, GDN_W), lambda i: (N - 1 - i, PZ // GDN_W)), hsp, vec],
        out_specs=[hsp] * 5 + [pl.BlockSpec((8, 128), rev), hsp, vec],
        out_shape=[hshape] * 5 + [jax.ShapeDtypeStruct((N * 8, 128), F32), jax.ShapeDtypeStruct((S, GDN_W), BF16),
                                  jax.ShapeDtypeStruct((1, GDN_DH), F32)],
        scratch_shapes=[pltpu.VMEM((GDN_W, GDN_DH), F32)],
        compiler_params=_cp("arbitrary"),
    )(w, u, qd, kd, attn, gc, states, o, proj, dmixed, nw)


def _gdn_prep_bwd(qkv, proj, alog, dtb, dw, du, dqd, dkd, dat, dgl, name):
    S = qkv.shape[0]
    N = S // GDN_C

    def body(qkv_ref, ab_ref, al_ref, dt_ref, dw_ref, du_ref, dqd_ref, dkd_ref, dat_ref, dgl_ref,
             dqkv_ref, dab_ref, gal_ref, gdt_ref):
        @pl.when(pl.program_id(0) == 0)
        def _():
            gal_ref[...] = jnp.zeros_like(gal_ref)
            gdt_ref[...] = jnp.zeros_like(gdt_ref)

        ab = ab_ref[...]
        glog, vjp_g = jax.vjp(lambda a, al, dt: _each(_gdn_glog, a, al, dt), _cols(ab, 0), _cols(al_ref[...], 0),
                              _cols(dt_ref[...], 0))
        dec = _each(_gdn_decay, glog)
        gcol, dmat = _each(lambda d: d[0], dec), _each(lambda d: d[1], dec)
        _, vjp_c = jax.vjp(_gdn_chunk, _heads(qkv_ref), _heads(qkv_ref, GDN_H), _heads(qkv_ref, 2 * GDN_H),
                           _cols(ab, GDN_H), gcol, dmat)
        dqa, dka, dva, dbt, dgcol, dd = vjp_c((_heads(dw_ref), _heads(du_ref), _heads(dqd_ref), _heads(dkd_ref),
                                               _heads(dat_ref, width=GDN_C)))
        last = _rows(dgcol[0].shape) == GDN_C - 1
        dgcol = _each(lambda d, g: d + jnp.where(last, g, 0.0), dgcol, _cols(dgl_ref[0:1, :], 0))
        da_col, dal, ddt = vjp_g(_each(_gdn_decay_bwd, dgcol, dd))
        dab = jnp.zeros((GDN_C, 128), F32)
        gal = jnp.zeros((1, 128), F32)
        gdt = jnp.zeros((1, 128), F32)
        for h in range(GDN_H):
            dqkv_ref[:, _hsl(h)] = dqa[h]
            dqkv_ref[:, _hsl(GDN_H + h)] = dka[h]
            dqkv_ref[:, _hsl(2 * GDN_H + h)] = dva[h]
            ln = _lanes(dab.shape)
            dab = dab + jnp.where(ln == h, da_col[h], 0.0) + jnp.where(ln == GDN_H + h, dbt[h], 0.0)
            l1 = _lanes(gal.shape)
            gal = gal + jnp.where(l1 == h, dal[h], 0.0)
            gdt = gdt + jnp.where(l1 == h, ddt[h], 0.0)
        dab_ref[...] = dab.astype(dab_ref.dtype)
        gal_ref[...] += gal
        gdt_ref[...] += gdt

    vec = pl.BlockSpec((1, 128), lambda i: (0, 0))
    hsp = pl.BlockSpec((GDN_C, GDN_W), lambda i: (i, 0))
    qsp = pl.BlockSpec((GDN_C, 3 * GDN_W), lambda i: (i, 0))
    return pl.pallas_call(
        body, name=name, grid=(N,),
        in_specs=[qsp, pl.BlockSpec((GDN_C, 128), lambda i: (i, PAB // 128)), vec, vec] + [hsp] * 5
        + [pl.BlockSpec((8, 128), lambda i: (i, 0))],
        out_specs=[qsp, pl.BlockSpec((GDN_C, 128), lambda i: (i, 0)), vec, vec],
        out_shape=[jax.ShapeDtypeStruct((S, 3 * GDN_W), F32), jax.ShapeDtypeStruct((S, 128), BF16),
                   jax.ShapeDtypeStruct((1, 128), F32), jax.ShapeDtypeStruct((1, 128), F32)],
        compiler_params=_cp("arbitrary"),
    )(qkv, proj, alog, dtb, dw, du, dqd, dkd, dat, dgl)


@jax.custom_vjp
def _expm1(x):
    u = jnp.exp(x)
    lu = jnp.log(u)
    small = (u - 1.0) * x / jnp.where(u == 1.0, 1.0, lu)
    small = jnp.where(u == 1.0, x, small)
    return jnp.where(jnp.abs(x) < 0.5, small, u - 1.0)


def _expm1_fwd(x):
    return _expm1(x), jnp.exp(x)


def _expm1_bwd(ex, g):
    return (g * ex,)


_expm1.defvjp(_expm1_fwd, _expm1_bwd)


def _lru_gates(xc, wa, ba, wx, bx, lam, first):
    r = jax.nn.sigmoid(_mm(xc, wa) + ba)
    i = jax.nn.sigmoid(_mm(xc, wx) + bx)
    log_a = -LRU_C * r * jax.nn.softplus(-lam)
    mult = jnp.sqrt(-_expm1(2.0 * log_a))
    mult = jnp.where(first, 1.0, mult)
    return jnp.exp(log_a), mult * i * xc


def _scan_fwd(a, b):
    T = a.shape[0]
    rows = _rows(a.shape)
    s = 1
    while s < T:
        ok = rows >= s
        b = a * jnp.where(ok, _down(b, s), 0.0) + b
        a = a * jnp.where(ok, _down(a, s), 1.0)
        s *= 2
    return a, b


def _scan_rev(a, b):
    T = a.shape[0]
    rows = _rows(a.shape)
    s = 1
    while s < T:
        ok = rows + s < T
        b = a * jnp.where(ok, _up(b, s), 0.0) + b
        a = a * jnp.where(ok, _up(a, s), 1.0)
        s *= 2
    return b


def _bsl(j):
    return slice(j * LRU_BD, (j + 1) * LRU_BD)


def _lru_tile(S):
    return _row_tile(S, 256)


def _lru_fwd(proj, conv_w, conv_b, wa, ba, wx, bx, lam, name):
    S = proj.shape[0]
    T = _lru_tile(S)
    taps = conv_w.shape[0]
    r = T // HALO

    def body(x_ref, xp_ref, cw_ref, cb_ref, wa_ref, ba_ref, wx_ref, bx_ref, lam_ref, h_ref, carry_ref):
        i = pl.program_id(0)

        @pl.when(i == 0)
        def _():
            carry_ref[...] = jnp.zeros_like(carry_ref)

        ext = jnp.concatenate([jnp.where(i > 0, xp_ref[...], 0.0), x_ref[...]], axis=0)
        xc = _conv_rows(ext, cw_ref, taps)[HALO:] + cb_ref[...]
        first = (i * T + _rows((T, LRU_BD))) == 0
        for j in range(LRU_NB):
            a, b = _lru_gates(xc[:, _bsl(j)], wa_ref[j], ba_ref[:, _bsl(j)], wx_ref[j], bx_ref[:, _bsl(j)],
                              lam_ref[:, _bsl(j)], first=first)
            pa, hb = _scan_fwd(a, b)
            h_ref[:, _bsl(j)] = pa * carry_ref[0:1, _bsl(j)] + hb
            carry_ref[0:1, _bsl(j)] = h_ref[T - 1:T, _bsl(j)]

    vec = pl.BlockSpec((1, LRU_W), lambda i: (0, 0))
    wsp = pl.BlockSpec((LRU_NB, LRU_BD, LRU_BD), lambda i: (0, 0, 0))
    return pl.pallas_call(
        body, name=name, grid=(S // T,),
        in_specs=[pl.BlockSpec((T, LRU_W), lambda i: (i, PXR // LRU_W)),
                  pl.BlockSpec((HALO, LRU_W), lambda i: (jnp.maximum(i * r - 1, 0), PXR // LRU_W)),
                  pl.BlockSpec((taps, LRU_W), lambda i: (0, 0)), vec, wsp, vec, wsp, vec, vec],
        out_specs=pl.BlockSpec((T, LRU_W), lambda i: (i, 0)), out_shape=jax.ShapeDtypeStruct((S, LRU_W), F32),
        scratch_shapes=[pltpu.VMEM((8, LRU_W), F32)],
        compiler_params=_cp("arbitrary"),
    )(proj, proj, conv_w, conv_b.reshape(1, LRU_W), wa, ba.reshape(1, LRU_W), wx, bx.reshape(1, LRU_W), lam.reshape(1, LRU_W))


def _lru_bwd(proj, hl, dmixed, conv_w, conv_b, wa, ba, wx, bx, lam, name):
    S = proj.shape[0]
    T = _lru_tile(S)
    n = S // T
    taps = conv_w.shape[0]
    r = T // HALO
    mb = 768 // LRU_W

    def body(x_ref, xp_ref, g_ref, h_ref, hp_ref, dy_ref, cw_ref, cb_ref, wa_ref, ba_ref, wx_ref, bx_ref, lam_ref,
             dx_ref, dg_ref, gcw_ref, gcb_ref, gwa_ref, gba_ref, gwx_ref, gbx_ref, glam_ref, carry_ref, dxc_ref, nxt_ref):
        s = pl.program_id(0)
        i = n - 1 - s

        @pl.when(s == 0)
        def _():
            carry_ref[...] = jnp.zeros_like(carry_ref)
            nxt_ref[...] = jnp.zeros_like(nxt_ref)
            for ref in (gcw_ref, gcb_ref, gwa_ref, gba_ref, gwx_ref, gbx_ref, glam_ref):
                ref[...] = jnp.zeros_like(ref)

        ext = jnp.concatenate([jnp.where(i > 0, xp_ref[...], 0.0), x_ref[...]], axis=0)
        xc = _conv_rows(ext, cw_ref, taps)[HALO:] + cb_ref[...]
        rows = _rows((T, LRU_BD))
        first = (i * T + rows) == 0
        h_before = jnp.where(i > 0, hp_ref[HALO - 1:HALO, :], 0.0)
        for j in range(LRU_NB):
            sl = _bsl(j)
            (a, _), vjp_g = jax.vjp(functools.partial(_lru_gates, first=first), xc[:, sl], wa_ref[j], ba_ref[:, sl],
                                    wx_ref[j], bx_ref[:, sl], lam_ref[:, sl])
            gelu_g, vjp_a = jax.vjp(jax.nn.gelu, g_ref[:, sl])
            h = h_ref[:, sl]
            dy = dy_ref[:, sl]
            dg_ref[:, sl] = vjp_a(dy * h)[0].astype(dg_ref.dtype)
            b_rev = dy * gelu_g + jnp.where(rows == T - 1, carry_ref[0:1, sl], 0.0)
            a_rev = jnp.where(rows == T - 1, 0.0, _up(a, 1))
            dh = _scan_rev(a_rev, b_rev)
            carry_ref[:, sl] = (a * dh)[:HALO]
            h_prev = jnp.where(rows == 0, h_before[:, sl], _down(h, 1))
            dxc, dwa, dba, dwx, dbx, dlam = vjp_g((dh * h_prev, dh))
            dxc_ref[:, sl] = dxc
            gwa_ref[j] += dwa
            gwx_ref[j] += dwx
            gba_ref[:, sl] += dba
            gbx_ref[:, sl] += dbx
            glam_ref[:, sl] += dlam
        dxc = dxc_ref[...]
        d_ext = jnp.concatenate([dxc, nxt_ref[...]], axis=0)
        dx_ref[...] = _conv_t_rows(d_ext, cw_ref, taps)[:T].astype(dx_ref.dtype)
        nxt_ref[...] = dxc[:HALO]
        gcb_ref[...] += jnp.sum(dxc, axis=0, keepdims=True)
        for k in range(taps):
            gcw_ref[taps - 1 - k:taps - k, :] += jnp.sum(dxc * _down(ext, k)[HALO:], axis=0, keepdims=True)

    vec = pl.BlockSpec((1, LRU_W), lambda s: (0, 0))
    wsp = pl.BlockSpec((LRU_NB, LRU_BD, LRU_BD), lambda s: (0, 0, 0))
    cwsp = pl.BlockSpec((taps, LRU_W), lambda s: (0, 0))

    def main(cb):
        return pl.BlockSpec((T, LRU_W), lambda s: (n - 1 - s, cb))

    def prev(cb):
        return pl.BlockSpec((HALO, LRU_W), lambda s: (jnp.maximum((n - 1 - s) * r - 1, 0), cb))

    vshape = jax.ShapeDtypeStruct((1, LRU_W), F32)
    wshape = jax.ShapeDtypeStruct((LRU_NB, LRU_BD, LRU_BD), F32)
    return pl.pallas_call(
        body, name=name, grid=(n,),
        in_specs=[main(PXR // LRU_W), prev(PXR // LRU_W), main(PGR // LRU_W), main(0), prev(0), main(mb),
                  cwsp, vec, wsp, vec, wsp, vec, vec],
        out_specs=[main(0), main(0), cwsp, vec, wsp, vec, wsp, vec, vec],
        out_shape=[jax.ShapeDtypeStruct((S, LRU_W), BF16), jax.ShapeDtypeStruct((S, LRU_W), BF16),
                   jax.ShapeDtypeStruct((taps, LRU_W), F32), vshape, wshape, vshape, wshape, vshape, vshape],
        scratch_shapes=[pltpu.VMEM((8, LRU_W), F32), pltpu.VMEM((T, LRU_W), F32), pltpu.VMEM((HALO, LRU_W), F32)],
        compiler_params=_cp("arbitrary"),
    )(proj, proj, proj, hl, hl, dmixed, conv_w, conv_b.reshape(1, LRU_W), wa, ba.reshape(1, LRU_W), wx,
      bx.reshape(1, LRU_W), lam.reshape(1, LRU_W))


def _mix_out(o, proj, hl, y_pool, nw, name):
    S = o.shape[0]
    T = _row_tile(S)

    def body(o_ref, z_ref, h_ref, g_ref, p_ref, nw_ref, m_ref):
        for h in range(GDN_H):
            m_ref[:, _hsl(h)] = _gated_norm(o_ref[:, _hsl(h)], z_ref[:, _hsl(h)], nw_ref[...]).astype(m_ref.dtype)
        m_ref[:, GDN_W:GDN_W + LRU_W] = (h_ref[...] * jax.nn.gelu(g_ref[...])).astype(m_ref.dtype)
        m_ref[:, GDN_W + LRU_W:] = p_ref[...].astype(m_ref.dtype)

    row = pl.BlockSpec((T, GDN_W), lambda i: (i, 0))
    return pl.pallas_call(
        body, name=name, grid=(S // T,),
        in_specs=[row, pl.BlockSpec((T, GDN_W), lambda i: (i, PZ // GDN_W)), row,
                  pl.BlockSpec((T, LRU_W), lambda i: (i, PGR // LRU_W)), pl.BlockSpec((T, POOL_W), lambda i: (i, 0)),
                  pl.BlockSpec((1, GDN_DH), lambda i: (0, 0))],
        out_specs=pl.BlockSpec((T, D_MODEL), lambda i: (i, 0)), out_shape=jax.ShapeDtypeStruct((S, D_MODEL), BF16),
        compiler_params=_cp("parallel"),
    )(o, proj, hl, proj, y_pool, nw)


def _as2d(a):
    return a.reshape(-1, a.shape[-1])


def _ew_rows(rows, cols):
    t = rows
    while t * cols * 4 > (2 << 20) and t % 16 == 0:
        t //= 2
    return t


def _rs_rows(rows, cols):
    t = rows
    while t * cols * 4 > (2 << 20) and t % 32 == 0:
        t //= 2
    return t


def _rs_add(src, recv, src_index, grid_lead, out_dtype, name, into=None):
    rows, cols = recv.shape[-2:]
    t = _rs_rows(rows, cols)
    nl = len(grid_lead)
    lead_none = (None,) * (src.ndim - 2)

    def body(s_ref, r_ref, *rest):
        o_ref = rest[-1]
        o_ref[...] = (s_ref[...].astype(F32) + r_ref[...].astype(F32)).astype(o_ref.dtype)

    def src_map(*a):
        return (*src_index(*a[:nl], _place()), a[nl], 0)

    def own_map(*a):
        return (*a[:nl], a[nl], 0)

    own = pl.BlockSpec(((None,) * nl) + (t, cols), own_map)
    in_specs = [pl.BlockSpec((*lead_none, t, cols), src_map), own]
    args = [src, recv]
    if into is None:
        out_spec, out_shape, aliases = own, jax.ShapeDtypeStruct(recv.shape, out_dtype), {}
    else:
        buf, layer = into
        shape = buf.shape
        args.append(buf.reshape((2, 2) + recv.shape))
        in_specs.append(pl.BlockSpec(memory_space=pl.ANY))
        out_spec = pl.BlockSpec(((None,) * (nl + 2)) + (t, cols), lambda *a: (layer, lax.axis_index("c"), *a[:nl], a[nl], 0))
        out_shape, aliases = jax.ShapeDtypeStruct((2, 2) + recv.shape, out_dtype), {2: 0}
    out = pl.pallas_call(
        body, name=name, grid=(*grid_lead, rows // t), in_specs=in_specs, out_specs=out_spec,
        out_shape=out_shape, input_output_aliases=aliases,
        compiler_params=pltpu.CompilerParams(dimension_semantics=("parallel",) * (nl + 1), vmem_limit_bytes=VMEM_LIMIT),
    )(*args)
    return out if into is None else out.reshape(shape)


def _add(a, b, name):
    shape = a.shape
    a2, b2 = _as2d(a), _as2d(b)
    rows, cols = a2.shape
    t = _ew_rows(rows, cols)

    def body(a_ref, b_ref, o_ref):
        o_ref[...] = a_ref[...] + b_ref[...]

    sp = pl.BlockSpec((t, cols), lambda i: (i, 0))
    return pl.pallas_call(body, name=name, grid=(rows // t,), in_specs=[sp, sp], out_specs=sp,
                          out_shape=jax.ShapeDtypeStruct((rows, cols), F32), compiler_params=_cp("parallel"))(a2, b2).reshape(shape)


def _adamw(w, g, m, v, name):
    shape = w.shape
    w2, g2, m2, v2 = _as2d(w), _as2d(g), _as2d(m), _as2d(v)
    rows, cols = w2.shape
    t = _ew_rows(rows, cols)

    def body(w_ref, g_ref, m_ref, v_ref, d_ref, nm_ref, nv_ref):
        gr = g_ref[...]
        nm = ADAM_B1 * m_ref[...] + (1.0 - ADAM_B1) * gr
        nv = ADAM_B2 * v_ref[...] + (1.0 - ADAM_B2) * (gr * gr)
        m_hat = nm / (1.0 - ADAM_B1 ** ADAM_STEP)
        v_hat = nv / (1.0 - ADAM_B2 ** ADAM_STEP)
        d_ref[...] = -ADAM_LR * (m_hat / (jnp.sqrt(v_hat) + ADAM_EPS) + ADAM_WD * w_ref[...])
        nm_ref[...] = nm
        nv_ref[...] = nv

    sp = pl.BlockSpec((t, cols), lambda i: (i, 0))
    sh = jax.ShapeDtypeStruct((rows, cols), F32)
    d, nm, nv = pl.pallas_call(body, name=name, grid=(rows // t,), in_specs=[sp] * 4, out_specs=[sp] * 3,
                               out_shape=[sh] * 3, compiler_params=_cp("parallel"))(w2, g2, m2, v2)
    return d.reshape(shape), nm.reshape(shape), nv.reshape(shape)


def _place():
    return lax.axis_index("x"), lax.axis_index("y"), lax.axis_index("c")


def _gather_weights(arrs, name):
    n = len(arrs)

    def body(*refs):
        outs = refs[n:2 * n]
        send, recv = refs[2 * n:]
        x, y, c = _place()
        s_me, s_x, s_y, s_d = 2 * x + y, 2 * (1 - x) + y, 2 * x + (1 - y), 2 * (1 - x) + (1 - y)
        xpeer, ypeer, sib = (1 - x, y, c), (x, 1 - y, c), (x, y, 1 - c)

        def rc(k, t, src, dst, to):
            return pltpu.make_async_remote_copy(src_ref=src, dst_ref=dst, send_sem=send.at[k, t], recv_sem=recv.at[k, t],
                                                device_id=to, device_id_type=MESH)

        def piece(k, s, top):
            rq = outs[k].shape[2] // 2
            return outs[k].at[s, c, pl.ds(0 if top else rq, rq)]

        sent = []

        def start(k, t, ref, to):
            cp = rc(k, t, ref, ref, to)
            cp.start()
            sent.append(cp)

        for k in range(n):
            start(k, 0, outs[k].at[s_me, c], xpeer)
            start(k, 1, outs[k].at[s_me, c], ypeer)
        for k in range(n):
            got = outs[k].at[s_x, c]
            rc(k, 0, got, got, xpeer).wait_recv()
            start(k, 2, piece(k, s_x, True), ypeer)
            start(k, 3, got, sib)
        for k in range(n):
            got = outs[k].at[s_y, c]
            rc(k, 1, got, got, ypeer).wait_recv()
            start(k, 6, piece(k, s_y, False), xpeer)
            start(k, 4, got, sib)
        for k in range(n):
            top, bottom = piece(k, s_d, True), piece(k, s_d, False)
            rc(k, 2, top, top, ypeer).wait_recv()
            rc(k, 6, bottom, bottom, xpeer).wait_recv()
            start(k, 5, outs[k].at[s_d, c], sib)
        for k in range(n):
            for t, s in ((3, s_x), (4, s_y), (5, s_d)):
                got = outs[k].at[s, 1 - c]
                rc(k, t, got, got, sib).wait_recv()
        for cp in sent:
            cp.wait_send()

    return pl.pallas_call(
        body, name=name, in_specs=[HBM] * n, out_specs=[HBM] * n,
        out_shape=[jax.ShapeDtypeStruct(a.shape, a.dtype) for a in arrs],
        input_output_aliases={k: k for k in range(n)},
        scratch_shapes=[pltpu.SemaphoreType.DMA((n, 7)), pltpu.SemaphoreType.DMA((n, 7))],
    )(*arrs)


def _exchange(arrs, axis, name, half=0):
    n = len(arrs)

    def body(*refs):
        srcs, outs = refs[:n], refs[n:2 * n]
        send, recv = refs[2 * n:]
        x, y, c = _place()
        p = {"x": x, "y": y, "c": c}[axis]
        peer = {"x": (1 - x, y, c), "y": (x, 1 - y, c), "c": (x, y, 1 - c)}[axis]
        cps = []
        for k in range(n):
            src = (srcs[k], srcs[k].at[1 - p], srcs[k].at[:, 1 - p])[half]
            cp = pltpu.make_async_remote_copy(src_ref=src, dst_ref=outs[k], send_sem=send.at[k], recv_sem=recv.at[k],
                                              device_id=peer, device_id_type=MESH)
            cp.start()
            cps.append(cp)
        for cp in cps:
            cp.wait()

    def out_shape(a):
        return (a.shape, a.shape[1:], a.shape[:1] + a.shape[2:])[half]

    return pl.pallas_call(
        body, name=name, in_specs=[HBM] * n, out_specs=[HBM] * n,
        out_shape=[jax.ShapeDtypeStruct(out_shape(a), a.dtype) for a in arrs],
        scratch_shapes=[pltpu.SemaphoreType.DMA((n,)), pltpu.SemaphoreType.DMA((n,))],
    )(*arrs)


def _share_halves(arrs, name):
    n = len(arrs)

    def body(*refs):
        outs = refs[n:2 * n]
        send, recv = refs[2 * n:]
        x, y, c = _place()
        cps = []
        for k in range(n):
            mine = outs[k].at[:, c]
            cp = pltpu.make_async_remote_copy(src_ref=mine, dst_ref=mine, send_sem=send.at[k], recv_sem=recv.at[k],
                                              device_id=(x, y, 1 - c), device_id_type=MESH)
            cp.start()
            cps.append(cp)
        for k in range(n):
            got = outs[k].at[:, 1 - c]
            pltpu.make_async_remote_copy(src_ref=got, dst_ref=got, send_sem=send.at[k], recv_sem=recv.at[k],
                                         device_id=(x, y, 1 - c), device_id_type=MESH).wait_recv()
        for cp in cps:
            cp.wait_send()

    return pl.pallas_call(
        body, name=name, in_specs=[HBM] * n, out_specs=[HBM] * n,
        out_shape=[jax.ShapeDtypeStruct(a.shape, a.dtype) for a in arrs],
        input_output_aliases={k: k for k in range(n)},
        scratch_shapes=[pltpu.SemaphoreType.DMA((n,)), pltpu.SemaphoreType.DMA((n,))],
    )(*arrs)


def _rs_exchange(arrs, phase, name):
    n = len(arrs)

    def body(*refs):
        srcs, outs = refs[:n], refs[n:2 * n]
        send, recv = refs[2 * n:]
        x, y, c = _place()
        xpeer, ypeer = (1 - x, y, c), (x, 1 - y, c)
        cps = []
        for k in range(n):
            if phase == 2:
                parts = ((srcs[k].at[1 - x, :, 0], xpeer), (srcs[k].at[:, 1 - y, 1], ypeer))
            else:
                parts = ((srcs[k].at[0, 1 - y], ypeer), (srcs[k].at[1, 1 - x], xpeer))
            for h, (src, to) in enumerate(parts):
                cp = pltpu.make_async_remote_copy(src_ref=src, dst_ref=outs[k].at[h], send_sem=send.at[k, h],
                                                  recv_sem=recv.at[k, h], device_id=to, device_id_type=MESH)
                cp.start()
                cps.append(cp)
        for cp in cps:
            cp.wait()

    def out_shape(a):
        return (2, 2) + a.shape[3:] if phase == 2 else (2,) + a.shape[2:]

    return pl.pallas_call(
        body, name=name, in_specs=[HBM] * n, out_specs=[HBM] * n,
        out_shape=[jax.ShapeDtypeStruct(out_shape(a), a.dtype) for a in arrs],
        scratch_shapes=[pltpu.SemaphoreType.DMA((n, 2)), pltpu.SemaphoreType.DMA((n, 2))],
    )(*arrs)


_REL = tuple((dx, dy, dc) for dx in (0, 1) for dy in (0, 1) for dc in (0, 1))[1:]
SEM = pl.BlockSpec(memory_space=pltpu.SEMAPHORE)
DATAFLOW = pltpu.SideEffectType.DATAFLOW_SIDE_EFFECTING


def _flip(v, d):
    return 1 - v if d else v


def _rs_direct_copies(srcs, land, send, recv):
    x, y, c = _place()
    cps = []
    for k in range(len(srcs)):
        for r, (dx, dy, dc) in enumerate(_REL):
            px, py, pc = _flip(x, dx), _flip(y, dy), _flip(c, dc)
            cps.append(pltpu.make_async_remote_copy(
                src_ref=srcs[k].at[2 * px + py, pc], dst_ref=land[k].at[r], send_sem=send.at[k * len(_REL) + r],
                recv_sem=recv.at[k * len(_REL) + r], device_id=(px, py, pc), device_id_type=MESH))
    return cps


def _rs_direct_start(grads, thru, name):
    n = len(grads)
    lands = [pltpu.with_memory_space_constraint(lax.empty((len(_REL),) + g.shape[2:], g.dtype), pltpu.HBM) for g in grads]

    def body(*refs):
        for cp in _rs_direct_copies(refs[:n], refs[n + 1:2 * n + 1], refs[2 * n + 1], refs[2 * n + 2]):
            cp.start()

    sems = pltpu.SemaphoreType.DMA((n * len(_REL),))
    keep = [pltpu.HBM(a.shape, a.dtype) for a in (*grads, thru, *lands)]
    out = pl.pallas_call(
        body, name=name, in_specs=[HBM] * (2 * n + 1), out_specs=(SEM, SEM) + (HBM,) * (2 * n + 1),
        out_shape=(sems, sems, *keep), input_output_aliases={i: 2 + i for i in range(2 * n + 1)},
        compiler_params=pltpu.CompilerParams(has_side_effects=DATAFLOW),
    )(*[pltpu.with_memory_space_constraint(a, pltpu.HBM) for a in (*grads, thru)], *lands)
    return out[0], out[1], out[2:2 + n], out[2 + n], out[3 + n:]


def _rs_direct_wait(send, recv, grads, lands, after, name):
    n = len(grads)

    def body(*refs):
        for cp in _rs_direct_copies(refs[:n], refs[n:2 * n], refs[2 * n], refs[2 * n + 1]):
            cp.wait_send()
            cp.wait_recv()

    keep = [pltpu.HBM(a.shape, a.dtype) for a in (*grads, *lands)]
    out = pl.pallas_call(
        body, name=name, in_specs=[HBM] * (2 * n) + [SEM, SEM, pl.BlockSpec(memory_space=pl.ANY)], out_specs=(HBM,) * (2 * n),
        out_shape=tuple(keep), input_output_aliases={i: i for i in range(2 * n)},
        compiler_params=pltpu.CompilerParams(has_side_effects=DATAFLOW),
    )(*grads, *lands, send, recv, after)
    return out[:n], out[n:]


_REL6 = tuple(r for r in _REL if r[0] or r[1])


def _gather_direct_copies(bufs, send, recv):
    x, y, c = _place()
    cps = []
    for k in range(len(bufs)):
        mine = bufs[k].at[2 * x + y, c]
        for r, (dx, dy, dc) in enumerate(_REL6):
            cps.append(pltpu.make_async_remote_copy(
                src_ref=mine, dst_ref=mine, send_sem=send.at[k * len(_REL6) + r], recv_sem=recv.at[k * len(_REL6) + r],
                device_id=(_flip(x, dx), _flip(y, dy), _flip(c, dc)), device_id_type=MESH))
    return cps


def _gather_direct_start(bufs, thru, name):
    n = len(bufs)

    def body(*refs):
        for cp in _gather_direct_copies(refs[:n], refs[n + 1], refs[n + 2]):
            cp.start()

    sems = pltpu.SemaphoreType.DMA((n * len(_REL6),))
    out = pl.pallas_call(
        body, name=name, in_specs=[HBM] * (n + 1), out_specs=(SEM, SEM) + (HBM,) * (n + 1),
        out_shape=(sems, sems, *[pltpu.HBM(a.shape, a.dtype) for a in (*bufs, thru)]),
        input_output_aliases={i: 2 + i for i in range(n + 1)},
        compiler_params=pltpu.CompilerParams(has_side_effects=DATAFLOW),
    )(*[pltpu.with_memory_space_constraint(a, pltpu.HBM) for a in (*bufs, thru)])
    return out[0], out[1], out[2:2 + n], out[2 + n]


def _gather_direct_wait(send, recv, bufs, after, name):
    n = len(bufs)

    def body(*refs):
        for cp in _gather_direct_copies(refs[:n], refs[n], refs[n + 1]):
            cp.wait_send()
            cp.wait_recv()

    return pl.pallas_call(
        body, name=name, in_specs=[HBM] * n + [SEM, SEM, pl.BlockSpec(memory_space=pl.ANY)], out_specs=(HBM,) * n,
        out_shape=tuple(pltpu.HBM(a.shape, a.dtype) for a in bufs), input_output_aliases={i: i for i in range(n)},
        compiler_params=pltpu.CompilerParams(has_side_effects=DATAFLOW),
    )(*bufs, send, recv, after)


def _rs_direct_sum(grad, land, layer, name, into=None):
    _, _, rows, cols = grad.shape
    t = _rs_rows(rows, cols)
    npieces = len(_REL) + 1

    def body(g_ref, l_ref, *rest):
        o_ref, acc_ref = rest[-2], rest[-1]
        j = pl.program_id(1)

        @pl.when(j == 0)
        def _():
            acc_ref[...] = g_ref[...].astype(F32)

        @pl.when(j > 0)
        def _():
            acc_ref[...] += l_ref[...].astype(F32)

        @pl.when(j == npieces - 1)
        def _():
            o_ref[...] = acc_ref[...]

    def mine(i, j):
        x, y, c = _place()
        return (2 * x + y, c, i, 0)

    in_specs = [pl.BlockSpec((None, None, t, cols), mine),
                pl.BlockSpec((None, t, cols), lambda i, j: (jnp.maximum(j - 1, 0), i, 0))]
    args = [grad, land]
    if into is not None:
        in_specs.append(pl.BlockSpec(memory_space=pl.ANY))
        args.append(into)
    return pl.pallas_call(
        body, name=name, grid=(rows // t, npieces), in_specs=in_specs,
        out_specs=pl.BlockSpec((None, None, t, cols), lambda i, j: (layer, lax.axis_index("c"), i, 0)),
        scratch_shapes=[pltpu.VMEM((t, cols), F32)],
        out_shape=jax.ShapeDtypeStruct((2, 2, rows, cols), F32), input_output_aliases={} if into is None else {2: 0},
        compiler_params=_cp("parallel", "arbitrary"),
    )(*args)


def _rs_tree(grads, into, layer):
    r1 = _exchange(grads, "c", "rs_c", half=2)
    a1 = [_rs_add(g, r, lambda s, p: (s, p[2]), (4,), BF16, f"rs_add1_{k}") for k, (g, r) in enumerate(zip(grads, r1))]
    a1 = [a.reshape(2, 2, 2, a.shape[1] // 2, a.shape[2]) for a in a1]
    r2 = _rs_exchange(a1, 2, "rs_p2")
    a2 = [_rs_add(a, r, lambda h, j, p: (p[0] * (1 - h) + j * h, j * (1 - h) + p[1] * h, h), (2, 2), BF16, f"rs_add2_{k}")
          for k, (a, r) in enumerate(zip(a1, r2))]
    r3 = _rs_exchange(a2, 3, "rs_p3")
    return [_rs_add(a, r, lambda h, p: (h, p[1] * (1 - h) + p[0] * h), (2,), F32, f"rs_add3_{k}", into=(buf, layer))
            for k, (a, r, buf) in enumerate(zip(a2, r3, into))]


def _all_reduce(buf):
    for axis in ("c", "x", "y"):
        (other,) = _exchange([buf], axis, f"ar_{axis}")
        buf = _add(buf, other, f"ar_add_{axis}")
    return buf


def _pad128(v):
    return jnp.zeros((1, 128), F32).at[0, :v.shape[0]].set(v)


def _layer_fwd(l, x, p):
    h1 = _rms_fwd(x, p["norm1_w"], f"rms1_{l}")
    proj = _matmul(h1, p["w_in"], "nn", name=f"mm_in_{l}", tn=768)
    y_pool = _pool_fwd(proj, p["pool_w"], p["pool_b"], p["pool_scale"], f"pool_{l}")
    qkv = _gdn_conv_fwd(proj, p["gdn_conv_w"], f"gconv_{l}")
    alog, dtb = _pad128(p["gdn_a_log"]), _pad128(p["gdn_dt_bias"])
    gw, gu, gqd, gkd, gat, gc = _gdn_prep(qkv, proj, alog, dtb, f"gprep_{l}")
    o, states = _gdn_scan(gw, gu, gqd, gkd, gat, gc, f"gscan_{l}")
    hl = _lru_fwd(proj, p["lru_conv_w"], p["lru_conv_b"], p["lru_wa"], p["lru_ba"], p["lru_wx"], p["lru_bx"],
                  p["lru_lambda"], f"lru_{l}")
    mixed = _mix_out(o, proj, hl, y_pool, p["gdn_norm_w"].reshape(1, GDN_DH), f"mix_{l}")
    x2 = _matmul(mixed, p["w_out"], "nn", name=f"mm_out_{l}", res=x)
    h2 = _rms_fwd(x2, p["norm2_w"], f"rms2_{l}")
    up = _matmul(h2, p["ffn_up"], "nn", name=f"mm_up_{l}", b_split=True)
    act = _ffn_act_fwd(up, p["ffn_conv_w"], f"ffn_{l}")
    x3 = _matmul(act, p["ffn_down"], "nn", name=f"mm_down_{l}", res=x2)
    saved = dict(x=x, h1=h1, proj=proj, qkv=qkv, gdn=(gw, gu, gqd, gkd, gat, gc), states=states, o=o, hl=hl, mixed=mixed,
                 x2=x2, h2=h2, up=up, act=act, alog=alog, dtb=dtb)
    return x3, saved


def _layer_bwd(l, dx3, p, s, after_ffn=None):
    g = {}
    dact = _matmul(dx3, p["ffn_down"], "nt", name=f"mm_ddown_{l}")
    g["ffn_down"] = _matmul(s["act"], dx3, "tn", name=f"mm_gdown_{l}", out_dtype=BF16)
    dup, g["ffn_conv_w"] = _ffn_act_bwd(s["up"], dact, p["ffn_conv_w"], f"ffn_b_{l}")
    dh2 = _matmul(dup, p["ffn_up"], "nt", name=f"mm_dup_{l}", b_split=True, tk=3072)
    g["ffn_up"] = _matmul(s["h2"], dup, "tn", name=f"mm_gup_{l}", b_split=True, o_split=4, tk=4096, out_dtype=BF16)
    dx2, g["norm2_w"] = _rms_bwd(s["x2"], p["norm2_w"], dh2, dx3, f"rms2_b_{l}")
    if after_ffn is not None:
        dx2 = after_ffn(dx2, g)
    dmixed = _matmul(dx2, p["w_out"], "nt", name=f"mm_dout_{l}")
    g["w_out"] = _matmul(s["mixed"], dx2, "tn", name=f"mm_gout_{l}", out_dtype=BF16)
    proj = s["proj"]
    du_pool, g["pool_w"], g["pool_b"], g["pool_scale"] = _pool_bwd(proj, dmixed, p["pool_w"], p["pool_b"], p["pool_scale"], f"pool_b_{l}")
    gw, gu, gqd, gkd, gat, gc = s["gdn"]
    dw, du, dqd, dkd, dat, dgl, dz, g["gdn_norm_w"] = _gdn_scan_bwd(
        gw, gu, gqd, gkd, gat, gc, s["states"], s["o"], proj, dmixed, p["gdn_norm_w"].reshape(1, GDN_DH), f"gscan_b_{l}")
    dqkv, dab, gal, gdt = _gdn_prep_bwd(s["qkv"], proj, s["alog"], s["dtb"], dw, du, dqd, dkd, dat, dgl, f"gprep_b_{l}")
    g["gdn_a_log"], g["gdn_dt_bias"] = gal[0, :GDN_H], gdt[0, :GDN_H]
    dpre, g["gdn_conv_w"] = _gdn_conv_bwd(proj, dqkv, p["gdn_conv_w"], f"gconv_b_{l}")
    (dxr, dgr, g["lru_conv_w"], g["lru_conv_b"], g["lru_wa"], g["lru_ba"], g["lru_wx"], g["lru_bx"], g["lru_lambda"]) = _lru_bwd(
        proj, s["hl"], dmixed, p["lru_conv_w"], p["lru_conv_b"], p["lru_wa"], p["lru_ba"], p["lru_wx"], p["lru_bx"],
        p["lru_lambda"], f"lru_b_{l}")
    S = proj.shape[0]
    dproj = jnp.concatenate([dpre, dz, dxr, dgr, du_pool, dab, jnp.zeros((S, PCOLS - PAB - 128), BF16)], axis=1)
    dh1 = _matmul(dproj, p["w_in"], "nt", name=f"mm_din_{l}", tk=1792)
    g["w_in"] = _matmul(s["h1"], dproj, "tn", name=f"mm_gin_{l}", tn=768, tk=4096, out_dtype=BF16)
    dx, g["norm1_w"] = _rms_bwd(s["x"], p["norm1_w"], dh1, dx2, f"rms1_b_{l}")
    return dx, g


_IN_PERM = ((512, 3584), (3596, 5132), (0, 512), (3584, 3596))


def _w_in_to_proj(w):
    parts = [w[:, a:b] for a, b in _IN_PERM]
    return jnp.concatenate(parts + [jnp.zeros((w.shape[0], PCOLS - IN_COLS), w.dtype)], axis=1)


def _proj_to_w_in(g):
    return jnp.concatenate([g[:, PPOOL:PPOOL + 512], g[:, 0:3072], g[:, PAB:PAB + 12], g[:, 3072:PPOOL]], axis=1)


def _rows_to_mixed(w):
    return jnp.concatenate([w[512:], w[:512]], axis=0)


def _mixed_to_rows(g):
    return jnp.concatenate([g[1536:], g[:1536]], axis=0)


SMALL_SHARDED = ("gdn_conv_w", "lru_conv_w", "ffn_conv_w")
BIG = ("w_in", "w_out", "ffn_up", "ffn_down")
SMALL_REPLICATED = ("norm1_w", "pool_w", "pool_b", "pool_scale", "gdn_a_log", "gdn_dt_bias", "gdn_norm_w", "lru_conv_b",
                    "lru_wa", "lru_ba", "lru_wx", "lru_bx", "lru_lambda", "norm2_w")
WEIGHTS = ("norm1_w", "w_in", "pool_w", "pool_b", "pool_scale", "gdn_conv_w", "gdn_a_log", "gdn_dt_bias", "gdn_norm_w",
           "lru_conv_w", "lru_conv_b", "lru_wa", "lru_ba", "lru_wx", "lru_bx", "lru_lambda", "w_out", "norm2_w", "ffn_up",
           "ffn_conv_w", "ffn_down", "final_norm_w")
FLAT_COLS = 1024


def _pack(arrs):
    flat = jnp.concatenate([a.reshape(-1) for a in arrs])
    rows = -(-flat.shape[0] // (8 * FLAT_COLS)) * 8
    return jnp.pad(flat, (0, rows * FLAT_COLS - flat.shape[0])).reshape(rows, FLAT_COLS)


def _unpack(buf, like):
    flat = buf.reshape(-1)
    out, off = [], 0
    for a in like:
        size = 1
        for d in a.shape:
            size *= d
        out.append(flat[off:off + size].reshape(a.shape))
        off += size
    return out


def kernel(x, norm1_w, w_in, pool_w, pool_b, pool_scale, gdn_conv_w, gdn_a_log, gdn_dt_bias, gdn_norm_w, lru_conv_w, lru_conv_b, lru_wa, lru_ba, lru_wx, lru_bx, lru_lambda, w_out, norm2_w, ffn_up, ffn_conv_w, ffn_down, final_norm_w, loss_target, m_norm1_w, m_w_in, m_pool_w, m_pool_b, m_pool_scale, m_gdn_conv_w, m_gdn_a_log, m_gdn_dt_bias, m_gdn_norm_w, m_lru_conv_w, m_lru_conv_b, m_lru_wa, m_lru_ba, m_lru_wx, m_lru_bx, m_lru_lambda, m_w_out, m_norm2_w, m_ffn_up, m_ffn_conv_w, m_ffn_down, m_final_norm_w, v_norm1_w, v_w_in, v_pool_w, v_pool_b, v_pool_scale, v_gdn_conv_w, v_gdn_a_log, v_gdn_dt_bias, v_gdn_norm_w, v_lru_conv_w, v_lru_conv_b, v_lru_wa, v_lru_ba, v_lru_wx, v_lru_bx, v_lru_lambda, v_w_out, v_norm2_w, v_ffn_up, v_ffn_conv_w, v_ffn_down, v_final_norm_w):
    W = dict(norm1_w=norm1_w, w_in=w_in, pool_w=pool_w, pool_b=pool_b, pool_scale=pool_scale, gdn_conv_w=gdn_conv_w,
             gdn_a_log=gdn_a_log, gdn_dt_bias=gdn_dt_bias, gdn_norm_w=gdn_norm_w, lru_conv_w=lru_conv_w, lru_conv_b=lru_conv_b,
             lru_wa=lru_wa, lru_ba=lru_ba, lru_wx=lru_wx, lru_bx=lru_bx, lru_lambda=lru_lambda, w_out=w_out, norm2_w=norm2_w,
             ffn_up=ffn_up, ffn_conv_w=ffn_conv_w, ffn_down=ffn_down, final_norm_w=final_norm_w)
    M = dict(norm1_w=m_norm1_w, w_in=m_w_in, pool_w=m_pool_w, pool_b=m_pool_b, pool_scale=m_pool_scale, gdn_conv_w=m_gdn_conv_w,
             gdn_a_log=m_gdn_a_log, gdn_dt_bias=m_gdn_dt_bias, gdn_norm_w=m_gdn_norm_w, lru_conv_w=m_lru_conv_w,
             lru_conv_b=m_lru_conv_b, lru_wa=m_lru_wa, lru_ba=m_lru_ba, lru_wx=m_lru_wx, lru_bx=m_lru_bx, lru_lambda=m_lru_lambda,
             w_out=m_w_out, norm2_w=m_norm2_w, ffn_up=m_ffn_up, ffn_conv_w=m_ffn_conv_w, ffn_down=m_ffn_down,
             final_norm_w=m_final_norm_w)
    V = dict(norm1_w=v_norm1_w, w_in=v_w_in, pool_w=v_pool_w, pool_b=v_pool_b, pool_scale=v_pool_scale, gdn_conv_w=v_gdn_conv_w,
             gdn_a_log=v_gdn_a_log, gdn_dt_bias=v_gdn_dt_bias, gdn_norm_w=v_gdn_norm_w, lru_conv_w=v_lru_conv_w,
             lru_conv_b=v_lru_conv_b, lru_wa=v_lru_wa, lru_ba=v_lru_ba, lru_wx=v_lru_wx, lru_bx=v_lru_bx, lru_lambda=v_lru_lambda,
             w_out=v_w_out, norm2_w=v_norm2_w, ffn_up=v_ffn_up, ffn_conv_w=v_ffn_conv_w, ffn_down=v_ffn_down,
             final_norm_w=v_final_norm_w)
    S = x.shape[1]
    xs = x.reshape(S, D_MODEL)
    tgt = loss_target.reshape(S, D_MODEL)
    mx, my, mc = _place()
    shard = 2 * mx + my

    small_sh = jnp.concatenate([W[k].reshape(N_LAYERS, -1) for k in SMALL_SHARDED], axis=1)
    n_small = small_sh.shape[1]
    pad = -n_small % 1024
    small_sh = jnp.pad(small_sh, ((0, 0), (0, pad))).reshape(N_LAYERS, -1, 1024)

    def own_slots(l):
        out = []
        for w in [W[k][l].astype(BF16) for k in BIG] + [small_sh[l]]:
            zeros = jnp.zeros((4,) + w.shape, w.dtype)
            buf = lax.dynamic_update_slice(zeros, w[None], (shard,) + (0,) * w.ndim)
            out.append(buf.reshape(4, 2, w.shape[0] // 2, w.shape[1]))
        return out

    def layer_params(l, gathered):
        g_in, g_out, g_up, g_down, g_small = [g.reshape(4, 2 * g.shape[2], g.shape[3]) for g in gathered]
        p = {k: W[k][l] for k in SMALL_REPLICATED}
        p["w_in"] = _w_in_to_proj(jnp.transpose(g_in, (1, 0, 2)).reshape(g_in.shape[1], IN_COLS))
        p["w_out"] = _rows_to_mixed(g_out.reshape(D_MODEL, D_MODEL))
        p["ffn_up"] = g_up
        p["ffn_down"] = g_down.reshape(D_FF, D_MODEL)
        g_small = g_small.reshape(4, -1)[:, :n_small]
        off = 0
        for k in SMALL_SHARDED:
            taps, width = W[k].shape[1], W[k].shape[2]
            piece = g_small[:, off:off + taps * width].reshape(4, taps, width)
            p[k] = jnp.transpose(piece, (1, 0, 2)).reshape(taps, 4 * width)
            off += taps * width
        return p

    layers, saved = [None] * N_LAYERS, [None] * N_LAYERS
    gathered0 = _gather_weights(own_slots(0), "gather_weights")
    g_send, g_recv, bufs1, first0 = _gather_direct_start(own_slots(1), gathered0[0], "gather_direct_start")
    layers[0] = layer_params(0, [first0, *gathered0[1:]])
    h, saved[0] = _layer_fwd(0, xs, layers[0])
    layers[1] = layer_params(1, _gather_direct_wait(g_send, g_recv, bufs1, h, "gather_direct_wait"))
    h, saved[1] = _layer_fwd(1, h, layers[1])
    loss_part, dh, g_final = _loss_head(h, final_norm_w, tgt, "loss_head")

    def big_partials(g_layer, names=BIG):
        out = []
        for k in names:
            g = g_layer[k]
            if k == "w_in":
                g = _proj_to_w_in(g)
                g = jnp.transpose(g.reshape(g.shape[0], 4, IN_COLS // 4), (1, 0, 2))
            elif k == "w_out":
                g = _mixed_to_rows(g).reshape(4, D_MODEL // 4, D_MODEL)
            elif k == "ffn_down":
                g = g.reshape(4, D_FF // 4, D_MODEL)
            out.append(g.reshape(4, 2, g.shape[1] // 2, g.shape[2]))
        return out

    FFN, MIX = ("ffn_up", "ffn_down"), ("w_in", "w_out")
    grads = [None] * N_LAYERS
    dh, grads[1] = _layer_bwd(1, dh, layers[1], saved[1])
    send1, recv1, part1, dh, lands1 = _rs_direct_start(big_partials(grads[1]), dh, "rs_direct_start_1")
    sent0 = []

    def after_ffn(dx2, g):
        send0, recv0, part0, dx2, lands0 = _rs_direct_start(big_partials(g, FFN), dx2, "rs_direct_start_0")
        sent0.extend([send0, recv0, part0, lands0])
        return dx2

    dh, grads[0] = _layer_bwd(0, dh, layers[0], saved[0], after_ffn)
    part1, lands1 = _rs_direct_wait(send1, recv1, part1, lands1, dh, "rs_direct_wait_1")
    part0, lands0 = _rs_direct_wait(*sent0, dh, "rs_direct_wait_0")
    red = {k: _rs_direct_sum(g, ld, 1, f"rs_sum_1_{k}") for k, g, ld in zip(BIG, part1, lands1)}
    for k, g, ld in zip(FFN, part0, lands0):
        red[k] = _rs_direct_sum(g, ld, 0, f"rs_sum_0_{k}", into=red[k])
    for k, r in zip(MIX, _rs_tree(big_partials(grads[0], MIX), [red[k] for k in MIX], 0)):
        red[k] = r
    reduced = _share_halves([red[k] for k in BIG], "rs_share")
    big_g = {k: r.reshape(N_LAYERS, 2 * r.shape[2], r.shape[3]) for k, r in zip(BIG, reduced)}
    grad_x = dh.reshape(x.shape)

    small_names = SMALL_REPLICATED + SMALL_SHARDED
    small_list = [jnp.stack([grads[l][k].reshape(W[k].shape[1:]) if k in SMALL_REPLICATED else grads[l][k] for l in range(N_LAYERS)])
                  for k in small_names]
    small_list += [g_final.reshape(D_MODEL), loss_part[0, 0:1]]
    reduced = _unpack(_all_reduce(_pack(small_list)), small_list)
    small_g = dict(zip(small_names, reduced[:len(small_names)]))
    small_g["final_norm_w"] = reduced[-2]
    loss = reduced[-1][0]
    for k in SMALL_SHARDED:
        width = W[k].shape[2]
        small_g[k] = lax.dynamic_slice_in_dim(small_g[k], shard * width, width, axis=2)

    G, DELTA, NM, NV = {}, {}, {}, {}
    for k in BIG:
        G[k] = big_g[k]
        DELTA[k], NM[k], NV[k] = _adamw(W[k], G[k], M[k], V[k], f"adam_{k}")
    small_all = small_names + ("final_norm_w",)
    dl, nm, nv = _adamw(_pack([W[k] for k in small_all]), _pack([small_g[k] for k in small_all]),
                        _pack([M[k] for k in small_all]), _pack([V[k] for k in small_all]), "adam_small")
    like = [W[k] for k in small_all]
    for k, d_, m_, v_ in zip(small_all, _unpack(dl, like), _unpack(nm, like), _unpack(nv, like)):
        G[k], DELTA[k], NM[k], NV[k] = small_g[k], d_, m_, v_

    return (loss, grad_x, *[G[k] for k in WEIGHTS], *[DELTA[k] for k in WEIGHTS], *[NM[k] for k in WEIGHTS],
            *[NV[k] for k in WEIGHTS])
```

```python
import functools

import jax
import jax.numpy as jnp
from jax import lax
from jax.experimental import pallas as pl
from jax.experimental.pallas import tpu as pltpu

F32 = jnp.float32
BF16 = jnp.bfloat16
_MXU = jnp.bfloat16

D_MODEL = 2048
N_LAYERS = 2
POOL_W = 512
POOL_G = 4
POOL_GD = 128
POOL_WINDOWS = (2, 4, 8, 16)
POOL_HALO = 16
GDN_W = 768
GDN_H = 6
GDN_DH = 128
GDN_C = 64
LRU_W = 768
LRU_NB = 6
LRU_BD = 128
LRU_C = 8.0
D_FF = 6144
EPS = 1e-6
IN_COLS = 5132
HALO = 8

PQ, PK, PV, PZ, PXR, PGR, PPOOL, PAB, PCOLS = 0, 768, 1536, 2304, 3072, 3840, 4608, 5120, 5376
CB = 768

ADAM_LR = 0.001
ADAM_B1 = 0.9
ADAM_B2 = 0.999
ADAM_EPS = 1e-08
ADAM_WD = 0.01
ADAM_STEP = 10

VMEM_LIMIT = 56 * 1024 * 1024
MESH = pl.DeviceIdType.MESH
HBM = pl.BlockSpec(memory_space=pltpu.HBM)


def _cp(*sem):
    return pltpu.CompilerParams(dimension_semantics=sem, vmem_limit_bytes=VMEM_LIMIT)


def _dg(a, b, ta, tb):
    dims = (((0 if ta else 1,), (1 if tb else 0,)), ((), ()))
    return lax.dot_general(a, b, dims, preferred_element_type=F32)


def _split2(a):
    hi = a.astype(BF16)
    lo = (a - hi.astype(F32)).astype(BF16)
    return hi, lo


def _mm_raw(a, b, ta, tb, hi):
    if _MXU == F32:
        return _dg(a, b, ta, tb)
    if not hi:
        return _dg(a.astype(_MXU), b.astype(_MXU), ta, tb)
    a1, a2 = _split2(a)
    b1, b2 = _split2(b)
    return _dg(a1, b1, ta, tb) + (_dg(a1, b2, ta, tb) + _dg(a2, b1, ta, tb))


@functools.partial(jax.custom_vjp, nondiff_argnums=(2, 3, 4))
def _mm(a, b, ta=False, tb=False, hi=False):
    return _mm_raw(a, b, ta, tb, hi)


def _mm_fwd(a, b, ta, tb, hi):
    return _mm_raw(a, b, ta, tb, hi), (a, b)


def _mm_bwd(ta, tb, hi, res, dc):
    a, b = res
    da = _mm(b, dc, tb, True, hi) if ta else _mm(dc, b, False, not tb, hi)
    db = _mm(dc, a, True, ta, hi) if tb else _mm(a, dc, not ta, False, hi)
    return da, db


_mm.defvjp(_mm_fwd, _mm_bwd)


def _mm01(m01, x):
    if _MXU == F32:
        return _dg(m01, x, False, False)
    m = m01.astype(BF16)
    x1 = x.astype(BF16)
    r = x - x1.astype(F32)
    x2 = r.astype(BF16)
    x3 = (r - x2.astype(F32)).astype(BF16)
    return _dg(m, x1, False, False) + (_dg(m, x2, False, False) + _dg(m, x3, False, False))


def _down(x, k):
    return x if k == 0 else pltpu.roll(x, k, 0)


def _up(x, k):
    return x if k == 0 else pltpu.roll(x, x.shape[0] - k, 0)


def _rows(shape):
    return lax.broadcasted_iota(jnp.int32, shape, 0)


def _lanes(shape):
    return lax.broadcasted_iota(jnp.int32, shape, 1)


def _matmul(a, b, mode, *, name, res=None, tm=1024, tn=1024, tk=2048, b_split=False, o_split=0, out_dtype=F32):
    ta, tb = mode == "tn", mode == "nt"
    a_split = a.ndim == 3
    if a_split:
        assert not ta
        M, K = a.shape[1], a.shape[0] * a.shape[2]
        tk = min(tk, a.shape[2])
    elif ta:
        K, M = a.shape
    else:
        M, K = a.shape
    if b_split:
        ns = b.shape[0]
        N = b.shape[1] if tb else ns * b.shape[2]
    else:
        N = b.shape[0] if tb else b.shape[1]
    tm, tn, tk = min(tm, M), min(tn, N), min(tk, K)
    if b_split:
        per = b.shape[2]
        if tb:
            tk = min(tk, per)
        else:
            tn = min(tn, per)
    if o_split:
        tn = min(tn, N // o_split)
    assert M % tm == 0 and N % tn == 0 and K % tk == 0, (name, M, N, K, tm, tn, tk)
    nk = K // tk
    if a_split:
        ka = a.shape[2] // tk
        a_spec = pl.BlockSpec((None, tm, tk), lambda i, j, k: (k // ka, i, k % ka))
    else:
        a_spec = pl.BlockSpec((tk, tm), lambda i, j, k: (k, i)) if ta else pl.BlockSpec((tm, tk), lambda i, j, k: (i, k))
    if not b_split:
        b_spec = pl.BlockSpec((tn, tk), lambda i, j, k: (j, k)) if tb else pl.BlockSpec((tk, tn), lambda i, j, k: (k, j))
    elif tb:
        kb = per // tk
        b_spec = pl.BlockSpec((None, tn, tk), lambda i, j, k: (k // kb, j, k % kb))
    else:
        nb = per // tn
        b_spec = pl.BlockSpec((None, tk, tn), lambda i, j, k: (j // nb, k, j % nb))
    if o_split:
        ob = (N // o_split) // tn
        out_shape = jax.ShapeDtypeStruct((o_split, M, N // o_split), out_dtype)
        o_spec = pl.BlockSpec((None, tm, tn), lambda i, j, k: (j // ob, i, j % ob))
    else:
        out_shape = jax.ShapeDtypeStruct((M, N), out_dtype)
        o_spec = pl.BlockSpec((tm, tn), lambda i, j, k: (i, j))
    in_specs = [a_spec, b_spec]
    args = [a, b]
    if res is not None:
        in_specs.append(pl.BlockSpec((tm, tn), lambda i, j, k: (i, j)))
        args.append(res)
    use_acc = nk > 1 and out_dtype != F32

    def body(*refs):
        a_ref, b_ref = refs[0], refs[1]
        o_ref = refs[2 + (res is not None)]
        acc_ref = refs[-1] if use_acc else o_ref
        p = _dg(a_ref[...].astype(_MXU), b_ref[...].astype(_MXU), ta, tb)
        first = p + refs[2][...] if res is not None else p
        if nk == 1:
            o_ref[...] = first.astype(o_ref.dtype)
        else:
            k = pl.program_id(2)

            @pl.when(k == 0)
            def _():
                acc_ref[...] = first

            @pl.when(k > 0)
            def _():
                acc_ref[...] += p

            if use_acc:
                @pl.when(k == nk - 1)
                def _():
                    o_ref[...] = acc_ref[...].astype(o_ref.dtype)

    return pl.pallas_call(
        body, name=name, grid=(M // tm, N // tn, nk), in_specs=in_specs, out_specs=o_spec, out_shape=out_shape,
        scratch_shapes=[pltpu.VMEM((tm, tn), F32)] if use_acc else [],
        compiler_params=_cp("parallel", "parallel", "arbitrary"),
    )(*args)


def _rms(x, w):
    return x * lax.rsqrt(jnp.mean(x * x, axis=-1, keepdims=True) + EPS) * w


def _row_tile(S, t=512):
    t = min(t, S)
    assert S % t == 0
    return t


def _rms_fwd(x, w, name):
    S, D = x.shape
    T = _row_tile(S)

    def body(x_ref, w_ref, o_ref):
        o_ref[...] = _rms(x_ref[...], w_ref[...]).astype(o_ref.dtype)

    return pl.pallas_call(
        body, name=name, grid=(S // T,),
        in_specs=[pl.BlockSpec((T, D), lambda i: (i, 0)), pl.BlockSpec((1, D), lambda i: (0, 0))],
        out_specs=pl.BlockSpec((T, D), lambda i: (i, 0)), out_shape=jax.ShapeDtypeStruct((S, D), BF16),
        compiler_params=_cp("parallel"),
    )(x, w.reshape(1, D))


def _rms_bwd(x, w, dh, dres, name):
    S, D = x.shape
    T = _row_tile(S)

    def body(x_ref, w_ref, dh_ref, dr_ref, dx_ref, gw_ref):
        _, vjp = jax.vjp(_rms, x_ref[...], w_ref[...])
        dx, dw = vjp(dh_ref[...])
        dx_ref[...] = dr_ref[...] + dx

        @pl.when(pl.program_id(0) == 0)
        def _():
            gw_ref[...] = jnp.zeros_like(gw_ref)

        gw_ref[...] += dw

    row = pl.BlockSpec((T, D), lambda i: (i, 0))
    vec = pl.BlockSpec((1, D), lambda i: (0, 0))
    return pl.pallas_call(
        body, name=name, grid=(S // T,), in_specs=[row, vec, row, row], out_specs=[row, vec],
        out_shape=[jax.ShapeDtypeStruct((S, D), F32), jax.ShapeDtypeStruct((1, D), F32)],
        compiler_params=_cp("arbitrary"),
    )(x, w.reshape(1, D), dh, dres)


def _loss_head(x, w, tgt, name):
    S, D = x.shape
    T = _row_tile(S)

    def body(x_ref, w_ref, t_ref, l_ref, dx_ref, gw_ref):
        y, vjp = jax.vjp(_rms, x_ref[...], w_ref[...])
        err = y - t_ref[...]
        part = 0.5 * jnp.sum(jnp.mean(err * err, axis=-1, keepdims=True), axis=0, keepdims=True)
        dx, dw = vjp(err * (1.0 / D))
        dx_ref[...] = dx

        @pl.when(pl.program_id(0) == 0)
        def _():
            gw_ref[...] = jnp.zeros_like(gw_ref)
            l_ref[...] = jnp.zeros_like(l_ref)

        gw_ref[...] += dw
        l_ref[...] += jnp.broadcast_to(part, l_ref.shape)

    row = pl.BlockSpec((T, D), lambda i: (i, 0))
    vec = pl.BlockSpec((1, D), lambda i: (0, 0))
    return pl.pallas_call(
        body, name=name, grid=(S // T,), in_specs=[row, vec, row],
        out_specs=[pl.BlockSpec((8, 128), lambda i: (0, 0)), row, vec],
        out_shape=[jax.ShapeDtypeStruct((8, 128), F32), jax.ShapeDtypeStruct((S, D), F32), jax.ShapeDtypeStruct((1, D), F32)],
        compiler_params=_cp("arbitrary"),
    )(x, w.reshape(1, D), tgt)


def _by_group(shape, vals):
    g = _lanes(shape) // POOL_GD
    out = vals[-1]
    for k in range(len(vals) - 2, -1, -1):
        out = jnp.where(g == k, vals[k], out)
    return out


def _pool_d(prev, u, t0):
    ext = jnp.concatenate([prev, u], axis=0)
    s2 = ext + _down(ext, 1)
    s4 = s2 + _down(s2, 2)
    s8 = s4 + _down(s4, 4)
    s16 = s8 + _down(s8, 8)
    ssel = _by_group(ext.shape, [s2, s4, s8, s16])[POOL_HALO:]
    win = _by_group(u.shape, [jnp.int32(w) for w in POOL_WINDOWS])
    cnt = jnp.minimum(t0 + _rows(u.shape) + 1, win).astype(F32)
    return ssel / cnt - u


def _pool_lin(d, w_ref, b):
    ys = [_mm(d[:, g * POOL_GD:(g + 1) * POOL_GD], w_ref[g]) for g in range(POOL_G)]
    return jnp.concatenate(ys, axis=1) + b


def _pool_fwd(proj, w, b, scale, name):
    S = proj.shape[0]
    T = _row_tile(S)
    r = T // POOL_HALO
    cb = PPOOL // POOL_W

    def body(u_ref, up_ref, w_ref, b_ref, sc_ref, y_ref):
        i = pl.program_id(0)
        prev = jnp.where(i > 0, up_ref[...], 0.0)
        d = _pool_d(prev, u_ref[...], i * T)
        y_ref[...] = _pool_lin(d, w_ref, b_ref[...]) * sc_ref[...]

    vec = pl.BlockSpec((1, POOL_W), lambda i: (0, 0))
    return pl.pallas_call(
        body, name=name, grid=(S // T,),
        in_specs=[pl.BlockSpec((T, POOL_W), lambda i: (i, cb)),
                  pl.BlockSpec((POOL_HALO, POOL_W), lambda i: (jnp.maximum(i * r - 1, 0), cb)),
                  pl.BlockSpec((POOL_G, POOL_GD, POOL_GD), lambda i: (0, 0, 0)), vec, vec],
        out_specs=pl.BlockSpec((T, POOL_W), lambda i: (i, 0)), out_shape=jax.ShapeDtypeStruct((S, POOL_W), F32),
        compiler_params=_cp("parallel"),
    )(proj, proj, w, b.reshape(1, POOL_W), scale.reshape(1, POOL_W))


def _pool_bwd(proj, dmixed, w, b, scale, name):
    S = proj.shape[0]
    T = _row_tile(S)
    n = S // T
    r = T // POOL_HALO
    cb = PPOOL // POOL_W
    mb = 1536 // POOL_W

    def body(u_ref, up_ref, dy_ref, dyn_ref, w_ref, b_ref, sc_ref, du_ref, gw_ref, gb_ref, gs_ref):
        i = pl.program_id(0)
        sc = sc_ref[...]
        dy = dy_ref[...]
        dy_ext = jnp.concatenate([dy, jnp.where(i < n - 1, dyn_ref[...], 0.0)], axis=0)
        dyl = dy_ext * sc
        dd = jnp.concatenate(
            [_mm(dyl[:, g * POOL_GD:(g + 1) * POOL_GD], w_ref[g], False, True) for g in range(POOL_G)], axis=1)
        t_ext = i * T + _rows(dd.shape)
        win = _by_group(dd.shape, [jnp.int32(v) for v in POOL_WINDOWS])
        cnt = jnp.minimum(t_ext + 1, win).astype(F32)
        e = jnp.where(t_ext < S, dd / cnt, 0.0)
        f2 = e + _up(e, 1)
        f4 = f2 + _up(f2, 2)
        f8 = f4 + _up(f4, 4)
        f16 = f8 + _up(f8, 8)
        du = (_by_group(dd.shape, [f2, f4, f8, f16]) - dd)[:T]
        du_ref[...] = du.astype(du_ref.dtype)

        prev = jnp.where(i > 0, up_ref[...], 0.0)
        d = _pool_d(prev, u_ref[...], i * T)
        ylin = _pool_lin(d, w_ref, b_ref[...])
        dyl_m = dy * sc

        @pl.when(i == 0)
        def _():
            gw_ref[...] = jnp.zeros_like(gw_ref)
            gb_ref[...] = jnp.zeros_like(gb_ref)
            gs_ref[...] = jnp.zeros_like(gs_ref)

        gs_ref[...] += jnp.sum(dy * ylin, axis=0, keepdims=True)
        gb_ref[...] += jnp.sum(dyl_m, axis=0, keepdims=True)
        for g in range(POOL_G):
            sl = slice(g * POOL_GD, (g + 1) * POOL_GD)
            gw_ref[g] += _mm(d[:, sl], dyl_m[:, sl], True, False)

    vec = pl.BlockSpec((1, POOL_W), lambda i: (0, 0))
    wsp = pl.BlockSpec((POOL_G, POOL_GD, POOL_GD), lambda i: (0, 0, 0))
    nh = S // POOL_HALO
    return pl.pallas_call(
        body, name=name, grid=(n,),
        in_specs=[pl.BlockSpec((T, POOL_W), lambda i: (i, cb)),
                  pl.BlockSpec((POOL_HALO, POOL_W), lambda i: (jnp.maximum(i * r - 1, 0), cb)),
                  pl.BlockSpec((T, POOL_W), lambda i: (i, mb)),
                  pl.BlockSpec((POOL_HALO, POOL_W), lambda i: (jnp.minimum((i + 1) * r, nh - 1), mb)),
                  wsp, vec, vec],
        out_specs=[pl.BlockSpec((T, POOL_W), lambda i: (i, 0)), wsp, vec, vec],
        out_shape=[jax.ShapeDtypeStruct((S, POOL_W), BF16), jax.ShapeDtypeStruct((POOL_G, POOL_GD, POOL_GD), F32),
                   jax.ShapeDtypeStruct((1, POOL_W), F32), jax.ShapeDtypeStruct((1, POOL_W), F32)],
        compiler_params=_cp("arbitrary"),
    )(proj, proj, dmixed, dmixed, w, b.reshape(1, POOL_W), scale.reshape(1, POOL_W))


def _conv_rows(ext, w_ref, taps):
    acc = w_ref[taps - 1:taps, :] * ext
    for k in range(1, taps):
        acc = acc + w_ref[taps - 1 - k:taps - k, :] * _down(ext, k)
    return acc


def _conv_t_rows(dc, w_ref, taps):
    acc = w_ref[taps - 1:taps, :] * dc
    for k in range(1, taps):
        acc = acc + w_ref[taps - 1 - k:taps - k, :] * _up(dc, k)
    return acc


def _conv_specs(T, S, ncb0, with_next):
    r = T // HALO
    nh = S // HALO
    main = pl.BlockSpec((T, CB), lambda j, i: (i, j + ncb0))
    prev = pl.BlockSpec((HALO, CB), lambda j, i: (jnp.maximum(i * r - 1, 0), j + ncb0))
    nxt = pl.BlockSpec((HALO, CB), lambda j, i: (jnp.minimum((i + 1) * r, nh - 1), j + ncb0))
    return (main, prev, nxt) if with_next else (main, prev)


def _gdn_conv_fwd(proj, w, name):
    S = proj.shape[0]
    T = _row_tile(S)
    taps = w.shape[0]
    ncb = 3 * GDN_W // CB

    def body(x_ref, xp_ref, w_ref, o_ref):
        i = pl.program_id(1)
        ext = jnp.concatenate([jnp.where(i > 0, xp_ref[...], 0.0), x_ref[...]], axis=0)
        o_ref[...] = jax.nn.silu(_conv_rows(ext, w_ref, taps)[HALO:])

    main, prev = _conv_specs(T, S, PQ // CB, False)
    return pl.pallas_call(
        body, name=name, grid=(ncb, S // T),
        in_specs=[main, prev, pl.BlockSpec((taps, CB), lambda j, i: (0, j))],
        out_specs=pl.BlockSpec((T, CB), lambda j, i: (i, j)), out_shape=jax.ShapeDtypeStruct((S, 3 * GDN_W), F32),
        compiler_params=_cp("parallel", "parallel"),
    )(proj, proj, w)


def _gdn_conv_bwd(proj, dact, w, name):
    S = proj.shape[0]
    T = _row_tile(S)
    n = S // T
    taps = w.shape[0]
    ncb = 3 * GDN_W // CB

    def body(x_ref, xp_ref, xn_ref, d_ref, dn_ref, w_ref, dx_ref, gw_ref):
        i = pl.program_id(1)
        last = i == n - 1
        ext = jnp.concatenate([jnp.where(i > 0, xp_ref[...], 0.0), x_ref[...], jnp.where(last, 0.0, xn_ref[...])], axis=0)
        c = _conv_rows(ext, w_ref, taps)[HALO:]
        d_ext = jnp.concatenate([d_ref[...], jnp.where(last, 0.0, dn_ref[...])], axis=0)
        _, vjp = jax.vjp(jax.nn.silu, c)
        dc = vjp(d_ext)[0]
        dx_ref[...] = _conv_t_rows(dc, w_ref, taps)[:T].astype(dx_ref.dtype)

        @pl.when(i == 0)
        def _():
            gw_ref[...] = jnp.zeros_like(gw_ref)

        dcm = dc[:T]
        for k in range(taps):
            gw_ref[taps - 1 - k:taps - k, :] += jnp.sum(dcm * _down(ext, k)[HALO:HALO + T], axis=0, keepdims=True)

    main, prev, nxt = _conv_specs(T, S, PQ // CB, True)
    dmain, _, dnxt = _conv_specs(T, S, 0, True)
    wsp = pl.BlockSpec((taps, CB), lambda j, i: (0, j))
    return pl.pallas_call(
        body, name=name, grid=(ncb, n), in_specs=[main, prev, nxt, dmain, dnxt, wsp],
        out_specs=[pl.BlockSpec((T, CB), lambda j, i: (i, j)), wsp],
        out_shape=[jax.ShapeDtypeStruct((S, 3 * GDN_W), BF16), jax.ShapeDtypeStruct((taps, 3 * GDN_W), F32)],
        compiler_params=_cp("parallel", "arbitrary"),
    )(proj, proj, proj, dact, dact, w)


def _ffn_act_fwd(up, w, name):
    S = up.shape[0]
    T = _row_tile(S)
    taps = w.shape[0]
    ncb = D_FF // CB

    def body(g_ref, gp_ref, v_ref, w_ref, o_ref):
        i = pl.program_id(1)
        ext = jnp.concatenate([jnp.where(i > 0, gp_ref[...], 0.0), g_ref[...]], axis=0)
        c = _conv_rows(ext, w_ref, taps)[HALO:]
        o_ref[...] = (jax.nn.gelu(c) * v_ref[...]).astype(o_ref.dtype)

    main, prev = _conv_specs(T, S, 0, False)
    val = pl.BlockSpec((T, CB), lambda j, i: (i, j + ncb))
    return pl.pallas_call(
        body, name=name, grid=(ncb, S // T),
        in_specs=[main, prev, val, pl.BlockSpec((taps, CB), lambda j, i: (0, j))],
        out_specs=pl.BlockSpec((T, CB), lambda j, i: (i, j)), out_shape=jax.ShapeDtypeStruct((S, D_FF), BF16),
        compiler_params=_cp("parallel", "parallel"),
    )(up, up, up, w)


def _ffn_act_bwd(up, dact, w, name):
    S = up.shape[0]
    T = _row_tile(S)
    n = S // T
    taps = w.shape[0]
    ncb = D_FF // CB

    def body(g_ref, gp_ref, gn_ref, v_ref, vn_ref, d_ref, dn_ref, w_ref, dup_ref, gw_ref):
        i = pl.program_id(1)
        last = i == n - 1
        ext = jnp.concatenate([jnp.where(i > 0, gp_ref[...], 0.0), g_ref[...], jnp.where(last, 0.0, gn_ref[...])], axis=0)
        c = _conv_rows(ext, w_ref, taps)[HALO:]
        v_ext = jnp.concatenate([v_ref[...], jnp.where(last, 0.0, vn_ref[...])], axis=0)
        d_ext = jnp.concatenate([d_ref[...], jnp.where(last, 0.0, dn_ref[...])], axis=0)
        gl, vjp = jax.vjp(jax.nn.gelu, c)
        dup_ref[1] = (d_ext * gl)[:T].astype(dup_ref.dtype)
        dc = vjp(d_ext * v_ext)[0]
        dup_ref[0] = _conv_t_rows(dc, w_ref, taps)[:T].astype(dup_ref.dtype)

        @pl.when(i == 0)
        def _():
            gw_ref[...] = jnp.zeros_like(gw_ref)

        dcm = dc[:T]
        for k in range(taps):
            gw_ref[taps - 1 - k:taps - k, :] += jnp.sum(dcm * _down(ext, k)[HALO:HALO + T], axis=0, keepdims=True)

    main, prev, nxt = _conv_specs(T, S, 0, True)
    vmain, _, vnxt = _conv_specs(T, S, ncb, True)
    wsp = pl.BlockSpec((taps, CB), lambda j, i: (0, j))
    osp = pl.BlockSpec((2, T, CB), lambda j, i: (0, i, j))
    return pl.pallas_call(
        body, name=name, grid=(ncb, n), in_specs=[main, prev, nxt, vmain, vnxt, main, nxt, wsp],
        out_specs=[osp, wsp],
        out_shape=[jax.ShapeDtypeStruct((2, S, D_FF), BF16), jax.ShapeDtypeStruct((taps, D_FF), F32)],
        compiler_params=_cp("parallel", "arbitrary"),
    )(up, up, up, up, up, dact, dact, w)


def _tri_masks():
    r = _rows((GDN_C, GDN_C))
    c = _lanes((GDN_C, GDN_C))
    return r >= c, r > c


def _each(fn, *cols):
    return tuple(fn(*args) for args in zip(*cols))


def _tri_inv_raw(lows):
    r = _rows(lows[0].shape)
    c = _lanes(lows[0].shape)
    eye = jnp.where(r == c, 1.0, 0.0)
    ps = _each(lambda low: eye - low, lows)
    lps = lows
    for _ in range(5):
        lps = _each(lambda lp: _mm(lp, lp, False, False, True), lps)
        ps = _each(lambda p, lp: p + _mm(p, lp, False, False, True), ps, lps)
    return ps


@jax.custom_vjp
def _tri_inv(lows):
    return _tri_inv_raw(lows)


def _tri_inv_fwd(lows):
    ts = _tri_inv_raw(lows)
    return ts, ts


def _tri_inv_bwd(ts, dts):
    inner = _each(lambda t, dt: _mm(t, dt, True, False, True), ts, dts)
    return (_each(lambda m, t: -_mm(m, t, False, True, True), inner, ts),)


_tri_inv.defvjp(_tri_inv_fwd, _tri_inv_bwd)


@jax.custom_vjp
def _tri_inv_given(lows, ts):
    return ts


def _tri_inv_given_fwd(lows, ts):
    return ts, ts


def _tri_inv_given_bwd(ts, dts):
    return _tri_inv_bwd(ts, dts)[0], _each(jnp.zeros_like, ts)


_tri_inv_given.defvjp(_tri_inv_given_fwd, _tri_inv_given_bwd)


def _gdn_glog(a_col, alog, dtb):
    return -jnp.exp(alog) * jax.nn.softplus(a_col + dtb)


def _decay_operand():
    r = _rows((GDN_C, 2 * GDN_C))
    c = _lanes((GDN_C, 2 * GDN_C))
    return jnp.where((c >= GDN_C) | (r > c), 1.0, 0.0)


def _gdn_decay(glog):
    causal, _ = _tri_masks()
    res = _mm01(jnp.where(causal, 1.0, 0.0), glog * _decay_operand())
    return res[:, GDN_C:GDN_C + 1], res[:, :GDN_C]


def _gdn_decay_bwd(dgcol, dd):
    r = _rows((GDN_C, GDN_C))
    c = _lanes((GDN_C, GDN_C))
    dres = jnp.concatenate([dd, jnp.where(c == 0, dgcol, 0.0)], axis=1)
    dx = _mm01(jnp.where(r <= c, 1.0, 0.0), dres)
    return jnp.sum(dx * _decay_operand(), axis=1, keepdims=True)


def _gdn_chunk(qa, ka, va, bt_col, gcol, dmat, t_saved=None):
    causal, strict = _tri_masks()
    qn = _each(lambda q: q * lax.rsqrt(jnp.sum(q * q, axis=-1, keepdims=True) + EPS) * (GDN_DH ** -0.5), qa)
    kn = _each(lambda k: k * lax.rsqrt(jnp.sum(k * k, axis=-1, keepdims=True) + EPS), ka)
    beta = _each(jax.nn.sigmoid, bt_col)
    eg = _each(jnp.exp, gcol)
    decay = _each(lambda d: jnp.where(causal, jnp.exp(d), 0.0), dmat)
    kk = _each(lambda k: _mm(k, k, False, True), kn)
    low = _each(lambda b, m, d: jnp.where(strict, b * m * d, 0.0), beta, kk, decay)
    t = _tri_inv(low) if t_saved is None else _tri_inv_given(low, t_saved)
    w = _each(lambda t_, k, b, e: _mm(t_, k * (b * e), False, False, True), t, kn, beta, eg)
    u = _each(lambda t_, v, b: _mm(t_, v * b, False, False, True), t, va, beta)
    attn = _each(lambda q, k, d: _mm(q, k, False, True) * d, qn, kn, decay)
    last = _rows(gcol[0].shape) == GDN_C - 1
    g_last = _each(lambda g: jnp.sum(jnp.where(last, g, 0.0), axis=0, keepdims=True), gcol)
    qd = _each(lambda q, e: q * e, qn, eg)
    kd = _each(lambda k, gl, g: k * jnp.exp(gl - g), kn, g_last, gcol)
    return (w, u, qd, kd, attn), t


def _gdn_step(state, w, u, qd, kd, attn, egl):
    v_new = _each(lambda u_, w_, s: u_ - _mm(w_, s), u, w, state)
    o_state = _each(_mm, qd, state)
    o = _each(lambda os, a, v: os + _mm(a, v), o_state, attn, v_new)
    new = _each(lambda s, e, k, v: s * e + _mm(k, v, True, False), state, egl, kd, v_new)
    return o, new


def _heads(ref, base=0, width=GDN_DH):
    return tuple(ref[:, (base + h) * GDN_DH:(base + h) * GDN_DH + width] for h in range(GDN_H))


def _cols(a, base):
    return tuple(a[:, base + h:base + h + 1] for h in range(GDN_H))


def _gated_norm(o, z, nw):
    return o * lax.rsqrt(jnp.mean(o * o, axis=-1, keepdims=True) + EPS) * nw * jax.nn.silu(z)


def _hsl(h):
    return slice(h * GDN_DH, (h + 1) * GDN_DH)


def _pad_lanes(a, width=GDN_DH):
    return jnp.concatenate([a, jnp.zeros((a.shape[0], width - a.shape[1]), a.dtype)], axis=1)


def _gdn_prep(qkv, proj, alog, dtb, name):
    S = qkv.shape[0]
    N = S // GDN_C

    def body(qkv_ref, ab_ref, al_ref, dt_ref, w_ref, u_ref, qd_ref, kd_ref, at_ref, ti_ref, gc_ref):
        ab = ab_ref[...]
        glog = _each(_gdn_glog, _cols(ab, 0), _cols(al_ref[...], 0), _cols(dt_ref[...], 0))
        dec = _each(_gdn_decay, glog)
        gcol, dmat = _each(lambda d: d[0], dec), _each(lambda d: d[1], dec)
        (w, u, qd, kd, attn), tinv = _gdn_chunk(_heads(qkv_ref), _heads(qkv_ref, GDN_H), _heads(qkv_ref, 2 * GDN_H),
                                                _cols(ab, GDN_H), gcol, dmat)
        gc = jnp.zeros((GDN_C, 128), F32)
        for h in range(GDN_H):
            w_ref[:, _hsl(h)] = w[h]
            u_ref[:, _hsl(h)] = u[h]
            qd_ref[:, _hsl(h)] = qd[h]
            kd_ref[:, _hsl(h)] = kd[h]
            at_ref[:, _hsl(h)] = _pad_lanes(attn[h])
            ti_ref[:, _hsl(h)] = _pad_lanes(tinv[h])
            gc = jnp.where(_lanes(gc.shape) == h, gcol[h], gc)
        gc_ref[...] = gc

    vec = pl.BlockSpec((1, 128), lambda i: (0, 0))
    hsp = pl.BlockSpec((GDN_C, GDN_W), lambda i: (i, 0))
    hshape = jax.ShapeDtypeStruct((S, GDN_W), F32)
    return pl.pallas_call(
        body, name=name, grid=(N,),
        in_specs=[pl.BlockSpec((GDN_C, 3 * GDN_W), lambda i: (i, 0)), pl.BlockSpec((GDN_C, 128), lambda i: (i, PAB // 128)), vec, vec],
        out_specs=[hsp] * 6 + [pl.BlockSpec((GDN_C, 128), lambda i: (i, 0))],
        out_shape=[hshape] * 6 + [jax.ShapeDtypeStruct((S, 128), F32)],
        compiler_params=_cp("parallel"),
    )(qkv, proj, alog, dtb)


def _gdn_scan(w, u, qd, kd, attn, gc, name):
    S = w.shape[0]
    N = S // GDN_C

    def body(w_ref, u_ref, qd_ref, kd_ref, at_ref, gc_ref, o_ref, st_ref, s_ref):
        @pl.when(pl.program_id(0) == 0)
        def _():
            s_ref[...] = jnp.zeros_like(s_ref)

        state = tuple(s_ref[_hsl(h), :] for h in range(GDN_H))
        egl = _each(jnp.exp, _cols(gc_ref[GDN_C - 1:GDN_C, :], 0))
        o, new = _gdn_step(state, _heads(w_ref), _heads(u_ref), _heads(qd_ref), _heads(kd_ref),
                           _heads(at_ref, width=GDN_C), egl)
        for h in range(GDN_H):
            st_ref[_hsl(h), :] = state[h]
            o_ref[:, _hsl(h)] = o[h]
            s_ref[_hsl(h), :] = new[h]

    hsp = pl.BlockSpec((GDN_C, GDN_W), lambda i: (i, 0))
    return pl.pallas_call(
        body, name=name, grid=(N,),
        in_specs=[hsp] * 5 + [pl.BlockSpec((GDN_C, 128), lambda i: (i, 0))],
        out_specs=[hsp, pl.BlockSpec((None, GDN_W, GDN_DH), lambda i: (i, 0, 0))],
        out_shape=[jax.ShapeDtypeStruct((S, GDN_W), F32), jax.ShapeDtypeStruct((N, GDN_W, GDN_DH), F32)],
        scratch_shapes=[pltpu.VMEM((GDN_W, GDN_DH), F32)],
        compiler_params=_cp("arbitrary"),
    )(w, u, qd, kd, attn, gc)


def _gdn_scan_bwd(w, u, qd, kd, attn, gc, states, o, proj, dmixed, nw, name):
    S = w.shape[0]
    N = S // GDN_C

    def body(w_ref, u_ref, qd_ref, kd_ref, at_ref, gc_ref, st_ref, o_ref, z_ref, dm_ref, nw_ref,
             dw_ref, du_ref, dqd_ref, dkd_ref, dat_ref, dgl_ref, dz_ref, gnw_ref, ds_ref):
        @pl.when(pl.program_id(0) == 0)
        def _():
            ds_ref[...] = jnp.zeros_like(ds_ref)
            gnw_ref[...] = jnp.zeros_like(gnw_ref)

        nw = nw_ref[...]
        _, vjp_n = jax.vjp(lambda o, z, w_: _each(lambda a, b: _gated_norm(a, b, w_), o, z), _heads(o_ref), _heads(z_ref), nw)
        do, dz, dnw = vjp_n(_heads(dm_ref))
        state = tuple(st_ref[_hsl(h), :] for h in range(GDN_H))
        egl = _each(jnp.exp, _cols(gc_ref[GDN_C - 1:GDN_C, :], 0))
        _, vjp_s = jax.vjp(_gdn_step, state, _heads(w_ref), _heads(u_ref), _heads(qd_ref), _heads(kd_ref),
                           _heads(at_ref, width=GDN_C), egl)
        ds, dw, du, dqd, dkd, dat, degl = vjp_s((do, tuple(ds_ref[_hsl(h), :] for h in range(GDN_H))))
        dgl = jnp.zeros((8, 128), F32)
        for h in range(GDN_H):
            dz_ref[:, _hsl(h)] = dz[h].astype(dz_ref.dtype)
            ds_ref[_hsl(h), :] = ds[h]
            dw_ref[:, _hsl(h)] = dw[h]
            du_ref[:, _hsl(h)] = du[h]
            dqd_ref[:, _hsl(h)] = dqd[h]
            dkd_ref[:, _hsl(h)] = dkd[h]
            dat_ref[:, _hsl(h)] = _pad_lanes(dat[h])
            dgl = jnp.where(_lanes(dgl.shape) == h, degl[h] * egl[h], dgl)
        dgl_ref[...] = dgl
        gnw_ref[...] += dnw

    rev = lambda i: (N - 1 - i, 0)
    hsp = pl.BlockSpec((GDN_C, GDN_W), rev)
    gsp = pl.BlockSpec((GDN_C, 128), rev)
    vec = pl.BlockSpec((1, GDN_DH), lambda i: (0, 0))
    hshape = jax.ShapeDtypeStruct((S, GDN_W), F32)
    return pl.pallas_call(
        body, name=name, grid=(N,),
        in_specs=[hsp] * 5 + [gsp, pl.BlockSpec((None, GDN_W, GDN_DH), lambda i: (N - 1 - i, 0, 0)), hsp,
                              pl.BlockSpec((GDN_C, GDN_W), lambda i: (N - 1 - i, PZ // GDN_W)), hsp, vec],
        out_specs=[hsp] * 5 + [pl.BlockSpec((8, 128), rev), hsp, vec],
        out_shape=[hshape] * 5 + [jax.ShapeDtypeStruct((N * 8, 128), F32), jax.ShapeDtypeStruct((S, GDN_W), BF16),
                                  jax.ShapeDtypeStruct((1, GDN_DH), F32)],
        scratch_shapes=[pltpu.VMEM((GDN_W, GDN_DH), F32)],
        compiler_params=_cp("arbitrary"),
    )(w, u, qd, kd, attn, gc, states, o, proj, dmixed, nw)


def _gdn_prep_bwd(qkv, proj, alog, dtb, tinv, dw, du, dqd, dkd, dat, dgl, name):
    S = qkv.shape[0]
    N = S // GDN_C

    def body(qkv_ref, ab_ref, al_ref, dt_ref, ti_ref, dw_ref, du_ref, dqd_ref, dkd_ref, dat_ref, dgl_ref,
             dqkv_ref, dab_ref, gal_ref, gdt_ref):
        @pl.when(pl.program_id(0) == 0)
        def _():
            gal_ref[...] = jnp.zeros_like(gal_ref)
            gdt_ref[...] = jnp.zeros_like(gdt_ref)

        ab = ab_ref[...]
        glog, vjp_g = jax.vjp(lambda a, al, dt: _each(_gdn_glog, a, al, dt), _cols(ab, 0), _cols(al_ref[...], 0),
                              _cols(dt_ref[...], 0))
        dec = _each(_gdn_decay, glog)
        gcol, dmat = _each(lambda d: d[0], dec), _each(lambda d: d[1], dec)
        _, vjp_c, _ = jax.vjp(functools.partial(_gdn_chunk, t_saved=_heads(ti_ref, width=GDN_C)), _heads(qkv_ref),
                              _heads(qkv_ref, GDN_H), _heads(qkv_ref, 2 * GDN_H), _cols(ab, GDN_H), gcol, dmat, has_aux=True)
        dqa, dka, dva, dbt, dgcol, dd = vjp_c((_heads(dw_ref), _heads(du_ref), _heads(dqd_ref), _heads(dkd_ref),
                                               _heads(dat_ref, width=GDN_C)))
        last = _rows(dgcol[0].shape) == GDN_C - 1
        dgcol = _each(lambda d, g: d + jnp.where(last, g, 0.0), dgcol, _cols(dgl_ref[0:1, :], 0))
        da_col, dal, ddt = vjp_g(_each(_gdn_decay_bwd, dgcol, dd))
        dab = jnp.zeros((GDN_C, 128), F32)
        gal = jnp.zeros((1, 128), F32)
        gdt = jnp.zeros((1, 128), F32)
        for h in range(GDN_H):
            dqkv_ref[:, _hsl(h)] = dqa[h]
            dqkv_ref[:, _hsl(GDN_H + h)] = dka[h]
            dqkv_ref[:, _hsl(2 * GDN_H + h)] = dva[h]
            ln = _lanes(dab.shape)
            dab = dab + jnp.where(ln == h, da_col[h], 0.0) + jnp.where(ln == GDN_H + h, dbt[h], 0.0)
            l1 = _lanes(gal.shape)
            gal = gal + jnp.where(l1 == h, dal[h], 0.0)
            gdt = gdt + jnp.where(l1 == h, ddt[h], 0.0)
        dab_ref[...] = dab.astype(dab_ref.dtype)
        gal_ref[...] += gal
        gdt_ref[...] += gdt

    vec = pl.BlockSpec((1, 128), lambda i: (0, 0))
    hsp = pl.BlockSpec((GDN_C, GDN_W), lambda i: (i, 0))
    qsp = pl.BlockSpec((GDN_C, 3 * GDN_W), lambda i: (i, 0))
    return pl.pallas_call(
        body, name=name, grid=(N,),
        in_specs=[qsp, pl.BlockSpec((GDN_C, 128), lambda i: (i, PAB // 128)), vec, vec] + [hsp] * 6
        + [pl.BlockSpec((8, 128), lambda i: (i, 0))],
        out_specs=[qsp, pl.BlockSpec((GDN_C, 128), lambda i: (i, 0)), vec, vec],
        out_shape=[jax.ShapeDtypeStruct((S, 3 * GDN_W), F32), jax.ShapeDtypeStruct((S, 128), BF16),
                   jax.ShapeDtypeStruct((1, 128), F32), jax.ShapeDtypeStruct((1, 128), F32)],
        compiler_params=_cp("arbitrary"),
    )(qkv, proj, alog, dtb, tinv, dw, du, dqd, dkd, dat, dgl)


@jax.custom_vjp
def _expm1(x):
    u = jnp.exp(x)
    lu = jnp.log(u)
    small = (u - 1.0) * x / jnp.where(u == 1.0, 1.0, lu)
    small = jnp.where(u == 1.0, x, small)
    return jnp.where(jnp.abs(x) < 0.5, small, u - 1.0)


def _expm1_fwd(x):
    return _expm1(x), jnp.exp(x)


def _expm1_bwd(ex, g):
    return (g * ex,)


_expm1.defvjp(_expm1_fwd, _expm1_bwd)


def _lru_gates(xc, wa, ba, wx, bx, lam, first):
    r = jax.nn.sigmoid(_mm(xc, wa) + ba)
    i = jax.nn.sigmoid(_mm(xc, wx) + bx)
    log_a = -LRU_C * r * jax.nn.softplus(-lam)
    mult = jnp.sqrt(-_expm1(2.0 * log_a))
    mult = jnp.where(first, 1.0, mult)
    return jnp.exp(log_a), mult * i * xc


def _scan_fwd(a, b):
    T = a.shape[0]
    rows = _rows(a.shape)
    s = 1
    while s < T:
        ok = rows >= s
        b = a * jnp.where(ok, _down(b, s), 0.0) + b
        a = a * jnp.where(ok, _down(a, s), 1.0)
        s *= 2
    return a, b


def _scan_rev(a, b):
    T = a.shape[0]
    rows = _rows(a.shape)
    s = 1
    while s < T:
        ok = rows + s < T
        b = a * jnp.where(ok, _up(b, s), 0.0) + b
        a = a * jnp.where(ok, _up(a, s), 1.0)
        s *= 2
    return b


def _bsl(j):
    return slice(j * LRU_BD, (j + 1) * LRU_BD)


def _lru_tile(S):
    return _row_tile(S, 256)


def _lru_fwd(proj, conv_w, conv_b, wa, ba, wx, bx, lam, name):
    S = proj.shape[0]
    T = _lru_tile(S)
    taps = conv_w.shape[0]
    r = T // HALO

    def body(x_ref, xp_ref, cw_ref, cb_ref, wa_ref, ba_ref, wx_ref, bx_ref, lam_ref, h_ref, carry_ref):
        i = pl.program_id(0)

        @pl.when(i == 0)
        def _():
            carry_ref[...] = jnp.zeros_like(carry_ref)

        ext = jnp.concatenate([jnp.where(i > 0, xp_ref[...], 0.0), x_ref[...]], axis=0)
        xc = _conv_rows(ext, cw_ref, taps)[HALO:] + cb_ref[...]
        first = (i * T + _rows((T, LRU_BD))) == 0
        for j in range(LRU_NB):
            a, b = _lru_gates(xc[:, _bsl(j)], wa_ref[j], ba_ref[:, _bsl(j)], wx_ref[j], bx_ref[:, _bsl(j)],
                              lam_ref[:, _bsl(j)], first=first)
            pa, hb = _scan_fwd(a, b)
            h_ref[:, _bsl(j)] = pa * carry_ref[0:1, _bsl(j)] + hb
            carry_ref[0:1, _bsl(j)] = h_ref[T - 1:T, _bsl(j)]

    vec = pl.BlockSpec((1, LRU_W), lambda i: (0, 0))
    wsp = pl.BlockSpec((LRU_NB, LRU_BD, LRU_BD), lambda i: (0, 0, 0))
    return pl.pallas_call(
        body, name=name, grid=(S // T,),
        in_specs=[pl.BlockSpec((T, LRU_W), lambda i: (i, PXR // LRU_W)),
                  pl.BlockSpec((HALO, LRU_W), lambda i: (jnp.maximum(i * r - 1, 0), PXR // LRU_W)),
                  pl.BlockSpec((taps, LRU_W), lambda i: (0, 0)), vec, wsp, vec, wsp, vec, vec],
        out_specs=pl.BlockSpec((T, LRU_W), lambda i: (i, 0)), out_shape=jax.ShapeDtypeStruct((S, LRU_W), F32),
        scratch_shapes=[pltpu.VMEM((8, LRU_W), F32)],
        compiler_params=_cp("arbitrary"),
    )(proj, proj, conv_w, conv_b.reshape(1, LRU_W), wa, ba.reshape(1, LRU_W), wx, bx.reshape(1, LRU_W), lam.reshape(1, LRU_W))


def _lru_bwd(proj, hl, dmixed, conv_w, conv_b, wa, ba, wx, bx, lam, name):
    S = proj.shape[0]
    T = _lru_tile(S)
    n = S // T
    taps = conv_w.shape[0]
    r = T // HALO
    mb = 768 // LRU_W

    def body(x_ref, xp_ref, g_ref, h_ref, hp_ref, dy_ref, cw_ref, cb_ref, wa_ref, ba_ref, wx_ref, bx_ref, lam_ref,
             dx_ref, dg_ref, gcw_ref, gcb_ref, gwa_ref, gba_ref, gwx_ref, gbx_ref, glam_ref, carry_ref, dxc_ref, nxt_ref):
        s = pl.program_id(0)
        i = n - 1 - s

        @pl.when(s == 0)
        def _():
            carry_ref[...] = jnp.zeros_like(carry_ref)
            nxt_ref[...] = jnp.zeros_like(nxt_ref)
            for ref in (gcw_ref, gcb_ref, gwa_ref, gba_ref, gwx_ref, gbx_ref, glam_ref):
                ref[...] = jnp.zeros_like(ref)

        ext = jnp.concatenate([jnp.where(i > 0, xp_ref[...], 0.0), x_ref[...]], axis=0)
        xc = _conv_rows(ext, cw_ref, taps)[HALO:] + cb_ref[...]
        rows = _rows((T, LRU_BD))
        first = (i * T + rows) == 0
        h_before = jnp.where(i > 0, hp_ref[HALO - 1:HALO, :], 0.0)
        for j in range(LRU_NB):
            sl = _bsl(j)
            (a, _), vjp_g = jax.vjp(functools.partial(_lru_gates, first=first), xc[:, sl], wa_ref[j], ba_ref[:, sl],
                                    wx_ref[j], bx_ref[:, sl], lam_ref[:, sl])
            gelu_g, vjp_a = jax.vjp(jax.nn.gelu, g_ref[:, sl])
            h = h_ref[:, sl]
            dy = dy_ref[:, sl]
            dg_ref[:, sl] = vjp_a(dy * h)[0].astype(dg_ref.dtype)
            b_rev = dy * gelu_g + jnp.where(rows == T - 1, carry_ref[0:1, sl], 0.0)
            a_rev = jnp.where(rows == T - 1, 0.0, _up(a, 1))
            dh = _scan_rev(a_rev, b_rev)
            carry_ref[:, sl] = (a * dh)[:HALO]
            h_prev = jnp.where(rows == 0, h_before[:, sl], _down(h, 1))
            dxc, dwa, dba, dwx, dbx, dlam = vjp_g((dh * h_prev, dh))
            dxc_ref[:, sl] = dxc
            gwa_ref[j] += dwa
            gwx_ref[j] += dwx
            gba_ref[:, sl] += dba
            gbx_ref[:, sl] += dbx
            glam_ref[:, sl] += dlam
        dxc = dxc_ref[...]
        d_ext = jnp.concatenate([dxc, nxt_ref[...]], axis=0)
        dx_ref[...] = _conv_t_rows(d_ext, cw_ref, taps)[:T].astype(dx_ref.dtype)
        nxt_ref[...] = dxc[:HALO]
        gcb_ref[...] += jnp.sum(dxc, axis=0, keepdims=True)
        for k in range(taps):
            gcw_ref[taps - 1 - k:taps - k, :] += jnp.sum(dxc * _down(ext, k)[HALO:], axis=0, keepdims=True)

    vec = pl.BlockSpec((1, LRU_W), lambda s: (0, 0))
    wsp = pl.BlockSpec((LRU_NB, LRU_BD, LRU_BD), lambda s: (0, 0, 0))
    cwsp = pl.BlockSpec((taps, LRU_W), lambda s: (0, 0))

    def main(cb):
        return pl.BlockSpec((T, LRU_W), lambda s: (n - 1 - s, cb))

    def prev(cb):
        return pl.BlockSpec((HALO, LRU_W), lambda s: (jnp.maximum((n - 1 - s) * r - 1, 0), cb))

    vshape = jax.ShapeDtypeStruct((1, LRU_W), F32)
    wshape = jax.ShapeDtypeStruct((LRU_NB, LRU_BD, LRU_BD), F32)
    return pl.pallas_call(
        body, name=name, grid=(n,),
        in_specs=[main(PXR // LRU_W), prev(PXR // LRU_W), main(PGR // LRU_W), main(0), prev(0), main(mb),
                  cwsp, vec, wsp, vec, wsp, vec, vec],
        out_specs=[main(0), main(0), cwsp, vec, wsp, vec, wsp, vec, vec],
        out_shape=[jax.ShapeDtypeStruct((S, LRU_W), BF16), jax.ShapeDtypeStruct((S, LRU_W), BF16),
                   jax.ShapeDtypeStruct((taps, LRU_W), F32), vshape, wshape, vshape, wshape, vshape, vshape],
        scratch_shapes=[pltpu.VMEM((8, LRU_W), F32), pltpu.VMEM((T, LRU_W), F32), pltpu.VMEM((HALO, LRU_W), F32)],
        compiler_params=_cp("arbitrary"),
    )(proj, proj, proj, hl, hl, dmixed, conv_w, conv_b.reshape(1, LRU_W), wa, ba.reshape(1, LRU_W), wx,
      bx.reshape(1, LRU_W), lam.reshape(1, LRU_W))


def _mix_out(o, proj, hl, y_pool, nw, name):
    S = o.shape[0]
    T = _row_tile(S)

    def body(o_ref, z_ref, h_ref, g_ref, p_ref, nw_ref, m_ref):
        for h in range(GDN_H):
            m_ref[:, _hsl(h)] = _gated_norm(o_ref[:, _hsl(h)], z_ref[:, _hsl(h)], nw_ref[...]).astype(m_ref.dtype)
        m_ref[:, GDN_W:GDN_W + LRU_W] = (h_ref[...] * jax.nn.gelu(g_ref[...])).astype(m_ref.dtype)
        m_ref[:, GDN_W + LRU_W:] = p_ref[...].astype(m_ref.dtype)

    row = pl.BlockSpec((T, GDN_W), lambda i: (i, 0))
    return pl.pallas_call(
        body, name=name, grid=(S // T,),
        in_specs=[row, pl.BlockSpec((T, GDN_W), lambda i: (i, PZ // GDN_W)), row,
                  pl.BlockSpec((T, LRU_W), lambda i: (i, PGR // LRU_W)), pl.BlockSpec((T, POOL_W), lambda i: (i, 0)),
                  pl.BlockSpec((1, GDN_DH), lambda i: (0, 0))],
        out_specs=pl.BlockSpec((T, D_MODEL), lambda i: (i, 0)), out_shape=jax.ShapeDtypeStruct((S, D_MODEL), BF16),
        compiler_params=_cp("parallel"),
    )(o, proj, hl, proj, y_pool, nw)


def _as2d(a):
    return a.reshape(-1, a.shape[-1])


def _ew_rows(rows, cols):
    t = rows
    while t * cols * 4 > (2 << 20) and t % 16 == 0:
        t //= 2
    return t


def _rs_rows(rows, cols):
    t = rows
    while t * cols * 4 > (2 << 20) and t % 32 == 0:
        t //= 2
    return t


def _rs_add(src, recv, src_index, grid_lead, out_dtype, name, into=None):
    rows, cols = recv.shape[-2:]
    t = _rs_rows(rows, cols)
    nl = len(grid_lead)
    lead_none = (None,) * (src.ndim - 2)

    def body(s_ref, r_ref, *rest):
        o_ref = rest[-1]
        o_ref[...] = (s_ref[...].astype(F32) + r_ref[...].astype(F32)).astype(o_ref.dtype)

    def src_map(*a):
        return (*src_index(*a[:nl], _place()), a[nl], 0)

    def own_map(*a):
        return (*a[:nl], a[nl], 0)

    own = pl.BlockSpec(((None,) * nl) + (t, cols), own_map)
    in_specs = [pl.BlockSpec((*lead_none, t, cols), src_map), own]
    args = [src, recv]
    if into is None:
        out_spec, out_shape, aliases = own, jax.ShapeDtypeStruct(recv.shape, out_dtype), {}
    else:
        buf, layer = into
        shape = buf.shape
        args.append(buf.reshape((2, 2) + recv.shape))
        in_specs.append(pl.BlockSpec(memory_space=pl.ANY))
        out_spec = pl.BlockSpec(((None,) * (nl + 2)) + (t, cols), lambda *a: (layer, lax.axis_index("c"), *a[:nl], a[nl], 0))
        out_shape, aliases = jax.ShapeDtypeStruct((2, 2) + recv.shape, out_dtype), {2: 0}
    out = pl.pallas_call(
        body, name=name, grid=(*grid_lead, rows // t), in_specs=in_specs, out_specs=out_spec,
        out_shape=out_shape, input_output_aliases=aliases,
        compiler_params=pltpu.CompilerParams(dimension_semantics=("parallel",) * (nl + 1), vmem_limit_bytes=VMEM_LIMIT),
    )(*args)
    return out if into is None else out.reshape(shape)


def _adamw(w, g, m, v, name):
    shape = w.shape
    w2, g2, m2, v2 = _as2d(w), _as2d(g), _as2d(m), _as2d(v)
    rows, cols = w2.shape
    t = _ew_rows(rows, cols)

    def body(w_ref, g_ref, m_ref, v_ref, d_ref, nm_ref, nv_ref):
        gr = g_ref[...]
        nm = ADAM_B1 * m_ref[...] + (1.0 - ADAM_B1) * gr
        nv = ADAM_B2 * v_ref[...] + (1.0 - ADAM_B2) * (gr * gr)
        m_hat = nm / (1.0 - ADAM_B1 ** ADAM_STEP)
        v_hat = nv / (1.0 - ADAM_B2 ** ADAM_STEP)
        d_ref[...] = -ADAM_LR * (m_hat / (jnp.sqrt(v_hat) + ADAM_EPS) + ADAM_WD * w_ref[...])
        nm_ref[...] = nm
        nv_ref[...] = nv

    sp = pl.BlockSpec((t, cols), lambda i: (i, 0))
    sh = jax.ShapeDtypeStruct((rows, cols), F32)
    d, nm, nv = pl.pallas_call(body, name=name, grid=(rows // t,), in_specs=[sp] * 4, out_specs=[sp] * 3,
                               out_shape=[sh] * 3, compiler_params=_cp("parallel"))(w2, g2, m2, v2)
    return d.reshape(shape), nm.reshape(shape), nv.reshape(shape)


def _place():
    return lax.axis_index("x"), lax.axis_index("y"), lax.axis_index("c")


def _gather_weights(arrs, name):
    n = len(arrs)

    def body(*refs):
        outs = refs[n:2 * n]
        send, recv = refs[2 * n:]
        x, y, c = _place()
        s_me, s_x, s_y, s_d = 2 * x + y, 2 * (1 - x) + y, 2 * x + (1 - y), 2 * (1 - x) + (1 - y)
        xpeer, ypeer, sib = (1 - x, y, c), (x, 1 - y, c), (x, y, 1 - c)

        def rc(k, t, src, dst, to):
            return pltpu.make_async_remote_copy(src_ref=src, dst_ref=dst, send_sem=send.at[k, t], recv_sem=recv.at[k, t],
                                                device_id=to, device_id_type=MESH)

        def piece(k, s, top):
            rq = outs[k].shape[2] // 2
            return outs[k].at[s, c, pl.ds(0 if top else rq, rq)]

        sent = []

        def start(k, t, ref, to):
            cp = rc(k, t, ref, ref, to)
            cp.start()
            sent.append(cp)

        for k in range(n):
            start(k, 0, outs[k].at[s_me, c], xpeer)
            start(k, 1, outs[k].at[s_me, c], ypeer)
        for k in range(n):
            got = outs[k].at[s_x, c]
            rc(k, 0, got, got, xpeer).wait_recv()
            start(k, 2, piece(k, s_x, True), ypeer)
            start(k, 3, got, sib)
        for k in range(n):
            got = outs[k].at[s_y, c]
            rc(k, 1, got, got, ypeer).wait_recv()
            start(k, 6, piece(k, s_y, False), xpeer)
            start(k, 4, got, sib)
        for k in range(n):
            top, bottom = piece(k, s_d, True), piece(k, s_d, False)
            rc(k, 2, top, top, ypeer).wait_recv()
            rc(k, 6, bottom, bottom, xpeer).wait_recv()
            start(k, 5, outs[k].at[s_d, c], sib)
        for k in range(n):
            for t, s in ((3, s_x), (4, s_y), (5, s_d)):
                got = outs[k].at[s, 1 - c]
                rc(k, t, got, got, sib).wait_recv()
        for cp in sent:
            cp.wait_send()

    return pl.pallas_call(
        body, name=name, in_specs=[HBM] * n, out_specs=[HBM] * n,
        out_shape=[jax.ShapeDtypeStruct(a.shape, a.dtype) for a in arrs],
        input_output_aliases={k: k for k in range(n)},
        scratch_shapes=[pltpu.SemaphoreType.DMA((n, 7)), pltpu.SemaphoreType.DMA((n, 7))],
    )(*arrs)


def _exchange(arrs, axis, name, half=0):
    n = len(arrs)

    def body(*refs):
        srcs, outs = refs[:n], refs[n:2 * n]
        send, recv = refs[2 * n:]
        x, y, c = _place()
        p = {"x": x, "y": y, "c": c}[axis]
        peer = {"x": (1 - x, y, c), "y": (x, 1 - y, c), "c": (x, y, 1 - c)}[axis]
        cps = []
        for k in range(n):
            src = (srcs[k], srcs[k].at[1 - p], srcs[k].at[:, 1 - p])[half]
            cp = pltpu.make_async_remote_copy(src_ref=src, dst_ref=outs[k], send_sem=send.at[k], recv_sem=recv.at[k],
                                              device_id=peer, device_id_type=MESH)
            cp.start()
            cps.append(cp)
        for cp in cps:
            cp.wait()

    def out_shape(a):
        return (a.shape, a.shape[1:], a.shape[:1] + a.shape[2:])[half]

    return pl.pallas_call(
        body, name=name, in_specs=[HBM] * n, out_specs=[HBM] * n,
        out_shape=[jax.ShapeDtypeStruct(out_shape(a), a.dtype) for a in arrs],
        scratch_shapes=[pltpu.SemaphoreType.DMA((n,)), pltpu.SemaphoreType.DMA((n,))],
    )(*arrs)


def _share_halves(arrs, name):
    n = len(arrs)

    def body(*refs):
        outs = refs[n:2 * n]
        send, recv = refs[2 * n:]
        x, y, c = _place()
        cps = []
        for k in range(n):
            mine = outs[k].at[:, c]
            cp = pltpu.make_async_remote_copy(src_ref=mine, dst_ref=mine, send_sem=send.at[k], recv_sem=recv.at[k],
                                              device_id=(x, y, 1 - c), device_id_type=MESH)
            cp.start()
            cps.append(cp)
        for k in range(n):
            got = outs[k].at[:, 1 - c]
            pltpu.make_async_remote_copy(src_ref=got, dst_ref=got, send_sem=send.at[k], recv_sem=recv.at[k],
                                         device_id=(x, y, 1 - c), device_id_type=MESH).wait_recv()
        for cp in cps:
            cp.wait_send()

    return pl.pallas_call(
        body, name=name, in_specs=[HBM] * n, out_specs=[HBM] * n,
        out_shape=[jax.ShapeDtypeStruct(a.shape, a.dtype) for a in arrs],
        input_output_aliases={k: k for k in range(n)},
        scratch_shapes=[pltpu.SemaphoreType.DMA((n,)), pltpu.SemaphoreType.DMA((n,))],
    )(*arrs)


def _rs_exchange(arrs, phase, name):
    n = len(arrs)

    def body(*refs):
        srcs, outs = refs[:n], refs[n:2 * n]
        send, recv = refs[2 * n:]
        x, y, c = _place()
        xpeer, ypeer = (1 - x, y, c), (x, 1 - y, c)
        cps = []
        for k in range(n):
            if phase == 2:
                parts = ((srcs[k].at[1 - x, :, 0], xpeer), (srcs[k].at[:, 1 - y, 1], ypeer))
            else:
                parts = ((srcs[k].at[0, 1 - y], ypeer), (srcs[k].at[1, 1 - x], xpeer))
            for h, (src, to) in enumerate(parts):
                cp = pltpu.make_async_remote_copy(src_ref=src, dst_ref=outs[k].at[h], send_sem=send.at[k, h],
                                                  recv_sem=recv.at[k, h], device_id=to, device_id_type=MESH)
                cp.start()
                cps.append(cp)
        for cp in cps:
            cp.wait()

    def out_shape(a):
        return (2, 2) + a.shape[3:] if phase == 2 else (2,) + a.shape[2:]

    return pl.pallas_call(
        body, name=name, in_specs=[HBM] * n, out_specs=[HBM] * n,
        out_shape=[jax.ShapeDtypeStruct(out_shape(a), a.dtype) for a in arrs],
        scratch_shapes=[pltpu.SemaphoreType.DMA((n, 2)), pltpu.SemaphoreType.DMA((n, 2))],
    )(*arrs)


_REL = tuple((dx, dy, dc) for dx in (0, 1) for dy in (0, 1) for dc in (0, 1))[1:]
SEM = pl.BlockSpec(memory_space=pltpu.SEMAPHORE)
DATAFLOW = pltpu.SideEffectType.DATAFLOW_SIDE_EFFECTING


def _flip(v, d):
    return 1 - v if d else v


def _rs_direct_copies(srcs, land, send, recv):
    x, y, c = _place()
    cps = []
    for k in range(len(srcs)):
        for r, (dx, dy, dc) in enumerate(_REL):
            px, py, pc = _flip(x, dx), _flip(y, dy), _flip(c, dc)
            cps.append(pltpu.make_async_remote_copy(
                src_ref=srcs[k].at[2 * px + py, pc], dst_ref=land[k].at[r], send_sem=send.at[k * len(_REL) + r],
                recv_sem=recv.at[k * len(_REL) + r], device_id=(px, py, pc), device_id_type=MESH))
    return cps


def _rs_direct_start(grads, thru, name):
    n = len(grads)
    lands = [pltpu.with_memory_space_constraint(lax.empty((len(_REL),) + g.shape[2:], g.dtype), pltpu.HBM) for g in grads]

    def body(*refs):
        for cp in _rs_direct_copies(refs[:n], refs[n + 1:2 * n + 1], refs[2 * n + 1], refs[2 * n + 2]):
            cp.start()

    sems = pltpu.SemaphoreType.DMA((n * len(_REL),))
    keep = [pltpu.HBM(a.shape, a.dtype) for a in (*grads, thru, *lands)]
    out = pl.pallas_call(
        body, name=name, in_specs=[HBM] * (2 * n + 1), out_specs=(SEM, SEM) + (HBM,) * (2 * n + 1),
        out_shape=(sems, sems, *keep), input_output_aliases={i: 2 + i for i in range(2 * n + 1)},
        compiler_params=pltpu.CompilerParams(has_side_effects=DATAFLOW),
    )(*[pltpu.with_memory_space_constraint(a, pltpu.HBM) for a in (*grads, thru)], *lands)
    return out[0], out[1], out[2:2 + n], out[2 + n], out[3 + n:]


def _rs_direct_wait(send, recv, grads, lands, after, name):
    n = len(grads)

    def body(*refs):
        for cp in _rs_direct_copies(refs[:n], refs[n:2 * n], refs[2 * n], refs[2 * n + 1]):
            cp.wait_send()
            cp.wait_recv()

    keep = [pltpu.HBM(a.shape, a.dtype) for a in (*grads, *lands)]
    out = pl.pallas_call(
        body, name=name, in_specs=[HBM] * (2 * n) + [SEM, SEM, pl.BlockSpec(memory_space=pl.ANY)], out_specs=(HBM,) * (2 * n),
        out_shape=tuple(keep), input_output_aliases={i: i for i in range(2 * n)},
        compiler_params=pltpu.CompilerParams(has_side_effects=DATAFLOW),
    )(*grads, *lands, send, recv, after)
    return out[:n], out[n:]


_REL6 = tuple(r for r in _REL if r[0] or r[1])


def _gather_direct_copies(bufs, send, recv):
    x, y, c = _place()
    cps = []
    for k in range(len(bufs)):
        mine = bufs[k].at[2 * x + y, c]
        for r, (dx, dy, dc) in enumerate(_REL6):
            cps.append(pltpu.make_async_remote_copy(
                src_ref=mine, dst_ref=mine, send_sem=send.at[k * len(_REL6) + r], recv_sem=recv.at[k * len(_REL6) + r],
                device_id=(_flip(x, dx), _flip(y, dy), _flip(c, dc)), device_id_type=MESH))
    return cps


def _gather_direct_start(bufs, thru, name):
    n = len(bufs)

    def body(*refs):
        for cp in _gather_direct_copies(refs[:n], refs[n + 1], refs[n + 2]):
            cp.start()

    sems = pltpu.SemaphoreType.DMA((n * len(_REL6),))
    out = pl.pallas_call(
        body, name=name, in_specs=[HBM] * (n + 1), out_specs=(SEM, SEM) + (HBM,) * (n + 1),
        out_shape=(sems, sems, *[pltpu.HBM(a.shape, a.dtype) for a in (*bufs, thru)]),
        input_output_aliases={i: 2 + i for i in range(n + 1)},
        compiler_params=pltpu.CompilerParams(has_side_effects=DATAFLOW),
    )(*[pltpu.with_memory_space_constraint(a, pltpu.HBM) for a in (*bufs, thru)])
    return out[0], out[1], out[2:2 + n], out[2 + n]


def _gather_direct_wait(send, recv, bufs, after, name):
    n = len(bufs)

    def body(*refs):
        for cp in _gather_direct_copies(refs[:n], refs[n], refs[n + 1]):
            cp.wait_send()
            cp.wait_recv()

    return pl.pallas_call(
        body, name=name, in_specs=[HBM] * n + [SEM, SEM, pl.BlockSpec(memory_space=pl.ANY)], out_specs=(HBM,) * n,
        out_shape=tuple(pltpu.HBM(a.shape, a.dtype) for a in bufs), input_output_aliases={i: i for i in range(n)},
        compiler_params=pltpu.CompilerParams(has_side_effects=DATAFLOW),
    )(*bufs, send, recv, after)


def _rs_direct_sum(grad, land, layer, name, into=None):
    _, _, rows, cols = grad.shape
    t = _rs_rows(rows, cols)
    npieces = len(_REL) + 1

    def body(g_ref, l_ref, *rest):
        o_ref, acc_ref = rest[-2], rest[-1]
        j = pl.program_id(1)

        @pl.when(j == 0)
        def _():
            acc_ref[...] = g_ref[...].astype(F32)

        @pl.when(j > 0)
        def _():
            acc_ref[...] += l_ref[...].astype(F32)

        @pl.when(j == npieces - 1)
        def _():
            o_ref[...] = acc_ref[...]

    def mine(i, j):
        x, y, c = _place()
        return (2 * x + y, c, i, 0)

    in_specs = [pl.BlockSpec((None, None, t, cols), mine),
                pl.BlockSpec((None, t, cols), lambda i, j: (jnp.maximum(j - 1, 0), i, 0))]
    args = [grad, land]
    if into is not None:
        in_specs.append(pl.BlockSpec(memory_space=pl.ANY))
        args.append(into)
    return pl.pallas_call(
        body, name=name, grid=(rows // t, npieces), in_specs=in_specs,
        out_specs=pl.BlockSpec((None, None, t, cols), lambda i, j: (layer, lax.axis_index("c"), i, 0)),
        scratch_shapes=[pltpu.VMEM((t, cols), F32)],
        out_shape=jax.ShapeDtypeStruct((2, 2, rows, cols), F32), input_output_aliases={} if into is None else {2: 0},
        compiler_params=_cp("parallel", "arbitrary"),
    )(*args)


def _rs_tree(grads, into, layer):
    r1 = _exchange(grads, "c", "rs_c", half=2)
    a1 = [_rs_add(g, r, lambda s, p: (s, p[2]), (4,), BF16, f"rs_add1_{k}") for k, (g, r) in enumerate(zip(grads, r1))]
    a1 = [a.reshape(2, 2, 2, a.shape[1] // 2, a.shape[2]) for a in a1]
    r2 = _rs_exchange(a1, 2, "rs_p2")
    a2 = [_rs_add(a, r, lambda h, j, p: (p[0] * (1 - h) + j * h, j * (1 - h) + p[1] * h, h), (2, 2), BF16, f"rs_add2_{k}")
          for k, (a, r) in enumerate(zip(a1, r2))]
    r3 = _rs_exchange(a2, 3, "rs_p3")
    return [_rs_add(a, r, lambda h, p: (h, p[1] * (1 - h) + p[0] * h), (2,), F32, f"rs_add3_{k}", into=(buf, layer))
            for k, (a, r, buf) in enumerate(zip(a2, r3, into))]


def _ar_copies(buf, land, send, recv):
    x, y, c = _place()
    return [pltpu.make_async_remote_copy(src_ref=buf, dst_ref=land.at[r], send_sem=send.at[r], recv_sem=recv.at[r],
                                         device_id=(_flip(x, dx), _flip(y, dy), _flip(c, dc)), device_id_type=MESH)
            for r, (dx, dy, dc) in enumerate(_REL)]


def _ar_start(buf, name):
    land = pltpu.with_memory_space_constraint(lax.empty((len(_REL),) + buf.shape, buf.dtype), pltpu.HBM)

    def body(buf_ref, land_ref, send, recv, *_):
        for cp in _ar_copies(buf_ref, land_ref, send, recv):
            cp.start()

    sems = pltpu.SemaphoreType.DMA((len(_REL),))
    return pl.pallas_call(
        body, name=name, in_specs=[HBM, HBM], out_specs=(SEM, SEM, HBM, HBM),
        out_shape=(sems, sems, pltpu.HBM(buf.shape, buf.dtype), pltpu.HBM(land.shape, land.dtype)),
        input_output_aliases={0: 2, 1: 3}, compiler_params=pltpu.CompilerParams(has_side_effects=DATAFLOW),
    )(pltpu.with_memory_space_constraint(buf, pltpu.HBM), land)


def _ar_wait(send, recv, buf, land, after, name):
    def body(buf_ref, land_ref, send_ref, recv_ref, *_):
        for cp in _ar_copies(buf_ref, land_ref, send_ref, recv_ref):
            cp.wait_send()
            cp.wait_recv()

    return pl.pallas_call(
        body, name=name, in_specs=[HBM, HBM, SEM, SEM, pl.BlockSpec(memory_space=pl.ANY)], out_specs=(HBM, HBM),
        out_shape=(pltpu.HBM(buf.shape, buf.dtype), pltpu.HBM(land.shape, land.dtype)), input_output_aliases={0: 0, 1: 1},
        compiler_params=pltpu.CompilerParams(has_side_effects=DATAFLOW),
    )(buf, land, send, recv, after)


def _ar_sum(buf, land, name):
    rows, cols = buf.shape
    t = _rs_rows(rows, cols)

    def slot(i, j):
        x, y, c = _place()
        xd, yd, cd = j // 4, (j // 2) % 2, j % 2
        rel = 4 * (x + xd - 2 * x * xd) + 2 * (y + yd - 2 * y * yd) + (c + cd - 2 * c * cd)
        return (jnp.maximum(rel - 1, 0), i, 0)

    def body(b_ref, l_ref, o_ref, acc_ref):
        j = pl.program_id(1)
        x, y, c = _place()
        val = jnp.where(j == 4 * x + 2 * y + c, b_ref[...], l_ref[...])

        @pl.when(j == 0)
        def _():
            acc_ref[...] = val

        @pl.when(j > 0)
        def _():
            acc_ref[...] += val

        @pl.when(j == len(_REL))
        def _():
            o_ref[...] = acc_ref[...]

    sp = pl.BlockSpec((t, cols), lambda i, j: (i, 0))
    return pl.pallas_call(
        body, name=name, grid=(rows // t, len(_REL) + 1), in_specs=[sp, pl.BlockSpec((None, t, cols), slot)], out_specs=sp,
        out_shape=jax.ShapeDtypeStruct((rows, cols), F32), scratch_shapes=[pltpu.VMEM((t, cols), F32)],
        compiler_params=_cp("parallel", "arbitrary"),
    )(buf, land)


def _pad128(v):
    return jnp.zeros((1, 128), F32).at[0, :v.shape[0]].set(v)


def _layer_fwd(l, x, p):
    h1 = _rms_fwd(x, p["norm1_w"], f"rms1_{l}")
    proj = _matmul(h1, p["w_in"], "nn", name=f"mm_in_{l}", tn=768)
    y_pool = _pool_fwd(proj, p["pool_w"], p["pool_b"], p["pool_scale"], f"pool_{l}")
    qkv = _gdn_conv_fwd(proj, p["gdn_conv_w"], f"gconv_{l}")
    alog, dtb = _pad128(p["gdn_a_log"]), _pad128(p["gdn_dt_bias"])
    gw, gu, gqd, gkd, gat, tinv, gc = _gdn_prep(qkv, proj, alog, dtb, f"gprep_{l}")
    o, states = _gdn_scan(gw, gu, gqd, gkd, gat, gc, f"gscan_{l}")
    hl = _lru_fwd(proj, p["lru_conv_w"], p["lru_conv_b"], p["lru_wa"], p["lru_ba"], p["lru_wx"], p["lru_bx"],
                  p["lru_lambda"], f"lru_{l}")
    mixed = _mix_out(o, proj, hl, y_pool, p["gdn_norm_w"].reshape(1, GDN_DH), f"mix_{l}")
    x2 = _matmul(mixed, p["w_out"], "nn", name=f"mm_out_{l}", res=x)
    h2 = _rms_fwd(x2, p["norm2_w"], f"rms2_{l}")
    up = _matmul(h2, p["ffn_up"], "nn", name=f"mm_up_{l}", b_split=True)
    act = _ffn_act_fwd(up, p["ffn_conv_w"], f"ffn_{l}")
    x3 = _matmul(act, p["ffn_down"], "nn", name=f"mm_down_{l}", res=x2)
    saved = dict(x=x, h1=h1, proj=proj, qkv=qkv, gdn=(gw, gu, gqd, gkd, gat, gc), tinv=tinv, states=states, o=o, hl=hl, mixed=mixed,
                 x2=x2, h2=h2, up=up, act=act, alog=alog, dtb=dtb)
    return x3, saved


def _layer_bwd(l, dx3, p, s, after_ffn=None):
    g = {}
    dact = _matmul(dx3, p["ffn_down"], "nt", name=f"mm_ddown_{l}")
    g["ffn_down"] = _matmul(s["act"], dx3, "tn", name=f"mm_gdown_{l}", out_dtype=BF16)
    dup, g["ffn_conv_w"] = _ffn_act_bwd(s["up"], dact, p["ffn_conv_w"], f"ffn_b_{l}")
    dh2 = _matmul(dup, p["ffn_up"], "nt", name=f"mm_dup_{l}", b_split=True, tk=3072)
    g["ffn_up"] = _matmul(s["h2"], dup, "tn", name=f"mm_gup_{l}", b_split=True, o_split=4, tk=4096, out_dtype=BF16)
    dx2, g["norm2_w"] = _rms_bwd(s["x2"], p["norm2_w"], dh2, dx3, f"rms2_b_{l}")
    if after_ffn is not None:
        dx2 = after_ffn(dx2, g)
    dmixed = _matmul(dx2, p["w_out"], "nt", name=f"mm_dout_{l}")
    g["w_out"] = _matmul(s["mixed"], dx2, "tn", name=f"mm_gout_{l}", out_dtype=BF16)
    proj = s["proj"]
    du_pool, g["pool_w"], g["pool_b"], g["pool_scale"] = _pool_bwd(proj, dmixed, p["pool_w"], p["pool_b"], p["pool_scale"], f"pool_b_{l}")
    gw, gu, gqd, gkd, gat, gc = s["gdn"]
    dw, du, dqd, dkd, dat, dgl, dz, g["gdn_norm_w"] = _gdn_scan_bwd(
        gw, gu, gqd, gkd, gat, gc, s["states"], s["o"], proj, dmixed, p["gdn_norm_w"].reshape(1, GDN_DH), f"gscan_b_{l}")
    dqkv, dab, gal, gdt = _gdn_prep_bwd(s["qkv"], proj, s["alog"], s["dtb"], s["tinv"], dw, du, dqd, dkd, dat, dgl, f"gprep_b_{l}")
    g["gdn_a_log"], g["gdn_dt_bias"] = gal[0, :GDN_H], gdt[0, :GDN_H]
    dpre, g["gdn_conv_w"] = _gdn_conv_bwd(proj, dqkv, p["gdn_conv_w"], f"gconv_b_{l}")
    (dxr, dgr, g["lru_conv_w"], g["lru_conv_b"], g["lru_wa"], g["lru_ba"], g["lru_wx"], g["lru_bx"], g["lru_lambda"]) = _lru_bwd(
        proj, s["hl"], dmixed, p["lru_conv_w"], p["lru_conv_b"], p["lru_wa"], p["lru_ba"], p["lru_wx"], p["lru_bx"],
        p["lru_lambda"], f"lru_b_{l}")
    S = proj.shape[0]
    dproj = jnp.concatenate([dpre, dz, dxr, dgr, du_pool, dab, jnp.zeros((S, PCOLS - PAB - 128), BF16)], axis=1)
    dh1 = _matmul(dproj, p["w_in"], "nt", name=f"mm_din_{l}", tk=1792)
    g["w_in"] = _matmul(s["h1"], dproj, "tn", name=f"mm_gin_{l}", tn=768, tk=4096, out_dtype=BF16)
    dx, g["norm1_w"] = _rms_bwd(s["x"], p["norm1_w"], dh1, dx2, f"rms1_b_{l}")
    return dx, g


_IN_PERM = ((512, 3584), (3596, 5132), (0, 512), (3584, 3596))


def _w_in_to_proj(w):
    parts = [w[:, a:b] for a, b in _IN_PERM]
    return jnp.concatenate(parts + [jnp.zeros((w.shape[0], PCOLS - IN_COLS), w.dtype)], axis=1)


def _proj_to_w_in(g):
    return jnp.concatenate([g[:, PPOOL:PPOOL + 512], g[:, 0:3072], g[:, PAB:PAB + 12], g[:, 3072:PPOOL]], axis=1)


def _rows_to_mixed(w):
    return jnp.concatenate([w[512:], w[:512]], axis=0)


def _mixed_to_rows(g):
    return jnp.concatenate([g[1536:], g[:1536]], axis=0)


SMALL_SHARDED = ("gdn_conv_w", "lru_conv_w", "ffn_conv_w")
BIG = ("w_in", "w_out", "ffn_up", "ffn_down")
SMALL_REPLICATED = ("norm1_w", "pool_w", "pool_b", "pool_scale", "gdn_a_log", "gdn_dt_bias", "gdn_norm_w", "lru_conv_b",
                    "lru_wa", "lru_ba", "lru_wx", "lru_bx", "lru_lambda", "norm2_w")
WEIGHTS = ("norm1_w", "w_in", "pool_w", "pool_b", "pool_scale", "gdn_conv_w", "gdn_a_log", "gdn_dt_bias", "gdn_norm_w",
           "lru_conv_w", "lru_conv_b", "lru_wa", "lru_ba", "lru_wx", "lru_bx", "lru_lambda", "w_out", "norm2_w", "ffn_up",
           "ffn_conv_w", "ffn_down", "final_norm_w")
FLAT_COLS = 1024


def _pack(arrs):
    flat = jnp.concatenate([a.reshape(-1) for a in arrs])
    rows = -(-flat.shape[0] // (8 * FLAT_COLS)) * 8
    return jnp.pad(flat, (0, rows * FLAT_COLS - flat.shape[0])).reshape(rows, FLAT_COLS)


def _unpack(buf, like):
    flat = buf.reshape(-1)
    out, off = [], 0
    for a in like:
        size = 1
        for d in a.shape:
            size *= d
        out.append(flat[off:off + size].reshape(a.shape))
        off += size
    return out


def kernel(x, norm1_w, w_in, pool_w, pool_b, pool_scale, gdn_conv_w, gdn_a_log, gdn_dt_bias, gdn_norm_w, lru_conv_w, lru_conv_b, lru_wa, lru_ba, lru_wx, lru_bx, lru_lambda, w_out, norm2_w, ffn_up, ffn_conv_w, ffn_down, final_norm_w, loss_target, m_norm1_w, m_w_in, m_pool_w, m_pool_b, m_pool_scale, m_gdn_conv_w, m_gdn_a_log, m_gdn_dt_bias, m_gdn_norm_w, m_lru_conv_w, m_lru_conv_b, m_lru_wa, m_lru_ba, m_lru_wx, m_lru_bx, m_lru_lambda, m_w_out, m_norm2_w, m_ffn_up, m_ffn_conv_w, m_ffn_down, m_final_norm_w, v_norm1_w, v_w_in, v_pool_w, v_pool_b, v_pool_scale, v_gdn_conv_w, v_gdn_a_log, v_gdn_dt_bias, v_gdn_norm_w, v_lru_conv_w, v_lru_conv_b, v_lru_wa, v_lru_ba, v_lru_wx, v_lru_bx, v_lru_lambda, v_w_out, v_norm2_w, v_ffn_up, v_ffn_conv_w, v_ffn_down, v_final_norm_w):
    W = dict(norm1_w=norm1_w, w_in=w_in, pool_w=pool_w, pool_b=pool_b, pool_scale=pool_scale, gdn_conv_w=gdn_conv_w,
             gdn_a_log=gdn_a_log, gdn_dt_bias=gdn_dt_bias, gdn_norm_w=gdn_norm_w, lru_conv_w=lru_conv_w, lru_conv_b=lru_conv_b,
             lru_wa=lru_wa, lru_ba=lru_ba, lru_wx=lru_wx, lru_bx=lru_bx, lru_lambda=lru_lambda, w_out=w_out, norm2_w=norm2_w,
             ffn_up=ffn_up, ffn_conv_w=ffn_conv_w, ffn_down=ffn_down, final_norm_w=final_norm_w)
    M = dict(norm1_w=m_norm1_w, w_in=m_w_in, pool_w=m_pool_w, pool_b=m_pool_b, pool_scale=m_pool_scale, gdn_conv_w=m_gdn_conv_w,
             gdn_a_log=m_gdn_a_log, gdn_dt_bias=m_gdn_dt_bias, gdn_norm_w=m_gdn_norm_w, lru_conv_w=m_lru_conv_w,
             lru_conv_b=m_lru_conv_b, lru_wa=m_lru_wa, lru_ba=m_lru_ba, lru_wx=m_lru_wx, lru_bx=m_lru_bx, lru_lambda=m_lru_lambda,
             w_out=m_w_out, norm2_w=m_norm2_w, ffn_up=m_ffn_up, ffn_conv_w=m_ffn_conv_w, ffn_down=m_ffn_down,
             final_norm_w=m_final_norm_w)
    V = dict(norm1_w=v_norm1_w, w_in=v_w_in, pool_w=v_pool_w, pool_b=v_pool_b, pool_scale=v_pool_scale, gdn_conv_w=v_gdn_conv_w,
             gdn_a_log=v_gdn_a_log, gdn_dt_bias=v_gdn_dt_bias, gdn_norm_w=v_gdn_norm_w, lru_conv_w=v_lru_conv_w,
             lru_conv_b=v_lru_conv_b, lru_wa=v_lru_wa, lru_ba=v_lru_ba, lru_wx=v_lru_wx, lru_bx=v_lru_bx, lru_lambda=v_lru_lambda,
             w_out=v_w_out, norm2_w=v_norm2_w, ffn_up=v_ffn_up, ffn_conv_w=v_ffn_conv_w, ffn_down=v_ffn_down,
             final_norm_w=v_final_norm_w)
    S = x.shape[1]
    xs = x.reshape(S, D_MODEL)
    tgt = loss_target.reshape(S, D_MODEL)
    mx, my, mc = _place()
    shard = 2 * mx + my

    small_sh = jnp.concatenate([W[k].reshape(N_LAYERS, -1) for k in SMALL_SHARDED], axis=1)
    n_small = small_sh.shape[1]
    pad = -n_small % 1024
    small_sh = jnp.pad(small_sh, ((0, 0), (0, pad))).reshape(N_LAYERS, -1, 1024)

    def own_slots(l):
        out = []
        for w in [W[k][l].astype(BF16) for k in BIG] + [small_sh[l]]:
            buf = lax.dynamic_update_slice(lax.empty((4,) + w.shape, w.dtype), w[None], (shard,) + (0,) * w.ndim)
            out.append(buf.reshape(4, 2, w.shape[0] // 2, w.shape[1]))
        return out

    def layer_params(l, gathered):
        g_in, g_out, g_up, g_down, g_small = [g.reshape(4, 2 * g.shape[2], g.shape[3]) for g in gathered]
        p = {k: W[k][l] for k in SMALL_REPLICATED}
        p["w_in"] = _w_in_to_proj(jnp.transpose(g_in, (1, 0, 2)).reshape(g_in.shape[1], IN_COLS))
        p["w_out"] = _rows_to_mixed(g_out.reshape(D_MODEL, D_MODEL))
        p["ffn_up"] = g_up
        p["ffn_down"] = g_down.reshape(D_FF, D_MODEL)
        g_small = g_small.reshape(4, -1)[:, :n_small]
        off = 0
        for k in SMALL_SHARDED:
            taps, width = W[k].shape[1], W[k].shape[2]
            piece = g_small[:, off:off + taps * width].reshape(4, taps, width)
            p[k] = jnp.transpose(piece, (1, 0, 2)).reshape(taps, 4 * width)
            off += taps * width
        return p

    layers, saved = [None] * N_LAYERS, [None] * N_LAYERS
    gathered0 = _gather_weights(own_slots(0), "gather_weights")
    g_send, g_recv, bufs1, first0 = _gather_direct_start(own_slots(1), gathered0[0], "gather_direct_start")
    layers[0] = layer_params(0, [first0, *gathered0[1:]])
    h, saved[0] = _layer_fwd(0, xs, layers[0])
    layers[1] = layer_params(1, _gather_direct_wait(g_send, g_recv, bufs1, h, "gather_direct_wait"))
    h, saved[1] = _layer_fwd(1, h, layers[1])
    loss_part, dh, g_final = _loss_head(h, final_norm_w, tgt, "loss_head")

    def big_partials(g_layer, names=BIG):
        out = []
        for k in names:
            g = g_layer[k]
            if k == "w_in":
                g = _proj_to_w_in(g)
                g = jnp.transpose(g.reshape(g.shape[0], 4, IN_COLS // 4), (1, 0, 2))
            elif k == "w_out":
                g = _mixed_to_rows(g).reshape(4, D_MODEL // 4, D_MODEL)
            elif k == "ffn_down":
                g = g.reshape(4, D_FF // 4, D_MODEL)
            out.append(g.reshape(4, 2, g.shape[1] // 2, g.shape[2]))
        return out

    FFN, MIX = ("ffn_up", "ffn_down"), ("w_in", "w_out")
    grads = [None] * N_LAYERS
    dh, grads[1] = _layer_bwd(1, dh, layers[1], saved[1])
    send1, recv1, part1, dh, lands1 = _rs_direct_start(big_partials(grads[1]), dh, "rs_direct_start_1")
    sent0 = []

    def after_ffn(dx2, g):
        send0, recv0, part0, dx2, lands0 = _rs_direct_start(big_partials(g, FFN), dx2, "rs_direct_start_0")
        sent0.extend([send0, recv0, part0, lands0])
        return dx2

    dh, grads[0] = _layer_bwd(0, dh, layers[0], saved[0], after_ffn)

    small_names = SMALL_REPLICATED + SMALL_SHARDED
    small_list = [jnp.stack([grads[l][k].reshape(W[k].shape[1:]) if k in SMALL_REPLICATED else grads[l][k] for l in range(N_LAYERS)])
                  for k in small_names]
    small_list += [g_final.reshape(D_MODEL), loss_part[0, 0:1]]
    a_send, a_recv, packed, a_land = _ar_start(_pack(small_list), "ar_start")

    part1, lands1 = _rs_direct_wait(send1, recv1, part1, lands1, dh, "rs_direct_wait_1")
    part0, lands0 = _rs_direct_wait(*sent0, dh, "rs_direct_wait_0")
    red = {k: _rs_direct_sum(g, ld, 1, f"rs_sum_1_{k}") for k, g, ld in zip(BIG, part1, lands1)}
    for k, g, ld in zip(FFN, part0, lands0):
        red[k] = _rs_direct_sum(g, ld, 0, f"rs_sum_0_{k}", into=red[k])
    for k, r in zip(MIX, _rs_tree(big_partials(grads[0], MIX), [red[k] for k in MIX], 0)):
        red[k] = r
    reduced = _share_halves([red[k] for k in BIG], "rs_share")
    big_g = {k: r.reshape(N_LAYERS, 2 * r.shape[2], r.shape[3]) for k, r in zip(BIG, reduced)}
    grad_x = dh.reshape(x.shape)

    packed, a_land = _ar_wait(a_send, a_recv, packed, a_land, reduced[0], "ar_wait")
    reduced = _unpack(_ar_sum(packed, a_land, "ar_sum"), small_list)
    small_g = dict(zip(small_names, reduced[:len(small_names)]))
    small_g["final_norm_w"] = reduced[-2]
    loss = reduced[-1][0]
    for k in SMALL_SHARDED:
        width = W[k].shape[2]
        small_g[k] = lax.dynamic_slice_in_dim(small_g[k], shard * width, width, axis=2)

    G, DELTA, NM, NV = {}, {}, {}, {}
    for k in BIG:
        G[k] = big_g[k]
        DELTA[k], NM[k], NV[k] = _adamw(W[k], G[k], M[k], V[k], f"adam_{k}")
    small_all = small_names + ("final_norm_w",)
    dl, nm, nv = _adamw(_pack([W[k] for k in small_all]), _pack([small_g[k] for k in small_all]),
                        _pack([M[k] for k in small_all]), _pack([V[k] for k in small_all]), "adam_small")
    like = [W[k] for k in small_all]
    for k, d_, m_, v_ in zip(small_all, _unpack(dl, like), _unpack(nm, like), _unpack(nv, like)):
        G[k], DELTA[k], NM[k], NV[k] = small_g[k], d_, m_, v_

    return (loss, grad_x, *[G[k] for k in WEIGHTS], *[DELTA[k] for k in WEIGHTS], *[NM[k] for k in WEIGHTS],
            *[NV[k] for k in WEIGHTS])
```

```python
import functools

import jax
import jax.numpy as jnp
from jax import lax
from jax.experimental import pallas as pl
from jax.experimental.pallas import tpu as pltpu

F32 = jnp.float32
BF16 = jnp.bfloat16
_MXU = jnp.bfloat16

D_MODEL = 2048
N_LAYERS = 2
POOL_W = 512
POOL_G = 4
POOL_GD = 128
POOL_WINDOWS = (2, 4, 8, 16)
POOL_HALO = 16
GDN_W = 768
GDN_H = 6
GDN_DH = 128
GDN_C = 64
LRU_W = 768
LRU_NB = 6
LRU_BD = 128
LRU_C = 8.0
D_FF = 6144
EPS = 1e-6
IN_COLS = 5132
HALO = 8

PQ, PK, PV, PZ, PXR, PGR, PPOOL, PAB, PCOLS = 0, 768, 1536, 2304, 3072, 3840, 4608, 5120, 5376
CB = 768

ADAM_LR = 0.001
ADAM_B1 = 0.9
ADAM_B2 = 0.999
ADAM_EPS = 1e-08
ADAM_WD = 0.01
ADAM_STEP = 10

VMEM_LIMIT = 56 * 1024 * 1024
MESH = pl.DeviceIdType.MESH
HBM = pl.BlockSpec(memory_space=pltpu.HBM)


def _cp(*sem):
    return pltpu.CompilerParams(dimension_semantics=sem, vmem_limit_bytes=VMEM_LIMIT)


def _dg(a, b, ta, tb):
    dims = (((0 if ta else 1,), (1 if tb else 0,)), ((), ()))
    return lax.dot_general(a, b, dims, preferred_element_type=F32)


def _split2(a):
    hi = a.astype(BF16)
    lo = (a - hi.astype(F32)).astype(BF16)
    return hi, lo


def _mm_raw(a, b, ta, tb, hi):
    if _MXU == F32:
        return _dg(a, b, ta, tb)
    if not hi:
        return _dg(a.astype(_MXU), b.astype(_MXU), ta, tb)
    a1, a2 = _split2(a)
    b1, b2 = _split2(b)
    return _dg(a1, b1, ta, tb) + (_dg(a1, b2, ta, tb) + _dg(a2, b1, ta, tb))


@functools.partial(jax.custom_vjp, nondiff_argnums=(2, 3, 4))
def _mm(a, b, ta=False, tb=False, hi=False):
    return _mm_raw(a, b, ta, tb, hi)


def _mm_fwd(a, b, ta, tb, hi):
    return _mm_raw(a, b, ta, tb, hi), (a, b)


def _mm_bwd(ta, tb, hi, res, dc):
    a, b = res
    da = _mm(b, dc, tb, True, hi) if ta else _mm(dc, b, False, not tb, hi)
    db = _mm(dc, a, True, ta, hi) if tb else _mm(a, dc, not ta, False, hi)
    return da, db


_mm.defvjp(_mm_fwd, _mm_bwd)


def _mm01(m01, x):
    if _MXU == F32:
        return _dg(m01, x, False, False)
    m = m01.astype(BF16)
    x1 = x.astype(BF16)
    r = x - x1.astype(F32)
    x2 = r.astype(BF16)
    x3 = (r - x2.astype(F32)).astype(BF16)
    return _dg(m, x1, False, False) + (_dg(m, x2, False, False) + _dg(m, x3, False, False))


def _down(x, k):
    return x if k == 0 else pltpu.roll(x, k, 0)


def _up(x, k):
    return x if k == 0 else pltpu.roll(x, x.shape[0] - k, 0)


def _rows(shape):
    return lax.broadcasted_iota(jnp.int32, shape, 0)


def _lanes(shape):
    return lax.broadcasted_iota(jnp.int32, shape, 1)


def _matmul(a, b, mode, *, name, res=None, tm=1024, tn=1024, tk=2048, b_split=False, o_split=0, out_dtype=F32):
    ta, tb = mode == "tn", mode == "nt"
    a_split = a.ndim == 3
    if a_split:
        assert not ta
        M, K = a.shape[1], a.shape[0] * a.shape[2]
        tk = min(tk, a.shape[2])
    elif ta:
        K, M = a.shape
    else:
        M, K = a.shape
    if b_split:
        ns = b.shape[0]
        N = b.shape[1] if tb else ns * b.shape[2]
    else:
        N = b.shape[0] if tb else b.shape[1]
    tm, tn, tk = min(tm, M), min(tn, N), min(tk, K)
    if b_split:
        per = b.shape[2]
        if tb:
            tk = min(tk, per)
        else:
            tn = min(tn, per)
    if o_split:
        tn = min(tn, N // o_split)
    assert M % tm == 0 and N % tn == 0 and K % tk == 0, (name, M, N, K, tm, tn, tk)
    nk = K // tk
    if a_split:
        ka = a.shape[2] // tk
        a_spec = pl.BlockSpec((None, tm, tk), lambda i, j, k: (k // ka, i, k % ka))
    else:
        a_spec = pl.BlockSpec((tk, tm), lambda i, j, k: (k, i)) if ta else pl.BlockSpec((tm, tk), lambda i, j, k: (i, k))
    if not b_split:
        b_spec = pl.BlockSpec((tn, tk), lambda i, j, k: (j, k)) if tb else pl.BlockSpec((tk, tn), lambda i, j, k: (k, j))
    elif tb:
        kb = per // tk
        b_spec = pl.BlockSpec((None, tn, tk), lambda i, j, k: (k // kb, j, k % kb))
    else:
        nb = per // tn
        b_spec = pl.BlockSpec((None, tk, tn), lambda i, j, k: (j // nb, k, j % nb))
    if o_split:
        ob = (N // o_split) // tn
        out_shape = jax.ShapeDtypeStruct((o_split, M, N // o_split), out_dtype)
        o_spec = pl.BlockSpec((None, tm, tn), lambda i, j, k: (j // ob, i, j % ob))
    else:
        out_shape = jax.ShapeDtypeStruct((M, N), out_dtype)
        o_spec = pl.BlockSpec((tm, tn), lambda i, j, k: (i, j))
    in_specs = [a_spec, b_spec]
    args = [a, b]
    if res is not None:
        in_specs.append(pl.BlockSpec((tm, tn), lambda i, j, k: (i, j)))
        args.append(res)
    use_acc = nk > 1 and out_dtype != F32

    def body(*refs):
        a_ref, b_ref = refs[0], refs[1]
        o_ref = refs[2 + (res is not None)]
        acc_ref = refs[-1] if use_acc else o_ref
        p = _dg(a_ref[...].astype(_MXU), b_ref[...].astype(_MXU), ta, tb)
        first = p + refs[2][...] if res is not None else p
        if nk == 1:
            o_ref[...] = first.astype(o_ref.dtype)
        else:
            k = pl.program_id(2)

            @pl.when(k == 0)
            def _():
                acc_ref[...] = first

            @pl.when(k > 0)
            def _():
                acc_ref[...] += p

            if use_acc:
                @pl.when(k == nk - 1)
                def _():
                    o_ref[...] = acc_ref[...].astype(o_ref.dtype)

    return pl.pallas_call(
        body, name=name, grid=(M // tm, N // tn, nk), in_specs=in_specs, out_specs=o_spec, out_shape=out_shape,
        scratch_shapes=[pltpu.VMEM((tm, tn), F32)] if use_acc else [],
        compiler_params=_cp("parallel", "parallel", "arbitrary"),
    )(*args)


def _rms(x, w):
    return x * lax.rsqrt(jnp.mean(x * x, axis=-1, keepdims=True) + EPS) * w


def _row_tile(S, t=512):
    t = min(t, S)
    assert S % t == 0
    return t


def _rms_fwd(x, w, name):
    S, D = x.shape
    T = _row_tile(S)

    def body(x_ref, w_ref, o_ref):
        o_ref[...] = _rms(x_ref[...], w_ref[...]).astype(o_ref.dtype)

    return pl.pallas_call(
        body, name=name, grid=(S // T,),
        in_specs=[pl.BlockSpec((T, D), lambda i: (i, 0)), pl.BlockSpec((1, D), lambda i: (0, 0))],
        out_specs=pl.BlockSpec((T, D), lambda i: (i, 0)), out_shape=jax.ShapeDtypeStruct((S, D), BF16),
        compiler_params=_cp("parallel"),
    )(x, w.reshape(1, D))


def _rms_bwd(x, w, dh, dres, name):
    S, D = x.shape
    T = _row_tile(S)

    def body(x_ref, w_ref, dh_ref, dr_ref, dx_ref, gw_ref):
        _, vjp = jax.vjp(_rms, x_ref[...], w_ref[...])
        dx, dw = vjp(dh_ref[...])
        dx_ref[...] = dr_ref[...] + dx

        @pl.when(pl.program_id(0) == 0)
        def _():
            gw_ref[...] = jnp.zeros_like(gw_ref)

        gw_ref[...] += dw

    row = pl.BlockSpec((T, D), lambda i: (i, 0))
    vec = pl.BlockSpec((1, D), lambda i: (0, 0))
    return pl.pallas_call(
        body, name=name, grid=(S // T,), in_specs=[row, vec, row, row], out_specs=[row, vec],
        out_shape=[jax.ShapeDtypeStruct((S, D), F32), jax.ShapeDtypeStruct((1, D), F32)],
        compiler_params=_cp("arbitrary"),
    )(x, w.reshape(1, D), dh, dres)


def _loss_head(x, w, tgt, name):
    S, D = x.shape
    T = _row_tile(S)

    def body(x_ref, w_ref, t_ref, l_ref, dx_ref, gw_ref):
        y, vjp = jax.vjp(_rms, x_ref[...], w_ref[...])
        err = y - t_ref[...]
        part = 0.5 * jnp.sum(jnp.mean(err * err, axis=-1, keepdims=True), axis=0, keepdims=True)
        dx, dw = vjp(err * (1.0 / D))
        dx_ref[...] = dx

        @pl.when(pl.program_id(0) == 0)
        def _():
            gw_ref[...] = jnp.zeros_like(gw_ref)
            l_ref[...] = jnp.zeros_like(l_ref)

        gw_ref[...] += dw
        l_ref[...] += jnp.broadcast_to(part, l_ref.shape)

    row = pl.BlockSpec((T, D), lambda i: (i, 0))
    vec = pl.BlockSpec((1, D), lambda i: (0, 0))
    return pl.pallas_call(
        body, name=name, grid=(S // T,), in_specs=[row, vec, row],
        out_specs=[pl.BlockSpec((8, 128), lambda i: (0, 0)), row, vec],
        out_shape=[jax.ShapeDtypeStruct((8, 128), F32), jax.ShapeDtypeStruct((S, D), F32), jax.ShapeDtypeStruct((1, D), F32)],
        compiler_params=_cp("arbitrary"),
    )(x, w.reshape(1, D), tgt)


def _by_group(shape, vals):
    g = _lanes(shape) // POOL_GD
    out = vals[-1]
    for k in range(len(vals) - 2, -1, -1):
        out = jnp.where(g == k, vals[k], out)
    return out


def _pool_d(prev, u, t0):
    ext = jnp.concatenate([prev, u], axis=0)
    s2 = ext + _down(ext, 1)
    s4 = s2 + _down(s2, 2)
    s8 = s4 + _down(s4, 4)
    s16 = s8 + _down(s8, 8)
    ssel = _by_group(ext.shape, [s2, s4, s8, s16])[POOL_HALO:]
    win = _by_group(u.shape, [jnp.int32(w) for w in POOL_WINDOWS])
    cnt = jnp.minimum(t0 + _rows(u.shape) + 1, win).astype(F32)
    return ssel / cnt - u


def _pool_lin(d, w_ref, b):
    ys = [_mm(d[:, g * POOL_GD:(g + 1) * POOL_GD], w_ref[g]) for g in range(POOL_G)]
    return jnp.concatenate(ys, axis=1) + b


def _pool_fwd(proj, w, b, scale, name):
    S = proj.shape[0]
    T = _row_tile(S)
    r = T // POOL_HALO
    cb = PPOOL // POOL_W

    def body(u_ref, up_ref, w_ref, b_ref, sc_ref, y_ref):
        i = pl.program_id(0)
        prev = jnp.where(i > 0, up_ref[...], 0.0)
        d = _pool_d(prev, u_ref[...], i * T)
        y_ref[...] = _pool_lin(d, w_ref, b_ref[...]) * sc_ref[...]

    vec = pl.BlockSpec((1, POOL_W), lambda i: (0, 0))
    return pl.pallas_call(
        body, name=name, grid=(S // T,),
        in_specs=[pl.BlockSpec((T, POOL_W), lambda i: (i, cb)),
                  pl.BlockSpec((POOL_HALO, POOL_W), lambda i: (jnp.maximum(i * r - 1, 0), cb)),
                  pl.BlockSpec((POOL_G, POOL_GD, POOL_GD), lambda i: (0, 0, 0)), vec, vec],
        out_specs=pl.BlockSpec((T, POOL_W), lambda i: (i, 0)), out_shape=jax.ShapeDtypeStruct((S, POOL_W), F32),
        compiler_params=_cp("parallel"),
    )(proj, proj, w, b.reshape(1, POOL_W), scale.reshape(1, POOL_W))


def _pool_bwd(proj, dmixed, w, b, scale, name):
    S = proj.shape[0]
    T = _row_tile(S)
    n = S // T
    r = T // POOL_HALO
    cb = PPOOL // POOL_W
    mb = 1536 // POOL_W

    def body(u_ref, up_ref, dy_ref, dyn_ref, w_ref, b_ref, sc_ref, du_ref, gw_ref, gb_ref, gs_ref):
        i = pl.program_id(0)
        sc = sc_ref[...]
        dy = dy_ref[...]
        dy_ext = jnp.concatenate([dy, jnp.where(i < n - 1, dyn_ref[...], 0.0)], axis=0)
        dyl = dy_ext * sc
        dd = jnp.concatenate(
            [_mm(dyl[:, g * POOL_GD:(g + 1) * POOL_GD], w_ref[g], False, True) for g in range(POOL_G)], axis=1)
        t_ext = i * T + _rows(dd.shape)
        win = _by_group(dd.shape, [jnp.int32(v) for v in POOL_WINDOWS])
        cnt = jnp.minimum(t_ext + 1, win).astype(F32)
        e = jnp.where(t_ext < S, dd / cnt, 0.0)
        f2 = e + _up(e, 1)
        f4 = f2 + _up(f2, 2)
        f8 = f4 + _up(f4, 4)
        f16 = f8 + _up(f8, 8)
        du = (_by_group(dd.shape, [f2, f4, f8, f16]) - dd)[:T]
        du_ref[...] = du.astype(du_ref.dtype)

        prev = jnp.where(i > 0, up_ref[...], 0.0)
        d = _pool_d(prev, u_ref[...], i * T)
        ylin = _pool_lin(d, w_ref, b_ref[...])
        dyl_m = dy * sc

        @pl.when(i == 0)
        def _():
            gw_ref[...] = jnp.zeros_like(gw_ref)
            gb_ref[...] = jnp.zeros_like(gb_ref)
            gs_ref[...] = jnp.zeros_like(gs_ref)

        gs_ref[...] += jnp.sum(dy * ylin, axis=0, keepdims=True)
        gb_ref[...] += jnp.sum(dyl_m, axis=0, keepdims=True)
        for g in range(POOL_G):
            sl = slice(g * POOL_GD, (g + 1) * POOL_GD)
            gw_ref[g] += _mm(d[:, sl], dyl_m[:, sl], True, False)

    vec = pl.BlockSpec((1, POOL_W), lambda i: (0, 0))
    wsp = pl.BlockSpec((POOL_G, POOL_GD, POOL_GD), lambda i: (0, 0, 0))
    nh = S // POOL_HALO
    return pl.pallas_call(
        body, name=name, grid=(n,),
        in_specs=[pl.BlockSpec((T, POOL_W), lambda i: (i, cb)),
                  pl.BlockSpec((POOL_HALO, POOL_W), lambda i: (jnp.maximum(i * r - 1, 0), cb)),
                  pl.BlockSpec((T, POOL_W), lambda i: (i, mb)),
                  pl.BlockSpec((POOL_HALO, POOL_W), lambda i: (jnp.minimum((i + 1) * r, nh - 1), mb)),
                  wsp, vec, vec],
        out_specs=[pl.BlockSpec((T, POOL_W), lambda i: (i, 0)), wsp, vec, vec],
        out_shape=[jax.ShapeDtypeStruct((S, POOL_W), BF16), jax.ShapeDtypeStruct((POOL_G, POOL_GD, POOL_GD), F32),
                   jax.ShapeDtypeStruct((1, POOL_W), F32), jax.ShapeDtypeStruct((1, POOL_W), F32)],
        compiler_params=_cp("arbitrary"),
    )(proj, proj, dmixed, dmixed, w, b.reshape(1, POOL_W), scale.reshape(1, POOL_W))


def _conv_rows(ext, w_ref, taps):
    acc = w_ref[taps - 1:taps, :] * ext
    for k in range(1, taps):
        acc = acc + w_ref[taps - 1 - k:taps - k, :] * _down(ext, k)
    return acc


def _conv_t_rows(dc, w_ref, taps):
    acc = w_ref[taps - 1:taps, :] * dc
    for k in range(1, taps):
        acc = acc + w_ref[taps - 1 - k:taps - k, :] * _up(dc, k)
    return acc


def _conv_specs(T, S, ncb0, with_next):
    r = T // HALO
    nh = S // HALO
    main = pl.BlockSpec((T, CB), lambda j, i: (i, j + ncb0))
    prev = pl.BlockSpec((HALO, CB), lambda j, i: (jnp.maximum(i * r - 1, 0), j + ncb0))
    nxt = pl.BlockSpec((HALO, CB), lambda j, i: (jnp.minimum((i + 1) * r, nh - 1), j + ncb0))
    return (main, prev, nxt) if with_next else (main, prev)


def _gdn_conv_fwd(proj, w, name):
    S = proj.shape[0]
    T = _row_tile(S)
    taps = w.shape[0]
    ncb = 3 * GDN_W // CB

    def body(x_ref, xp_ref, w_ref, o_ref):
        i = pl.program_id(1)
        ext = jnp.concatenate([jnp.where(i > 0, xp_ref[...], 0.0), x_ref[...]], axis=0)
        o_ref[...] = jax.nn.silu(_conv_rows(ext, w_ref, taps)[HALO:])

    main, prev = _conv_specs(T, S, PQ // CB, False)
    return pl.pallas_call(
        body, name=name, grid=(ncb, S // T),
        in_specs=[main, prev, pl.BlockSpec((taps, CB), lambda j, i: (0, j))],
        out_specs=pl.BlockSpec((T, CB), lambda j, i: (i, j)), out_shape=jax.ShapeDtypeStruct((S, 3 * GDN_W), F32),
        compiler_params=_cp("parallel", "parallel"),
    )(proj, proj, w)


def _gdn_conv_bwd(proj, dact, w, name):
    S = proj.shape[0]
    T = _row_tile(S)
    n = S // T
    taps = w.shape[0]
    ncb = 3 * GDN_W // CB

    def body(x_ref, xp_ref, xn_ref, d_ref, dn_ref, w_ref, dx_ref, gw_ref):
        i = pl.program_id(1)
        last = i == n - 1
        ext = jnp.concatenate([jnp.where(i > 0, xp_ref[...], 0.0), x_ref[...], jnp.where(last, 0.0, xn_ref[...])], axis=0)
        c = _conv_rows(ext, w_ref, taps)[HALO:]
        d_ext = jnp.concatenate([d_ref[...], jnp.where(last, 0.0, dn_ref[...])], axis=0)
        _, vjp = jax.vjp(jax.nn.silu, c)
        dc = vjp(d_ext)[0]
        dx_ref[...] = _conv_t_rows(dc, w_ref, taps)[:T].astype(dx_ref.dtype)

        @pl.when(i == 0)
        def _():
            gw_ref[...] = jnp.zeros_like(gw_ref)

        dcm = dc[:T]
        for k in range(taps):
            gw_ref[taps - 1 - k:taps - k, :] += jnp.sum(dcm * _down(ext, k)[HALO:HALO + T], axis=0, keepdims=True)

    main, prev, nxt = _conv_specs(T, S, PQ // CB, True)
    dmain, _, dnxt = _conv_specs(T, S, 0, True)
    wsp = pl.BlockSpec((taps, CB), lambda j, i: (0, j))
    return pl.pallas_call(
        body, name=name, grid=(ncb, n), in_specs=[main, prev, nxt, dmain, dnxt, wsp],
        out_specs=[pl.BlockSpec((T, CB), lambda j, i: (i, j)), wsp],
        out_shape=[jax.ShapeDtypeStruct((S, 3 * GDN_W), BF16), jax.ShapeDtypeStruct((taps, 3 * GDN_W), F32)],
        compiler_params=_cp("parallel", "arbitrary"),
    )(proj, proj, proj, dact, dact, w)


def _ffn_act_fwd(up, w, name):
    S = up.shape[0]
    T = _row_tile(S)
    taps = w.shape[0]
    ncb = D_FF // CB

    def body(g_ref, gp_ref, v_ref, w_ref, o_ref):
        i = pl.program_id(1)
        ext = jnp.concatenate([jnp.where(i > 0, gp_ref[...], 0.0), g_ref[...]], axis=0)
        c = _conv_rows(ext, w_ref, taps)[HALO:]
        o_ref[...] = (jax.nn.gelu(c) * v_ref[...]).astype(o_ref.dtype)

    main, prev = _conv_specs(T, S, 0, False)
    val = pl.BlockSpec((T, CB), lambda j, i: (i, j + ncb))
    return pl.pallas_call(
        body, name=name, grid=(ncb, S // T),
        in_specs=[main, prev, val, pl.BlockSpec((taps, CB), lambda j, i: (0, j))],
        out_specs=pl.BlockSpec((T, CB), lambda j, i: (i, j)), out_shape=jax.ShapeDtypeStruct((S, D_FF), BF16),
        compiler_params=_cp("parallel", "parallel"),
    )(up, up, up, w)


def _ffn_act_bwd(up, dact, w, name):
    S = up.shape[0]
    T = _row_tile(S)
    n = S // T
    taps = w.shape[0]
    ncb = D_FF // CB

    def body(g_ref, gp_ref, gn_ref, v_ref, vn_ref, d_ref, dn_ref, w_ref, dup_ref, gw_ref):
        i = pl.program_id(1)
        last = i == n - 1
        ext = jnp.concatenate([jnp.where(i > 0, gp_ref[...], 0.0), g_ref[...], jnp.where(last, 0.0, gn_ref[...])], axis=0)
        c = _conv_rows(ext, w_ref, taps)[HALO:]
        v_ext = jnp.concatenate([v_ref[...], jnp.where(last, 0.0, vn_ref[...])], axis=0)
        d_ext = jnp.concatenate([d_ref[...], jnp.where(last, 0.0, dn_ref[...])], axis=0)
        gl, vjp = jax.vjp(jax.nn.gelu, c)
        dup_ref[1] = (d_ext * gl)[:T].astype(dup_ref.dtype)
        dc = vjp(d_ext * v_ext)[0]
        dup_ref[0] = _conv_t_rows(dc, w_ref, taps)[:T].astype(dup_ref.dtype)

        @pl.when(i == 0)
        def _():
            gw_ref[...] = jnp.zeros_like(gw_ref)

        dcm = dc[:T]
        for k in range(taps):
            gw_ref[taps - 1 - k:taps - k, :] += jnp.sum(dcm * _down(ext, k)[HALO:HALO + T], axis=0, keepdims=True)

    main, prev, nxt = _conv_specs(T, S, 0, True)
    vmain, _, vnxt = _conv_specs(T, S, ncb, True)
    wsp = pl.BlockSpec((taps, CB), lambda j, i: (0, j))
    osp = pl.BlockSpec((2, T, CB), lambda j, i: (0, i, j))
    return pl.pallas_call(
        body, name=name, grid=(ncb, n), in_specs=[main, prev, nxt, vmain, vnxt, main, nxt, wsp],
        out_specs=[osp, wsp],
        out_shape=[jax.ShapeDtypeStruct((2, S, D_FF), BF16), jax.ShapeDtypeStruct((taps, D_FF), F32)],
        compiler_params=_cp("parallel", "arbitrary"),
    )(up, up, up, up, up, dact, dact, w)


def _tri_masks():
    r = _rows((GDN_C, GDN_C))
    c = _lanes((GDN_C, GDN_C))
    return r >= c, r > c


def _each(fn, *cols):
    return tuple(fn(*args) for args in zip(*cols))


def _tri_inv_raw(lows):
    r = _rows(lows[0].shape)
    c = _lanes(lows[0].shape)
    eye = jnp.where(r == c, 1.0, 0.0)
    ps = _each(lambda low: eye - low, lows)
    lps = lows
    for _ in range(5):
        lps = _each(lambda lp: _mm(lp, lp, False, False, True), lps)
        ps = _each(lambda p, lp: p + _mm(p, lp, False, False, True), ps, lps)
    return ps


@jax.custom_vjp
def _tri_inv(lows):
    return _tri_inv_raw(lows)


def _tri_inv_fwd(lows):
    ts = _tri_inv_raw(lows)
    return ts, ts


def _tri_inv_bwd(ts, dts):
    inner = _each(lambda t, dt: _mm(t, dt, True, False, True), ts, dts)
    return (_each(lambda m, t: -_mm(m, t, False, True, True), inner, ts),)


_tri_inv.defvjp(_tri_inv_fwd, _tri_inv_bwd)


@jax.custom_vjp
def _tri_inv_given(lows, ts):
    return ts


def _tri_inv_given_fwd(lows, ts):
    return ts, ts


def _tri_inv_given_bwd(ts, dts):
    return _tri_inv_bwd(ts, dts)[0], _each(jnp.zeros_like, ts)


_tri_inv_given.defvjp(_tri_inv_given_fwd, _tri_inv_given_bwd)


def _gdn_glog(a_col, alog, dtb):
    return -jnp.exp(alog) * jax.nn.softplus(a_col + dtb)


def _decay_operand():
    r = _rows((GDN_C, 2 * GDN_C))
    c = _lanes((GDN_C, 2 * GDN_C))
    return jnp.where((c >= GDN_C) | (r > c), 1.0, 0.0)


def _gdn_decay(glog):
    causal, _ = _tri_masks()
    res = _mm01(jnp.where(causal, 1.0, 0.0), glog * _decay_operand())
    return res[:, GDN_C:GDN_C + 1], res[:, :GDN_C]


def _gdn_decay_bwd(dgcol, dd):
    r = _rows((GDN_C, GDN_C))
    c = _lanes((GDN_C, GDN_C))
    dres = jnp.concatenate([dd, jnp.where(c == 0, dgcol, 0.0)], axis=1)
    dx = _mm01(jnp.where(r <= c, 1.0, 0.0), dres)
    return jnp.sum(dx * _decay_operand(), axis=1, keepdims=True)


def _gdn_chunk(qa, ka, va, bt_col, gcol, dmat, t_saved=None):
    causal, strict = _tri_masks()
    qn = _each(lambda q: q * lax.rsqrt(jnp.sum(q * q, axis=-1, keepdims=True) + EPS) * (GDN_DH ** -0.5), qa)
    kn = _each(lambda k: k * lax.rsqrt(jnp.sum(k * k, axis=-1, keepdims=True) + EPS), ka)
    beta = _each(jax.nn.sigmoid, bt_col)
    eg = _each(jnp.exp, gcol)
    decay = _each(lambda d: jnp.where(causal, jnp.exp(d), 0.0), dmat)
    kk = _each(lambda k: _mm(k, k, False, True), kn)
    low = _each(lambda b, m, d: jnp.where(strict, b * m * d, 0.0), beta, kk, decay)
    t = _tri_inv(low) if t_saved is None else _tri_inv_given(low, t_saved)
    w = _each(lambda t_, k, b, e: _mm(t_, k * (b * e), False, False, True), t, kn, beta, eg)
    u = _each(lambda t_, v, b: _mm(t_, v * b, False, False, True), t, va, beta)
    attn = _each(lambda q, k, d: _mm(q, k, False, True) * d, qn, kn, decay)
    last = _rows(gcol[0].shape) == GDN_C - 1
    g_last = _each(lambda g: jnp.sum(jnp.where(last, g, 0.0), axis=0, keepdims=True), gcol)
    qd = _each(lambda q, e: q * e, qn, eg)
    kd = _each(lambda k, gl, g: k * jnp.exp(gl - g), kn, g_last, gcol)
    return (w, u, qd, kd, attn), t


def _gdn_step(state, w, u, qd, kd, attn, egl):
    v_new = _each(lambda u_, w_, s: u_ - _mm(w_, s), u, w, state)
    o_state = _each(_mm, qd, state)
    o = _each(lambda os, a, v: os + _mm(a, v), o_state, attn, v_new)
    new = _each(lambda s, e, k, v: s * e + _mm(k, v, True, False), state, egl, kd, v_new)
    return o, new


def _heads(ref, base=0, width=GDN_DH):
    return tuple(ref[:, (base + h) * GDN_DH:(base + h) * GDN_DH + width] for h in range(GDN_H))


def _cols(a, base):
    return tuple(a[:, base + h:base + h + 1] for h in range(GDN_H))


def _gated_norm(o, z, nw):
    return o * lax.rsqrt(jnp.mean(o * o, axis=-1, keepdims=True) + EPS) * nw * jax.nn.silu(z)


def _hsl(h):
    return slice(h * GDN_DH, (h + 1) * GDN_DH)


def _pad_lanes(a, width=GDN_DH):
    return jnp.concatenate([a, jnp.zeros((a.shape[0], width - a.shape[1]), a.dtype)], axis=1)


def _gdn_prep(qkv, proj, alog, dtb, name):
    S = qkv.shape[0]
    N = S // GDN_C

    def body(qkv_ref, ab_ref, al_ref, dt_ref, w_ref, u_ref, qd_ref, kd_ref, at_ref, ti_ref, gc_ref):
        ab = ab_ref[...]
        glog = _each(_gdn_glog, _cols(ab, 0), _cols(al_ref[...], 0), _cols(dt_ref[...], 0))
        dec = _each(_gdn_decay, glog)
        gcol, dmat = _each(lambda d: d[0], dec), _each(lambda d: d[1], dec)
        (w, u, qd, kd, attn), tinv = _gdn_chunk(_heads(qkv_ref), _heads(qkv_ref, GDN_H), _heads(qkv_ref, 2 * GDN_H),
                                                _cols(ab, GDN_H), gcol, dmat)
        gc = jnp.zeros((GDN_C, 128), F32)
        for h in range(GDN_H):
            w_ref[:, _hsl(h)] = w[h]
            u_ref[:, _hsl(h)] = u[h]
            qd_ref[:, _hsl(h)] = qd[h]
            kd_ref[:, _hsl(h)] = kd[h]
            at_ref[:, _hsl(h)] = _pad_lanes(attn[h])
            ti_ref[:, _hsl(h)] = _pad_lanes(tinv[h])
            gc = jnp.where(_lanes(gc.shape) == h, gcol[h], gc)
        gc_ref[...] = gc

    vec = pl.BlockSpec((1, 128), lambda i: (0, 0))
    hsp = pl.BlockSpec((GDN_C, GDN_W), lambda i: (i, 0))
    hshape = jax.ShapeDtypeStruct((S, GDN_W), F32)
    return pl.pallas_call(
        body, name=name, grid=(N,),
        in_specs=[pl.BlockSpec((GDN_C, 3 * GDN_W), lambda i: (i, 0)), pl.BlockSpec((GDN_C, 128), lambda i: (i, PAB // 128)), vec, vec],
        out_specs=[hsp] * 6 + [pl.BlockSpec((GDN_C, 128), lambda i: (i, 0))],
        out_shape=[hshape] * 6 + [jax.ShapeDtypeStruct((S, 128), F32)],
        compiler_params=_cp("parallel"),
    )(qkv, proj, alog, dtb)


def _gdn_scan(w, u, qd, kd, attn, gc, name):
    S = w.shape[0]
    N = S // GDN_C

    def body(w_ref, u_ref, qd_ref, kd_ref, at_ref, gc_ref, o_ref, st_ref, s_ref):
        @pl.when(pl.program_id(0) == 0)
        def _():
            s_ref[...] = jnp.zeros_like(s_ref)

        state = tuple(s_ref[_hsl(h), :] for h in range(GDN_H))
        egl = _each(jnp.exp, _cols(gc_ref[GDN_C - 1:GDN_C, :], 0))
        o, new = _gdn_step(state, _heads(w_ref), _heads(u_ref), _heads(qd_ref), _heads(kd_ref),
                           _heads(at_ref, width=GDN_C), egl)
        for h in range(GDN_H):
            st_ref[_hsl(h), :] = state[h]
            o_ref[:, _hsl(h)] = o[h]
            s_ref[_hsl(h), :] = new[h]

    hsp = pl.BlockSpec((GDN_C, GDN_W), lambda i: (i, 0))
    return pl.pallas_call(
        body, name=name, grid=(N,),
        in_specs=[hsp] * 5 + [pl.BlockSpec((GDN_C, 128), lambda i: (i, 0))],
        out_specs=[hsp, pl.BlockSpec((None, GDN_W, GDN_DH), lambda i: (i, 0, 0))],
        out_shape=[jax.ShapeDtypeStruct((S, GDN_W), F32), jax.ShapeDtypeStruct((N, GDN_W, GDN_DH), F32)],
        scratch_shapes=[pltpu.VMEM((GDN_W, GDN_DH), F32)],
        compiler_params=_cp("arbitrary"),
    )(w, u, qd, kd, attn, gc)


def _gdn_scan_bwd(w, u, qd, kd, attn, gc, states, o, proj, dmixed, nw, name):
    S = w.shape[0]
    N = S // GDN_C

    def body(w_ref, u_ref, qd_ref, kd_ref, at_ref, gc_ref, st_ref, o_ref, z_ref, dm_ref, nw_ref,
             dw_ref, du_ref, dqd_ref, dkd_ref, dat_ref, dgl_ref, dz_ref, gnw_ref, ds_ref):
        @pl.when(pl.program_id(0) == 0)
        def _():
            ds_ref[...] = jnp.zeros_like(ds_ref)
            gnw_ref[...] = jnp.zeros_like(gnw_ref)

        nw = nw_ref[...]
        _, vjp_n = jax.vjp(lambda o, z, w_: _each(lambda a, b: _gated_norm(a, b, w_), o, z), _heads(o_ref), _heads(z_ref), nw)
        do, dz, dnw = vjp_n(_heads(dm_ref))
        state = tuple(st_ref[_hsl(h), :] for h in range(GDN_H))
        egl = _each(jnp.exp, _cols(gc_ref[GDN_C - 1:GDN_C, :], 0))
        _, vjp_s = jax.vjp(_gdn_step, state, _heads(w_ref), _heads(u_ref), _heads(qd_ref), _heads(kd_ref),
                           _heads(at_ref, width=GDN_C), egl)
        ds, dw, du, dqd, dkd, dat, degl = vjp_s((do, tuple(ds_ref[_hsl(h), :] for h in range(GDN_H))))
        dgl = jnp.zeros((8, 128), F32)
        for h in range(GDN_H):
            dz_ref[:, _hsl(h)] = dz[h].astype(dz_ref.dtype)
            ds_ref[_hsl(h), :] = ds[h]
            dw_ref[:, _hsl(h)] = dw[h]
            du_ref[:, _hsl(h)] = du[h]
            dqd_ref[:, _hsl(h)] = dqd[h]
            dkd_ref[:, _hsl(h)] = dkd[h]
            dat_ref[:, _hsl(h)] = _pad_lanes(dat[h])
            dgl = jnp.where(_lanes(dgl.shape) == h, degl[h] * egl[h], dgl)
        dgl_ref[...] = dgl
        gnw_ref[...] += dnw

    rev = lambda i: (N - 1 - i, 0)
    hsp = pl.BlockSpec((GDN_C, GDN_W), rev)
    gsp = pl.BlockSpec((GDN_C, 128), rev)
    vec = pl.BlockSpec((1, GDN_DH), lambda i: (0, 0))
    hshape = jax.ShapeDtypeStruct((S, GDN_W), F32)
    return pl.pallas_call(
        body, name=name, grid=(N,),
        in_specs=[hsp] * 5 + [gsp, pl.BlockSpec((None, GDN_W, GDN_DH), lambda i: (N - 1 - i, 0, 0)), hsp,
                              pl.BlockSpec((GDN_C, GDN_W), lambda i: (N - 1 - i, PZ // GDN_W)), hsp, vec],
        out_specs=[hsp] * 5 + [pl.BlockSpec((8, 128), rev), hsp, vec],
        out_shape=[hshape] * 5 + [jax.ShapeDtypeStruct((N * 8, 128), F32), jax.ShapeDtypeStruct((S, GDN_W), BF16),
                                  jax.ShapeDtypeStruct((1, GDN_DH), F32)],
        scratch_shapes=[pltpu.VMEM((GDN_W, GDN_DH), F32)],
        compiler_params=_cp("arbitrary"),
    )(w, u, qd, kd, attn, gc, states, o, proj, dmixed, nw)


def _gdn_prep_bwd(qkv, proj, alog, dtb, tinv, dw, du, dqd, dkd, dat, dgl, name):
    S = qkv.shape[0]
    N = S // GDN_C

    def body(qkv_ref, ab_ref, al_ref, dt_ref, ti_ref, dw_ref, du_ref, dqd_ref, dkd_ref, dat_ref, dgl_ref,
             dqkv_ref, dab_ref, gal_ref, gdt_ref):
        @pl.when(pl.program_id(0) == 0)
        def _():
            gal_ref[...] = jnp.zeros_like(gal_ref)
            gdt_ref[...] = jnp.zeros_like(gdt_ref)

        ab = ab_ref[...]
        glog, vjp_g = jax.vjp(lambda a, al, dt: _each(_gdn_glog, a, al, dt), _cols(ab, 0), _cols(al_ref[...], 0),
                              _cols(dt_ref[...], 0))
        dec = _each(_gdn_decay, glog)
        gcol, dmat = _each(lambda d: d[0], dec), _each(lambda d: d[1], dec)
        _, vjp_c, _ = jax.vjp(functools.partial(_gdn_chunk, t_saved=_heads(ti_ref, width=GDN_C)), _heads(qkv_ref),
                              _heads(qkv_ref, GDN_H), _heads(qkv_ref, 2 * GDN_H), _cols(ab, GDN_H), gcol, dmat, has_aux=True)
        dqa, dka, dva, dbt, dgcol, dd = vjp_c((_heads(dw_ref), _heads(du_ref), _heads(dqd_ref), _heads(dkd_ref),
                                               _heads(dat_ref, width=GDN_C)))
        last = _rows(dgcol[0].shape) == GDN_C - 1
        dgcol = _each(lambda d, g: d + jnp.where(last, g, 0.0), dgcol, _cols(dgl_ref[0:1, :], 0))
        da_col, dal, ddt = vjp_g(_each(_gdn_decay_bwd, dgcol, dd))
        dab = jnp.zeros((GDN_C, 128), F32)
        gal = jnp.zeros((1, 128), F32)
        gdt = jnp.zeros((1, 128), F32)
        for h in range(GDN_H):
            dqkv_ref[:, _hsl(h)] = dqa[h]
            dqkv_ref[:, _hsl(GDN_H + h)] = dka[h]
            dqkv_ref[:, _hsl(2 * GDN_H + h)] = dva[h]
            ln = _lanes(dab.shape)
            dab = dab + jnp.where(ln == h, da_col[h], 0.0) + jnp.where(ln == GDN_H + h, dbt[h], 0.0)
            l1 = _lanes(gal.shape)
            gal = gal + jnp.where(l1 == h, dal[h], 0.0)
            gdt = gdt + jnp.where(l1 == h, ddt[h], 0.0)
        dab_ref[...] = dab.astype(dab_ref.dtype)
        gal_ref[...] += gal
        gdt_ref[...] += gdt

    vec = pl.BlockSpec((1, 128), lambda i: (0, 0))
    hsp = pl.BlockSpec((GDN_C, GDN_W), lambda i: (i, 0))
    qsp = pl.BlockSpec((GDN_C, 3 * GDN_W), lambda i: (i, 0))
    return pl.pallas_call(
        body, name=name, grid=(N,),
        in_specs=[qsp, pl.BlockSpec((GDN_C, 128), lambda i: (i, PAB // 128)), vec, vec] + [hsp] * 6
        + [pl.BlockSpec((8, 128), lambda i: (i, 0))],
        out_specs=[qsp, pl.BlockSpec((GDN_C, 128), lambda i: (i, 0)), vec, vec],
        out_shape=[jax.ShapeDtypeStruct((S, 3 * GDN_W), F32), jax.ShapeDtypeStruct((S, 128), BF16),
                   jax.ShapeDtypeStruct((1, 128), F32), jax.ShapeDtypeStruct((1, 128), F32)],
        compiler_params=_cp("arbitrary"),
    )(qkv, proj, alog, dtb, tinv, dw, du, dqd, dkd, dat, dgl)


@jax.custom_vjp
def _expm1(x):
    u = jnp.exp(x)
    lu = jnp.log(u)
    small = (u - 1.0) * x / jnp.where(u == 1.0, 1.0, lu)
    small = jnp.where(u == 1.0, x, small)
    return jnp.where(jnp.abs(x) < 0.5, small, u - 1.0)


def _expm1_fwd(x):
    return _expm1(x), jnp.exp(x)


def _expm1_bwd(ex, g):
    return (g * ex,)


_expm1.defvjp(_expm1_fwd, _expm1_bwd)


def _lru_gates(xc, wa, ba, wx, bx, lam, first):
    r = jax.nn.sigmoid(_mm(xc, wa) + ba)
    i = jax.nn.sigmoid(_mm(xc, wx) + bx)
    log_a = -LRU_C * r * jax.nn.softplus(-lam)
    mult = jnp.sqrt(-_expm1(2.0 * log_a))
    mult = jnp.where(first, 1.0, mult)
    return jnp.exp(log_a), mult * i * xc


def _scan_fwd(a, b):
    T = a.shape[0]
    rows = _rows(a.shape)
    s = 1
    while s < T:
        ok = rows >= s
        b = a * jnp.where(ok, _down(b, s), 0.0) + b
        a = a * jnp.where(ok, _down(a, s), 1.0)
        s *= 2
    return a, b


def _scan_rev(a, b):
    T = a.shape[0]
    rows = _rows(a.shape)
    s = 1
    while s < T:
        ok = rows + s < T
        b = a * jnp.where(ok, _up(b, s), 0.0) + b
        a = a * jnp.where(ok, _up(a, s), 1.0)
        s *= 2
    return b


def _bsl(j):
    return slice(j * LRU_BD, (j + 1) * LRU_BD)


def _lru_tile(S):
    return _row_tile(S, 256)


def _lru_fwd(proj, conv_w, conv_b, wa, ba, wx, bx, lam, name):
    S = proj.shape[0]
    T = _lru_tile(S)
    taps = conv_w.shape[0]
    r = T // HALO

    def body(x_ref, xp_ref, cw_ref, cb_ref, wa_ref, ba_ref, wx_ref, bx_ref, lam_ref, h_ref, carry_ref):
        i = pl.program_id(0)

        @pl.when(i == 0)
        def _():
            carry_ref[...] = jnp.zeros_like(carry_ref)

        ext = jnp.concatenate([jnp.where(i > 0, xp_ref[...], 0.0), x_ref[...]], axis=0)
        xc = _conv_rows(ext, cw_ref, taps)[HALO:] + cb_ref[...]
        first = (i * T + _rows((T, LRU_BD))) == 0
        for j in range(LRU_NB):
            a, b = _lru_gates(xc[:, _bsl(j)], wa_ref[j], ba_ref[:, _bsl(j)], wx_ref[j], bx_ref[:, _bsl(j)],
                              lam_ref[:, _bsl(j)], first=first)
            pa, hb = _scan_fwd(a, b)
            h_ref[:, _bsl(j)] = pa * carry_ref[0:1, _bsl(j)] + hb
            carry_ref[0:1, _bsl(j)] = h_ref[T - 1:T, _bsl(j)]

    vec = pl.BlockSpec((1, LRU_W), lambda i: (0, 0))
    wsp = pl.BlockSpec((LRU_NB, LRU_BD, LRU_BD), lambda i: (0, 0, 0))
    return pl.pallas_call(
        body, name=name, grid=(S // T,),
        in_specs=[pl.BlockSpec((T, LRU_W), lambda i: (i, PXR // LRU_W)),
                  pl.BlockSpec((HALO, LRU_W), lambda i: (jnp.maximum(i * r - 1, 0), PXR // LRU_W)),
                  pl.BlockSpec((taps, LRU_W), lambda i: (0, 0)), vec, wsp, vec, wsp, vec, vec],
        out_specs=pl.BlockSpec((T, LRU_W), lambda i: (i, 0)), out_shape=jax.ShapeDtypeStruct((S, LRU_W), F32),
        scratch_shapes=[pltpu.VMEM((8, LRU_W), F32)],
        compiler_params=_cp("arbitrary"),
    )(proj, proj, conv_w, conv_b.reshape(1, LRU_W), wa, ba.reshape(1, LRU_W), wx, bx.reshape(1, LRU_W), lam.reshape(1, LRU_W))


def _lru_bwd(proj, hl, dmixed, conv_w, conv_b, wa, ba, wx, bx, lam, name):
    S = proj.shape[0]
    T = _lru_tile(S)
    n = S // T
    taps = conv_w.shape[0]
    r = T // HALO
    mb = 768 // LRU_W

    def body(x_ref, xp_ref, g_ref, h_ref, hp_ref, dy_ref, cw_ref, cb_ref, wa_ref, ba_ref, wx_ref, bx_ref, lam_ref,
             dx_ref, dg_ref, gcw_ref, gcb_ref, gwa_ref, gba_ref, gwx_ref, gbx_ref, glam_ref, carry_ref, dxc_ref, nxt_ref):
        s = pl.program_id(0)
        i = n - 1 - s

        @pl.when(s == 0)
        def _():
            carry_ref[...] = jnp.zeros_like(carry_ref)
            nxt_ref[...] = jnp.zeros_like(nxt_ref)
            for ref in (gcw_ref, gcb_ref, gwa_ref, gba_ref, gwx_ref, gbx_ref, glam_ref):
                ref[...] = jnp.zeros_like(ref)

        ext = jnp.concatenate([jnp.where(i > 0, xp_ref[...], 0.0), x_ref[...]], axis=0)
        xc = _conv_rows(ext, cw_ref, taps)[HALO:] + cb_ref[...]
        rows = _rows((T, LRU_BD))
        first = (i * T + rows) == 0
        h_before = jnp.where(i > 0, hp_ref[HALO - 1:HALO, :], 0.0)
        for j in range(LRU_NB):
            sl = _bsl(j)
            (a, _), vjp_g = jax.vjp(functools.partial(_lru_gates, first=first), xc[:, sl], wa_ref[j], ba_ref[:, sl],
                                    wx_ref[j], bx_ref[:, sl], lam_ref[:, sl])
            gelu_g, vjp_a = jax.vjp(jax.nn.gelu, g_ref[:, sl])
            h = h_ref[:, sl]
            dy = dy_ref[:, sl]
            dg_ref[:, sl] = vjp_a(dy * h)[0].astype(dg_ref.dtype)
            b_rev = dy * gelu_g + jnp.where(rows == T - 1, carry_ref[0:1, sl], 0.0)
            a_rev = jnp.where(rows == T - 1, 0.0, _up(a, 1))
            dh = _scan_rev(a_rev, b_rev)
            carry_ref[:, sl] = (a * dh)[:HALO]
            h_prev = jnp.where(rows == 0, h_before[:, sl], _down(h, 1))
            dxc, dwa, dba, dwx, dbx, dlam = vjp_g((dh * h_prev, dh))
            dxc_ref[:, sl] = dxc
            gwa_ref[j] += dwa
            gwx_ref[j] += dwx
            gba_ref[:, sl] += dba
            gbx_ref[:, sl] += dbx
            glam_ref[:, sl] += dlam
        dxc = dxc_ref[...]
        d_ext = jnp.concatenate([dxc, nxt_ref[...]], axis=0)
        dx_ref[...] = _conv_t_rows(d_ext, cw_ref, taps)[:T].astype(dx_ref.dtype)
        nxt_ref[...] = dxc[:HALO]
        gcb_ref[...] += jnp.sum(dxc, axis=0, keepdims=True)
        for k in range(taps):
            gcw_ref[taps - 1 - k:taps - k, :] += jnp.sum(dxc * _down(ext, k)[HALO:], axis=0, keepdims=True)

    vec = pl.BlockSpec((1, LRU_W), lambda s: (0, 0))
    wsp = pl.BlockSpec((LRU_NB, LRU_BD, LRU_BD), lambda s: (0, 0, 0))
    cwsp = pl.BlockSpec((taps, LRU_W), lambda s: (0, 0))

    def main(cb):
        return pl.BlockSpec((T, LRU_W), lambda s: (n - 1 - s, cb))

    def prev(cb):
        return pl.BlockSpec((HALO, LRU_W), lambda s: (jnp.maximum((n - 1 - s) * r - 1, 0), cb))

    vshape = jax.ShapeDtypeStruct((1, LRU_W), F32)
    wshape = jax.ShapeDtypeStruct((LRU_NB, LRU_BD, LRU_BD), F32)
    return pl.pallas_call(
        body, name=name, grid=(n,),
        in_specs=[main(PXR // LRU_W), prev(PXR // LRU_W), main(PGR // LRU_W), main(0), prev(0), main(mb),
                  cwsp, vec, wsp, vec, wsp, vec, vec],
        out_specs=[main(0), main(0), cwsp, vec, wsp, vec, wsp, vec, vec],
        out_shape=[jax.ShapeDtypeStruct((S, LRU_W), BF16), jax.ShapeDtypeStruct((S, LRU_W), BF16),
                   jax.ShapeDtypeStruct((taps, LRU_W), F32), vshape, wshape, vshape, wshape, vshape, vshape],
        scratch_shapes=[pltpu.VMEM((8, LRU_W), F32), pltpu.VMEM((T, LRU_W), F32), pltpu.VMEM((HALO, LRU_W), F32)],
        compiler_params=_cp("arbitrary"),
    )(proj, proj, proj, hl, hl, dmixed, conv_w, conv_b.reshape(1, LRU_W), wa, ba.reshape(1, LRU_W), wx,
      bx.reshape(1, LRU_W), lam.reshape(1, LRU_W))


def _mix_out(o, proj, hl, y_pool, nw, name):
    S = o.shape[0]
    T = _row_tile(S)

    def body(o_ref, z_ref, h_ref, g_ref, p_ref, nw_ref, m_ref):
        for h in range(GDN_H):
            m_ref[:, _hsl(h)] = _gated_norm(o_ref[:, _hsl(h)], z_ref[:, _hsl(h)], nw_ref[...]).astype(m_ref.dtype)
        m_ref[:, GDN_W:GDN_W + LRU_W] = (h_ref[...] * jax.nn.gelu(g_ref[...])).astype(m_ref.dtype)
        m_ref[:, GDN_W + LRU_W:] = p_ref[...].astype(m_ref.dtype)

    row = pl.BlockSpec((T, GDN_W), lambda i: (i, 0))
    return pl.pallas_call(
        body, name=name, grid=(S // T,),
        in_specs=[row, pl.BlockSpec((T, GDN_W), lambda i: (i, PZ // GDN_W)), row,
                  pl.BlockSpec((T, LRU_W), lambda i: (i, PGR // LRU_W)), pl.BlockSpec((T, POOL_W), lambda i: (i, 0)),
                  pl.BlockSpec((1, GDN_DH), lambda i: (0, 0))],
        out_specs=pl.BlockSpec((T, D_MODEL), lambda i: (i, 0)), out_shape=jax.ShapeDtypeStruct((S, D_MODEL), BF16),
        compiler_params=_cp("parallel"),
    )(o, proj, hl, proj, y_pool, nw)


def _as2d(a):
    return a.reshape(-1, a.shape[-1])


def _ew_rows(rows, cols):
    t = rows
    while t * cols * 4 > (2 << 20) and t % 16 == 0:
        t //= 2
    return t


def _rs_rows(rows, cols):
    t = rows
    while t * cols * 4 > (2 << 20) and t % 32 == 0:
        t //= 2
    return t


def _rs_add(src, recv, src_index, grid_lead, out_dtype, name, into=None):
    rows, cols = recv.shape[-2:]
    t = _rs_rows(rows, cols)
    nl = len(grid_lead)
    lead_none = (None,) * (src.ndim - 2)

    def body(s_ref, r_ref, *rest):
        o_ref = rest[-1]
        o_ref[...] = (s_ref[...].astype(F32) + r_ref[...].astype(F32)).astype(o_ref.dtype)

    def src_map(*a):
        return (*src_index(*a[:nl], _place()), a[nl], 0)

    def own_map(*a):
        return (*a[:nl], a[nl], 0)

    own = pl.BlockSpec(((None,) * nl) + (t, cols), own_map)
    in_specs = [pl.BlockSpec((*lead_none, t, cols), src_map), own]
    args = [src, recv]
    if into is None:
        out_spec, out_shape, aliases = own, jax.ShapeDtypeStruct(recv.shape, out_dtype), {}
    else:
        buf, layer = into
        shape = buf.shape
        args.append(buf.reshape((2, 2) + recv.shape))
        in_specs.append(pl.BlockSpec(memory_space=pl.ANY))
        out_spec = pl.BlockSpec(((None,) * (nl + 2)) + (t, cols), lambda *a: (layer, lax.axis_index("c"), *a[:nl], a[nl], 0))
        out_shape, aliases = jax.ShapeDtypeStruct((2, 2) + recv.shape, out_dtype), {2: 0}
    out = pl.pallas_call(
        body, name=name, grid=(*grid_lead, rows // t), in_specs=in_specs, out_specs=out_spec,
        out_shape=out_shape, input_output_aliases=aliases,
        compiler_params=pltpu.CompilerParams(dimension_semantics=("parallel",) * (nl + 1), vmem_limit_bytes=VMEM_LIMIT),
    )(*args)
    return out if into is None else out.reshape(shape)


def _adamw(w, g, m, v, name):
    shape = w.shape
    w2, g2, m2, v2 = _as2d(w), _as2d(g), _as2d(m), _as2d(v)
    rows, cols = w2.shape
    t = _ew_rows(rows, cols)

    def body(w_ref, g_ref, m_ref, v_ref, d_ref, nm_ref, nv_ref):
        gr = g_ref[...]
        nm = ADAM_B1 * m_ref[...] + (1.0 - ADAM_B1) * gr
        nv = ADAM_B2 * v_ref[...] + (1.0 - ADAM_B2) * (gr * gr)
        m_hat = nm / (1.0 - ADAM_B1 ** ADAM_STEP)
        v_hat = nv / (1.0 - ADAM_B2 ** ADAM_STEP)
        d_ref[...] = -ADAM_LR * (m_hat / (jnp.sqrt(v_hat) + ADAM_EPS) + ADAM_WD * w_ref[...])
        nm_ref[...] = nm
        nv_ref[...] = nv

    sp = pl.BlockSpec((t, cols), lambda i: (i, 0))
    sh = jax.ShapeDtypeStruct((rows, cols), F32)
    d, nm, nv = pl.pallas_call(body, name=name, grid=(rows // t,), in_specs=[sp] * 4, out_specs=[sp] * 3,
                               out_shape=[sh] * 3, compiler_params=_cp("parallel"))(w2, g2, m2, v2)
    return d.reshape(shape), nm.reshape(shape), nv.reshape(shape)


def _place():
    return lax.axis_index("x"), lax.axis_index("y"), lax.axis_index("c")


def _gather_weights(arrs, name):
    n = len(arrs)

    def body(*refs):
        outs = refs[n:2 * n]
        send, recv = refs[2 * n:]
        x, y, c = _place()
        s_me, s_x, s_y, s_d = 2 * x + y, 2 * (1 - x) + y, 2 * x + (1 - y), 2 * (1 - x) + (1 - y)
        xpeer, ypeer, sib = (1 - x, y, c), (x, 1 - y, c), (x, y, 1 - c)

        def rc(k, t, src, dst, to):
            return pltpu.make_async_remote_copy(src_ref=src, dst_ref=dst, send_sem=send.at[k, t], recv_sem=recv.at[k, t],
                                                device_id=to, device_id_type=MESH)

        def piece(k, s, top):
            rq = outs[k].shape[2] // 2
            return outs[k].at[s, c, pl.ds(0 if top else rq, rq)]

        sent = []

        def start(k, t, ref, to):
            cp = rc(k, t, ref, ref, to)
            cp.start()
            sent.append(cp)

        for k in range(n):
            start(k, 0, outs[k].at[s_me, c], xpeer)
            start(k, 1, outs[k].at[s_me, c], ypeer)
        for k in range(n):
            got = outs[k].at[s_x, c]
            rc(k, 0, got, got, xpeer).wait_recv()
            start(k, 2, piece(k, s_x, True), ypeer)
            start(k, 3, got, sib)
        for k in range(n):
            got = outs[k].at[s_y, c]
            rc(k, 1, got, got, ypeer).wait_recv()
            start(k, 6, piece(k, s_y, False), xpeer)
            start(k, 4, got, sib)
        for k in range(n):
            top, bottom = piece(k, s_d, True), piece(k, s_d, False)
            rc(k, 2, top, top, ypeer).wait_recv()
            rc(k, 6, bottom, bottom, xpeer).wait_recv()
            start(k, 5, outs[k].at[s_d, c], sib)
        for k in range(n):
            for t, s in ((3, s_x), (4, s_y), (5, s_d)):
                got = outs[k].at[s, 1 - c]
                rc(k, t, got, got, sib).wait_recv()
        for cp in sent:
            cp.wait_send()

    return pl.pallas_call(
        body, name=name, in_specs=[HBM] * n, out_specs=[HBM] * n,
        out_shape=[jax.ShapeDtypeStruct(a.shape, a.dtype) for a in arrs],
        input_output_aliases={k: k for k in range(n)},
        scratch_shapes=[pltpu.SemaphoreType.DMA((n, 7)), pltpu.SemaphoreType.DMA((n, 7))],
    )(*arrs)


def _exchange(arrs, axis, name, half=0):
    n = len(arrs)

    def body(*refs):
        srcs, outs = refs[:n], refs[n:2 * n]
        send, recv = refs[2 * n:]
        x, y, c = _place()
        p = {"x": x, "y": y, "c": c}[axis]
        peer = {"x": (1 - x, y, c), "y": (x, 1 - y, c), "c": (x, y, 1 - c)}[axis]
        cps = []
        for k in range(n):
            src = (srcs[k], srcs[k].at[1 - p], srcs[k].at[:, 1 - p])[half]
            cp = pltpu.make_async_remote_copy(src_ref=src, dst_ref=outs[k], send_sem=send.at[k], recv_sem=recv.at[k],
                                              device_id=peer, device_id_type=MESH)
            cp.start()
            cps.append(cp)
        for cp in cps:
            cp.wait()

    def out_shape(a):
        return (a.shape, a.shape[1:], a.shape[:1] + a.shape[2:])[half]

    return pl.pallas_call(
        body, name=name, in_specs=[HBM] * n, out_specs=[HBM] * n,
        out_shape=[jax.ShapeDtypeStruct(out_shape(a), a.dtype) for a in arrs],
        scratch_shapes=[pltpu.SemaphoreType.DMA((n,)), pltpu.SemaphoreType.DMA((n,))],
    )(*arrs)


def _share_halves(arrs, name):
    n = len(arrs)

    def body(*refs):
        outs = refs[n:2 * n]
        send, recv = refs[2 * n:]
        x, y, c = _place()
        cps = []
        for k in range(n):
            mine = outs[k].at[:, c]
            cp = pltpu.make_async_remote_copy(src_ref=mine, dst_ref=mine, send_sem=send.at[k], recv_sem=recv.at[k],
                                              device_id=(x, y, 1 - c), device_id_type=MESH)
            cp.start()
            cps.append(cp)
        for k in range(n):
            got = outs[k].at[:, 1 - c]
            pltpu.make_async_remote_copy(src_ref=got, dst_ref=got, send_sem=send.at[k], recv_sem=recv.at[k],
                                         device_id=(x, y, 1 - c), device_id_type=MESH).wait_recv()
        for cp in cps:
            cp.wait_send()

    return pl.pallas_call(
        body, name=name, in_specs=[HBM] * n, out_specs=[HBM] * n,
        out_shape=[jax.ShapeDtypeStruct(a.shape, a.dtype) for a in arrs],
        input_output_aliases={k: k for k in range(n)},
        scratch_shapes=[pltpu.SemaphoreType.DMA((n,)), pltpu.SemaphoreType.DMA((n,))],
    )(*arrs)


def _rs_exchange(arrs, phase, name):
    n = len(arrs)

    def body(*refs):
        srcs, outs = refs[:n], refs[n:2 * n]
        send, recv = refs[2 * n:]
        x, y, c = _place()
        xpeer, ypeer = (1 - x, y, c), (x, 1 - y, c)
        cps = []
        for k in range(n):
            if phase == 2:
                parts = ((srcs[k].at[1 - x, :, 0], xpeer), (srcs[k].at[:, 1 - y, 1], ypeer))
            else:
                parts = ((srcs[k].at[0, 1 - y], ypeer), (srcs[k].at[1, 1 - x], xpeer))
            for h, (src, to) in enumerate(parts):
                cp = pltpu.make_async_remote_copy(src_ref=src, dst_ref=outs[k].at[h], send_sem=send.at[k, h],
                                                  recv_sem=recv.at[k, h], device_id=to, device_id_type=MESH)
                cp.start()
                cps.append(cp)
        for cp in cps:
            cp.wait()

    def out_shape(a):
        return (2, 2) + a.shape[3:] if phase == 2 else (2,) + a.shape[2:]

    return pl.pallas_call(
        body, name=name, in_specs=[HBM] * n, out_specs=[HBM] * n,
        out_shape=[jax.ShapeDtypeStruct(out_shape(a), a.dtype) for a in arrs],
        scratch_shapes=[pltpu.SemaphoreType.DMA((n, 2)), pltpu.SemaphoreType.DMA((n, 2))],
    )(*arrs)


_REL = tuple((dx, dy, dc) for dx in (0, 1) for dy in (0, 1) for dc in (0, 1))[1:]
SEM = pl.BlockSpec(memory_space=pltpu.SEMAPHORE)
DATAFLOW = pltpu.SideEffectType.DATAFLOW_SIDE_EFFECTING


def _flip(v, d):
    return 1 - v if d else v


def _rs_direct_copies(srcs, land, send, recv):
    x, y, c = _place()
    cps = []
    for k in range(len(srcs)):
        for r, (dx, dy, dc) in enumerate(_REL):
            px, py, pc = _flip(x, dx), _flip(y, dy), _flip(c, dc)
            cps.append(pltpu.make_async_remote_copy(
                src_ref=srcs[k].at[2 * px + py, pc], dst_ref=land[k].at[r], send_sem=send.at[k * len(_REL) + r],
                recv_sem=recv.at[k * len(_REL) + r], device_id=(px, py, pc), device_id_type=MESH))
    return cps


def _rs_direct_start(grads, thru, name):
    n = len(grads)
    lands = [pltpu.with_memory_space_constraint(lax.empty((len(_REL),) + g.shape[2:], g.dtype), pltpu.HBM) for g in grads]

    def body(*refs):
        for cp in _rs_direct_copies(refs[:n], refs[n + 1:2 * n + 1], refs[2 * n + 1], refs[2 * n + 2]):
            cp.start()

    sems = pltpu.SemaphoreType.DMA((n * len(_REL),))
    keep = [pltpu.HBM(a.shape, a.dtype) for a in (*grads, thru, *lands)]
    out = pl.pallas_call(
        body, name=name, in_specs=[HBM] * (2 * n + 1), out_specs=(SEM, SEM) + (HBM,) * (2 * n + 1),
        out_shape=(sems, sems, *keep), input_output_aliases={i: 2 + i for i in range(2 * n + 1)},
        compiler_params=pltpu.CompilerParams(has_side_effects=DATAFLOW),
    )(*[pltpu.with_memory_space_constraint(a, pltpu.HBM) for a in (*grads, thru)], *lands)
    return out[0], out[1], out[2:2 + n], out[2 + n], out[3 + n:]


def _rs_direct_wait(send, recv, grads, lands, after, name):
    n = len(grads)

    def body(*refs):
        for cp in _rs_direct_copies(refs[:n], refs[n:2 * n], refs[2 * n], refs[2 * n + 1]):
            cp.wait_send()
            cp.wait_recv()

    keep = [pltpu.HBM(a.shape, a.dtype) for a in (*grads, *lands)]
    out = pl.pallas_call(
        body, name=name, in_specs=[HBM] * (2 * n) + [SEM, SEM, pl.BlockSpec(memory_space=pl.ANY)], out_specs=(HBM,) * (2 * n),
        out_shape=tuple(keep), input_output_aliases={i: i for i in range(2 * n)},
        compiler_params=pltpu.CompilerParams(has_side_effects=DATAFLOW),
    )(*grads, *lands, send, recv, after)
    return out[:n], out[n:]


_CHIPS = ((1, 0), (0, 1), (1, 1))


def _gather_ici_copies(bufs, send, recv):
    x, y, c = _place()
    cps = []
    for k in range(len(bufs)):
        mine = bufs[k].at[2 * x + y, c]
        for j, (dx, dy) in enumerate(_CHIPS):
            cps.append(pltpu.make_async_remote_copy(
                src_ref=mine, dst_ref=mine, send_sem=send.at[k * len(_CHIPS) + j], recv_sem=recv.at[k * len(_CHIPS) + j],
                device_id=(_flip(x, dx), _flip(y, dy), c), device_id_type=MESH))
    return cps


def _gather_d2d_copies(bufs, send, recv):
    x, y, c = _place()
    cps = []
    for k in range(len(bufs)):
        for j, (dx, dy) in enumerate(_CHIPS):
            got = bufs[k].at[2 * _flip(x, dx) + _flip(y, dy), c]
            cps.append(pltpu.make_async_remote_copy(
                src_ref=got, dst_ref=got, send_sem=send.at[k * len(_CHIPS) + j], recv_sem=recv.at[k * len(_CHIPS) + j],
                device_id=(x, y, 1 - c), device_id_type=MESH))
    return cps


def _copies_start(bufs, thru, copies, name):
    n = len(bufs)

    def body(*refs):
        for cp in copies(refs[:n], refs[n + 1], refs[n + 2]):
            cp.start()

    sems = pltpu.SemaphoreType.DMA((n * len(_CHIPS),))
    out = pl.pallas_call(
        body, name=name, in_specs=[HBM] * (n + 1), out_specs=(SEM, SEM) + (HBM,) * (n + 1),
        out_shape=(sems, sems, *[pltpu.HBM(a.shape, a.dtype) for a in (*bufs, thru)]),
        input_output_aliases={i: 2 + i for i in range(n + 1)},
        compiler_params=pltpu.CompilerParams(has_side_effects=DATAFLOW),
    )(*[pltpu.with_memory_space_constraint(a, pltpu.HBM) for a in (*bufs, thru)])
    return out[0], out[1], out[2:2 + n], out[2 + n]


def _copies_wait(send, recv, bufs, after, copies, name):
    n = len(bufs)

    def body(*refs):
        for cp in copies(refs[:n], refs[n], refs[n + 1]):
            cp.wait_send()
            cp.wait_recv()

    return pl.pallas_call(
        body, name=name, in_specs=[HBM] * n + [SEM, SEM, pl.BlockSpec(memory_space=pl.ANY)], out_specs=(HBM,) * n,
        out_shape=tuple(pltpu.HBM(a.shape, a.dtype) for a in bufs), input_output_aliases={i: i for i in range(n)},
        compiler_params=pltpu.CompilerParams(has_side_effects=DATAFLOW),
    )(*bufs, send, recv, after)


def _rs_direct_sum(grad, land, layer, name, into=None):
    _, _, rows, cols = grad.shape
    t = _rs_rows(rows, cols)
    npieces = len(_REL) + 1

    def body(g_ref, l_ref, *rest):
        o_ref, acc_ref = rest[-2], rest[-1]
        j = pl.program_id(1)

        @pl.when(j == 0)
        def _():
            acc_ref[...] = g_ref[...].astype(F32)

        @pl.when(j > 0)
        def _():
            acc_ref[...] += l_ref[...].astype(F32)

        @pl.when(j == npieces - 1)
        def _():
            o_ref[...] = acc_ref[...]

    def mine(i, j):
        x, y, c = _place()
        return (2 * x + y, c, i, 0)

    in_specs = [pl.BlockSpec((None, None, t, cols), mine),
                pl.BlockSpec((None, t, cols), lambda i, j: (jnp.maximum(j - 1, 0), i, 0))]
    args = [grad, land]
    if into is not None:
        in_specs.append(pl.BlockSpec(memory_space=pl.ANY))
        args.append(into)
    return pl.pallas_call(
        body, name=name, grid=(rows // t, npieces), in_specs=in_specs,
        out_specs=pl.BlockSpec((None, None, t, cols), lambda i, j: (layer, lax.axis_index("c"), i, 0)),
        scratch_shapes=[pltpu.VMEM((t, cols), F32)],
        out_shape=jax.ShapeDtypeStruct((2, 2, rows, cols), F32), input_output_aliases={} if into is None else {2: 0},
        compiler_params=_cp("parallel", "arbitrary"),
    )(*args)


def _rs_tree(grads, into, layer):
    r1 = _exchange(grads, "c", "rs_c", half=2)
    a1 = [_rs_add(g, r, lambda s, p: (s, p[2]), (4,), BF16, f"rs_add1_{k}") for k, (g, r) in enumerate(zip(grads, r1))]
    a1 = [a.reshape(2, 2, 2, a.shape[1] // 2, a.shape[2]) for a in a1]
    r2 = _rs_exchange(a1, 2, "rs_p2")
    a2 = [_rs_add(a, r, lambda h, j, p: (p[0] * (1 - h) + j * h, j * (1 - h) + p[1] * h, h), (2, 2), BF16, f"rs_add2_{k}")
          for k, (a, r) in enumerate(zip(a1, r2))]
    r3 = _rs_exchange(a2, 3, "rs_p3")
    return [_rs_add(a, r, lambda h, p: (h, p[1] * (1 - h) + p[0] * h), (2,), F32, f"rs_add3_{k}", into=(buf, layer))
            for k, (a, r, buf) in enumerate(zip(a2, r3, into))]


def _ar_copies(buf, land, send, recv):
    x, y, c = _place()
    return [pltpu.make_async_remote_copy(src_ref=buf, dst_ref=land.at[r], send_sem=send.at[r], recv_sem=recv.at[r],
                                         device_id=(_flip(x, dx), _flip(y, dy), _flip(c, dc)), device_id_type=MESH)
            for r, (dx, dy, dc) in enumerate(_REL)]


def _ar_start(buf, name):
    land = pltpu.with_memory_space_constraint(lax.empty((len(_REL),) + buf.shape, buf.dtype), pltpu.HBM)

    def body(buf_ref, land_ref, send, recv, *_):
        for cp in _ar_copies(buf_ref, land_ref, send, recv):
            cp.start()

    sems = pltpu.SemaphoreType.DMA((len(_REL),))
    return pl.pallas_call(
        body, name=name, in_specs=[HBM, HBM], out_specs=(SEM, SEM, HBM, HBM),
        out_shape=(sems, sems, pltpu.HBM(buf.shape, buf.dtype), pltpu.HBM(land.shape, land.dtype)),
        input_output_aliases={0: 2, 1: 3}, compiler_params=pltpu.CompilerParams(has_side_effects=DATAFLOW),
    )(pltpu.with_memory_space_constraint(buf, pltpu.HBM), land)


def _ar_wait(send, recv, buf, land, after, name):
    def body(buf_ref, land_ref, send_ref, recv_ref, *_):
        for cp in _ar_copies(buf_ref, land_ref, send_ref, recv_ref):
            cp.wait_send()
            cp.wait_recv()

    return pl.pallas_call(
        body, name=name, in_specs=[HBM, HBM, SEM, SEM, pl.BlockSpec(memory_space=pl.ANY)], out_specs=(HBM, HBM),
        out_shape=(pltpu.HBM(buf.shape, buf.dtype), pltpu.HBM(land.shape, land.dtype)), input_output_aliases={0: 0, 1: 1},
        compiler_params=pltpu.CompilerParams(has_side_effects=DATAFLOW),
    )(buf, land, send, recv, after)


def _ar_sum(buf, land, name):
    rows, cols = buf.shape
    t = _rs_rows(rows, cols)

    def slot(i, j):
        x, y, c = _place()
        xd, yd, cd = j // 4, (j // 2) % 2, j % 2
        rel = 4 * (x + xd - 2 * x * xd) + 2 * (y + yd - 2 * y * yd) + (c + cd - 2 * c * cd)
        return (jnp.maximum(rel - 1, 0), i, 0)

    def body(b_ref, l_ref, o_ref, acc_ref):
        j = pl.program_id(1)
        x, y, c = _place()
        val = jnp.where(j == 4 * x + 2 * y + c, b_ref[...], l_ref[...])

        @pl.when(j == 0)
        def _():
            acc_ref[...] = val

        @pl.when(j > 0)
        def _():
            acc_ref[...] += val

        @pl.when(j == len(_REL))
        def _():
            o_ref[...] = acc_ref[...]

    sp = pl.BlockSpec((t, cols), lambda i, j: (i, 0))
    return pl.pallas_call(
        body, name=name, grid=(rows // t, len(_REL) + 1), in_specs=[sp, pl.BlockSpec((None, t, cols), slot)], out_specs=sp,
        out_shape=jax.ShapeDtypeStruct((rows, cols), F32), scratch_shapes=[pltpu.VMEM((t, cols), F32)],
        compiler_params=_cp("parallel", "arbitrary"),
    )(buf, land)


def _pad128(v):
    return jnp.zeros((1, 128), F32).at[0, :v.shape[0]].set(v)


def _layer_fwd(l, x, p, hooks=None):
    hooks = hooks or {}
    h1 = _rms_fwd(x, p["norm1_w"], f"rms1_{l}")
    proj = _matmul(h1, p["w_in"], "nn", name=f"mm_in_{l}", tn=768)
    y_pool = _pool_fwd(proj, p["pool_w"], p["pool_b"], p["pool_scale"], f"pool_{l}")
    qkv = _gdn_conv_fwd(proj, p["gdn_conv_w"], f"gconv_{l}")
    alog, dtb = _pad128(p["gdn_a_log"]), _pad128(p["gdn_dt_bias"])
    gw, gu, gqd, gkd, gat, tinv, gc = _gdn_prep(qkv, proj, alog, dtb, f"gprep_{l}")
    o, states = _gdn_scan(gw, gu, gqd, gkd, gat, gc, f"gscan_{l}")
    if "mid" in hooks:
        o = hooks["mid"](o)
    hl = _lru_fwd(proj, p["lru_conv_w"], p["lru_conv_b"], p["lru_wa"], p["lru_ba"], p["lru_wx"], p["lru_bx"],
                  p["lru_lambda"], f"lru_{l}")
    mixed = _mix_out(o, proj, hl, y_pool, p["gdn_norm_w"].reshape(1, GDN_DH), f"mix_{l}")
    x2 = _matmul(mixed, p["w_out"], "nn", name=f"mm_out_{l}", res=x)
    h2 = _rms_fwd(x2, p["norm2_w"], f"rms2_{l}")
    if "ffn" in hooks:
        h2 = hooks["ffn"](h2)
    up = _matmul(h2, p["ffn_up"], "nn", name=f"mm_up_{l}", b_split=True)
    act = _ffn_act_fwd(up, p["ffn_conv_w"], f"ffn_{l}")
    if "down" in hooks:
        act = hooks["down"](act)
    x3 = _matmul(act, p["ffn_down"], "nn", name=f"mm_down_{l}", res=x2)
    saved = dict(x=x, h1=h1, proj=proj, qkv=qkv, gdn=(gw, gu, gqd, gkd, gat, gc), tinv=tinv, states=states, o=o, hl=hl, mixed=mixed,
                 x2=x2, h2=h2, up=up, act=act, alog=alog, dtb=dtb)
    return x3, saved


def _layer_bwd(l, dx3, p, s, after_ffn=None):
    g = {}
    dact = _matmul(dx3, p["ffn_down"], "nt", name=f"mm_ddown_{l}")
    g["ffn_down"] = _matmul(s["act"], dx3, "tn", name=f"mm_gdown_{l}", out_dtype=BF16)
    dup, g["ffn_conv_w"] = _ffn_act_bwd(s["up"], dact, p["ffn_conv_w"], f"ffn_b_{l}")
    dh2 = _matmul(dup, p["ffn_up"], "nt", name=f"mm_dup_{l}", b_split=True, tk=3072)
    g["ffn_up"] = _matmul(s["h2"], dup, "tn", name=f"mm_gup_{l}", b_split=True, o_split=4, tk=4096, out_dtype=BF16)
    dx2, g["norm2_w"] = _rms_bwd(s["x2"], p["norm2_w"], dh2, dx3, f"rms2_b_{l}")
    if after_ffn is not None:
        dx2 = after_ffn(dx2, g)
    dmixed = _matmul(dx2, p["w_out"], "nt", name=f"mm_dout_{l}")
    g["w_out"] = _matmul(s["mixed"], dx2, "tn", name=f"mm_gout_{l}", out_dtype=BF16)
    proj = s["proj"]
    du_pool, g["pool_w"], g["pool_b"], g["pool_scale"] = _pool_bwd(proj, dmixed, p["pool_w"], p["pool_b"], p["pool_scale"], f"pool_b_{l}")
    gw, gu, gqd, gkd, gat, gc = s["gdn"]
    dw, du, dqd, dkd, dat, dgl, dz, g["gdn_norm_w"] = _gdn_scan_bwd(
        gw, gu, gqd, gkd, gat, gc, s["states"], s["o"], proj, dmixed, p["gdn_norm_w"].reshape(1, GDN_DH), f"gscan_b_{l}")
    dqkv, dab, gal, gdt = _gdn_prep_bwd(s["qkv"], proj, s["alog"], s["dtb"], s["tinv"], dw, du, dqd, dkd, dat, dgl, f"gprep_b_{l}")
    g["gdn_a_log"], g["gdn_dt_bias"] = gal[0, :GDN_H], gdt[0, :GDN_H]
    dpre, g["gdn_conv_w"] = _gdn_conv_bwd(proj, dqkv, p["gdn_conv_w"], f"gconv_b_{l}")
    (dxr, dgr, g["lru_conv_w"], g["lru_conv_b"], g["lru_wa"], g["lru_ba"], g["lru_wx"], g["lru_bx"], g["lru_lambda"]) = _lru_bwd(
        proj, s["hl"], dmixed, p["lru_conv_w"], p["lru_conv_b"], p["lru_wa"], p["lru_ba"], p["lru_wx"], p["lru_bx"],
        p["lru_lambda"], f"lru_b_{l}")
    S = proj.shape[0]
    dproj = jnp.concatenate([dpre, dz, dxr, dgr, du_pool, dab, jnp.zeros((S, PCOLS - PAB - 128), BF16)], axis=1)
    dh1 = _matmul(dproj, p["w_in"], "nt", name=f"mm_din_{l}", tk=1792)
    g["w_in"] = _matmul(s["h1"], dproj, "tn", name=f"mm_gin_{l}", tn=768, tk=4096, out_dtype=BF16)
    dx, g["norm1_w"] = _rms_bwd(s["x"], p["norm1_w"], dh1, dx2, f"rms1_b_{l}")
    return dx, g


_IN_PERM = ((512, 3584), (3596, 5132), (0, 512), (3584, 3596))


def _w_in_to_proj(w):
    parts = [w[:, a:b] for a, b in _IN_PERM]
    return jnp.concatenate(parts + [jnp.zeros((w.shape[0], PCOLS - IN_COLS), w.dtype)], axis=1)


def _proj_to_w_in(g):
    return jnp.concatenate([g[:, PPOOL:PPOOL + 512], g[:, 0:3072], g[:, PAB:PAB + 12], g[:, 3072:PPOOL]], axis=1)


def _rows_to_mixed(w):
    return jnp.concatenate([w[512:], w[:512]], axis=0)


def _mixed_to_rows(g):
    return jnp.concatenate([g[1536:], g[:1536]], axis=0)


SMALL_SHARDED = ("gdn_conv_w", "lru_conv_w", "ffn_conv_w")
BIG = ("w_in", "w_out", "ffn_up", "ffn_down")
SMALL_REPLICATED = ("norm1_w", "pool_w", "pool_b", "pool_scale", "gdn_a_log", "gdn_dt_bias", "gdn_norm_w", "lru_conv_b",
                    "lru_wa", "lru_ba", "lru_wx", "lru_bx", "lru_lambda", "norm2_w")
WEIGHTS = ("norm1_w", "w_in", "pool_w", "pool_b", "pool_scale", "gdn_conv_w", "gdn_a_log", "gdn_dt_bias", "gdn_norm_w",
           "lru_conv_w", "lru_conv_b", "lru_wa", "lru_ba", "lru_wx", "lru_bx", "lru_lambda", "w_out", "norm2_w", "ffn_up",
           "ffn_conv_w", "ffn_down", "final_norm_w")
FLAT_COLS = 1024


def _pack(arrs):
    flat = jnp.concatenate([a.reshape(-1) for a in arrs])
    rows = -(-flat.shape[0] // (8 * FLAT_COLS)) * 8
    return jnp.pad(flat, (0, rows * FLAT_COLS - flat.shape[0])).reshape(rows, FLAT_COLS)


def _unpack(buf, like):
    flat = buf.reshape(-1)
    out, off = [], 0
    for a in like:
        size = 1
        for d in a.shape:
            size *= d
        out.append(flat[off:off + size].reshape(a.shape))
        off += size
    return out


def kernel(x, norm1_w, w_in, pool_w, pool_b, pool_scale, gdn_conv_w, gdn_a_log, gdn_dt_bias, gdn_norm_w, lru_conv_w, lru_conv_b, lru_wa, lru_ba, lru_wx, lru_bx, lru_lambda, w_out, norm2_w, ffn_up, ffn_conv_w, ffn_down, final_norm_w, loss_target, m_norm1_w, m_w_in, m_pool_w, m_pool_b, m_pool_scale, m_gdn_conv_w, m_gdn_a_log, m_gdn_dt_bias, m_gdn_norm_w, m_lru_conv_w, m_lru_conv_b, m_lru_wa, m_lru_ba, m_lru_wx, m_lru_bx, m_lru_lambda, m_w_out, m_norm2_w, m_ffn_up, m_ffn_conv_w, m_ffn_down, m_final_norm_w, v_norm1_w, v_w_in, v_pool_w, v_pool_b, v_pool_scale, v_gdn_conv_w, v_gdn_a_log, v_gdn_dt_bias, v_gdn_norm_w, v_lru_conv_w, v_lru_conv_b, v_lru_wa, v_lru_ba, v_lru_wx, v_lru_bx, v_lru_lambda, v_w_out, v_norm2_w, v_ffn_up, v_ffn_conv_w, v_ffn_down, v_final_norm_w):
    W = dict(norm1_w=norm1_w, w_in=w_in, pool_w=pool_w, pool_b=pool_b, pool_scale=pool_scale, gdn_conv_w=gdn_conv_w,
             gdn_a_log=gdn_a_log, gdn_dt_bias=gdn_dt_bias, gdn_norm_w=gdn_norm_w, lru_conv_w=lru_conv_w, lru_conv_b=lru_conv_b,
             lru_wa=lru_wa, lru_ba=lru_ba, lru_wx=lru_wx, lru_bx=lru_bx, lru_lambda=lru_lambda, w_out=w_out, norm2_w=norm2_w,
             ffn_up=ffn_up, ffn_conv_w=ffn_conv_w, ffn_down=ffn_down, final_norm_w=final_norm_w)
    M = dict(norm1_w=m_norm1_w, w_in=m_w_in, pool_w=m_pool_w, pool_b=m_pool_b, pool_scale=m_pool_scale, gdn_conv_w=m_gdn_conv_w,
             gdn_a_log=m_gdn_a_log, gdn_dt_bias=m_gdn_dt_bias, gdn_norm_w=m_gdn_norm_w, lru_conv_w=m_lru_conv_w,
             lru_conv_b=m_lru_conv_b, lru_wa=m_lru_wa, lru_ba=m_lru_ba, lru_wx=m_lru_wx, lru_bx=m_lru_bx, lru_lambda=m_lru_lambda,
             w_out=m_w_out, norm2_w=m_norm2_w, ffn_up=m_ffn_up, ffn_conv_w=m_ffn_conv_w, ffn_down=m_ffn_down,
             final_norm_w=m_final_norm_w)
    V = dict(norm1_w=v_norm1_w, w_in=v_w_in, pool_w=v_pool_w, pool_b=v_pool_b, pool_scale=v_pool_scale, gdn_conv_w=v_gdn_conv_w,
             gdn_a_log=v_gdn_a_log, gdn_dt_bias=v_gdn_dt_bias, gdn_norm_w=v_gdn_norm_w, lru_conv_w=v_lru_conv_w,
             lru_conv_b=v_lru_conv_b, lru_wa=v_lru_wa, lru_ba=v_lru_ba, lru_wx=v_lru_wx, lru_bx=v_lru_bx, lru_lambda=v_lru_lambda,
             w_out=v_w_out, norm2_w=v_norm2_w, ffn_up=v_ffn_up, ffn_conv_w=v_ffn_conv_w, ffn_down=v_ffn_down,
             final_norm_w=v_final_norm_w)
    S = x.shape[1]
    xs = x.reshape(S, D_MODEL)
    tgt = loss_target.reshape(S, D_MODEL)
    mx, my, mc = _place()
    shard = 2 * mx + my

    small_sh = jnp.concatenate([W[k].reshape(N_LAYERS, -1) for k in SMALL_SHARDED], axis=1)
    n_small = small_sh.shape[1]
    pad = -n_small % 1024
    small_sh = jnp.pad(small_sh, ((0, 0), (0, pad))).reshape(N_LAYERS, -1, 1024)

    def own_slots(l):
        out = []
        for w in [W[k][l].astype(BF16) for k in BIG] + [small_sh[l]]:
            buf = lax.dynamic_update_slice(lax.empty((4,) + w.shape, w.dtype), w[None], (shard,) + (0,) * w.ndim)
            out.append(buf.reshape(4, 2, w.shape[0] // 2, w.shape[1]))
        return out

    def whole(g):
        return g.reshape(4, 2 * g.shape[2], g.shape[3])

    def mixer_params(l, g_in, g_out, g_small):
        p = {k: W[k][l] for k in SMALL_REPLICATED}
        g_in = whole(g_in)
        p["w_in"] = _w_in_to_proj(jnp.transpose(g_in, (1, 0, 2)).reshape(g_in.shape[1], IN_COLS))
        p["w_out"] = _rows_to_mixed(whole(g_out).reshape(D_MODEL, D_MODEL))
        g_small = whole(g_small).reshape(4, -1)[:, :n_small]
        off = 0
        for k in SMALL_SHARDED:
            taps, width = W[k].shape[1], W[k].shape[2]
            piece = g_small[:, off:off + taps * width].reshape(4, taps, width)
            p[k] = jnp.transpose(piece, (1, 0, 2)).reshape(taps, 4 * width)
            off += taps * width
        return p

    def ffn_params(g_up, g_down):
        return dict(ffn_up=whole(g_up), ffn_down=whole(g_down).reshape(D_FF, D_MODEL))

    layers, saved = [None] * N_LAYERS, [None] * N_LAYERS
    s0 = own_slots(0)
    g_in0, g_out0, g_small0 = _gather_weights([s0[0], s0[1], s0[4]], "gather_weights")
    f_send, f_recv, ffn0, g_in0 = _copies_start(s0[2:4], g_in0, _gather_ici_copies, "gather_ffn0_ici_start")
    l_send, l_recv, bufs1, g_in0 = _copies_start(own_slots(1), g_in0, _gather_ici_copies, "gather_l1_ici_start")
    layers[0] = mixer_params(0, g_in0, g_out0, g_small0)
    stage = {}

    def mid(o):
        bufs = _copies_wait(f_send, f_recv, ffn0, o, _gather_ici_copies, "gather_ffn0_ici_wait")
        stage["ffn0"] = _copies_start(bufs, o, _gather_d2d_copies, "gather_ffn0_d2d_start")
        return stage["ffn0"][3]

    def ffn(h2):
        send, recv, bufs, _ = stage["ffn0"]
        layers[0].update(ffn_params(*_copies_wait(send, recv, bufs, h2, _gather_d2d_copies, "gather_ffn0_d2d_wait")))
        return h2

    def down(act):
        bufs = _copies_wait(l_send, l_recv, bufs1, act, _gather_ici_copies, "gather_l1_ici_wait")
        stage["l1"] = _copies_start(bufs, act, _gather_d2d_copies, "gather_l1_d2d_start")
        return stage["l1"][3]

    h, saved[0] = _layer_fwd(0, xs, layers[0], dict(mid=mid, ffn=ffn, down=down))
    send, recv, bufs, _ = stage["l1"]
    g1 = _copies_wait(send, recv, bufs, h, _gather_d2d_copies, "gather_l1_d2d_wait")
    layers[1] = {**mixer_params(1, g1[0], g1[1], g1[4]), **ffn_params(g1[2], g1[3])}
    h, saved[1] = _layer_fwd(1, h, layers[1])
    loss_part, dh, g_final = _loss_head(h, final_norm_w, tgt, "loss_head")

    def big_partials(g_layer, names=BIG):
        out = []
        for k in names:
            g = g_layer[k]
            if k == "w_in":
                g = _proj_to_w_in(g)
                g = jnp.transpose(g.reshape(g.shape[0], 4, IN_COLS // 4), (1, 0, 2))
            elif k == "w_out":
                g = _mixed_to_rows(g).reshape(4, D_MODEL // 4, D_MODEL)
            elif k == "ffn_down":
                g = g.reshape(4, D_FF // 4, D_MODEL)
            out.append(g.reshape(4, 2, g.shape[1] // 2, g.shape[2]))
        return out

    FFN, MIX = ("ffn_up", "ffn_down"), ("w_in", "w_out")
    grads = [None] * N_LAYERS
    dh, grads[1] = _layer_bwd(1, dh, layers[1], saved[1])
    send1, recv1, part1, dh, lands1 = _rs_direct_start(big_partials(grads[1]), dh, "rs_direct_start_1")
    sent0 = []

    def after_ffn(dx2, g):
        send0, recv0, part0, dx2, lands0 = _rs_direct_start(big_partials(g, FFN), dx2, "rs_direct_start_0")
        sent0.extend([send0, recv0, part0, lands0])
        return dx2

    dh, grads[0] = _layer_bwd(0, dh, layers[0], saved[0], after_ffn)

    small_names = SMALL_REPLICATED + SMALL_SHARDED
    small_list = [jnp.stack([grads[l][k].reshape(W[k].shape[1:]) if k in SMALL_REPLICATED else grads[l][k] for l in range(N_LAYERS)])
                  for k in small_names]
    small_list += [g_final.reshape(D_MODEL), loss_part[0, 0:1]]
    a_send, a_recv, packed, a_land = _ar_start(_pack(small_list), "ar_start")

    part1, lands1 = _rs_direct_wait(send1, recv1, part1, lands1, dh, "rs_direct_wait_1")
    part0, lands0 = _rs_direct_wait(*sent0, dh, "rs_direct_wait_0")
    red = {k: _rs_direct_sum(g, ld, 1, f"rs_sum_1_{k}") for k, g, ld in zip(BIG, part1, lands1)}
    for k, g, ld in zip(FFN, part0, lands0):
        red[k] = _rs_direct_sum(g, ld, 0, f"rs_sum_0_{k}", into=red[k])
    for k, r in zip(MIX, _rs_tree(big_partials(grads[0], MIX), [red[k] for k in MIX], 0)):
        red[k] = r
    reduced = _share_halves([red[k] for k in BIG], "rs_share")
    big_g = {k: r.reshape(N_LAYERS, 2 * r.shape[2], r.shape[3]) for k, r in zip(BIG, reduced)}
    grad_x = dh.reshape(x.shape)

    packed, a_land = _ar_wait(a_send, a_recv, packed, a_land, reduced[0], "ar_wait")
    reduced = _unpack(_ar_sum(packed, a_land, "ar_sum"), small_list)
    small_g = dict(zip(small_names, reduced[:len(small_names)]))
    small_g["final_norm_w"] = reduced[-2]
    loss = reduced[-1][0]
    for k in SMALL_SHARDED:
        width = W[k].shape[2]
        small_g[k] = lax.dynamic_slice_in_dim(small_g[k], shard * width, width, axis=2)

    G, DELTA, NM, NV = {}, {}, {}, {}
    for k in BIG:
        G[k] = big_g[k]
        DELTA[k], NM[k], NV[k] = _adamw(W[k], G[k], M[k], V[k], f"adam_{k}")
    small_all = small_names + ("final_norm_w",)
    dl, nm, nv = _adamw(_pack([W[k] for k in small_all]), _pack([small_g[k] for k in small_all]),
                        _pack([M[k] for k in small_all]), _pack([V[k] for k in small_all]), "adam_small")
    like = [W[k] for k in small_all]
    for k, d_, m_, v_ in zip(small_all, _unpack(dl, like), _unpack(nm, like), _unpack(nv, like)):
        G[k], DELTA[k], NM[k], NV[k] = small_g[k], d_, m_, v_

    return (loss, grad_x, *[G[k] for k in WEIGHTS], *[DELTA[k] for k in WEIGHTS], *[NM[k] for k in WEIGHTS],
            *[NV[k] for k in WEIGHTS])
```

```python
import functools

import jax
import jax.numpy as jnp
from jax import lax
from jax.experimental import pallas as pl
from jax.experimental.pallas import tpu as pltpu

F32 = jnp.float32
BF16 = jnp.bfloat16
_MXU = jnp.bfloat16

D_MODEL = 2048
N_LAYERS = 2
POOL_W = 512
POOL_G = 4
POOL_GD = 128
POOL_WINDOWS = (2, 4, 8, 16)
POOL_HALO = 16
GDN_W = 768
GDN_H = 6
GDN_DH = 128
GDN_C = 64
LRU_W = 768
LRU_NB = 6
LRU_BD = 128
LRU_C = 8.0
D_FF = 6144
EPS = 1e-6
IN_COLS = 5132
HALO = 8

PQ, PK, PV, PZ, PXR, PGR, PPOOL, PAB, PCOLS = 0, 768, 1536, 2304, 3072, 3840, 4608, 5120, 5376
CB = 768

ADAM_LR = 0.001
ADAM_B1 = 0.9
ADAM_B2 = 0.999
ADAM_EPS = 1e-08
ADAM_WD = 0.01
ADAM_STEP = 10

VMEM_LIMIT = 56 * 1024 * 1024
MESH = pl.DeviceIdType.MESH
HBM = pl.BlockSpec(memory_space=pltpu.HBM)


def _cp(*sem):
    return pltpu.CompilerParams(dimension_semantics=sem, vmem_limit_bytes=VMEM_LIMIT)


def _dg(a, b, ta, tb):
    dims = (((0 if ta else 1,), (1 if tb else 0,)), ((), ()))
    return lax.dot_general(a, b, dims, preferred_element_type=F32)


def _split2(a):
    hi = a.astype(BF16)
    lo = (a - hi.astype(F32)).astype(BF16)
    return hi, lo


def _mm_raw(a, b, ta, tb, hi):
    if _MXU == F32:
        return _dg(a, b, ta, tb)
    if not hi:
        return _dg(a.astype(_MXU), b.astype(_MXU), ta, tb)
    a1, a2 = _split2(a)
    b1, b2 = _split2(b)
    return _dg(a1, b1, ta, tb) + (_dg(a1, b2, ta, tb) + _dg(a2, b1, ta, tb))


@functools.partial(jax.custom_vjp, nondiff_argnums=(2, 3, 4))
def _mm(a, b, ta=False, tb=False, hi=False):
    return _mm_raw(a, b, ta, tb, hi)


def _mm_fwd(a, b, ta, tb, hi):
    return _mm_raw(a, b, ta, tb, hi), (a, b)


def _mm_bwd(ta, tb, hi, res, dc):
    a, b = res
    da = _mm(b, dc, tb, True, hi) if ta else _mm(dc, b, False, not tb, hi)
    db = _mm(dc, a, True, ta, hi) if tb else _mm(a, dc, not ta, False, hi)
    return da, db


_mm.defvjp(_mm_fwd, _mm_bwd)


def _mm01(m01, x):
    if _MXU == F32:
        return _dg(m01, x, False, False)
    m = m01.astype(BF16)
    x1 = x.astype(BF16)
    r = x - x1.astype(F32)
    x2 = r.astype(BF16)
    x3 = (r - x2.astype(F32)).astype(BF16)
    return _dg(m, x1, False, False) + (_dg(m, x2, False, False) + _dg(m, x3, False, False))


def _down(x, k):
    return x if k == 0 else pltpu.roll(x, k, 0)


def _up(x, k):
    return x if k == 0 else pltpu.roll(x, x.shape[0] - k, 0)


def _rows(shape):
    return lax.broadcasted_iota(jnp.int32, shape, 0)


def _lanes(shape):
    return lax.broadcasted_iota(jnp.int32, shape, 1)


def _matmul(a, b, mode, *, name, res=None, tm=1024, tn=1024, tk=2048, b_split=False, o_split=0, out_dtype=F32):
    ta, tb = mode == "tn", mode == "nt"
    a_split = a.ndim == 3
    if a_split:
        assert not ta
        M, K = a.shape[1], a.shape[0] * a.shape[2]
        tk = min(tk, a.shape[2])
    elif ta:
        K, M = a.shape
    else:
        M, K = a.shape
    if b_split:
        ns = b.shape[0]
        N = b.shape[1] if tb else ns * b.shape[2]
    else:
        N = b.shape[0] if tb else b.shape[1]
    tm, tn, tk = min(tm, M), min(tn, N), min(tk, K)
    if b_split:
        per = b.shape[2]
        if tb:
            tk = min(tk, per)
        else:
            tn = min(tn, per)
    if o_split:
        tn = min(tn, N // o_split)
    assert M % tm == 0 and N % tn == 0 and K % tk == 0, (name, M, N, K, tm, tn, tk)
    nk = K // tk
    if a_split:
        ka = a.shape[2] // tk
        a_spec = pl.BlockSpec((None, tm, tk), lambda i, j, k: (k // ka, i, k % ka))
    else:
        a_spec = pl.BlockSpec((tk, tm), lambda i, j, k: (k, i)) if ta else pl.BlockSpec((tm, tk), lambda i, j, k: (i, k))
    if not b_split:
        b_spec = pl.BlockSpec((tn, tk), lambda i, j, k: (j, k)) if tb else pl.BlockSpec((tk, tn), lambda i, j, k: (k, j))
    elif tb:
        kb = per // tk
        b_spec = pl.BlockSpec((None, tn, tk), lambda i, j, k: (k // kb, j, k % kb))
    else:
        nb = per // tn
        b_spec = pl.BlockSpec((None, tk, tn), lambda i, j, k: (j // nb, k, j % nb))
    if o_split:
        ob = (N // o_split) // tn
        out_shape = jax.ShapeDtypeStruct((o_split, M, N // o_split), out_dtype)
        o_spec = pl.BlockSpec((None, tm, tn), lambda i, j, k: (j // ob, i, j % ob))
    else:
        out_shape = jax.ShapeDtypeStruct((M, N), out_dtype)
        o_spec = pl.BlockSpec((tm, tn), lambda i, j, k: (i, j))
    in_specs = [a_spec, b_spec]
    args = [a, b]
    if res is not None:
        in_specs.append(pl.BlockSpec((tm, tn), lambda i, j, k: (i, j)))
        args.append(res)
    use_acc = nk > 1 and out_dtype != F32

    def body(*refs):
        a_ref, b_ref = refs[0], refs[1]
        o_ref = refs[2 + (res is not None)]
        acc_ref = refs[-1] if use_acc else o_ref
        p = _dg(a_ref[...].astype(_MXU), b_ref[...].astype(_MXU), ta, tb)
        first = p + refs[2][...] if res is not None else p
        if nk == 1:
            o_ref[...] = first.astype(o_ref.dtype)
        else:
            k = pl.program_id(2)

            @pl.when(k == 0)
            def _():
                acc_ref[...] = first

            @pl.when(k > 0)
            def _():
                acc_ref[...] += p

            if use_acc:
                @pl.when(k == nk - 1)
                def _():
                    o_ref[...] = acc_ref[...].astype(o_ref.dtype)

    return pl.pallas_call(
        body, name=name, grid=(M // tm, N // tn, nk), in_specs=in_specs, out_specs=o_spec, out_shape=out_shape,
        scratch_shapes=[pltpu.VMEM((tm, tn), F32)] if use_acc else [],
        compiler_params=_cp("parallel", "parallel", "arbitrary"),
    )(*args)


def _rms(x, w):
    return x * lax.rsqrt(jnp.mean(x * x, axis=-1, keepdims=True) + EPS) * w


def _row_tile(S, t=512):
    t = min(t, S)
    assert S % t == 0
    return t


def _rms_fwd(x, w, name):
    S, D = x.shape
    T = _row_tile(S)

    def body(x_ref, w_ref, o_ref):
        o_ref[...] = _rms(x_ref[...], w_ref[...]).astype(o_ref.dtype)

    return pl.pallas_call(
        body, name=name, grid=(S // T,),
        in_specs=[pl.BlockSpec((T, D), lambda i: (i, 0)), pl.BlockSpec((1, D), lambda i: (0, 0))],
        out_specs=pl.BlockSpec((T, D), lambda i: (i, 0)), out_shape=jax.ShapeDtypeStruct((S, D), BF16),
        compiler_params=_cp("parallel"),
    )(x, w.reshape(1, D))


def _rms_bwd(x, w, dh, dres, name):
    S, D = x.shape
    T = _row_tile(S)

    def body(x_ref, w_ref, dh_ref, dr_ref, dx_ref, gw_ref):
        _, vjp = jax.vjp(_rms, x_ref[...], w_ref[...])
        dx, dw = vjp(dh_ref[...])
        dx_ref[...] = dr_ref[...] + dx

        @pl.when(pl.program_id(0) == 0)
        def _():
            gw_ref[...] = jnp.zeros_like(gw_ref)

        gw_ref[...] += dw

    row = pl.BlockSpec((T, D), lambda i: (i, 0))
    vec = pl.BlockSpec((1, D), lambda i: (0, 0))
    return pl.pallas_call(
        body, name=name, grid=(S // T,), in_specs=[row, vec, row, row], out_specs=[row, vec],
        out_shape=[jax.ShapeDtypeStruct((S, D), F32), jax.ShapeDtypeStruct((1, D), F32)],
        compiler_params=_cp("arbitrary"),
    )(x, w.reshape(1, D), dh, dres)


def _loss_head(x, w, tgt, name):
    S, D = x.shape
    T = _row_tile(S)

    def body(x_ref, w_ref, t_ref, l_ref, dx_ref, gw_ref):
        y, vjp = jax.vjp(_rms, x_ref[...], w_ref[...])
        err = y - t_ref[...]
        part = 0.5 * jnp.sum(jnp.mean(err * err, axis=-1, keepdims=True), axis=0, keepdims=True)
        dx, dw = vjp(err * (1.0 / D))
        dx_ref[...] = dx

        @pl.when(pl.program_id(0) == 0)
        def _():
            gw_ref[...] = jnp.zeros_like(gw_ref)
            l_ref[...] = jnp.zeros_like(l_ref)

        gw_ref[...] += dw
        l_ref[...] += jnp.broadcast_to(part, l_ref.shape)

    row = pl.BlockSpec((T, D), lambda i: (i, 0))
    vec = pl.BlockSpec((1, D), lambda i: (0, 0))
    return pl.pallas_call(
        body, name=name, grid=(S // T,), in_specs=[row, vec, row],
        out_specs=[pl.BlockSpec((8, 128), lambda i: (0, 0)), row, vec],
        out_shape=[jax.ShapeDtypeStruct((8, 128), F32), jax.ShapeDtypeStruct((S, D), F32), jax.ShapeDtypeStruct((1, D), F32)],
        compiler_params=_cp("arbitrary"),
    )(x, w.reshape(1, D), tgt)


def _by_group(shape, vals):
    g = _lanes(shape) // POOL_GD
    out = vals[-1]
    for k in range(len(vals) - 2, -1, -1):
        out = jnp.where(g == k, vals[k], out)
    return out


def _pool_d(prev, u, t0):
    ext = jnp.concatenate([prev, u], axis=0)
    s2 = ext + _down(ext, 1)
    s4 = s2 + _down(s2, 2)
    s8 = s4 + _down(s4, 4)
    s16 = s8 + _down(s8, 8)
    ssel = _by_group(ext.shape, [s2, s4, s8, s16])[POOL_HALO:]
    win = _by_group(u.shape, [jnp.int32(w) for w in POOL_WINDOWS])
    cnt = jnp.minimum(t0 + _rows(u.shape) + 1, win).astype(F32)
    return ssel / cnt - u


def _pool_lin(d, w_ref, b):
    ys = [_mm(d[:, g * POOL_GD:(g + 1) * POOL_GD], w_ref[g]) for g in range(POOL_G)]
    return jnp.concatenate(ys, axis=1) + b


def _pool_fwd(proj, w, b, scale, name):
    S = proj.shape[0]
    T = _row_tile(S)
    r = T // POOL_HALO
    cb = PPOOL // POOL_W

    def body(u_ref, up_ref, w_ref, b_ref, sc_ref, y_ref):
        i = pl.program_id(0)
        prev = jnp.where(i > 0, up_ref[...], 0.0)
        d = _pool_d(prev, u_ref[...], i * T)
        y_ref[...] = _pool_lin(d, w_ref, b_ref[...]) * sc_ref[...]

    vec = pl.BlockSpec((1, POOL_W), lambda i: (0, 0))
    return pl.pallas_call(
        body, name=name, grid=(S // T,),
        in_specs=[pl.BlockSpec((T, POOL_W), lambda i: (i, cb)),
                  pl.BlockSpec((POOL_HALO, POOL_W), lambda i: (jnp.maximum(i * r - 1, 0), cb)),
                  pl.BlockSpec((POOL_G, POOL_GD, POOL_GD), lambda i: (0, 0, 0)), vec, vec],
        out_specs=pl.BlockSpec((T, POOL_W), lambda i: (i, 0)), out_shape=jax.ShapeDtypeStruct((S, POOL_W), F32),
        compiler_params=_cp("parallel"),
    )(proj, proj, w, b.reshape(1, POOL_W), scale.reshape(1, POOL_W))


def _pool_bwd(proj, dmixed, w, b, scale, name):
    S = proj.shape[0]
    T = _row_tile(S)
    n = S // T
    r = T // POOL_HALO
    cb = PPOOL // POOL_W
    mb = 1536 // POOL_W

    def body(u_ref, up_ref, dy_ref, dyn_ref, w_ref, b_ref, sc_ref, du_ref, gw_ref, gb_ref, gs_ref):
        i = pl.program_id(0)
        sc = sc_ref[...]
        dy = dy_ref[...]
        dy_ext = jnp.concatenate([dy, jnp.where(i < n - 1, dyn_ref[...], 0.0)], axis=0)
        dyl = dy_ext * sc
        dd = jnp.concatenate(
            [_mm(dyl[:, g * POOL_GD:(g + 1) * POOL_GD], w_ref[g], False, True) for g in range(POOL_G)], axis=1)
        t_ext = i * T + _rows(dd.shape)
        win = _by_group(dd.shape, [jnp.int32(v) for v in POOL_WINDOWS])
        cnt = jnp.minimum(t_ext + 1, win).astype(F32)
        e = jnp.where(t_ext < S, dd / cnt, 0.0)
        f2 = e + _up(e, 1)
        f4 = f2 + _up(f2, 2)
        f8 = f4 + _up(f4, 4)
        f16 = f8 + _up(f8, 8)
        du = (_by_group(dd.shape, [f2, f4, f8, f16]) - dd)[:T]
        du_ref[...] = du.astype(du_ref.dtype)

        prev = jnp.where(i > 0, up_ref[...], 0.0)
        d = _pool_d(prev, u_ref[...], i * T)
        ylin = _pool_lin(d, w_ref, b_ref[...])
        dyl_m = dy * sc

        @pl.when(i == 0)
        def _():
            gw_ref[...] = jnp.zeros_like(gw_ref)
            gb_ref[...] = jnp.zeros_like(gb_ref)
            gs_ref[...] = jnp.zeros_like(gs_ref)

        gs_ref[...] += jnp.sum(dy * ylin, axis=0, keepdims=True)
        gb_ref[...] += jnp.sum(dyl_m, axis=0, keepdims=True)
        for g in range(POOL_G):
            sl = slice(g * POOL_GD, (g + 1) * POOL_GD)
            gw_ref[g] += _mm(d[:, sl], dyl_m[:, sl], True, False)

    vec = pl.BlockSpec((1, POOL_W), lambda i: (0, 0))
    wsp = pl.BlockSpec((POOL_G, POOL_GD, POOL_GD), lambda i: (0, 0, 0))
    nh = S // POOL_HALO
    return pl.pallas_call(
        body, name=name, grid=(n,),
        in_specs=[pl.BlockSpec((T, POOL_W), lambda i: (i, cb)),
                  pl.BlockSpec((POOL_HALO, POOL_W), lambda i: (jnp.maximum(i * r - 1, 0), cb)),
                  pl.BlockSpec((T, POOL_W), lambda i: (i, mb)),
                  pl.BlockSpec((POOL_HALO, POOL_W), lambda i: (jnp.minimum((i + 1) * r, nh - 1), mb)),
                  wsp, vec, vec],
        out_specs=[pl.BlockSpec((T, POOL_W), lambda i: (i, 0)), wsp, vec, vec],
        out_shape=[jax.ShapeDtypeStruct((S, POOL_W), BF16), jax.ShapeDtypeStruct((POOL_G, POOL_GD, POOL_GD), F32),
                   jax.ShapeDtypeStruct((1, POOL_W), F32), jax.ShapeDtypeStruct((1, POOL_W), F32)],
        compiler_params=_cp("arbitrary"),
    )(proj, proj, dmixed, dmixed, w, b.reshape(1, POOL_W), scale.reshape(1, POOL_W))


def _conv_rows(ext, w_ref, taps):
    acc = w_ref[taps - 1:taps, :] * ext
    for k in range(1, taps):
        acc = acc + w_ref[taps - 1 - k:taps - k, :] * _down(ext, k)
    return acc


def _conv_t_rows(dc, w_ref, taps):
    acc = w_ref[taps - 1:taps, :] * dc
    for k in range(1, taps):
        acc = acc + w_ref[taps - 1 - k:taps - k, :] * _up(dc, k)
    return acc


def _conv_specs(T, S, ncb0, with_next):
    r = T // HALO
    nh = S // HALO
    main = pl.BlockSpec((T, CB), lambda j, i: (i, j + ncb0))
    prev = pl.BlockSpec((HALO, CB), lambda j, i: (jnp.maximum(i * r - 1, 0), j + ncb0))
    nxt = pl.BlockSpec((HALO, CB), lambda j, i: (jnp.minimum((i + 1) * r, nh - 1), j + ncb0))
    return (main, prev, nxt) if with_next else (main, prev)


def _gdn_conv_fwd(proj, w, name):
    S = proj.shape[0]
    T = _row_tile(S)
    taps = w.shape[0]
    ncb = 3 * GDN_W // CB

    def body(x_ref, xp_ref, w_ref, o_ref):
        i = pl.program_id(1)
        ext = jnp.concatenate([jnp.where(i > 0, xp_ref[...], 0.0), x_ref[...]], axis=0)
        o_ref[...] = jax.nn.silu(_conv_rows(ext, w_ref, taps)[HALO:])

    main, prev = _conv_specs(T, S, PQ // CB, False)
    return pl.pallas_call(
        body, name=name, grid=(ncb, S // T),
        in_specs=[main, prev, pl.BlockSpec((taps, CB), lambda j, i: (0, j))],
        out_specs=pl.BlockSpec((T, CB), lambda j, i: (i, j)), out_shape=jax.ShapeDtypeStruct((S, 3 * GDN_W), F32),
        compiler_params=_cp("parallel", "parallel"),
    )(proj, proj, w)


def _gdn_conv_bwd(proj, dact, w, name):
    S = proj.shape[0]
    T = _row_tile(S)
    n = S // T
    taps = w.shape[0]
    ncb = 3 * GDN_W // CB

    def body(x_ref, xp_ref, xn_ref, d_ref, dn_ref, w_ref, dx_ref, gw_ref):
        i = pl.program_id(1)
        last = i == n - 1
        ext = jnp.concatenate([jnp.where(i > 0, xp_ref[...], 0.0), x_ref[...], jnp.where(last, 0.0, xn_ref[...])], axis=0)
        c = _conv_rows(ext, w_ref, taps)[HALO:]
        d_ext = jnp.concatenate([d_ref[...], jnp.where(last, 0.0, dn_ref[...])], axis=0)
        _, vjp = jax.vjp(jax.nn.silu, c)
        dc = vjp(d_ext)[0]
        dx_ref[...] = _conv_t_rows(dc, w_ref, taps)[:T].astype(dx_ref.dtype)

        @pl.when(i == 0)
        def _():
            gw_ref[...] = jnp.zeros_like(gw_ref)

        dcm = dc[:T]
        for k in range(taps):
            gw_ref[taps - 1 - k:taps - k, :] += jnp.sum(dcm * _down(ext, k)[HALO:HALO + T], axis=0, keepdims=True)

    main, prev, nxt = _conv_specs(T, S, PQ // CB, True)
    dmain, _, dnxt = _conv_specs(T, S, 0, True)
    wsp = pl.BlockSpec((taps, CB), lambda j, i: (0, j))
    return pl.pallas_call(
        body, name=name, grid=(ncb, n), in_specs=[main, prev, nxt, dmain, dnxt, wsp],
        out_specs=[pl.BlockSpec((T, CB), lambda j, i: (i, j)), wsp],
        out_shape=[jax.ShapeDtypeStruct((S, 3 * GDN_W), BF16), jax.ShapeDtypeStruct((taps, 3 * GDN_W), F32)],
        compiler_params=_cp("parallel", "arbitrary"),
    )(proj, proj, proj, dact, dact, w)


def _ffn_act_fwd(up, w, name):
    S = up.shape[0]
    T = _row_tile(S)
    taps = w.shape[0]
    ncb = D_FF // CB

    def body(g_ref, gp_ref, v_ref, w_ref, o_ref):
        i = pl.program_id(1)
        ext = jnp.concatenate([jnp.where(i > 0, gp_ref[...], 0.0), g_ref[...]], axis=0)
        c = _conv_rows(ext, w_ref, taps)[HALO:]
        o_ref[...] = (jax.nn.gelu(c) * v_ref[...]).astype(o_ref.dtype)

    main, prev = _conv_specs(T, S, 0, False)
    val = pl.BlockSpec((T, CB), lambda j, i: (i, j + ncb))
    return pl.pallas_call(
        body, name=name, grid=(ncb, S // T),
        in_specs=[main, prev, val, pl.BlockSpec((taps, CB), lambda j, i: (0, j))],
        out_specs=pl.BlockSpec((T, CB), lambda j, i: (i, j)), out_shape=jax.ShapeDtypeStruct((S, D_FF), BF16),
        compiler_params=_cp("parallel", "parallel"),
    )(up, up, up, w)


def _ffn_act_bwd(up, dact, w, name):
    S = up.shape[0]
    T = _row_tile(S)
    n = S // T
    taps = w.shape[0]
    ncb = D_FF // CB

    def body(g_ref, gp_ref, gn_ref, v_ref, vn_ref, d_ref, dn_ref, w_ref, dup_ref, gw_ref):
        i = pl.program_id(1)
        last = i == n - 1
        ext = jnp.concatenate([jnp.where(i > 0, gp_ref[...], 0.0), g_ref[...], jnp.where(last, 0.0, gn_ref[...])], axis=0)
        c = _conv_rows(ext, w_ref, taps)[HALO:]
        v_ext = jnp.concatenate([v_ref[...], jnp.where(last, 0.0, vn_ref[...])], axis=0)
        d_ext = jnp.concatenate([d_ref[...], jnp.where(last, 0.0, dn_ref[...])], axis=0)
        gl, vjp = jax.vjp(jax.nn.gelu, c)
        dup_ref[1] = (d_ext * gl)[:T].astype(dup_ref.dtype)
        dc = vjp(d_ext * v_ext)[0]
        dup_ref[0] = _conv_t_rows(dc, w_ref, taps)[:T].astype(dup_ref.dtype)

        @pl.when(i == 0)
        def _():
            gw_ref[...] = jnp.zeros_like(gw_ref)

        dcm = dc[:T]
        for k in range(taps):
            gw_ref[taps - 1 - k:taps - k, :] += jnp.sum(dcm * _down(ext, k)[HALO:HALO + T], axis=0, keepdims=True)

    main, prev, nxt = _conv_specs(T, S, 0, True)
    vmain, _, vnxt = _conv_specs(T, S, ncb, True)
    wsp = pl.BlockSpec((taps, CB), lambda j, i: (0, j))
    osp = pl.BlockSpec((2, T, CB), lambda j, i: (0, i, j))
    return pl.pallas_call(
        body, name=name, grid=(ncb, n), in_specs=[main, prev, nxt, vmain, vnxt, main, nxt, wsp],
        out_specs=[osp, wsp],
        out_shape=[jax.ShapeDtypeStruct((2, S, D_FF), BF16), jax.ShapeDtypeStruct((taps, D_FF), F32)],
        compiler_params=_cp("parallel", "arbitrary"),
    )(up, up, up, up, up, dact, dact, w)


def _tri_masks():
    r = _rows((GDN_C, GDN_C))
    c = _lanes((GDN_C, GDN_C))
    return r >= c, r > c


def _each(fn, *cols):
    return tuple(fn(*args) for args in zip(*cols))


def _tri_inv_raw(lows):
    r = _rows(lows[0].shape)
    c = _lanes(lows[0].shape)
    eye = jnp.where(r == c, 1.0, 0.0)
    ps = _each(lambda low: eye - low, lows)
    lps = lows
    for _ in range(5):
        lps = _each(lambda lp: _mm(lp, lp, False, False, True), lps)
        ps = _each(lambda p, lp: p + _mm(p, lp, False, False, True), ps, lps)
    return ps


@jax.custom_vjp
def _tri_inv(lows):
    return _tri_inv_raw(lows)


def _tri_inv_fwd(lows):
    ts = _tri_inv_raw(lows)
    return ts, ts


def _tri_inv_bwd(ts, dts):
    inner = _each(lambda t, dt: _mm(t, dt, True, False, True), ts, dts)
    return (_each(lambda m, t: -_mm(m, t, False, True, True), inner, ts),)


_tri_inv.defvjp(_tri_inv_fwd, _tri_inv_bwd)


@jax.custom_vjp
def _tri_inv_given(lows, ts):
    return ts


def _tri_inv_given_fwd(lows, ts):
    return ts, ts


def _tri_inv_given_bwd(ts, dts):
    return _tri_inv_bwd(ts, dts)[0], _each(jnp.zeros_like, ts)


_tri_inv_given.defvjp(_tri_inv_given_fwd, _tri_inv_given_bwd)


def _gdn_glog(a_col, alog, dtb):
    return -jnp.exp(alog) * jax.nn.softplus(a_col + dtb)


def _decay_operand():
    r = _rows((GDN_C, 2 * GDN_C))
    c = _lanes((GDN_C, 2 * GDN_C))
    return jnp.where((c >= GDN_C) | (r > c), 1.0, 0.0)


def _gdn_decay(glog):
    causal, _ = _tri_masks()
    res = _mm01(jnp.where(causal, 1.0, 0.0), glog * _decay_operand())
    return res[:, GDN_C:GDN_C + 1], res[:, :GDN_C]


def _gdn_decay_bwd(dgcol, dd):
    r = _rows((GDN_C, GDN_C))
    c = _lanes((GDN_C, GDN_C))
    dres = jnp.concatenate([dd, jnp.where(c == 0, dgcol, 0.0)], axis=1)
    dx = _mm01(jnp.where(r <= c, 1.0, 0.0), dres)
    return jnp.sum(dx * _decay_operand(), axis=1, keepdims=True)


def _gdn_chunk(qa, ka, va, bt_col, gcol, dmat, t_saved=None):
    causal, strict = _tri_masks()
    qn = _each(lambda q: q * lax.rsqrt(jnp.sum(q * q, axis=-1, keepdims=True) + EPS) * (GDN_DH ** -0.5), qa)
    kn = _each(lambda k: k * lax.rsqrt(jnp.sum(k * k, axis=-1, keepdims=True) + EPS), ka)
    beta = _each(jax.nn.sigmoid, bt_col)
    eg = _each(jnp.exp, gcol)
    decay = _each(lambda d: jnp.where(causal, jnp.exp(d), 0.0), dmat)
    kk = _each(lambda k: _mm(k, k, False, True), kn)
    low = _each(lambda b, m, d: jnp.where(strict, b * m * d, 0.0), beta, kk, decay)
    t = _tri_inv(low) if t_saved is None else _tri_inv_given(low, t_saved)
    w = _each(lambda t_, k, b, e: _mm(t_, k * (b * e), False, False, True), t, kn, beta, eg)
    u = _each(lambda t_, v, b: _mm(t_, v * b, False, False, True), t, va, beta)
    attn = _each(lambda q, k, d: _mm(q, k, False, True) * d, qn, kn, decay)
    last = _rows(gcol[0].shape) == GDN_C - 1
    g_last = _each(lambda g: jnp.sum(jnp.where(last, g, 0.0), axis=0, keepdims=True), gcol)
    qd = _each(lambda q, e: q * e, qn, eg)
    kd = _each(lambda k, gl, g: k * jnp.exp(gl - g), kn, g_last, gcol)
    return (w, u, qd, kd, attn), t


def _gdn_step(state, w, u, qd, kd, attn, egl):
    v_new = _each(lambda u_, w_, s: u_ - _mm(w_, s), u, w, state)
    o_state = _each(_mm, qd, state)
    o = _each(lambda os, a, v: os + _mm(a, v), o_state, attn, v_new)
    new = _each(lambda s, e, k, v: s * e + _mm(k, v, True, False), state, egl, kd, v_new)
    return o, new


def _heads(ref, base=0, width=GDN_DH):
    return tuple(ref[:, (base + h) * GDN_DH:(base + h) * GDN_DH + width] for h in range(GDN_H))


def _cols(a, base):
    return tuple(a[:, base + h:base + h + 1] for h in range(GDN_H))


def _gated_norm(o, z, nw):
    return o * lax.rsqrt(jnp.mean(o * o, axis=-1, keepdims=True) + EPS) * nw * jax.nn.silu(z)


def _hsl(h):
    return slice(h * GDN_DH, (h + 1) * GDN_DH)


def _pad_lanes(a, width=GDN_DH):
    return jnp.concatenate([a, jnp.zeros((a.shape[0], width - a.shape[1]), a.dtype)], axis=1)


def _gdn_prep(qkv, proj, alog, dtb, name):
    S = qkv.shape[0]
    N = S // GDN_C

    def body(qkv_ref, ab_ref, al_ref, dt_ref, w_ref, u_ref, qd_ref, kd_ref, at_ref, ti_ref, gc_ref):
        ab = ab_ref[...]
        glog = _each(_gdn_glog, _cols(ab, 0), _cols(al_ref[...], 0), _cols(dt_ref[...], 0))
        dec = _each(_gdn_decay, glog)
        gcol, dmat = _each(lambda d: d[0], dec), _each(lambda d: d[1], dec)
        (w, u, qd, kd, attn), tinv = _gdn_chunk(_heads(qkv_ref), _heads(qkv_ref, GDN_H), _heads(qkv_ref, 2 * GDN_H),
                                                _cols(ab, GDN_H), gcol, dmat)
        gc = jnp.zeros((GDN_C, 128), F32)
        for h in range(GDN_H):
            w_ref[:, _hsl(h)] = w[h]
            u_ref[:, _hsl(h)] = u[h]
            qd_ref[:, _hsl(h)] = qd[h]
            kd_ref[:, _hsl(h)] = kd[h]
            at_ref[:, _hsl(h)] = _pad_lanes(attn[h])
            ti_ref[:, _hsl(h)] = _pad_lanes(tinv[h])
            gc = jnp.where(_lanes(gc.shape) == h, gcol[h], gc)
        gc_ref[...] = gc

    vec = pl.BlockSpec((1, 128), lambda i: (0, 0))
    hsp = pl.BlockSpec((GDN_C, GDN_W), lambda i: (i, 0))
    hshape = jax.ShapeDtypeStruct((S, GDN_W), F32)
    return pl.pallas_call(
        body, name=name, grid=(N,),
        in_specs=[pl.BlockSpec((GDN_C, 3 * GDN_W), lambda i: (i, 0)), pl.BlockSpec((GDN_C, 128), lambda i: (i, PAB // 128)), vec, vec],
        out_specs=[hsp] * 6 + [pl.BlockSpec((GDN_C, 128), lambda i: (i, 0))],
        out_shape=[hshape] * 6 + [jax.ShapeDtypeStruct((S, 128), F32)],
        compiler_params=_cp("parallel"),
    )(qkv, proj, alog, dtb)


def _gdn_scan(w, u, qd, kd, attn, gc, name):
    S = w.shape[0]
    N = S // GDN_C

    def body(w_ref, u_ref, qd_ref, kd_ref, at_ref, gc_ref, o_ref, st_ref, s_ref):
        @pl.when(pl.program_id(0) == 0)
        def _():
            s_ref[...] = jnp.zeros_like(s_ref)

        state = tuple(s_ref[_hsl(h), :] for h in range(GDN_H))
        egl = _each(jnp.exp, _cols(gc_ref[GDN_C - 1:GDN_C, :], 0))
        o, new = _gdn_step(state, _heads(w_ref), _heads(u_ref), _heads(qd_ref), _heads(kd_ref),
                           _heads(at_ref, width=GDN_C), egl)
        for h in range(GDN_H):
            st_ref[_hsl(h), :] = state[h]
            o_ref[:, _hsl(h)] = o[h]
            s_ref[_hsl(h), :] = new[h]

    hsp = pl.BlockSpec((GDN_C, GDN_W), lambda i: (i, 0))
    return pl.pallas_call(
        body, name=name, grid=(N,),
        in_specs=[hsp] * 5 + [pl.BlockSpec((GDN_C, 128), lambda i: (i, 0))],
        out_specs=[hsp, pl.BlockSpec((None, GDN_W, GDN_DH), lambda i: (i, 0, 0))],
        out_shape=[jax.ShapeDtypeStruct((S, GDN_W), F32), jax.ShapeDtypeStruct((N, GDN_W, GDN_DH), F32)],
        scratch_shapes=[pltpu.VMEM((GDN_W, GDN_DH), F32)],
        compiler_params=_cp("arbitrary"),
    )(w, u, qd, kd, attn, gc)


def _gdn_scan_bwd(w, u, qd, kd, attn, gc, states, o, proj, dmixed, nw, name):
    S = w.shape[0]
    N = S // GDN_C

    def body(w_ref, u_ref, qd_ref, kd_ref, at_ref, gc_ref, st_ref, o_ref, z_ref, dm_ref, nw_ref,
             dw_ref, du_ref, dqd_ref, dkd_ref, dat_ref, dgl_ref, dz_ref, gnw_ref, ds_ref):
        @pl.when(pl.program_id(0) == 0)
        def _():
            ds_ref[...] = jnp.zeros_like(ds_ref)
            gnw_ref[...] = jnp.zeros_like(gnw_ref)

        nw = nw_ref[...]
        _, vjp_n = jax.vjp(lambda o, z, w_: _each(lambda a, b: _gated_norm(a, b, w_), o, z), _heads(o_ref), _heads(z_ref), nw)
        do, dz, dnw = vjp_n(_heads(dm_ref))
        state = tuple(st_ref[_hsl(h), :] for h in range(GDN_H))
        egl = _each(jnp.exp, _cols(gc_ref[GDN_C - 1:GDN_C, :], 0))
        _, vjp_s = jax.vjp(_gdn_step, state, _heads(w_ref), _heads(u_ref), _heads(qd_ref), _heads(kd_ref),
                           _heads(at_ref, width=GDN_C), egl)
        ds, dw, du, dqd, dkd, dat, degl = vjp_s((do, tuple(ds_ref[_hsl(h), :] for h in range(GDN_H))))
        dgl = jnp.zeros((8, 128), F32)
        for h in range(GDN_H):
            dz_ref[:, _hsl(h)] = dz[h].astype(dz_ref.dtype)
            ds_ref[_hsl(h), :] = ds[h]
            dw_ref[:, _hsl(h)] = dw[h]
            du_ref[:, _hsl(h)] = du[h]
            dqd_ref[:, _hsl(h)] = dqd[h]
            dkd_ref[:, _hsl(h)] = dkd[h]
            dat_ref[:, _hsl(h)] = _pad_lanes(dat[h])
            dgl = jnp.where(_lanes(dgl.shape) == h, degl[h] * egl[h], dgl)
        dgl_ref[...] = dgl
        gnw_ref[...] += dnw

    rev = lambda i: (N - 1 - i, 0)
    hsp = pl.BlockSpec((GDN_C, GDN_W), rev)
    gsp = pl.BlockSpec((GDN_C, 128), rev)
    vec = pl.BlockSpec((1, GDN_DH), lambda i: (0, 0))
    hshape = jax.ShapeDtypeStruct((S, GDN_W), F32)
    return pl.pallas_call(
        body, name=name, grid=(N,),
        in_specs=[hsp] * 5 + [gsp, pl.BlockSpec((None, GDN_W, GDN_DH), lambda i: (N - 1 - i, 0, 0)), hsp,
                              pl.BlockSpec((GDN_C, GDN_W), lambda i: (N - 1 - i, PZ // GDN_W)), hsp, vec],
        out_specs=[hsp] * 5 + [pl.BlockSpec((8, 128), rev), hsp, vec],
        out_shape=[hshape] * 5 + [jax.ShapeDtypeStruct((N * 8, 128), F32), jax.ShapeDtypeStruct((S, GDN_W), BF16),
                                  jax.ShapeDtypeStruct((1, GDN_DH), F32)],
        scratch_shapes=[pltpu.VMEM((GDN_W, GDN_DH), F32)],
        compiler_params=_cp("arbitrary"),
    )(w, u, qd, kd, attn, gc, states, o, proj, dmixed, nw)


def _gdn_prep_bwd(qkv, proj, alog, dtb, tinv, dw, du, dqd, dkd, dat, dgl, name):
    S = qkv.shape[0]
    N = S // GDN_C

    def body(qkv_ref, ab_ref, al_ref, dt_ref, ti_ref, dw_ref, du_ref, dqd_ref, dkd_ref, dat_ref, dgl_ref,
             dqkv_ref, dab_ref, gal_ref, gdt_ref):
        @pl.when(pl.program_id(0) == 0)
        def _():
            gal_ref[...] = jnp.zeros_like(gal_ref)
            gdt_ref[...] = jnp.zeros_like(gdt_ref)

        ab = ab_ref[...]
        glog, vjp_g = jax.vjp(lambda a, al, dt: _each(_gdn_glog, a, al, dt), _cols(ab, 0), _cols(al_ref[...], 0),
                              _cols(dt_ref[...], 0))
        dec = _each(_gdn_decay, glog)
        gcol, dmat = _each(lambda d: d[0], dec), _each(lambda d: d[1], dec)
        _, vjp_c, _ = jax.vjp(functools.partial(_gdn_chunk, t_saved=_heads(ti_ref, width=GDN_C)), _heads(qkv_ref),
                              _heads(qkv_ref, GDN_H), _heads(qkv_ref, 2 * GDN_H), _cols(ab, GDN_H), gcol, dmat, has_aux=True)
        dqa, dka, dva, dbt, dgcol, dd = vjp_c((_heads(dw_ref), _heads(du_ref), _heads(dqd_ref), _heads(dkd_ref),
                                               _heads(dat_ref, width=GDN_C)))
        last = _rows(dgcol[0].shape) == GDN_C - 1
        dgcol = _each(lambda d, g: d + jnp.where(last, g, 0.0), dgcol, _cols(dgl_ref[0:1, :], 0))
        da_col, dal, ddt = vjp_g(_each(_gdn_decay_bwd, dgcol, dd))
        dab = jnp.zeros((GDN_C, 128), F32)
        gal = jnp.zeros((1, 128), F32)
        gdt = jnp.zeros((1, 128), F32)
        for h in range(GDN_H):
            dqkv_ref[:, _hsl(h)] = dqa[h]
            dqkv_ref[:, _hsl(GDN_H + h)] = dka[h]
            dqkv_ref[:, _hsl(2 * GDN_H + h)] = dva[h]
            ln = _lanes(dab.shape)
            dab = dab + jnp.where(ln == h, da_col[h], 0.0) + jnp.where(ln == GDN_H + h, dbt[h], 0.0)
            l1 = _lanes(gal.shape)
            gal = gal + jnp.where(l1 == h, dal[h], 0.0)
            gdt = gdt + jnp.where(l1 == h, ddt[h], 0.0)
        dab_ref[...] = dab.astype(dab_ref.dtype)
        gal_ref[...] += gal
        gdt_ref[...] += gdt

    vec = pl.BlockSpec((1, 128), lambda i: (0, 0))
    hsp = pl.BlockSpec((GDN_C, GDN_W), lambda i: (i, 0))
    qsp = pl.BlockSpec((GDN_C, 3 * GDN_W), lambda i: (i, 0))
    return pl.pallas_call(
        body, name=name, grid=(N,),
        in_specs=[qsp, pl.BlockSpec((GDN_C, 128), lambda i: (i, PAB // 128)), vec, vec] + [hsp] * 6
        + [pl.BlockSpec((8, 128), lambda i: (i, 0))],
        out_specs=[qsp, pl.BlockSpec((GDN_C, 128), lambda i: (i, 0)), vec, vec],
        out_shape=[jax.ShapeDtypeStruct((S, 3 * GDN_W), F32), jax.ShapeDtypeStruct((S, 128), BF16),
                   jax.ShapeDtypeStruct((1, 128), F32), jax.ShapeDtypeStruct((1, 128), F32)],
        compiler_params=_cp("arbitrary"),
    )(qkv, proj, alog, dtb, tinv, dw, du, dqd, dkd, dat, dgl)


@jax.custom_vjp
def _expm1(x):
    u = jnp.exp(x)
    lu = jnp.log(u)
    small = (u - 1.0) * x / jnp.where(u == 1.0, 1.0, lu)
    small = jnp.where(u == 1.0, x, small)
    return jnp.where(jnp.abs(x) < 0.5, small, u - 1.0)


def _expm1_fwd(x):
    return _expm1(x), jnp.exp(x)


def _expm1_bwd(ex, g):
    return (g * ex,)


_expm1.defvjp(_expm1_fwd, _expm1_bwd)


def _lru_gates(xc, wa, ba, wx, bx, lam, first):
    r = jax.nn.sigmoid(_mm(xc, wa) + ba)
    i = jax.nn.sigmoid(_mm(xc, wx) + bx)
    log_a = -LRU_C * r * jax.nn.softplus(-lam)
    mult = jnp.sqrt(-_expm1(2.0 * log_a))
    mult = jnp.where(first, 1.0, mult)
    return jnp.exp(log_a), mult * i * xc


def _scan_fwd(a, b):
    T = a.shape[0]
    rows = _rows(a.shape)
    s = 1
    while s < T:
        ok = rows >= s
        b = a * jnp.where(ok, _down(b, s), 0.0) + b
        a = a * jnp.where(ok, _down(a, s), 1.0)
        s *= 2
    return a, b


def _scan_rev(a, b):
    T = a.shape[0]
    rows = _rows(a.shape)
    s = 1
    while s < T:
        ok = rows + s < T
        b = a * jnp.where(ok, _up(b, s), 0.0) + b
        a = a * jnp.where(ok, _up(a, s), 1.0)
        s *= 2
    return b


def _bsl(j):
    return slice(j * LRU_BD, (j + 1) * LRU_BD)


def _lru_tile(S):
    return _row_tile(S, 256)


def _lru_fwd(proj, conv_w, conv_b, wa, ba, wx, bx, lam, name):
    S = proj.shape[0]
    T = _lru_tile(S)
    taps = conv_w.shape[0]
    r = T // HALO

    def body(x_ref, xp_ref, cw_ref, cb_ref, wa_ref, ba_ref, wx_ref, bx_ref, lam_ref, h_ref, carry_ref):
        i = pl.program_id(0)

        @pl.when(i == 0)
        def _():
            carry_ref[...] = jnp.zeros_like(carry_ref)

        ext = jnp.concatenate([jnp.where(i > 0, xp_ref[...], 0.0), x_ref[...]], axis=0)
        xc = _conv_rows(ext, cw_ref, taps)[HALO:] + cb_ref[...]
        first = (i * T + _rows((T, LRU_BD))) == 0
        for j in range(LRU_NB):
            a, b = _lru_gates(xc[:, _bsl(j)], wa_ref[j], ba_ref[:, _bsl(j)], wx_ref[j], bx_ref[:, _bsl(j)],
                              lam_ref[:, _bsl(j)], first=first)
            pa, hb = _scan_fwd(a, b)
            h_ref[:, _bsl(j)] = pa * carry_ref[0:1, _bsl(j)] + hb
            carry_ref[0:1, _bsl(j)] = h_ref[T - 1:T, _bsl(j)]

    vec = pl.BlockSpec((1, LRU_W), lambda i: (0, 0))
    wsp = pl.BlockSpec((LRU_NB, LRU_BD, LRU_BD), lambda i: (0, 0, 0))
    return pl.pallas_call(
        body, name=name, grid=(S // T,),
        in_specs=[pl.BlockSpec((T, LRU_W), lambda i: (i, PXR // LRU_W)),
                  pl.BlockSpec((HALO, LRU_W), lambda i: (jnp.maximum(i * r - 1, 0), PXR // LRU_W)),
                  pl.BlockSpec((taps, LRU_W), lambda i: (0, 0)), vec, wsp, vec, wsp, vec, vec],
        out_specs=pl.BlockSpec((T, LRU_W), lambda i: (i, 0)), out_shape=jax.ShapeDtypeStruct((S, LRU_W), F32),
        scratch_shapes=[pltpu.VMEM((8, LRU_W), F32)],
        compiler_params=_cp("arbitrary"),
    )(proj, proj, conv_w, conv_b.reshape(1, LRU_W), wa, ba.reshape(1, LRU_W), wx, bx.reshape(1, LRU_W), lam.reshape(1, LRU_W))


def _lru_bwd(proj, hl, dmixed, conv_w, conv_b, wa, ba, wx, bx, lam, name):
    S = proj.shape[0]
    T = _lru_tile(S)
    n = S // T
    taps = conv_w.shape[0]
    r = T // HALO
    mb = 768 // LRU_W

    def body(x_ref, xp_ref, g_ref, h_ref, hp_ref, dy_ref, cw_ref, cb_ref, wa_ref, ba_ref, wx_ref, bx_ref, lam_ref,
             dx_ref, dg_ref, gcw_ref, gcb_ref, gwa_ref, gba_ref, gwx_ref, gbx_ref, glam_ref, carry_ref, dxc_ref, nxt_ref):
        s = pl.program_id(0)
        i = n - 1 - s

        @pl.when(s == 0)
        def _():
            carry_ref[...] = jnp.zeros_like(carry_ref)
            nxt_ref[...] = jnp.zeros_like(nxt_ref)
            for ref in (gcw_ref, gcb_ref, gwa_ref, gba_ref, gwx_ref, gbx_ref, glam_ref):
                ref[...] = jnp.zeros_like(ref)

        ext = jnp.concatenate([jnp.where(i > 0, xp_ref[...], 0.0), x_ref[...]], axis=0)
        xc = _conv_rows(ext, cw_ref, taps)[HALO:] + cb_ref[...]
        rows = _rows((T, LRU_BD))
        first = (i * T + rows) == 0
        h_before = jnp.where(i > 0, hp_ref[HALO - 1:HALO, :], 0.0)
        for j in range(LRU_NB):
            sl = _bsl(j)
            (a, _), vjp_g = jax.vjp(functools.partial(_lru_gates, first=first), xc[:, sl], wa_ref[j], ba_ref[:, sl],
                                    wx_ref[j], bx_ref[:, sl], lam_ref[:, sl])
            gelu_g, vjp_a = jax.vjp(jax.nn.gelu, g_ref[:, sl])
            h = h_ref[:, sl]
            dy = dy_ref[:, sl]
            dg_ref[:, sl] = vjp_a(dy * h)[0].astype(dg_ref.dtype)
            b_rev = dy * gelu_g + jnp.where(rows == T - 1, carry_ref[0:1, sl], 0.0)
            a_rev = jnp.where(rows == T - 1, 0.0, _up(a, 1))
            dh = _scan_rev(a_rev, b_rev)
            carry_ref[:, sl] = (a * dh)[:HALO]
            h_prev = jnp.where(rows == 0, h_before[:, sl], _down(h, 1))
            dxc, dwa, dba, dwx, dbx, dlam = vjp_g((dh * h_prev, dh))
            dxc_ref[:, sl] = dxc
            gwa_ref[j] += dwa
            gwx_ref[j] += dwx
            gba_ref[:, sl] += dba
            gbx_ref[:, sl] += dbx
            glam_ref[:, sl] += dlam
        dxc = dxc_ref[...]
        d_ext = jnp.concatenate([dxc, nxt_ref[...]], axis=0)
        dx_ref[...] = _conv_t_rows(d_ext, cw_ref, taps)[:T].astype(dx_ref.dtype)
        nxt_ref[...] = dxc[:HALO]
        gcb_ref[...] += jnp.sum(dxc, axis=0, keepdims=True)
        for k in range(taps):
            gcw_ref[taps - 1 - k:taps - k, :] += jnp.sum(dxc * _down(ext, k)[HALO:], axis=0, keepdims=True)

    vec = pl.BlockSpec((1, LRU_W), lambda s: (0, 0))
    wsp = pl.BlockSpec((LRU_NB, LRU_BD, LRU_BD), lambda s: (0, 0, 0))
    cwsp = pl.BlockSpec((taps, LRU_W), lambda s: (0, 0))

    def main(cb):
        return pl.BlockSpec((T, LRU_W), lambda s: (n - 1 - s, cb))

    def prev(cb):
        return pl.BlockSpec((HALO, LRU_W), lambda s: (jnp.maximum((n - 1 - s) * r - 1, 0), cb))

    vshape = jax.ShapeDtypeStruct((1, LRU_W), F32)
    wshape = jax.ShapeDtypeStruct((LRU_NB, LRU_BD, LRU_BD), F32)
    return pl.pallas_call(
        body, name=name, grid=(n,),
        in_specs=[main(PXR // LRU_W), prev(PXR // LRU_W), main(PGR // LRU_W), main(0), prev(0), main(mb),
                  cwsp, vec, wsp, vec, wsp, vec, vec],
        out_specs=[main(0), main(0), cwsp, vec, wsp, vec, wsp, vec, vec],
        out_shape=[jax.ShapeDtypeStruct((S, LRU_W), BF16), jax.ShapeDtypeStruct((S, LRU_W), BF16),
                   jax.ShapeDtypeStruct((taps, LRU_W), F32), vshape, wshape, vshape, wshape, vshape, vshape],
        scratch_shapes=[pltpu.VMEM((8, LRU_W), F32), pltpu.VMEM((T, LRU_W), F32), pltpu.VMEM((HALO, LRU_W), F32)],
        compiler_params=_cp("arbitrary"),
    )(proj, proj, proj, hl, hl, dmixed, conv_w, conv_b.reshape(1, LRU_W), wa, ba.reshape(1, LRU_W), wx,
      bx.reshape(1, LRU_W), lam.reshape(1, LRU_W))


def _mix_out(o, proj, hl, y_pool, nw, name):
    S = o.shape[0]
    T = _row_tile(S)

    def body(o_ref, z_ref, h_ref, g_ref, p_ref, nw_ref, m_ref):
        for h in range(GDN_H):
            m_ref[:, _hsl(h)] = _gated_norm(o_ref[:, _hsl(h)], z_ref[:, _hsl(h)], nw_ref[...]).astype(m_ref.dtype)
        m_ref[:, GDN_W:GDN_W + LRU_W] = (h_ref[...] * jax.nn.gelu(g_ref[...])).astype(m_ref.dtype)
        m_ref[:, GDN_W + LRU_W:] = p_ref[...].astype(m_ref.dtype)

    row = pl.BlockSpec((T, GDN_W), lambda i: (i, 0))
    return pl.pallas_call(
        body, name=name, grid=(S // T,),
        in_specs=[row, pl.BlockSpec((T, GDN_W), lambda i: (i, PZ // GDN_W)), row,
                  pl.BlockSpec((T, LRU_W), lambda i: (i, PGR // LRU_W)), pl.BlockSpec((T, POOL_W), lambda i: (i, 0)),
                  pl.BlockSpec((1, GDN_DH), lambda i: (0, 0))],
        out_specs=pl.BlockSpec((T, D_MODEL), lambda i: (i, 0)), out_shape=jax.ShapeDtypeStruct((S, D_MODEL), BF16),
        compiler_params=_cp("parallel"),
    )(o, proj, hl, proj, y_pool, nw)


def _as2d(a):
    return a.reshape(-1, a.shape[-1])


def _ew_rows(rows, cols):
    t = rows
    while t * cols * 4 > (2 << 20) and t % 16 == 0:
        t //= 2
    return t


def _rs_rows(rows, cols, budget=2 << 20):
    t = rows
    while t * cols * 4 > budget and t % 32 == 0:
        t //= 2
    return t


def _adamw(w, g, m, v, name):
    shape = w.shape
    w2, g2, m2, v2 = _as2d(w), _as2d(g), _as2d(m), _as2d(v)
    rows, cols = w2.shape
    t = _ew_rows(rows, cols)

    def body(w_ref, g_ref, m_ref, v_ref, d_ref, nm_ref, nv_ref):
        gr = g_ref[...]
        nm = ADAM_B1 * m_ref[...] + (1.0 - ADAM_B1) * gr
        nv = ADAM_B2 * v_ref[...] + (1.0 - ADAM_B2) * (gr * gr)
        m_hat = nm / (1.0 - ADAM_B1 ** ADAM_STEP)
        v_hat = nv / (1.0 - ADAM_B2 ** ADAM_STEP)
        d_ref[...] = -ADAM_LR * (m_hat / (jnp.sqrt(v_hat) + ADAM_EPS) + ADAM_WD * w_ref[...])
        nm_ref[...] = nm
        nv_ref[...] = nv

    sp = pl.BlockSpec((t, cols), lambda i: (i, 0))
    sh = jax.ShapeDtypeStruct((rows, cols), F32)
    d, nm, nv = pl.pallas_call(body, name=name, grid=(rows // t,), in_specs=[sp] * 4, out_specs=[sp] * 3,
                               out_shape=[sh] * 3, compiler_params=_cp("parallel"))(w2, g2, m2, v2)
    return d.reshape(shape), nm.reshape(shape), nv.reshape(shape)


def _place():
    return lax.axis_index("x"), lax.axis_index("y"), lax.axis_index("c")


def _gather_weights(arrs, name):
    n = len(arrs)

    def body(*refs):
        outs = refs[n:2 * n]
        send, recv = refs[2 * n:]
        x, y, c = _place()
        s_me, s_x, s_y, s_d = 2 * x + y, 2 * (1 - x) + y, 2 * x + (1 - y), 2 * (1 - x) + (1 - y)
        xpeer, ypeer, sib = (1 - x, y, c), (x, 1 - y, c), (x, y, 1 - c)

        def rc(k, t, src, dst, to):
            return pltpu.make_async_remote_copy(src_ref=src, dst_ref=dst, send_sem=send.at[k, t], recv_sem=recv.at[k, t],
                                                device_id=to, device_id_type=MESH)

        def piece(k, s, top):
            rq = outs[k].shape[2] // 2
            return outs[k].at[s, c, pl.ds(0 if top else rq, rq)]

        sent = []

        def start(k, t, ref, to):
            cp = rc(k, t, ref, ref, to)
            cp.start()
            sent.append(cp)

        for k in range(n):
            start(k, 0, outs[k].at[s_me, c], xpeer)
            start(k, 1, outs[k].at[s_me, c], ypeer)
        for k in range(n):
            got = outs[k].at[s_x, c]
            rc(k, 0, got, got, xpeer).wait_recv()
            start(k, 2, piece(k, s_x, True), ypeer)
            start(k, 3, got, sib)
        for k in range(n):
            got = outs[k].at[s_y, c]
            rc(k, 1, got, got, ypeer).wait_recv()
            start(k, 6, piece(k, s_y, False), xpeer)
            start(k, 4, got, sib)
        for k in range(n):
            top, bottom = piece(k, s_d, True), piece(k, s_d, False)
            rc(k, 2, top, top, ypeer).wait_recv()
            rc(k, 6, bottom, bottom, xpeer).wait_recv()
            start(k, 5, outs[k].at[s_d, c], sib)
        for k in range(n):
            for t, s in ((3, s_x), (4, s_y), (5, s_d)):
                got = outs[k].at[s, 1 - c]
                rc(k, t, got, got, sib).wait_recv()
        for cp in sent:
            cp.wait_send()

    return pl.pallas_call(
        body, name=name, in_specs=[HBM] * n, out_specs=[HBM] * n,
        out_shape=[jax.ShapeDtypeStruct(a.shape, a.dtype) for a in arrs],
        input_output_aliases={k: k for k in range(n)},
        scratch_shapes=[pltpu.SemaphoreType.DMA((n, 7)), pltpu.SemaphoreType.DMA((n, 7))],
    )(*arrs)


def _share_halves(arrs, name):
    n = len(arrs)

    def body(*refs):
        outs = refs[n:2 * n]
        send, recv = refs[2 * n:]
        x, y, c = _place()
        cps = []
        for k in range(n):
            mine = outs[k].at[:, c]
            cp = pltpu.make_async_remote_copy(src_ref=mine, dst_ref=mine, send_sem=send.at[k], recv_sem=recv.at[k],
                                              device_id=(x, y, 1 - c), device_id_type=MESH)
            cp.start()
            cps.append(cp)
        for k in range(n):
            got = outs[k].at[:, 1 - c]
            pltpu.make_async_remote_copy(src_ref=got, dst_ref=got, send_sem=send.at[k], recv_sem=recv.at[k],
                                         device_id=(x, y, 1 - c), device_id_type=MESH).wait_recv()
        for cp in cps:
            cp.wait_send()

    return pl.pallas_call(
        body, name=name, in_specs=[HBM] * n, out_specs=[HBM] * n,
        out_shape=[jax.ShapeDtypeStruct(a.shape, a.dtype) for a in arrs],
        input_output_aliases={k: k for k in range(n)},
        scratch_shapes=[pltpu.SemaphoreType.DMA((n,)), pltpu.SemaphoreType.DMA((n,))],
    )(*arrs)


_REL = tuple((dx, dy, dc) for dx in (0, 1) for dy in (0, 1) for dc in (0, 1))[1:]
SEM = pl.BlockSpec(memory_space=pltpu.SEMAPHORE)
DATAFLOW = pltpu.SideEffectType.DATAFLOW_SIDE_EFFECTING


def _flip(v, d):
    return 1 - v if d else v


def _rs_direct_copies(srcs, land, send, recv):
    x, y, c = _place()
    cps = []
    for k in range(len(srcs)):
        for r, (dx, dy, dc) in enumerate(_REL):
            px, py, pc = _flip(x, dx), _flip(y, dy), _flip(c, dc)
            cps.append(pltpu.make_async_remote_copy(
                src_ref=srcs[k].at[2 * px + py, pc], dst_ref=land[k].at[r], send_sem=send.at[k * len(_REL) + r],
                recv_sem=recv.at[k * len(_REL) + r], device_id=(px, py, pc), device_id_type=MESH))
    return cps


def _rs_direct_start(grads, thru, name):
    n = len(grads)
    lands = [pltpu.with_memory_space_constraint(lax.empty((len(_REL),) + g.shape[2:], g.dtype), pltpu.HBM) for g in grads]

    def body(*refs):
        for cp in _rs_direct_copies(refs[:n], refs[n + 1:2 * n + 1], refs[2 * n + 1], refs[2 * n + 2]):
            cp.start()

    sems = pltpu.SemaphoreType.DMA((n * len(_REL),))
    keep = [pltpu.HBM(a.shape, a.dtype) for a in (*grads, thru, *lands)]
    out = pl.pallas_call(
        body, name=name, in_specs=[HBM] * (2 * n + 1), out_specs=(SEM, SEM) + (HBM,) * (2 * n + 1),
        out_shape=(sems, sems, *keep), input_output_aliases={i: 2 + i for i in range(2 * n + 1)},
        compiler_params=pltpu.CompilerParams(has_side_effects=DATAFLOW),
    )(*[pltpu.with_memory_space_constraint(a, pltpu.HBM) for a in (*grads, thru)], *lands)
    return out[0], out[1], out[2:2 + n], out[2 + n], out[3 + n:]


def _rs_direct_wait(send, recv, grads, lands, after, name):
    n = len(grads)

    def body(*refs):
        for cp in _rs_direct_copies(refs[:n], refs[n:2 * n], refs[2 * n], refs[2 * n + 1]):
            cp.wait_send()
            cp.wait_recv()

    keep = [pltpu.HBM(a.shape, a.dtype) for a in (*grads, *lands)]
    out = pl.pallas_call(
        body, name=name, in_specs=[HBM] * (2 * n) + [SEM, SEM, pl.BlockSpec(memory_space=pl.ANY)], out_specs=(HBM,) * (2 * n),
        out_shape=tuple(keep), input_output_aliases={i: i for i in range(2 * n)},
        compiler_params=pltpu.CompilerParams(has_side_effects=DATAFLOW),
    )(*grads, *lands, send, recv, after)
    return out[:n], out[n:]


_CHIPS = ((1, 0), (0, 1), (1, 1))


def _gather_ici_copies(bufs, send, recv):
    x, y, c = _place()
    cps = []
    for k in range(len(bufs)):
        mine = bufs[k].at[2 * x + y, c]
        for j, (dx, dy) in enumerate(_CHIPS):
            cps.append(pltpu.make_async_remote_copy(
                src_ref=mine, dst_ref=mine, send_sem=send.at[k * len(_CHIPS) + j], recv_sem=recv.at[k * len(_CHIPS) + j],
                device_id=(_flip(x, dx), _flip(y, dy), c), device_id_type=MESH))
    return cps


def _gather_d2d_copies(bufs, send, recv):
    x, y, c = _place()
    cps = []
    for k in range(len(bufs)):
        for j, (dx, dy) in enumerate(_CHIPS):
            got = bufs[k].at[2 * _flip(x, dx) + _flip(y, dy), c]
            cps.append(pltpu.make_async_remote_copy(
                src_ref=got, dst_ref=got, send_sem=send.at[k * len(_CHIPS) + j], recv_sem=recv.at[k * len(_CHIPS) + j],
                device_id=(x, y, 1 - c), device_id_type=MESH))
    return cps


def _copies_start(bufs, thru, copies, name):
    n = len(bufs)

    def body(*refs):
        for cp in copies(refs[:n], refs[n + 1], refs[n + 2]):
            cp.start()

    sems = pltpu.SemaphoreType.DMA((n * len(_CHIPS),))
    out = pl.pallas_call(
        body, name=name, in_specs=[HBM] * (n + 1), out_specs=(SEM, SEM) + (HBM,) * (n + 1),
        out_shape=(sems, sems, *[pltpu.HBM(a.shape, a.dtype) for a in (*bufs, thru)]),
        input_output_aliases={i: 2 + i for i in range(n + 1)},
        compiler_params=pltpu.CompilerParams(has_side_effects=DATAFLOW),
    )(*[pltpu.with_memory_space_constraint(a, pltpu.HBM) for a in (*bufs, thru)])
    return out[0], out[1], out[2:2 + n], out[2 + n]


def _copies_wait(send, recv, bufs, after, copies, name):
    n = len(bufs)

    def body(*refs):
        for cp in copies(refs[:n], refs[n], refs[n + 1]):
            cp.wait_send()
            cp.wait_recv()

    return pl.pallas_call(
        body, name=name, in_specs=[HBM] * n + [SEM, SEM, pl.BlockSpec(memory_space=pl.ANY)], out_specs=(HBM,) * n,
        out_shape=tuple(pltpu.HBM(a.shape, a.dtype) for a in bufs), input_output_aliases={i: i for i in range(n)},
        compiler_params=pltpu.CompilerParams(has_side_effects=DATAFLOW),
    )(*bufs, send, recv, after)


def _rs_direct_sum(grad, land, layer, name, into=None):
    _, _, rows, cols = grad.shape
    t = _rs_rows(rows, cols, 6 << 20)
    npieces = len(_REL) + 1

    def body(g_ref, l_ref, *rest):
        o_ref, acc_ref = rest[-2], rest[-1]
        j = pl.program_id(1)

        @pl.when(j == 0)
        def _():
            acc_ref[...] = g_ref[...].astype(F32)

        @pl.when(j > 0)
        def _():
            acc_ref[...] += l_ref[...].astype(F32)

        @pl.when(j == npieces - 1)
        def _():
            o_ref[...] = acc_ref[...]

    def mine(i, j):
        x, y, c = _place()
        return (2 * x + y, c, i, 0)

    in_specs = [pl.BlockSpec((None, None, t, cols), mine),
                pl.BlockSpec((None, t, cols), lambda i, j: (jnp.maximum(j - 1, 0), i, 0))]
    args = [grad, land]
    if into is not None:
        in_specs.append(pl.BlockSpec(memory_space=pl.ANY))
        args.append(into)
    return pl.pallas_call(
        body, name=name, grid=(rows // t, npieces), in_specs=in_specs,
        out_specs=pl.BlockSpec((None, None, t, cols), lambda i, j: (layer, lax.axis_index("c"), i, 0)),
        scratch_shapes=[pltpu.VMEM((t, cols), F32)],
        out_shape=jax.ShapeDtypeStruct((2, 2, rows, cols), F32), input_output_aliases={} if into is None else {2: 0},
        compiler_params=_cp("parallel", "arbitrary"),
    )(*args)


def _ar_copies(buf, land, send, recv):
    x, y, c = _place()
    return [pltpu.make_async_remote_copy(src_ref=buf, dst_ref=land.at[r], send_sem=send.at[r], recv_sem=recv.at[r],
                                         device_id=(_flip(x, dx), _flip(y, dy), _flip(c, dc)), device_id_type=MESH)
            for r, (dx, dy, dc) in enumerate(_REL)]


def _ar_start(buf, name):
    land = pltpu.with_memory_space_constraint(lax.empty((len(_REL),) + buf.shape, buf.dtype), pltpu.HBM)

    def body(buf_ref, land_ref, send, recv, *_):
        for cp in _ar_copies(buf_ref, land_ref, send, recv):
            cp.start()

    sems = pltpu.SemaphoreType.DMA((len(_REL),))
    return pl.pallas_call(
        body, name=name, in_specs=[HBM, HBM], out_specs=(SEM, SEM, HBM, HBM),
        out_shape=(sems, sems, pltpu.HBM(buf.shape, buf.dtype), pltpu.HBM(land.shape, land.dtype)),
        input_output_aliases={0: 2, 1: 3}, compiler_params=pltpu.CompilerParams(has_side_effects=DATAFLOW),
    )(pltpu.with_memory_space_constraint(buf, pltpu.HBM), land)


def _ar_wait(send, recv, buf, land, after, name):
    def body(buf_ref, land_ref, send_ref, recv_ref, *_):
        for cp in _ar_copies(buf_ref, land_ref, send_ref, recv_ref):
            cp.wait_send()
            cp.wait_recv()

    return pl.pallas_call(
        body, name=name, in_specs=[HBM, HBM, SEM, SEM, pl.BlockSpec(memory_space=pl.ANY)], out_specs=(HBM, HBM),
        out_shape=(pltpu.HBM(buf.shape, buf.dtype), pltpu.HBM(land.shape, land.dtype)), input_output_aliases={0: 0, 1: 1},
        compiler_params=pltpu.CompilerParams(has_side_effects=DATAFLOW),
    )(buf, land, send, recv, after)


def _ar_sum(buf, land, name):
    rows, cols = buf.shape
    t = _rs_rows(rows, cols)

    def slot(i, j):
        x, y, c = _place()
        xd, yd, cd = j // 4, (j // 2) % 2, j % 2
        rel = 4 * (x + xd - 2 * x * xd) + 2 * (y + yd - 2 * y * yd) + (c + cd - 2 * c * cd)
        return (jnp.maximum(rel - 1, 0), i, 0)

    def body(b_ref, l_ref, o_ref, acc_ref):
        j = pl.program_id(1)
        x, y, c = _place()
        val = jnp.where(j == 4 * x + 2 * y + c, b_ref[...], l_ref[...])

        @pl.when(j == 0)
        def _():
            acc_ref[...] = val

        @pl.when(j > 0)
        def _():
            acc_ref[...] += val

        @pl.when(j == len(_REL))
        def _():
            o_ref[...] = acc_ref[...]

    sp = pl.BlockSpec((t, cols), lambda i, j: (i, 0))
    return pl.pallas_call(
        body, name=name, grid=(rows // t, len(_REL) + 1), in_specs=[sp, pl.BlockSpec((None, t, cols), slot)], out_specs=sp,
        out_shape=jax.ShapeDtypeStruct((rows, cols), F32), scratch_shapes=[pltpu.VMEM((t, cols), F32)],
        compiler_params=_cp("parallel", "arbitrary"),
    )(buf, land)


def _pad128(v):
    return jnp.zeros((1, 128), F32).at[0, :v.shape[0]].set(v)


def _layer_fwd(l, x, p, hooks=None):
    hooks = hooks or {}
    h1 = _rms_fwd(x, p["norm1_w"], f"rms1_{l}")
    proj = _matmul(h1, p["w_in"], "nn", name=f"mm_in_{l}", tn=768)
    y_pool = _pool_fwd(proj, p["pool_w"], p["pool_b"], p["pool_scale"], f"pool_{l}")
    qkv = _gdn_conv_fwd(proj, p["gdn_conv_w"], f"gconv_{l}")
    alog, dtb = _pad128(p["gdn_a_log"]), _pad128(p["gdn_dt_bias"])
    gw, gu, gqd, gkd, gat, tinv, gc = _gdn_prep(qkv, proj, alog, dtb, f"gprep_{l}")
    o, states = _gdn_scan(gw, gu, gqd, gkd, gat, gc, f"gscan_{l}")
    if "mid" in hooks:
        o = hooks["mid"](o)
    hl = _lru_fwd(proj, p["lru_conv_w"], p["lru_conv_b"], p["lru_wa"], p["lru_ba"], p["lru_wx"], p["lru_bx"],
                  p["lru_lambda"], f"lru_{l}")
    mixed = _mix_out(o, proj, hl, y_pool, p["gdn_norm_w"].reshape(1, GDN_DH), f"mix_{l}")
    x2 = _matmul(mixed, p["w_out"], "nn", name=f"mm_out_{l}", res=x)
    h2 = _rms_fwd(x2, p["norm2_w"], f"rms2_{l}")
    if "ffn" in hooks:
        h2 = hooks["ffn"](h2)
    up = _matmul(h2, p["ffn_up"], "nn", name=f"mm_up_{l}", b_split=True)
    act = _ffn_act_fwd(up, p["ffn_conv_w"], f"ffn_{l}")
    if "down" in hooks:
        act = hooks["down"](act)
    x3 = _matmul(act, p["ffn_down"], "nn", name=f"mm_down_{l}", res=x2)
    saved = dict(x=x, h1=h1, proj=proj, qkv=qkv, gdn=(gw, gu, gqd, gkd, gat, gc), tinv=tinv, states=states, o=o, hl=hl, mixed=mixed,
                 x2=x2, h2=h2, up=up, act=act, alog=alog, dtb=dtb)
    return x3, saved


def _layer_bwd(l, dx3, p, s, after_ffn=None):
    g = {}
    dact = _matmul(dx3, p["ffn_down"], "nt", name=f"mm_ddown_{l}")
    g["ffn_down"] = _matmul(s["act"], dx3, "tn", name=f"mm_gdown_{l}", out_dtype=BF16)
    dup, g["ffn_conv_w"] = _ffn_act_bwd(s["up"], dact, p["ffn_conv_w"], f"ffn_b_{l}")
    dh2 = _matmul(dup, p["ffn_up"], "nt", name=f"mm_dup_{l}", b_split=True, tk=3072)
    g["ffn_up"] = _matmul(s["h2"], dup, "tn", name=f"mm_gup_{l}", b_split=True, o_split=4, tk=4096, out_dtype=BF16)
    dx2, g["norm2_w"] = _rms_bwd(s["x2"], p["norm2_w"], dh2, dx3, f"rms2_b_{l}")
    if after_ffn is not None:
        dx2 = after_ffn(dx2, g)
    dmixed = _matmul(dx2, p["w_out"], "nt", name=f"mm_dout_{l}")
    g["w_out"] = _matmul(s["mixed"], dx2, "tn", name=f"mm_gout_{l}", out_dtype=BF16)
    proj = s["proj"]
    du_pool, g["pool_w"], g["pool_b"], g["pool_scale"] = _pool_bwd(proj, dmixed, p["pool_w"], p["pool_b"], p["pool_scale"], f"pool_b_{l}")
    gw, gu, gqd, gkd, gat, gc = s["gdn"]
    dw, du, dqd, dkd, dat, dgl, dz, g["gdn_norm_w"] = _gdn_scan_bwd(
        gw, gu, gqd, gkd, gat, gc, s["states"], s["o"], proj, dmixed, p["gdn_norm_w"].reshape(1, GDN_DH), f"gscan_b_{l}")
    dqkv, dab, gal, gdt = _gdn_prep_bwd(s["qkv"], proj, s["alog"], s["dtb"], s["tinv"], dw, du, dqd, dkd, dat, dgl, f"gprep_b_{l}")
    g["gdn_a_log"], g["gdn_dt_bias"] = gal[0, :GDN_H], gdt[0, :GDN_H]
    dpre, g["gdn_conv_w"] = _gdn_conv_bwd(proj, dqkv, p["gdn_conv_w"], f"gconv_b_{l}")
    (dxr, dgr, g["lru_conv_w"], g["lru_conv_b"], g["lru_wa"], g["lru_ba"], g["lru_wx"], g["lru_bx"], g["lru_lambda"]) = _lru_bwd(
        proj, s["hl"], dmixed, p["lru_conv_w"], p["lru_conv_b"], p["lru_wa"], p["lru_ba"], p["lru_wx"], p["lru_bx"],
        p["lru_lambda"], f"lru_b_{l}")
    S = proj.shape[0]
    dproj = jnp.concatenate([dpre, dz, dxr, dgr, du_pool, dab, jnp.zeros((S, PCOLS - PAB - 128), BF16)], axis=1)
    dh1 = _matmul(dproj, p["w_in"], "nt", name=f"mm_din_{l}", tk=1792)
    g["w_in"] = _matmul(s["h1"], dproj, "tn", name=f"mm_gin_{l}", tn=768, tk=4096, out_dtype=BF16)
    dx, g["norm1_w"] = _rms_bwd(s["x"], p["norm1_w"], dh1, dx2, f"rms1_b_{l}")
    return dx, g


_IN_PERM = ((512, 3584), (3596, 5132), (0, 512), (3584, 3596))


def _w_in_to_proj(w):
    parts = [w[:, a:b] for a, b in _IN_PERM]
    return jnp.concatenate(parts + [jnp.zeros((w.shape[0], PCOLS - IN_COLS), w.dtype)], axis=1)


def _proj_to_w_in(g):
    return jnp.concatenate([g[:, PPOOL:PPOOL + 512], g[:, 0:3072], g[:, PAB:PAB + 12], g[:, 3072:PPOOL]], axis=1)


def _rows_to_mixed(w):
    return jnp.concatenate([w[512:], w[:512]], axis=0)


def _mixed_to_rows(g):
    return jnp.concatenate([g[1536:], g[:1536]], axis=0)


SMALL_SHARDED = ("gdn_conv_w", "lru_conv_w", "ffn_conv_w")
BIG = ("w_in", "w_out", "ffn_up", "ffn_down")
SMALL_REPLICATED = ("norm1_w", "pool_w", "pool_b", "pool_scale", "gdn_a_log", "gdn_dt_bias", "gdn_norm_w", "lru_conv_b",
                    "lru_wa", "lru_ba", "lru_wx", "lru_bx", "lru_lambda", "norm2_w")
WEIGHTS = ("norm1_w", "w_in", "pool_w", "pool_b", "pool_scale", "gdn_conv_w", "gdn_a_log", "gdn_dt_bias", "gdn_norm_w",
           "lru_conv_w", "lru_conv_b", "lru_wa", "lru_ba", "lru_wx", "lru_bx", "lru_lambda", "w_out", "norm2_w", "ffn_up",
           "ffn_conv_w", "ffn_down", "final_norm_w")
FLAT_COLS = 1024


def _pack(arrs):
    flat = jnp.concatenate([a.reshape(-1) for a in arrs])
    rows = -(-flat.shape[0] // (8 * FLAT_COLS)) * 8
    return jnp.pad(flat, (0, rows * FLAT_COLS - flat.shape[0])).reshape(rows, FLAT_COLS)


def _unpack(buf, like):
    flat = buf.reshape(-1)
    out, off = [], 0
    for a in like:
        size = 1
        for d in a.shape:
            size *= d
        out.append(flat[off:off + size].reshape(a.shape))
        off += size
    return out


def kernel(x, norm1_w, w_in, pool_w, pool_b, pool_scale, gdn_conv_w, gdn_a_log, gdn_dt_bias, gdn_norm_w, lru_conv_w, lru_conv_b, lru_wa, lru_ba, lru_wx, lru_bx, lru_lambda, w_out, norm2_w, ffn_up, ffn_conv_w, ffn_down, final_norm_w, loss_target, m_norm1_w, m_w_in, m_pool_w, m_pool_b, m_pool_scale, m_gdn_conv_w, m_gdn_a_log, m_gdn_dt_bias, m_gdn_norm_w, m_lru_conv_w, m_lru_conv_b, m_lru_wa, m_lru_ba, m_lru_wx, m_lru_bx, m_lru_lambda, m_w_out, m_norm2_w, m_ffn_up, m_ffn_conv_w, m_ffn_down, m_final_norm_w, v_norm1_w, v_w_in, v_pool_w, v_pool_b, v_pool_scale, v_gdn_conv_w, v_gdn_a_log, v_gdn_dt_bias, v_gdn_norm_w, v_lru_conv_w, v_lru_conv_b, v_lru_wa, v_lru_ba, v_lru_wx, v_lru_bx, v_lru_lambda, v_w_out, v_norm2_w, v_ffn_up, v_ffn_conv_w, v_ffn_down, v_final_norm_w):
    W = dict(norm1_w=norm1_w, w_in=w_in, pool_w=pool_w, pool_b=pool_b, pool_scale=pool_scale, gdn_conv_w=gdn_conv_w,
             gdn_a_log=gdn_a_log, gdn_dt_bias=gdn_dt_bias, gdn_norm_w=gdn_norm_w, lru_conv_w=lru_conv_w, lru_conv_b=lru_conv_b,
             lru_wa=lru_wa, lru_ba=lru_ba, lru_wx=lru_wx, lru_bx=lru_bx, lru_lambda=lru_lambda, w_out=w_out, norm2_w=norm2_w,
             ffn_up=ffn_up, ffn_conv_w=ffn_conv_w, ffn_down=ffn_down, final_norm_w=final_norm_w)
    M = dict(norm1_w=m_norm1_w, w_in=m_w_in, pool_w=m_pool_w, pool_b=m_pool_b, pool_scale=m_pool_scale, gdn_conv_w=m_gdn_conv_w,
             gdn_a_log=m_gdn_a_log, gdn_dt_bias=m_gdn_dt_bias, gdn_norm_w=m_gdn_norm_w, lru_conv_w=m_lru_conv_w,
             lru_conv_b=m_lru_conv_b, lru_wa=m_lru_wa, lru_ba=m_lru_ba, lru_wx=m_lru_wx, lru_bx=m_lru_bx, lru_lambda=m_lru_lambda,
             w_out=m_w_out, norm2_w=m_norm2_w, ffn_up=m_ffn_up, ffn_conv_w=m_ffn_conv_w, ffn_down=m_ffn_down,
             final_norm_w=m_final_norm_w)
    V = dict(norm1_w=v_norm1_w, w_in=v_w_in, pool_w=v_pool_w, pool_b=v_pool_b, pool_scale=v_pool_scale, gdn_conv_w=v_gdn_conv_w,
             gdn_a_log=v_gdn_a_log, gdn_dt_bias=v_gdn_dt_bias, gdn_norm_w=v_gdn_norm_w, lru_conv_w=v_lru_conv_w,
             lru_conv_b=v_lru_conv_b, lru_wa=v_lru_wa, lru_ba=v_lru_ba, lru_wx=v_lru_wx, lru_bx=v_lru_bx, lru_lambda=v_lru_lambda,
             w_out=v_w_out, norm2_w=v_norm2_w, ffn_up=v_ffn_up, ffn_conv_w=v_ffn_conv_w, ffn_down=v_ffn_down,
             final_norm_w=v_final_norm_w)
    S = x.shape[1]
    xs = x.reshape(S, D_MODEL)
    tgt = loss_target.reshape(S, D_MODEL)
    mx, my, mc = _place()
    shard = 2 * mx + my

    small_sh = jnp.concatenate([W[k].reshape(N_LAYERS, -1) for k in SMALL_SHARDED], axis=1)
    n_small = small_sh.shape[1]
    pad = -n_small % 1024
    small_sh = jnp.pad(small_sh, ((0, 0), (0, pad))).reshape(N_LAYERS, -1, 1024)

    def own_slots(l):
        out = []
        for w in [W[k][l].astype(BF16) for k in BIG] + [small_sh[l]]:
            buf = lax.dynamic_update_slice(lax.empty((4,) + w.shape, w.dtype), w[None], (shard,) + (0,) * w.ndim)
            out.append(buf.reshape(4, 2, w.shape[0] // 2, w.shape[1]))
        return out

    def whole(g):
        return g.reshape(4, 2 * g.shape[2], g.shape[3])

    def mixer_params(l, g_in, g_out, g_small):
        p = {k: W[k][l] for k in SMALL_REPLICATED}
        g_in = whole(g_in)
        p["w_in"] = _w_in_to_proj(jnp.transpose(g_in, (1, 0, 2)).reshape(g_in.shape[1], IN_COLS))
        p["w_out"] = _rows_to_mixed(whole(g_out).reshape(D_MODEL, D_MODEL))
        g_small = whole(g_small).reshape(4, -1)[:, :n_small]
        off = 0
        for k in SMALL_SHARDED:
            taps, width = W[k].shape[1], W[k].shape[2]
            piece = g_small[:, off:off + taps * width].reshape(4, taps, width)
            p[k] = jnp.transpose(piece, (1, 0, 2)).reshape(taps, 4 * width)
            off += taps * width
        return p

    def ffn_params(g_up, g_down):
        return dict(ffn_up=whole(g_up), ffn_down=whole(g_down).reshape(D_FF, D_MODEL))

    layers, saved = [None] * N_LAYERS, [None] * N_LAYERS
    s0 = own_slots(0)
    g_in0, g_out0, g_small0 = _gather_weights([s0[0], s0[1], s0[4]], "gather_weights")
    f_send, f_recv, ffn0, g_in0 = _copies_start(s0[2:4], g_in0, _gather_ici_copies, "gather_ffn0_ici_start")
    l_send, l_recv, bufs1, g_in0 = _copies_start(own_slots(1), g_in0, _gather_ici_copies, "gather_l1_ici_start")
    layers[0] = mixer_params(0, g_in0, g_out0, g_small0)
    stage = {}

    def mid(o):
        bufs = _copies_wait(f_send, f_recv, ffn0, o, _gather_ici_copies, "gather_ffn0_ici_wait")
        stage["ffn0"] = _copies_start(bufs, o, _gather_d2d_copies, "gather_ffn0_d2d_start")
        return stage["ffn0"][3]

    def ffn(h2):
        send, recv, bufs, _ = stage["ffn0"]
        layers[0].update(ffn_params(*_copies_wait(send, recv, bufs, h2, _gather_d2d_copies, "gather_ffn0_d2d_wait")))
        return h2

    def down(act):
        bufs = _copies_wait(l_send, l_recv, bufs1, act, _gather_ici_copies, "gather_l1_ici_wait")
        stage["l1"] = _copies_start(bufs, act, _gather_d2d_copies, "gather_l1_d2d_start")
        return stage["l1"][3]

    h, saved[0] = _layer_fwd(0, xs, layers[0], dict(mid=mid, ffn=ffn, down=down))
    send, recv, bufs, _ = stage["l1"]
    g1 = _copies_wait(send, recv, bufs, h, _gather_d2d_copies, "gather_l1_d2d_wait")
    layers[1] = {**mixer_params(1, g1[0], g1[1], g1[4]), **ffn_params(g1[2], g1[3])}
    h, saved[1] = _layer_fwd(1, h, layers[1])
    loss_part, dh, g_final = _loss_head(h, final_norm_w, tgt, "loss_head")

    def big_partials(g_layer, names=BIG):
        out = []
        for k in names:
            g = g_layer[k]
            if k == "w_in":
                g = _proj_to_w_in(g)
                g = jnp.transpose(g.reshape(g.shape[0], 4, IN_COLS // 4), (1, 0, 2))
            elif k == "w_out":
                g = _mixed_to_rows(g).reshape(4, D_MODEL // 4, D_MODEL)
            elif k == "ffn_down":
                g = g.reshape(4, D_FF // 4, D_MODEL)
            out.append(g.reshape(4, 2, g.shape[1] // 2, g.shape[2]))
        return out

    FFN, MIX = ("ffn_up", "ffn_down"), ("w_in", "w_out")
    grads = [None] * N_LAYERS
    dh, grads[1] = _layer_bwd(1, dh, layers[1], saved[1])
    send1, recv1, part1, dh, lands1 = _rs_direct_start(big_partials(grads[1]), dh, "rs_direct_start_1")
    sent0 = []

    def after_ffn(dx2, g):
        send0, recv0, part0, dx2, lands0 = _rs_direct_start(big_partials(g, FFN), dx2, "rs_direct_start_0")
        sent0.extend([send0, recv0, part0, lands0])
        return dx2

    dh, grads[0] = _layer_bwd(0, dh, layers[0], saved[0], after_ffn)

    small_names = SMALL_REPLICATED + SMALL_SHARDED
    small_list = [jnp.stack([grads[l][k].reshape(W[k].shape[1:]) if k in SMALL_REPLICATED else grads[l][k] for l in range(N_LAYERS)])
                  for k in small_names]
    small_list += [g_final.reshape(D_MODEL), loss_part[0, 0:1]]
    a_send, a_recv, packed, a_land = _ar_start(_pack(small_list), "ar_start")

    sendm, recvm, partm, dh, landsm = _rs_direct_start(big_partials(grads[0], MIX), dh, "rs_direct_start_0m")
    part1, lands1 = _rs_direct_wait(send1, recv1, part1, lands1, dh, "rs_direct_wait_1")
    part0, lands0 = _rs_direct_wait(*sent0, dh, "rs_direct_wait_0")
    red = {k: _rs_direct_sum(g, ld, 1, f"rs_sum_1_{k}") for k, g, ld in zip(BIG, part1, lands1)}
    for k, g, ld in zip(FFN, part0, lands0):
        red[k] = _rs_direct_sum(g, ld, 0, f"rs_sum_0_{k}", into=red[k])
    G, DELTA, NM, NV = {}, {}, {}, {}

    def update_big(names, shared):
        for k, r in zip(names, shared):
            G[k] = r.reshape(N_LAYERS, 2 * r.shape[2], r.shape[3])
            DELTA[k], NM[k], NV[k] = _adamw(W[k], G[k], M[k], V[k], f"adam_{k}")

    update_big(FFN, _share_halves([red[k] for k in FFN], "rs_share_ffn"))
    partm, landsm = _rs_direct_wait(sendm, recvm, partm, landsm, DELTA[FFN[-1]], "rs_direct_wait_0m")
    for k, g, ld in zip(MIX, partm, landsm):
        red[k] = _rs_direct_sum(g, ld, 0, f"rs_sum_0_{k}", into=red[k])
    update_big(MIX, _share_halves([red[k] for k in MIX], "rs_share_mix"))
    grad_x = dh.reshape(x.shape)

    packed, a_land = _ar_wait(a_send, a_recv, packed, a_land, G[MIX[-1]], "ar_wait")
    reduced = _unpack(_ar_sum(packed, a_land, "ar_sum"), small_list)
    small_g = dict(zip(small_names, reduced[:len(small_names)]))
    small_g["final_norm_w"] = reduced[-2]
    loss = reduced[-1][0]
    for k in SMALL_SHARDED:
        width = W[k].shape[2]
        small_g[k] = lax.dynamic_slice_in_dim(small_g[k], shard * width, width, axis=2)

    small_all = small_names + ("final_norm_w",)
    dl, nm, nv = _adamw(_pack([W[k] for k in small_all]), _pack([small_g[k] for k in small_all]),
                        _pack([M[k] for k in small_all]), _pack([V[k] for k in small_all]), "adam_small")
    like = [W[k] for k in small_all]
    for k, d_, m_, v_ in zip(small_all, _unpack(dl, like), _unpack(nm, like), _unpack(nv, like)):
        G[k], DELTA[k], NM[k], NV[k] = small_g[k], d_, m_, v_

    return (loss, grad_x, *[G[k] for k in WEIGHTS], *[DELTA[k] for k in WEIGHTS], *[NM[k] for k in WEIGHTS],
            *[NV[k] for k in WEIGHTS])
```

```python
import functools

import jax
import jax.numpy as jnp
from jax import lax
from jax.experimental import pallas as pl
from jax.experimental.pallas import tpu as pltpu

F32 = jnp.float32
BF16 = jnp.bfloat16
_MXU = jnp.bfloat16

D_MODEL = 2048
N_LAYERS = 2
POOL_W = 512
POOL_G = 4
POOL_GD = 128
POOL_WINDOWS = (2, 4, 8, 16)
POOL_HALO = 16
GDN_W = 768
GDN_H = 6
GDN_DH = 128
GDN_C = 64
LRU_W = 768
LRU_NB = 6
LRU_BD = 128
LRU_C = 8.0
D_FF = 6144
EPS = 1e-6
IN_COLS = 5132
HALO = 8

PQ, PK, PV, PZ, PXR, PGR, PPOOL, PAB, PCOLS = 0, 768, 1536, 2304, 3072, 3840, 4608, 5120, 5376
CB = 768

ADAM_LR = 0.001
ADAM_B1 = 0.9
ADAM_B2 = 0.999
ADAM_EPS = 1e-08
ADAM_WD = 0.01
ADAM_STEP = 10

VMEM_LIMIT = 56 * 1024 * 1024
MESH = pl.DeviceIdType.MESH
HBM = pl.BlockSpec(memory_space=pltpu.HBM)


def _cp(*sem):
    return pltpu.CompilerParams(dimension_semantics=sem, vmem_limit_bytes=VMEM_LIMIT)


def _dg(a, b, ta, tb):
    dims = (((0 if ta else 1,), (1 if tb else 0,)), ((), ()))
    return lax.dot_general(a, b, dims, preferred_element_type=F32)


def _split2(a):
    hi = a.astype(BF16)
    lo = (a - hi.astype(F32)).astype(BF16)
    return hi, lo


def _mm_raw(a, b, ta, tb, hi):
    if _MXU == F32:
        return _dg(a, b, ta, tb)
    if not hi:
        return _dg(a.astype(_MXU), b.astype(_MXU), ta, tb)
    a1, a2 = _split2(a)
    b1, b2 = _split2(b)
    return _dg(a1, b1, ta, tb) + (_dg(a1, b2, ta, tb) + _dg(a2, b1, ta, tb))


@functools.partial(jax.custom_vjp, nondiff_argnums=(2, 3, 4))
def _mm(a, b, ta=False, tb=False, hi=False):
    return _mm_raw(a, b, ta, tb, hi)


def _mm_fwd(a, b, ta, tb, hi):
    return _mm_raw(a, b, ta, tb, hi), (a, b)


def _mm_bwd(ta, tb, hi, res, dc):
    a, b = res
    da = _mm(b, dc, tb, True, hi) if ta else _mm(dc, b, False, not tb, hi)
    db = _mm(dc, a, True, ta, hi) if tb else _mm(a, dc, not ta, False, hi)
    return da, db


_mm.defvjp(_mm_fwd, _mm_bwd)


def _mm01(m01, x):
    if _MXU == F32:
        return _dg(m01, x, False, False)
    m = m01.astype(BF16)
    x1 = x.astype(BF16)
    r = x - x1.astype(F32)
    x2 = r.astype(BF16)
    x3 = (r - x2.astype(F32)).astype(BF16)
    return _dg(m, x1, False, False) + (_dg(m, x2, False, False) + _dg(m, x3, False, False))


def _down(x, k):
    return x if k == 0 else pltpu.roll(x, k, 0)


def _up(x, k):
    return x if k == 0 else pltpu.roll(x, x.shape[0] - k, 0)


def _rows(shape):
    return lax.broadcasted_iota(jnp.int32, shape, 0)


def _lanes(shape):
    return lax.broadcasted_iota(jnp.int32, shape, 1)


def _matmul(a, b, mode, *, name, res=None, tm=1024, tn=1024, tk=2048, b_split=False, o_split=0, out_dtype=F32):
    ta, tb = mode == "tn", mode == "nt"
    a_split = a.ndim == 3
    if a_split:
        assert not ta
        M, K = a.shape[1], a.shape[0] * a.shape[2]
        tk = min(tk, a.shape[2])
    elif ta:
        K, M = a.shape
    else:
        M, K = a.shape
    if b_split:
        ns = b.shape[0]
        N = b.shape[1] if tb else ns * b.shape[2]
    else:
        N = b.shape[0] if tb else b.shape[1]
    tm, tn, tk = min(tm, M), min(tn, N), min(tk, K)
    if b_split:
        per = b.shape[2]
        if tb:
            tk = min(tk, per)
        else:
            tn = min(tn, per)
    if o_split:
        tn = min(tn, N // o_split)
    assert M % tm == 0 and N % tn == 0 and K % tk == 0, (name, M, N, K, tm, tn, tk)
    nk = K // tk
    if a_split:
        ka = a.shape[2] // tk
        a_spec = pl.BlockSpec((None, tm, tk), lambda i, j, k: (k // ka, i, k % ka))
    else:
        a_spec = pl.BlockSpec((tk, tm), lambda i, j, k: (k, i)) if ta else pl.BlockSpec((tm, tk), lambda i, j, k: (i, k))
    if not b_split:
        b_spec = pl.BlockSpec((tn, tk), lambda i, j, k: (j, k)) if tb else pl.BlockSpec((tk, tn), lambda i, j, k: (k, j))
    elif tb:
        kb = per // tk
        b_spec = pl.BlockSpec((None, tn, tk), lambda i, j, k: (k // kb, j, k % kb))
    else:
        nb = per // tn
        b_spec = pl.BlockSpec((None, tk, tn), lambda i, j, k: (j // nb, k, j % nb))
    if o_split:
        ob = (N // o_split) // tn
        out_shape = jax.ShapeDtypeStruct((o_split, M, N // o_split), out_dtype)
        o_spec = pl.BlockSpec((None, tm, tn), lambda i, j, k: (j // ob, i, j % ob))
    else:
        out_shape = jax.ShapeDtypeStruct((M, N), out_dtype)
        o_spec = pl.BlockSpec((tm, tn), lambda i, j, k: (i, j))
    in_specs = [a_spec, b_spec]
    args = [a, b]
    if res is not None:
        in_specs.append(pl.BlockSpec((tm, tn), lambda i, j, k: (i, j)))
        args.append(res)
    use_acc = nk > 1 and out_dtype != F32

    def body(*refs):
        a_ref, b_ref = refs[0], refs[1]
        o_ref = refs[2 + (res is not None)]
        acc_ref = refs[-1] if use_acc else o_ref
        p = _dg(a_ref[...].astype(_MXU), b_ref[...].astype(_MXU), ta, tb)
        first = p + refs[2][...] if res is not None else p
        if nk == 1:
            o_ref[...] = first.astype(o_ref.dtype)
        else:
            k = pl.program_id(2)

            @pl.when(k == 0)
            def _():
                acc_ref[...] = first

            @pl.when(k > 0)
            def _():
                acc_ref[...] += p

            if use_acc:
                @pl.when(k == nk - 1)
                def _():
                    o_ref[...] = acc_ref[...].astype(o_ref.dtype)

    return pl.pallas_call(
        body, name=name, grid=(M // tm, N // tn, nk), in_specs=in_specs, out_specs=o_spec, out_shape=out_shape,
        scratch_shapes=[pltpu.VMEM((tm, tn), F32)] if use_acc else [],
        compiler_params=_cp("parallel", "parallel", "arbitrary"),
    )(*args)


def _rms(x, w):
    return x * lax.rsqrt(jnp.mean(x * x, axis=-1, keepdims=True) + EPS) * w


def _row_tile(S, t=512):
    t = min(t, S)
    assert S % t == 0
    return t


def _rms_fwd(x, w, name):
    S, D = x.shape
    T = _row_tile(S)

    def body(x_ref, w_ref, o_ref):
        o_ref[...] = _rms(x_ref[...], w_ref[...]).astype(o_ref.dtype)

    return pl.pallas_call(
        body, name=name, grid=(S // T,),
        in_specs=[pl.BlockSpec((T, D), lambda i: (i, 0)), pl.BlockSpec((1, D), lambda i: (0, 0))],
        out_specs=pl.BlockSpec((T, D), lambda i: (i, 0)), out_shape=jax.ShapeDtypeStruct((S, D), BF16),
        compiler_params=_cp("parallel"),
    )(x, w.reshape(1, D))


def _rms_bwd(x, w, dh, dres, name):
    S, D = x.shape
    T = _row_tile(S)

    def body(x_ref, w_ref, dh_ref, dr_ref, dx_ref, gw_ref):
        _, vjp = jax.vjp(_rms, x_ref[...], w_ref[...])
        dx, dw = vjp(dh_ref[...])
        dx_ref[...] = dr_ref[...] + dx

        @pl.when(pl.program_id(0) == 0)
        def _():
            gw_ref[...] = jnp.zeros_like(gw_ref)

        gw_ref[...] += dw

    row = pl.BlockSpec((T, D), lambda i: (i, 0))
    vec = pl.BlockSpec((1, D), lambda i: (0, 0))
    return pl.pallas_call(
        body, name=name, grid=(S // T,), in_specs=[row, vec, row, row], out_specs=[row, vec],
        out_shape=[jax.ShapeDtypeStruct((S, D), F32), jax.ShapeDtypeStruct((1, D), F32)],
        compiler_params=_cp("arbitrary"),
    )(x, w.reshape(1, D), dh, dres)


def _loss_head(x, w, tgt, name):
    S, D = x.shape
    T = _row_tile(S)

    def body(x_ref, w_ref, t_ref, l_ref, dx_ref, gw_ref):
        y, vjp = jax.vjp(_rms, x_ref[...], w_ref[...])
        err = y - t_ref[...]
        part = 0.5 * jnp.sum(jnp.mean(err * err, axis=-1, keepdims=True), axis=0, keepdims=True)
        dx, dw = vjp(err * (1.0 / D))
        dx_ref[...] = dx

        @pl.when(pl.program_id(0) == 0)
        def _():
            gw_ref[...] = jnp.zeros_like(gw_ref)
            l_ref[...] = jnp.zeros_like(l_ref)

        gw_ref[...] += dw
        l_ref[...] += jnp.broadcast_to(part, l_ref.shape)

    row = pl.BlockSpec((T, D), lambda i: (i, 0))
    vec = pl.BlockSpec((1, D), lambda i: (0, 0))
    return pl.pallas_call(
        body, name=name, grid=(S // T,), in_specs=[row, vec, row],
        out_specs=[pl.BlockSpec((8, 128), lambda i: (0, 0)), row, vec],
        out_shape=[jax.ShapeDtypeStruct((8, 128), F32), jax.ShapeDtypeStruct((S, D), F32), jax.ShapeDtypeStruct((1, D), F32)],
        compiler_params=_cp("arbitrary"),
    )(x, w.reshape(1, D), tgt)


def _by_group(shape, vals):
    g = _lanes(shape) // POOL_GD
    out = vals[-1]
    for k in range(len(vals) - 2, -1, -1):
        out = jnp.where(g == k, vals[k], out)
    return out


def _pool_d(prev, u, t0):
    ext = jnp.concatenate([prev, u], axis=0)
    s2 = ext + _down(ext, 1)
    s4 = s2 + _down(s2, 2)
    s8 = s4 + _down(s4, 4)
    s16 = s8 + _down(s8, 8)
    ssel = _by_group(ext.shape, [s2, s4, s8, s16])[POOL_HALO:]
    win = _by_group(u.shape, [jnp.int32(w) for w in POOL_WINDOWS])
    cnt = jnp.minimum(t0 + _rows(u.shape) + 1, win).astype(F32)
    return ssel / cnt - u


def _pool_lin(d, w_ref, b):
    ys = [_mm(d[:, g * POOL_GD:(g + 1) * POOL_GD], w_ref[g]) for g in range(POOL_G)]
    return jnp.concatenate(ys, axis=1) + b


def _pool_fwd(proj, w, b, scale, name):
    S = proj.shape[0]
    T = _row_tile(S)
    r = T // POOL_HALO
    cb = PPOOL // POOL_W

    def body(u_ref, up_ref, w_ref, b_ref, sc_ref, y_ref):
        i = pl.program_id(0)
        prev = jnp.where(i > 0, up_ref[...], 0.0)
        d = _pool_d(prev, u_ref[...], i * T)
        y_ref[...] = _pool_lin(d, w_ref, b_ref[...]) * sc_ref[...]

    vec = pl.BlockSpec((1, POOL_W), lambda i: (0, 0))
    return pl.pallas_call(
        body, name=name, grid=(S // T,),
        in_specs=[pl.BlockSpec((T, POOL_W), lambda i: (i, cb)),
                  pl.BlockSpec((POOL_HALO, POOL_W), lambda i: (jnp.maximum(i * r - 1, 0), cb)),
                  pl.BlockSpec((POOL_G, POOL_GD, POOL_GD), lambda i: (0, 0, 0)), vec, vec],
        out_specs=pl.BlockSpec((T, POOL_W), lambda i: (i, 0)), out_shape=jax.ShapeDtypeStruct((S, POOL_W), F32),
        compiler_params=_cp("parallel"),
    )(proj, proj, w, b.reshape(1, POOL_W), scale.reshape(1, POOL_W))


def _pool_bwd(proj, dmixed, w, b, scale, name):
    S = proj.shape[0]
    T = _row_tile(S)
    n = S // T
    r = T // POOL_HALO
    cb = PPOOL // POOL_W
    mb = 1536 // POOL_W

    def body(u_ref, up_ref, dy_ref, dyn_ref, w_ref, b_ref, sc_ref, du_ref, gw_ref, gb_ref, gs_ref):
        i = pl.program_id(0)
        sc = sc_ref[...]
        dy = dy_ref[...]
        dy_ext = jnp.concatenate([dy, jnp.where(i < n - 1, dyn_ref[...], 0.0)], axis=0)
        dyl = dy_ext * sc
        dd = jnp.concatenate(
            [_mm(dyl[:, g * POOL_GD:(g + 1) * POOL_GD], w_ref[g], False, True) for g in range(POOL_G)], axis=1)
        t_ext = i * T + _rows(dd.shape)
        win = _by_group(dd.shape, [jnp.int32(v) for v in POOL_WINDOWS])
        cnt = jnp.minimum(t_ext + 1, win).astype(F32)
        e = jnp.where(t_ext < S, dd / cnt, 0.0)
        f2 = e + _up(e, 1)
        f4 = f2 + _up(f2, 2)
        f8 = f4 + _up(f4, 4)
        f16 = f8 + _up(f8, 8)
        du = (_by_group(dd.shape, [f2, f4, f8, f16]) - dd)[:T]
        du_ref[...] = du.astype(du_ref.dtype)

        prev = jnp.where(i > 0, up_ref[...], 0.0)
        d = _pool_d(prev, u_ref[...], i * T)
        ylin = _pool_lin(d, w_ref, b_ref[...])
        dyl_m = dy * sc

        @pl.when(i == 0)
        def _():
            gw_ref[...] = jnp.zeros_like(gw_ref)
            gb_ref[...] = jnp.zeros_like(gb_ref)
            gs_ref[...] = jnp.zeros_like(gs_ref)

        gs_ref[...] += jnp.sum(dy * ylin, axis=0, keepdims=True)
        gb_ref[...] += jnp.sum(dyl_m, axis=0, keepdims=True)
        for g in range(POOL_G):
            sl = slice(g * POOL_GD, (g + 1) * POOL_GD)
            gw_ref[g] += _mm(d[:, sl], dyl_m[:, sl], True, False)

    vec = pl.BlockSpec((1, POOL_W), lambda i: (0, 0))
    wsp = pl.BlockSpec((POOL_G, POOL_GD, POOL_GD), lambda i: (0, 0, 0))
    nh = S // POOL_HALO
    return pl.pallas_call(
        body, name=name, grid=(n,),
        in_specs=[pl.BlockSpec((T, POOL_W), lambda i: (i, cb)),
                  pl.BlockSpec((POOL_HALO, POOL_W), lambda i: (jnp.maximum(i * r - 1, 0), cb)),
                  pl.BlockSpec((T, POOL_W), lambda i: (i, mb)),
                  pl.BlockSpec((POOL_HALO, POOL_W), lambda i: (jnp.minimum((i + 1) * r, nh - 1), mb)),
                  wsp, vec, vec],
        out_specs=[pl.BlockSpec((T, POOL_W), lambda i: (i, 0)), wsp, vec, vec],
        out_shape=[jax.ShapeDtypeStruct((S, POOL_W), BF16), jax.ShapeDtypeStruct((POOL_G, POOL_GD, POOL_GD), F32),
                   jax.ShapeDtypeStruct((1, POOL_W), F32), jax.ShapeDtypeStruct((1, POOL_W), F32)],
        compiler_params=_cp("arbitrary"),
    )(proj, proj, dmixed, dmixed, w, b.reshape(1, POOL_W), scale.reshape(1, POOL_W))


def _conv_rows(ext, w_ref, taps):
    acc = w_ref[taps - 1:taps, :] * ext
    for k in range(1, taps):
        acc = acc + w_ref[taps - 1 - k:taps - k, :] * _down(ext, k)
    return acc


def _conv_t_rows(dc, w_ref, taps):
    acc = w_ref[taps - 1:taps, :] * dc
    for k in range(1, taps):
        acc = acc + w_ref[taps - 1 - k:taps - k, :] * _up(dc, k)
    return acc


def _conv_specs(T, S, ncb0, with_next):
    r = T // HALO
    nh = S // HALO
    main = pl.BlockSpec((T, CB), lambda j, i: (i, j + ncb0))
    prev = pl.BlockSpec((HALO, CB), lambda j, i: (jnp.maximum(i * r - 1, 0), j + ncb0))
    nxt = pl.BlockSpec((HALO, CB), lambda j, i: (jnp.minimum((i + 1) * r, nh - 1), j + ncb0))
    return (main, prev, nxt) if with_next else (main, prev)


def _gdn_conv_fwd(proj, w, name):
    S = proj.shape[0]
    T = _row_tile(S)
    taps = w.shape[0]
    ncb = 3 * GDN_W // CB

    def body(x_ref, xp_ref, w_ref, o_ref):
        i = pl.program_id(1)
        ext = jnp.concatenate([jnp.where(i > 0, xp_ref[...], 0.0), x_ref[...]], axis=0)
        o_ref[...] = jax.nn.silu(_conv_rows(ext, w_ref, taps)[HALO:])

    main, prev = _conv_specs(T, S, PQ // CB, False)
    return pl.pallas_call(
        body, name=name, grid=(ncb, S // T),
        in_specs=[main, prev, pl.BlockSpec((taps, CB), lambda j, i: (0, j))],
        out_specs=pl.BlockSpec((T, CB), lambda j, i: (i, j)), out_shape=jax.ShapeDtypeStruct((S, 3 * GDN_W), F32),
        compiler_params=_cp("parallel", "parallel"),
    )(proj, proj, w)


def _gdn_conv_bwd(proj, dact, w, name):
    S = proj.shape[0]
    T = _row_tile(S)
    n = S // T
    taps = w.shape[0]
    ncb = 3 * GDN_W // CB

    def body(x_ref, xp_ref, xn_ref, d_ref, dn_ref, w_ref, dx_ref, gw_ref):
        i = pl.program_id(1)
        last = i == n - 1
        ext = jnp.concatenate([jnp.where(i > 0, xp_ref[...], 0.0), x_ref[...], jnp.where(last, 0.0, xn_ref[...])], axis=0)
        c = _conv_rows(ext, w_ref, taps)[HALO:]
        d_ext = jnp.concatenate([d_ref[...], jnp.where(last, 0.0, dn_ref[...])], axis=0)
        _, vjp = jax.vjp(jax.nn.silu, c)
        dc = vjp(d_ext)[0]
        dx_ref[...] = _conv_t_rows(dc, w_ref, taps)[:T].astype(dx_ref.dtype)

        @pl.when(i == 0)
        def _():
            gw_ref[...] = jnp.zeros_like(gw_ref)

        dcm = dc[:T]
        for k in range(taps):
            gw_ref[taps - 1 - k:taps - k, :] += jnp.sum(dcm * _down(ext, k)[HALO:HALO + T], axis=0, keepdims=True)

    main, prev, nxt = _conv_specs(T, S, PQ // CB, True)
    dmain, _, dnxt = _conv_specs(T, S, 0, True)
    wsp = pl.BlockSpec((taps, CB), lambda j, i: (0, j))
    return pl.pallas_call(
        body, name=name, grid=(ncb, n), in_specs=[main, prev, nxt, dmain, dnxt, wsp],
        out_specs=[pl.BlockSpec((T, CB), lambda j, i: (i, j)), wsp],
        out_shape=[jax.ShapeDtypeStruct((S, 3 * GDN_W), BF16), jax.ShapeDtypeStruct((taps, 3 * GDN_W), F32)],
        compiler_params=_cp("parallel", "arbitrary"),
    )(proj, proj, proj, dact, dact, w)


def _ffn_act_fwd(up, w, name):
    S = up.shape[0]
    T = _row_tile(S)
    taps = w.shape[0]
    ncb = D_FF // CB

    def body(g_ref, gp_ref, v_ref, w_ref, o_ref):
        i = pl.program_id(1)
        ext = jnp.concatenate([jnp.where(i > 0, gp_ref[...], 0.0), g_ref[...]], axis=0)
        c = _conv_rows(ext, w_ref, taps)[HALO:]
        o_ref[...] = (jax.nn.gelu(c) * v_ref[...]).astype(o_ref.dtype)

    main, prev = _conv_specs(T, S, 0, False)
    val = pl.BlockSpec((T, CB), lambda j, i: (i, j + ncb))
    return pl.pallas_call(
        body, name=name, grid=(ncb, S // T),
        in_specs=[main, prev, val, pl.BlockSpec((taps, CB), lambda j, i: (0, j))],
        out_specs=pl.BlockSpec((T, CB), lambda j, i: (i, j)), out_shape=jax.ShapeDtypeStruct((S, D_FF), BF16),
        compiler_params=_cp("parallel", "parallel"),
    )(up, up, up, w)


def _ffn_act_bwd(up, dact, w, name):
    S = up.shape[0]
    T = _row_tile(S)
    n = S // T
    taps = w.shape[0]
    ncb = D_FF // CB

    def body(g_ref, gp_ref, gn_ref, v_ref, vn_ref, d_ref, dn_ref, w_ref, dup_ref, gw_ref):
        i = pl.program_id(1)
        last = i == n - 1
        ext = jnp.concatenate([jnp.where(i > 0, gp_ref[...], 0.0), g_ref[...], jnp.where(last, 0.0, gn_ref[...])], axis=0)
        c = _conv_rows(ext, w_ref, taps)[HALO:]
        v_ext = jnp.concatenate([v_ref[...], jnp.where(last, 0.0, vn_ref[...])], axis=0)
        d_ext = jnp.concatenate([d_ref[...], jnp.where(last, 0.0, dn_ref[...])], axis=0)
        gl, vjp = jax.vjp(jax.nn.gelu, c)
        dup_ref[1] = (d_ext * gl)[:T].astype(dup_ref.dtype)
        dc = vjp(d_ext * v_ext)[0]
        dup_ref[0] = _conv_t_rows(dc, w_ref, taps)[:T].astype(dup_ref.dtype)

        @pl.when(i == 0)
        def _():
            gw_ref[...] = jnp.zeros_like(gw_ref)

        dcm = dc[:T]
        for k in range(taps):
            gw_ref[taps - 1 - k:taps - k, :] += jnp.sum(dcm * _down(ext, k)[HALO:HALO + T], axis=0, keepdims=True)

    main, prev, nxt = _conv_specs(T, S, 0, True)
    vmain, _, vnxt = _conv_specs(T, S, ncb, True)
    wsp = pl.BlockSpec((taps, CB), lambda j, i: (0, j))
    osp = pl.BlockSpec((2, T, CB), lambda j, i: (0, i, j))
    return pl.pallas_call(
        body, name=name, grid=(ncb, n), in_specs=[main, prev, nxt, vmain, vnxt, main, nxt, wsp],
        out_specs=[osp, wsp],
        out_shape=[jax.ShapeDtypeStruct((2, S, D_FF), BF16), jax.ShapeDtypeStruct((taps, D_FF), F32)],
        compiler_params=_cp("parallel", "arbitrary"),
    )(up, up, up, up, up, dact, dact, w)


def _tri_masks():
    r = _rows((GDN_C, GDN_C))
    c = _lanes((GDN_C, GDN_C))
    return r >= c, r > c


def _each(fn, *cols):
    return tuple(fn(*args) for args in zip(*cols))


def _tri_inv_raw(lows):
    r = _rows(lows[0].shape)
    c = _lanes(lows[0].shape)
    eye = jnp.where(r == c, 1.0, 0.0)
    ps = _each(lambda low: eye - low, lows)
    lps = lows
    for _ in range(5):
        lps = _each(lambda lp: _mm(lp, lp, False, False, True), lps)
        ps = _each(lambda p, lp: p + _mm(p, lp, False, False, True), ps, lps)
    return ps


@jax.custom_vjp
def _tri_inv(lows):
    return _tri_inv_raw(lows)


def _tri_inv_fwd(lows):
    ts = _tri_inv_raw(lows)
    return ts, ts


def _tri_inv_bwd(ts, dts):
    inner = _each(lambda t, dt: _mm(t, dt, True, False, True), ts, dts)
    return (_each(lambda m, t: -_mm(m, t, False, True, True), inner, ts),)


_tri_inv.defvjp(_tri_inv_fwd, _tri_inv_bwd)


@jax.custom_vjp
def _tri_inv_given(lows, ts):
    return ts


def _tri_inv_given_fwd(lows, ts):
    return ts, ts


def _tri_inv_given_bwd(ts, dts):
    return _tri_inv_bwd(ts, dts)[0], _each(jnp.zeros_like, ts)


_tri_inv_given.defvjp(_tri_inv_given_fwd, _tri_inv_given_bwd)


def _gdn_glog(a_col, alog, dtb):
    return -jnp.exp(alog) * jax.nn.softplus(a_col + dtb)


def _decay_operand():
    r = _rows((GDN_C, 2 * GDN_C))
    c = _lanes((GDN_C, 2 * GDN_C))
    return jnp.where((c >= GDN_C) | (r > c), 1.0, 0.0)


def _gdn_decay(glog):
    causal, _ = _tri_masks()
    res = _mm01(jnp.where(causal, 1.0, 0.0), glog * _decay_operand())
    return res[:, GDN_C:GDN_C + 1], res[:, :GDN_C]


def _gdn_decay_bwd(dgcol, dd):
    r = _rows((GDN_C, GDN_C))
    c = _lanes((GDN_C, GDN_C))
    dres = jnp.concatenate([dd, jnp.where(c == 0, dgcol, 0.0)], axis=1)
    dx = _mm01(jnp.where(r <= c, 1.0, 0.0), dres)
    return jnp.sum(dx * _decay_operand(), axis=1, keepdims=True)


def _gdn_chunk(qa, ka, va, bt_col, gcol, dmat, t_saved=None):
    causal, strict = _tri_masks()
    qn = _each(lambda q: q * lax.rsqrt(jnp.sum(q * q, axis=-1, keepdims=True) + EPS) * (GDN_DH ** -0.5), qa)
    kn = _each(lambda k: k * lax.rsqrt(jnp.sum(k * k, axis=-1, keepdims=True) + EPS), ka)
    beta = _each(jax.nn.sigmoid, bt_col)
    eg = _each(jnp.exp, gcol)
    decay = _each(lambda d: jnp.where(causal, jnp.exp(d), 0.0), dmat)
    kk = _each(lambda k: _mm(k, k, False, True), kn)
    low = _each(lambda b, m, d: jnp.where(strict, b * m * d, 0.0), beta, kk, decay)
    t = _tri_inv(low) if t_saved is None else _tri_inv_given(low, t_saved)
    w = _each(lambda t_, k, b, e: _mm(t_, k * (b * e), False, False, True), t, kn, beta, eg)
    u = _each(lambda t_, v, b: _mm(t_, v * b, False, False, True), t, va, beta)
    attn = _each(lambda q, k, d: _mm(q, k, False, True) * d, qn, kn, decay)
    last = _rows(gcol[0].shape) == GDN_C - 1
    g_last = _each(lambda g: jnp.sum(jnp.where(last, g, 0.0), axis=0, keepdims=True), gcol)
    qd = _each(lambda q, e: q * e, qn, eg)
    kd = _each(lambda k, gl, g: k * jnp.exp(gl - g), kn, g_last, gcol)
    return (w, u, qd, kd, attn), t


def _gdn_step(state, w, u, qd, kd, attn, egl):
    v_new = _each(lambda u_, w_, s: u_ - _mm(w_, s), u, w, state)
    o_state = _each(_mm, qd, state)
    o = _each(lambda os, a, v: os + _mm(a, v), o_state, attn, v_new)
    new = _each(lambda s, e, k, v: s * e + _mm(k, v, True, False), state, egl, kd, v_new)
    return o, new


def _heads(ref, base=0, width=GDN_DH):
    return tuple(ref[:, (base + h) * GDN_DH:(base + h) * GDN_DH + width] for h in range(GDN_H))


def _cols(a, base):
    return tuple(a[:, base + h:base + h + 1] for h in range(GDN_H))


def _gated_norm(o, z, nw):
    return o * lax.rsqrt(jnp.mean(o * o, axis=-1, keepdims=True) + EPS) * nw * jax.nn.silu(z)


def _hsl(h):
    return slice(h * GDN_DH, (h + 1) * GDN_DH)


def _pad_lanes(a, width=GDN_DH):
    return jnp.concatenate([a, jnp.zeros((a.shape[0], width - a.shape[1]), a.dtype)], axis=1)


def _gdn_prep(qkv, proj, alog, dtb, name):
    S = qkv.shape[0]
    N = S // GDN_C

    def body(qkv_ref, ab_ref, al_ref, dt_ref, w_ref, u_ref, qd_ref, kd_ref, at_ref, ti_ref, gc_ref):
        ab = ab_ref[...]
        glog = _each(_gdn_glog, _cols(ab, 0), _cols(al_ref[...], 0), _cols(dt_ref[...], 0))
        dec = _each(_gdn_decay, glog)
        gcol, dmat = _each(lambda d: d[0], dec), _each(lambda d: d[1], dec)
        (w, u, qd, kd, attn), tinv = _gdn_chunk(_heads(qkv_ref), _heads(qkv_ref, GDN_H), _heads(qkv_ref, 2 * GDN_H),
                                                _cols(ab, GDN_H), gcol, dmat)
        gc = jnp.zeros((GDN_C, 128), F32)
        for h in range(GDN_H):
            w_ref[:, _hsl(h)] = w[h]
            u_ref[:, _hsl(h)] = u[h]
            qd_ref[:, _hsl(h)] = qd[h]
            kd_ref[:, _hsl(h)] = kd[h]
            at_ref[:, _hsl(h)] = _pad_lanes(attn[h])
            ti_ref[:, _hsl(h)] = _pad_lanes(tinv[h])
            gc = jnp.where(_lanes(gc.shape) == h, gcol[h], gc)
        gc_ref[...] = gc

    vec = pl.BlockSpec((1, 128), lambda i: (0, 0))
    hsp = pl.BlockSpec((GDN_C, GDN_W), lambda i: (i, 0))
    hshape = jax.ShapeDtypeStruct((S, GDN_W), F32)
    return pl.pallas_call(
        body, name=name, grid=(N,),
        in_specs=[pl.BlockSpec((GDN_C, 3 * GDN_W), lambda i: (i, 0)), pl.BlockSpec((GDN_C, 128), lambda i: (i, PAB // 128)), vec, vec],
        out_specs=[hsp] * 6 + [pl.BlockSpec((GDN_C, 128), lambda i: (i, 0))],
        out_shape=[hshape] * 6 + [jax.ShapeDtypeStruct((S, 128), F32)],
        compiler_params=_cp("parallel"),
    )(qkv, proj, alog, dtb)


def _gdn_scan(w, u, qd, kd, attn, gc, name):
    S = w.shape[0]
    N = S // GDN_C

    def body(w_ref, u_ref, qd_ref, kd_ref, at_ref, gc_ref, o_ref, st_ref, s_ref):
        @pl.when(pl.program_id(0) == 0)
        def _():
            s_ref[...] = jnp.zeros_like(s_ref)

        state = tuple(s_ref[_hsl(h), :] for h in range(GDN_H))
        egl = _each(jnp.exp, _cols(gc_ref[GDN_C - 1:GDN_C, :], 0))
        o, new = _gdn_step(state, _heads(w_ref), _heads(u_ref), _heads(qd_ref), _heads(kd_ref),
                           _heads(at_ref, width=GDN_C), egl)
        for h in range(GDN_H):
            st_ref[_hsl(h), :] = state[h]
            o_ref[:, _hsl(h)] = o[h]
            s_ref[_hsl(h), :] = new[h]

    hsp = pl.BlockSpec((GDN_C, GDN_W), lambda i: (i, 0))
    return pl.pallas_call(
        body, name=name, grid=(N,),
        in_specs=[hsp] * 5 + [pl.BlockSpec((GDN_C, 128), lambda i: (i, 0))],
        out_specs=[hsp, pl.BlockSpec((None, GDN_W, GDN_DH), lambda i: (i, 0, 0))],
        out_shape=[jax.ShapeDtypeStruct((S, GDN_W), F32), jax.ShapeDtypeStruct((N, GDN_W, GDN_DH), F32)],
        scratch_shapes=[pltpu.VMEM((GDN_W, GDN_DH), F32)],
        compiler_params=_cp("arbitrary"),
    )(w, u, qd, kd, attn, gc)


def _gdn_scan_bwd(w, u, qd, kd, attn, gc, states, o, proj, dmixed, nw, name):
    S = w.shape[0]
    N = S // GDN_C

    def body(w_ref, u_ref, qd_ref, kd_ref, at_ref, gc_ref, st_ref, o_ref, z_ref, dm_ref, nw_ref,
             dw_ref, du_ref, dqd_ref, dkd_ref, dat_ref, dgl_ref, dz_ref, gnw_ref, ds_ref):
        @pl.when(pl.program_id(0) == 0)
        def _():
            ds_ref[...] = jnp.zeros_like(ds_ref)
            gnw_ref[...] = jnp.zeros_like(gnw_ref)

        nw = nw_ref[...]
        _, vjp_n = jax.vjp(lambda o, z, w_: _each(lambda a, b: _gated_norm(a, b, w_), o, z), _heads(o_ref), _heads(z_ref), nw)
        do, dz, dnw = vjp_n(_heads(dm_ref))
        state = tuple(st_ref[_hsl(h), :] for h in range(GDN_H))
        egl = _each(jnp.exp, _cols(gc_ref[GDN_C - 1:GDN_C, :], 0))
        _, vjp_s = jax.vjp(_gdn_step, state, _heads(w_ref), _heads(u_ref), _heads(qd_ref), _heads(kd_ref),
                           _heads(at_ref, width=GDN_C), egl)
        ds, dw, du, dqd, dkd, dat, degl = vjp_s((do, tuple(ds_ref[_hsl(h), :] for h in range(GDN_H))))
        dgl = jnp.zeros((8, 128), F32)
        for h in range(GDN_H):
            dz_ref[:, _hsl(h)] = dz[h].astype(dz_ref.dtype)
            ds_ref[_hsl(h), :] = ds[h]
            dw_ref[:, _hsl(h)] = dw[h]
            du_ref[:, _hsl(h)] = du[h]
            dqd_ref[:, _hsl(h)] = dqd[h]
            dkd_ref[:, _hsl(h)] = dkd[h]
            dat_ref[:, _hsl(h)] = _pad_lanes(dat[h])
            dgl = jnp.where(_lanes(dgl.shape) == h, degl[h] * egl[h], dgl)
        dgl_ref[...] = dgl
        gnw_ref[...] += dnw

    rev = lambda i: (N - 1 - i, 0)
    hsp = pl.BlockSpec((GDN_C, GDN_W), rev)
    gsp = pl.BlockSpec((GDN_C, 128), rev)
    vec = pl.BlockSpec((1, GDN_DH), lambda i: (0, 0))
    hshape = jax.ShapeDtypeStruct((S, GDN_W), F32)
    return pl.pallas_call(
        body, name=name, grid=(N,),
        in_specs=[hsp] * 5 + [gsp, pl.BlockSpec((None, GDN_W, GDN_DH), lambda i: (N - 1 - i, 0, 0)), hsp,
                              pl.BlockSpec((GDN_C, GDN_W), lambda i: (N - 1 - i, PZ // GDN_W)), hsp, vec],
        out_specs=[hsp] * 5 + [pl.BlockSpec((8, 128), rev), hsp, vec],
        out_shape=[hshape] * 5 + [jax.ShapeDtypeStruct((N * 8, 128), F32), jax.ShapeDtypeStruct((S, GDN_W), BF16),
                                  jax.ShapeDtypeStruct((1, GDN_DH), F32)],
        scratch_shapes=[pltpu.VMEM((GDN_W, GDN_DH), F32)],
        compiler_params=_cp("arbitrary"),
    )(w, u, qd, kd, attn, gc, states, o, proj, dmixed, nw)


def _gdn_prep_bwd(qkv, proj, alog, dtb, tinv, dw, du, dqd, dkd, dat, dgl, name):
    S = qkv.shape[0]
    N = S // GDN_C

    def body(qkv_ref, ab_ref, al_ref, dt_ref, ti_ref, dw_ref, du_ref, dqd_ref, dkd_ref, dat_ref, dgl_ref,
             dqkv_ref, dab_ref, gal_ref, gdt_ref):
        @pl.when(pl.program_id(0) == 0)
        def _():
            gal_ref[...] = jnp.zeros_like(gal_ref)
            gdt_ref[...] = jnp.zeros_like(gdt_ref)

        ab = ab_ref[...]
        glog, vjp_g = jax.vjp(lambda a, al, dt: _each(_gdn_glog, a, al, dt), _cols(ab, 0), _cols(al_ref[...], 0),
                              _cols(dt_ref[...], 0))
        dec = _each(_gdn_decay, glog)
        gcol, dmat = _each(lambda d: d[0], dec), _each(lambda d: d[1], dec)
        _, vjp_c, _ = jax.vjp(functools.partial(_gdn_chunk, t_saved=_heads(ti_ref, width=GDN_C)), _heads(qkv_ref),
                              _heads(qkv_ref, GDN_H), _heads(qkv_ref, 2 * GDN_H), _cols(ab, GDN_H), gcol, dmat, has_aux=True)
        dqa, dka, dva, dbt, dgcol, dd = vjp_c((_heads(dw_ref), _heads(du_ref), _heads(dqd_ref), _heads(dkd_ref),
                                               _heads(dat_ref, width=GDN_C)))
        last = _rows(dgcol[0].shape) == GDN_C - 1
        dgcol = _each(lambda d, g: d + jnp.where(last, g, 0.0), dgcol, _cols(dgl_ref[0:1, :], 0))
        da_col, dal, ddt = vjp_g(_each(_gdn_decay_bwd, dgcol, dd))
        dab = jnp.zeros((GDN_C, 128), F32)
        gal = jnp.zeros((1, 128), F32)
        gdt = jnp.zeros((1, 128), F32)
        for h in range(GDN_H):
            dqkv_ref[:, _hsl(h)] = dqa[h]
            dqkv_ref[:, _hsl(GDN_H + h)] = dka[h]
            dqkv_ref[:, _hsl(2 * GDN_H + h)] = dva[h]
            ln = _lanes(dab.shape)
            dab = dab + jnp.where(ln == h, da_col[h], 0.0) + jnp.where(ln == GDN_H + h, dbt[h], 0.0)
            l1 = _lanes(gal.shape)
            gal = gal + jnp.where(l1 == h, dal[h], 0.0)
            gdt = gdt + jnp.where(l1 == h, ddt[h], 0.0)
        dab_ref[...] = dab.astype(dab_ref.dtype)
        gal_ref[...] += gal
        gdt_ref[...] += gdt

    vec = pl.BlockSpec((1, 128), lambda i: (0, 0))
    hsp = pl.BlockSpec((GDN_C, GDN_W), lambda i: (i, 0))
    qsp = pl.BlockSpec((GDN_C, 3 * GDN_W), lambda i: (i, 0))
    return pl.pallas_call(
        body, name=name, grid=(N,),
        in_specs=[qsp, pl.BlockSpec((GDN_C, 128), lambda i: (i, PAB // 128)), vec, vec] + [hsp] * 6
        + [pl.BlockSpec((8, 128), lambda i: (i, 0))],
        out_specs=[qsp, pl.BlockSpec((GDN_C, 128), lambda i: (i, 0)), vec, vec],
        out_shape=[jax.ShapeDtypeStruct((S, 3 * GDN_W), F32), jax.ShapeDtypeStruct((S, 128), BF16),
                   jax.ShapeDtypeStruct((1, 128), F32), jax.ShapeDtypeStruct((1, 128), F32)],
        compiler_params=_cp("arbitrary"),
    )(qkv, proj, alog, dtb, tinv, dw, du, dqd, dkd, dat, dgl)


@jax.custom_vjp
def _expm1(x):
    u = jnp.exp(x)
    lu = jnp.log(u)
    small = (u - 1.0) * x / jnp.where(u == 1.0, 1.0, lu)
    small = jnp.where(u == 1.0, x, small)
    return jnp.where(jnp.abs(x) < 0.5, small, u - 1.0)


def _expm1_fwd(x):
    return _expm1(x), jnp.exp(x)


def _expm1_bwd(ex, g):
    return (g * ex,)


_expm1.defvjp(_expm1_fwd, _expm1_bwd)


def _lru_gates(xc, wa, ba, wx, bx, lam, first):
    r = jax.nn.sigmoid(_mm(xc, wa) + ba)
    i = jax.nn.sigmoid(_mm(xc, wx) + bx)
    log_a = -LRU_C * r * jax.nn.softplus(-lam)
    mult = jnp.sqrt(-_expm1(2.0 * log_a))
    mult = jnp.where(first, 1.0, mult)
    return jnp.exp(log_a), mult * i * xc


def _scan_fwd(a, b):
    T = a.shape[0]
    rows = _rows(a.shape)
    s = 1
    while s < T:
        ok = rows >= s
        b = a * jnp.where(ok, _down(b, s), 0.0) + b
        a = a * jnp.where(ok, _down(a, s), 1.0)
        s *= 2
    return a, b


def _scan_rev(a, b):
    T = a.shape[0]
    rows = _rows(a.shape)
    s = 1
    while s < T:
        ok = rows + s < T
        b = a * jnp.where(ok, _up(b, s), 0.0) + b
        a = a * jnp.where(ok, _up(a, s), 1.0)
        s *= 2
    return b


def _bsl(j):
    return slice(j * LRU_BD, (j + 1) * LRU_BD)


def _lru_tile(S):
    return _row_tile(S, 256)


def _lru_fwd(proj, conv_w, conv_b, wa, ba, wx, bx, lam, name):
    S = proj.shape[0]
    T = _lru_tile(S)
    taps = conv_w.shape[0]
    r = T // HALO

    def body(x_ref, xp_ref, cw_ref, cb_ref, wa_ref, ba_ref, wx_ref, bx_ref, lam_ref, h_ref, carry_ref):
        i = pl.program_id(0)

        @pl.when(i == 0)
        def _():
            carry_ref[...] = jnp.zeros_like(carry_ref)

        ext = jnp.concatenate([jnp.where(i > 0, xp_ref[...], 0.0), x_ref[...]], axis=0)
        xc = _conv_rows(ext, cw_ref, taps)[HALO:] + cb_ref[...]
        first = (i * T + _rows((T, LRU_BD))) == 0
        for j in range(LRU_NB):
            a, b = _lru_gates(xc[:, _bsl(j)], wa_ref[j], ba_ref[:, _bsl(j)], wx_ref[j], bx_ref[:, _bsl(j)],
                              lam_ref[:, _bsl(j)], first=first)
            pa, hb = _scan_fwd(a, b)
            h_ref[:, _bsl(j)] = pa * carry_ref[0:1, _bsl(j)] + hb
            carry_ref[0:1, _bsl(j)] = h_ref[T - 1:T, _bsl(j)]

    vec = pl.BlockSpec((1, LRU_W), lambda i: (0, 0))
    wsp = pl.BlockSpec((LRU_NB, LRU_BD, LRU_BD), lambda i: (0, 0, 0))
    return pl.pallas_call(
        body, name=name, grid=(S // T,),
        in_specs=[pl.BlockSpec((T, LRU_W), lambda i: (i, PXR // LRU_W)),
                  pl.BlockSpec((HALO, LRU_W), lambda i: (jnp.maximum(i * r - 1, 0), PXR // LRU_W)),
                  pl.BlockSpec((taps, LRU_W), lambda i: (0, 0)), vec, wsp, vec, wsp, vec, vec],
        out_specs=pl.BlockSpec((T, LRU_W), lambda i: (i, 0)), out_shape=jax.ShapeDtypeStruct((S, LRU_W), F32),
        scratch_shapes=[pltpu.VMEM((8, LRU_W), F32)],
        compiler_params=_cp("arbitrary"),
    )(proj, proj, conv_w, conv_b.reshape(1, LRU_W), wa, ba.reshape(1, LRU_W), wx, bx.reshape(1, LRU_W), lam.reshape(1, LRU_W))


def _lru_bwd(proj, hl, dmixed, conv_w, conv_b, wa, ba, wx, bx, lam, name):
    S = proj.shape[0]
    T = _lru_tile(S)
    n = S // T
    taps = conv_w.shape[0]
    r = T // HALO
    mb = 768 // LRU_W

    def body(x_ref, xp_ref, g_ref, h_ref, hp_ref, dy_ref, cw_ref, cb_ref, wa_ref, ba_ref, wx_ref, bx_ref, lam_ref,
             dx_ref, dg_ref, gcw_ref, gcb_ref, gwa_ref, gba_ref, gwx_ref, gbx_ref, glam_ref, carry_ref, dxc_ref, nxt_ref):
        s = pl.program_id(0)
        i = n - 1 - s

        @pl.when(s == 0)
        def _():
            carry_ref[...] = jnp.zeros_like(carry_ref)
            nxt_ref[...] = jnp.zeros_like(nxt_ref)
            for ref in (gcw_ref, gcb_ref, gwa_ref, gba_ref, gwx_ref, gbx_ref, glam_ref):
                ref[...] = jnp.zeros_like(ref)

        ext = jnp.concatenate([jnp.where(i > 0, xp_ref[...], 0.0), x_ref[...]], axis=0)
        xc = _conv_rows(ext, cw_ref, taps)[HALO:] + cb_ref[...]
        rows = _rows((T, LRU_BD))
        first = (i * T + rows) == 0
        h_before = jnp.where(i > 0, hp_ref[HALO - 1:HALO, :], 0.0)
        for j in range(LRU_NB):
            sl = _bsl(j)
            (a, _), vjp_g = jax.vjp(functools.partial(_lru_gates, first=first), xc[:, sl], wa_ref[j], ba_ref[:, sl],
                                    wx_ref[j], bx_ref[:, sl], lam_ref[:, sl])
            gelu_g, vjp_a = jax.vjp(jax.nn.gelu, g_ref[:, sl])
            h = h_ref[:, sl]
            dy = dy_ref[:, sl]
            dg_ref[:, sl] = vjp_a(dy * h)[0].astype(dg_ref.dtype)
            b_rev = dy * gelu_g + jnp.where(rows == T - 1, carry_ref[0:1, sl], 0.0)
            a_rev = jnp.where(rows == T - 1, 0.0, _up(a, 1))
            dh = _scan_rev(a_rev, b_rev)
            carry_ref[:, sl] = (a * dh)[:HALO]
            h_prev = jnp.where(rows == 0, h_before[:, sl], _down(h, 1))
            dxc, dwa, dba, dwx, dbx, dlam = vjp_g((dh * h_prev, dh))
            dxc_ref[:, sl] = dxc
            gwa_ref[j] += dwa
            gwx_ref[j] += dwx
            gba_ref[:, sl] += dba
            gbx_ref[:, sl] += dbx
            glam_ref[:, sl] += dlam
        dxc = dxc_ref[...]
        d_ext = jnp.concatenate([dxc, nxt_ref[...]], axis=0)
        dx_ref[...] = _conv_t_rows(d_ext, cw_ref, taps)[:T].astype(dx_ref.dtype)
        nxt_ref[...] = dxc[:HALO]
        gcb_ref[...] += jnp.sum(dxc, axis=0, keepdims=True)
        for k in range(taps):
            gcw_ref[taps - 1 - k:taps - k, :] += jnp.sum(dxc * _down(ext, k)[HALO:], axis=0, keepdims=True)

    vec = pl.BlockSpec((1, LRU_W), lambda s: (0, 0))
    wsp = pl.BlockSpec((LRU_NB, LRU_BD, LRU_BD), lambda s: (0, 0, 0))
    cwsp = pl.BlockSpec((taps, LRU_W), lambda s: (0, 0))

    def main(cb):
        return pl.BlockSpec((T, LRU_W), lambda s: (n - 1 - s, cb))

    def prev(cb):
        return pl.BlockSpec((HALO, LRU_W), lambda s: (jnp.maximum((n - 1 - s) * r - 1, 0), cb))

    vshape = jax.ShapeDtypeStruct((1, LRU_W), F32)
    wshape = jax.ShapeDtypeStruct((LRU_NB, LRU_BD, LRU_BD), F32)
    return pl.pallas_call(
        body, name=name, grid=(n,),
        in_specs=[main(PXR // LRU_W), prev(PXR // LRU_W), main(PGR // LRU_W), main(0), prev(0), main(mb),
                  cwsp, vec, wsp, vec, wsp, vec, vec],
        out_specs=[main(0), main(0), cwsp, vec, wsp, vec, wsp, vec, vec],
        out_shape=[jax.ShapeDtypeStruct((S, LRU_W), BF16), jax.ShapeDtypeStruct((S, LRU_W), BF16),
                   jax.ShapeDtypeStruct((taps, LRU_W), F32), vshape, wshape, vshape, wshape, vshape, vshape],
        scratch_shapes=[pltpu.VMEM((8, LRU_W), F32), pltpu.VMEM((T, LRU_W), F32), pltpu.VMEM((HALO, LRU_W), F32)],
        compiler_params=_cp("arbitrary"),
    )(proj, proj, proj, hl, hl, dmixed, conv_w, conv_b.reshape(1, LRU_W), wa, ba.reshape(1, LRU_W), wx,
      bx.reshape(1, LRU_W), lam.reshape(1, LRU_W))


def _mix_out(o, proj, hl, y_pool, nw, name):
    S = o.shape[0]
    T = _row_tile(S)

    def body(o_ref, z_ref, h_ref, g_ref, p_ref, nw_ref, m_ref):
        for h in range(GDN_H):
            m_ref[:, _hsl(h)] = _gated_norm(o_ref[:, _hsl(h)], z_ref[:, _hsl(h)], nw_ref[...]).astype(m_ref.dtype)
        m_ref[:, GDN_W:GDN_W + LRU_W] = (h_ref[...] * jax.nn.gelu(g_ref[...])).astype(m_ref.dtype)
        m_ref[:, GDN_W + LRU_W:] = p_ref[...].astype(m_ref.dtype)

    row = pl.BlockSpec((T, GDN_W), lambda i: (i, 0))
    return pl.pallas_call(
        body, name=name, grid=(S // T,),
        in_specs=[row, pl.BlockSpec((T, GDN_W), lambda i: (i, PZ // GDN_W)), row,
                  pl.BlockSpec((T, LRU_W), lambda i: (i, PGR // LRU_W)), pl.BlockSpec((T, POOL_W), lambda i: (i, 0)),
                  pl.BlockSpec((1, GDN_DH), lambda i: (0, 0))],
        out_specs=pl.BlockSpec((T, D_MODEL), lambda i: (i, 0)), out_shape=jax.ShapeDtypeStruct((S, D_MODEL), BF16),
        compiler_params=_cp("parallel"),
    )(o, proj, hl, proj, y_pool, nw)


def _as2d(a):
    return a.reshape(-1, a.shape[-1])


def _ew_rows(rows, cols):
    t = rows
    while t * cols * 4 > (2 << 20) and t % 16 == 0:
        t //= 2
    return t


def _rs_rows(rows, cols, budget=2 << 20):
    t = rows
    while t * cols * 4 > budget and t % 32 == 0:
        t //= 2
    return t


def _adamw(w, g, m, v, name):
    shape = w.shape
    w2, g2, m2, v2 = _as2d(w), _as2d(g), _as2d(m), _as2d(v)
    rows, cols = w2.shape
    t = _ew_rows(rows, cols)

    def body(w_ref, g_ref, m_ref, v_ref, d_ref, nm_ref, nv_ref):
        gr = g_ref[...]
        nm = ADAM_B1 * m_ref[...] + (1.0 - ADAM_B1) * gr
        nv = ADAM_B2 * v_ref[...] + (1.0 - ADAM_B2) * (gr * gr)
        m_hat = nm / (1.0 - ADAM_B1 ** ADAM_STEP)
        v_hat = nv / (1.0 - ADAM_B2 ** ADAM_STEP)
        d_ref[...] = -ADAM_LR * (m_hat / (jnp.sqrt(v_hat) + ADAM_EPS) + ADAM_WD * w_ref[...])
        nm_ref[...] = nm
        nv_ref[...] = nv

    sp = pl.BlockSpec((t, cols), lambda i: (i, 0))
    sh = jax.ShapeDtypeStruct((rows, cols), F32)
    d, nm, nv = pl.pallas_call(body, name=name, grid=(rows // t,), in_specs=[sp] * 4, out_specs=[sp] * 3,
                               out_shape=[sh] * 3, compiler_params=_cp("parallel"))(w2, g2, m2, v2)
    return d.reshape(shape), nm.reshape(shape), nv.reshape(shape)


def _place():
    return lax.axis_index("x"), lax.axis_index("y"), lax.axis_index("c")


def _gather_weights(arrs, name):
    n = len(arrs)

    def body(*refs):
        outs = refs[n:2 * n]
        send, recv = refs[2 * n:]
        x, y, c = _place()
        s_me, s_x, s_y, s_d = 2 * x + y, 2 * (1 - x) + y, 2 * x + (1 - y), 2 * (1 - x) + (1 - y)
        xpeer, ypeer, sib = (1 - x, y, c), (x, 1 - y, c), (x, y, 1 - c)

        def rc(k, t, src, dst, to):
            return pltpu.make_async_remote_copy(src_ref=src, dst_ref=dst, send_sem=send.at[k, t], recv_sem=recv.at[k, t],
                                                device_id=to, device_id_type=MESH)

        def piece(k, s, top):
            rq = outs[k].shape[2] // 2
            return outs[k].at[s, c, pl.ds(0 if top else rq, rq)]

        sent = []

        def start(k, t, ref, to):
            cp = rc(k, t, ref, ref, to)
            cp.start()
            sent.append(cp)

        for k in range(n):
            start(k, 0, outs[k].at[s_me, c], xpeer)
            start(k, 1, outs[k].at[s_me, c], ypeer)
        for k in range(n):
            got = outs[k].at[s_x, c]
            rc(k, 0, got, got, xpeer).wait_recv()
            start(k, 2, piece(k, s_x, True), ypeer)
            start(k, 3, got, sib)
        for k in range(n):
            got = outs[k].at[s_y, c]
            rc(k, 1, got, got, ypeer).wait_recv()
            start(k, 6, piece(k, s_y, False), xpeer)
            start(k, 4, got, sib)
        for k in range(n):
            top, bottom = piece(k, s_d, True), piece(k, s_d, False)
            rc(k, 2, top, top, ypeer).wait_recv()
            rc(k, 6, bottom, bottom, xpeer).wait_recv()
            start(k, 5, outs[k].at[s_d, c], sib)
        for k in range(n):
            for t, s in ((3, s_x), (4, s_y), (5, s_d)):
                got = outs[k].at[s, 1 - c]
                rc(k, t, got, got, sib).wait_recv()
        for cp in sent:
            cp.wait_send()

    return pl.pallas_call(
        body, name=name, in_specs=[HBM] * n, out_specs=[HBM] * n,
        out_shape=[jax.ShapeDtypeStruct(a.shape, a.dtype) for a in arrs],
        input_output_aliases={k: k for k in range(n)},
        scratch_shapes=[pltpu.SemaphoreType.DMA((n, 7)), pltpu.SemaphoreType.DMA((n, 7))],
    )(*arrs)


def _share_halves(arrs, name):
    n = len(arrs)

    def body(*refs):
        outs = refs[n:2 * n]
        send, recv = refs[2 * n:]
        x, y, c = _place()
        cps = []
        for k in range(n):
            mine = outs[k].at[:, c]
            cp = pltpu.make_async_remote_copy(src_ref=mine, dst_ref=mine, send_sem=send.at[k], recv_sem=recv.at[k],
                                              device_id=(x, y, 1 - c), device_id_type=MESH)
            cp.start()
            cps.append(cp)
        for k in range(n):
            got = outs[k].at[:, 1 - c]
            pltpu.make_async_remote_copy(src_ref=got, dst_ref=got, send_sem=send.at[k], recv_sem=recv.at[k],
                                         device_id=(x, y, 1 - c), device_id_type=MESH).wait_recv()
        for cp in cps:
            cp.wait_send()

    return pl.pallas_call(
        body, name=name, in_specs=[HBM] * n, out_specs=[HBM] * n,
        out_shape=[jax.ShapeDtypeStruct(a.shape, a.dtype) for a in arrs],
        input_output_aliases={k: k for k in range(n)},
        scratch_shapes=[pltpu.SemaphoreType.DMA((n,)), pltpu.SemaphoreType.DMA((n,))],
    )(*arrs)


_REL = tuple((dx, dy, dc) for dx in (0, 1) for dy in (0, 1) for dc in (0, 1))[1:]
SEM = pl.BlockSpec(memory_space=pltpu.SEMAPHORE)
DATAFLOW = pltpu.SideEffectType.DATAFLOW_SIDE_EFFECTING


def _flip(v, d):
    return 1 - v if d else v


def _rs_direct_copies(srcs, land, send, recv):
    x, y, c = _place()
    cps = []
    for k in range(len(srcs)):
        for r, (dx, dy, dc) in enumerate(_REL):
            px, py, pc = _flip(x, dx), _flip(y, dy), _flip(c, dc)
            cps.append(pltpu.make_async_remote_copy(
                src_ref=srcs[k].at[2 * px + py, pc], dst_ref=land[k].at[r], send_sem=send.at[k * len(_REL) + r],
                recv_sem=recv.at[k * len(_REL) + r], device_id=(px, py, pc), device_id_type=MESH))
    return cps


def _rs_direct_start(grads, thru, name):
    n = len(grads)
    lands = [pltpu.with_memory_space_constraint(lax.empty((len(_REL),) + g.shape[2:], g.dtype), pltpu.HBM) for g in grads]

    def body(*refs):
        for cp in _rs_direct_copies(refs[:n], refs[n + 1:2 * n + 1], refs[2 * n + 1], refs[2 * n + 2]):
            cp.start()

    sems = pltpu.SemaphoreType.DMA((n * len(_REL),))
    keep = [pltpu.HBM(a.shape, a.dtype) for a in (*grads, thru, *lands)]
    out = pl.pallas_call(
        body, name=name, in_specs=[HBM] * (2 * n + 1), out_specs=(SEM, SEM) + (HBM,) * (2 * n + 1),
        out_shape=(sems, sems, *keep), input_output_aliases={i: 2 + i for i in range(2 * n + 1)},
        compiler_params=pltpu.CompilerParams(has_side_effects=DATAFLOW),
    )(*[pltpu.with_memory_space_constraint(a, pltpu.HBM) for a in (*grads, thru)], *lands)
    return out[0], out[1], out[2:2 + n], out[2 + n], out[3 + n:]


def _rs_direct_wait(send, recv, grads, lands, after, name):
    n = len(grads)

    def body(*refs):
        for cp in _rs_direct_copies(refs[:n], refs[n:2 * n], refs[2 * n], refs[2 * n + 1]):
            cp.wait_send()
            cp.wait_recv()

    keep = [pltpu.HBM(a.shape, a.dtype) for a in (*grads, *lands)]
    out = pl.pallas_call(
        body, name=name, in_specs=[HBM] * (2 * n) + [SEM, SEM, pl.BlockSpec(memory_space=pl.ANY)], out_specs=(HBM,) * (2 * n),
        out_shape=tuple(keep), input_output_aliases={i: i for i in range(2 * n)},
        compiler_params=pltpu.CompilerParams(has_side_effects=DATAFLOW),
    )(*grads, *lands, send, recv, after)
    return out[:n], out[n:]


_CHIPS = ((1, 0), (0, 1), (1, 1))


def _gather_ici_copies(bufs, send, recv):
    x, y, c = _place()
    cps = []
    for k in range(len(bufs)):
        mine = bufs[k].at[2 * x + y, c]
        for j, (dx, dy) in enumerate(_CHIPS):
            cps.append(pltpu.make_async_remote_copy(
                src_ref=mine, dst_ref=mine, send_sem=send.at[k * len(_CHIPS) + j], recv_sem=recv.at[k * len(_CHIPS) + j],
                device_id=(_flip(x, dx), _flip(y, dy), c), device_id_type=MESH))
    return cps


def _gather_d2d_copies(bufs, send, recv):
    x, y, c = _place()
    cps = []
    for k in range(len(bufs)):
        for j, (dx, dy) in enumerate(_CHIPS):
            got = bufs[k].at[2 * _flip(x, dx) + _flip(y, dy), c]
            cps.append(pltpu.make_async_remote_copy(
                src_ref=got, dst_ref=got, send_sem=send.at[k * len(_CHIPS) + j], recv_sem=recv.at[k * len(_CHIPS) + j],
                device_id=(x, y, 1 - c), device_id_type=MESH))
    return cps


def _copies_start(bufs, thru, copies, name):
    n = len(bufs)

    def body(*refs):
        for cp in copies(refs[:n], refs[n + 1], refs[n + 2]):
            cp.start()

    sems = pltpu.SemaphoreType.DMA((n * len(_CHIPS),))
    out = pl.pallas_call(
        body, name=name, in_specs=[HBM] * (n + 1), out_specs=(SEM, SEM) + (HBM,) * (n + 1),
        out_shape=(sems, sems, *[pltpu.HBM(a.shape, a.dtype) for a in (*bufs, thru)]),
        input_output_aliases={i: 2 + i for i in range(n + 1)},
        compiler_params=pltpu.CompilerParams(has_side_effects=DATAFLOW),
    )(*[pltpu.with_memory_space_constraint(a, pltpu.HBM) for a in (*bufs, thru)])
    return out[0], out[1], out[2:2 + n], out[2 + n]


def _copies_wait(send, recv, bufs, after, copies, name):
    n = len(bufs)

    def body(*refs):
        for cp in copies(refs[:n], refs[n], refs[n + 1]):
            cp.wait_send()
            cp.wait_recv()

    return pl.pallas_call(
        body, name=name, in_specs=[HBM] * n + [SEM, SEM, pl.BlockSpec(memory_space=pl.ANY)], out_specs=(HBM,) * n,
        out_shape=tuple(pltpu.HBM(a.shape, a.dtype) for a in bufs), input_output_aliases={i: i for i in range(n)},
        compiler_params=pltpu.CompilerParams(has_side_effects=DATAFLOW),
    )(*bufs, send, recv, after)


def _rs_direct_sum(grad, land, layer, name, into=None):
    _, _, rows, cols = grad.shape
    t = _rs_rows(rows, cols, 6 << 20)
    npieces = len(_REL) + 1

    def body(g_ref, l_ref, *rest):
        o_ref, acc_ref = rest[-2], rest[-1]
        j = pl.program_id(1)

        @pl.when(j == 0)
        def _():
            acc_ref[...] = g_ref[...].astype(F32)

        @pl.when(j > 0)
        def _():
            acc_ref[...] += l_ref[...].astype(F32)

        @pl.when(j == npieces - 1)
        def _():
            o_ref[...] = acc_ref[...]

    def mine(i, j):
        x, y, c = _place()
        return (2 * x + y, c, i, 0)

    in_specs = [pl.BlockSpec((None, None, t, cols), mine),
                pl.BlockSpec((None, t, cols), lambda i, j: (jnp.maximum(j - 1, 0), i, 0))]
    args = [grad, land]
    if into is not None:
        in_specs.append(pl.BlockSpec(memory_space=pl.ANY))
        args.append(into)
    return pl.pallas_call(
        body, name=name, grid=(rows // t, npieces), in_specs=in_specs,
        out_specs=pl.BlockSpec((None, None, t, cols), lambda i, j: (layer, lax.axis_index("c"), i, 0)),
        scratch_shapes=[pltpu.VMEM((t, cols), F32)],
        out_shape=jax.ShapeDtypeStruct((2, 2, rows, cols), F32), input_output_aliases={} if into is None else {2: 0},
        compiler_params=_cp("parallel", "arbitrary"),
    )(*args)


def _ar_copies(buf, land, send, recv):
    x, y, c = _place()
    return [pltpu.make_async_remote_copy(src_ref=buf, dst_ref=land.at[r], send_sem=send.at[r], recv_sem=recv.at[r],
                                         device_id=(_flip(x, dx), _flip(y, dy), _flip(c, dc)), device_id_type=MESH)
            for r, (dx, dy, dc) in enumerate(_REL)]


def _ar_start(buf, name):
    land = pltpu.with_memory_space_constraint(lax.empty((len(_REL),) + buf.shape, buf.dtype), pltpu.HBM)

    def body(buf_ref, land_ref, send, recv, *_):
        for cp in _ar_copies(buf_ref, land_ref, send, recv):
            cp.start()

    sems = pltpu.SemaphoreType.DMA((len(_REL),))
    return pl.pallas_call(
        body, name=name, in_specs=[HBM, HBM], out_specs=(SEM, SEM, HBM, HBM),
        out_shape=(sems, sems, pltpu.HBM(buf.shape, buf.dtype), pltpu.HBM(land.shape, land.dtype)),
        input_output_aliases={0: 2, 1: 3}, compiler_params=pltpu.CompilerParams(has_side_effects=DATAFLOW),
    )(pltpu.with_memory_space_constraint(buf, pltpu.HBM), land)


def _ar_wait(send, recv, buf, land, after, name):
    def body(buf_ref, land_ref, send_ref, recv_ref, *_):
        for cp in _ar_copies(buf_ref, land_ref, send_ref, recv_ref):
            cp.wait_send()
            cp.wait_recv()

    return pl.pallas_call(
        body, name=name, in_specs=[HBM, HBM, SEM, SEM, pl.BlockSpec(memory_space=pl.ANY)], out_specs=(HBM, HBM),
        out_shape=(pltpu.HBM(buf.shape, buf.dtype), pltpu.HBM(land.shape, land.dtype)), input_output_aliases={0: 0, 1: 1},
        compiler_params=pltpu.CompilerParams(has_side_effects=DATAFLOW),
    )(buf, land, send, recv, after)


def _ar_sum(buf, land, name):
    rows, cols = buf.shape
    t = _rs_rows(rows, cols)

    def slot(i, j):
        x, y, c = _place()
        xd, yd, cd = j // 4, (j // 2) % 2, j % 2
        rel = 4 * (x + xd - 2 * x * xd) + 2 * (y + yd - 2 * y * yd) + (c + cd - 2 * c * cd)
        return (jnp.maximum(rel - 1, 0), i, 0)

    def body(b_ref, l_ref, o_ref, acc_ref):
        j = pl.program_id(1)
        x, y, c = _place()
        val = jnp.where(j == 4 * x + 2 * y + c, b_ref[...], l_ref[...])

        @pl.when(j == 0)
        def _():
            acc_ref[...] = val

        @pl.when(j > 0)
        def _():
            acc_ref[...] += val

        @pl.when(j == len(_REL))
        def _():
            o_ref[...] = acc_ref[...]

    sp = pl.BlockSpec((t, cols), lambda i, j: (i, 0))
    return pl.pallas_call(
        body, name=name, grid=(rows // t, len(_REL) + 1), in_specs=[sp, pl.BlockSpec((None, t, cols), slot)], out_specs=sp,
        out_shape=jax.ShapeDtypeStruct((rows, cols), F32), scratch_shapes=[pltpu.VMEM((t, cols), F32)],
        compiler_params=_cp("parallel", "arbitrary"),
    )(buf, land)


def _pad128(v):
    return jnp.zeros((1, 128), F32).at[0, :v.shape[0]].set(v)


def _layer_fwd(l, x, p, hooks=None):
    hooks = hooks or {}
    h1 = _rms_fwd(x, p["norm1_w"], f"rms1_{l}")
    proj = _matmul(h1, p["w_in"], "nn", name=f"mm_in_{l}", tn=768)
    y_pool = _pool_fwd(proj, p["pool_w"], p["pool_b"], p["pool_scale"], f"pool_{l}")
    qkv = _gdn_conv_fwd(proj, p["gdn_conv_w"], f"gconv_{l}")
    alog, dtb = _pad128(p["gdn_a_log"]), _pad128(p["gdn_dt_bias"])
    gw, gu, gqd, gkd, gat, tinv, gc = _gdn_prep(qkv, proj, alog, dtb, f"gprep_{l}")
    o, states = _gdn_scan(gw, gu, gqd, gkd, gat, gc, f"gscan_{l}")
    if "mid" in hooks:
        o = hooks["mid"](o)
    hl = _lru_fwd(proj, p["lru_conv_w"], p["lru_conv_b"], p["lru_wa"], p["lru_ba"], p["lru_wx"], p["lru_bx"],
                  p["lru_lambda"], f"lru_{l}")
    mixed = _mix_out(o, proj, hl, y_pool, p["gdn_norm_w"].reshape(1, GDN_DH), f"mix_{l}")
    x2 = _matmul(mixed, p["w_out"], "nn", name=f"mm_out_{l}", res=x)
    h2 = _rms_fwd(x2, p["norm2_w"], f"rms2_{l}")
    if "ffn" in hooks:
        h2 = hooks["ffn"](h2)
    up = _matmul(h2, p["ffn_up"], "nn", name=f"mm_up_{l}", b_split=True)
    act = _ffn_act_fwd(up, p["ffn_conv_w"], f"ffn_{l}")
    if "down" in hooks:
        act = hooks["down"](act)
    x3 = _matmul(act, p["ffn_down"], "nn", name=f"mm_down_{l}", res=x2)
    saved = dict(x=x, h1=h1, proj=proj, qkv=qkv, gdn=(gw, gu, gqd, gkd, gat, gc), tinv=tinv, states=states, o=o, hl=hl, mixed=mixed,
                 x2=x2, h2=h2, up=up, act=act, alog=alog, dtb=dtb)
    return x3, saved


def _layer_bwd(l, dx3, p, s, after_ffn=None):
    g = {}
    dact = _matmul(dx3, p["ffn_down"], "nt", name=f"mm_ddown_{l}")
    g["ffn_down"] = _matmul(s["act"], dx3, "tn", name=f"mm_gdown_{l}", out_dtype=BF16)
    dup, g["ffn_conv_w"] = _ffn_act_bwd(s["up"], dact, p["ffn_conv_w"], f"ffn_b_{l}")
    dh2 = _matmul(dup, p["ffn_up"], "nt", name=f"mm_dup_{l}", b_split=True, tk=3072)
    g["ffn_up"] = _matmul(s["h2"], dup, "tn", name=f"mm_gup_{l}", b_split=True, o_split=4, tk=4096, out_dtype=BF16)
    dx2, g["norm2_w"] = _rms_bwd(s["x2"], p["norm2_w"], dh2, dx3, f"rms2_b_{l}")
    g["w_out"] = _matmul(s["mixed"], dx2, "tn", name=f"mm_gout_{l}", out_dtype=BF16)
    if after_ffn is not None:
        dx2 = after_ffn(dx2, g)
    dmixed = _matmul(dx2, p["w_out"], "nt", name=f"mm_dout_{l}")
    proj = s["proj"]
    du_pool, g["pool_w"], g["pool_b"], g["pool_scale"] = _pool_bwd(proj, dmixed, p["pool_w"], p["pool_b"], p["pool_scale"], f"pool_b_{l}")
    gw, gu, gqd, gkd, gat, gc = s["gdn"]
    dw, du, dqd, dkd, dat, dgl, dz, g["gdn_norm_w"] = _gdn_scan_bwd(
        gw, gu, gqd, gkd, gat, gc, s["states"], s["o"], proj, dmixed, p["gdn_norm_w"].reshape(1, GDN_DH), f"gscan_b_{l}")
    dqkv, dab, gal, gdt = _gdn_prep_bwd(s["qkv"], proj, s["alog"], s["dtb"], s["tinv"], dw, du, dqd, dkd, dat, dgl, f"gprep_b_{l}")
    g["gdn_a_log"], g["gdn_dt_bias"] = gal[0, :GDN_H], gdt[0, :GDN_H]
    dpre, g["gdn_conv_w"] = _gdn_conv_bwd(proj, dqkv, p["gdn_conv_w"], f"gconv_b_{l}")
    (dxr, dgr, g["lru_conv_w"], g["lru_conv_b"], g["lru_wa"], g["lru_ba"], g["lru_wx"], g["lru_bx"], g["lru_lambda"]) = _lru_bwd(
        proj, s["hl"], dmixed, p["lru_conv_w"], p["lru_conv_b"], p["lru_wa"], p["lru_ba"], p["lru_wx"], p["lru_bx"],
        p["lru_lambda"], f"lru_b_{l}")
    S = proj.shape[0]
    dproj = jnp.concatenate([dpre, dz, dxr, dgr, du_pool, dab, jnp.zeros((S, PCOLS - PAB - 128), BF16)], axis=1)
    dh1 = _matmul(dproj, p["w_in"], "nt", name=f"mm_din_{l}", tk=1792)
    g["w_in"] = _matmul(s["h1"], dproj, "tn", name=f"mm_gin_{l}", tn=768, tk=4096, out_dtype=BF16)
    dx, g["norm1_w"] = _rms_bwd(s["x"], p["norm1_w"], dh1, dx2, f"rms1_b_{l}")
    return dx, g


_IN_PERM = ((512, 3584), (3596, 5132), (0, 512), (3584, 3596))


def _w_in_to_proj(w):
    parts = [w[:, a:b] for a, b in _IN_PERM]
    return jnp.concatenate(parts + [jnp.zeros((w.shape[0], PCOLS - IN_COLS), w.dtype)], axis=1)


def _proj_to_w_in(g):
    return jnp.concatenate([g[:, PPOOL:PPOOL + 512], g[:, 0:3072], g[:, PAB:PAB + 12], g[:, 3072:PPOOL]], axis=1)


def _rows_to_mixed(w):
    return jnp.concatenate([w[512:], w[:512]], axis=0)


def _mixed_to_rows(g):
    return jnp.concatenate([g[1536:], g[:1536]], axis=0)


SMALL_SHARDED = ("gdn_conv_w", "lru_conv_w", "ffn_conv_w")
BIG = ("w_in", "w_out", "ffn_up", "ffn_down")
SMALL_REPLICATED = ("norm1_w", "pool_w", "pool_b", "pool_scale", "gdn_a_log", "gdn_dt_bias", "gdn_norm_w", "lru_conv_b",
                    "lru_wa", "lru_ba", "lru_wx", "lru_bx", "lru_lambda", "norm2_w")
WEIGHTS = ("norm1_w", "w_in", "pool_w", "pool_b", "pool_scale", "gdn_conv_w", "gdn_a_log", "gdn_dt_bias", "gdn_norm_w",
           "lru_conv_w", "lru_conv_b", "lru_wa", "lru_ba", "lru_wx", "lru_bx", "lru_lambda", "w_out", "norm2_w", "ffn_up",
           "ffn_conv_w", "ffn_down", "final_norm_w")
FLAT_COLS = 1024


def _pack(arrs):
    flat = jnp.concatenate([a.reshape(-1) for a in arrs])
    rows = -(-flat.shape[0] // (8 * FLAT_COLS)) * 8
    return jnp.pad(flat, (0, rows * FLAT_COLS - flat.shape[0])).reshape(rows, FLAT_COLS)


def _unpack(buf, like):
    flat = buf.reshape(-1)
    out, off = [], 0
    for a in like:
        size = 1
        for d in a.shape:
            size *= d
        out.append(flat[off:off + size].reshape(a.shape))
        off += size
    return out


def kernel(x, norm1_w, w_in, pool_w, pool_b, pool_scale, gdn_conv_w, gdn_a_log, gdn_dt_bias, gdn_norm_w, lru_conv_w, lru_conv_b, lru_wa, lru_ba, lru_wx, lru_bx, lru_lambda, w_out, norm2_w, ffn_up, ffn_conv_w, ffn_down, final_norm_w, loss_target, m_norm1_w, m_w_in, m_pool_w, m_pool_b, m_pool_scale, m_gdn_conv_w, m_gdn_a_log, m_gdn_dt_bias, m_gdn_norm_w, m_lru_conv_w, m_lru_conv_b, m_lru_wa, m_lru_ba, m_lru_wx, m_lru_bx, m_lru_lambda, m_w_out, m_norm2_w, m_ffn_up, m_ffn_conv_w, m_ffn_down, m_final_norm_w, v_norm1_w, v_w_in, v_pool_w, v_pool_b, v_pool_scale, v_gdn_conv_w, v_gdn_a_log, v_gdn_dt_bias, v_gdn_norm_w, v_lru_conv_w, v_lru_conv_b, v_lru_wa, v_lru_ba, v_lru_wx, v_lru_bx, v_lru_lambda, v_w_out, v_norm2_w, v_ffn_up, v_ffn_conv_w, v_ffn_down, v_final_norm_w):
    W = dict(norm1_w=norm1_w, w_in=w_in, pool_w=pool_w, pool_b=pool_b, pool_scale=pool_scale, gdn_conv_w=gdn_conv_w,
             gdn_a_log=gdn_a_log, gdn_dt_bias=gdn_dt_bias, gdn_norm_w=gdn_norm_w, lru_conv_w=lru_conv_w, lru_conv_b=lru_conv_b,
             lru_wa=lru_wa, lru_ba=lru_ba, lru_wx=lru_wx, lru_bx=lru_bx, lru_lambda=lru_lambda, w_out=w_out, norm2_w=norm2_w,
             ffn_up=ffn_up, ffn_conv_w=ffn_conv_w, ffn_down=ffn_down, final_norm_w=final_norm_w)
    M = dict(norm1_w=m_norm1_w, w_in=m_w_in, pool_w=m_pool_w, pool_b=m_pool_b, pool_scale=m_pool_scale, gdn_conv_w=m_gdn_conv_w,
             gdn_a_log=m_gdn_a_log, gdn_dt_bias=m_gdn_dt_bias, gdn_norm_w=m_gdn_norm_w, lru_conv_w=m_lru_conv_w,
             lru_conv_b=m_lru_conv_b, lru_wa=m_lru_wa, lru_ba=m_lru_ba, lru_wx=m_lru_wx, lru_bx=m_lru_bx, lru_lambda=m_lru_lambda,
             w_out=m_w_out, norm2_w=m_norm2_w, ffn_up=m_ffn_up, ffn_conv_w=m_ffn_conv_w, ffn_down=m_ffn_down,
             final_norm_w=m_final_norm_w)
    V = dict(norm1_w=v_norm1_w, w_in=v_w_in, pool_w=v_pool_w, pool_b=v_pool_b, pool_scale=v_pool_scale, gdn_conv_w=v_gdn_conv_w,
             gdn_a_log=v_gdn_a_log, gdn_dt_bias=v_gdn_dt_bias, gdn_norm_w=v_gdn_norm_w, lru_conv_w=v_lru_conv_w,
             lru_conv_b=v_lru_conv_b, lru_wa=v_lru_wa, lru_ba=v_lru_ba, lru_wx=v_lru_wx, lru_bx=v_lru_bx, lru_lambda=v_lru_lambda,
             w_out=v_w_out, norm2_w=v_norm2_w, ffn_up=v_ffn_up, ffn_conv_w=v_ffn_conv_w, ffn_down=v_ffn_down,
             final_norm_w=v_final_norm_w)
    S = x.shape[1]
    xs = x.reshape(S, D_MODEL)
    tgt = loss_target.reshape(S, D_MODEL)
    mx, my, mc = _place()
    shard = 2 * mx + my

    small_sh = jnp.concatenate([W[k].reshape(N_LAYERS, -1) for k in SMALL_SHARDED], axis=1)
    n_small = small_sh.shape[1]
    pad = -n_small % 1024
    small_sh = jnp.pad(small_sh, ((0, 0), (0, pad))).reshape(N_LAYERS, -1, 1024)

    def own_slots(l):
        out = []
        for w in [W[k][l].astype(BF16) for k in BIG] + [small_sh[l]]:
            buf = lax.dynamic_update_slice(lax.empty((4,) + w.shape, w.dtype), w[None], (shard,) + (0,) * w.ndim)
            out.append(buf.reshape(4, 2, w.shape[0] // 2, w.shape[1]))
        return out

    def whole(g):
        return g.reshape(4, 2 * g.shape[2], g.shape[3])

    def mixer_params(l, g_in, g_out, g_small):
        p = {k: W[k][l] for k in SMALL_REPLICATED}
        g_in = whole(g_in)
        p["w_in"] = _w_in_to_proj(jnp.transpose(g_in, (1, 0, 2)).reshape(g_in.shape[1], IN_COLS))
        p["w_out"] = _rows_to_mixed(whole(g_out).reshape(D_MODEL, D_MODEL))
        g_small = whole(g_small).reshape(4, -1)[:, :n_small]
        off = 0
        for k in SMALL_SHARDED:
            taps, width = W[k].shape[1], W[k].shape[2]
            piece = g_small[:, off:off + taps * width].reshape(4, taps, width)
            p[k] = jnp.transpose(piece, (1, 0, 2)).reshape(taps, 4 * width)
            off += taps * width
        return p

    def ffn_params(g_up, g_down):
        return dict(ffn_up=whole(g_up), ffn_down=whole(g_down).reshape(D_FF, D_MODEL))

    layers, saved = [None] * N_LAYERS, [None] * N_LAYERS
    s0 = own_slots(0)
    g_in0, g_out0, g_small0 = _gather_weights([s0[0], s0[1], s0[4]], "gather_weights")
    f_send, f_recv, ffn0, g_in0 = _copies_start(s0[2:4], g_in0, _gather_ici_copies, "gather_ffn0_ici_start")
    l_send, l_recv, bufs1, g_in0 = _copies_start(own_slots(1), g_in0, _gather_ici_copies, "gather_l1_ici_start")
    layers[0] = mixer_params(0, g_in0, g_out0, g_small0)
    stage = {}

    def mid(o):
        bufs = _copies_wait(f_send, f_recv, ffn0, o, _gather_ici_copies, "gather_ffn0_ici_wait")
        stage["ffn0"] = _copies_start(bufs, o, _gather_d2d_copies, "gather_ffn0_d2d_start")
        return stage["ffn0"][3]

    def ffn(h2):
        send, recv, bufs, _ = stage["ffn0"]
        layers[0].update(ffn_params(*_copies_wait(send, recv, bufs, h2, _gather_d2d_copies, "gather_ffn0_d2d_wait")))
        return h2

    def down(act):
        bufs = _copies_wait(l_send, l_recv, bufs1, act, _gather_ici_copies, "gather_l1_ici_wait")
        stage["l1"] = _copies_start(bufs, act, _gather_d2d_copies, "gather_l1_d2d_start")
        return stage["l1"][3]

    h, saved[0] = _layer_fwd(0, xs, layers[0], dict(mid=mid, ffn=ffn, down=down))
    send, recv, bufs, _ = stage["l1"]
    g1 = _copies_wait(send, recv, bufs, h, _gather_d2d_copies, "gather_l1_d2d_wait")
    layers[1] = {**mixer_params(1, g1[0], g1[1], g1[4]), **ffn_params(g1[2], g1[3])}
    h, saved[1] = _layer_fwd(1, h, layers[1])
    loss_part, dh, g_final = _loss_head(h, final_norm_w, tgt, "loss_head")

    def big_partials(g_layer, names=BIG):
        out = []
        for k in names:
            g = g_layer[k]
            if k == "w_in":
                g = _proj_to_w_in(g)
                g = jnp.transpose(g.reshape(g.shape[0], 4, IN_COLS // 4), (1, 0, 2))
            elif k == "w_out":
                g = _mixed_to_rows(g).reshape(4, D_MODEL // 4, D_MODEL)
            elif k == "ffn_down":
                g = g.reshape(4, D_FF // 4, D_MODEL)
            out.append(g.reshape(4, 2, g.shape[1] // 2, g.shape[2]))
        return out

    FFN, MIX = ("ffn_up", "ffn_down", "w_out"), ("w_in",)
    grads = [None] * N_LAYERS
    dh, grads[1] = _layer_bwd(1, dh, layers[1], saved[1])
    send1, recv1, part1, dh, lands1 = _rs_direct_start(big_partials(grads[1]), dh, "rs_direct_start_1")
    sent0 = []

    def after_ffn(dx2, g):
        send0, recv0, part0, dx2, lands0 = _rs_direct_start(big_partials(g, FFN), dx2, "rs_direct_start_0")
        sent0.extend([send0, recv0, part0, lands0])
        return dx2

    dh, grads[0] = _layer_bwd(0, dh, layers[0], saved[0], after_ffn)

    small_names = SMALL_REPLICATED + SMALL_SHARDED
    small_list = [jnp.stack([grads[l][k].reshape(W[k].shape[1:]) if k in SMALL_REPLICATED else grads[l][k] for l in range(N_LAYERS)])
                  for k in small_names]
    small_list += [g_final.reshape(D_MODEL), loss_part[0, 0:1]]
    a_send, a_recv, packed, a_land = _ar_start(_pack(small_list), "ar_start")

    sendm, recvm, partm, dh, landsm = _rs_direct_start(big_partials(grads[0], MIX), dh, "rs_direct_start_0m")
    part1, lands1 = _rs_direct_wait(send1, recv1, part1, lands1, dh, "rs_direct_wait_1")
    part0, lands0 = _rs_direct_wait(*sent0, dh, "rs_direct_wait_0")
    red = {k: _rs_direct_sum(g, ld, 1, f"rs_sum_1_{k}") for k, g, ld in zip(BIG, part1, lands1)}
    for k, g, ld in zip(FFN, part0, lands0):
        red[k] = _rs_direct_sum(g, ld, 0, f"rs_sum_0_{k}", into=red[k])
    G, DELTA, NM, NV = {}, {}, {}, {}

    def update_big(names, shared):
        for k, r in zip(names, shared):
            G[k] = r.reshape(N_LAYERS, 2 * r.shape[2], r.shape[3])
            DELTA[k], NM[k], NV[k] = _adamw(W[k], G[k], M[k], V[k], f"adam_{k}")

    update_big(FFN, _share_halves([red[k] for k in FFN], "rs_share_ffn"))
    partm, landsm = _rs_direct_wait(sendm, recvm, partm, landsm, DELTA[FFN[-1]], "rs_direct_wait_0m")
    for k, g, ld in zip(MIX, partm, landsm):
        red[k] = _rs_direct_sum(g, ld, 0, f"rs_sum_0_{k}", into=red[k])
    update_big(MIX, _share_halves([red[k] for k in MIX], "rs_share_mix"))
    grad_x = dh.reshape(x.shape)

    packed, a_land = _ar_wait(a_send, a_recv, packed, a_land, G[MIX[-1]], "ar_wait")
    reduced = _unpack(_ar_sum(packed, a_land, "ar_sum"), small_list)
    small_g = dict(zip(small_names, reduced[:len(small_names)]))
    small_g["final_norm_w"] = reduced[-2]
    loss = reduced[-1][0]
    for k in SMALL_SHARDED:
        width = W[k].shape[2]
        small_g[k] = lax.dynamic_slice_in_dim(small_g[k], shard * width, width, axis=2)

    small_all = small_names + ("final_norm_w",)
    dl, nm, nv = _adamw(_pack([W[k] for k in small_all]), _pack([small_g[k] for k in small_all]),
                        _pack([M[k] for k in small_all]), _pack([V[k] for k in small_all]), "adam_small")
    like = [W[k] for k in small_all]
    for k, d_, m_, v_ in zip(small_all, _unpack(dl, like), _unpack(nm, like), _unpack(nv, like)):
        G[k], DELTA[k], NM[k], NV[k] = small_g[k], d_, m_, v_

    return (loss, grad_x, *[G[k] for k in WEIGHTS], *[DELTA[k] for k in WEIGHTS], *[NM[k] for k in WEIGHTS],
            *[NV[k] for k in WEIGHTS])
```

```python
import functools

import jax
import jax.numpy as jnp
from jax import lax
from jax.experimental import pallas as pl
from jax.experimental.pallas import tpu as pltpu

F32 = jnp.float32
BF16 = jnp.bfloat16
_MXU = jnp.bfloat16

D_MODEL = 2048
N_LAYERS = 2
POOL_W = 512
POOL_G = 4
POOL_GD = 128
POOL_WINDOWS = (2, 4, 8, 16)
POOL_HALO = 16
GDN_W = 768
GDN_H = 6
GDN_DH = 128
GDN_C = 64
LRU_W = 768
LRU_NB = 6
LRU_BD = 128
LRU_C = 8.0
D_FF = 6144
EPS = 1e-6
IN_COLS = 5132
HALO = 8

PQ, PK, PV, PZ, PXR, PGR, PPOOL, PAB, PCOLS = 0, 768, 1536, 2304, 3072, 3840, 4608, 5120, 5376
CB = 768

ADAM_LR = 0.001
ADAM_B1 = 0.9
ADAM_B2 = 0.999
ADAM_EPS = 1e-08
ADAM_WD = 0.01
ADAM_STEP = 10

VMEM_LIMIT = 56 * 1024 * 1024
MESH = pl.DeviceIdType.MESH
HBM = pl.BlockSpec(memory_space=pltpu.HBM)


def _cp(*sem):
    return pltpu.CompilerParams(dimension_semantics=sem, vmem_limit_bytes=VMEM_LIMIT)


def _dg(a, b, ta, tb):
    dims = (((0 if ta else 1,), (1 if tb else 0,)), ((), ()))
    return lax.dot_general(a, b, dims, preferred_element_type=F32)


def _split2(a):
    hi = a.astype(BF16)
    lo = (a - hi.astype(F32)).astype(BF16)
    return hi, lo


def _mm_raw(a, b, ta, tb, hi):
    if _MXU == F32:
        return _dg(a, b, ta, tb)
    if not hi:
        return _dg(a.astype(_MXU), b.astype(_MXU), ta, tb)
    a1, a2 = _split2(a)
    b1, b2 = _split2(b)
    return _dg(a1, b1, ta, tb) + (_dg(a1, b2, ta, tb) + _dg(a2, b1, ta, tb))


@functools.partial(jax.custom_vjp, nondiff_argnums=(2, 3, 4))
def _mm(a, b, ta=False, tb=False, hi=False):
    return _mm_raw(a, b, ta, tb, hi)


def _mm_fwd(a, b, ta, tb, hi):
    return _mm_raw(a, b, ta, tb, hi), (a, b)


def _mm_bwd(ta, tb, hi, res, dc):
    a, b = res
    da = _mm(b, dc, tb, True, hi) if ta else _mm(dc, b, False, not tb, hi)
    db = _mm(dc, a, True, ta, hi) if tb else _mm(a, dc, not ta, False, hi)
    return da, db


_mm.defvjp(_mm_fwd, _mm_bwd)


def _mm01(m01, x):
    if _MXU == F32:
        return _dg(m01, x, False, False)
    m = m01.astype(BF16)
    x1 = x.astype(BF16)
    r = x - x1.astype(F32)
    x2 = r.astype(BF16)
    x3 = (r - x2.astype(F32)).astype(BF16)
    return _dg(m, x1, False, False) + (_dg(m, x2, False, False) + _dg(m, x3, False, False))


def _down(x, k):
    return x if k == 0 else pltpu.roll(x, k, 0)


def _up(x, k):
    return x if k == 0 else pltpu.roll(x, x.shape[0] - k, 0)


def _rows(shape):
    return lax.broadcasted_iota(jnp.int32, shape, 0)


def _lanes(shape):
    return lax.broadcasted_iota(jnp.int32, shape, 1)


def _matmul(a, b, mode, *, name, res=None, tm=1024, tn=1024, tk=2048, b_split=False, o_split=0, out_dtype=F32):
    ta, tb = mode == "tn", mode == "nt"
    a_split = a.ndim == 3
    if a_split:
        assert not ta
        M, K = a.shape[1], a.shape[0] * a.shape[2]
        tk = min(tk, a.shape[2])
    elif ta:
        K, M = a.shape
    else:
        M, K = a.shape
    if b_split:
        ns = b.shape[0]
        N = b.shape[1] if tb else ns * b.shape[2]
    else:
        N = b.shape[0] if tb else b.shape[1]
    tm, tn, tk = min(tm, M), min(tn, N), min(tk, K)
    if b_split:
        per = b.shape[2]
        if tb:
            tk = min(tk, per)
        else:
            tn = min(tn, per)
    if o_split:
        tn = min(tn, N // o_split)
    assert M % tm == 0 and N % tn == 0 and K % tk == 0, (name, M, N, K, tm, tn, tk)
    nk = K // tk
    if a_split:
        ka = a.shape[2] // tk
        a_spec = pl.BlockSpec((None, tm, tk), lambda i, j, k: (k // ka, i, k % ka))
    else:
        a_spec = pl.BlockSpec((tk, tm), lambda i, j, k: (k, i)) if ta else pl.BlockSpec((tm, tk), lambda i, j, k: (i, k))
    if not b_split:
        b_spec = pl.BlockSpec((tn, tk), lambda i, j, k: (j, k)) if tb else pl.BlockSpec((tk, tn), lambda i, j, k: (k, j))
    elif tb:
        kb = per // tk
        b_spec = pl.BlockSpec((None, tn, tk), lambda i, j, k: (k // kb, j, k % kb))
    else:
        nb = per // tn
        b_spec = pl.BlockSpec((None, tk, tn), lambda i, j, k: (j // nb, k, j % nb))
    if o_split:
        ob = (N // o_split) // tn
        out_shape = jax.ShapeDtypeStruct((o_split, M, N // o_split), out_dtype)
        o_spec = pl.BlockSpec((None, tm, tn), lambda i, j, k: (j // ob, i, j % ob))
    else:
        out_shape = jax.ShapeDtypeStruct((M, N), out_dtype)
        o_spec = pl.BlockSpec((tm, tn), lambda i, j, k: (i, j))
    in_specs = [a_spec, b_spec]
    args = [a, b]
    if res is not None:
        in_specs.append(pl.BlockSpec((tm, tn), lambda i, j, k: (i, j)))
        args.append(res)
    use_acc = nk > 1 and out_dtype != F32

    def body(*refs):
        a_ref, b_ref = refs[0], refs[1]
        o_ref = refs[2 + (res is not None)]
        acc_ref = refs[-1] if use_acc else o_ref
        p = _dg(a_ref[...].astype(_MXU), b_ref[...].astype(_MXU), ta, tb)
        first = p + refs[2][...] if res is not None else p
        if nk == 1:
            o_ref[...] = first.astype(o_ref.dtype)
        else:
            k = pl.program_id(2)

            @pl.when(k == 0)
            def _():
                acc_ref[...] = first

            @pl.when(k > 0)
            def _():
                acc_ref[...] += p

            if use_acc:
                @pl.when(k == nk - 1)
                def _():
                    o_ref[...] = acc_ref[...].astype(o_ref.dtype)

    return pl.pallas_call(
        body, name=name, grid=(M // tm, N // tn, nk), in_specs=in_specs, out_specs=o_spec, out_shape=out_shape,
        scratch_shapes=[pltpu.VMEM((tm, tn), F32)] if use_acc else [],
        compiler_params=_cp("parallel", "parallel", "arbitrary"),
    )(*args)


def _rms(x, w):
    return x * lax.rsqrt(jnp.mean(x * x, axis=-1, keepdims=True) + EPS) * w


def _row_tile(S, t=512):
    t = min(t, S)
    assert S % t == 0
    return t


def _rms_fwd(x, w, name):
    S, D = x.shape
    T = _row_tile(S)

    def body(x_ref, w_ref, o_ref):
        o_ref[...] = _rms(x_ref[...], w_ref[...]).astype(o_ref.dtype)

    return pl.pallas_call(
        body, name=name, grid=(S // T,),
        in_specs=[pl.BlockSpec((T, D), lambda i: (i, 0)), pl.BlockSpec((1, D), lambda i: (0, 0))],
        out_specs=pl.BlockSpec((T, D), lambda i: (i, 0)), out_shape=jax.ShapeDtypeStruct((S, D), BF16),
        compiler_params=_cp("parallel"),
    )(x, w.reshape(1, D))


def _rms_bwd(x, w, dh, dres, name):
    S, D = x.shape
    T = _row_tile(S)

    def body(x_ref, w_ref, dh_ref, dr_ref, dx_ref, gw_ref):
        _, vjp = jax.vjp(_rms, x_ref[...], w_ref[...])
        dx, dw = vjp(dh_ref[...])
        dx_ref[...] = dr_ref[...] + dx

        @pl.when(pl.program_id(0) == 0)
        def _():
            gw_ref[...] = jnp.zeros_like(gw_ref)

        gw_ref[...] += dw

    row = pl.BlockSpec((T, D), lambda i: (i, 0))
    vec = pl.BlockSpec((1, D), lambda i: (0, 0))
    return pl.pallas_call(
        body, name=name, grid=(S // T,), in_specs=[row, vec, row, row], out_specs=[row, vec],
        out_shape=[jax.ShapeDtypeStruct((S, D), F32), jax.ShapeDtypeStruct((1, D), F32)],
        compiler_params=_cp("arbitrary"),
    )(x, w.reshape(1, D), dh, dres)


def _loss_head(x, w, tgt, name):
    S, D = x.shape
    T = _row_tile(S)

    def body(x_ref, w_ref, t_ref, l_ref, dx_ref, gw_ref):
        y, vjp = jax.vjp(_rms, x_ref[...], w_ref[...])
        err = y - t_ref[...]
        part = 0.5 * jnp.sum(jnp.mean(err * err, axis=-1, keepdims=True), axis=0, keepdims=True)
        dx, dw = vjp(err * (1.0 / D))
        dx_ref[...] = dx

        @pl.when(pl.program_id(0) == 0)
        def _():
            gw_ref[...] = jnp.zeros_like(gw_ref)
            l_ref[...] = jnp.zeros_like(l_ref)

        gw_ref[...] += dw
        l_ref[...] += jnp.broadcast_to(part, l_ref.shape)

    row = pl.BlockSpec((T, D), lambda i: (i, 0))
    vec = pl.BlockSpec((1, D), lambda i: (0, 0))
    return pl.pallas_call(
        body, name=name, grid=(S // T,), in_specs=[row, vec, row],
        out_specs=[pl.BlockSpec((8, 128), lambda i: (0, 0)), row, vec],
        out_shape=[jax.ShapeDtypeStruct((8, 128), F32), jax.ShapeDtypeStruct((S, D), F32), jax.ShapeDtypeStruct((1, D), F32)],
        compiler_params=_cp("arbitrary"),
    )(x, w.reshape(1, D), tgt)


def _by_group(shape, vals):
    g = _lanes(shape) // POOL_GD
    out = vals[-1]
    for k in range(len(vals) - 2, -1, -1):
        out = jnp.where(g == k, vals[k], out)
    return out


def _pool_d(prev, u, t0):
    ext = jnp.concatenate([prev, u], axis=0)
    s2 = ext + _down(ext, 1)
    s4 = s2 + _down(s2, 2)
    s8 = s4 + _down(s4, 4)
    s16 = s8 + _down(s8, 8)
    ssel = _by_group(ext.shape, [s2, s4, s8, s16])[POOL_HALO:]
    win = _by_group(u.shape, [jnp.int32(w) for w in POOL_WINDOWS])
    cnt = jnp.minimum(t0 + _rows(u.shape) + 1, win).astype(F32)
    return ssel / cnt - u


def _pool_lin(d, w_ref, b):
    ys = [_mm(d[:, g * POOL_GD:(g + 1) * POOL_GD], w_ref[g]) for g in range(POOL_G)]
    return jnp.concatenate(ys, axis=1) + b


def _pool_fwd(proj, w, b, scale, name):
    S = proj.shape[0]
    T = _row_tile(S)
    r = T // POOL_HALO
    cb = PPOOL // POOL_W

    def body(u_ref, up_ref, w_ref, b_ref, sc_ref, y_ref):
        i = pl.program_id(0)
        prev = jnp.where(i > 0, up_ref[...], 0.0)
        d = _pool_d(prev, u_ref[...], i * T)
        y_ref[...] = _pool_lin(d, w_ref, b_ref[...]) * sc_ref[...]

    vec = pl.BlockSpec((1, POOL_W), lambda i: (0, 0))
    return pl.pallas_call(
        body, name=name, grid=(S // T,),
        in_specs=[pl.BlockSpec((T, POOL_W), lambda i: (i, cb)),
                  pl.BlockSpec((POOL_HALO, POOL_W), lambda i: (jnp.maximum(i * r - 1, 0), cb)),
                  pl.BlockSpec((POOL_G, POOL_GD, POOL_GD), lambda i: (0, 0, 0)), vec, vec],
        out_specs=pl.BlockSpec((T, POOL_W), lambda i: (i, 0)), out_shape=jax.ShapeDtypeStruct((S, POOL_W), F32),
        compiler_params=_cp("parallel"),
    )(proj, proj, w, b.reshape(1, POOL_W), scale.reshape(1, POOL_W))


def _pool_bwd(proj, dmixed, w, b, scale, name):
    S = proj.shape[0]
    T = _row_tile(S)
    n = S // T
    r = T // POOL_HALO
    cb = PPOOL // POOL_W
    mb = 1536 // POOL_W

    def body(u_ref, up_ref, dy_ref, dyn_ref, w_ref, b_ref, sc_ref, du_ref, gw_ref, gb_ref, gs_ref):
        i = pl.program_id(0)
        sc = sc_ref[...]
        dy = dy_ref[...]
        dy_ext = jnp.concatenate([dy, jnp.where(i < n - 1, dyn_ref[...], 0.0)], axis=0)
        dyl = dy_ext * sc
        dd = jnp.concatenate(
            [_mm(dyl[:, g * POOL_GD:(g + 1) * POOL_GD], w_ref[g], False, True) for g in range(POOL_G)], axis=1)
        t_ext = i * T + _rows(dd.shape)
        win = _by_group(dd.shape, [jnp.int32(v) for v in POOL_WINDOWS])
        cnt = jnp.minimum(t_ext + 1, win).astype(F32)
        e = jnp.where(t_ext < S, dd / cnt, 0.0)
        f2 = e + _up(e, 1)
        f4 = f2 + _up(f2, 2)
        f8 = f4 + _up(f4, 4)
        f16 = f8 + _up(f8, 8)
        du = (_by_group(dd.shape, [f2, f4, f8, f16]) - dd)[:T]
        du_ref[...] = du.astype(du_ref.dtype)

        prev = jnp.where(i > 0, up_ref[...], 0.0)
        d = _pool_d(prev, u_ref[...], i * T)
        ylin = _pool_lin(d, w_ref, b_ref[...])
        dyl_m = dy * sc

        @pl.when(i == 0)
        def _():
            gw_ref[...] = jnp.zeros_like(gw_ref)
            gb_ref[...] = jnp.zeros_like(gb_ref)
            gs_ref[...] = jnp.zeros_like(gs_ref)

        gs_ref[...] += jnp.sum(dy * ylin, axis=0, keepdims=True)
        gb_ref[...] += jnp.sum(dyl_m, axis=0, keepdims=True)
        for g in range(POOL_G):
            sl = slice(g * POOL_GD, (g + 1) * POOL_GD)
            gw_ref[g] += _mm(d[:, sl], dyl_m[:, sl], True, False)

    vec = pl.BlockSpec((1, POOL_W), lambda i: (0, 0))
    wsp = pl.BlockSpec((POOL_G, POOL_GD, POOL_GD), lambda i: (0, 0, 0))
    nh = S // POOL_HALO
    return pl.pallas_call(
        body, name=name, grid=(n,),
        in_specs=[pl.BlockSpec((T, POOL_W), lambda i: (i, cb)),
                  pl.BlockSpec((POOL_HALO, POOL_W), lambda i: (jnp.maximum(i * r - 1, 0), cb)),
                  pl.BlockSpec((T, POOL_W), lambda i: (i, mb)),
                  pl.BlockSpec((POOL_HALO, POOL_W), lambda i: (jnp.minimum((i + 1) * r, nh - 1), mb)),
                  wsp, vec, vec],
        out_specs=[pl.BlockSpec((T, POOL_W), lambda i: (i, 0)), wsp, vec, vec],
        out_shape=[jax.ShapeDtypeStruct((S, POOL_W), BF16), jax.ShapeDtypeStruct((POOL_G, POOL_GD, POOL_GD), F32),
                   jax.ShapeDtypeStruct((1, POOL_W), F32), jax.ShapeDtypeStruct((1, POOL_W), F32)],
        compiler_params=_cp("arbitrary"),
    )(proj, proj, dmixed, dmixed, w, b.reshape(1, POOL_W), scale.reshape(1, POOL_W))


def _conv_rows(ext, w_ref, taps):
    acc = w_ref[taps - 1:taps, :] * ext
    for k in range(1, taps):
        acc = acc + w_ref[taps - 1 - k:taps - k, :] * _down(ext, k)
    return acc


def _conv_t_rows(dc, w_ref, taps):
    acc = w_ref[taps - 1:taps, :] * dc
    for k in range(1, taps):
        acc = acc + w_ref[taps - 1 - k:taps - k, :] * _up(dc, k)
    return acc


def _conv_specs(T, S, ncb0, with_next):
    r = T // HALO
    nh = S // HALO
    main = pl.BlockSpec((T, CB), lambda j, i: (i, j + ncb0))
    prev = pl.BlockSpec((HALO, CB), lambda j, i: (jnp.maximum(i * r - 1, 0), j + ncb0))
    nxt = pl.BlockSpec((HALO, CB), lambda j, i: (jnp.minimum((i + 1) * r, nh - 1), j + ncb0))
    return (main, prev, nxt) if with_next else (main, prev)


def _gdn_conv_fwd(proj, w, name):
    S = proj.shape[0]
    T = _row_tile(S)
    taps = w.shape[0]
    ncb = 3 * GDN_W // CB

    def body(x_ref, xp_ref, w_ref, o_ref):
        i = pl.program_id(1)
        ext = jnp.concatenate([jnp.where(i > 0, xp_ref[...], 0.0), x_ref[...]], axis=0)
        o_ref[...] = jax.nn.silu(_conv_rows(ext, w_ref, taps)[HALO:])

    main, prev = _conv_specs(T, S, PQ // CB, False)
    return pl.pallas_call(
        body, name=name, grid=(ncb, S // T),
        in_specs=[main, prev, pl.BlockSpec((taps, CB), lambda j, i: (0, j))],
        out_specs=pl.BlockSpec((T, CB), lambda j, i: (i, j)), out_shape=jax.ShapeDtypeStruct((S, 3 * GDN_W), F32),
        compiler_params=_cp("parallel", "parallel"),
    )(proj, proj, w)


def _gdn_conv_bwd(proj, dact, w, name):
    S = proj.shape[0]
    T = _row_tile(S)
    n = S // T
    taps = w.shape[0]
    ncb = 3 * GDN_W // CB

    def body(x_ref, xp_ref, xn_ref, d_ref, dn_ref, w_ref, dx_ref, gw_ref):
        i = pl.program_id(1)
        last = i == n - 1
        ext = jnp.concatenate([jnp.where(i > 0, xp_ref[...], 0.0), x_ref[...], jnp.where(last, 0.0, xn_ref[...])], axis=0)
        c = _conv_rows(ext, w_ref, taps)[HALO:]
        d_ext = jnp.concatenate([d_ref[...], jnp.where(last, 0.0, dn_ref[...])], axis=0)
        _, vjp = jax.vjp(jax.nn.silu, c)
        dc = vjp(d_ext)[0]
        dx_ref[...] = _conv_t_rows(dc, w_ref, taps)[:T].astype(dx_ref.dtype)

        @pl.when(i == 0)
        def _():
            gw_ref[...] = jnp.zeros_like(gw_ref)

        dcm = dc[:T]
        for k in range(taps):
            gw_ref[taps - 1 - k:taps - k, :] += jnp.sum(dcm * _down(ext, k)[HALO:HALO + T], axis=0, keepdims=True)

    main, prev, nxt = _conv_specs(T, S, PQ // CB, True)
    dmain, _, dnxt = _conv_specs(T, S, 0, True)
    wsp = pl.BlockSpec((taps, CB), lambda j, i: (0, j))
    return pl.pallas_call(
        body, name=name, grid=(ncb, n), in_specs=[main, prev, nxt, dmain, dnxt, wsp],
        out_specs=[pl.BlockSpec((T, CB), lambda j, i: (i, j)), wsp],
        out_shape=[jax.ShapeDtypeStruct((S, 3 * GDN_W), BF16), jax.ShapeDtypeStruct((taps, 3 * GDN_W), F32)],
        compiler_params=_cp("parallel", "arbitrary"),
    )(proj, proj, proj, dact, dact, w)


def _ffn_act_fwd(up, w, name):
    S = up.shape[0]
    T = _row_tile(S)
    taps = w.shape[0]
    ncb = D_FF // CB

    def body(g_ref, gp_ref, v_ref, w_ref, o_ref):
        i = pl.program_id(1)
        ext = jnp.concatenate([jnp.where(i > 0, gp_ref[...], 0.0), g_ref[...]], axis=0)
        c = _conv_rows(ext, w_ref, taps)[HALO:]
        o_ref[...] = (jax.nn.gelu(c) * v_ref[...]).astype(o_ref.dtype)

    main, prev = _conv_specs(T, S, 0, False)
    val = pl.BlockSpec((T, CB), lambda j, i: (i, j + ncb))
    return pl.pallas_call(
        body, name=name, grid=(ncb, S // T),
        in_specs=[main, prev, val, pl.BlockSpec((taps, CB), lambda j, i: (0, j))],
        out_specs=pl.BlockSpec((T, CB), lambda j, i: (i, j)), out_shape=jax.ShapeDtypeStruct((S, D_FF), BF16),
        compiler_params=_cp("parallel", "parallel"),
    )(up, up, up, w)


def _ffn_act_bwd(up, dact, w, name):
    S = up.shape[0]
    T = _row_tile(S)
    n = S // T
    taps = w.shape[0]
    ncb = D_FF // CB

    def body(g_ref, gp_ref, gn_ref, v_ref, vn_ref, d_ref, dn_ref, w_ref, dup_ref, gw_ref):
        i = pl.program_id(1)
        last = i == n - 1
        ext = jnp.concatenate([jnp.where(i > 0, gp_ref[...], 0.0), g_ref[...], jnp.where(last, 0.0, gn_ref[...])], axis=0)
        c = _conv_rows(ext, w_ref, taps)[HALO:]
        v_ext = jnp.concatenate([v_ref[...], jnp.where(last, 0.0, vn_ref[...])], axis=0)
        d_ext = jnp.concatenate([d_ref[...], jnp.where(last, 0.0, dn_ref[...])], axis=0)
        gl, vjp = jax.vjp(jax.nn.gelu, c)
        dup_ref[1] = (d_ext * gl)[:T].astype(dup_ref.dtype)
        dc = vjp(d_ext * v_ext)[0]
        dup_ref[0] = _conv_t_rows(dc, w_ref, taps)[:T].astype(dup_ref.dtype)

        @pl.when(i == 0)
        def _():
            gw_ref[...] = jnp.zeros_like(gw_ref)

        dcm = dc[:T]
        for k in range(taps):
            gw_ref[taps - 1 - k:taps - k, :] += jnp.sum(dcm * _down(ext, k)[HALO:HALO + T], axis=0, keepdims=True)

    main, prev, nxt = _conv_specs(T, S, 0, True)
    vmain, _, vnxt = _conv_specs(T, S, ncb, True)
    wsp = pl.BlockSpec((taps, CB), lambda j, i: (0, j))
    osp = pl.BlockSpec((2, T, CB), lambda j, i: (0, i, j))
    return pl.pallas_call(
        body, name=name, grid=(ncb, n), in_specs=[main, prev, nxt, vmain, vnxt, main, nxt, wsp],
        out_specs=[osp, wsp],
        out_shape=[jax.ShapeDtypeStruct((2, S, D_FF), BF16), jax.ShapeDtypeStruct((taps, D_FF), F32)],
        compiler_params=_cp("parallel", "arbitrary"),
    )(up, up, up, up, up, dact, dact, w)


def _tri_masks():
    r = _rows((GDN_C, GDN_C))
    c = _lanes((GDN_C, GDN_C))
    return r >= c, r > c


def _each(fn, *cols):
    return tuple(fn(*args) for args in zip(*cols))


def _tri_inv_raw(lows):
    r = _rows(lows[0].shape)
    c = _lanes(lows[0].shape)
    eye = jnp.where(r == c, 1.0, 0.0)
    ps = _each(lambda low: eye - low, lows)
    lps = lows
    for _ in range(5):
        lps = _each(lambda lp: _mm(lp, lp, False, False, True), lps)
        ps = _each(lambda p, lp: p + _mm(p, lp, False, False, True), ps, lps)
    return ps


@jax.custom_vjp
def _tri_inv(lows):
    return _tri_inv_raw(lows)


def _tri_inv_fwd(lows):
    ts = _tri_inv_raw(lows)
    return ts, ts


def _tri_inv_bwd(ts, dts):
    inner = _each(lambda t, dt: _mm(t, dt, True, False, True), ts, dts)
    return (_each(lambda m, t: -_mm(m, t, False, True, True), inner, ts),)


_tri_inv.defvjp(_tri_inv_fwd, _tri_inv_bwd)


@jax.custom_vjp
def _tri_inv_given(lows, ts):
    return ts


def _tri_inv_given_fwd(lows, ts):
    return ts, ts


def _tri_inv_given_bwd(ts, dts):
    return _tri_inv_bwd(ts, dts)[0], _each(jnp.zeros_like, ts)


_tri_inv_given.defvjp(_tri_inv_given_fwd, _tri_inv_given_bwd)


def _gdn_glog(a_col, alog, dtb):
    return -jnp.exp(alog) * jax.nn.softplus(a_col + dtb)


def _decay_operand():
    r = _rows((GDN_C, 2 * GDN_C))
    c = _lanes((GDN_C, 2 * GDN_C))
    return jnp.where((c >= GDN_C) | (r > c), 1.0, 0.0)


def _gdn_decay(glog):
    causal, _ = _tri_masks()
    res = _mm01(jnp.where(causal, 1.0, 0.0), glog * _decay_operand())
    return res[:, GDN_C:GDN_C + 1], res[:, :GDN_C]


def _gdn_decay_bwd(dgcol, dd):
    r = _rows((GDN_C, GDN_C))
    c = _lanes((GDN_C, GDN_C))
    dres = jnp.concatenate([dd, jnp.where(c == 0, dgcol, 0.0)], axis=1)
    dx = _mm01(jnp.where(r <= c, 1.0, 0.0), dres)
    return jnp.sum(dx * _decay_operand(), axis=1, keepdims=True)


def _gdn_chunk(qa, ka, va, bt_col, gcol, dmat, t_saved=None):
    causal, strict = _tri_masks()
    qn = _each(lambda q: q * lax.rsqrt(jnp.sum(q * q, axis=-1, keepdims=True) + EPS) * (GDN_DH ** -0.5), qa)
    kn = _each(lambda k: k * lax.rsqrt(jnp.sum(k * k, axis=-1, keepdims=True) + EPS), ka)
    beta = _each(jax.nn.sigmoid, bt_col)
    eg = _each(jnp.exp, gcol)
    decay = _each(lambda d: jnp.where(causal, jnp.exp(d), 0.0), dmat)
    kk = _each(lambda k: _mm(k, k, False, True), kn)
    low = _each(lambda b, m, d: jnp.where(strict, b * m * d, 0.0), beta, kk, decay)
    t = _tri_inv(low) if t_saved is None else _tri_inv_given(low, t_saved)
    w = _each(lambda t_, k, b, e: _mm(t_, k * (b * e), False, False, True), t, kn, beta, eg)
    u = _each(lambda t_, v, b: _mm(t_, v * b, False, False, True), t, va, beta)
    attn = _each(lambda q, k, d: _mm(q, k, False, True) * d, qn, kn, decay)
    last = _rows(gcol[0].shape) == GDN_C - 1
    g_last = _each(lambda g: jnp.sum(jnp.where(last, g, 0.0), axis=0, keepdims=True), gcol)
    qd = _each(lambda q, e: q * e, qn, eg)
    kd = _each(lambda k, gl, g: k * jnp.exp(gl - g), kn, g_last, gcol)
    return (w, u, qd, kd, attn), t


def _gdn_step(state, w, u, qd, kd, attn, egl):
    v_new = _each(lambda u_, w_, s: u_ - _mm(w_, s), u, w, state)
    o_state = _each(_mm, qd, state)
    o = _each(lambda os, a, v: os + _mm(a, v), o_state, attn, v_new)
    new = _each(lambda s, e, k, v: s * e + _mm(k, v, True, False), state, egl, kd, v_new)
    return o, new


def _heads(ref, base=0, width=GDN_DH):
    return tuple(ref[:, (base + h) * GDN_DH:(base + h) * GDN_DH + width] for h in range(GDN_H))


def _cols(a, base):
    return tuple(a[:, base + h:base + h + 1] for h in range(GDN_H))


def _gated_norm(o, z, nw):
    return o * lax.rsqrt(jnp.mean(o * o, axis=-1, keepdims=True) + EPS) * nw * jax.nn.silu(z)


def _hsl(h):
    return slice(h * GDN_DH, (h + 1) * GDN_DH)


def _pad_lanes(a, width=GDN_DH):
    return jnp.concatenate([a, jnp.zeros((a.shape[0], width - a.shape[1]), a.dtype)], axis=1)


def _gdn_prep(qkv, proj, alog, dtb, name):
    S = qkv.shape[0]
    N = S // GDN_C

    def body(qkv_ref, ab_ref, al_ref, dt_ref, w_ref, u_ref, qd_ref, kd_ref, at_ref, ti_ref, gc_ref):
        ab = ab_ref[...]
        glog = _each(_gdn_glog, _cols(ab, 0), _cols(al_ref[...], 0), _cols(dt_ref[...], 0))
        dec = _each(_gdn_decay, glog)
        gcol, dmat = _each(lambda d: d[0], dec), _each(lambda d: d[1], dec)
        (w, u, qd, kd, attn), tinv = _gdn_chunk(_heads(qkv_ref), _heads(qkv_ref, GDN_H), _heads(qkv_ref, 2 * GDN_H),
                                                _cols(ab, GDN_H), gcol, dmat)
        gc = jnp.zeros((GDN_C, 128), F32)
        for h in range(GDN_H):
            w_ref[:, _hsl(h)] = w[h]
            u_ref[:, _hsl(h)] = u[h]
            qd_ref[:, _hsl(h)] = qd[h]
            kd_ref[:, _hsl(h)] = kd[h]
            at_ref[:, _hsl(h)] = _pad_lanes(attn[h])
            ti_ref[:, _hsl(h)] = _pad_lanes(tinv[h])
            gc = jnp.where(_lanes(gc.shape) == h, gcol[h], gc)
        gc_ref[...] = gc

    vec = pl.BlockSpec((1, 128), lambda i: (0, 0))
    hsp = pl.BlockSpec((GDN_C, GDN_W), lambda i: (i, 0))
    hshape = jax.ShapeDtypeStruct((S, GDN_W), F32)
    return pl.pallas_call(
        body, name=name, grid=(N,),
        in_specs=[pl.BlockSpec((GDN_C, 3 * GDN_W), lambda i: (i, 0)), pl.BlockSpec((GDN_C, 128), lambda i: (i, PAB // 128)), vec, vec],
        out_specs=[hsp] * 6 + [pl.BlockSpec((GDN_C, 128), lambda i: (i, 0))],
        out_shape=[hshape] * 6 + [jax.ShapeDtypeStruct((S, 128), F32)],
        compiler_params=_cp("parallel"),
    )(qkv, proj, alog, dtb)


def _gdn_scan(w, u, qd, kd, attn, gc, name):
    S = w.shape[0]
    N = S // GDN_C

    def body(w_ref, u_ref, qd_ref, kd_ref, at_ref, gc_ref, o_ref, st_ref, s_ref):
        @pl.when(pl.program_id(0) == 0)
        def _():
            s_ref[...] = jnp.zeros_like(s_ref)

        state = tuple(s_ref[_hsl(h), :] for h in range(GDN_H))
        egl = _each(jnp.exp, _cols(gc_ref[GDN_C - 1:GDN_C, :], 0))
        o, new = _gdn_step(state, _heads(w_ref), _heads(u_ref), _heads(qd_ref), _heads(kd_ref),
                           _heads(at_ref, width=GDN_C), egl)
        for h in range(GDN_H):
            st_ref[_hsl(h), :] = state[h]
            o_ref[:, _hsl(h)] = o[h]
            s_ref[_hsl(h), :] = new[h]

    hsp = pl.BlockSpec((GDN_C, GDN_W), lambda i: (i, 0))
    return pl.pallas_call(
        body, name=name, grid=(N,),
        in_specs=[hsp] * 5 + [pl.BlockSpec((GDN_C, 128), lambda i: (i, 0))],
        out_specs=[hsp, pl.BlockSpec((None, GDN_W, GDN_DH), lambda i: (i, 0, 0))],
        out_shape=[jax.ShapeDtypeStruct((S, GDN_W), F32), jax.ShapeDtypeStruct((N, GDN_W, GDN_DH), F32)],
        scratch_shapes=[pltpu.VMEM((GDN_W, GDN_DH), F32)],
        compiler_params=_cp("arbitrary"),
    )(w, u, qd, kd, attn, gc)


def _gdn_scan_bwd(w, u, qd, kd, attn, gc, states, o, proj, dmixed, nw, name):
    S = w.shape[0]
    N = S // GDN_C

    def body(w_ref, u_ref, qd_ref, kd_ref, at_ref, gc_ref, st_ref, o_ref, z_ref, dm_ref, nw_ref,
             dw_ref, du_ref, dqd_ref, dkd_ref, dat_ref, dgl_ref, dz_ref, gnw_ref, ds_ref):
        @pl.when(pl.program_id(0) == 0)
        def _():
            ds_ref[...] = jnp.zeros_like(ds_ref)
            gnw_ref[...] = jnp.zeros_like(gnw_ref)

        nw = nw_ref[...]
        _, vjp_n = jax.vjp(lambda o, z, w_: _each(lambda a, b: _gated_norm(a, b, w_), o, z), _heads(o_ref), _heads(z_ref), nw)
        do, dz, dnw = vjp_n(_heads(dm_ref))
        state = tuple(st_ref[_hsl(h), :] for h in range(GDN_H))
        egl = _each(jnp.exp, _cols(gc_ref[GDN_C - 1:GDN_C, :], 0))
        _, vjp_s = jax.vjp(_gdn_step, state, _heads(w_ref), _heads(u_ref), _heads(qd_ref), _heads(kd_ref),
                           _heads(at_ref, width=GDN_C), egl)
        ds, dw, du, dqd, dkd, dat, degl = vjp_s((do, tuple(ds_ref[_hsl(h), :] for h in range(GDN_H))))
        dgl = jnp.zeros((8, 128), F32)
        for h in range(GDN_H):
            dz_ref[:, _hsl(h)] = dz[h].astype(dz_ref.dtype)
            ds_ref[_hsl(h), :] = ds[h]
            dw_ref[:, _hsl(h)] = dw[h]
            du_ref[:, _hsl(h)] = du[h]
            dqd_ref[:, _hsl(h)] = dqd[h]
            dkd_ref[:, _hsl(h)] = dkd[h]
            dat_ref[:, _hsl(h)] = _pad_lanes(dat[h])
            dgl = jnp.where(_lanes(dgl.shape) == h, degl[h] * egl[h], dgl)
        dgl_ref[...] = dgl
        gnw_ref[...] += dnw

    rev = lambda i: (N - 1 - i, 0)
    hsp = pl.BlockSpec((GDN_C, GDN_W), rev)
    gsp = pl.BlockSpec((GDN_C, 128), rev)
    vec = pl.BlockSpec((1, GDN_DH), lambda i: (0, 0))
    hshape = jax.ShapeDtypeStruct((S, GDN_W), F32)
    return pl.pallas_call(
        body, name=name, grid=(N,),
        in_specs=[hsp] * 5 + [gsp, pl.BlockSpec((None, GDN_W, GDN_DH), lambda i: (N - 1 - i, 0, 0)), hsp,
                              pl.BlockSpec((GDN_C, GDN_W), lambda i: (N - 1 - i, PZ // GDN_W)), hsp, vec],
        out_specs=[hsp] * 5 + [pl.BlockSpec((8, 128), rev), hsp, vec],
        out_shape=[hshape] * 5 + [jax.ShapeDtypeStruct((N * 8, 128), F32), jax.ShapeDtypeStruct((S, GDN_W), BF16),
                                  jax.ShapeDtypeStruct((1, GDN_DH), F32)],
        scratch_shapes=[pltpu.VMEM((GDN_W, GDN_DH), F32)],
        compiler_params=_cp("arbitrary"),
    )(w, u, qd, kd, attn, gc, states, o, proj, dmixed, nw)


def _gdn_prep_bwd(qkv, proj, alog, dtb, tinv, dw, du, dqd, dkd, dat, dgl, name):
    S = qkv.shape[0]
    N = S // GDN_C

    def body(qkv_ref, ab_ref, al_ref, dt_ref, ti_ref, dw_ref, du_ref, dqd_ref, dkd_ref, dat_ref, dgl_ref,
             dqkv_ref, dab_ref, gal_ref, gdt_ref):
        @pl.when(pl.program_id(0) == 0)
        def _():
            gal_ref[...] = jnp.zeros_like(gal_ref)
            gdt_ref[...] = jnp.zeros_like(gdt_ref)

        ab = ab_ref[...]
        glog, vjp_g = jax.vjp(lambda a, al, dt: _each(_gdn_glog, a, al, dt), _cols(ab, 0), _cols(al_ref[...], 0),
                              _cols(dt_ref[...], 0))
        dec = _each(_gdn_decay, glog)
        gcol, dmat = _each(lambda d: d[0], dec), _each(lambda d: d[1], dec)
        _, vjp_c, _ = jax.vjp(functools.partial(_gdn_chunk, t_saved=_heads(ti_ref, width=GDN_C)), _heads(qkv_ref),
                              _heads(qkv_ref, GDN_H), _heads(qkv_ref, 2 * GDN_H), _cols(ab, GDN_H), gcol, dmat, has_aux=True)
        dqa, dka, dva, dbt, dgcol, dd = vjp_c((_heads(dw_ref), _heads(du_ref), _heads(dqd_ref), _heads(dkd_ref),
                                               _heads(dat_ref, width=GDN_C)))
        last = _rows(dgcol[0].shape) == GDN_C - 1
        dgcol = _each(lambda d, g: d + jnp.where(last, g, 0.0), dgcol, _cols(dgl_ref[0:1, :], 0))
        da_col, dal, ddt = vjp_g(_each(_gdn_decay_bwd, dgcol, dd))
        dab = jnp.zeros((GDN_C, 128), F32)
        gal = jnp.zeros((1, 128), F32)
        gdt = jnp.zeros((1, 128), F32)
        for h in range(GDN_H):
            dqkv_ref[:, _hsl(h)] = dqa[h]
            dqkv_ref[:, _hsl(GDN_H + h)] = dka[h]
            dqkv_ref[:, _hsl(2 * GDN_H + h)] = dva[h]
            ln = _lanes(dab.shape)
            dab = dab + jnp.where(ln == h, da_col[h], 0.0) + jnp.where(ln == GDN_H + h, dbt[h], 0.0)
            l1 = _lanes(gal.shape)
            gal = gal + jnp.where(l1 == h, dal[h], 0.0)
            gdt = gdt + jnp.where(l1 == h, ddt[h], 0.0)
        dab_ref[...] = dab.astype(dab_ref.dtype)
        gal_ref[...] += gal
        gdt_ref[...] += gdt

    vec = pl.BlockSpec((1, 128), lambda i: (0, 0))
    hsp = pl.BlockSpec((GDN_C, GDN_W), lambda i: (i, 0))
    qsp = pl.BlockSpec((GDN_C, 3 * GDN_W), lambda i: (i, 0))
    return pl.pallas_call(
        body, name=name, grid=(N,),
        in_specs=[qsp, pl.BlockSpec((GDN_C, 128), lambda i: (i, PAB // 128)), vec, vec] + [hsp] * 6
        + [pl.BlockSpec((8, 128), lambda i: (i, 0))],
        out_specs=[qsp, pl.BlockSpec((GDN_C, 128), lambda i: (i, 0)), vec, vec],
        out_shape=[jax.ShapeDtypeStruct((S, 3 * GDN_W), F32), jax.ShapeDtypeStruct((S, 128), BF16),
                   jax.ShapeDtypeStruct((1, 128), F32), jax.ShapeDtypeStruct((1, 128), F32)],
        compiler_params=_cp("arbitrary"),
    )(qkv, proj, alog, dtb, tinv, dw, du, dqd, dkd, dat, dgl)


@jax.custom_vjp
def _expm1(x):
    u = jnp.exp(x)
    lu = jnp.log(u)
    small = (u - 1.0) * x / jnp.where(u == 1.0, 1.0, lu)
    small = jnp.where(u == 1.0, x, small)
    return jnp.where(jnp.abs(x) < 0.5, small, u - 1.0)


def _expm1_fwd(x):
    return _expm1(x), jnp.exp(x)


def _expm1_bwd(ex, g):
    return (g * ex,)


_expm1.defvjp(_expm1_fwd, _expm1_bwd)


def _lru_gates(xc, wa, ba, wx, bx, lam, first):
    r = jax.nn.sigmoid(_mm(xc, wa) + ba)
    i = jax.nn.sigmoid(_mm(xc, wx) + bx)
    log_a = -LRU_C * r * jax.nn.softplus(-lam)
    mult = jnp.sqrt(-_expm1(2.0 * log_a))
    mult = jnp.where(first, 1.0, mult)
    return jnp.exp(log_a), mult * i * xc


def _scan_fwd(a, b):
    T = a.shape[0]
    rows = _rows(a.shape)
    s = 1
    while s < T:
        ok = rows >= s
        b = a * jnp.where(ok, _down(b, s), 0.0) + b
        a = a * jnp.where(ok, _down(a, s), 1.0)
        s *= 2
    return a, b


def _scan_rev(a, b):
    T = a.shape[0]
    rows = _rows(a.shape)
    s = 1
    while s < T:
        ok = rows + s < T
        b = a * jnp.where(ok, _up(b, s), 0.0) + b
        a = a * jnp.where(ok, _up(a, s), 1.0)
        s *= 2
    return b


def _bsl(j):
    return slice(j * LRU_BD, (j + 1) * LRU_BD)


def _lru_tile(S):
    return _row_tile(S, 256)


def _lru_fwd(proj, conv_w, conv_b, wa, ba, wx, bx, lam, name):
    S = proj.shape[0]
    T = _lru_tile(S)
    taps = conv_w.shape[0]
    r = T // HALO

    def body(x_ref, xp_ref, cw_ref, cb_ref, wa_ref, ba_ref, wx_ref, bx_ref, lam_ref, h_ref, carry_ref):
        i = pl.program_id(0)

        @pl.when(i == 0)
        def _():
            carry_ref[...] = jnp.zeros_like(carry_ref)

        ext = jnp.concatenate([jnp.where(i > 0, xp_ref[...], 0.0), x_ref[...]], axis=0)
        xc = _conv_rows(ext, cw_ref, taps)[HALO:] + cb_ref[...]
        first = (i * T + _rows((T, LRU_BD))) == 0
        for j in range(LRU_NB):
            a, b = _lru_gates(xc[:, _bsl(j)], wa_ref[j], ba_ref[:, _bsl(j)], wx_ref[j], bx_ref[:, _bsl(j)],
                              lam_ref[:, _bsl(j)], first=first)
            pa, hb = _scan_fwd(a, b)
            h_ref[:, _bsl(j)] = pa * carry_ref[0:1, _bsl(j)] + hb
            carry_ref[0:1, _bsl(j)] = h_ref[T - 1:T, _bsl(j)]

    vec = pl.BlockSpec((1, LRU_W), lambda i: (0, 0))
    wsp = pl.BlockSpec((LRU_NB, LRU_BD, LRU_BD), lambda i: (0, 0, 0))
    return pl.pallas_call(
        body, name=name, grid=(S // T,),
        in_specs=[pl.BlockSpec((T, LRU_W), lambda i: (i, PXR // LRU_W)),
                  pl.BlockSpec((HALO, LRU_W), lambda i: (jnp.maximum(i * r - 1, 0), PXR // LRU_W)),
                  pl.BlockSpec((taps, LRU_W), lambda i: (0, 0)), vec, wsp, vec, wsp, vec, vec],
        out_specs=pl.BlockSpec((T, LRU_W), lambda i: (i, 0)), out_shape=jax.ShapeDtypeStruct((S, LRU_W), F32),
        scratch_shapes=[pltpu.VMEM((8, LRU_W), F32)],
        compiler_params=_cp("arbitrary"),
    )(proj, proj, conv_w, conv_b.reshape(1, LRU_W), wa, ba.reshape(1, LRU_W), wx, bx.reshape(1, LRU_W), lam.reshape(1, LRU_W))


def _lru_bwd(proj, hl, dmixed, conv_w, conv_b, wa, ba, wx, bx, lam, name):
    S = proj.shape[0]
    T = _lru_tile(S)
    n = S // T
    taps = conv_w.shape[0]
    r = T // HALO
    mb = 768 // LRU_W

    def body(x_ref, xp_ref, g_ref, h_ref, hp_ref, dy_ref, cw_ref, cb_ref, wa_ref, ba_ref, wx_ref, bx_ref, lam_ref,
             dx_ref, dg_ref, gcw_ref, gcb_ref, gwa_ref, gba_ref, gwx_ref, gbx_ref, glam_ref, carry_ref, dxc_ref, nxt_ref):
        s = pl.program_id(0)
        i = n - 1 - s

        @pl.when(s == 0)
        def _():
            carry_ref[...] = jnp.zeros_like(carry_ref)
            nxt_ref[...] = jnp.zeros_like(nxt_ref)
            for ref in (gcw_ref, gcb_ref, gwa_ref, gba_ref, gwx_ref, gbx_ref, glam_ref):
                ref[...] = jnp.zeros_like(ref)

        ext = jnp.concatenate([jnp.where(i > 0, xp_ref[...], 0.0), x_ref[...]], axis=0)
        xc = _conv_rows(ext, cw_ref, taps)[HALO:] + cb_ref[...]
        rows = _rows((T, LRU_BD))
        first = (i * T + rows) == 0
        h_before = jnp.where(i > 0, hp_ref[HALO - 1:HALO, :], 0.0)
        for j in range(LRU_NB):
            sl = _bsl(j)
            (a, _), vjp_g = jax.vjp(functools.partial(_lru_gates, first=first), xc[:, sl], wa_ref[j], ba_ref[:, sl],
                                    wx_ref[j], bx_ref[:, sl], lam_ref[:, sl])
            gelu_g, vjp_a = jax.vjp(jax.nn.gelu, g_ref[:, sl])
            h = h_ref[:, sl]
            dy = dy_ref[:, sl]
            dg_ref[:, sl] = vjp_a(dy * h)[0].astype(dg_ref.dtype)
            b_rev = dy * gelu_g + jnp.where(rows == T - 1, carry_ref[0:1, sl], 0.0)
            a_rev = jnp.where(rows == T - 1, 0.0, _up(a, 1))
            dh = _scan_rev(a_rev, b_rev)
            carry_ref[:, sl] = (a * dh)[:HALO]
            h_prev = jnp.where(rows == 0, h_before[:, sl], _down(h, 1))
            dxc, dwa, dba, dwx, dbx, dlam = vjp_g((dh * h_prev, dh))
            dxc_ref[:, sl] = dxc
            gwa_ref[j] += dwa
            gwx_ref[j] += dwx
            gba_ref[:, sl] += dba
            gbx_ref[:, sl] += dbx
            glam_ref[:, sl] += dlam
        dxc = dxc_ref[...]
        d_ext = jnp.concatenate([dxc, nxt_ref[...]], axis=0)
        dx_ref[...] = _conv_t_rows(d_ext, cw_ref, taps)[:T].astype(dx_ref.dtype)
        nxt_ref[...] = dxc[:HALO]
        gcb_ref[...] += jnp.sum(dxc, axis=0, keepdims=True)
        for k in range(taps):
            gcw_ref[taps - 1 - k:taps - k, :] += jnp.sum(dxc * _down(ext, k)[HALO:], axis=0, keepdims=True)

    vec = pl.BlockSpec((1, LRU_W), lambda s: (0, 0))
    wsp = pl.BlockSpec((LRU_NB, LRU_BD, LRU_BD), lambda s: (0, 0, 0))
    cwsp = pl.BlockSpec((taps, LRU_W), lambda s: (0, 0))

    def main(cb):
        return pl.BlockSpec((T, LRU_W), lambda s: (n - 1 - s, cb))

    def prev(cb):
        return pl.BlockSpec((HALO, LRU_W), lambda s: (jnp.maximum((n - 1 - s) * r - 1, 0), cb))

    vshape = jax.ShapeDtypeStruct((1, LRU_W), F32)
    wshape = jax.ShapeDtypeStruct((LRU_NB, LRU_BD, LRU_BD), F32)
    return pl.pallas_call(
        body, name=name, grid=(n,),
        in_specs=[main(PXR // LRU_W), prev(PXR // LRU_W), main(PGR // LRU_W), main(0), prev(0), main(mb),
                  cwsp, vec, wsp, vec, wsp, vec, vec],
        out_specs=[main(0), main(0), cwsp, vec, wsp, vec, wsp, vec, vec],
        out_shape=[jax.ShapeDtypeStruct((S, LRU_W), BF16), jax.ShapeDtypeStruct((S, LRU_W), BF16),
                   jax.ShapeDtypeStruct((taps, LRU_W), F32), vshape, wshape, vshape, wshape, vshape, vshape],
        scratch_shapes=[pltpu.VMEM((8, LRU_W), F32), pltpu.VMEM((T, LRU_W), F32), pltpu.VMEM((HALO, LRU_W), F32)],
        compiler_params=_cp("arbitrary"),
    )(proj, proj, proj, hl, hl, dmixed, conv_w, conv_b.reshape(1, LRU_W), wa, ba.reshape(1, LRU_W), wx,
      bx.reshape(1, LRU_W), lam.reshape(1, LRU_W))


def _mix_out(o, proj, hl, y_pool, nw, name):
    S = o.shape[0]
    T = _row_tile(S)

    def body(o_ref, z_ref, h_ref, g_ref, p_ref, nw_ref, m_ref):
        for h in range(GDN_H):
            m_ref[:, _hsl(h)] = _gated_norm(o_ref[:, _hsl(h)], z_ref[:, _hsl(h)], nw_ref[...]).astype(m_ref.dtype)
        m_ref[:, GDN_W:GDN_W + LRU_W] = (h_ref[...] * jax.nn.gelu(g_ref[...])).astype(m_ref.dtype)
        m_ref[:, GDN_W + LRU_W:] = p_ref[...].astype(m_ref.dtype)

    row = pl.BlockSpec((T, GDN_W), lambda i: (i, 0))
    return pl.pallas_call(
        body, name=name, grid=(S // T,),
        in_specs=[row, pl.BlockSpec((T, GDN_W), lambda i: (i, PZ // GDN_W)), row,
                  pl.BlockSpec((T, LRU_W), lambda i: (i, PGR // LRU_W)), pl.BlockSpec((T, POOL_W), lambda i: (i, 0)),
                  pl.BlockSpec((1, GDN_DH), lambda i: (0, 0))],
        out_specs=pl.BlockSpec((T, D_MODEL), lambda i: (i, 0)), out_shape=jax.ShapeDtypeStruct((S, D_MODEL), BF16),
        compiler_params=_cp("parallel"),
    )(o, proj, hl, proj, y_pool, nw)


def _as2d(a):
    return a.reshape(-1, a.shape[-1])


def _ew_rows(rows, cols):
    t = rows
    while t * cols * 4 > (2 << 20) and t % 16 == 0:
        t //= 2
    return t


def _rs_rows(rows, cols, budget=2 << 20):
    t = rows
    while t * cols * 4 > budget and t % 32 == 0:
        t //= 2
    return t


def _adamw(w, g, m, v, name):
    shape = w.shape
    w2, g2, m2, v2 = _as2d(w), _as2d(g), _as2d(m), _as2d(v)
    rows, cols = w2.shape
    t = _ew_rows(rows, cols)

    def body(w_ref, g_ref, m_ref, v_ref, d_ref, nm_ref, nv_ref):
        gr = g_ref[...]
        nm = ADAM_B1 * m_ref[...] + (1.0 - ADAM_B1) * gr
        nv = ADAM_B2 * v_ref[...] + (1.0 - ADAM_B2) * (gr * gr)
        m_hat = nm / (1.0 - ADAM_B1 ** ADAM_STEP)
        v_hat = nv / (1.0 - ADAM_B2 ** ADAM_STEP)
        d_ref[...] = -ADAM_LR * (m_hat / (jnp.sqrt(v_hat) + ADAM_EPS) + ADAM_WD * w_ref[...])
        nm_ref[...] = nm
        nv_ref[...] = nv

    sp = pl.BlockSpec((t, cols), lambda i: (i, 0))
    sh = jax.ShapeDtypeStruct((rows, cols), F32)
    d, nm, nv = pl.pallas_call(body, name=name, grid=(rows // t,), in_specs=[sp] * 4, out_specs=[sp] * 3,
                               out_shape=[sh] * 3, compiler_params=_cp("parallel"))(w2, g2, m2, v2)
    return d.reshape(shape), nm.reshape(shape), nv.reshape(shape)


def _place():
    return lax.axis_index("x"), lax.axis_index("y"), lax.axis_index("c")


def _gather_weights(arrs, name):
    n = len(arrs)

    def body(*refs):
        outs = refs[n:2 * n]
        send, recv = refs[2 * n:]
        x, y, c = _place()
        s_me, s_x, s_y, s_d = 2 * x + y, 2 * (1 - x) + y, 2 * x + (1 - y), 2 * (1 - x) + (1 - y)
        xpeer, ypeer, sib = (1 - x, y, c), (x, 1 - y, c), (x, y, 1 - c)

        def rc(k, t, src, dst, to):
            return pltpu.make_async_remote_copy(src_ref=src, dst_ref=dst, send_sem=send.at[k, t], recv_sem=recv.at[k, t],
                                                device_id=to, device_id_type=MESH)

        def piece(k, s, top):
            rq = outs[k].shape[2] // 2
            return outs[k].at[s, c, pl.ds(0 if top else rq, rq)]

        sent = []

        def start(k, t, ref, to):
            cp = rc(k, t, ref, ref, to)
            cp.start()
            sent.append(cp)

        for k in range(n):
            start(k, 0, outs[k].at[s_me, c], xpeer)
            start(k, 1, outs[k].at[s_me, c], ypeer)
        for k in range(n):
            got = outs[k].at[s_x, c]
            rc(k, 0, got, got, xpeer).wait_recv()
            start(k, 2, piece(k, s_x, True), ypeer)
            start(k, 3, got, sib)
        for k in range(n):
            got = outs[k].at[s_y, c]
            rc(k, 1, got, got, ypeer).wait_recv()
            start(k, 6, piece(k, s_y, False), xpeer)
            start(k, 4, got, sib)
        for k in range(n):
            top, bottom = piece(k, s_d, True), piece(k, s_d, False)
            rc(k, 2, top, top, ypeer).wait_recv()
            rc(k, 6, bottom, bottom, xpeer).wait_recv()
            start(k, 5, outs[k].at[s_d, c], sib)
        for k in range(n):
            for t, s in ((3, s_x), (4, s_y), (5, s_d)):
                got = outs[k].at[s, 1 - c]
                rc(k, t, got, got, sib).wait_recv()
        for cp in sent:
            cp.wait_send()

    return pl.pallas_call(
        body, name=name, in_specs=[HBM] * n, out_specs=[HBM] * n,
        out_shape=[jax.ShapeDtypeStruct(a.shape, a.dtype) for a in arrs],
        input_output_aliases={k: k for k in range(n)},
        scratch_shapes=[pltpu.SemaphoreType.DMA((n, 7)), pltpu.SemaphoreType.DMA((n, 7))],
    )(*arrs)


def _share_halves(arrs, name):
    n = len(arrs)

    def body(*refs):
        outs = refs[n:2 * n]
        send, recv = refs[2 * n:]
        x, y, c = _place()
        cps = []
        for k in range(n):
            mine = outs[k].at[:, c]
            cp = pltpu.make_async_remote_copy(src_ref=mine, dst_ref=mine, send_sem=send.at[k], recv_sem=recv.at[k],
                                              device_id=(x, y, 1 - c), device_id_type=MESH)
            cp.start()
            cps.append(cp)
        for k in range(n):
            got = outs[k].at[:, 1 - c]
            pltpu.make_async_remote_copy(src_ref=got, dst_ref=got, send_sem=send.at[k], recv_sem=recv.at[k],
                                         device_id=(x, y, 1 - c), device_id_type=MESH).wait_recv()
        for cp in cps:
            cp.wait_send()

    return pl.pallas_call(
        body, name=name, in_specs=[HBM] * n, out_specs=[HBM] * n,
        out_shape=[jax.ShapeDtypeStruct(a.shape, a.dtype) for a in arrs],
        input_output_aliases={k: k for k in range(n)},
        scratch_shapes=[pltpu.SemaphoreType.DMA((n,)), pltpu.SemaphoreType.DMA((n,))],
    )(*arrs)


_REL = tuple((dx, dy, dc) for dx in (0, 1) for dy in (0, 1) for dc in (0, 1))[1:]
SEM = pl.BlockSpec(memory_space=pltpu.SEMAPHORE)
DATAFLOW = pltpu.SideEffectType.DATAFLOW_SIDE_EFFECTING


def _flip(v, d):
    return 1 - v if d else v


def _rs_direct_copies(srcs, land, send, recv):
    x, y, c = _place()
    cps = []
    for k in range(len(srcs)):
        for r, (dx, dy, dc) in enumerate(_REL):
            px, py, pc = _flip(x, dx), _flip(y, dy), _flip(c, dc)
            cps.append(pltpu.make_async_remote_copy(
                src_ref=srcs[k].at[2 * px + py, pc], dst_ref=land[k].at[r], send_sem=send.at[k * len(_REL) + r],
                recv_sem=recv.at[k * len(_REL) + r], device_id=(px, py, pc), device_id_type=MESH))
    return cps


def _rs_direct_start(grads, thru, name):
    n = len(grads)
    lands = [pltpu.with_memory_space_constraint(lax.empty((len(_REL),) + g.shape[2:], g.dtype), pltpu.HBM) for g in grads]

    def body(*refs):
        for cp in _rs_direct_copies(refs[:n], refs[n + 1:2 * n + 1], refs[2 * n + 1], refs[2 * n + 2]):
            cp.start()

    sems = pltpu.SemaphoreType.DMA((n * len(_REL),))
    keep = [pltpu.HBM(a.shape, a.dtype) for a in (*grads, thru, *lands)]
    out = pl.pallas_call(
        body, name=name, in_specs=[HBM] * (2 * n + 1), out_specs=(SEM, SEM) + (HBM,) * (2 * n + 1),
        out_shape=(sems, sems, *keep), input_output_aliases={i: 2 + i for i in range(2 * n + 1)},
        compiler_params=pltpu.CompilerParams(has_side_effects=DATAFLOW),
    )(*[pltpu.with_memory_space_constraint(a, pltpu.HBM) for a in (*grads, thru)], *lands)
    return out[0], out[1], out[2:2 + n], out[2 + n], out[3 + n:]


def _rs_direct_wait(send, recv, grads, lands, after, name):
    n = len(grads)
    after = list(after) if isinstance(after, (list, tuple)) else [after]

    def body(*refs):
        for cp in _rs_direct_copies(refs[:n], refs[n:2 * n], refs[2 * n], refs[2 * n + 1]):
            cp.wait_send()
            cp.wait_recv()

    keep = [pltpu.HBM(a.shape, a.dtype) for a in (*grads, *lands)]
    out = pl.pallas_call(
        body, name=name, in_specs=[HBM] * (2 * n) + [SEM, SEM] + [pl.BlockSpec(memory_space=pl.ANY)] * len(after),
        out_specs=(HBM,) * (2 * n), out_shape=tuple(keep), input_output_aliases={i: i for i in range(2 * n)},
        compiler_params=pltpu.CompilerParams(has_side_effects=DATAFLOW),
    )(*grads, *lands, send, recv, *after)
    return out[:n], out[n:]


_CHIPS = ((1, 0), (0, 1), (1, 1))


def _gather_ici_copies(bufs, send, recv):
    x, y, c = _place()
    cps = []
    for k in range(len(bufs)):
        mine = bufs[k].at[2 * x + y, c]
        for j, (dx, dy) in enumerate(_CHIPS):
            cps.append(pltpu.make_async_remote_copy(
                src_ref=mine, dst_ref=mine, send_sem=send.at[k * len(_CHIPS) + j], recv_sem=recv.at[k * len(_CHIPS) + j],
                device_id=(_flip(x, dx), _flip(y, dy), c), device_id_type=MESH))
    return cps


def _gather_d2d_copies(bufs, send, recv):
    x, y, c = _place()
    cps = []
    for k in range(len(bufs)):
        for j, (dx, dy) in enumerate(_CHIPS):
            got = bufs[k].at[2 * _flip(x, dx) + _flip(y, dy), c]
            cps.append(pltpu.make_async_remote_copy(
                src_ref=got, dst_ref=got, send_sem=send.at[k * len(_CHIPS) + j], recv_sem=recv.at[k * len(_CHIPS) + j],
                device_id=(x, y, 1 - c), device_id_type=MESH))
    return cps


def _copies_start(bufs, thru, copies, name):
    n = len(bufs)

    def body(*refs):
        for cp in copies(refs[:n], refs[n + 1], refs[n + 2]):
            cp.start()

    sems = pltpu.SemaphoreType.DMA((n * len(_CHIPS),))
    out = pl.pallas_call(
        body, name=name, in_specs=[HBM] * (n + 1), out_specs=(SEM, SEM) + (HBM,) * (n + 1),
        out_shape=(sems, sems, *[pltpu.HBM(a.shape, a.dtype) for a in (*bufs, thru)]),
        input_output_aliases={i: 2 + i for i in range(n + 1)},
        compiler_params=pltpu.CompilerParams(has_side_effects=DATAFLOW),
    )(*[pltpu.with_memory_space_constraint(a, pltpu.HBM) for a in (*bufs, thru)])
    return out[0], out[1], out[2:2 + n], out[2 + n]


def _copies_wait(send, recv, bufs, after, copies, name):
    n = len(bufs)

    def body(*refs):
        for cp in copies(refs[:n], refs[n], refs[n + 1]):
            cp.wait_send()
            cp.wait_recv()

    return pl.pallas_call(
        body, name=name, in_specs=[HBM] * n + [SEM, SEM, pl.BlockSpec(memory_space=pl.ANY)], out_specs=(HBM,) * n,
        out_shape=tuple(pltpu.HBM(a.shape, a.dtype) for a in bufs), input_output_aliases={i: i for i in range(n)},
        compiler_params=pltpu.CompilerParams(has_side_effects=DATAFLOW),
    )(*bufs, send, recv, after)


def _rs_direct_sum(grad, land, layer, name, into=None):
    _, _, rows, cols = grad.shape
    t = _rs_rows(rows, cols, 6 << 20)
    npieces = len(_REL) + 1

    def body(g_ref, l_ref, *rest):
        o_ref, acc_ref = rest[-2], rest[-1]
        j = pl.program_id(1)

        @pl.when(j == 0)
        def _():
            acc_ref[...] = g_ref[...].astype(F32)

        @pl.when(j > 0)
        def _():
            acc_ref[...] += l_ref[...].astype(F32)

        @pl.when(j == npieces - 1)
        def _():
            o_ref[...] = acc_ref[...]

    def mine(i, j):
        x, y, c = _place()
        return (2 * x + y, c, i, 0)

    in_specs = [pl.BlockSpec((None, None, t, cols), mine),
                pl.BlockSpec((None, t, cols), lambda i, j: (jnp.maximum(j - 1, 0), i, 0))]
    args = [grad, land]
    if into is not None:
        in_specs.append(pl.BlockSpec(memory_space=pl.ANY))
        args.append(into)
    return pl.pallas_call(
        body, name=name, grid=(rows // t, npieces), in_specs=in_specs,
        out_specs=pl.BlockSpec((None, None, t, cols), lambda i, j: (layer, lax.axis_index("c"), i, 0)),
        scratch_shapes=[pltpu.VMEM((t, cols), F32)],
        out_shape=jax.ShapeDtypeStruct((2, 2, rows, cols), F32), input_output_aliases={} if into is None else {2: 0},
        compiler_params=_cp("parallel", "arbitrary"),
    )(*args)


def _ar_copies(buf, land, send, recv):
    x, y, c = _place()
    return [pltpu.make_async_remote_copy(src_ref=buf, dst_ref=land.at[r], send_sem=send.at[r], recv_sem=recv.at[r],
                                         device_id=(_flip(x, dx), _flip(y, dy), _flip(c, dc)), device_id_type=MESH)
            for r, (dx, dy, dc) in enumerate(_REL)]


def _ar_start(buf, name):
    land = pltpu.with_memory_space_constraint(lax.empty((len(_REL),) + buf.shape, buf.dtype), pltpu.HBM)

    def body(buf_ref, land_ref, send, recv, *_):
        for cp in _ar_copies(buf_ref, land_ref, send, recv):
            cp.start()

    sems = pltpu.SemaphoreType.DMA((len(_REL),))
    return pl.pallas_call(
        body, name=name, in_specs=[HBM, HBM], out_specs=(SEM, SEM, HBM, HBM),
        out_shape=(sems, sems, pltpu.HBM(buf.shape, buf.dtype), pltpu.HBM(land.shape, land.dtype)),
        input_output_aliases={0: 2, 1: 3}, compiler_params=pltpu.CompilerParams(has_side_effects=DATAFLOW),
    )(pltpu.with_memory_space_constraint(buf, pltpu.HBM), land)


def _ar_wait(send, recv, buf, land, after, name):
    def body(buf_ref, land_ref, send_ref, recv_ref, *_):
        for cp in _ar_copies(buf_ref, land_ref, send_ref, recv_ref):
            cp.wait_send()
            cp.wait_recv()

    return pl.pallas_call(
        body, name=name, in_specs=[HBM, HBM, SEM, SEM, pl.BlockSpec(memory_space=pl.ANY)], out_specs=(HBM, HBM),
        out_shape=(pltpu.HBM(buf.shape, buf.dtype), pltpu.HBM(land.shape, land.dtype)), input_output_aliases={0: 0, 1: 1},
        compiler_params=pltpu.CompilerParams(has_side_effects=DATAFLOW),
    )(buf, land, send, recv, after)


def _ar_sum(buf, land, name):
    rows, cols = buf.shape
    t = _rs_rows(rows, cols)

    def slot(i, j):
        x, y, c = _place()
        xd, yd, cd = j // 4, (j // 2) % 2, j % 2
        rel = 4 * (x + xd - 2 * x * xd) + 2 * (y + yd - 2 * y * yd) + (c + cd - 2 * c * cd)
        return (jnp.maximum(rel - 1, 0), i, 0)

    def body(b_ref, l_ref, o_ref, acc_ref):
        j = pl.program_id(1)
        x, y, c = _place()
        val = jnp.where(j == 4 * x + 2 * y + c, b_ref[...], l_ref[...])

        @pl.when(j == 0)
        def _():
            acc_ref[...] = val

        @pl.when(j > 0)
        def _():
            acc_ref[...] += val

        @pl.when(j == len(_REL))
        def _():
            o_ref[...] = acc_ref[...]

    sp = pl.BlockSpec((t, cols), lambda i, j: (i, 0))
    return pl.pallas_call(
        body, name=name, grid=(rows // t, len(_REL) + 1), in_specs=[sp, pl.BlockSpec((None, t, cols), slot)], out_specs=sp,
        out_shape=jax.ShapeDtypeStruct((rows, cols), F32), scratch_shapes=[pltpu.VMEM((t, cols), F32)],
        compiler_params=_cp("parallel", "arbitrary"),
    )(buf, land)


def _pad128(v):
    return jnp.zeros((1, 128), F32).at[0, :v.shape[0]].set(v)


def _layer_fwd(l, x, p, hooks=None):
    hooks = hooks or {}
    h1 = _rms_fwd(x, p["norm1_w"], f"rms1_{l}")
    proj = _matmul(h1, p["w_in"], "nn", name=f"mm_in_{l}", tn=768)
    y_pool = _pool_fwd(proj, p["pool_w"], p["pool_b"], p["pool_scale"], f"pool_{l}")
    qkv = _gdn_conv_fwd(proj, p["gdn_conv_w"], f"gconv_{l}")
    alog, dtb = _pad128(p["gdn_a_log"]), _pad128(p["gdn_dt_bias"])
    gw, gu, gqd, gkd, gat, tinv, gc = _gdn_prep(qkv, proj, alog, dtb, f"gprep_{l}")
    o, states = _gdn_scan(gw, gu, gqd, gkd, gat, gc, f"gscan_{l}")
    if "mid" in hooks:
        o = hooks["mid"](o)
    hl = _lru_fwd(proj, p["lru_conv_w"], p["lru_conv_b"], p["lru_wa"], p["lru_ba"], p["lru_wx"], p["lru_bx"],
                  p["lru_lambda"], f"lru_{l}")
    mixed = _mix_out(o, proj, hl, y_pool, p["gdn_norm_w"].reshape(1, GDN_DH), f"mix_{l}")
    x2 = _matmul(mixed, p["w_out"], "nn", name=f"mm_out_{l}", res=x)
    h2 = _rms_fwd(x2, p["norm2_w"], f"rms2_{l}")
    if "ffn" in hooks:
        h2 = hooks["ffn"](h2)
    up = _matmul(h2, p["ffn_up"], "nn", name=f"mm_up_{l}", b_split=True)
    act = _ffn_act_fwd(up, p["ffn_conv_w"], f"ffn_{l}")
    if "down" in hooks:
        act = hooks["down"](act)
    x3 = _matmul(act, p["ffn_down"], "nn", name=f"mm_down_{l}", res=x2)
    saved = dict(x=x, h1=h1, proj=proj, qkv=qkv, gdn=(gw, gu, gqd, gkd, gat, gc), tinv=tinv, states=states, o=o, hl=hl, mixed=mixed,
                 x2=x2, h2=h2, up=up, act=act, alog=alog, dtb=dtb)
    return x3, saved


def _layer_bwd(l, dx3, p, s, after_ffn=None):
    g = {}
    dact = _matmul(dx3, p["ffn_down"], "nt", name=f"mm_ddown_{l}")
    g["ffn_down"] = _matmul(s["act"], dx3, "tn", name=f"mm_gdown_{l}", out_dtype=BF16)
    dup, g["ffn_conv_w"] = _ffn_act_bwd(s["up"], dact, p["ffn_conv_w"], f"ffn_b_{l}")
    dh2 = _matmul(dup, p["ffn_up"], "nt", name=f"mm_dup_{l}", b_split=True, tk=3072)
    g["ffn_up"] = _matmul(s["h2"], dup, "tn", name=f"mm_gup_{l}", b_split=True, o_split=4, tk=4096, out_dtype=BF16)
    dx2, g["norm2_w"] = _rms_bwd(s["x2"], p["norm2_w"], dh2, dx3, f"rms2_b_{l}")
    g["w_out"] = _matmul(s["mixed"], dx2, "tn", name=f"mm_gout_{l}", out_dtype=BF16)
    if after_ffn is not None:
        dx2 = after_ffn(dx2, g)
    dmixed = _matmul(dx2, p["w_out"], "nt", name=f"mm_dout_{l}")
    proj = s["proj"]
    du_pool, g["pool_w"], g["pool_b"], g["pool_scale"] = _pool_bwd(proj, dmixed, p["pool_w"], p["pool_b"], p["pool_scale"], f"pool_b_{l}")
    gw, gu, gqd, gkd, gat, gc = s["gdn"]
    dw, du, dqd, dkd, dat, dgl, dz, g["gdn_norm_w"] = _gdn_scan_bwd(
        gw, gu, gqd, gkd, gat, gc, s["states"], s["o"], proj, dmixed, p["gdn_norm_w"].reshape(1, GDN_DH), f"gscan_b_{l}")
    dqkv, dab, gal, gdt = _gdn_prep_bwd(s["qkv"], proj, s["alog"], s["dtb"], s["tinv"], dw, du, dqd, dkd, dat, dgl, f"gprep_b_{l}")
    g["gdn_a_log"], g["gdn_dt_bias"] = gal[0, :GDN_H], gdt[0, :GDN_H]
    dpre, g["gdn_conv_w"] = _gdn_conv_bwd(proj, dqkv, p["gdn_conv_w"], f"gconv_b_{l}")
    (dxr, dgr, g["lru_conv_w"], g["lru_conv_b"], g["lru_wa"], g["lru_ba"], g["lru_wx"], g["lru_bx"], g["lru_lambda"]) = _lru_bwd(
        proj, s["hl"], dmixed, p["lru_conv_w"], p["lru_conv_b"], p["lru_wa"], p["lru_ba"], p["lru_wx"], p["lru_bx"],
        p["lru_lambda"], f"lru_b_{l}")
    S = proj.shape[0]
    dproj = jnp.concatenate([dpre, dz, dxr, dgr, du_pool, dab, jnp.zeros((S, PCOLS - PAB - 128), BF16)], axis=1)
    dh1 = _matmul(dproj, p["w_in"], "nt", name=f"mm_din_{l}", tk=1792)
    g["w_in"] = _matmul(s["h1"], dproj, "tn", name=f"mm_gin_{l}", tn=768, tk=4096, out_dtype=BF16)
    dx, g["norm1_w"] = _rms_bwd(s["x"], p["norm1_w"], dh1, dx2, f"rms1_b_{l}")
    return dx, g


_IN_PERM = ((512, 3584), (3596, 5132), (0, 512), (3584, 3596))


def _w_in_to_proj(w):
    parts = [w[:, a:b] for a, b in _IN_PERM]
    return jnp.concatenate(parts + [jnp.zeros((w.shape[0], PCOLS - IN_COLS), w.dtype)], axis=1)


def _proj_to_w_in(g):
    return jnp.concatenate([g[:, PPOOL:PPOOL + 512], g[:, 0:3072], g[:, PAB:PAB + 12], g[:, 3072:PPOOL]], axis=1)


def _rows_to_mixed(w):
    return jnp.concatenate([w[512:], w[:512]], axis=0)


def _mixed_to_rows(g):
    return jnp.concatenate([g[1536:], g[:1536]], axis=0)


SMALL_SHARDED = ("gdn_conv_w", "lru_conv_w", "ffn_conv_w")
BIG = ("w_in", "w_out", "ffn_up", "ffn_down")
SMALL_REPLICATED = ("norm1_w", "pool_w", "pool_b", "pool_scale", "gdn_a_log", "gdn_dt_bias", "gdn_norm_w", "lru_conv_b",
                    "lru_wa", "lru_ba", "lru_wx", "lru_bx", "lru_lambda", "norm2_w")
WEIGHTS = ("norm1_w", "w_in", "pool_w", "pool_b", "pool_scale", "gdn_conv_w", "gdn_a_log", "gdn_dt_bias", "gdn_norm_w",
           "lru_conv_w", "lru_conv_b", "lru_wa", "lru_ba", "lru_wx", "lru_bx", "lru_lambda", "w_out", "norm2_w", "ffn_up",
           "ffn_conv_w", "ffn_down", "final_norm_w")
FLAT_COLS = 1024


def _pack(arrs):
    flat = jnp.concatenate([a.reshape(-1) for a in arrs])
    rows = -(-flat.shape[0] // (8 * FLAT_COLS)) * 8
    return jnp.pad(flat, (0, rows * FLAT_COLS - flat.shape[0])).reshape(rows, FLAT_COLS)


def _unpack(buf, like):
    flat = buf.reshape(-1)
    out, off = [], 0
    for a in like:
        size = 1
        for d in a.shape:
            size *= d
        out.append(flat[off:off + size].reshape(a.shape))
        off += size
    return out


def kernel(x, norm1_w, w_in, pool_w, pool_b, pool_scale, gdn_conv_w, gdn_a_log, gdn_dt_bias, gdn_norm_w, lru_conv_w, lru_conv_b, lru_wa, lru_ba, lru_wx, lru_bx, lru_lambda, w_out, norm2_w, ffn_up, ffn_conv_w, ffn_down, final_norm_w, loss_target, m_norm1_w, m_w_in, m_pool_w, m_pool_b, m_pool_scale, m_gdn_conv_w, m_gdn_a_log, m_gdn_dt_bias, m_gdn_norm_w, m_lru_conv_w, m_lru_conv_b, m_lru_wa, m_lru_ba, m_lru_wx, m_lru_bx, m_lru_lambda, m_w_out, m_norm2_w, m_ffn_up, m_ffn_conv_w, m_ffn_down, m_final_norm_w, v_norm1_w, v_w_in, v_pool_w, v_pool_b, v_pool_scale, v_gdn_conv_w, v_gdn_a_log, v_gdn_dt_bias, v_gdn_norm_w, v_lru_conv_w, v_lru_conv_b, v_lru_wa, v_lru_ba, v_lru_wx, v_lru_bx, v_lru_lambda, v_w_out, v_norm2_w, v_ffn_up, v_ffn_conv_w, v_ffn_down, v_final_norm_w):
    W = dict(norm1_w=norm1_w, w_in=w_in, pool_w=pool_w, pool_b=pool_b, pool_scale=pool_scale, gdn_conv_w=gdn_conv_w,
             gdn_a_log=gdn_a_log, gdn_dt_bias=gdn_dt_bias, gdn_norm_w=gdn_norm_w, lru_conv_w=lru_conv_w, lru_conv_b=lru_conv_b,
             lru_wa=lru_wa, lru_ba=lru_ba, lru_wx=lru_wx, lru_bx=lru_bx, lru_lambda=lru_lambda, w_out=w_out, norm2_w=norm2_w,
             ffn_up=ffn_up, ffn_conv_w=ffn_conv_w, ffn_down=ffn_down, final_norm_w=final_norm_w)
    M = dict(norm1_w=m_norm1_w, w_in=m_w_in, pool_w=m_pool_w, pool_b=m_pool_b, pool_scale=m_pool_scale, gdn_conv_w=m_gdn_conv_w,
             gdn_a_log=m_gdn_a_log, gdn_dt_bias=m_gdn_dt_bias, gdn_norm_w=m_gdn_norm_w, lru_conv_w=m_lru_conv_w,
             lru_conv_b=m_lru_conv_b, lru_wa=m_lru_wa, lru_ba=m_lru_ba, lru_wx=m_lru_wx, lru_bx=m_lru_bx, lru_lambda=m_lru_lambda,
             w_out=m_w_out, norm2_w=m_norm2_w, ffn_up=m_ffn_up, ffn_conv_w=m_ffn_conv_w, ffn_down=m_ffn_down,
             final_norm_w=m_final_norm_w)
    V = dict(norm1_w=v_norm1_w, w_in=v_w_in, pool_w=v_pool_w, pool_b=v_pool_b, pool_scale=v_pool_scale, gdn_conv_w=v_gdn_conv_w,
             gdn_a_log=v_gdn_a_log, gdn_dt_bias=v_gdn_dt_bias, gdn_norm_w=v_gdn_norm_w, lru_conv_w=v_lru_conv_w,
             lru_conv_b=v_lru_conv_b, lru_wa=v_lru_wa, lru_ba=v_lru_ba, lru_wx=v_lru_wx, lru_bx=v_lru_bx, lru_lambda=v_lru_lambda,
             w_out=v_w_out, norm2_w=v_norm2_w, ffn_up=v_ffn_up, ffn_conv_w=v_ffn_conv_w, ffn_down=v_ffn_down,
             final_norm_w=v_final_norm_w)
    S = x.shape[1]
    xs = x.reshape(S, D_MODEL)
    tgt = loss_target.reshape(S, D_MODEL)
    mx, my, mc = _place()
    shard = 2 * mx + my

    small_sh = jnp.concatenate([W[k].reshape(N_LAYERS, -1) for k in SMALL_SHARDED], axis=1)
    n_small = small_sh.shape[1]
    pad = -n_small % 1024
    small_sh = jnp.pad(small_sh, ((0, 0), (0, pad))).reshape(N_LAYERS, -1, 1024)

    def own_slots(l):
        out = []
        for w in [W[k][l].astype(BF16) for k in BIG] + [small_sh[l]]:
            buf = lax.dynamic_update_slice(lax.empty((4,) + w.shape, w.dtype), w[None], (shard,) + (0,) * w.ndim)
            out.append(buf.reshape(4, 2, w.shape[0] // 2, w.shape[1]))
        return out

    def whole(g):
        return g.reshape(4, 2 * g.shape[2], g.shape[3])

    def mixer_params(l, g_in, g_out, g_small):
        p = {k: W[k][l] for k in SMALL_REPLICATED}
        g_in = whole(g_in)
        p["w_in"] = _w_in_to_proj(jnp.transpose(g_in, (1, 0, 2)).reshape(g_in.shape[1], IN_COLS))
        p["w_out"] = _rows_to_mixed(whole(g_out).reshape(D_MODEL, D_MODEL))
        g_small = whole(g_small).reshape(4, -1)[:, :n_small]
        off = 0
        for k in SMALL_SHARDED:
            taps, width = W[k].shape[1], W[k].shape[2]
            piece = g_small[:, off:off + taps * width].reshape(4, taps, width)
            p[k] = jnp.transpose(piece, (1, 0, 2)).reshape(taps, 4 * width)
            off += taps * width
        return p

    def ffn_params(g_up, g_down):
        return dict(ffn_up=whole(g_up), ffn_down=whole(g_down).reshape(D_FF, D_MODEL))

    layers, saved = [None] * N_LAYERS, [None] * N_LAYERS
    s0 = own_slots(0)
    g_in0, g_out0, g_small0 = _gather_weights([s0[0], s0[1], s0[4]], "gather_weights")
    f_send, f_recv, ffn0, g_in0 = _copies_start(s0[2:4], g_in0, _gather_ici_copies, "gather_ffn0_ici_start")
    l_send, l_recv, bufs1, g_in0 = _copies_start(own_slots(1), g_in0, _gather_ici_copies, "gather_l1_ici_start")
    layers[0] = mixer_params(0, g_in0, g_out0, g_small0)
    stage = {}

    def mid(o):
        bufs = _copies_wait(f_send, f_recv, ffn0, o, _gather_ici_copies, "gather_ffn0_ici_wait")
        stage["ffn0"] = _copies_start(bufs, o, _gather_d2d_copies, "gather_ffn0_d2d_start")
        return stage["ffn0"][3]

    def ffn(h2):
        send, recv, bufs, _ = stage["ffn0"]
        layers[0].update(ffn_params(*_copies_wait(send, recv, bufs, h2, _gather_d2d_copies, "gather_ffn0_d2d_wait")))
        return h2

    def down(act):
        bufs = _copies_wait(l_send, l_recv, bufs1, act, _gather_ici_copies, "gather_l1_ici_wait")
        stage["l1"] = _copies_start(bufs, act, _gather_d2d_copies, "gather_l1_d2d_start")
        return stage["l1"][3]

    h, saved[0] = _layer_fwd(0, xs, layers[0], dict(mid=mid, ffn=ffn, down=down))
    send, recv, bufs, _ = stage["l1"]
    g1 = _copies_wait(send, recv, bufs, h, _gather_d2d_copies, "gather_l1_d2d_wait")
    layers[1] = {**mixer_params(1, g1[0], g1[1], g1[4]), **ffn_params(g1[2], g1[3])}
    h, saved[1] = _layer_fwd(1, h, layers[1])
    loss_part, dh, g_final = _loss_head(h, final_norm_w, tgt, "loss_head")

    def big_partials(g_layer, names=BIG):
        out = []
        for k in names:
            g = g_layer[k]
            if k == "w_in":
                g = _proj_to_w_in(g)
                g = jnp.transpose(g.reshape(g.shape[0], 4, IN_COLS // 4), (1, 0, 2))
            elif k == "w_out":
                g = _mixed_to_rows(g).reshape(4, D_MODEL // 4, D_MODEL)
            elif k == "ffn_down":
                g = g.reshape(4, D_FF // 4, D_MODEL)
            out.append(g.reshape(4, 2, g.shape[1] // 2, g.shape[2]))
        return out

    FFN, MIX = ("ffn_up", "ffn_down", "w_out"), ("w_in",)
    grads = [None] * N_LAYERS
    dh, grads[1] = _layer_bwd(1, dh, layers[1], saved[1])
    send1, recv1, part1, dh, lands1 = _rs_direct_start(big_partials(grads[1]), dh, "rs_direct_start_1")
    sent0 = []

    def after_ffn(dx2, g):
        send0, recv0, part0, dx2, lands0 = _rs_direct_start(big_partials(g, FFN), dx2, "rs_direct_start_0")
        sent0.extend([send0, recv0, part0, lands0])
        return dx2

    dh, grads[0] = _layer_bwd(0, dh, layers[0], saved[0], after_ffn)

    small_names = SMALL_REPLICATED + SMALL_SHARDED
    small_list = [jnp.stack([grads[l][k].reshape(W[k].shape[1:]) if k in SMALL_REPLICATED else grads[l][k] for l in range(N_LAYERS)])
                  for k in small_names]
    small_list += [g_final.reshape(D_MODEL), loss_part[0, 0:1]]

    sendm, recvm, partm, packed, landsm = _rs_direct_start(big_partials(grads[0], MIX), _pack(small_list), "rs_direct_start_0m")
    a_send, a_recv, packed, a_land = _ar_start(packed, "ar_start")
    part1, lands1 = _rs_direct_wait(send1, recv1, part1, lands1, dh, "rs_direct_wait_1")
    part0, lands0 = _rs_direct_wait(*sent0, dh, "rs_direct_wait_0")
    red = {k: _rs_direct_sum(g, ld, 1, f"rs_sum_1_{k}") for k, g, ld in zip(BIG, part1, lands1)}
    for k, g, ld in zip(FFN, part0, lands0):
        red[k] = _rs_direct_sum(g, ld, 0, f"rs_sum_0_{k}", into=red[k])
    G, DELTA, NM, NV = {}, {}, {}, {}

    def update_big(names, shared):
        for k, r in zip(names, shared):
            G[k] = r.reshape(N_LAYERS, 2 * r.shape[2], r.shape[3])
            DELTA[k], NM[k], NV[k] = _adamw(W[k], G[k], M[k], V[k], f"adam_{k}")

    update_big(FFN, _share_halves([red[k] for k in FFN], "rs_share_ffn"))
    partm, landsm = _rs_direct_wait(sendm, recvm, partm, landsm, [DELTA[k] for k in FFN], "rs_direct_wait_0m")
    for k, g, ld in zip(MIX, partm, landsm):
        red[k] = _rs_direct_sum(g, ld, 0, f"rs_sum_0_{k}", into=red[k])
    update_big(MIX, _share_halves([red[k] for k in MIX], "rs_share_mix"))
    grad_x = dh.reshape(x.shape)

    packed, a_land = _ar_wait(a_send, a_recv, packed, a_land, G[MIX[-1]], "ar_wait")
    reduced = _unpack(_ar_sum(packed, a_land, "ar_sum"), small_list)
    small_g = dict(zip(small_names, reduced[:len(small_names)]))
    small_g["final_norm_w"] = reduced[-2]
    loss = reduced[-1][0]
    for k in SMALL_SHARDED:
        width = W[k].shape[2]
        small_g[k] = lax.dynamic_slice_in_dim(small_g[k], shard * width, width, axis=2)

    small_all = small_names + ("final_norm_w",)
    dl, nm, nv = _adamw(_pack([W[k] for k in small_all]), _pack([small_g[k] for k in small_all]),
                        _pack([M[k] for k in small_all]), _pack([V[k] for k in small_all]), "adam_small")
    like = [W[k] for k in small_all]
    for k, d_, m_, v_ in zip(small_all, _unpack(dl, like), _unpack(nm, like), _unpack(nv, like)):
        G[k], DELTA[k], NM[k], NV[k] = small_g[k], d_, m_, v_

    return (loss, grad_x, *[G[k] for k in WEIGHTS], *[DELTA[k] for k in WEIGHTS], *[NM[k] for k in WEIGHTS],
            *[NV[k] for k in WEIGHTS])
```

```python
import functools

import jax
import jax.numpy as jnp
from jax import lax
from jax.experimental import pallas as pl
from jax.experimental.pallas import tpu as pltpu

F32 = jnp.float32
BF16 = jnp.bfloat16
_MXU = jnp.bfloat16

D_MODEL = 2048
N_LAYERS = 2
POOL_W = 512
POOL_G = 4
POOL_GD = 128
POOL_WINDOWS = (2, 4, 8, 16)
POOL_HALO = 16
GDN_W = 768
GDN_H = 6
GDN_DH = 128
GDN_C = 64
LRU_W = 768
LRU_NB = 6
LRU_BD = 128
LRU_C = 8.0
D_FF = 6144
EPS = 1e-6
IN_COLS = 5132
HALO = 8

PQ, PK, PV, PZ, PXR, PGR, PPOOL, PAB, PCOLS = 0, 768, 1536, 2304, 3072, 3840, 4608, 5120, 5376
CB = 768

ADAM_LR = 0.001
ADAM_B1 = 0.9
ADAM_B2 = 0.999
ADAM_EPS = 1e-08
ADAM_WD = 0.01
ADAM_STEP = 10

VMEM_LIMIT = 56 * 1024 * 1024
MESH = pl.DeviceIdType.MESH
HBM = pl.BlockSpec(memory_space=pltpu.HBM)


def _cp(*sem):
    return pltpu.CompilerParams(dimension_semantics=sem, vmem_limit_bytes=VMEM_LIMIT)


def _dg(a, b, ta, tb):
    dims = (((0 if ta else 1,), (1 if tb else 0,)), ((), ()))
    return lax.dot_general(a, b, dims, preferred_element_type=F32)


def _split2(a):
    hi = a.astype(BF16)
    lo = (a - hi.astype(F32)).astype(BF16)
    return hi, lo


def _mm_raw(a, b, ta, tb, hi):
    if _MXU == F32:
        return _dg(a, b, ta, tb)
    if not hi:
        return _dg(a.astype(_MXU), b.astype(_MXU), ta, tb)
    a1, a2 = _split2(a)
    b1, b2 = _split2(b)
    return _dg(a1, b1, ta, tb) + (_dg(a1, b2, ta, tb) + _dg(a2, b1, ta, tb))


@functools.partial(jax.custom_vjp, nondiff_argnums=(2, 3, 4))
def _mm(a, b, ta=False, tb=False, hi=False):
    return _mm_raw(a, b, ta, tb, hi)


def _mm_fwd(a, b, ta, tb, hi):
    return _mm_raw(a, b, ta, tb, hi), (a, b)


def _mm_bwd(ta, tb, hi, res, dc):
    a, b = res
    da = _mm(b, dc, tb, True, hi) if ta else _mm(dc, b, False, not tb, hi)
    db = _mm(dc, a, True, ta, hi) if tb else _mm(a, dc, not ta, False, hi)
    return da, db


_mm.defvjp(_mm_fwd, _mm_bwd)


def _mm01(m01, x):
    if _MXU == F32:
        return _dg(m01, x, False, False)
    m = m01.astype(BF16)
    x1 = x.astype(BF16)
    r = x - x1.astype(F32)
    x2 = r.astype(BF16)
    x3 = (r - x2.astype(F32)).astype(BF16)
    return _dg(m, x1, False, False) + (_dg(m, x2, False, False) + _dg(m, x3, False, False))


def _down(x, k):
    return x if k == 0 else pltpu.roll(x, k, 0)


def _up(x, k):
    return x if k == 0 else pltpu.roll(x, x.shape[0] - k, 0)


def _rows(shape):
    return lax.broadcasted_iota(jnp.int32, shape, 0)


def _lanes(shape):
    return lax.broadcasted_iota(jnp.int32, shape, 1)


def _matmul(a, b, mode, *, name, res=None, tm=1024, tn=1024, tk=2048, b_split=False, o_split=0, out_dtype=F32):
    ta, tb = mode == "tn", mode == "nt"
    a_split = a.ndim == 3
    if a_split:
        assert not ta
        M, K = a.shape[1], a.shape[0] * a.shape[2]
        tk = min(tk, a.shape[2])
    elif ta:
        K, M = a.shape
    else:
        M, K = a.shape
    if b_split:
        ns = b.shape[0]
        N = b.shape[1] if tb else ns * b.shape[2]
    else:
        N = b.shape[0] if tb else b.shape[1]
    tm, tn, tk = min(tm, M), min(tn, N), min(tk, K)
    if b_split:
        per = b.shape[2]
        if tb:
            tk = min(tk, per)
        else:
            tn = min(tn, per)
    if o_split:
        tn = min(tn, N // o_split)
    assert M % tm == 0 and N % tn == 0 and K % tk == 0, (name, M, N, K, tm, tn, tk)
    nk = K // tk
    if a_split:
        ka = a.shape[2] // tk
        a_spec = pl.BlockSpec((None, tm, tk), lambda i, j, k: (k // ka, i, k % ka))
    else:
        a_spec = pl.BlockSpec((tk, tm), lambda i, j, k: (k, i)) if ta else pl.BlockSpec((tm, tk), lambda i, j, k: (i, k))
    if not b_split:
        b_spec = pl.BlockSpec((tn, tk), lambda i, j, k: (j, k)) if tb else pl.BlockSpec((tk, tn), lambda i, j, k: (k, j))
    elif tb:
        kb = per // tk
        b_spec = pl.BlockSpec((None, tn, tk), lambda i, j, k: (k // kb, j, k % kb))
    else:
        nb = per // tn
        b_spec = pl.BlockSpec((None, tk, tn), lambda i, j, k: (j // nb, k, j % nb))
    if o_split:
        ob = (N // o_split) // tn
        out_shape = jax.ShapeDtypeStruct((o_split, M, N // o_split), out_dtype)
        o_spec = pl.BlockSpec((None, tm, tn), lambda i, j, k: (j // ob, i, j % ob))
    else:
        out_shape = jax.ShapeDtypeStruct((M, N), out_dtype)
        o_spec = pl.BlockSpec((tm, tn), lambda i, j, k: (i, j))
    in_specs = [a_spec, b_spec]
    args = [a, b]
    if res is not None:
        in_specs.append(pl.BlockSpec((tm, tn), lambda i, j, k: (i, j)))
        args.append(res)
    use_acc = nk > 1 and out_dtype != F32

    def body(*refs):
        a_ref, b_ref = refs[0], refs[1]
        o_ref = refs[2 + (res is not None)]
        acc_ref = refs[-1] if use_acc else o_ref
        p = _dg(a_ref[...].astype(_MXU), b_ref[...].astype(_MXU), ta, tb)
        first = p + refs[2][...] if res is not None else p
        if nk == 1:
            o_ref[...] = first.astype(o_ref.dtype)
        else:
            k = pl.program_id(2)

            @pl.when(k == 0)
            def _():
                acc_ref[...] = first

            @pl.when(k > 0)
            def _():
                acc_ref[...] += p

            if use_acc:
                @pl.when(k == nk - 1)
                def _():
                    o_ref[...] = acc_ref[...].astype(o_ref.dtype)

    return pl.pallas_call(
        body, name=name, grid=(M // tm, N // tn, nk), in_specs=in_specs, out_specs=o_spec, out_shape=out_shape,
        scratch_shapes=[pltpu.VMEM((tm, tn), F32)] if use_acc else [],
        compiler_params=_cp("parallel", "parallel", "arbitrary"),
    )(*args)


def _rms(x, w):
    return x * lax.rsqrt(jnp.mean(x * x, axis=-1, keepdims=True) + EPS) * w


def _row_tile(S, t=512):
    t = min(t, S)
    assert S % t == 0
    return t


def _rms_fwd(x, w, name):
    S, D = x.shape
    T = _row_tile(S)

    def body(x_ref, w_ref, o_ref):
        o_ref[...] = _rms(x_ref[...], w_ref[...]).astype(o_ref.dtype)

    return pl.pallas_call(
        body, name=name, grid=(S // T,),
        in_specs=[pl.BlockSpec((T, D), lambda i: (i, 0)), pl.BlockSpec((1, D), lambda i: (0, 0))],
        out_specs=pl.BlockSpec((T, D), lambda i: (i, 0)), out_shape=jax.ShapeDtypeStruct((S, D), BF16),
        compiler_params=_cp("parallel"),
    )(x, w.reshape(1, D))


def _rms_bwd(x, w, dh, dres, name):
    S, D = x.shape
    T = _row_tile(S)

    def body(x_ref, w_ref, dh_ref, dr_ref, dx_ref, gw_ref):
        _, vjp = jax.vjp(_rms, x_ref[...], w_ref[...])
        dx, dw = vjp(dh_ref[...])
        dx_ref[...] = dr_ref[...] + dx

        @pl.when(pl.program_id(0) == 0)
        def _():
            gw_ref[...] = jnp.zeros_like(gw_ref)

        gw_ref[...] += dw

    row = pl.BlockSpec((T, D), lambda i: (i, 0))
    vec = pl.BlockSpec((1, D), lambda i: (0, 0))
    return pl.pallas_call(
        body, name=name, grid=(S // T,), in_specs=[row, vec, row, row], out_specs=[row, vec],
        out_shape=[jax.ShapeDtypeStruct((S, D), F32), jax.ShapeDtypeStruct((1, D), F32)],
        compiler_params=_cp("arbitrary"),
    )(x, w.reshape(1, D), dh, dres)


def _loss_head(x, w, tgt, name):
    S, D = x.shape
    T = _row_tile(S)

    def body(x_ref, w_ref, t_ref, l_ref, dx_ref, gw_ref):
        y, vjp = jax.vjp(_rms, x_ref[...], w_ref[...])
        err = y - t_ref[...]
        part = 0.5 * jnp.sum(jnp.mean(err * err, axis=-1, keepdims=True), axis=0, keepdims=True)
        dx, dw = vjp(err * (1.0 / D))
        dx_ref[...] = dx

        @pl.when(pl.program_id(0) == 0)
        def _():
            gw_ref[...] = jnp.zeros_like(gw_ref)
            l_ref[...] = jnp.zeros_like(l_ref)

        gw_ref[...] += dw
        l_ref[...] += jnp.broadcast_to(part, l_ref.shape)

    row = pl.BlockSpec((T, D), lambda i: (i, 0))
    vec = pl.BlockSpec((1, D), lambda i: (0, 0))
    return pl.pallas_call(
        body, name=name, grid=(S // T,), in_specs=[row, vec, row],
        out_specs=[pl.BlockSpec((8, 128), lambda i: (0, 0)), row, vec],
        out_shape=[jax.ShapeDtypeStruct((8, 128), F32), jax.ShapeDtypeStruct((S, D), F32), jax.ShapeDtypeStruct((1, D), F32)],
        compiler_params=_cp("arbitrary"),
    )(x, w.reshape(1, D), tgt)


def _by_group(shape, vals):
    g = _lanes(shape) // POOL_GD
    out = vals[-1]
    for k in range(len(vals) - 2, -1, -1):
        out = jnp.where(g == k, vals[k], out)
    return out


def _pool_d(prev, u, t0):
    ext = jnp.concatenate([prev, u], axis=0)
    s2 = ext + _down(ext, 1)
    s4 = s2 + _down(s2, 2)
    s8 = s4 + _down(s4, 4)
    s16 = s8 + _down(s8, 8)
    ssel = _by_group(ext.shape, [s2, s4, s8, s16])[POOL_HALO:]
    win = _by_group(u.shape, [jnp.int32(w) for w in POOL_WINDOWS])
    cnt = jnp.minimum(t0 + _rows(u.shape) + 1, win).astype(F32)
    return ssel / cnt - u


def _pool_lin(d, w_ref, b):
    ys = [_mm(d[:, g * POOL_GD:(g + 1) * POOL_GD], w_ref[g]) for g in range(POOL_G)]
    return jnp.concatenate(ys, axis=1) + b


def _pool_fwd(proj, w, b, scale, name):
    S = proj.shape[0]
    T = _row_tile(S)
    r = T // POOL_HALO
    cb = PPOOL // POOL_W

    def body(u_ref, up_ref, w_ref, b_ref, sc_ref, y_ref):
        i = pl.program_id(0)
        prev = jnp.where(i > 0, up_ref[...], 0.0)
        d = _pool_d(prev, u_ref[...], i * T)
        y_ref[...] = _pool_lin(d, w_ref, b_ref[...]) * sc_ref[...]

    vec = pl.BlockSpec((1, POOL_W), lambda i: (0, 0))
    return pl.pallas_call(
        body, name=name, grid=(S // T,),
        in_specs=[pl.BlockSpec((T, POOL_W), lambda i: (i, cb)),
                  pl.BlockSpec((POOL_HALO, POOL_W), lambda i: (jnp.maximum(i * r - 1, 0), cb)),
                  pl.BlockSpec((POOL_G, POOL_GD, POOL_GD), lambda i: (0, 0, 0)), vec, vec],
        out_specs=pl.BlockSpec((T, POOL_W), lambda i: (i, 0)), out_shape=jax.ShapeDtypeStruct((S, POOL_W), F32),
        compiler_params=_cp("parallel"),
    )(proj, proj, w, b.reshape(1, POOL_W), scale.reshape(1, POOL_W))


def _pool_bwd(proj, dmixed, w, b, scale, name):
    S = proj.shape[0]
    T = _row_tile(S)
    n = S // T
    r = T // POOL_HALO
    cb = PPOOL // POOL_W
    mb = 1536 // POOL_W

    def body(u_ref, up_ref, dy_ref, dyn_ref, w_ref, b_ref, sc_ref, du_ref, gw_ref, gb_ref, gs_ref):
        i = pl.program_id(0)
        sc = sc_ref[...]
        dy = dy_ref[...]
        dy_ext = jnp.concatenate([dy, jnp.where(i < n - 1, dyn_ref[...], 0.0)], axis=0)
        dyl = dy_ext * sc
        dd = jnp.concatenate(
            [_mm(dyl[:, g * POOL_GD:(g + 1) * POOL_GD], w_ref[g], False, True) for g in range(POOL_G)], axis=1)
        t_ext = i * T + _rows(dd.shape)
        win = _by_group(dd.shape, [jnp.int32(v) for v in POOL_WINDOWS])
        cnt = jnp.minimum(t_ext + 1, win).astype(F32)
        e = jnp.where(t_ext < S, dd / cnt, 0.0)
        f2 = e + _up(e, 1)
        f4 = f2 + _up(f2, 2)
        f8 = f4 + _up(f4, 4)
        f16 = f8 + _up(f8, 8)
        du = (_by_group(dd.shape, [f2, f4, f8, f16]) - dd)[:T]
        du_ref[...] = du.astype(du_ref.dtype)

        prev = jnp.where(i > 0, up_ref[...], 0.0)
        d = _pool_d(prev, u_ref[...], i * T)
        ylin = _pool_lin(d, w_ref, b_ref[...])
        dyl_m = dy * sc

        @pl.when(i == 0)
        def _():
            gw_ref[...] = jnp.zeros_like(gw_ref)
            gb_ref[...] = jnp.zeros_like(gb_ref)
            gs_ref[...] = jnp.zeros_like(gs_ref)

        gs_ref[...] += jnp.sum(dy * ylin, axis=0, keepdims=True)
        gb_ref[...] += jnp.sum(dyl_m, axis=0, keepdims=True)
        for g in range(POOL_G):
            sl = slice(g * POOL_GD, (g + 1) * POOL_GD)
            gw_ref[g] += _mm(d[:, sl], dyl_m[:, sl], True, False)

    vec = pl.BlockSpec((1, POOL_W), lambda i: (0, 0))
    wsp = pl.BlockSpec((POOL_G, POOL_GD, POOL_GD), lambda i: (0, 0, 0))
    nh = S // POOL_HALO
    return pl.pallas_call(
        body, name=name, grid=(n,),
        in_specs=[pl.BlockSpec((T, POOL_W), lambda i: (i, cb)),
                  pl.BlockSpec((POOL_HALO, POOL_W), lambda i: (jnp.maximum(i * r - 1, 0), cb)),
                  pl.BlockSpec((T, POOL_W), lambda i: (i, mb)),
                  pl.BlockSpec((POOL_HALO, POOL_W), lambda i: (jnp.minimum((i + 1) * r, nh - 1), mb)),
                  wsp, vec, vec],
        out_specs=[pl.BlockSpec((T, POOL_W), lambda i: (i, 0)), wsp, vec, vec],
        out_shape=[jax.ShapeDtypeStruct((S, POOL_W), BF16), jax.ShapeDtypeStruct((POOL_G, POOL_GD, POOL_GD), F32),
                   jax.ShapeDtypeStruct((1, POOL_W), F32), jax.ShapeDtypeStruct((1, POOL_W), F32)],
        compiler_params=_cp("arbitrary"),
    )(proj, proj, dmixed, dmixed, w, b.reshape(1, POOL_W), scale.reshape(1, POOL_W))


def _conv_rows(ext, w_ref, taps):
    acc = w_ref[taps - 1:taps, :] * ext
    for k in range(1, taps):
        acc = acc + w_ref[taps - 1 - k:taps - k, :] * _down(ext, k)
    return acc


def _conv_t_rows(dc, w_ref, taps):
    acc = w_ref[taps - 1:taps, :] * dc
    for k in range(1, taps):
        acc = acc + w_ref[taps - 1 - k:taps - k, :] * _up(dc, k)
    return acc


def _conv_specs(T, S, ncb0, with_next):
    r = T // HALO
    nh = S // HALO
    main = pl.BlockSpec((T, CB), lambda j, i: (i, j + ncb0))
    prev = pl.BlockSpec((HALO, CB), lambda j, i: (jnp.maximum(i * r - 1, 0), j + ncb0))
    nxt = pl.BlockSpec((HALO, CB), lambda j, i: (jnp.minimum((i + 1) * r, nh - 1), j + ncb0))
    return (main, prev, nxt) if with_next else (main, prev)


def _gdn_conv_fwd(proj, w, name):
    S = proj.shape[0]
    T = _row_tile(S)
    taps = w.shape[0]
    ncb = 3 * GDN_W // CB

    def body(x_ref, xp_ref, w_ref, o_ref):
        i = pl.program_id(1)
        ext = jnp.concatenate([jnp.where(i > 0, xp_ref[...], 0.0), x_ref[...]], axis=0)
        o_ref[...] = jax.nn.silu(_conv_rows(ext, w_ref, taps)[HALO:])

    main, prev = _conv_specs(T, S, PQ // CB, False)
    return pl.pallas_call(
        body, name=name, grid=(ncb, S // T),
        in_specs=[main, prev, pl.BlockSpec((taps, CB), lambda j, i: (0, j))],
        out_specs=pl.BlockSpec((T, CB), lambda j, i: (i, j)), out_shape=jax.ShapeDtypeStruct((S, 3 * GDN_W), F32),
        compiler_params=_cp("parallel", "parallel"),
    )(proj, proj, w)


def _gdn_conv_bwd(proj, dact, w, name):
    S = proj.shape[0]
    T = _row_tile(S)
    n = S // T
    taps = w.shape[0]
    ncb = 3 * GDN_W // CB

    def body(x_ref, xp_ref, xn_ref, d_ref, dn_ref, w_ref, dx_ref, gw_ref):
        i = pl.program_id(1)
        last = i == n - 1
        ext = jnp.concatenate([jnp.where(i > 0, xp_ref[...], 0.0), x_ref[...], jnp.where(last, 0.0, xn_ref[...])], axis=0)
        c = _conv_rows(ext, w_ref, taps)[HALO:]
        d_ext = jnp.concatenate([d_ref[...], jnp.where(last, 0.0, dn_ref[...])], axis=0)
        _, vjp = jax.vjp(jax.nn.silu, c)
        dc = vjp(d_ext)[0]
        dx_ref[...] = _conv_t_rows(dc, w_ref, taps)[:T].astype(dx_ref.dtype)

        @pl.when(i == 0)
        def _():
            gw_ref[...] = jnp.zeros_like(gw_ref)

        dcm = dc[:T]
        for k in range(taps):
            gw_ref[taps - 1 - k:taps - k, :] += jnp.sum(dcm * _down(ext, k)[HALO:HALO + T], axis=0, keepdims=True)

    main, prev, nxt = _conv_specs(T, S, PQ // CB, True)
    dmain, _, dnxt = _conv_specs(T, S, 0, True)
    wsp = pl.BlockSpec((taps, CB), lambda j, i: (0, j))
    return pl.pallas_call(
        body, name=name, grid=(ncb, n), in_specs=[main, prev, nxt, dmain, dnxt, wsp],
        out_specs=[pl.BlockSpec((T, CB), lambda j, i: (i, j)), wsp],
        out_shape=[jax.ShapeDtypeStruct((S, 3 * GDN_W), BF16), jax.ShapeDtypeStruct((taps, 3 * GDN_W), F32)],
        compiler_params=_cp("parallel", "arbitrary"),
    )(proj, proj, proj, dact, dact, w)


def _ffn_act_fwd(up, w, name):
    S = up.shape[0]
    T = _row_tile(S)
    taps = w.shape[0]
    ncb = D_FF // CB

    def body(g_ref, gp_ref, v_ref, w_ref, o_ref):
        i = pl.program_id(1)
        ext = jnp.concatenate([jnp.where(i > 0, gp_ref[...], 0.0), g_ref[...]], axis=0)
        c = _conv_rows(ext, w_ref, taps)[HALO:]
        o_ref[...] = (jax.nn.gelu(c) * v_ref[...]).astype(o_ref.dtype)

    main, prev = _conv_specs(T, S, 0, False)
    val = pl.BlockSpec((T, CB), lambda j, i: (i, j + ncb))
    return pl.pallas_call(
        body, name=name, grid=(ncb, S // T),
        in_specs=[main, prev, val, pl.BlockSpec((taps, CB), lambda j, i: (0, j))],
        out_specs=pl.BlockSpec((T, CB), lambda j, i: (i, j)), out_shape=jax.ShapeDtypeStruct((S, D_FF), BF16),
        compiler_params=_cp("parallel", "parallel"),
    )(up, up, up, w)


def _ffn_act_bwd(up, dact, w, name):
    S = up.shape[0]
    T = _row_tile(S)
    n = S // T
    taps = w.shape[0]
    ncb = D_FF // CB

    def body(g_ref, gp_ref, gn_ref, v_ref, vn_ref, d_ref, dn_ref, w_ref, dup_ref, gw_ref):
        i = pl.program_id(1)
        last = i == n - 1
        ext = jnp.concatenate([jnp.where(i > 0, gp_ref[...], 0.0), g_ref[...], jnp.where(last, 0.0, gn_ref[...])], axis=0)
        c = _conv_rows(ext, w_ref, taps)[HALO:]
        v_ext = jnp.concatenate([v_ref[...], jnp.where(last, 0.0, vn_ref[...])], axis=0)
        d_ext = jnp.concatenate([d_ref[...], jnp.where(last, 0.0, dn_ref[...])], axis=0)
        gl, vjp = jax.vjp(jax.nn.gelu, c)
        dup_ref[1] = (d_ext * gl)[:T].astype(dup_ref.dtype)
        dc = vjp(d_ext * v_ext)[0]
        dup_ref[0] = _conv_t_rows(dc, w_ref, taps)[:T].astype(dup_ref.dtype)

        @pl.when(i == 0)
        def _():
            gw_ref[...] = jnp.zeros_like(gw_ref)

        dcm = dc[:T]
        for k in range(taps):
            gw_ref[taps - 1 - k:taps - k, :] += jnp.sum(dcm * _down(ext, k)[HALO:HALO + T], axis=0, keepdims=True)

    main, prev, nxt = _conv_specs(T, S, 0, True)
    vmain, _, vnxt = _conv_specs(T, S, ncb, True)
    wsp = pl.BlockSpec((taps, CB), lambda j, i: (0, j))
    osp = pl.BlockSpec((2, T, CB), lambda j, i: (0, i, j))
    return pl.pallas_call(
        body, name=name, grid=(ncb, n), in_specs=[main, prev, nxt, vmain, vnxt, main, nxt, wsp],
        out_specs=[osp, wsp],
        out_shape=[jax.ShapeDtypeStruct((2, S, D_FF), BF16), jax.ShapeDtypeStruct((taps, D_FF), F32)],
        compiler_params=_cp("parallel", "arbitrary"),
    )(up, up, up, up, up, dact, dact, w)


def _tri_masks():
    r = _rows((GDN_C, GDN_C))
    c = _lanes((GDN_C, GDN_C))
    return r >= c, r > c


def _each(fn, *cols):
    return tuple(fn(*args) for args in zip(*cols))


def _tri_inv_raw(lows):
    r = _rows(lows[0].shape)
    c = _lanes(lows[0].shape)
    eye = jnp.where(r == c, 1.0, 0.0)
    ps = _each(lambda low: eye - low, lows)
    lps = lows
    for _ in range(5):
        lps = _each(lambda lp: _mm(lp, lp, False, False, True), lps)
        ps = _each(lambda p, lp: p + _mm(p, lp, False, False, True), ps, lps)
    return ps


@jax.custom_vjp
def _tri_inv(lows):
    return _tri_inv_raw(lows)


def _tri_inv_fwd(lows):
    ts = _tri_inv_raw(lows)
    return ts, ts


def _tri_inv_bwd(ts, dts):
    inner = _each(lambda t, dt: _mm(t, dt, True, False, True), ts, dts)
    return (_each(lambda m, t: -_mm(m, t, False, True, True), inner, ts),)


_tri_inv.defvjp(_tri_inv_fwd, _tri_inv_bwd)


@jax.custom_vjp
def _tri_inv_given(lows, ts):
    return ts


def _tri_inv_given_fwd(lows, ts):
    return ts, ts


def _tri_inv_given_bwd(ts, dts):
    return _tri_inv_bwd(ts, dts)[0], _each(jnp.zeros_like, ts)


_tri_inv_given.defvjp(_tri_inv_given_fwd, _tri_inv_given_bwd)


def _gdn_glog(a_col, alog, dtb):
    return -jnp.exp(alog) * jax.nn.softplus(a_col + dtb)


def _decay_operand():
    r = _rows((GDN_C, 2 * GDN_C))
    c = _lanes((GDN_C, 2 * GDN_C))
    return jnp.where((c >= GDN_C) | (r > c), 1.0, 0.0)


def _gdn_decay(glog):
    causal, _ = _tri_masks()
    res = _mm01(jnp.where(causal, 1.0, 0.0), glog * _decay_operand())
    return res[:, GDN_C:GDN_C + 1], res[:, :GDN_C]


def _gdn_decay_bwd(dgcol, dd):
    r = _rows((GDN_C, GDN_C))
    c = _lanes((GDN_C, GDN_C))
    dres = jnp.concatenate([dd, jnp.where(c == 0, dgcol, 0.0)], axis=1)
    dx = _mm01(jnp.where(r <= c, 1.0, 0.0), dres)
    return jnp.sum(dx * _decay_operand(), axis=1, keepdims=True)


def _gdn_chunk(qa, ka, va, bt_col, gcol, dmat, t_saved=None):
    causal, strict = _tri_masks()
    qn = _each(lambda q: q * lax.rsqrt(jnp.sum(q * q, axis=-1, keepdims=True) + EPS) * (GDN_DH ** -0.5), qa)
    kn = _each(lambda k: k * lax.rsqrt(jnp.sum(k * k, axis=-1, keepdims=True) + EPS), ka)
    beta = _each(jax.nn.sigmoid, bt_col)
    eg = _each(jnp.exp, gcol)
    decay = _each(lambda d: jnp.where(causal, jnp.exp(d), 0.0), dmat)
    kk = _each(lambda k: _mm(k, k, False, True), kn)
    low = _each(lambda b, m, d: jnp.where(strict, b * m * d, 0.0), beta, kk, decay)
    t = _tri_inv(low) if t_saved is None else _tri_inv_given(low, t_saved)
    w = _each(lambda t_, k, b, e: _mm(t_, k * (b * e), False, False, True), t, kn, beta, eg)
    u = _each(lambda t_, v, b: _mm(t_, v * b, False, False, True), t, va, beta)
    attn = _each(lambda q, k, d: _mm(q, k, False, True) * d, qn, kn, decay)
    last = _rows(gcol[0].shape) == GDN_C - 1
    g_last = _each(lambda g: jnp.sum(jnp.where(last, g, 0.0), axis=0, keepdims=True), gcol)
    qd = _each(lambda q, e: q * e, qn, eg)
    kd = _each(lambda k, gl, g: k * jnp.exp(gl - g), kn, g_last, gcol)
    return (w, u, qd, kd, attn), t


def _gdn_step(state, w, u, qd, kd, attn, egl):
    v_new = _each(lambda u_, w_, s: u_ - _mm(w_, s), u, w, state)
    o_state = _each(_mm, qd, state)
    o = _each(lambda os, a, v: os + _mm(a, v), o_state, attn, v_new)
    new = _each(lambda s, e, k, v: s * e + _mm(k, v, True, False), state, egl, kd, v_new)
    return o, new


def _heads(ref, base=0, width=GDN_DH):
    return tuple(ref[:, (base + h) * GDN_DH:(base + h) * GDN_DH + width] for h in range(GDN_H))


def _cols(a, base):
    return tuple(a[:, base + h:base + h + 1] for h in range(GDN_H))


def _gated_norm(o, z, nw):
    return o * lax.rsqrt(jnp.mean(o * o, axis=-1, keepdims=True) + EPS) * nw * jax.nn.silu(z)


def _hsl(h):
    return slice(h * GDN_DH, (h + 1) * GDN_DH)


def _pad_lanes(a, width=GDN_DH):
    return jnp.concatenate([a, jnp.zeros((a.shape[0], width - a.shape[1]), a.dtype)], axis=1)


def _gdn_prep(qkv, proj, alog, dtb, name):
    S = qkv.shape[0]
    N = S // GDN_C

    def body(qkv_ref, ab_ref, al_ref, dt_ref, w_ref, u_ref, qd_ref, kd_ref, at_ref, ti_ref, gc_ref):
        ab = ab_ref[...]
        glog = _each(_gdn_glog, _cols(ab, 0), _cols(al_ref[...], 0), _cols(dt_ref[...], 0))
        dec = _each(_gdn_decay, glog)
        gcol, dmat = _each(lambda d: d[0], dec), _each(lambda d: d[1], dec)
        (w, u, qd, kd, attn), tinv = _gdn_chunk(_heads(qkv_ref), _heads(qkv_ref, GDN_H), _heads(qkv_ref, 2 * GDN_H),
                                                _cols(ab, GDN_H), gcol, dmat)
        gc = jnp.zeros((GDN_C, 128), F32)
        for h in range(GDN_H):
            w_ref[:, _hsl(h)] = w[h]
            u_ref[:, _hsl(h)] = u[h]
            qd_ref[:, _hsl(h)] = qd[h]
            kd_ref[:, _hsl(h)] = kd[h]
            at_ref[:, _hsl(h)] = _pad_lanes(attn[h])
            ti_ref[:, _hsl(h)] = _pad_lanes(tinv[h])
            gc = jnp.where(_lanes(gc.shape) == h, gcol[h], gc)
        gc_ref[...] = gc

    vec = pl.BlockSpec((1, 128), lambda i: (0, 0))
    hsp = pl.BlockSpec((GDN_C, GDN_W), lambda i: (i, 0))
    hshape = jax.ShapeDtypeStruct((S, GDN_W), F32)
    return pl.pallas_call(
        body, name=name, grid=(N,),
        in_specs=[pl.BlockSpec((GDN_C, 3 * GDN_W), lambda i: (i, 0)), pl.BlockSpec((GDN_C, 128), lambda i: (i, PAB // 128)), vec, vec],
        out_specs=[hsp] * 6 + [pl.BlockSpec((GDN_C, 128), lambda i: (i, 0))],
        out_shape=[hshape] * 6 + [jax.ShapeDtypeStruct((S, 128), F32)],
        compiler_params=_cp("parallel"),
    )(qkv, proj, alog, dtb)


def _gdn_scan(w, u, qd, kd, attn, gc, name):
    S = w.shape[0]
    N = S // GDN_C

    def body(w_ref, u_ref, qd_ref, kd_ref, at_ref, gc_ref, o_ref, st_ref, s_ref):
        @pl.when(pl.program_id(0) == 0)
        def _():
            s_ref[...] = jnp.zeros_like(s_ref)

        state = tuple(s_ref[_hsl(h), :] for h in range(GDN_H))
        egl = _each(jnp.exp, _cols(gc_ref[GDN_C - 1:GDN_C, :], 0))
        o, new = _gdn_step(state, _heads(w_ref), _heads(u_ref), _heads(qd_ref), _heads(kd_ref),
                           _heads(at_ref, width=GDN_C), egl)
        for h in range(GDN_H):
            st_ref[_hsl(h), :] = state[h]
            o_ref[:, _hsl(h)] = o[h]
            s_ref[_hsl(h), :] = new[h]

    hsp = pl.BlockSpec((GDN_C, GDN_W), lambda i: (i, 0))
    return pl.pallas_call(
        body, name=name, grid=(N,),
        in_specs=[hsp] * 5 + [pl.BlockSpec((GDN_C, 128), lambda i: (i, 0))],
        out_specs=[hsp, pl.BlockSpec((None, GDN_W, GDN_DH), lambda i: (i, 0, 0))],
        out_shape=[jax.ShapeDtypeStruct((S, GDN_W), F32), jax.ShapeDtypeStruct((N, GDN_W, GDN_DH), F32)],
        scratch_shapes=[pltpu.VMEM((GDN_W, GDN_DH), F32)],
        compiler_params=_cp("arbitrary"),
    )(w, u, qd, kd, attn, gc)


def _gdn_scan_bwd(w, u, qd, kd, attn, gc, states, o, proj, dmixed, nw, name):
    S = w.shape[0]
    N = S // GDN_C

    def body(w_ref, u_ref, qd_ref, kd_ref, at_ref, gc_ref, st_ref, o_ref, z_ref, dm_ref, nw_ref,
             dw_ref, du_ref, dqd_ref, dkd_ref, dat_ref, dgl_ref, dz_ref, gnw_ref, ds_ref):
        @pl.when(pl.program_id(0) == 0)
        def _():
            ds_ref[...] = jnp.zeros_like(ds_ref)
            gnw_ref[...] = jnp.zeros_like(gnw_ref)

        nw = nw_ref[...]
        _, vjp_n = jax.vjp(lambda o, z, w_: _each(lambda a, b: _gated_norm(a, b, w_), o, z), _heads(o_ref), _heads(z_ref), nw)
        do, dz, dnw = vjp_n(_heads(dm_ref))
        state = tuple(st_ref[_hsl(h), :] for h in range(GDN_H))
        egl = _each(jnp.exp, _cols(gc_ref[GDN_C - 1:GDN_C, :], 0))
        _, vjp_s = jax.vjp(_gdn_step, state, _heads(w_ref), _heads(u_ref), _heads(qd_ref), _heads(kd_ref),
                           _heads(at_ref, width=GDN_C), egl)
        ds, dw, du, dqd, dkd, dat, degl = vjp_s((do, tuple(ds_ref[_hsl(h), :] for h in range(GDN_H))))
        dgl = jnp.zeros((8, 128), F32)
        for h in range(GDN_H):
            dz_ref[:, _hsl(h)] = dz[h].astype(dz_ref.dtype)
            ds_ref[_hsl(h), :] = ds[h]
            dw_ref[:, _hsl(h)] = dw[h]
            du_ref[:, _hsl(h)] = du[h]
            dqd_ref[:, _hsl(h)] = dqd[h]
            dkd_ref[:, _hsl(h)] = dkd[h]
            dat_ref[:, _hsl(h)] = _pad_lanes(dat[h])
            dgl = jnp.where(_lanes(dgl.shape) == h, degl[h] * egl[h], dgl)
        dgl_ref[...] = dgl
        gnw_ref[...] += dnw

    rev = lambda i: (N - 1 - i, 0)
    hsp = pl.BlockSpec((GDN_C, GDN_W), rev)
    gsp = pl.BlockSpec((GDN_C, 128), rev)
    vec = pl.BlockSpec((1, GDN_DH), lambda i: (0, 0))
    hshape = jax.ShapeDtypeStruct((S, GDN_W), F32)
    return pl.pallas_call(
        body, name=name, grid=(N,),
        in_specs=[hsp] * 5 + [gsp, pl.BlockSpec((None, GDN_W, GDN_DH), lambda i: (N - 1 - i, 0, 0)), hsp,
                              pl.BlockSpec((GDN_C, GDN_W), lambda i: (N - 1 - i, PZ // GDN_W)), hsp, vec],
        out_specs=[hsp] * 5 + [pl.BlockSpec((8, 128), rev), hsp, vec],
        out_shape=[hshape] * 5 + [jax.ShapeDtypeStruct((N * 8, 128), F32), jax.ShapeDtypeStruct((S, GDN_W), BF16),
                                  jax.ShapeDtypeStruct((1, GDN_DH), F32)],
        scratch_shapes=[pltpu.VMEM((GDN_W, GDN_DH), F32)],
        compiler_params=_cp("arbitrary"),
    )(w, u, qd, kd, attn, gc, states, o, proj, dmixed, nw)


def _gdn_prep_bwd(qkv, proj, alog, dtb, tinv, dw, du, dqd, dkd, dat, dgl, name):
    S = qkv.shape[0]
    N = S // GDN_C

    def body(qkv_ref, ab_ref, al_ref, dt_ref, ti_ref, dw_ref, du_ref, dqd_ref, dkd_ref, dat_ref, dgl_ref,
             dqkv_ref, dab_ref, gal_ref, gdt_ref):
        @pl.when(pl.program_id(0) == 0)
        def _():
            gal_ref[...] = jnp.zeros_like(gal_ref)
            gdt_ref[...] = jnp.zeros_like(gdt_ref)

        ab = ab_ref[...]
        glog, vjp_g = jax.vjp(lambda a, al, dt: _each(_gdn_glog, a, al, dt), _cols(ab, 0), _cols(al_ref[...], 0),
                              _cols(dt_ref[...], 0))
        dec = _each(_gdn_decay, glog)
        gcol, dmat = _each(lambda d: d[0], dec), _each(lambda d: d[1], dec)
        _, vjp_c, _ = jax.vjp(functools.partial(_gdn_chunk, t_saved=_heads(ti_ref, width=GDN_C)), _heads(qkv_ref),
                              _heads(qkv_ref, GDN_H), _heads(qkv_ref, 2 * GDN_H), _cols(ab, GDN_H), gcol, dmat, has_aux=True)
        dqa, dka, dva, dbt, dgcol, dd = vjp_c((_heads(dw_ref), _heads(du_ref), _heads(dqd_ref), _heads(dkd_ref),
                                               _heads(dat_ref, width=GDN_C)))
        last = _rows(dgcol[0].shape) == GDN_C - 1
        dgcol = _each(lambda d, g: d + jnp.where(last, g, 0.0), dgcol, _cols(dgl_ref[0:1, :], 0))
        da_col, dal, ddt = vjp_g(_each(_gdn_decay_bwd, dgcol, dd))
        dab = jnp.zeros((GDN_C, 128), F32)
        gal = jnp.zeros((1, 128), F32)
        gdt = jnp.zeros((1, 128), F32)
        for h in range(GDN_H):
            dqkv_ref[:, _hsl(h)] = dqa[h]
            dqkv_ref[:, _hsl(GDN_H + h)] = dka[h]
            dqkv_ref[:, _hsl(2 * GDN_H + h)] = dva[h]
            ln = _lanes(dab.shape)
            dab = dab + jnp.where(ln == h, da_col[h], 0.0) + jnp.where(ln == GDN_H + h, dbt[h], 0.0)
            l1 = _lanes(gal.shape)
            gal = gal + jnp.where(l1 == h, dal[h], 0.0)
            gdt = gdt + jnp.where(l1 == h, ddt[h], 0.0)
        dab_ref[...] = dab.astype(dab_ref.dtype)
        gal_ref[...] += gal
        gdt_ref[...] += gdt

    vec = pl.BlockSpec((1, 128), lambda i: (0, 0))
    hsp = pl.BlockSpec((GDN_C, GDN_W), lambda i: (i, 0))
    qsp = pl.BlockSpec((GDN_C, 3 * GDN_W), lambda i: (i, 0))
    return pl.pallas_call(
        body, name=name, grid=(N,),
        in_specs=[qsp, pl.BlockSpec((GDN_C, 128), lambda i: (i, PAB // 128)), vec, vec] + [hsp] * 6
        + [pl.BlockSpec((8, 128), lambda i: (i, 0))],
        out_specs=[qsp, pl.BlockSpec((GDN_C, 128), lambda i: (i, 0)), vec, vec],
        out_shape=[jax.ShapeDtypeStruct((S, 3 * GDN_W), F32), jax.ShapeDtypeStruct((S, 128), BF16),
                   jax.ShapeDtypeStruct((1, 128), F32), jax.ShapeDtypeStruct((1, 128), F32)],
        compiler_params=_cp("arbitrary"),
    )(qkv, proj, alog, dtb, tinv, dw, du, dqd, dkd, dat, dgl)


@jax.custom_vjp
def _expm1(x):
    u = jnp.exp(x)
    lu = jnp.log(u)
    small = (u - 1.0) * x / jnp.where(u == 1.0, 1.0, lu)
    small = jnp.where(u == 1.0, x, small)
    return jnp.where(jnp.abs(x) < 0.5, small, u - 1.0)


def _expm1_fwd(x):
    return _expm1(x), jnp.exp(x)


def _expm1_bwd(ex, g):
    return (g * ex,)


_expm1.defvjp(_expm1_fwd, _expm1_bwd)


def _lru_gates(xc, wa, ba, wx, bx, lam, first):
    r = jax.nn.sigmoid(_mm(xc, wa) + ba)
    i = jax.nn.sigmoid(_mm(xc, wx) + bx)
    log_a = -LRU_C * r * jax.nn.softplus(-lam)
    mult = jnp.sqrt(-_expm1(2.0 * log_a))
    mult = jnp.where(first, 1.0, mult)
    return jnp.exp(log_a), mult * i * xc


def _scan_fwd(a, b):
    T = a.shape[0]
    rows = _rows(a.shape)
    s = 1
    while s < T:
        ok = rows >= s
        b = a * jnp.where(ok, _down(b, s), 0.0) + b
        a = a * jnp.where(ok, _down(a, s), 1.0)
        s *= 2
    return a, b


def _scan_rev(a, b):
    T = a.shape[0]
    rows = _rows(a.shape)
    s = 1
    while s < T:
        ok = rows + s < T
        b = a * jnp.where(ok, _up(b, s), 0.0) + b
        a = a * jnp.where(ok, _up(a, s), 1.0)
        s *= 2
    return b


def _bsl(j):
    return slice(j * LRU_BD, (j + 1) * LRU_BD)


def _lru_tile(S):
    return _row_tile(S, 256)


def _lru_fwd(proj, conv_w, conv_b, wa, ba, wx, bx, lam, name):
    S = proj.shape[0]
    T = _lru_tile(S)
    taps = conv_w.shape[0]
    r = T // HALO

    def body(x_ref, xp_ref, cw_ref, cb_ref, wa_ref, ba_ref, wx_ref, bx_ref, lam_ref, h_ref, carry_ref):
        i = pl.program_id(0)

        @pl.when(i == 0)
        def _():
            carry_ref[...] = jnp.zeros_like(carry_ref)

        ext = jnp.concatenate([jnp.where(i > 0, xp_ref[...], 0.0), x_ref[...]], axis=0)
        xc = _conv_rows(ext, cw_ref, taps)[HALO:] + cb_ref[...]
        first = (i * T + _rows((T, LRU_BD))) == 0
        for j in range(LRU_NB):
            a, b = _lru_gates(xc[:, _bsl(j)], wa_ref[j], ba_ref[:, _bsl(j)], wx_ref[j], bx_ref[:, _bsl(j)],
                              lam_ref[:, _bsl(j)], first=first)
            pa, hb = _scan_fwd(a, b)
            h_ref[:, _bsl(j)] = pa * carry_ref[0:1, _bsl(j)] + hb
            carry_ref[0:1, _bsl(j)] = h_ref[T - 1:T, _bsl(j)]

    vec = pl.BlockSpec((1, LRU_W), lambda i: (0, 0))
    wsp = pl.BlockSpec((LRU_NB, LRU_BD, LRU_BD), lambda i: (0, 0, 0))
    return pl.pallas_call(
        body, name=name, grid=(S // T,),
        in_specs=[pl.BlockSpec((T, LRU_W), lambda i: (i, PXR // LRU_W)),
                  pl.BlockSpec((HALO, LRU_W), lambda i: (jnp.maximum(i * r - 1, 0), PXR // LRU_W)),
                  pl.BlockSpec((taps, LRU_W), lambda i: (0, 0)), vec, wsp, vec, wsp, vec, vec],
        out_specs=pl.BlockSpec((T, LRU_W), lambda i: (i, 0)), out_shape=jax.ShapeDtypeStruct((S, LRU_W), F32),
        scratch_shapes=[pltpu.VMEM((8, LRU_W), F32)],
        compiler_params=_cp("arbitrary"),
    )(proj, proj, conv_w, conv_b.reshape(1, LRU_W), wa, ba.reshape(1, LRU_W), wx, bx.reshape(1, LRU_W), lam.reshape(1, LRU_W))


def _lru_bwd(proj, hl, dmixed, conv_w, conv_b, wa, ba, wx, bx, lam, name):
    S = proj.shape[0]
    T = _lru_tile(S)
    n = S // T
    taps = conv_w.shape[0]
    r = T // HALO
    mb = 768 // LRU_W

    def body(x_ref, xp_ref, g_ref, h_ref, hp_ref, dy_ref, cw_ref, cb_ref, wa_ref, ba_ref, wx_ref, bx_ref, lam_ref,
             dx_ref, dg_ref, gcw_ref, gcb_ref, gwa_ref, gba_ref, gwx_ref, gbx_ref, glam_ref, carry_ref, dxc_ref, nxt_ref):
        s = pl.program_id(0)
        i = n - 1 - s

        @pl.when(s == 0)
        def _():
            carry_ref[...] = jnp.zeros_like(carry_ref)
            nxt_ref[...] = jnp.zeros_like(nxt_ref)
            for ref in (gcw_ref, gcb_ref, gwa_ref, gba_ref, gwx_ref, gbx_ref, glam_ref):
                ref[...] = jnp.zeros_like(ref)

        ext = jnp.concatenate([jnp.where(i > 0, xp_ref[...], 0.0), x_ref[...]], axis=0)
        xc = _conv_rows(ext, cw_ref, taps)[HALO:] + cb_ref[...]
        rows = _rows((T, LRU_BD))
        first = (i * T + rows) == 0
        h_before = jnp.where(i > 0, hp_ref[HALO - 1:HALO, :], 0.0)
        for j in range(LRU_NB):
            sl = _bsl(j)
            (a, _), vjp_g = jax.vjp(functools.partial(_lru_gates, first=first), xc[:, sl], wa_ref[j], ba_ref[:, sl],
                                    wx_ref[j], bx_ref[:, sl], lam_ref[:, sl])
            gelu_g, vjp_a = jax.vjp(jax.nn.gelu, g_ref[:, sl])
            h = h_ref[:, sl]
            dy = dy_ref[:, sl]
            dg_ref[:, sl] = vjp_a(dy * h)[0].astype(dg_ref.dtype)
            b_rev = dy * gelu_g + jnp.where(rows == T - 1, carry_ref[0:1, sl], 0.0)
            a_rev = jnp.where(rows == T - 1, 0.0, _up(a, 1))
            dh = _scan_rev(a_rev, b_rev)
            carry_ref[:, sl] = (a * dh)[:HALO]
            h_prev = jnp.where(rows == 0, h_before[:, sl], _down(h, 1))
            dxc, dwa, dba, dwx, dbx, dlam = vjp_g((dh * h_prev, dh))
            dxc_ref[:, sl] = dxc
            gwa_ref[j] += dwa
            gwx_ref[j] += dwx
            gba_ref[:, sl] += dba
            gbx_ref[:, sl] += dbx
            glam_ref[:, sl] += dlam
        dxc = dxc_ref[...]
        d_ext = jnp.concatenate([dxc, nxt_ref[...]], axis=0)
        dx_ref[...] = _conv_t_rows(d_ext, cw_ref, taps)[:T].astype(dx_ref.dtype)
        nxt_ref[...] = dxc[:HALO]
        gcb_ref[...] += jnp.sum(dxc, axis=0, keepdims=True)
        for k in range(taps):
            gcw_ref[taps - 1 - k:taps - k, :] += jnp.sum(dxc * _down(ext, k)[HALO:], axis=0, keepdims=True)

    vec = pl.BlockSpec((1, LRU_W), lambda s: (0, 0))
    wsp = pl.BlockSpec((LRU_NB, LRU_BD, LRU_BD), lambda s: (0, 0, 0))
    cwsp = pl.BlockSpec((taps, LRU_W), lambda s: (0, 0))

    def main(cb):
        return pl.BlockSpec((T, LRU_W), lambda s: (n - 1 - s, cb))

    def prev(cb):
        return pl.BlockSpec((HALO, LRU_W), lambda s: (jnp.maximum((n - 1 - s) * r - 1, 0), cb))

    vshape = jax.ShapeDtypeStruct((1, LRU_W), F32)
    wshape = jax.ShapeDtypeStruct((LRU_NB, LRU_BD, LRU_BD), F32)
    return pl.pallas_call(
        body, name=name, grid=(n,),
        in_specs=[main(PXR // LRU_W), prev(PXR // LRU_W), main(PGR // LRU_W), main(0), prev(0), main(mb),
                  cwsp, vec, wsp, vec, wsp, vec, vec],
        out_specs=[main(0), main(0), cwsp, vec, wsp, vec, wsp, vec, vec],
        out_shape=[jax.ShapeDtypeStruct((S, LRU_W), BF16), jax.ShapeDtypeStruct((S, LRU_W), BF16),
                   jax.ShapeDtypeStruct((taps, LRU_W), F32), vshape, wshape, vshape, wshape, vshape, vshape],
        scratch_shapes=[pltpu.VMEM((8, LRU_W), F32), pltpu.VMEM((T, LRU_W), F32), pltpu.VMEM((HALO, LRU_W), F32)],
        compiler_params=_cp("arbitrary"),
    )(proj, proj, proj, hl, hl, dmixed, conv_w, conv_b.reshape(1, LRU_W), wa, ba.reshape(1, LRU_W), wx,
      bx.reshape(1, LRU_W), lam.reshape(1, LRU_W))


def _mix_out(o, proj, hl, y_pool, nw, name):
    S = o.shape[0]
    T = _row_tile(S)

    def body(o_ref, z_ref, h_ref, g_ref, p_ref, nw_ref, m_ref):
        for h in range(GDN_H):
            m_ref[:, _hsl(h)] = _gated_norm(o_ref[:, _hsl(h)], z_ref[:, _hsl(h)], nw_ref[...]).astype(m_ref.dtype)
        m_ref[:, GDN_W:GDN_W + LRU_W] = (h_ref[...] * jax.nn.gelu(g_ref[...])).astype(m_ref.dtype)
        m_ref[:, GDN_W + LRU_W:] = p_ref[...].astype(m_ref.dtype)

    row = pl.BlockSpec((T, GDN_W), lambda i: (i, 0))
    return pl.pallas_call(
        body, name=name, grid=(S // T,),
        in_specs=[row, pl.BlockSpec((T, GDN_W), lambda i: (i, PZ // GDN_W)), row,
                  pl.BlockSpec((T, LRU_W), lambda i: (i, PGR // LRU_W)), pl.BlockSpec((T, POOL_W), lambda i: (i, 0)),
                  pl.BlockSpec((1, GDN_DH), lambda i: (0, 0))],
        out_specs=pl.BlockSpec((T, D_MODEL), lambda i: (i, 0)), out_shape=jax.ShapeDtypeStruct((S, D_MODEL), BF16),
        compiler_params=_cp("parallel"),
    )(o, proj, hl, proj, y_pool, nw)


def _as2d(a):
    return a.reshape(-1, a.shape[-1])


def _ew_rows(rows, cols):
    t = rows
    while t * cols * 4 > (2 << 20) and t % 16 == 0:
        t //= 2
    return t


def _rs_rows(rows, cols, budget=2 << 20):
    t = rows
    while t * cols * 4 > budget and t % 32 == 0:
        t //= 2
    return t


def _adamw(w, g, m, v, name):
    shape = w.shape
    w2, g2, m2, v2 = _as2d(w), _as2d(g), _as2d(m), _as2d(v)
    rows, cols = w2.shape
    t = _ew_rows(rows, cols)

    def body(w_ref, g_ref, m_ref, v_ref, d_ref, nm_ref, nv_ref):
        gr = g_ref[...]
        nm = ADAM_B1 * m_ref[...] + (1.0 - ADAM_B1) * gr
        nv = ADAM_B2 * v_ref[...] + (1.0 - ADAM_B2) * (gr * gr)
        m_hat = nm / (1.0 - ADAM_B1 ** ADAM_STEP)
        v_hat = nv / (1.0 - ADAM_B2 ** ADAM_STEP)
        d_ref[...] = -ADAM_LR * (m_hat / (jnp.sqrt(v_hat) + ADAM_EPS) + ADAM_WD * w_ref[...])
        nm_ref[...] = nm
        nv_ref[...] = nv

    sp = pl.BlockSpec((t, cols), lambda i: (i, 0))
    sh = jax.ShapeDtypeStruct((rows, cols), F32)
    d, nm, nv = pl.pallas_call(body, name=name, grid=(rows // t,), in_specs=[sp] * 4, out_specs=[sp] * 3,
                               out_shape=[sh] * 3, compiler_params=_cp("parallel"))(w2, g2, m2, v2)
    return d.reshape(shape), nm.reshape(shape), nv.reshape(shape)


def _place():
    return lax.axis_index("x"), lax.axis_index("y"), lax.axis_index("c")


def _gather_weights(arrs, name):
    n = len(arrs)

    def body(*refs):
        outs = refs[n:2 * n]
        send, recv = refs[2 * n:]
        x, y, c = _place()
        s_me, s_x, s_y, s_d = 2 * x + y, 2 * (1 - x) + y, 2 * x + (1 - y), 2 * (1 - x) + (1 - y)
        xpeer, ypeer, sib = (1 - x, y, c), (x, 1 - y, c), (x, y, 1 - c)

        def rc(k, t, src, dst, to):
            return pltpu.make_async_remote_copy(src_ref=src, dst_ref=dst, send_sem=send.at[k, t], recv_sem=recv.at[k, t],
                                                device_id=to, device_id_type=MESH)

        def piece(k, s, top):
            rq = outs[k].shape[2] // 2
            return outs[k].at[s, c, pl.ds(0 if top else rq, rq)]

        sent = []

        def start(k, t, ref, to):
            cp = rc(k, t, ref, ref, to)
            cp.start()
            sent.append(cp)

        for k in range(n):
            start(k, 0, outs[k].at[s_me, c], xpeer)
            start(k, 1, outs[k].at[s_me, c], ypeer)
        for k in range(n):
            got = outs[k].at[s_x, c]
            rc(k, 0, got, got, xpeer).wait_recv()
            start(k, 2, piece(k, s_x, True), ypeer)
            start(k, 3, got, sib)
        for k in range(n):
            got = outs[k].at[s_y, c]
            rc(k, 1, got, got, ypeer).wait_recv()
            start(k, 6, piece(k, s_y, False), xpeer)
            start(k, 4, got, sib)
        for k in range(n):
            top, bottom = piece(k, s_d, True), piece(k, s_d, False)
            rc(k, 2, top, top, ypeer).wait_recv()
            rc(k, 6, bottom, bottom, xpeer).wait_recv()
            start(k, 5, outs[k].at[s_d, c], sib)
        for k in range(n):
            for t, s in ((3, s_x), (4, s_y), (5, s_d)):
                got = outs[k].at[s, 1 - c]
                rc(k, t, got, got, sib).wait_recv()
        for cp in sent:
            cp.wait_send()

    return pl.pallas_call(
        body, name=name, in_specs=[HBM] * n, out_specs=[HBM] * n,
        out_shape=[jax.ShapeDtypeStruct(a.shape, a.dtype) for a in arrs],
        input_output_aliases={k: k for k in range(n)},
        scratch_shapes=[pltpu.SemaphoreType.DMA((n, 7)), pltpu.SemaphoreType.DMA((n, 7))],
    )(*arrs)


def _share_halves(arrs, name):
    n = len(arrs)

    def body(*refs):
        outs = refs[n:2 * n]
        send, recv = refs[2 * n:]
        x, y, c = _place()
        cps = []
        for k in range(n):
            mine = outs[k].at[:, c]
            cp = pltpu.make_async_remote_copy(src_ref=mine, dst_ref=mine, send_sem=send.at[k], recv_sem=recv.at[k],
                                              device_id=(x, y, 1 - c), device_id_type=MESH)
            cp.start()
            cps.append(cp)
        for k in range(n):
            got = outs[k].at[:, 1 - c]
            pltpu.make_async_remote_copy(src_ref=got, dst_ref=got, send_sem=send.at[k], recv_sem=recv.at[k],
                                         device_id=(x, y, 1 - c), device_id_type=MESH).wait_recv()
        for cp in cps:
            cp.wait_send()

    return pl.pallas_call(
        body, name=name, in_specs=[HBM] * n, out_specs=[HBM] * n,
        out_shape=[jax.ShapeDtypeStruct(a.shape, a.dtype) for a in arrs],
        input_output_aliases={k: k for k in range(n)},
        scratch_shapes=[pltpu.SemaphoreType.DMA((n,)), pltpu.SemaphoreType.DMA((n,))],
    )(*arrs)


_REL = tuple((dx, dy, dc) for dx in (0, 1) for dy in (0, 1) for dc in (0, 1))[1:]
SEM = pl.BlockSpec(memory_space=pltpu.SEMAPHORE)
DATAFLOW = pltpu.SideEffectType.DATAFLOW_SIDE_EFFECTING


def _flip(v, d):
    return 1 - v if d else v


def _rs_direct_copies(srcs, land, send, recv):
    x, y, c = _place()
    cps = []
    for k in range(len(srcs)):
        for r, (dx, dy, dc) in enumerate(_REL):
            px, py, pc = _flip(x, dx), _flip(y, dy), _flip(c, dc)
            cps.append(pltpu.make_async_remote_copy(
                src_ref=srcs[k].at[2 * px + py, pc], dst_ref=land[k].at[r], send_sem=send.at[k * len(_REL) + r],
                recv_sem=recv.at[k * len(_REL) + r], device_id=(px, py, pc), device_id_type=MESH))
    return cps


def _rs_direct_start(grads, thru, name):
    n = len(grads)
    lands = [pltpu.with_memory_space_constraint(lax.empty((len(_REL),) + g.shape[2:], g.dtype), pltpu.HBM) for g in grads]

    def body(*refs):
        for cp in _rs_direct_copies(refs[:n], refs[n + 1:2 * n + 1], refs[2 * n + 1], refs[2 * n + 2]):
            cp.start()

    sems = pltpu.SemaphoreType.DMA((n * len(_REL),))
    keep = [pltpu.HBM(a.shape, a.dtype) for a in (*grads, thru, *lands)]
    out = pl.pallas_call(
        body, name=name, in_specs=[HBM] * (2 * n + 1), out_specs=(SEM, SEM) + (HBM,) * (2 * n + 1),
        out_shape=(sems, sems, *keep), input_output_aliases={i: 2 + i for i in range(2 * n + 1)},
        compiler_params=pltpu.CompilerParams(has_side_effects=DATAFLOW),
    )(*[pltpu.with_memory_space_constraint(a, pltpu.HBM) for a in (*grads, thru)], *lands)
    return out[0], out[1], out[2:2 + n], out[2 + n], out[3 + n:]


def _rs_direct_wait(send, recv, grads, lands, after, name):
    n = len(grads)
    after = list(after) if isinstance(after, (list, tuple)) else [after]

    def body(*refs):
        for cp in _rs_direct_copies(refs[:n], refs[n:2 * n], refs[2 * n], refs[2 * n + 1]):
            cp.wait_send()
            cp.wait_recv()

    keep = [pltpu.HBM(a.shape, a.dtype) for a in (*grads, *lands)]
    out = pl.pallas_call(
        body, name=name, in_specs=[HBM] * (2 * n) + [SEM, SEM] + [pl.BlockSpec(memory_space=pl.ANY)] * len(after),
        out_specs=(HBM,) * (2 * n), out_shape=tuple(keep), input_output_aliases={i: i for i in range(2 * n)},
        compiler_params=pltpu.CompilerParams(has_side_effects=DATAFLOW),
    )(*grads, *lands, send, recv, *after)
    return out[:n], out[n:]


_CHIPS = ((1, 0), (0, 1), (1, 1))


def _gather_ici_copies(bufs, send, recv):
    x, y, c = _place()
    cps = []
    for k in range(len(bufs)):
        mine = bufs[k].at[2 * x + y, c]
        for j, (dx, dy) in enumerate(_CHIPS):
            cps.append(pltpu.make_async_remote_copy(
                src_ref=mine, dst_ref=mine, send_sem=send.at[k * len(_CHIPS) + j], recv_sem=recv.at[k * len(_CHIPS) + j],
                device_id=(_flip(x, dx), _flip(y, dy), c), device_id_type=MESH))
    return cps


def _gather_d2d_copies(bufs, send, recv):
    x, y, c = _place()
    cps = []
    for k in range(len(bufs)):
        for j, (dx, dy) in enumerate(_CHIPS):
            got = bufs[k].at[2 * _flip(x, dx) + _flip(y, dy), c]
            cps.append(pltpu.make_async_remote_copy(
                src_ref=got, dst_ref=got, send_sem=send.at[k * len(_CHIPS) + j], recv_sem=recv.at[k * len(_CHIPS) + j],
                device_id=(x, y, 1 - c), device_id_type=MESH))
    return cps


def _copies_start(bufs, thru, copies, name):
    n = len(bufs)

    def body(*refs):
        for cp in copies(refs[:n], refs[n + 1], refs[n + 2]):
            cp.start()

    sems = pltpu.SemaphoreType.DMA((n * len(_CHIPS),))
    out = pl.pallas_call(
        body, name=name, in_specs=[HBM] * (n + 1), out_specs=(SEM, SEM) + (HBM,) * (n + 1),
        out_shape=(sems, sems, *[pltpu.HBM(a.shape, a.dtype) for a in (*bufs, thru)]),
        input_output_aliases={i: 2 + i for i in range(n + 1)},
        compiler_params=pltpu.CompilerParams(has_side_effects=DATAFLOW),
    )(*[pltpu.with_memory_space_constraint(a, pltpu.HBM) for a in (*bufs, thru)])
    return out[0], out[1], out[2:2 + n], out[2 + n]


def _copies_wait(send, recv, bufs, after, copies, name):
    n = len(bufs)

    def body(*refs):
        for cp in copies(refs[:n], refs[n], refs[n + 1]):
            cp.wait_send()
            cp.wait_recv()

    return pl.pallas_call(
        body, name=name, in_specs=[HBM] * n + [SEM, SEM, pl.BlockSpec(memory_space=pl.ANY)], out_specs=(HBM,) * n,
        out_shape=tuple(pltpu.HBM(a.shape, a.dtype) for a in bufs), input_output_aliases={i: i for i in range(n)},
        compiler_params=pltpu.CompilerParams(has_side_effects=DATAFLOW),
    )(*bufs, send, recv, after)


def _rs_direct_sum(grad, land, layer, name, into=None):
    _, _, rows, cols = grad.shape
    t = _rs_rows(rows, cols, 6 << 20)
    npieces = len(_REL) + 1

    def body(g_ref, l_ref, *rest):
        o_ref, acc_ref = rest[-2], rest[-1]
        j = pl.program_id(1)

        @pl.when(j == 0)
        def _():
            acc_ref[...] = g_ref[...].astype(F32)

        @pl.when(j > 0)
        def _():
            acc_ref[...] += l_ref[...].astype(F32)

        @pl.when(j == npieces - 1)
        def _():
            o_ref[...] = acc_ref[...]

    def mine(i, j):
        x, y, c = _place()
        return (2 * x + y, c, i, 0)

    in_specs = [pl.BlockSpec((None, None, t, cols), mine),
                pl.BlockSpec((None, t, cols), lambda i, j: (jnp.maximum(j - 1, 0), i, 0))]
    args = [grad, land]
    if into is not None:
        in_specs.append(pl.BlockSpec(memory_space=pl.ANY))
        args.append(into)
    return pl.pallas_call(
        body, name=name, grid=(rows // t, npieces), in_specs=in_specs,
        out_specs=pl.BlockSpec((None, None, t, cols), lambda i, j: (layer, lax.axis_index("c"), i, 0)),
        scratch_shapes=[pltpu.VMEM((t, cols), F32)],
        out_shape=jax.ShapeDtypeStruct((2, 2, rows, cols), F32), input_output_aliases={} if into is None else {2: 0},
        compiler_params=_cp("parallel", "arbitrary"),
    )(*args)


def _ar_copies(buf, land, send, recv):
    x, y, c = _place()
    return [pltpu.make_async_remote_copy(src_ref=buf, dst_ref=land.at[r], send_sem=send.at[r], recv_sem=recv.at[r],
                                         device_id=(_flip(x, dx), _flip(y, dy), _flip(c, dc)), device_id_type=MESH)
            for r, (dx, dy, dc) in enumerate(_REL)]


def _ar_start(buf, name):
    land = pltpu.with_memory_space_constraint(lax.empty((len(_REL),) + buf.shape, buf.dtype), pltpu.HBM)

    def body(buf_ref, land_ref, send, recv, *_):
        for cp in _ar_copies(buf_ref, land_ref, send, recv):
            cp.start()

    sems = pltpu.SemaphoreType.DMA((len(_REL),))
    return pl.pallas_call(
        body, name=name, in_specs=[HBM, HBM], out_specs=(SEM, SEM, HBM, HBM),
        out_shape=(sems, sems, pltpu.HBM(buf.shape, buf.dtype), pltpu.HBM(land.shape, land.dtype)),
        input_output_aliases={0: 2, 1: 3}, compiler_params=pltpu.CompilerParams(has_side_effects=DATAFLOW),
    )(pltpu.with_memory_space_constraint(buf, pltpu.HBM), land)


def _ar_wait(send, recv, buf, land, after, name):
    def body(buf_ref, land_ref, send_ref, recv_ref, *_):
        for cp in _ar_copies(buf_ref, land_ref, send_ref, recv_ref):
            cp.wait_send()
            cp.wait_recv()

    return pl.pallas_call(
        body, name=name, in_specs=[HBM, HBM, SEM, SEM, pl.BlockSpec(memory_space=pl.ANY)], out_specs=(HBM, HBM),
        out_shape=(pltpu.HBM(buf.shape, buf.dtype), pltpu.HBM(land.shape, land.dtype)), input_output_aliases={0: 0, 1: 1},
        compiler_params=pltpu.CompilerParams(has_side_effects=DATAFLOW),
    )(buf, land, send, recv, after)


def _ar_sum(buf, land, name):
    rows, cols = buf.shape
    t = _rs_rows(rows, cols)

    def slot(i, j):
        x, y, c = _place()
        xd, yd, cd = j // 4, (j // 2) % 2, j % 2
        rel = 4 * (x + xd - 2 * x * xd) + 2 * (y + yd - 2 * y * yd) + (c + cd - 2 * c * cd)
        return (jnp.maximum(rel - 1, 0), i, 0)

    def body(b_ref, l_ref, o_ref, acc_ref):
        j = pl.program_id(1)
        x, y, c = _place()
        val = jnp.where(j == 4 * x + 2 * y + c, b_ref[...], l_ref[...])

        @pl.when(j == 0)
        def _():
            acc_ref[...] = val

        @pl.when(j > 0)
        def _():
            acc_ref[...] += val

        @pl.when(j == len(_REL))
        def _():
            o_ref[...] = acc_ref[...]

    sp = pl.BlockSpec((t, cols), lambda i, j: (i, 0))
    return pl.pallas_call(
        body, name=name, grid=(rows // t, len(_REL) + 1), in_specs=[sp, pl.BlockSpec((None, t, cols), slot)], out_specs=sp,
        out_shape=jax.ShapeDtypeStruct((rows, cols), F32), scratch_shapes=[pltpu.VMEM((t, cols), F32)],
        compiler_params=_cp("parallel", "arbitrary"),
    )(buf, land)


def _pad128(v):
    return jnp.zeros((1, 128), F32).at[0, :v.shape[0]].set(v)


def _layer_fwd(l, x, p, hooks=None):
    hooks = hooks or {}
    h1 = _rms_fwd(x, p["norm1_w"], f"rms1_{l}")
    proj = _matmul(h1, p["w_in"], "nn", name=f"mm_in_{l}", tn=768)
    y_pool = _pool_fwd(proj, p["pool_w"], p["pool_b"], p["pool_scale"], f"pool_{l}")
    qkv = _gdn_conv_fwd(proj, p["gdn_conv_w"], f"gconv_{l}")
    alog, dtb = _pad128(p["gdn_a_log"]), _pad128(p["gdn_dt_bias"])
    gw, gu, gqd, gkd, gat, tinv, gc = _gdn_prep(qkv, proj, alog, dtb, f"gprep_{l}")
    o, states = _gdn_scan(gw, gu, gqd, gkd, gat, gc, f"gscan_{l}")
    if "mid" in hooks:
        o = hooks["mid"](o)
    hl = _lru_fwd(proj, p["lru_conv_w"], p["lru_conv_b"], p["lru_wa"], p["lru_ba"], p["lru_wx"], p["lru_bx"],
                  p["lru_lambda"], f"lru_{l}")
    mixed = _mix_out(o, proj, hl, y_pool, p["gdn_norm_w"].reshape(1, GDN_DH), f"mix_{l}")
    x2 = _matmul(mixed, p["w_out"], "nn", name=f"mm_out_{l}", res=x)
    h2 = _rms_fwd(x2, p["norm2_w"], f"rms2_{l}")
    if "ffn" in hooks:
        h2 = hooks["ffn"](h2)
    up = _matmul(h2, p["ffn_up"], "nn", name=f"mm_up_{l}", b_split=True)
    act = _ffn_act_fwd(up, p["ffn_conv_w"], f"ffn_{l}")
    if "down" in hooks:
        act = hooks["down"](act)
    x3 = _matmul(act, p["ffn_down"], "nn", name=f"mm_down_{l}", res=x2)
    saved = dict(x=x, h1=h1, proj=proj, qkv=qkv, gdn=(gw, gu, gqd, gkd, gat, gc), tinv=tinv, states=states, o=o, hl=hl, mixed=mixed,
                 x2=x2, h2=h2, up=up, act=act, alog=alog, dtb=dtb)
    return x3, saved


def _layer_bwd(l, dx3, p, s, after_ffn=None, after_mix=None):
    g = {}
    dact = _matmul(dx3, p["ffn_down"], "nt", name=f"mm_ddown_{l}")
    g["ffn_down"] = _matmul(s["act"], dx3, "tn", name=f"mm_gdown_{l}", out_dtype=BF16)
    dup, g["ffn_conv_w"] = _ffn_act_bwd(s["up"], dact, p["ffn_conv_w"], f"ffn_b_{l}")
    dh2 = _matmul(dup, p["ffn_up"], "nt", name=f"mm_dup_{l}", b_split=True, tk=3072)
    g["ffn_up"] = _matmul(s["h2"], dup, "tn", name=f"mm_gup_{l}", b_split=True, o_split=4, tk=4096, out_dtype=BF16)
    dx2, g["norm2_w"] = _rms_bwd(s["x2"], p["norm2_w"], dh2, dx3, f"rms2_b_{l}")
    g["w_out"] = _matmul(s["mixed"], dx2, "tn", name=f"mm_gout_{l}", out_dtype=BF16)
    if after_ffn is not None:
        dx2 = after_ffn(dx2, g)
    dmixed = _matmul(dx2, p["w_out"], "nt", name=f"mm_dout_{l}")
    proj = s["proj"]
    du_pool, g["pool_w"], g["pool_b"], g["pool_scale"] = _pool_bwd(proj, dmixed, p["pool_w"], p["pool_b"], p["pool_scale"], f"pool_b_{l}")
    gw, gu, gqd, gkd, gat, gc = s["gdn"]
    dw, du, dqd, dkd, dat, dgl, dz, g["gdn_norm_w"] = _gdn_scan_bwd(
        gw, gu, gqd, gkd, gat, gc, s["states"], s["o"], proj, dmixed, p["gdn_norm_w"].reshape(1, GDN_DH), f"gscan_b_{l}")
    dqkv, dab, gal, gdt = _gdn_prep_bwd(s["qkv"], proj, s["alog"], s["dtb"], s["tinv"], dw, du, dqd, dkd, dat, dgl, f"gprep_b_{l}")
    g["gdn_a_log"], g["gdn_dt_bias"] = gal[0, :GDN_H], gdt[0, :GDN_H]
    dpre, g["gdn_conv_w"] = _gdn_conv_bwd(proj, dqkv, p["gdn_conv_w"], f"gconv_b_{l}")
    (dxr, dgr, g["lru_conv_w"], g["lru_conv_b"], g["lru_wa"], g["lru_ba"], g["lru_wx"], g["lru_bx"], g["lru_lambda"]) = _lru_bwd(
        proj, s["hl"], dmixed, p["lru_conv_w"], p["lru_conv_b"], p["lru_wa"], p["lru_ba"], p["lru_wx"], p["lru_bx"],
        p["lru_lambda"], f"lru_b_{l}")
    S = proj.shape[0]
    dproj = jnp.concatenate([dpre, dz, dxr, dgr, du_pool, dab, jnp.zeros((S, PCOLS - PAB - 128), BF16)], axis=1)
    g["w_in"] = _matmul(s["h1"], dproj, "tn", name=f"mm_gin_{l}", tn=768, tk=4096, out_dtype=BF16)
    if after_mix is not None:
        dproj = after_mix(dproj, g)
    dh1 = _matmul(dproj, p["w_in"], "nt", name=f"mm_din_{l}", tk=1792)
    dx, g["norm1_w"] = _rms_bwd(s["x"], p["norm1_w"], dh1, dx2, f"rms1_b_{l}")
    return dx, g


_IN_PERM = ((512, 3584), (3596, 5132), (0, 512), (3584, 3596))


def _w_in_to_proj(w):
    parts = [w[:, a:b] for a, b in _IN_PERM]
    return jnp.concatenate(parts + [jnp.zeros((w.shape[0], PCOLS - IN_COLS), w.dtype)], axis=1)


def _proj_to_w_in(g):
    return jnp.concatenate([g[:, PPOOL:PPOOL + 512], g[:, 0:3072], g[:, PAB:PAB + 12], g[:, 3072:PPOOL]], axis=1)


def _rows_to_mixed(w):
    return jnp.concatenate([w[512:], w[:512]], axis=0)


def _mixed_to_rows(g):
    return jnp.concatenate([g[1536:], g[:1536]], axis=0)


SMALL_SHARDED = ("gdn_conv_w", "lru_conv_w", "ffn_conv_w")
BIG = ("w_in", "w_out", "ffn_up", "ffn_down")
SMALL_REPLICATED = ("norm1_w", "pool_w", "pool_b", "pool_scale", "gdn_a_log", "gdn_dt_bias", "gdn_norm_w", "lru_conv_b",
                    "lru_wa", "lru_ba", "lru_wx", "lru_bx", "lru_lambda", "norm2_w")
WEIGHTS = ("norm1_w", "w_in", "pool_w", "pool_b", "pool_scale", "gdn_conv_w", "gdn_a_log", "gdn_dt_bias", "gdn_norm_w",
           "lru_conv_w", "lru_conv_b", "lru_wa", "lru_ba", "lru_wx", "lru_bx", "lru_lambda", "w_out", "norm2_w", "ffn_up",
           "ffn_conv_w", "ffn_down", "final_norm_w")
FLAT_COLS = 1024


def _pack(arrs):
    flat = jnp.concatenate([a.reshape(-1) for a in arrs])
    rows = -(-flat.shape[0] // (8 * FLAT_COLS)) * 8
    return jnp.pad(flat, (0, rows * FLAT_COLS - flat.shape[0])).reshape(rows, FLAT_COLS)


def _unpack(buf, like):
    flat = buf.reshape(-1)
    out, off = [], 0
    for a in like:
        size = 1
        for d in a.shape:
            size *= d
        out.append(flat[off:off + size].reshape(a.shape))
        off += size
    return out


def kernel(x, norm1_w, w_in, pool_w, pool_b, pool_scale, gdn_conv_w, gdn_a_log, gdn_dt_bias, gdn_norm_w, lru_conv_w, lru_conv_b, lru_wa, lru_ba, lru_wx, lru_bx, lru_lambda, w_out, norm2_w, ffn_up, ffn_conv_w, ffn_down, final_norm_w, loss_target, m_norm1_w, m_w_in, m_pool_w, m_pool_b, m_pool_scale, m_gdn_conv_w, m_gdn_a_log, m_gdn_dt_bias, m_gdn_norm_w, m_lru_conv_w, m_lru_conv_b, m_lru_wa, m_lru_ba, m_lru_wx, m_lru_bx, m_lru_lambda, m_w_out, m_norm2_w, m_ffn_up, m_ffn_conv_w, m_ffn_down, m_final_norm_w, v_norm1_w, v_w_in, v_pool_w, v_pool_b, v_pool_scale, v_gdn_conv_w, v_gdn_a_log, v_gdn_dt_bias, v_gdn_norm_w, v_lru_conv_w, v_lru_conv_b, v_lru_wa, v_lru_ba, v_lru_wx, v_lru_bx, v_lru_lambda, v_w_out, v_norm2_w, v_ffn_up, v_ffn_conv_w, v_ffn_down, v_final_norm_w):
    W = dict(norm1_w=norm1_w, w_in=w_in, pool_w=pool_w, pool_b=pool_b, pool_scale=pool_scale, gdn_conv_w=gdn_conv_w,
             gdn_a_log=gdn_a_log, gdn_dt_bias=gdn_dt_bias, gdn_norm_w=gdn_norm_w, lru_conv_w=lru_conv_w, lru_conv_b=lru_conv_b,
             lru_wa=lru_wa, lru_ba=lru_ba, lru_wx=lru_wx, lru_bx=lru_bx, lru_lambda=lru_lambda, w_out=w_out, norm2_w=norm2_w,
             ffn_up=ffn_up, ffn_conv_w=ffn_conv_w, ffn_down=ffn_down, final_norm_w=final_norm_w)
    M = dict(norm1_w=m_norm1_w, w_in=m_w_in, pool_w=m_pool_w, pool_b=m_pool_b, pool_scale=m_pool_scale, gdn_conv_w=m_gdn_conv_w,
             gdn_a_log=m_gdn_a_log, gdn_dt_bias=m_gdn_dt_bias, gdn_norm_w=m_gdn_norm_w, lru_conv_w=m_lru_conv_w,
             lru_conv_b=m_lru_conv_b, lru_wa=m_lru_wa, lru_ba=m_lru_ba, lru_wx=m_lru_wx, lru_bx=m_lru_bx, lru_lambda=m_lru_lambda,
             w_out=m_w_out, norm2_w=m_norm2_w, ffn_up=m_ffn_up, ffn_conv_w=m_ffn_conv_w, ffn_down=m_ffn_down,
             final_norm_w=m_final_norm_w)
    V = dict(norm1_w=v_norm1_w, w_in=v_w_in, pool_w=v_pool_w, pool_b=v_pool_b, pool_scale=v_pool_scale, gdn_conv_w=v_gdn_conv_w,
             gdn_a_log=v_gdn_a_log, gdn_dt_bias=v_gdn_dt_bias, gdn_norm_w=v_gdn_norm_w, lru_conv_w=v_lru_conv_w,
             lru_conv_b=v_lru_conv_b, lru_wa=v_lru_wa, lru_ba=v_lru_ba, lru_wx=v_lru_wx, lru_bx=v_lru_bx, lru_lambda=v_lru_lambda,
             w_out=v_w_out, norm2_w=v_norm2_w, ffn_up=v_ffn_up, ffn_conv_w=v_ffn_conv_w, ffn_down=v_ffn_down,
             final_norm_w=v_final_norm_w)
    S = x.shape[1]
    xs = x.reshape(S, D_MODEL)
    tgt = loss_target.reshape(S, D_MODEL)
    mx, my, mc = _place()
    shard = 2 * mx + my

    small_sh = jnp.concatenate([W[k].reshape(N_LAYERS, -1) for k in SMALL_SHARDED], axis=1)
    n_small = small_sh.shape[1]
    pad = -n_small % 1024
    small_sh = jnp.pad(small_sh, ((0, 0), (0, pad))).reshape(N_LAYERS, -1, 1024)

    def own_slots(l):
        out = []
        for w in [W[k][l].astype(BF16) for k in BIG] + [small_sh[l]]:
            buf = lax.dynamic_update_slice(lax.empty((4,) + w.shape, w.dtype), w[None], (shard,) + (0,) * w.ndim)
            out.append(buf.reshape(4, 2, w.shape[0] // 2, w.shape[1]))
        return out

    def whole(g):
        return g.reshape(4, 2 * g.shape[2], g.shape[3])

    def mixer_params(l, g_in, g_out, g_small):
        p = {k: W[k][l] for k in SMALL_REPLICATED}
        g_in = whole(g_in)
        p["w_in"] = _w_in_to_proj(jnp.transpose(g_in, (1, 0, 2)).reshape(g_in.shape[1], IN_COLS))
        p["w_out"] = _rows_to_mixed(whole(g_out).reshape(D_MODEL, D_MODEL))
        g_small = whole(g_small).reshape(4, -1)[:, :n_small]
        off = 0
        for k in SMALL_SHARDED:
            taps, width = W[k].shape[1], W[k].shape[2]
            piece = g_small[:, off:off + taps * width].reshape(4, taps, width)
            p[k] = jnp.transpose(piece, (1, 0, 2)).reshape(taps, 4 * width)
            off += taps * width
        return p

    def ffn_params(g_up, g_down):
        return dict(ffn_up=whole(g_up), ffn_down=whole(g_down).reshape(D_FF, D_MODEL))

    layers, saved = [None] * N_LAYERS, [None] * N_LAYERS
    s0 = own_slots(0)
    g_in0, g_out0, g_small0 = _gather_weights([s0[0], s0[1], s0[4]], "gather_weights")
    f_send, f_recv, ffn0, g_in0 = _copies_start(s0[2:4], g_in0, _gather_ici_copies, "gather_ffn0_ici_start")
    l_send, l_recv, bufs1, g_in0 = _copies_start(own_slots(1), g_in0, _gather_ici_copies, "gather_l1_ici_start")
    layers[0] = mixer_params(0, g_in0, g_out0, g_small0)
    stage = {}

    def mid(o):
        bufs = _copies_wait(f_send, f_recv, ffn0, o, _gather_ici_copies, "gather_ffn0_ici_wait")
        stage["ffn0"] = _copies_start(bufs, o, _gather_d2d_copies, "gather_ffn0_d2d_start")
        return stage["ffn0"][3]

    def ffn(h2):
        send, recv, bufs, _ = stage["ffn0"]
        layers[0].update(ffn_params(*_copies_wait(send, recv, bufs, h2, _gather_d2d_copies, "gather_ffn0_d2d_wait")))
        return h2

    def down(act):
        bufs = _copies_wait(l_send, l_recv, bufs1, act, _gather_ici_copies, "gather_l1_ici_wait")
        stage["l1"] = _copies_start(bufs, act, _gather_d2d_copies, "gather_l1_d2d_start")
        return stage["l1"][3]

    h, saved[0] = _layer_fwd(0, xs, layers[0], dict(mid=mid, ffn=ffn, down=down))
    send, recv, bufs, _ = stage["l1"]
    g1 = _copies_wait(send, recv, bufs, h, _gather_d2d_copies, "gather_l1_d2d_wait")
    layers[1] = {**mixer_params(1, g1[0], g1[1], g1[4]), **ffn_params(g1[2], g1[3])}
    h, saved[1] = _layer_fwd(1, h, layers[1])
    loss_part, dh, g_final = _loss_head(h, final_norm_w, tgt, "loss_head")

    def big_partials(g_layer, names=BIG):
        out = []
        for k in names:
            g = g_layer[k]
            if k == "w_in":
                g = _proj_to_w_in(g)
                g = jnp.transpose(g.reshape(g.shape[0], 4, IN_COLS // 4), (1, 0, 2))
            elif k == "w_out":
                g = _mixed_to_rows(g).reshape(4, D_MODEL // 4, D_MODEL)
            elif k == "ffn_down":
                g = g.reshape(4, D_FF // 4, D_MODEL)
            out.append(g.reshape(4, 2, g.shape[1] // 2, g.shape[2]))
        return out

    FFN, MIX = ("ffn_up", "ffn_down", "w_out"), ("w_in",)
    grads = [None] * N_LAYERS
    dh, grads[1] = _layer_bwd(1, dh, layers[1], saved[1])
    send1, recv1, part1, dh, lands1 = _rs_direct_start(big_partials(grads[1]), dh, "rs_direct_start_1")
    sent0 = []

    def after_ffn(dx2, g):
        send0, recv0, part0, dx2, lands0 = _rs_direct_start(big_partials(g, FFN), dx2, "rs_direct_start_0")
        sent0.extend([send0, recv0, part0, lands0])
        return dx2

    sentm = []

    def after_mix(dproj, g):
        sendm, recvm, partm, dproj, landsm = _rs_direct_start(big_partials(g, MIX), dproj, "rs_direct_start_0m")
        sentm.extend([sendm, recvm, partm, landsm])
        return dproj

    dh, grads[0] = _layer_bwd(0, dh, layers[0], saved[0], after_ffn, after_mix)
    sendm, recvm, partm, landsm = sentm

    small_names = SMALL_REPLICATED + SMALL_SHARDED
    small_list = [jnp.stack([grads[l][k].reshape(W[k].shape[1:]) if k in SMALL_REPLICATED else grads[l][k] for l in range(N_LAYERS)])
                  for k in small_names]
    small_list += [g_final.reshape(D_MODEL), loss_part[0, 0:1]]
    a_send, a_recv, packed, a_land = _ar_start(_pack(small_list), "ar_start")

    part1, lands1 = _rs_direct_wait(send1, recv1, part1, lands1, dh, "rs_direct_wait_1")
    part0, lands0 = _rs_direct_wait(*sent0, dh, "rs_direct_wait_0")
    red = {k: _rs_direct_sum(g, ld, 1, f"rs_sum_1_{k}") for k, g, ld in zip(BIG, part1, lands1)}
    for k, g, ld in zip(FFN, part0, lands0):
        red[k] = _rs_direct_sum(g, ld, 0, f"rs_sum_0_{k}", into=red[k])
    G, DELTA, NM, NV = {}, {}, {}, {}

    def update_big(names, shared):
        for k, r in zip(names, shared):
            G[k] = r.reshape(N_LAYERS, 2 * r.shape[2], r.shape[3])
            DELTA[k], NM[k], NV[k] = _adamw(W[k], G[k], M[k], V[k], f"adam_{k}")

    update_big(FFN, _share_halves([red[k] for k in FFN], "rs_share_ffn"))
    partm, landsm = _rs_direct_wait(sendm, recvm, partm, landsm, DELTA[FFN[-1]], "rs_direct_wait_0m")
    for k, g, ld in zip(MIX, partm, landsm):
        red[k] = _rs_direct_sum(g, ld, 0, f"rs_sum_0_{k}", into=red[k])
    update_big(MIX, _share_halves([red[k] for k in MIX], "rs_share_mix"))
    grad_x = dh.reshape(x.shape)

    packed, a_land = _ar_wait(a_send, a_recv, packed, a_land, G[MIX[-1]], "ar_wait")
    reduced = _unpack(_ar_sum(packed, a_land, "ar_sum"), small_list)
    small_g = dict(zip(small_names, reduced[:len(small_names)]))
    small_g["final_norm_w"] = reduced[-2]
    loss = reduced[-1][0]
    for k in SMALL_SHARDED:
        width = W[k].shape[2]
        small_g[k] = lax.dynamic_slice_in_dim(small_g[k], shard * width, width, axis=2)

    small_all = small_names + ("final_norm_w",)
    dl, nm, nv = _adamw(_pack([W[k] for k in small_all]), _pack([small_g[k] for k in small_all]),
                        _pack([M[k] for k in small_all]), _pack([V[k] for k in small_all]), "adam_small")
    like = [W[k] for k in small_all]
    for k, d_, m_, v_ in zip(small_all, _unpack(dl, like), _unpack(nm, like), _unpack(nv, like)):
        G[k], DELTA[k], NM[k], NV[k] = small_g[k], d_, m_, v_

    return (loss, grad_x, *[G[k] for k in WEIGHTS], *[DELTA[k] for k in WEIGHTS], *[NM[k] for k in WEIGHTS],
            *[NV[k] for k in WEIGHTS])
```

```python
import functools

import jax
import jax.numpy as jnp
from jax import lax
from jax.experimental import pallas as pl
from jax.experimental.pallas import tpu as pltpu

F32 = jnp.float32
BF16 = jnp.bfloat16
_MXU = jnp.bfloat16

D_MODEL = 2048
N_LAYERS = 2
POOL_W = 512
POOL_G = 4
POOL_GD = 128
POOL_WINDOWS = (2, 4, 8, 16)
POOL_HALO = 16
GDN_W = 768
GDN_H = 6
GDN_DH = 128
GDN_C = 64
LRU_W = 768
LRU_NB = 6
LRU_BD = 128
LRU_C = 8.0
D_FF = 6144
EPS = 1e-6
IN_COLS = 5132
HALO = 8

PQ, PK, PV, PZ, PXR, PGR, PPOOL, PAB, PCOLS = 0, 768, 1536, 2304, 3072, 3840, 4608, 5120, 5376
CB = 768

ADAM_LR = 0.001
ADAM_B1 = 0.9
ADAM_B2 = 0.999
ADAM_EPS = 1e-08
ADAM_WD = 0.01
ADAM_STEP = 10

VMEM_LIMIT = 56 * 1024 * 1024
MESH = pl.DeviceIdType.MESH
HBM = pl.BlockSpec(memory_space=pltpu.HBM)


def _cp(*sem):
    return pltpu.CompilerParams(dimension_semantics=sem, vmem_limit_bytes=VMEM_LIMIT)


def _dg(a, b, ta, tb):
    dims = (((0 if ta else 1,), (1 if tb else 0,)), ((), ()))
    return lax.dot_general(a, b, dims, preferred_element_type=F32)


def _split2(a):
    hi = a.astype(BF16)
    lo = (a - hi.astype(F32)).astype(BF16)
    return hi, lo


def _mm_raw(a, b, ta, tb, hi):
    if _MXU == F32:
        return _dg(a, b, ta, tb)
    if not hi:
        return _dg(a.astype(_MXU), b.astype(_MXU), ta, tb)
    a1, a2 = _split2(a)
    b1, b2 = _split2(b)
    return _dg(a1, b1, ta, tb) + (_dg(a1, b2, ta, tb) + _dg(a2, b1, ta, tb))


@functools.partial(jax.custom_vjp, nondiff_argnums=(2, 3, 4))
def _mm(a, b, ta=False, tb=False, hi=False):
    return _mm_raw(a, b, ta, tb, hi)


def _mm_fwd(a, b, ta, tb, hi):
    return _mm_raw(a, b, ta, tb, hi), (a, b)


def _mm_bwd(ta, tb, hi, res, dc):
    a, b = res
    da = _mm(b, dc, tb, True, hi) if ta else _mm(dc, b, False, not tb, hi)
    db = _mm(dc, a, True, ta, hi) if tb else _mm(a, dc, not ta, False, hi)
    return da, db


_mm.defvjp(_mm_fwd, _mm_bwd)


def _mm01(m01, x):
    if _MXU == F32:
        return _dg(m01, x, False, False)
    m = m01.astype(BF16)
    x1 = x.astype(BF16)
    r = x - x1.astype(F32)
    x2 = r.astype(BF16)
    x3 = (r - x2.astype(F32)).astype(BF16)
    return _dg(m, x1, False, False) + (_dg(m, x2, False, False) + _dg(m, x3, False, False))


def _down(x, k):
    return x if k == 0 else pltpu.roll(x, k, 0)


def _up(x, k):
    return x if k == 0 else pltpu.roll(x, x.shape[0] - k, 0)


def _rows(shape):
    return lax.broadcasted_iota(jnp.int32, shape, 0)


def _lanes(shape):
    return lax.broadcasted_iota(jnp.int32, shape, 1)


def _matmul(a, b, mode, *, name, res=None, tm=1024, tn=1024, tk=2048, b_split=False, o_split=0, out_dtype=F32):
    ta, tb = mode == "tn", mode == "nt"
    a_split = a.ndim == 3
    if a_split:
        assert not ta
        M, K = a.shape[1], a.shape[0] * a.shape[2]
        tk = min(tk, a.shape[2])
    elif ta:
        K, M = a.shape
    else:
        M, K = a.shape
    if b_split:
        ns = b.shape[0]
        N = b.shape[1] if tb else ns * b.shape[2]
    else:
        N = b.shape[0] if tb else b.shape[1]
    tm, tn, tk = min(tm, M), min(tn, N), min(tk, K)
    if b_split:
        per = b.shape[2]
        if tb:
            tk = min(tk, per)
        else:
            tn = min(tn, per)
    if o_split:
        tn = min(tn, N // o_split)
    assert M % tm == 0 and N % tn == 0 and K % tk == 0, (name, M, N, K, tm, tn, tk)
    nk = K // tk
    if a_split:
        ka = a.shape[2] // tk
        a_spec = pl.BlockSpec((None, tm, tk), lambda i, j, k: (k // ka, i, k % ka))
    else:
        a_spec = pl.BlockSpec((tk, tm), lambda i, j, k: (k, i)) if ta else pl.BlockSpec((tm, tk), lambda i, j, k: (i, k))
    if not b_split:
        b_spec = pl.BlockSpec((tn, tk), lambda i, j, k: (j, k)) if tb else pl.BlockSpec((tk, tn), lambda i, j, k: (k, j))
    elif tb:
        kb = per // tk
        b_spec = pl.BlockSpec((None, tn, tk), lambda i, j, k: (k // kb, j, k % kb))
    else:
        nb = per // tn
        b_spec = pl.BlockSpec((None, tk, tn), lambda i, j, k: (j // nb, k, j % nb))
    if o_split:
        ob = (N // o_split) // tn
        out_shape = jax.ShapeDtypeStruct((o_split, M, N // o_split), out_dtype)
        o_spec = pl.BlockSpec((None, tm, tn), lambda i, j, k: (j // ob, i, j % ob))
    else:
        out_shape = jax.ShapeDtypeStruct((M, N), out_dtype)
        o_spec = pl.BlockSpec((tm, tn), lambda i, j, k: (i, j))
    in_specs = [a_spec, b_spec]
    args = [a, b]
    if res is not None:
        in_specs.append(pl.BlockSpec((tm, tn), lambda i, j, k: (i, j)))
        args.append(res)
    use_acc = nk > 1 and out_dtype != F32

    def body(*refs):
        a_ref, b_ref = refs[0], refs[1]
        o_ref = refs[2 + (res is not None)]
        acc_ref = refs[-1] if use_acc else o_ref
        p = _dg(a_ref[...].astype(_MXU), b_ref[...].astype(_MXU), ta, tb)
        first = p + refs[2][...] if res is not None else p
        if nk == 1:
            o_ref[...] = first.astype(o_ref.dtype)
        else:
            k = pl.program_id(2)

            @pl.when(k == 0)
            def _():
                acc_ref[...] = first

            @pl.when(k > 0)
            def _():
                acc_ref[...] += p

            if use_acc:
                @pl.when(k == nk - 1)
                def _():
                    o_ref[...] = acc_ref[...].astype(o_ref.dtype)

    return pl.pallas_call(
        body, name=name, grid=(M // tm, N // tn, nk), in_specs=in_specs, out_specs=o_spec, out_shape=out_shape,
        scratch_shapes=[pltpu.VMEM((tm, tn), F32)] if use_acc else [],
        compiler_params=_cp("parallel", "parallel", "arbitrary"),
    )(*args)


def _rms(x, w):
    return x * lax.rsqrt(jnp.mean(x * x, axis=-1, keepdims=True) + EPS) * w


def _row_tile(S, t=512):
    t = min(t, S)
    assert S % t == 0
    return t


def _rms_fwd(x, w, name):
    S, D = x.shape
    T = _row_tile(S)

    def body(x_ref, w_ref, o_ref):
        o_ref[...] = _rms(x_ref[...], w_ref[...]).astype(o_ref.dtype)

    return pl.pallas_call(
        body, name=name, grid=(S // T,),
        in_specs=[pl.BlockSpec((T, D), lambda i: (i, 0)), pl.BlockSpec((1, D), lambda i: (0, 0))],
        out_specs=pl.BlockSpec((T, D), lambda i: (i, 0)), out_shape=jax.ShapeDtypeStruct((S, D), BF16),
        compiler_params=_cp("parallel"),
    )(x, w.reshape(1, D))


def _rms_bwd(x, w, dh, dres, name):
    S, D = x.shape
    T = _row_tile(S)

    def body(x_ref, w_ref, dh_ref, dr_ref, dx_ref, gw_ref):
        _, vjp = jax.vjp(_rms, x_ref[...], w_ref[...])
        dx, dw = vjp(dh_ref[...])
        dx_ref[...] = dr_ref[...] + dx

        @pl.when(pl.program_id(0) == 0)
        def _():
            gw_ref[...] = jnp.zeros_like(gw_ref)

        gw_ref[...] += dw

    row = pl.BlockSpec((T, D), lambda i: (i, 0))
    vec = pl.BlockSpec((1, D), lambda i: (0, 0))
    return pl.pallas_call(
        body, name=name, grid=(S // T,), in_specs=[row, vec, row, row], out_specs=[row, vec],
        out_shape=[jax.ShapeDtypeStruct((S, D), F32), jax.ShapeDtypeStruct((1, D), F32)],
        compiler_params=_cp("arbitrary"),
    )(x, w.reshape(1, D), dh, dres)


def _loss_head(x, w, tgt, name):
    S, D = x.shape
    T = _row_tile(S)

    def body(x_ref, w_ref, t_ref, l_ref, dx_ref, gw_ref):
        y, vjp = jax.vjp(_rms, x_ref[...], w_ref[...])
        err = y - t_ref[...]
        part = 0.5 * jnp.sum(jnp.mean(err * err, axis=-1, keepdims=True), axis=0, keepdims=True)
        dx, dw = vjp(err * (1.0 / D))
        dx_ref[...] = dx

        @pl.when(pl.program_id(0) == 0)
        def _():
            gw_ref[...] = jnp.zeros_like(gw_ref)
            l_ref[...] = jnp.zeros_like(l_ref)

        gw_ref[...] += dw
        l_ref[...] += jnp.broadcast_to(part, l_ref.shape)

    row = pl.BlockSpec((T, D), lambda i: (i, 0))
    vec = pl.BlockSpec((1, D), lambda i: (0, 0))
    return pl.pallas_call(
        body, name=name, grid=(S // T,), in_specs=[row, vec, row],
        out_specs=[pl.BlockSpec((8, 128), lambda i: (0, 0)), row, vec],
        out_shape=[jax.ShapeDtypeStruct((8, 128), F32), jax.ShapeDtypeStruct((S, D), F32), jax.ShapeDtypeStruct((1, D), F32)],
        compiler_params=_cp("arbitrary"),
    )(x, w.reshape(1, D), tgt)


def _by_group(shape, vals):
    g = _lanes(shape) // POOL_GD
    out = vals[-1]
    for k in range(len(vals) - 2, -1, -1):
        out = jnp.where(g == k, vals[k], out)
    return out


def _pool_d(prev, u, t0):
    ext = jnp.concatenate([prev, u], axis=0)
    s2 = ext + _down(ext, 1)
    s4 = s2 + _down(s2, 2)
    s8 = s4 + _down(s4, 4)
    s16 = s8 + _down(s8, 8)
    ssel = _by_group(ext.shape, [s2, s4, s8, s16])[POOL_HALO:]
    win = _by_group(u.shape, [jnp.int32(w) for w in POOL_WINDOWS])
    cnt = jnp.minimum(t0 + _rows(u.shape) + 1, win).astype(F32)
    return ssel / cnt - u


def _pool_lin(d, w_ref, b):
    ys = [_mm(d[:, g * POOL_GD:(g + 1) * POOL_GD], w_ref[g]) for g in range(POOL_G)]
    return jnp.concatenate(ys, axis=1) + b


def _pool_fwd(proj, w, b, scale, name):
    S = proj.shape[0]
    T = _row_tile(S)
    r = T // POOL_HALO
    cb = PPOOL // POOL_W

    def body(u_ref, up_ref, w_ref, b_ref, sc_ref, y_ref):
        i = pl.program_id(0)
        prev = jnp.where(i > 0, up_ref[...], 0.0)
        d = _pool_d(prev, u_ref[...], i * T)
        y_ref[...] = _pool_lin(d, w_ref, b_ref[...]) * sc_ref[...]

    vec = pl.BlockSpec((1, POOL_W), lambda i: (0, 0))
    return pl.pallas_call(
        body, name=name, grid=(S // T,),
        in_specs=[pl.BlockSpec((T, POOL_W), lambda i: (i, cb)),
                  pl.BlockSpec((POOL_HALO, POOL_W), lambda i: (jnp.maximum(i * r - 1, 0), cb)),
                  pl.BlockSpec((POOL_G, POOL_GD, POOL_GD), lambda i: (0, 0, 0)), vec, vec],
        out_specs=pl.BlockSpec((T, POOL_W), lambda i: (i, 0)), out_shape=jax.ShapeDtypeStruct((S, POOL_W), F32),
        compiler_params=_cp("parallel"),
    )(proj, proj, w, b.reshape(1, POOL_W), scale.reshape(1, POOL_W))


def _pool_bwd(proj, dmixed, w, b, scale, name):
    S = proj.shape[0]
    T = _row_tile(S)
    n = S // T
    r = T // POOL_HALO
    cb = PPOOL // POOL_W
    mb = 1536 // POOL_W

    def body(u_ref, up_ref, dy_ref, dyn_ref, w_ref, b_ref, sc_ref, du_ref, gw_ref, gb_ref, gs_ref):
        i = pl.program_id(0)
        sc = sc_ref[...]
        dy = dy_ref[...]
        dy_ext = jnp.concatenate([dy, jnp.where(i < n - 1, dyn_ref[...], 0.0)], axis=0)
        dyl = dy_ext * sc
        dd = jnp.concatenate(
            [_mm(dyl[:, g * POOL_GD:(g + 1) * POOL_GD], w_ref[g], False, True) for g in range(POOL_G)], axis=1)
        t_ext = i * T + _rows(dd.shape)
        win = _by_group(dd.shape, [jnp.int32(v) for v in POOL_WINDOWS])
        cnt = jnp.minimum(t_ext + 1, win).astype(F32)
        e = jnp.where(t_ext < S, dd / cnt, 0.0)
        f2 = e + _up(e, 1)
        f4 = f2 + _up(f2, 2)
        f8 = f4 + _up(f4, 4)
        f16 = f8 + _up(f8, 8)
        du = (_by_group(dd.shape, [f2, f4, f8, f16]) - dd)[:T]
        du_ref[...] = du.astype(du_ref.dtype)

        prev = jnp.where(i > 0, up_ref[...], 0.0)
        d = _pool_d(prev, u_ref[...], i * T)
        ylin = _pool_lin(d, w_ref, b_ref[...])
        dyl_m = dy * sc

        @pl.when(i == 0)
        def _():
            gw_ref[...] = jnp.zeros_like(gw_ref)
            gb_ref[...] = jnp.zeros_like(gb_ref)
            gs_ref[...] = jnp.zeros_like(gs_ref)

        gs_ref[...] += jnp.sum(dy * ylin, axis=0, keepdims=True)
        gb_ref[...] += jnp.sum(dyl_m, axis=0, keepdims=True)
        for g in range(POOL_G):
            sl = slice(g * POOL_GD, (g + 1) * POOL_GD)
            gw_ref[g] += _mm(d[:, sl], dyl_m[:, sl], True, False)

    vec = pl.BlockSpec((1, POOL_W), lambda i: (0, 0))
    wsp = pl.BlockSpec((POOL_G, POOL_GD, POOL_GD), lambda i: (0, 0, 0))
    nh = S // POOL_HALO
    return pl.pallas_call(
        body, name=name, grid=(n,),
        in_specs=[pl.BlockSpec((T, POOL_W), lambda i: (i, cb)),
                  pl.BlockSpec((POOL_HALO, POOL_W), lambda i: (jnp.maximum(i * r - 1, 0), cb)),
                  pl.BlockSpec((T, POOL_W), lambda i: (i, mb)),
                  pl.BlockSpec((POOL_HALO, POOL_W), lambda i: (jnp.minimum((i + 1) * r, nh - 1), mb)),
                  wsp, vec, vec],
        out_specs=[pl.BlockSpec((T, POOL_W), lambda i: (i, 0)), wsp, vec, vec],
        out_shape=[jax.ShapeDtypeStruct((S, POOL_W), BF16), jax.ShapeDtypeStruct((POOL_G, POOL_GD, POOL_GD), F32),
                   jax.ShapeDtypeStruct((1, POOL_W), F32), jax.ShapeDtypeStruct((1, POOL_W), F32)],
        compiler_params=_cp("arbitrary"),
    )(proj, proj, dmixed, dmixed, w, b.reshape(1, POOL_W), scale.reshape(1, POOL_W))


def _conv_rows(ext, w_ref, taps):
    acc = w_ref[taps - 1:taps, :] * ext
    for k in range(1, taps):
        acc = acc + w_ref[taps - 1 - k:taps - k, :] * _down(ext, k)
    return acc


def _conv_t_rows(dc, w_ref, taps):
    acc = w_ref[taps - 1:taps, :] * dc
    for k in range(1, taps):
        acc = acc + w_ref[taps - 1 - k:taps - k, :] * _up(dc, k)
    return acc


def _conv_specs(T, S, ncb0, with_next):
    r = T // HALO
    nh = S // HALO
    main = pl.BlockSpec((T, CB), lambda j, i: (i, j + ncb0))
    prev = pl.BlockSpec((HALO, CB), lambda j, i: (jnp.maximum(i * r - 1, 0), j + ncb0))
    nxt = pl.BlockSpec((HALO, CB), lambda j, i: (jnp.minimum((i + 1) * r, nh - 1), j + ncb0))
    return (main, prev, nxt) if with_next else (main, prev)


def _gdn_conv_fwd(proj, w, name):
    S = proj.shape[0]
    T = _row_tile(S)
    taps = w.shape[0]
    ncb = 3 * GDN_W // CB

    def body(x_ref, xp_ref, w_ref, o_ref):
        i = pl.program_id(1)
        ext = jnp.concatenate([jnp.where(i > 0, xp_ref[...], 0.0), x_ref[...]], axis=0)
        o_ref[...] = jax.nn.silu(_conv_rows(ext, w_ref, taps)[HALO:])

    main, prev = _conv_specs(T, S, PQ // CB, False)
    return pl.pallas_call(
        body, name=name, grid=(ncb, S // T),
        in_specs=[main, prev, pl.BlockSpec((taps, CB), lambda j, i: (0, j))],
        out_specs=pl.BlockSpec((T, CB), lambda j, i: (i, j)), out_shape=jax.ShapeDtypeStruct((S, 3 * GDN_W), F32),
        compiler_params=_cp("parallel", "parallel"),
    )(proj, proj, w)


def _gdn_conv_bwd(proj, dact, w, name):
    S = proj.shape[0]
    T = _row_tile(S)
    n = S // T
    taps = w.shape[0]
    ncb = 3 * GDN_W // CB

    def body(x_ref, xp_ref, xn_ref, d_ref, dn_ref, w_ref, dx_ref, gw_ref):
        i = pl.program_id(1)
        last = i == n - 1
        ext = jnp.concatenate([jnp.where(i > 0, xp_ref[...], 0.0), x_ref[...], jnp.where(last, 0.0, xn_ref[...])], axis=0)
        c = _conv_rows(ext, w_ref, taps)[HALO:]
        d_ext = jnp.concatenate([d_ref[...], jnp.where(last, 0.0, dn_ref[...])], axis=0)
        _, vjp = jax.vjp(jax.nn.silu, c)
        dc = vjp(d_ext)[0]
        dx_ref[...] = _conv_t_rows(dc, w_ref, taps)[:T].astype(dx_ref.dtype)

        @pl.when(i == 0)
        def _():
            gw_ref[...] = jnp.zeros_like(gw_ref)

        dcm = dc[:T]
        for k in range(taps):
            gw_ref[taps - 1 - k:taps - k, :] += jnp.sum(dcm * _down(ext, k)[HALO:HALO + T], axis=0, keepdims=True)

    main, prev, nxt = _conv_specs(T, S, PQ // CB, True)
    dmain, _, dnxt = _conv_specs(T, S, 0, True)
    wsp = pl.BlockSpec((taps, CB), lambda j, i: (0, j))
    return pl.pallas_call(
        body, name=name, grid=(ncb, n), in_specs=[main, prev, nxt, dmain, dnxt, wsp],
        out_specs=[pl.BlockSpec((T, CB), lambda j, i: (i, j)), wsp],
        out_shape=[jax.ShapeDtypeStruct((S, 3 * GDN_W), BF16), jax.ShapeDtypeStruct((taps, 3 * GDN_W), F32)],
        compiler_params=_cp("parallel", "arbitrary"),
    )(proj, proj, proj, dact, dact, w)


def _ffn_act_fwd(up, w, name):
    S = up.shape[0]
    T = _row_tile(S)
    taps = w.shape[0]
    ncb = D_FF // CB

    def body(g_ref, gp_ref, v_ref, w_ref, o_ref):
        i = pl.program_id(1)
        ext = jnp.concatenate([jnp.where(i > 0, gp_ref[...], 0.0), g_ref[...]], axis=0)
        c = _conv_rows(ext, w_ref, taps)[HALO:]
        o_ref[...] = (jax.nn.gelu(c) * v_ref[...]).astype(o_ref.dtype)

    main, prev = _conv_specs(T, S, 0, False)
    val = pl.BlockSpec((T, CB), lambda j, i: (i, j + ncb))
    return pl.pallas_call(
        body, name=name, grid=(ncb, S // T),
        in_specs=[main, prev, val, pl.BlockSpec((taps, CB), lambda j, i: (0, j))],
        out_specs=pl.BlockSpec((T, CB), lambda j, i: (i, j)), out_shape=jax.ShapeDtypeStruct((S, D_FF), BF16),
        compiler_params=_cp("parallel", "parallel"),
    )(up, up, up, w)


def _ffn_act_bwd(up, dact, w, name):
    S = up.shape[0]
    T = _row_tile(S)
    n = S // T
    taps = w.shape[0]
    ncb = D_FF // CB

    def body(g_ref, gp_ref, gn_ref, v_ref, vn_ref, d_ref, dn_ref, w_ref, dup_ref, gw_ref):
        i = pl.program_id(1)
        last = i == n - 1
        ext = jnp.concatenate([jnp.where(i > 0, gp_ref[...], 0.0), g_ref[...], jnp.where(last, 0.0, gn_ref[...])], axis=0)
        c = _conv_rows(ext, w_ref, taps)[HALO:]
        v_ext = jnp.concatenate([v_ref[...], jnp.where(last, 0.0, vn_ref[...])], axis=0)
        d_ext = jnp.concatenate([d_ref[...], jnp.where(last, 0.0, dn_ref[...])], axis=0)
        gl, vjp = jax.vjp(jax.nn.gelu, c)
        dup_ref[1] = (d_ext * gl)[:T].astype(dup_ref.dtype)
        dc = vjp(d_ext * v_ext)[0]
        dup_ref[0] = _conv_t_rows(dc, w_ref, taps)[:T].astype(dup_ref.dtype)

        @pl.when(i == 0)
        def _():
            gw_ref[...] = jnp.zeros_like(gw_ref)

        dcm = dc[:T]
        for k in range(taps):
            gw_ref[taps - 1 - k:taps - k, :] += jnp.sum(dcm * _down(ext, k)[HALO:HALO + T], axis=0, keepdims=True)

    main, prev, nxt = _conv_specs(T, S, 0, True)
    vmain, _, vnxt = _conv_specs(T, S, ncb, True)
    wsp = pl.BlockSpec((taps, CB), lambda j, i: (0, j))
    osp = pl.BlockSpec((2, T, CB), lambda j, i: (0, i, j))
    return pl.pallas_call(
        body, name=name, grid=(ncb, n), in_specs=[main, prev, nxt, vmain, vnxt, main, nxt, wsp],
        out_specs=[osp, wsp],
        out_shape=[jax.ShapeDtypeStruct((2, S, D_FF), BF16), jax.ShapeDtypeStruct((taps, D_FF), F32)],
        compiler_params=_cp("parallel", "arbitrary"),
    )(up, up, up, up, up, dact, dact, w)


def _tri_masks():
    r = _rows((GDN_C, GDN_C))
    c = _lanes((GDN_C, GDN_C))
    return r >= c, r > c


def _each(fn, *cols):
    return tuple(fn(*args) for args in zip(*cols))


def _tri_inv_raw(lows):
    r = _rows(lows[0].shape)
    c = _lanes(lows[0].shape)
    eye = jnp.where(r == c, 1.0, 0.0)
    ps = _each(lambda low: eye - low, lows)
    lps = lows
    for _ in range(5):
        lps = _each(lambda lp: _mm(lp, lp, False, False, True), lps)
        ps = _each(lambda p, lp: p + _mm(p, lp, False, False, True), ps, lps)
    return ps


@jax.custom_vjp
def _tri_inv(lows):
    return _tri_inv_raw(lows)


def _tri_inv_fwd(lows):
    ts = _tri_inv_raw(lows)
    return ts, ts


def _tri_inv_bwd(ts, dts):
    inner = _each(lambda t, dt: _mm(t, dt, True, False, True), ts, dts)
    return (_each(lambda m, t: -_mm(m, t, False, True, True), inner, ts),)


_tri_inv.defvjp(_tri_inv_fwd, _tri_inv_bwd)


@jax.custom_vjp
def _tri_inv_given(lows, ts):
    return ts


def _tri_inv_given_fwd(lows, ts):
    return ts, ts


def _tri_inv_given_bwd(ts, dts):
    return _tri_inv_bwd(ts, dts)[0], _each(jnp.zeros_like, ts)


_tri_inv_given.defvjp(_tri_inv_given_fwd, _tri_inv_given_bwd)


def _gdn_glog(a_col, alog, dtb):
    return -jnp.exp(alog) * jax.nn.softplus(a_col + dtb)


def _decay_operand():
    r = _rows((GDN_C, 2 * GDN_C))
    c = _lanes((GDN_C, 2 * GDN_C))
    return jnp.where((c >= GDN_C) | (r > c), 1.0, 0.0)


def _gdn_decay(glog):
    causal, _ = _tri_masks()
    res = _mm01(jnp.where(causal, 1.0, 0.0), glog * _decay_operand())
    return res[:, GDN_C:GDN_C + 1], res[:, :GDN_C]


def _gdn_decay_bwd(dgcol, dd):
    r = _rows((GDN_C, GDN_C))
    c = _lanes((GDN_C, GDN_C))
    dres = jnp.concatenate([dd, jnp.where(c == 0, dgcol, 0.0)], axis=1)
    dx = _mm01(jnp.where(r <= c, 1.0, 0.0), dres)
    return jnp.sum(dx * _decay_operand(), axis=1, keepdims=True)


def _gdn_chunk(qa, ka, va, bt_col, gcol, dmat, t_saved=None):
    causal, strict = _tri_masks()
    qn = _each(lambda q: q * lax.rsqrt(jnp.sum(q * q, axis=-1, keepdims=True) + EPS) * (GDN_DH ** -0.5), qa)
    kn = _each(lambda k: k * lax.rsqrt(jnp.sum(k * k, axis=-1, keepdims=True) + EPS), ka)
    beta = _each(jax.nn.sigmoid, bt_col)
    eg = _each(jnp.exp, gcol)
    decay = _each(lambda d: jnp.where(causal, jnp.exp(d), 0.0), dmat)
    kk = _each(lambda k: _mm(k, k, False, True), kn)
    low = _each(lambda b, m, d: jnp.where(strict, b * m * d, 0.0), beta, kk, decay)
    t = _tri_inv(low) if t_saved is None else _tri_inv_given(low, t_saved)
    w = _each(lambda t_, k, b, e: _mm(t_, k * (b * e), False, False, True), t, kn, beta, eg)
    u = _each(lambda t_, v, b: _mm(t_, v * b, False, False, True), t, va, beta)
    attn = _each(lambda q, k, d: _mm(q, k, False, True) * d, qn, kn, decay)
    last = _rows(gcol[0].shape) == GDN_C - 1
    g_last = _each(lambda g: jnp.sum(jnp.where(last, g, 0.0), axis=0, keepdims=True), gcol)
    qd = _each(lambda q, e: q * e, qn, eg)
    kd = _each(lambda k, gl, g: k * jnp.exp(gl - g), kn, g_last, gcol)
    return (w, u, qd, kd, attn), t


def _gdn_step(state, w, u, qd, kd, attn, egl):
    v_new = _each(lambda u_, w_, s: u_ - _mm(w_, s), u, w, state)
    o_state = _each(_mm, qd, state)
    o = _each(lambda os, a, v: os + _mm(a, v), o_state, attn, v_new)
    new = _each(lambda s, e, k, v: s * e + _mm(k, v, True, False), state, egl, kd, v_new)
    return o, new


def _heads(ref, base=0, width=GDN_DH):
    return tuple(ref[:, (base + h) * GDN_DH:(base + h) * GDN_DH + width] for h in range(GDN_H))


def _cols(a, base):
    return tuple(a[:, base + h:base + h + 1] for h in range(GDN_H))


def _gated_norm(o, z, nw):
    return o * lax.rsqrt(jnp.mean(o * o, axis=-1, keepdims=True) + EPS) * nw * jax.nn.silu(z)


def _hsl(h):
    return slice(h * GDN_DH, (h + 1) * GDN_DH)


def _pad_lanes(a, width=GDN_DH):
    return jnp.concatenate([a, jnp.zeros((a.shape[0], width - a.shape[1]), a.dtype)], axis=1)


def _gdn_prep(qkv, proj, alog, dtb, name):
    S = qkv.shape[0]
    N = S // GDN_C

    def body(qkv_ref, ab_ref, al_ref, dt_ref, w_ref, u_ref, qd_ref, kd_ref, at_ref, ti_ref, gc_ref):
        ab = ab_ref[...]
        glog = _each(_gdn_glog, _cols(ab, 0), _cols(al_ref[...], 0), _cols(dt_ref[...], 0))
        dec = _each(_gdn_decay, glog)
        gcol, dmat = _each(lambda d: d[0], dec), _each(lambda d: d[1], dec)
        (w, u, qd, kd, attn), tinv = _gdn_chunk(_heads(qkv_ref), _heads(qkv_ref, GDN_H), _heads(qkv_ref, 2 * GDN_H),
                                                _cols(ab, GDN_H), gcol, dmat)
        gc = jnp.zeros((GDN_C, 128), F32)
        for h in range(GDN_H):
            w_ref[:, _hsl(h)] = w[h]
            u_ref[:, _hsl(h)] = u[h]
            qd_ref[:, _hsl(h)] = qd[h]
            kd_ref[:, _hsl(h)] = kd[h]
            at_ref[:, _hsl(h)] = _pad_lanes(attn[h])
            ti_ref[:, _hsl(h)] = _pad_lanes(tinv[h])
            gc = jnp.where(_lanes(gc.shape) == h, gcol[h], gc)
        gc_ref[...] = gc

    vec = pl.BlockSpec((1, 128), lambda i: (0, 0))
    hsp = pl.BlockSpec((GDN_C, GDN_W), lambda i: (i, 0))
    hshape = jax.ShapeDtypeStruct((S, GDN_W), F32)
    return pl.pallas_call(
        body, name=name, grid=(N,),
        in_specs=[pl.BlockSpec((GDN_C, 3 * GDN_W), lambda i: (i, 0)), pl.BlockSpec((GDN_C, 128), lambda i: (i, PAB // 128)), vec, vec],
        out_specs=[hsp] * 6 + [pl.BlockSpec((GDN_C, 128), lambda i: (i, 0))],
        out_shape=[hshape] * 6 + [jax.ShapeDtypeStruct((S, 128), F32)],
        compiler_params=_cp("parallel"),
    )(qkv, proj, alog, dtb)


def _gdn_scan(w, u, qd, kd, attn, gc, name):
    S = w.shape[0]
    N = S // GDN_C

    def body(w_ref, u_ref, qd_ref, kd_ref, at_ref, gc_ref, o_ref, st_ref, s_ref):
        @pl.when(pl.program_id(0) == 0)
        def _():
            s_ref[...] = jnp.zeros_like(s_ref)

        state = tuple(s_ref[_hsl(h), :] for h in range(GDN_H))
        egl = _each(jnp.exp, _cols(gc_ref[GDN_C - 1:GDN_C, :], 0))
        o, new = _gdn_step(state, _heads(w_ref), _heads(u_ref), _heads(qd_ref), _heads(kd_ref),
                           _heads(at_ref, width=GDN_C), egl)
        for h in range(GDN_H):
            st_ref[_hsl(h), :] = state[h]
            o_ref[:, _hsl(h)] = o[h]
            s_ref[_hsl(h), :] = new[h]

    hsp = pl.BlockSpec((GDN_C, GDN_W), lambda i: (i, 0))
    return pl.pallas_call(
        body, name=name, grid=(N,),
        in_specs=[hsp] * 5 + [pl.BlockSpec((GDN_C, 128), lambda i: (i, 0))],
        out_specs=[hsp, pl.BlockSpec((None, GDN_W, GDN_DH), lambda i: (i, 0, 0))],
        out_shape=[jax.ShapeDtypeStruct((S, GDN_W), F32), jax.ShapeDtypeStruct((N, GDN_W, GDN_DH), F32)],
        scratch_shapes=[pltpu.VMEM((GDN_W, GDN_DH), F32)],
        compiler_params=_cp("arbitrary"),
    )(w, u, qd, kd, attn, gc)


def _gdn_scan_bwd(w, u, qd, kd, attn, gc, states, o, proj, dmixed, nw, name):
    S = w.shape[0]
    N = S // GDN_C

    def body(w_ref, u_ref, qd_ref, kd_ref, at_ref, gc_ref, st_ref, o_ref, z_ref, dm_ref, nw_ref,
             dw_ref, du_ref, dqd_ref, dkd_ref, dat_ref, dgl_ref, dz_ref, gnw_ref, ds_ref):
        @pl.when(pl.program_id(0) == 0)
        def _():
            ds_ref[...] = jnp.zeros_like(ds_ref)
            gnw_ref[...] = jnp.zeros_like(gnw_ref)

        nw = nw_ref[...]
        _, vjp_n = jax.vjp(lambda o, z, w_: _each(lambda a, b: _gated_norm(a, b, w_), o, z), _heads(o_ref), _heads(z_ref), nw)
        do, dz, dnw = vjp_n(_heads(dm_ref))
        state = tuple(st_ref[_hsl(h), :] for h in range(GDN_H))
        egl = _each(jnp.exp, _cols(gc_ref[GDN_C - 1:GDN_C, :], 0))
        _, vjp_s = jax.vjp(_gdn_step, state, _heads(w_ref), _heads(u_ref), _heads(qd_ref), _heads(kd_ref),
                           _heads(at_ref, width=GDN_C), egl)
        ds, dw, du, dqd, dkd, dat, degl = vjp_s((do, tuple(ds_ref[_hsl(h), :] for h in range(GDN_H))))
        dgl = jnp.zeros((8, 128), F32)
        for h in range(GDN_H):
            dz_ref[:, _hsl(h)] = dz[h].astype(dz_ref.dtype)
            ds_ref[_hsl(h), :] = ds[h]
            dw_ref[:, _hsl(h)] = dw[h]
            du_ref[:, _hsl(h)] = du[h]
            dqd_ref[:, _hsl(h)] = dqd[h]
            dkd_ref[:, _hsl(h)] = dkd[h]
            dat_ref[:, _hsl(h)] = _pad_lanes(dat[h])
            dgl = jnp.where(_lanes(dgl.shape) == h, degl[h] * egl[h], dgl)
        dgl_ref[...] = dgl
        gnw_ref[...] += dnw

    rev = lambda i: (N - 1 - i, 0)
    hsp = pl.BlockSpec((GDN_C, GDN_W), rev)
    gsp = pl.BlockSpec((GDN_C, 128), rev)
    vec = pl.BlockSpec((1, GDN_DH), lambda i: (0, 0))
    hshape = jax.ShapeDtypeStruct((S, GDN_W), F32)
    return pl.pallas_call(
        body, name=name, grid=(N,),
        in_specs=[hsp] * 5 + [gsp, pl.BlockSpec((None, GDN_W, GDN_DH), lambda i: (N - 1 - i, 0, 0)), hsp,
                              pl.BlockSpec((GDN_C, GDN_W), lambda i: (N - 1 - i, PZ // GDN_W)), hsp, vec],
        out_specs=[hsp] * 5 + [pl.BlockSpec((8, 128), rev), hsp, vec],
        out_shape=[hshape] * 5 + [jax.ShapeDtypeStruct((N * 8, 128), F32), jax.ShapeDtypeStruct((S, GDN_W), BF16),
                                  jax.ShapeDtypeStruct((1, GDN_DH), F32)],
        scratch_shapes=[pltpu.VMEM((GDN_W, GDN_DH), F32)],
        compiler_params=_cp("arbitrary"),
    )(w, u, qd, kd, attn, gc, states, o, proj, dmixed, nw)


def _gdn_prep_bwd(qkv, proj, alog, dtb, tinv, dw, du, dqd, dkd, dat, dgl, name):
    S = qkv.shape[0]
    N = S // GDN_C

    def body(qkv_ref, ab_ref, al_ref, dt_ref, ti_ref, dw_ref, du_ref, dqd_ref, dkd_ref, dat_ref, dgl_ref,
             dqkv_ref, dab_ref, gal_ref, gdt_ref):
        @pl.when(pl.program_id(0) == 0)
        def _():
            gal_ref[...] = jnp.zeros_like(gal_ref)
            gdt_ref[...] = jnp.zeros_like(gdt_ref)

        ab = ab_ref[...]
        glog, vjp_g = jax.vjp(lambda a, al, dt: _each(_gdn_glog, a, al, dt), _cols(ab, 0), _cols(al_ref[...], 0),
                              _cols(dt_ref[...], 0))
        dec = _each(_gdn_decay, glog)
        gcol, dmat = _each(lambda d: d[0], dec), _each(lambda d: d[1], dec)
        _, vjp_c, _ = jax.vjp(functools.partial(_gdn_chunk, t_saved=_heads(ti_ref, width=GDN_C)), _heads(qkv_ref),
                              _heads(qkv_ref, GDN_H), _heads(qkv_ref, 2 * GDN_H), _cols(ab, GDN_H), gcol, dmat, has_aux=True)
        dqa, dka, dva, dbt, dgcol, dd = vjp_c((_heads(dw_ref), _heads(du_ref), _heads(dqd_ref), _heads(dkd_ref),
                                               _heads(dat_ref, width=GDN_C)))
        last = _rows(dgcol[0].shape) == GDN_C - 1
        dgcol = _each(lambda d, g: d + jnp.where(last, g, 0.0), dgcol, _cols(dgl_ref[0:1, :], 0))
        da_col, dal, ddt = vjp_g(_each(_gdn_decay_bwd, dgcol, dd))
        dab = jnp.zeros((GDN_C, 128), F32)
        gal = jnp.zeros((1, 128), F32)
        gdt = jnp.zeros((1, 128), F32)
        for h in range(GDN_H):
            dqkv_ref[:, _hsl(h)] = dqa[h]
            dqkv_ref[:, _hsl(GDN_H + h)] = dka[h]
            dqkv_ref[:, _hsl(2 * GDN_H + h)] = dva[h]
            ln = _lanes(dab.shape)
            dab = dab + jnp.where(ln == h, da_col[h], 0.0) + jnp.where(ln == GDN_H + h, dbt[h], 0.0)
            l1 = _lanes(gal.shape)
            gal = gal + jnp.where(l1 == h, dal[h], 0.0)
            gdt = gdt + jnp.where(l1 == h, ddt[h], 0.0)
        dab_ref[...] = dab.astype(dab_ref.dtype)
        gal_ref[...] += gal
        gdt_ref[...] += gdt

    vec = pl.BlockSpec((1, 128), lambda i: (0, 0))
    hsp = pl.BlockSpec((GDN_C, GDN_W), lambda i: (i, 0))
    qsp = pl.BlockSpec((GDN_C, 3 * GDN_W), lambda i: (i, 0))
    return pl.pallas_call(
        body, name=name, grid=(N,),
        in_specs=[qsp, pl.BlockSpec((GDN_C, 128), lambda i: (i, PAB // 128)), vec, vec] + [hsp] * 6
        + [pl.BlockSpec((8, 128), lambda i: (i, 0))],
        out_specs=[qsp, pl.BlockSpec((GDN_C, 128), lambda i: (i, 0)), vec, vec],
        out_shape=[jax.ShapeDtypeStruct((S, 3 * GDN_W), F32), jax.ShapeDtypeStruct((S, 128), BF16),
                   jax.ShapeDtypeStruct((1, 128), F32), jax.ShapeDtypeStruct((1, 128), F32)],
        compiler_params=_cp("arbitrary"),
    )(qkv, proj, alog, dtb, tinv, dw, du, dqd, dkd, dat, dgl)


@jax.custom_vjp
def _expm1(x):
    u = jnp.exp(x)
    lu = jnp.log(u)
    small = (u - 1.0) * x / jnp.where(u == 1.0, 1.0, lu)
    small = jnp.where(u == 1.0, x, small)
    return jnp.where(jnp.abs(x) < 0.5, small, u - 1.0)


def _expm1_fwd(x):
    return _expm1(x), jnp.exp(x)


def _expm1_bwd(ex, g):
    return (g * ex,)


_expm1.defvjp(_expm1_fwd, _expm1_bwd)


def _lru_gates(xc, wa, ba, wx, bx, lam, first):
    r = jax.nn.sigmoid(_mm(xc, wa) + ba)
    i = jax.nn.sigmoid(_mm(xc, wx) + bx)
    log_a = -LRU_C * r * jax.nn.softplus(-lam)
    mult = jnp.sqrt(-_expm1(2.0 * log_a))
    mult = jnp.where(first, 1.0, mult)
    return jnp.exp(log_a), mult * i * xc


def _scan_fwd(a, b):
    T = a.shape[0]
    rows = _rows(a.shape)
    s = 1
    while s < T:
        ok = rows >= s
        b = a * jnp.where(ok, _down(b, s), 0.0) + b
        a = a * jnp.where(ok, _down(a, s), 1.0)
        s *= 2
    return a, b


def _scan_rev(a, b):
    T = a.shape[0]
    rows = _rows(a.shape)
    s = 1
    while s < T:
        ok = rows + s < T
        b = a * jnp.where(ok, _up(b, s), 0.0) + b
        a = a * jnp.where(ok, _up(a, s), 1.0)
        s *= 2
    return b


def _bsl(j):
    return slice(j * LRU_BD, (j + 1) * LRU_BD)


def _lru_tile(S):
    return _row_tile(S, 256)


def _lru_fwd(proj, conv_w, conv_b, wa, ba, wx, bx, lam, name):
    S = proj.shape[0]
    T = _lru_tile(S)
    taps = conv_w.shape[0]
    r = T // HALO

    def body(x_ref, xp_ref, cw_ref, cb_ref, wa_ref, ba_ref, wx_ref, bx_ref, lam_ref, h_ref, carry_ref):
        i = pl.program_id(0)

        @pl.when(i == 0)
        def _():
            carry_ref[...] = jnp.zeros_like(carry_ref)

        ext = jnp.concatenate([jnp.where(i > 0, xp_ref[...], 0.0), x_ref[...]], axis=0)
        xc = _conv_rows(ext, cw_ref, taps)[HALO:] + cb_ref[...]
        first = (i * T + _rows((T, LRU_BD))) == 0
        for j in range(LRU_NB):
            a, b = _lru_gates(xc[:, _bsl(j)], wa_ref[j], ba_ref[:, _bsl(j)], wx_ref[j], bx_ref[:, _bsl(j)],
                              lam_ref[:, _bsl(j)], first=first)
            pa, hb = _scan_fwd(a, b)
            h_ref[:, _bsl(j)] = pa * carry_ref[0:1, _bsl(j)] + hb
            carry_ref[0:1, _bsl(j)] = h_ref[T - 1:T, _bsl(j)]

    vec = pl.BlockSpec((1, LRU_W), lambda i: (0, 0))
    wsp = pl.BlockSpec((LRU_NB, LRU_BD, LRU_BD), lambda i: (0, 0, 0))
    return pl.pallas_call(
        body, name=name, grid=(S // T,),
        in_specs=[pl.BlockSpec((T, LRU_W), lambda i: (i, PXR // LRU_W)),
                  pl.BlockSpec((HALO, LRU_W), lambda i: (jnp.maximum(i * r - 1, 0), PXR // LRU_W)),
                  pl.BlockSpec((taps, LRU_W), lambda i: (0, 0)), vec, wsp, vec, wsp, vec, vec],
        out_specs=pl.BlockSpec((T, LRU_W), lambda i: (i, 0)), out_shape=jax.ShapeDtypeStruct((S, LRU_W), F32),
        scratch_shapes=[pltpu.VMEM((8, LRU_W), F32)],
        compiler_params=_cp("arbitrary"),
    )(proj, proj, conv_w, conv_b.reshape(1, LRU_W), wa, ba.reshape(1, LRU_W), wx, bx.reshape(1, LRU_W), lam.reshape(1, LRU_W))


def _lru_bwd(proj, hl, dmixed, conv_w, conv_b, wa, ba, wx, bx, lam, name):
    S = proj.shape[0]
    T = _lru_tile(S)
    n = S // T
    taps = conv_w.shape[0]
    r = T // HALO
    mb = 768 // LRU_W

    def body(x_ref, xp_ref, g_ref, h_ref, hp_ref, dy_ref, cw_ref, cb_ref, wa_ref, ba_ref, wx_ref, bx_ref, lam_ref,
             dx_ref, dg_ref, gcw_ref, gcb_ref, gwa_ref, gba_ref, gwx_ref, gbx_ref, glam_ref, carry_ref, dxc_ref, nxt_ref):
        s = pl.program_id(0)
        i = n - 1 - s

        @pl.when(s == 0)
        def _():
            carry_ref[...] = jnp.zeros_like(carry_ref)
            nxt_ref[...] = jnp.zeros_like(nxt_ref)
            for ref in (gcw_ref, gcb_ref, gwa_ref, gba_ref, gwx_ref, gbx_ref, glam_ref):
                ref[...] = jnp.zeros_like(ref)

        ext = jnp.concatenate([jnp.where(i > 0, xp_ref[...], 0.0), x_ref[...]], axis=0)
        xc = _conv_rows(ext, cw_ref, taps)[HALO:] + cb_ref[...]
        rows = _rows((T, LRU_BD))
        first = (i * T + rows) == 0
        h_before = jnp.where(i > 0, hp_ref[HALO - 1:HALO, :], 0.0)
        for j in range(LRU_NB):
            sl = _bsl(j)
            (a, _), vjp_g = jax.vjp(functools.partial(_lru_gates, first=first), xc[:, sl], wa_ref[j], ba_ref[:, sl],
                                    wx_ref[j], bx_ref[:, sl], lam_ref[:, sl])
            gelu_g, vjp_a = jax.vjp(jax.nn.gelu, g_ref[:, sl])
            h = h_ref[:, sl]
            dy = dy_ref[:, sl]
            dg_ref[:, sl] = vjp_a(dy * h)[0].astype(dg_ref.dtype)
            b_rev = dy * gelu_g + jnp.where(rows == T - 1, carry_ref[0:1, sl], 0.0)
            a_rev = jnp.where(rows == T - 1, 0.0, _up(a, 1))
            dh = _scan_rev(a_rev, b_rev)
            carry_ref[:, sl] = (a * dh)[:HALO]
            h_prev = jnp.where(rows == 0, h_before[:, sl], _down(h, 1))
            dxc, dwa, dba, dwx, dbx, dlam = vjp_g((dh * h_prev, dh))
            dxc_ref[:, sl] = dxc
            gwa_ref[j] += dwa
            gwx_ref[j] += dwx
            gba_ref[:, sl] += dba
            gbx_ref[:, sl] += dbx
            glam_ref[:, sl] += dlam
        dxc = dxc_ref[...]
        d_ext = jnp.concatenate([dxc, nxt_ref[...]], axis=0)
        dx_ref[...] = _conv_t_rows(d_ext, cw_ref, taps)[:T].astype(dx_ref.dtype)
        nxt_ref[...] = dxc[:HALO]
        gcb_ref[...] += jnp.sum(dxc, axis=0, keepdims=True)
        for k in range(taps):
            gcw_ref[taps - 1 - k:taps - k, :] += jnp.sum(dxc * _down(ext, k)[HALO:], axis=0, keepdims=True)

    vec = pl.BlockSpec((1, LRU_W), lambda s: (0, 0))
    wsp = pl.BlockSpec((LRU_NB, LRU_BD, LRU_BD), lambda s: (0, 0, 0))
    cwsp = pl.BlockSpec((taps, LRU_W), lambda s: (0, 0))

    def main(cb):
        return pl.BlockSpec((T, LRU_W), lambda s: (n - 1 - s, cb))

    def prev(cb):
        return pl.BlockSpec((HALO, LRU_W), lambda s: (jnp.maximum((n - 1 - s) * r - 1, 0), cb))

    vshape = jax.ShapeDtypeStruct((1, LRU_W), F32)
    wshape = jax.ShapeDtypeStruct((LRU_NB, LRU_BD, LRU_BD), F32)
    return pl.pallas_call(
        body, name=name, grid=(n,),
        in_specs=[main(PXR // LRU_W), prev(PXR // LRU_W), main(PGR // LRU_W), main(0), prev(0), main(mb),
                  cwsp, vec, wsp, vec, wsp, vec, vec],
        out_specs=[main(0), main(0), cwsp, vec, wsp, vec, wsp, vec, vec],
        out_shape=[jax.ShapeDtypeStruct((S, LRU_W), BF16), jax.ShapeDtypeStruct((S, LRU_W), BF16),
                   jax.ShapeDtypeStruct((taps, LRU_W), F32), vshape, wshape, vshape, wshape, vshape, vshape],
        scratch_shapes=[pltpu.VMEM((8, LRU_W), F32), pltpu.VMEM((T, LRU_W), F32), pltpu.VMEM((HALO, LRU_W), F32)],
        compiler_params=_cp("arbitrary"),
    )(proj, proj, proj, hl, hl, dmixed, conv_w, conv_b.reshape(1, LRU_W), wa, ba.reshape(1, LRU_W), wx,
      bx.reshape(1, LRU_W), lam.reshape(1, LRU_W))


def _mix_out(o, proj, hl, y_pool, nw, name):
    S = o.shape[0]
    T = _row_tile(S)

    def body(o_ref, z_ref, h_ref, g_ref, p_ref, nw_ref, m_ref):
        for h in range(GDN_H):
            m_ref[:, _hsl(h)] = _gated_norm(o_ref[:, _hsl(h)], z_ref[:, _hsl(h)], nw_ref[...]).astype(m_ref.dtype)
        m_ref[:, GDN_W:GDN_W + LRU_W] = (h_ref[...] * jax.nn.gelu(g_ref[...])).astype(m_ref.dtype)
        m_ref[:, GDN_W + LRU_W:] = p_ref[...].astype(m_ref.dtype)

    row = pl.BlockSpec((T, GDN_W), lambda i: (i, 0))
    return pl.pallas_call(
        body, name=name, grid=(S // T,),
        in_specs=[row, pl.BlockSpec((T, GDN_W), lambda i: (i, PZ // GDN_W)), row,
                  pl.BlockSpec((T, LRU_W), lambda i: (i, PGR // LRU_W)), pl.BlockSpec((T, POOL_W), lambda i: (i, 0)),
                  pl.BlockSpec((1, GDN_DH), lambda i: (0, 0))],
        out_specs=pl.BlockSpec((T, D_MODEL), lambda i: (i, 0)), out_shape=jax.ShapeDtypeStruct((S, D_MODEL), BF16),
        compiler_params=_cp("parallel"),
    )(o, proj, hl, proj, y_pool, nw)


def _as2d(a):
    return a.reshape(-1, a.shape[-1])


def _ew_rows(rows, cols):
    t = rows
    while t * cols * 4 > (2 << 20) and t % 16 == 0:
        t //= 2
    return t


def _rs_rows(rows, cols, budget=2 << 20):
    t = rows
    while t * cols * 4 > budget and t % 32 == 0:
        t //= 2
    return t


def _adamw(w, g, m, v, name):
    shape = w.shape
    w2, g2, m2, v2 = _as2d(w), _as2d(g), _as2d(m), _as2d(v)
    rows, cols = w2.shape
    t = _ew_rows(rows, cols)

    def body(w_ref, g_ref, m_ref, v_ref, d_ref, nm_ref, nv_ref):
        gr = g_ref[...]
        nm = ADAM_B1 * m_ref[...] + (1.0 - ADAM_B1) * gr
        nv = ADAM_B2 * v_ref[...] + (1.0 - ADAM_B2) * (gr * gr)
        m_hat = nm / (1.0 - ADAM_B1 ** ADAM_STEP)
        v_hat = nv / (1.0 - ADAM_B2 ** ADAM_STEP)
        d_ref[...] = -ADAM_LR * (m_hat / (jnp.sqrt(v_hat) + ADAM_EPS) + ADAM_WD * w_ref[...])
        nm_ref[...] = nm
        nv_ref[...] = nv

    sp = pl.BlockSpec((t, cols), lambda i: (i, 0))
    sh = jax.ShapeDtypeStruct((rows, cols), F32)
    d, nm, nv = pl.pallas_call(body, name=name, grid=(rows // t,), in_specs=[sp] * 4, out_specs=[sp] * 3,
                               out_shape=[sh] * 3, compiler_params=_cp("parallel"))(w2, g2, m2, v2)
    return d.reshape(shape), nm.reshape(shape), nv.reshape(shape)


def _place():
    return lax.axis_index("x"), lax.axis_index("y"), lax.axis_index("c")


def _gather_weights(arrs, name):
    n = len(arrs)

    def body(*refs):
        outs = refs[n:2 * n]
        send, recv = refs[2 * n:]
        x, y, c = _place()
        s_me, s_x, s_y, s_d = 2 * x + y, 2 * (1 - x) + y, 2 * x + (1 - y), 2 * (1 - x) + (1 - y)
        xpeer, ypeer, sib = (1 - x, y, c), (x, 1 - y, c), (x, y, 1 - c)

        def rc(k, t, src, dst, to):
            return pltpu.make_async_remote_copy(src_ref=src, dst_ref=dst, send_sem=send.at[k, t], recv_sem=recv.at[k, t],
                                                device_id=to, device_id_type=MESH)

        def piece(k, s, top):
            rq = outs[k].shape[2] // 2
            return outs[k].at[s, c, pl.ds(0 if top else rq, rq)]

        sent = []

        def start(k, t, ref, to):
            cp = rc(k, t, ref, ref, to)
            cp.start()
            sent.append(cp)

        for k in range(n):
            start(k, 0, outs[k].at[s_me, c], xpeer)
            start(k, 1, outs[k].at[s_me, c], ypeer)
        for k in range(n):
            got = outs[k].at[s_x, c]
            rc(k, 0, got, got, xpeer).wait_recv()
            start(k, 2, piece(k, s_x, True), ypeer)
            start(k, 3, got, sib)
        for k in range(n):
            got = outs[k].at[s_y, c]
            rc(k, 1, got, got, ypeer).wait_recv()
            start(k, 6, piece(k, s_y, False), xpeer)
            start(k, 4, got, sib)
        for k in range(n):
            top, bottom = piece(k, s_d, True), piece(k, s_d, False)
            rc(k, 2, top, top, ypeer).wait_recv()
            rc(k, 6, bottom, bottom, xpeer).wait_recv()
            start(k, 5, outs[k].at[s_d, c], sib)
        for k in range(n):
            for t, s in ((3, s_x), (4, s_y), (5, s_d)):
                got = outs[k].at[s, 1 - c]
                rc(k, t, got, got, sib).wait_recv()
        for cp in sent:
            cp.wait_send()

    return pl.pallas_call(
        body, name=name, in_specs=[HBM] * n, out_specs=[HBM] * n,
        out_shape=[jax.ShapeDtypeStruct(a.shape, a.dtype) for a in arrs],
        input_output_aliases={k: k for k in range(n)},
        scratch_shapes=[pltpu.SemaphoreType.DMA((n, 7)), pltpu.SemaphoreType.DMA((n, 7))],
    )(*arrs)


def _share_halves(arrs, name):
    n = len(arrs)

    def body(*refs):
        outs = refs[n:2 * n]
        send, recv = refs[2 * n:]
        x, y, c = _place()
        cps = []
        for k in range(n):
            mine = outs[k].at[:, c]
            cp = pltpu.make_async_remote_copy(src_ref=mine, dst_ref=mine, send_sem=send.at[k], recv_sem=recv.at[k],
                                              device_id=(x, y, 1 - c), device_id_type=MESH)
            cp.start()
            cps.append(cp)
        for k in range(n):
            got = outs[k].at[:, 1 - c]
            pltpu.make_async_remote_copy(src_ref=got, dst_ref=got, send_sem=send.at[k], recv_sem=recv.at[k],
                                         device_id=(x, y, 1 - c), device_id_type=MESH).wait_recv()
        for cp in cps:
            cp.wait_send()

    return pl.pallas_call(
        body, name=name, in_specs=[HBM] * n, out_specs=[HBM] * n,
        out_shape=[jax.ShapeDtypeStruct(a.shape, a.dtype) for a in arrs],
        input_output_aliases={k: k for k in range(n)},
        scratch_shapes=[pltpu.SemaphoreType.DMA((n,)), pltpu.SemaphoreType.DMA((n,))],
    )(*arrs)


_REL = tuple((dx, dy, dc) for dx in (0, 1) for dy in (0, 1) for dc in (0, 1))[1:]
SEM = pl.BlockSpec(memory_space=pltpu.SEMAPHORE)
DATAFLOW = pltpu.SideEffectType.DATAFLOW_SIDE_EFFECTING


def _flip(v, d):
    return 1 - v if d else v


def _rs_direct_copies(srcs, land, send, recv):
    x, y, c = _place()
    cps = []
    for k in range(len(srcs)):
        for r, (dx, dy, dc) in enumerate(_REL):
            px, py, pc = _flip(x, dx), _flip(y, dy), _flip(c, dc)
            cps.append(pltpu.make_async_remote_copy(
                src_ref=srcs[k].at[2 * px + py, pc], dst_ref=land[k].at[r], send_sem=send.at[k * len(_REL) + r],
                recv_sem=recv.at[k * len(_REL) + r], device_id=(px, py, pc), device_id_type=MESH))
    return cps


def _rs_direct_start(grads, thru, name):
    n = len(grads)
    lands = [pltpu.with_memory_space_constraint(lax.empty((len(_REL),) + g.shape[2:], g.dtype), pltpu.HBM) for g in grads]

    def body(*refs):
        for cp in _rs_direct_copies(refs[:n], refs[n + 1:2 * n + 1], refs[2 * n + 1], refs[2 * n + 2]):
            cp.start()

    sems = pltpu.SemaphoreType.DMA((n * len(_REL),))
    keep = [pltpu.HBM(a.shape, a.dtype) for a in (*grads, thru, *lands)]
    out = pl.pallas_call(
        body, name=name, in_specs=[HBM] * (2 * n + 1), out_specs=(SEM, SEM) + (HBM,) * (2 * n + 1),
        out_shape=(sems, sems, *keep), input_output_aliases={i: 2 + i for i in range(2 * n + 1)},
        compiler_params=pltpu.CompilerParams(has_side_effects=DATAFLOW),
    )(*[pltpu.with_memory_space_constraint(a, pltpu.HBM) for a in (*grads, thru)], *lands)
    return out[0], out[1], out[2:2 + n], out[2 + n], out[3 + n:]


def _rs_direct_wait(send, recv, grads, lands, after, name):
    n = len(grads)
    after = list(after) if isinstance(after, (list, tuple)) else [after]

    def body(*refs):
        for cp in _rs_direct_copies(refs[:n], refs[n:2 * n], refs[2 * n], refs[2 * n + 1]):
            cp.wait_send()
            cp.wait_recv()

    keep = [pltpu.HBM(a.shape, a.dtype) for a in (*grads, *lands)]
    out = pl.pallas_call(
        body, name=name, in_specs=[HBM] * (2 * n) + [SEM, SEM] + [pl.BlockSpec(memory_space=pl.ANY)] * len(after),
        out_specs=(HBM,) * (2 * n), out_shape=tuple(keep), input_output_aliases={i: i for i in range(2 * n)},
        compiler_params=pltpu.CompilerParams(has_side_effects=DATAFLOW),
    )(*grads, *lands, send, recv, *after)
    return out[:n], out[n:]


_CHIPS = ((1, 0), (0, 1), (1, 1))


def _gather_ici_copies(bufs, send, recv):
    x, y, c = _place()
    cps = []
    for k in range(len(bufs)):
        mine = bufs[k].at[2 * x + y, c]
        for j, (dx, dy) in enumerate(_CHIPS):
            cps.append(pltpu.make_async_remote_copy(
                src_ref=mine, dst_ref=mine, send_sem=send.at[k * len(_CHIPS) + j], recv_sem=recv.at[k * len(_CHIPS) + j],
                device_id=(_flip(x, dx), _flip(y, dy), c), device_id_type=MESH))
    return cps


def _gather_d2d_copies(bufs, send, recv):
    x, y, c = _place()
    cps = []
    for k in range(len(bufs)):
        for j, (dx, dy) in enumerate(_CHIPS):
            got = bufs[k].at[2 * _flip(x, dx) + _flip(y, dy), c]
            cps.append(pltpu.make_async_remote_copy(
                src_ref=got, dst_ref=got, send_sem=send.at[k * len(_CHIPS) + j], recv_sem=recv.at[k * len(_CHIPS) + j],
                device_id=(x, y, 1 - c), device_id_type=MESH))
    return cps


def _copies_start(bufs, thru, copies, name):
    n = len(bufs)

    def body(*refs):
        for cp in copies(refs[:n], refs[n + 1], refs[n + 2]):
            cp.start()

    sems = pltpu.SemaphoreType.DMA((n * len(_CHIPS),))
    out = pl.pallas_call(
        body, name=name, in_specs=[HBM] * (n + 1), out_specs=(SEM, SEM) + (HBM,) * (n + 1),
        out_shape=(sems, sems, *[pltpu.HBM(a.shape, a.dtype) for a in (*bufs, thru)]),
        input_output_aliases={i: 2 + i for i in range(n + 1)},
        compiler_params=pltpu.CompilerParams(has_side_effects=DATAFLOW),
    )(*[pltpu.with_memory_space_constraint(a, pltpu.HBM) for a in (*bufs, thru)])
    return out[0], out[1], out[2:2 + n], out[2 + n]


def _copies_wait(send, recv, bufs, after, copies, name):
    n = len(bufs)

    def body(*refs):
        for cp in copies(refs[:n], refs[n], refs[n + 1]):
            cp.wait_send()
            cp.wait_recv()

    return pl.pallas_call(
        body, name=name, in_specs=[HBM] * n + [SEM, SEM, pl.BlockSpec(memory_space=pl.ANY)], out_specs=(HBM,) * n,
        out_shape=tuple(pltpu.HBM(a.shape, a.dtype) for a in bufs), input_output_aliases={i: i for i in range(n)},
        compiler_params=pltpu.CompilerParams(has_side_effects=DATAFLOW),
    )(*bufs, send, recv, after)


def _rs_direct_sum(grad, land, layer, name, into=None):
    _, _, rows, cols = grad.shape
    t = _rs_rows(rows, cols, 6 << 20)
    npieces = len(_REL) + 1

    def body(g_ref, l_ref, *rest):
        o_ref, acc_ref = rest[-2], rest[-1]
        j = pl.program_id(1)

        @pl.when(j == 0)
        def _():
            acc_ref[...] = g_ref[...].astype(F32)

        @pl.when(j > 0)
        def _():
            acc_ref[...] += l_ref[...].astype(F32)

        @pl.when(j == npieces - 1)
        def _():
            o_ref[...] = acc_ref[...]

    def mine(i, j):
        x, y, c = _place()
        return (2 * x + y, c, i, 0)

    in_specs = [pl.BlockSpec((None, None, t, cols), mine),
                pl.BlockSpec((None, t, cols), lambda i, j: (jnp.maximum(j - 1, 0), i, 0))]
    args = [grad, land]
    if into is not None:
        in_specs.append(pl.BlockSpec(memory_space=pl.ANY))
        args.append(into)
    return pl.pallas_call(
        body, name=name, grid=(rows // t, npieces), in_specs=in_specs,
        out_specs=pl.BlockSpec((None, None, t, cols), lambda i, j: (layer, lax.axis_index("c"), i, 0)),
        scratch_shapes=[pltpu.VMEM((t, cols), F32)],
        out_shape=jax.ShapeDtypeStruct((2, 2, rows, cols), F32), input_output_aliases={} if into is None else {2: 0},
        compiler_params=_cp("parallel", "arbitrary"),
    )(*args)


def _ar_copies(buf, land, send, recv):
    x, y, c = _place()
    return [pltpu.make_async_remote_copy(src_ref=buf, dst_ref=land.at[r], send_sem=send.at[r], recv_sem=recv.at[r],
                                         device_id=(_flip(x, dx), _flip(y, dy), _flip(c, dc)), device_id_type=MESH)
            for r, (dx, dy, dc) in enumerate(_REL)]


def _ar_start(buf, name):
    land = pltpu.with_memory_space_constraint(lax.empty((len(_REL),) + buf.shape, buf.dtype), pltpu.HBM)

    def body(buf_ref, land_ref, send, recv, *_):
        for cp in _ar_copies(buf_ref, land_ref, send, recv):
            cp.start()

    sems = pltpu.SemaphoreType.DMA((len(_REL),))
    return pl.pallas_call(
        body, name=name, in_specs=[HBM, HBM], out_specs=(SEM, SEM, HBM, HBM),
        out_shape=(sems, sems, pltpu.HBM(buf.shape, buf.dtype), pltpu.HBM(land.shape, land.dtype)),
        input_output_aliases={0: 2, 1: 3}, compiler_params=pltpu.CompilerParams(has_side_effects=DATAFLOW),
    )(pltpu.with_memory_space_constraint(buf, pltpu.HBM), land)


def _ar_wait(send, recv, buf, land, after, name):
    def body(buf_ref, land_ref, send_ref, recv_ref, *_):
        for cp in _ar_copies(buf_ref, land_ref, send_ref, recv_ref):
            cp.wait_send()
            cp.wait_recv()

    return pl.pallas_call(
        body, name=name, in_specs=[HBM, HBM, SEM, SEM, pl.BlockSpec(memory_space=pl.ANY)], out_specs=(HBM, HBM),
        out_shape=(pltpu.HBM(buf.shape, buf.dtype), pltpu.HBM(land.shape, land.dtype)), input_output_aliases={0: 0, 1: 1},
        compiler_params=pltpu.CompilerParams(has_side_effects=DATAFLOW),
    )(buf, land, send, recv, after)


def _ar_sum(buf, land, name):
    rows, cols = buf.shape
    t = _rs_rows(rows, cols)

    def slot(i, j):
        x, y, c = _place()
        xd, yd, cd = j // 4, (j // 2) % 2, j % 2
        rel = 4 * (x + xd - 2 * x * xd) + 2 * (y + yd - 2 * y * yd) + (c + cd - 2 * c * cd)
        return (jnp.maximum(rel - 1, 0), i, 0)

    def body(b_ref, l_ref, o_ref, acc_ref):
        j = pl.program_id(1)
        x, y, c = _place()
        val = jnp.where(j == 4 * x + 2 * y + c, b_ref[...], l_ref[...])

        @pl.when(j == 0)
        def _():
            acc_ref[...] = val

        @pl.when(j > 0)
        def _():
            acc_ref[...] += val

        @pl.when(j == len(_REL))
        def _():
            o_ref[...] = acc_ref[...]

    sp = pl.BlockSpec((t, cols), lambda i, j: (i, 0))
    return pl.pallas_call(
        body, name=name, grid=(rows // t, len(_REL) + 1), in_specs=[sp, pl.BlockSpec((None, t, cols), slot)], out_specs=sp,
        out_shape=jax.ShapeDtypeStruct((rows, cols), F32), scratch_shapes=[pltpu.VMEM((t, cols), F32)],
        compiler_params=_cp("parallel", "arbitrary"),
    )(buf, land)


def _pad128(v):
    return jnp.zeros((1, 128), F32).at[0, :v.shape[0]].set(v)


def _layer_fwd(l, x, p, hooks=None):
    hooks = hooks or {}
    h1 = _rms_fwd(x, p["norm1_w"], f"rms1_{l}")
    proj = _matmul(h1, p["w_in"], "nn", name=f"mm_in_{l}", tn=768)
    y_pool = _pool_fwd(proj, p["pool_w"], p["pool_b"], p["pool_scale"], f"pool_{l}")
    qkv = _gdn_conv_fwd(proj, p["gdn_conv_w"], f"gconv_{l}")
    alog, dtb = _pad128(p["gdn_a_log"]), _pad128(p["gdn_dt_bias"])
    gw, gu, gqd, gkd, gat, tinv, gc = _gdn_prep(qkv, proj, alog, dtb, f"gprep_{l}")
    o, states = _gdn_scan(gw, gu, gqd, gkd, gat, gc, f"gscan_{l}")
    if "mid" in hooks:
        o = hooks["mid"](o)
    hl = _lru_fwd(proj, p["lru_conv_w"], p["lru_conv_b"], p["lru_wa"], p["lru_ba"], p["lru_wx"], p["lru_bx"],
                  p["lru_lambda"], f"lru_{l}")
    mixed = _mix_out(o, proj, hl, y_pool, p["gdn_norm_w"].reshape(1, GDN_DH), f"mix_{l}")
    x2 = _matmul(mixed, p["w_out"], "nn", name=f"mm_out_{l}", res=x)
    h2 = _rms_fwd(x2, p["norm2_w"], f"rms2_{l}")
    if "ffn" in hooks:
        h2 = hooks["ffn"](h2)
    up = _matmul(h2, p["ffn_up"], "nn", name=f"mm_up_{l}", b_split=True)
    act = _ffn_act_fwd(up, p["ffn_conv_w"], f"ffn_{l}")
    if "down" in hooks:
        act = hooks["down"](act)
    x3 = _matmul(act, p["ffn_down"], "nn", name=f"mm_down_{l}", res=x2)
    saved = dict(x=x, h1=h1, proj=proj, qkv=qkv, gdn=(gw, gu, gqd, gkd, gat, gc), tinv=tinv, states=states, o=o, hl=hl, mixed=mixed,
                 x2=x2, h2=h2, up=up, act=act, alog=alog, dtb=dtb)
    return x3, saved


def _layer_bwd(l, dx3, p, s, after_ffn=None, after_mix=None):
    g = {}
    dact = _matmul(dx3, p["ffn_down"], "nt", name=f"mm_ddown_{l}")
    g["ffn_down"] = _matmul(s["act"], dx3, "tn", name=f"mm_gdown_{l}", out_dtype=BF16)
    dup, g["ffn_conv_w"] = _ffn_act_bwd(s["up"], dact, p["ffn_conv_w"], f"ffn_b_{l}")
    dh2 = _matmul(dup, p["ffn_up"], "nt", name=f"mm_dup_{l}", b_split=True, tk=3072)
    g["ffn_up"] = _matmul(s["h2"], dup, "tn", name=f"mm_gup_{l}", b_split=True, o_split=4, tk=4096, out_dtype=BF16)
    dx2, g["norm2_w"] = _rms_bwd(s["x2"], p["norm2_w"], dh2, dx3, f"rms2_b_{l}")
    g["w_out"] = _matmul(s["mixed"], dx2, "tn", name=f"mm_gout_{l}", out_dtype=BF16)
    if after_ffn is not None:
        dx2 = after_ffn(dx2, g)
    dmixed = _matmul(dx2, p["w_out"], "nt", name=f"mm_dout_{l}")
    proj = s["proj"]
    du_pool, g["pool_w"], g["pool_b"], g["pool_scale"] = _pool_bwd(proj, dmixed, p["pool_w"], p["pool_b"], p["pool_scale"], f"pool_b_{l}")
    gw, gu, gqd, gkd, gat, gc = s["gdn"]
    dw, du, dqd, dkd, dat, dgl, dz, g["gdn_norm_w"] = _gdn_scan_bwd(
        gw, gu, gqd, gkd, gat, gc, s["states"], s["o"], proj, dmixed, p["gdn_norm_w"].reshape(1, GDN_DH), f"gscan_b_{l}")
    dqkv, dab, gal, gdt = _gdn_prep_bwd(s["qkv"], proj, s["alog"], s["dtb"], s["tinv"], dw, du, dqd, dkd, dat, dgl, f"gprep_b_{l}")
    g["gdn_a_log"], g["gdn_dt_bias"] = gal[0, :GDN_H], gdt[0, :GDN_H]
    dpre, g["gdn_conv_w"] = _gdn_conv_bwd(proj, dqkv, p["gdn_conv_w"], f"gconv_b_{l}")
    (dxr, dgr, g["lru_conv_w"], g["lru_conv_b"], g["lru_wa"], g["lru_ba"], g["lru_wx"], g["lru_bx"], g["lru_lambda"]) = _lru_bwd(
        proj, s["hl"], dmixed, p["lru_conv_w"], p["lru_conv_b"], p["lru_wa"], p["lru_ba"], p["lru_wx"], p["lru_bx"],
        p["lru_lambda"], f"lru_b_{l}")
    S = proj.shape[0]
    dproj = jnp.concatenate([dpre, dz, dxr, dgr, du_pool, dab, jnp.zeros((S, PCOLS - PAB - 128), BF16)], axis=1)
    g["w_in"] = _matmul(s["h1"], dproj, "tn", name=f"mm_gin_{l}", tn=768, tk=4096, out_dtype=BF16)
    if after_mix is not None:
        dproj = after_mix(dproj, g)
    dh1 = _matmul(dproj, p["w_in"], "nt", name=f"mm_din_{l}", tk=1792)
    dx, g["norm1_w"] = _rms_bwd(s["x"], p["norm1_w"], dh1, dx2, f"rms1_b_{l}")
    return dx, g


_IN_PERM = ((512, 3584), (3596, 5132), (0, 512), (3584, 3596))


def _w_in_to_proj(w):
    parts = [w[:, a:b] for a, b in _IN_PERM]
    return jnp.concatenate(parts + [jnp.zeros((w.shape[0], PCOLS - IN_COLS), w.dtype)], axis=1)


def _proj_to_w_in(g):
    return jnp.concatenate([g[:, PPOOL:PPOOL + 512], g[:, 0:3072], g[:, PAB:PAB + 12], g[:, 3072:PPOOL]], axis=1)


def _rows_to_mixed(w):
    return jnp.concatenate([w[512:], w[:512]], axis=0)


def _mixed_to_rows(g):
    return jnp.concatenate([g[1536:], g[:1536]], axis=0)


SMALL_SHARDED = ("gdn_conv_w", "lru_conv_w", "ffn_conv_w")
BIG = ("w_in", "w_out", "ffn_up", "ffn_down")
SMALL_REPLICATED = ("norm1_w", "pool_w", "pool_b", "pool_scale", "gdn_a_log", "gdn_dt_bias", "gdn_norm_w", "lru_conv_b",
                    "lru_wa", "lru_ba", "lru_wx", "lru_bx", "lru_lambda", "norm2_w")
WEIGHTS = ("norm1_w", "w_in", "pool_w", "pool_b", "pool_scale", "gdn_conv_w", "gdn_a_log", "gdn_dt_bias", "gdn_norm_w",
           "lru_conv_w", "lru_conv_b", "lru_wa", "lru_ba", "lru_wx", "lru_bx", "lru_lambda", "w_out", "norm2_w", "ffn_up",
           "ffn_conv_w", "ffn_down", "final_norm_w")
FLAT_COLS = 1024


def _pack(arrs):
    flat = jnp.concatenate([a.reshape(-1) for a in arrs])
    rows = -(-flat.shape[0] // (8 * FLAT_COLS)) * 8
    return jnp.pad(flat, (0, rows * FLAT_COLS - flat.shape[0])).reshape(rows, FLAT_COLS)


def _unpack(buf, like):
    flat = buf.reshape(-1)
    out, off = [], 0
    for a in like:
        size = 1
        for d in a.shape:
            size *= d
        out.append(flat[off:off + size].reshape(a.shape))
        off += size
    return out


def kernel(x, norm1_w, w_in, pool_w, pool_b, pool_scale, gdn_conv_w, gdn_a_log, gdn_dt_bias, gdn_norm_w, lru_conv_w, lru_conv_b, lru_wa, lru_ba, lru_wx, lru_bx, lru_lambda, w_out, norm2_w, ffn_up, ffn_conv_w, ffn_down, final_norm_w, loss_target, m_norm1_w, m_w_in, m_pool_w, m_pool_b, m_pool_scale, m_gdn_conv_w, m_gdn_a_log, m_gdn_dt_bias, m_gdn_norm_w, m_lru_conv_w, m_lru_conv_b, m_lru_wa, m_lru_ba, m_lru_wx, m_lru_bx, m_lru_lambda, m_w_out, m_norm2_w, m_ffn_up, m_ffn_conv_w, m_ffn_down, m_final_norm_w, v_norm1_w, v_w_in, v_pool_w, v_pool_b, v_pool_scale, v_gdn_conv_w, v_gdn_a_log, v_gdn_dt_bias, v_gdn_norm_w, v_lru_conv_w, v_lru_conv_b, v_lru_wa, v_lru_ba, v_lru_wx, v_lru_bx, v_lru_lambda, v_w_out, v_norm2_w, v_ffn_up, v_ffn_conv_w, v_ffn_down, v_final_norm_w):
    W = dict(norm1_w=norm1_w, w_in=w_in, pool_w=pool_w, pool_b=pool_b, pool_scale=pool_scale, gdn_conv_w=gdn_conv_w,
             gdn_a_log=gdn_a_log, gdn_dt_bias=gdn_dt_bias, gdn_norm_w=gdn_norm_w, lru_conv_w=lru_conv_w, lru_conv_b=lru_conv_b,
             lru_wa=lru_wa, lru_ba=lru_ba, lru_wx=lru_wx, lru_bx=lru_bx, lru_lambda=lru_lambda, w_out=w_out, norm2_w=norm2_w,
             ffn_up=ffn_up, ffn_conv_w=ffn_conv_w, ffn_down=ffn_down, final_norm_w=final_norm_w)
    M = dict(norm1_w=m_norm1_w, w_in=m_w_in, pool_w=m_pool_w, pool_b=m_pool_b, pool_scale=m_pool_scale, gdn_conv_w=m_gdn_conv_w,
             gdn_a_log=m_gdn_a_log, gdn_dt_bias=m_gdn_dt_bias, gdn_norm_w=m_gdn_norm_w, lru_conv_w=m_lru_conv_w,
             lru_conv_b=m_lru_conv_b, lru_wa=m_lru_wa, lru_ba=m_lru_ba, lru_wx=m_lru_wx, lru_bx=m_lru_bx, lru_lambda=m_lru_lambda,
             w_out=m_w_out, norm2_w=m_norm2_w, ffn_up=m_ffn_up, ffn_conv_w=m_ffn_conv_w, ffn_down=m_ffn_down,
             final_norm_w=m_final_norm_w)
    V = dict(norm1_w=v_norm1_w, w_in=v_w_in, pool_w=v_pool_w, pool_b=v_pool_b, pool_scale=v_pool_scale, gdn_conv_w=v_gdn_conv_w,
             gdn_a_log=v_gdn_a_log, gdn_dt_bias=v_gdn_dt_bias, gdn_norm_w=v_gdn_norm_w, lru_conv_w=v_lru_conv_w,
             lru_conv_b=v_lru_conv_b, lru_wa=v_lru_wa, lru_ba=v_lru_ba, lru_wx=v_lru_wx, lru_bx=v_lru_bx, lru_lambda=v_lru_lambda,
             w_out=v_w_out, norm2_w=v_norm2_w, ffn_up=v_ffn_up, ffn_conv_w=v_ffn_conv_w, ffn_down=v_ffn_down,
             final_norm_w=v_final_norm_w)
    S = x.shape[1]
    xs = x.reshape(S, D_MODEL)
    tgt = loss_target.reshape(S, D_MODEL)
    mx, my, mc = _place()
    shard = 2 * mx + my

    small_sh = jnp.concatenate([W[k].reshape(N_LAYERS, -1) for k in SMALL_SHARDED], axis=1)
    n_small = small_sh.shape[1]
    pad = -n_small % 1024
    small_sh = jnp.pad(small_sh, ((0, 0), (0, pad))).reshape(N_LAYERS, -1, 1024)

    def own_slots(l):
        out = []
        for w in [W[k][l].astype(BF16) for k in BIG] + [small_sh[l]]:
            buf = lax.dynamic_update_slice(lax.empty((4,) + w.shape, w.dtype), w[None], (shard,) + (0,) * w.ndim)
            out.append(buf.reshape(4, 2, w.shape[0] // 2, w.shape[1]))
        return out

    def whole(g):
        return g.reshape(4, 2 * g.shape[2], g.shape[3])

    def mixer_params(l, g_in, g_out, g_small):
        p = {k: W[k][l] for k in SMALL_REPLICATED}
        g_in = whole(g_in)
        p["w_in"] = _w_in_to_proj(jnp.transpose(g_in, (1, 0, 2)).reshape(g_in.shape[1], IN_COLS))
        p["w_out"] = _rows_to_mixed(whole(g_out).reshape(D_MODEL, D_MODEL))
        g_small = whole(g_small).reshape(4, -1)[:, :n_small]
        off = 0
        for k in SMALL_SHARDED:
            taps, width = W[k].shape[1], W[k].shape[2]
            piece = g_small[:, off:off + taps * width].reshape(4, taps, width)
            p[k] = jnp.transpose(piece, (1, 0, 2)).reshape(taps, 4 * width)
            off += taps * width
        return p

    def ffn_params(g_up, g_down):
        return dict(ffn_up=whole(g_up), ffn_down=whole(g_down).reshape(D_FF, D_MODEL))

    layers, saved = [None] * N_LAYERS, [None] * N_LAYERS
    s0 = own_slots(0)
    g_in0, g_out0, g_small0 = _gather_weights([s0[0], s0[1], s0[4]], "gather_weights")
    f_send, f_recv, ffn0, g_in0 = _copies_start(s0[2:4], g_in0, _gather_ici_copies, "gather_ffn0_ici_start")
    l_send, l_recv, bufs1, g_in0 = _copies_start(own_slots(1), g_in0, _gather_ici_copies, "gather_l1_ici_start")
    layers[0] = mixer_params(0, g_in0, g_out0, g_small0)
    stage = {}

    def mid(o):
        bufs = _copies_wait(f_send, f_recv, ffn0, o, _gather_ici_copies, "gather_ffn0_ici_wait")
        stage["ffn0"] = _copies_start(bufs, o, _gather_d2d_copies, "gather_ffn0_d2d_start")
        return stage["ffn0"][3]

    def ffn(h2):
        send, recv, bufs, _ = stage["ffn0"]
        layers[0].update(ffn_params(*_copies_wait(send, recv, bufs, h2, _gather_d2d_copies, "gather_ffn0_d2d_wait")))
        return h2

    def down(act):
        bufs = _copies_wait(l_send, l_recv, bufs1, act, _gather_ici_copies, "gather_l1_ici_wait")
        stage["l1"] = _copies_start(bufs, act, _gather_d2d_copies, "gather_l1_d2d_start")
        return stage["l1"][3]

    h, saved[0] = _layer_fwd(0, xs, layers[0], dict(mid=mid, ffn=ffn, down=down))
    send, recv, bufs, _ = stage["l1"]
    g1 = _copies_wait(send, recv, bufs, h, _gather_d2d_copies, "gather_l1_d2d_wait")
    layers[1] = {**mixer_params(1, g1[0], g1[1], g1[4]), **ffn_params(g1[2], g1[3])}
    h, saved[1] = _layer_fwd(1, h, layers[1])
    loss_part, dh, g_final = _loss_head(h, final_norm_w, tgt, "loss_head")

    def big_partials(g_layer, names=BIG):
        out = []
        for k in names:
            g = g_layer[k]
            if k == "w_in":
                g = _proj_to_w_in(g)
                g = jnp.transpose(g.reshape(g.shape[0], 4, IN_COLS // 4), (1, 0, 2))
            elif k == "w_out":
                g = _mixed_to_rows(g).reshape(4, D_MODEL // 4, D_MODEL)
            elif k == "ffn_down":
                g = g.reshape(4, D_FF // 4, D_MODEL)
            out.append(g.reshape(4, 2, g.shape[1] // 2, g.shape[2]))
        return out

    FFN, MIX = ("ffn_up", "ffn_down", "w_out"), ("w_in",)
    grads = [None] * N_LAYERS
    dh, grads[1] = _layer_bwd(1, dh, layers[1], saved[1])
    send1, recv1, part1, dh, lands1 = _rs_direct_start(big_partials(grads[1]), dh, "rs_direct_start_1")
    small_names = SMALL_REPLICATED + SMALL_SHARDED

    def small_of(l):
        return [grads[l][k].reshape(W[k].shape[1:]) if k in SMALL_REPLICATED else grads[l][k] for k in small_names]

    small_list1 = small_of(1) + [g_final.reshape(D_MODEL), loss_part[0, 0:1]]
    ar1 = _ar_start(_pack(small_list1), "ar_start_1")
    sent0 = []

    def after_ffn(dx2, g):
        send0, recv0, part0, dx2, lands0 = _rs_direct_start(big_partials(g, FFN), dx2, "rs_direct_start_0")
        sent0.extend([send0, recv0, part0, lands0])
        return dx2

    sentm = []

    def after_mix(dproj, g):
        sendm, recvm, partm, dproj, landsm = _rs_direct_start(big_partials(g, MIX), dproj, "rs_direct_start_0m")
        sentm.extend([sendm, recvm, partm, landsm])
        return dproj

    dh, grads[0] = _layer_bwd(0, dh, layers[0], saved[0], after_ffn, after_mix)
    sendm, recvm, partm, landsm = sentm

    small_list0 = small_of(0)
    ar0 = _ar_start(_pack(small_list0), "ar_start_0")

    part1, lands1 = _rs_direct_wait(send1, recv1, part1, lands1, dh, "rs_direct_wait_1")
    part0, lands0 = _rs_direct_wait(*sent0, dh, "rs_direct_wait_0")
    red = {k: _rs_direct_sum(g, ld, 1, f"rs_sum_1_{k}") for k, g, ld in zip(BIG, part1, lands1)}
    for k, g, ld in zip(FFN, part0, lands0):
        red[k] = _rs_direct_sum(g, ld, 0, f"rs_sum_0_{k}", into=red[k])
    G, DELTA, NM, NV = {}, {}, {}, {}

    def update_big(names, shared):
        for k, r in zip(names, shared):
            G[k] = r.reshape(N_LAYERS, 2 * r.shape[2], r.shape[3])
            DELTA[k], NM[k], NV[k] = _adamw(W[k], G[k], M[k], V[k], f"adam_{k}")

    update_big(FFN, _share_halves([red[k] for k in FFN], "rs_share_ffn"))
    partm, landsm = _rs_direct_wait(sendm, recvm, partm, landsm, DELTA[FFN[-1]], "rs_direct_wait_0m")
    for k, g, ld in zip(MIX, partm, landsm):
        red[k] = _rs_direct_sum(g, ld, 0, f"rs_sum_0_{k}", into=red[k])
    update_big(MIX, _share_halves([red[k] for k in MIX], "rs_share_mix"))
    grad_x = dh.reshape(x.shape)

    red1 = _unpack(_ar_sum(*_ar_wait(*ar1, G[MIX[-1]], "ar_wait_1"), "ar_sum_1"), small_list1)
    red0 = _unpack(_ar_sum(*_ar_wait(*ar0, G[MIX[-1]], "ar_wait_0"), "ar_sum_0"), small_list0)
    small_g = {k: jnp.stack([r0, r1]) for k, r0, r1 in zip(small_names, red0, red1)}
    small_g["final_norm_w"] = red1[-2]
    loss = red1[-1][0]
    for k in SMALL_SHARDED:
        width = W[k].shape[2]
        small_g[k] = lax.dynamic_slice_in_dim(small_g[k], shard * width, width, axis=2)

    small_all = small_names + ("final_norm_w",)
    dl, nm, nv = _adamw(_pack([W[k] for k in small_all]), _pack([small_g[k] for k in small_all]),
                        _pack([M[k] for k in small_all]), _pack([V[k] for k in small_all]), "adam_small")
    like = [W[k] for k in small_all]
    for k, d_, m_, v_ in zip(small_all, _unpack(dl, like), _unpack(nm, like), _unpack(nv, like)):
        G[k], DELTA[k], NM[k], NV[k] = small_g[k], d_, m_, v_

    return (loss, grad_x, *[G[k] for k in WEIGHTS], *[DELTA[k] for k in WEIGHTS], *[NM[k] for k in WEIGHTS],
            *[NV[k] for k in WEIGHTS])
```

```python
import functools

import jax
import jax.numpy as jnp
from jax import lax
from jax.experimental import pallas as pl
from jax.experimental.pallas import tpu as pltpu

F32 = jnp.float32
BF16 = jnp.bfloat16
_MXU = jnp.bfloat16

D_MODEL = 2048
N_LAYERS = 2
POOL_W = 512
POOL_G = 4
POOL_GD = 128
POOL_WINDOWS = (2, 4, 8, 16)
POOL_HALO = 16
GDN_W = 768
GDN_H = 6
GDN_DH = 128
GDN_C = 64
LRU_W = 768
LRU_NB = 6
LRU_BD = 128
LRU_C = 8.0
D_FF = 6144
EPS = 1e-6
IN_COLS = 5132
HALO = 8

PQ, PK, PV, PZ, PXR, PGR, PPOOL, PAB, PCOLS = 0, 768, 1536, 2304, 3072, 3840, 4608, 5120, 5376
CB = 768

ADAM_LR = 0.001
ADAM_B1 = 0.9
ADAM_B2 = 0.999
ADAM_EPS = 1e-08
ADAM_WD = 0.01
ADAM_STEP = 10

VMEM_LIMIT = 56 * 1024 * 1024
MESH = pl.DeviceIdType.MESH
HBM = pl.BlockSpec(memory_space=pltpu.HBM)


def _cp(*sem):
    return pltpu.CompilerParams(dimension_semantics=sem, vmem_limit_bytes=VMEM_LIMIT)


def _dg(a, b, ta, tb):
    dims = (((0 if ta else 1,), (1 if tb else 0,)), ((), ()))
    return lax.dot_general(a, b, dims, preferred_element_type=F32)


def _split2(a):
    hi = a.astype(BF16)
    lo = (a - hi.astype(F32)).astype(BF16)
    return hi, lo


def _mm_raw(a, b, ta, tb, hi):
    if _MXU == F32:
        return _dg(a, b, ta, tb)
    if not hi:
        return _dg(a.astype(_MXU), b.astype(_MXU), ta, tb)
    a1, a2 = _split2(a)
    b1, b2 = _split2(b)
    return _dg(a1, b1, ta, tb) + (_dg(a1, b2, ta, tb) + _dg(a2, b1, ta, tb))


@functools.partial(jax.custom_vjp, nondiff_argnums=(2, 3, 4))
def _mm(a, b, ta=False, tb=False, hi=False):
    return _mm_raw(a, b, ta, tb, hi)


def _mm_fwd(a, b, ta, tb, hi):
    return _mm_raw(a, b, ta, tb, hi), (a, b)


def _mm_bwd(ta, tb, hi, res, dc):
    a, b = res
    da = _mm(b, dc, tb, True, hi) if ta else _mm(dc, b, False, not tb, hi)
    db = _mm(dc, a, True, ta, hi) if tb else _mm(a, dc, not ta, False, hi)
    return da, db


_mm.defvjp(_mm_fwd, _mm_bwd)


def _mm01(m01, x):
    if _MXU == F32:
        return _dg(m01, x, False, False)
    m = m01.astype(BF16)
    x1 = x.astype(BF16)
    r = x - x1.astype(F32)
    x2 = r.astype(BF16)
    x3 = (r - x2.astype(F32)).astype(BF16)
    return _dg(m, x1, False, False) + (_dg(m, x2, False, False) + _dg(m, x3, False, False))


def _down(x, k):
    return x if k == 0 else pltpu.roll(x, k, 0)


def _up(x, k):
    return x if k == 0 else pltpu.roll(x, x.shape[0] - k, 0)


def _rows(shape):
    return lax.broadcasted_iota(jnp.int32, shape, 0)


def _lanes(shape):
    return lax.broadcasted_iota(jnp.int32, shape, 1)


def _matmul(a, b, mode, *, name, res=None, tm=1024, tn=1024, tk=2048, b_split=False, o_split=0, out_dtype=F32):
    ta, tb = mode == "tn", mode == "nt"
    a_split = a.ndim == 3
    if a_split:
        assert not ta
        M, K = a.shape[1], a.shape[0] * a.shape[2]
        tk = min(tk, a.shape[2])
    elif ta:
        K, M = a.shape
    else:
        M, K = a.shape
    if b_split:
        ns = b.shape[0]
        N = b.shape[1] if tb else ns * b.shape[2]
    else:
        N = b.shape[0] if tb else b.shape[1]
    tm, tn, tk = min(tm, M), min(tn, N), min(tk, K)
    if b_split:
        per = b.shape[2]
        if tb:
            tk = min(tk, per)
        else:
            tn = min(tn, per)
    if o_split:
        tn = min(tn, N // o_split)
    assert M % tm == 0 and N % tn == 0 and K % tk == 0, (name, M, N, K, tm, tn, tk)
    nk = K // tk
    if a_split:
        ka = a.shape[2] // tk
        a_spec = pl.BlockSpec((None, tm, tk), lambda i, j, k: (k // ka, i, k % ka))
    else:
        a_spec = pl.BlockSpec((tk, tm), lambda i, j, k: (k, i)) if ta else pl.BlockSpec((tm, tk), lambda i, j, k: (i, k))
    if not b_split:
        b_spec = pl.BlockSpec((tn, tk), lambda i, j, k: (j, k)) if tb else pl.BlockSpec((tk, tn), lambda i, j, k: (k, j))
    elif tb:
        kb = per // tk
        b_spec = pl.BlockSpec((None, tn, tk), lambda i, j, k: (k // kb, j, k % kb))
    else:
        nb = per // tn
        b_spec = pl.BlockSpec((None, tk, tn), lambda i, j, k: (j // nb, k, j % nb))
    if o_split:
        ob = (N // o_split) // tn
        out_shape = jax.ShapeDtypeStruct((o_split, M, N // o_split), out_dtype)
        o_spec = pl.BlockSpec((None, tm, tn), lambda i, j, k: (j // ob, i, j % ob))
    else:
        out_shape = jax.ShapeDtypeStruct((M, N), out_dtype)
        o_spec = pl.BlockSpec((tm, tn), lambda i, j, k: (i, j))
    in_specs = [a_spec, b_spec]
    args = [a, b]
    if res is not None:
        in_specs.append(pl.BlockSpec((tm, tn), lambda i, j, k: (i, j)))
        args.append(res)
    use_acc = nk > 1 and out_dtype != F32

    def body(*refs):
        a_ref, b_ref = refs[0], refs[1]
        o_ref = refs[2 + (res is not None)]
        acc_ref = refs[-1] if use_acc else o_ref
        p = _dg(a_ref[...].astype(_MXU), b_ref[...].astype(_MXU), ta, tb)
        first = p + refs[2][...] if res is not None else p
        if nk == 1:
            o_ref[...] = first.astype(o_ref.dtype)
        else:
            k = pl.program_id(2)

            @pl.when(k == 0)
            def _():
                acc_ref[...] = first

            @pl.when(k > 0)
            def _():
                acc_ref[...] += p

            if use_acc:
                @pl.when(k == nk - 1)
                def _():
                    o_ref[...] = acc_ref[...].astype(o_ref.dtype)

    return pl.pallas_call(
        body, name=name, grid=(M // tm, N // tn, nk), in_specs=in_specs, out_specs=o_spec, out_shape=out_shape,
        scratch_shapes=[pltpu.VMEM((tm, tn), F32)] if use_acc else [],
        compiler_params=_cp("parallel", "parallel", "arbitrary"),
    )(*args)


def _rms(x, w):
    return x * lax.rsqrt(jnp.mean(x * x, axis=-1, keepdims=True) + EPS) * w


def _row_tile(S, t=512):
    t = min(t, S)
    assert S % t == 0
    return t


def _rms_fwd(x, w, name):
    S, D = x.shape
    T = _row_tile(S)

    def body(x_ref, w_ref, o_ref):
        o_ref[...] = _rms(x_ref[...], w_ref[...]).astype(o_ref.dtype)

    return pl.pallas_call(
        body, name=name, grid=(S // T,),
        in_specs=[pl.BlockSpec((T, D), lambda i: (i, 0)), pl.BlockSpec((1, D), lambda i: (0, 0))],
        out_specs=pl.BlockSpec((T, D), lambda i: (i, 0)), out_shape=jax.ShapeDtypeStruct((S, D), BF16),
        compiler_params=_cp("parallel"),
    )(x, w.reshape(1, D))


def _rms_bwd(x, w, dh, dres, name):
    S, D = x.shape
    T = _row_tile(S)

    def body(x_ref, w_ref, dh_ref, dr_ref, dx_ref, gw_ref):
        _, vjp = jax.vjp(_rms, x_ref[...], w_ref[...])
        dx, dw = vjp(dh_ref[...])
        dx_ref[...] = dr_ref[...] + dx

        @pl.when(pl.program_id(0) == 0)
        def _():
            gw_ref[...] = jnp.zeros_like(gw_ref)

        gw_ref[...] += dw

    row = pl.BlockSpec((T, D), lambda i: (i, 0))
    vec = pl.BlockSpec((1, D), lambda i: (0, 0))
    return pl.pallas_call(
        body, name=name, grid=(S // T,), in_specs=[row, vec, row, row], out_specs=[row, vec],
        out_shape=[jax.ShapeDtypeStruct((S, D), F32), jax.ShapeDtypeStruct((1, D), F32)],
        compiler_params=_cp("arbitrary"),
    )(x, w.reshape(1, D), dh, dres)


def _loss_head(x, w, tgt, name):
    S, D = x.shape
    T = _row_tile(S)

    def body(x_ref, w_ref, t_ref, l_ref, dx_ref, gw_ref):
        y, vjp = jax.vjp(_rms, x_ref[...], w_ref[...])
        err = y - t_ref[...]
        part = 0.5 * jnp.sum(jnp.mean(err * err, axis=-1, keepdims=True), axis=0, keepdims=True)
        dx, dw = vjp(err * (1.0 / D))
        dx_ref[...] = dx

        @pl.when(pl.program_id(0) == 0)
        def _():
            gw_ref[...] = jnp.zeros_like(gw_ref)
            l_ref[...] = jnp.zeros_like(l_ref)

        gw_ref[...] += dw
        l_ref[...] += jnp.broadcast_to(part, l_ref.shape)

    row = pl.BlockSpec((T, D), lambda i: (i, 0))
    vec = pl.BlockSpec((1, D), lambda i: (0, 0))
    return pl.pallas_call(
        body, name=name, grid=(S // T,), in_specs=[row, vec, row],
        out_specs=[pl.BlockSpec((8, 128), lambda i: (0, 0)), row, vec],
        out_shape=[jax.ShapeDtypeStruct((8, 128), F32), jax.ShapeDtypeStruct((S, D), F32), jax.ShapeDtypeStruct((1, D), F32)],
        compiler_params=_cp("arbitrary"),
    )(x, w.reshape(1, D), tgt)


def _by_group(shape, vals):
    g = _lanes(shape) // POOL_GD
    out = vals[-1]
    for k in range(len(vals) - 2, -1, -1):
        out = jnp.where(g == k, vals[k], out)
    return out


def _pool_d(prev, u, t0):
    ext = jnp.concatenate([prev, u], axis=0)
    s2 = ext + _down(ext, 1)
    s4 = s2 + _down(s2, 2)
    s8 = s4 + _down(s4, 4)
    s16 = s8 + _down(s8, 8)
    ssel = _by_group(ext.shape, [s2, s4, s8, s16])[POOL_HALO:]
    win = _by_group(u.shape, [jnp.int32(w) for w in POOL_WINDOWS])
    cnt = jnp.minimum(t0 + _rows(u.shape) + 1, win).astype(F32)
    return ssel / cnt - u


def _pool_lin(d, w_ref, b):
    ys = [_mm(d[:, g * POOL_GD:(g + 1) * POOL_GD], w_ref[g]) for g in range(POOL_G)]
    return jnp.concatenate(ys, axis=1) + b


def _pool_fwd(proj, w, b, scale, name):
    S = proj.shape[0]
    T = _row_tile(S)
    r = T // POOL_HALO
    cb = PPOOL // POOL_W

    def body(u_ref, up_ref, w_ref, b_ref, sc_ref, y_ref):
        i = pl.program_id(0)
        prev = jnp.where(i > 0, up_ref[...], 0.0)
        d = _pool_d(prev, u_ref[...], i * T)
        y_ref[...] = _pool_lin(d, w_ref, b_ref[...]) * sc_ref[...]

    vec = pl.BlockSpec((1, POOL_W), lambda i: (0, 0))
    return pl.pallas_call(
        body, name=name, grid=(S // T,),
        in_specs=[pl.BlockSpec((T, POOL_W), lambda i: (i, cb)),
                  pl.BlockSpec((POOL_HALO, POOL_W), lambda i: (jnp.maximum(i * r - 1, 0), cb)),
                  pl.BlockSpec((POOL_G, POOL_GD, POOL_GD), lambda i: (0, 0, 0)), vec, vec],
        out_specs=pl.BlockSpec((T, POOL_W), lambda i: (i, 0)), out_shape=jax.ShapeDtypeStruct((S, POOL_W), F32),
        compiler_params=_cp("parallel"),
    )(proj, proj, w, b.reshape(1, POOL_W), scale.reshape(1, POOL_W))


def _pool_bwd(proj, dmixed, w, b, scale, name):
    S = proj.shape[0]
    T = _row_tile(S)
    n = S // T
    r = T // POOL_HALO
    cb = PPOOL // POOL_W
    mb = 1536 // POOL_W

    def body(u_ref, up_ref, dy_ref, dyn_ref, w_ref, b_ref, sc_ref, du_ref, gw_ref, gb_ref, gs_ref):
        i = pl.program_id(0)
        sc = sc_ref[...]
        dy = dy_ref[...]
        dy_ext = jnp.concatenate([dy, jnp.where(i < n - 1, dyn_ref[...], 0.0)], axis=0)
        dyl = dy_ext * sc
        dd = jnp.concatenate(
            [_mm(dyl[:, g * POOL_GD:(g + 1) * POOL_GD], w_ref[g], False, True) for g in range(POOL_G)], axis=1)
        t_ext = i * T + _rows(dd.shape)
        win = _by_group(dd.shape, [jnp.int32(v) for v in POOL_WINDOWS])
        cnt = jnp.minimum(t_ext + 1, win).astype(F32)
        e = jnp.where(t_ext < S, dd / cnt, 0.0)
        f2 = e + _up(e, 1)
        f4 = f2 + _up(f2, 2)
        f8 = f4 + _up(f4, 4)
        f16 = f8 + _up(f8, 8)
        du = (_by_group(dd.shape, [f2, f4, f8, f16]) - dd)[:T]
        du_ref[...] = du.astype(du_ref.dtype)

        prev = jnp.where(i > 0, up_ref[...], 0.0)
        d = _pool_d(prev, u_ref[...], i * T)
        ylin = _pool_lin(d, w_ref, b_ref[...])
        dyl_m = dy * sc

        @pl.when(i == 0)
        def _():
            gw_ref[...] = jnp.zeros_like(gw_ref)
            gb_ref[...] = jnp.zeros_like(gb_ref)
            gs_ref[...] = jnp.zeros_like(gs_ref)

        gs_ref[...] += jnp.sum(dy * ylin, axis=0, keepdims=True)
        gb_ref[...] += jnp.sum(dyl_m, axis=0, keepdims=True)
        for g in range(POOL_G):
            sl = slice(g * POOL_GD, (g + 1) * POOL_GD)
            gw_ref[g] += _mm(d[:, sl], dyl_m[:, sl], True, False)

    vec = pl.BlockSpec((1, POOL_W), lambda i: (0, 0))
    wsp = pl.BlockSpec((POOL_G, POOL_GD, POOL_GD), lambda i: (0, 0, 0))
    nh = S // POOL_HALO
    return pl.pallas_call(
        body, name=name, grid=(n,),
        in_specs=[pl.BlockSpec((T, POOL_W), lambda i: (i, cb)),
                  pl.BlockSpec((POOL_HALO, POOL_W), lambda i: (jnp.maximum(i * r - 1, 0), cb)),
                  pl.BlockSpec((T, POOL_W), lambda i: (i, mb)),
                  pl.BlockSpec((POOL_HALO, POOL_W), lambda i: (jnp.minimum((i + 1) * r, nh - 1), mb)),
                  wsp, vec, vec],
        out_specs=[pl.BlockSpec((T, POOL_W), lambda i: (i, 0)), wsp, vec, vec],
        out_shape=[jax.ShapeDtypeStruct((S, POOL_W), BF16), jax.ShapeDtypeStruct((POOL_G, POOL_GD, POOL_GD), F32),
                   jax.ShapeDtypeStruct((1, POOL_W), F32), jax.ShapeDtypeStruct((1, POOL_W), F32)],
        compiler_params=_cp("arbitrary"),
    )(proj, proj, dmixed, dmixed, w, b.reshape(1, POOL_W), scale.reshape(1, POOL_W))


def _conv_rows(ext, w_ref, taps):
    acc = w_ref[taps - 1:taps, :] * ext
    for k in range(1, taps):
        acc = acc + w_ref[taps - 1 - k:taps - k, :] * _down(ext, k)
    return acc


def _conv_t_rows(dc, w_ref, taps):
    acc = w_ref[taps - 1:taps, :] * dc
    for k in range(1, taps):
        acc = acc + w_ref[taps - 1 - k:taps - k, :] * _up(dc, k)
    return acc


def _conv_specs(T, S, ncb0, with_next):
    r = T // HALO
    nh = S // HALO
    main = pl.BlockSpec((T, CB), lambda j, i: (i, j + ncb0))
    prev = pl.BlockSpec((HALO, CB), lambda j, i: (jnp.maximum(i * r - 1, 0), j + ncb0))
    nxt = pl.BlockSpec((HALO, CB), lambda j, i: (jnp.minimum((i + 1) * r, nh - 1), j + ncb0))
    return (main, prev, nxt) if with_next else (main, prev)


def _gdn_conv_fwd(proj, w, name):
    S = proj.shape[0]
    T = _row_tile(S)
    taps = w.shape[0]
    ncb = 3 * GDN_W // CB

    def body(x_ref, xp_ref, w_ref, o_ref):
        i = pl.program_id(1)
        ext = jnp.concatenate([jnp.where(i > 0, xp_ref[...], 0.0), x_ref[...]], axis=0)
        o_ref[...] = jax.nn.silu(_conv_rows(ext, w_ref, taps)[HALO:])

    main, prev = _conv_specs(T, S, PQ // CB, False)
    return pl.pallas_call(
        body, name=name, grid=(ncb, S // T),
        in_specs=[main, prev, pl.BlockSpec((taps, CB), lambda j, i: (0, j))],
        out_specs=pl.BlockSpec((T, CB), lambda j, i: (i, j)), out_shape=jax.ShapeDtypeStruct((S, 3 * GDN_W), F32),
        compiler_params=_cp("parallel", "parallel"),
    )(proj, proj, w)


def _gdn_conv_bwd(proj, dact, w, name):
    S = proj.shape[0]
    T = _row_tile(S)
    n = S // T
    taps = w.shape[0]
    ncb = 3 * GDN_W // CB

    def body(x_ref, xp_ref, xn_ref, d_ref, dn_ref, w_ref, dx_ref, gw_ref):
        i = pl.program_id(1)
        last = i == n - 1
        ext = jnp.concatenate([jnp.where(i > 0, xp_ref[...], 0.0), x_ref[...], jnp.where(last, 0.0, xn_ref[...])], axis=0)
        c = _conv_rows(ext, w_ref, taps)[HALO:]
        d_ext = jnp.concatenate([d_ref[...], jnp.where(last, 0.0, dn_ref[...])], axis=0)
        _, vjp = jax.vjp(jax.nn.silu, c)
        dc = vjp(d_ext)[0]
        dx_ref[...] = _conv_t_rows(dc, w_ref, taps)[:T].astype(dx_ref.dtype)

        @pl.when(i == 0)
        def _():
            gw_ref[...] = jnp.zeros_like(gw_ref)

        dcm = dc[:T]
        for k in range(taps):
            gw_ref[taps - 1 - k:taps - k, :] += jnp.sum(dcm * _down(ext, k)[HALO:HALO + T], axis=0, keepdims=True)

    main, prev, nxt = _conv_specs(T, S, PQ // CB, True)
    dmain, _, dnxt = _conv_specs(T, S, 0, True)
    wsp = pl.BlockSpec((taps, CB), lambda j, i: (0, j))
    return pl.pallas_call(
        body, name=name, grid=(ncb, n), in_specs=[main, prev, nxt, dmain, dnxt, wsp],
        out_specs=[pl.BlockSpec((T, CB), lambda j, i: (i, j)), wsp],
        out_shape=[jax.ShapeDtypeStruct((S, 3 * GDN_W), BF16), jax.ShapeDtypeStruct((taps, 3 * GDN_W), F32)],
        compiler_params=_cp("parallel", "arbitrary"),
    )(proj, proj, proj, dact, dact, w)


def _ffn_act_fwd(up, w, name):
    S = up.shape[0]
    T = _row_tile(S)
    taps = w.shape[0]
    ncb = D_FF // CB

    def body(g_ref, gp_ref, v_ref, w_ref, o_ref):
        i = pl.program_id(1)
        ext = jnp.concatenate([jnp.where(i > 0, gp_ref[...], 0.0), g_ref[...]], axis=0)
        c = _conv_rows(ext, w_ref, taps)[HALO:]
        o_ref[...] = (jax.nn.gelu(c) * v_ref[...]).astype(o_ref.dtype)

    main, prev = _conv_specs(T, S, 0, False)
    val = pl.BlockSpec((T, CB), lambda j, i: (i, j + ncb))
    return pl.pallas_call(
        body, name=name, grid=(ncb, S // T),
        in_specs=[main, prev, val, pl.BlockSpec((taps, CB), lambda j, i: (0, j))],
        out_specs=pl.BlockSpec((T, CB), lambda j, i: (i, j)), out_shape=jax.ShapeDtypeStruct((S, D_FF), BF16),
        compiler_params=_cp("parallel", "parallel"),
    )(up, up, up, w)


def _ffn_act_bwd(up, dact, w, name):
    S = up.shape[0]
    T = _row_tile(S)
    n = S // T
    taps = w.shape[0]
    ncb = D_FF // CB

    def body(g_ref, gp_ref, gn_ref, v_ref, vn_ref, d_ref, dn_ref, w_ref, dup_ref, gw_ref):
        i = pl.program_id(1)
        last = i == n - 1
        ext = jnp.concatenate([jnp.where(i > 0, gp_ref[...], 0.0), g_ref[...], jnp.where(last, 0.0, gn_ref[...])], axis=0)
        c = _conv_rows(ext, w_ref, taps)[HALO:]
        v_ext = jnp.concatenate([v_ref[...], jnp.where(last, 0.0, vn_ref[...])], axis=0)
        d_ext = jnp.concatenate([d_ref[...], jnp.where(last, 0.0, dn_ref[...])], axis=0)
        gl, vjp = jax.vjp(jax.nn.gelu, c)
        dup_ref[1] = (d_ext * gl)[:T].astype(dup_ref.dtype)
        dc = vjp(d_ext * v_ext)[0]
        dup_ref[0] = _conv_t_rows(dc, w_ref, taps)[:T].astype(dup_ref.dtype)

        @pl.when(i == 0)
        def _():
            gw_ref[...] = jnp.zeros_like(gw_ref)

        dcm = dc[:T]
        for k in range(taps):
            gw_ref[taps - 1 - k:taps - k, :] += jnp.sum(dcm * _down(ext, k)[HALO:HALO + T], axis=0, keepdims=True)

    main, prev, nxt = _conv_specs(T, S, 0, True)
    vmain, _, vnxt = _conv_specs(T, S, ncb, True)
    wsp = pl.BlockSpec((taps, CB), lambda j, i: (0, j))
    osp = pl.BlockSpec((2, T, CB), lambda j, i: (0, i, j))
    return pl.pallas_call(
        body, name=name, grid=(ncb, n), in_specs=[main, prev, nxt, vmain, vnxt, main, nxt, wsp],
        out_specs=[osp, wsp],
        out_shape=[jax.ShapeDtypeStruct((2, S, D_FF), BF16), jax.ShapeDtypeStruct((taps, D_FF), F32)],
        compiler_params=_cp("parallel", "arbitrary"),
    )(up, up, up, up, up, dact, dact, w)


def _tri_masks():
    r = _rows((GDN_C, GDN_C))
    c = _lanes((GDN_C, GDN_C))
    return r >= c, r > c


def _each(fn, *cols):
    return tuple(fn(*args) for args in zip(*cols))


def _tri_inv_raw(lows):
    r = _rows(lows[0].shape)
    c = _lanes(lows[0].shape)
    eye = jnp.where(r == c, 1.0, 0.0)
    ps = _each(lambda low: eye - low, lows)
    lps = lows
    for _ in range(5):
        lps = _each(lambda lp: _mm(lp, lp, False, False, True), lps)
        ps = _each(lambda p, lp: p + _mm(p, lp, False, False, True), ps, lps)
    return ps


@jax.custom_vjp
def _tri_inv(lows):
    return _tri_inv_raw(lows)


def _tri_inv_fwd(lows):
    ts = _tri_inv_raw(lows)
    return ts, ts


def _tri_inv_bwd(ts, dts):
    inner = _each(lambda t, dt: _mm(t, dt, True, False, True), ts, dts)
    return (_each(lambda m, t: -_mm(m, t, False, True, True), inner, ts),)


_tri_inv.defvjp(_tri_inv_fwd, _tri_inv_bwd)


@jax.custom_vjp
def _tri_inv_given(lows, ts):
    return ts


def _tri_inv_given_fwd(lows, ts):
    return ts, ts


def _tri_inv_given_bwd(ts, dts):
    return _tri_inv_bwd(ts, dts)[0], _each(jnp.zeros_like, ts)


_tri_inv_given.defvjp(_tri_inv_given_fwd, _tri_inv_given_bwd)


def _gdn_glog(a_col, alog, dtb):
    return -jnp.exp(alog) * jax.nn.softplus(a_col + dtb)


def _decay_operand():
    r = _rows((GDN_C, 2 * GDN_C))
    c = _lanes((GDN_C, 2 * GDN_C))
    return jnp.where((c >= GDN_C) | (r > c), 1.0, 0.0)


def _gdn_decay(glog):
    causal, _ = _tri_masks()
    res = _mm01(jnp.where(causal, 1.0, 0.0), glog * _decay_operand())
    return res[:, GDN_C:GDN_C + 1], res[:, :GDN_C]


def _gdn_decay_bwd(dgcol, dd):
    r = _rows((GDN_C, GDN_C))
    c = _lanes((GDN_C, GDN_C))
    dres = jnp.concatenate([dd, jnp.where(c == 0, dgcol, 0.0)], axis=1)
    dx = _mm01(jnp.where(r <= c, 1.0, 0.0), dres)
    return jnp.sum(dx * _decay_operand(), axis=1, keepdims=True)


def _gdn_chunk(qa, ka, va, bt_col, gcol, dmat, t_saved=None):
    causal, strict = _tri_masks()
    qn = _each(lambda q: q * lax.rsqrt(jnp.sum(q * q, axis=-1, keepdims=True) + EPS) * (GDN_DH ** -0.5), qa)
    kn = _each(lambda k: k * lax.rsqrt(jnp.sum(k * k, axis=-1, keepdims=True) + EPS), ka)
    beta = _each(jax.nn.sigmoid, bt_col)
    eg = _each(jnp.exp, gcol)
    decay = _each(lambda d: jnp.where(causal, jnp.exp(d), 0.0), dmat)
    kk = _each(lambda k: _mm(k, k, False, True), kn)
    low = _each(lambda b, m, d: jnp.where(strict, b * m * d, 0.0), beta, kk, decay)
    t = _tri_inv(low) if t_saved is None else _tri_inv_given(low, t_saved)
    w = _each(lambda t_, k, b, e: _mm(t_, k * (b * e), False, False, True), t, kn, beta, eg)
    u = _each(lambda t_, v, b: _mm(t_, v * b, False, False, True), t, va, beta)
    attn = _each(lambda q, k, d: _mm(q, k, False, True) * d, qn, kn, decay)
    last = _rows(gcol[0].shape) == GDN_C - 1
    g_last = _each(lambda g: jnp.sum(jnp.where(last, g, 0.0), axis=0, keepdims=True), gcol)
    qd = _each(lambda q, e: q * e, qn, eg)
    kd = _each(lambda k, gl, g: k * jnp.exp(gl - g), kn, g_last, gcol)
    return (w, u, qd, kd, attn), t


def _gdn_step(state, w, u, qd, kd, attn, egl):
    v_new = _each(lambda u_, w_, s: u_ - _mm(w_, s), u, w, state)
    o_state = _each(_mm, qd, state)
    o = _each(lambda os, a, v: os + _mm(a, v), o_state, attn, v_new)
    new = _each(lambda s, e, k, v: s * e + _mm(k, v, True, False), state, egl, kd, v_new)
    return o, new


def _heads(ref, base=0, width=GDN_DH):
    return tuple(ref[:, (base + h) * GDN_DH:(base + h) * GDN_DH + width] for h in range(GDN_H))


def _cols(a, base):
    return tuple(a[:, base + h:base + h + 1] for h in range(GDN_H))


def _gated_norm(o, z, nw):
    return o * lax.rsqrt(jnp.mean(o * o, axis=-1, keepdims=True) + EPS) * nw * jax.nn.silu(z)


def _hsl(h):
    return slice(h * GDN_DH, (h + 1) * GDN_DH)


def _pad_lanes(a, width=GDN_DH):
    return jnp.concatenate([a, jnp.zeros((a.shape[0], width - a.shape[1]), a.dtype)], axis=1)


def _gdn_prep(qkv, proj, alog, dtb, name):
    S = qkv.shape[0]
    N = S // GDN_C

    def body(qkv_ref, ab_ref, al_ref, dt_ref, w_ref, u_ref, qd_ref, kd_ref, at_ref, ti_ref, gc_ref):
        ab = ab_ref[...]
        glog = _each(_gdn_glog, _cols(ab, 0), _cols(al_ref[...], 0), _cols(dt_ref[...], 0))
        dec = _each(_gdn_decay, glog)
        gcol, dmat = _each(lambda d: d[0], dec), _each(lambda d: d[1], dec)
        (w, u, qd, kd, attn), tinv = _gdn_chunk(_heads(qkv_ref), _heads(qkv_ref, GDN_H), _heads(qkv_ref, 2 * GDN_H),
                                                _cols(ab, GDN_H), gcol, dmat)
        gc = jnp.zeros((GDN_C, 128), F32)
        for h in range(GDN_H):
            w_ref[:, _hsl(h)] = w[h]
            u_ref[:, _hsl(h)] = u[h]
            qd_ref[:, _hsl(h)] = qd[h]
            kd_ref[:, _hsl(h)] = kd[h]
            at_ref[:, _hsl(h)] = _pad_lanes(attn[h])
            ti_ref[:, _hsl(h)] = _pad_lanes(tinv[h])
            gc = jnp.where(_lanes(gc.shape) == h, gcol[h], gc)
        gc_ref[...] = gc

    vec = pl.BlockSpec((1, 128), lambda i: (0, 0))
    hsp = pl.BlockSpec((GDN_C, GDN_W), lambda i: (i, 0))
    hshape = jax.ShapeDtypeStruct((S, GDN_W), F32)
    return pl.pallas_call(
        body, name=name, grid=(N,),
        in_specs=[pl.BlockSpec((GDN_C, 3 * GDN_W), lambda i: (i, 0)), pl.BlockSpec((GDN_C, 128), lambda i: (i, PAB // 128)), vec, vec],
        out_specs=[hsp] * 6 + [pl.BlockSpec((GDN_C, 128), lambda i: (i, 0))],
        out_shape=[hshape] * 6 + [jax.ShapeDtypeStruct((S, 128), F32)],
        compiler_params=_cp("parallel"),
    )(qkv, proj, alog, dtb)


def _gdn_scan(w, u, qd, kd, attn, gc, name):
    S = w.shape[0]
    N = S // GDN_C

    def body(w_ref, u_ref, qd_ref, kd_ref, at_ref, gc_ref, o_ref, st_ref, s_ref):
        @pl.when(pl.program_id(0) == 0)
        def _():
            s_ref[...] = jnp.zeros_like(s_ref)

        state = tuple(s_ref[_hsl(h), :] for h in range(GDN_H))
        egl = _each(jnp.exp, _cols(gc_ref[GDN_C - 1:GDN_C, :], 0))
        o, new = _gdn_step(state, _heads(w_ref), _heads(u_ref), _heads(qd_ref), _heads(kd_ref),
                           _heads(at_ref, width=GDN_C), egl)
        for h in range(GDN_H):
            st_ref[_hsl(h), :] = state[h]
            o_ref[:, _hsl(h)] = o[h]
            s_ref[_hsl(h), :] = new[h]

    hsp = pl.BlockSpec((GDN_C, GDN_W), lambda i: (i, 0))
    return pl.pallas_call(
        body, name=name, grid=(N,),
        in_specs=[hsp] * 5 + [pl.BlockSpec((GDN_C, 128), lambda i: (i, 0))],
        out_specs=[hsp, pl.BlockSpec((None, GDN_W, GDN_DH), lambda i: (i, 0, 0))],
        out_shape=[jax.ShapeDtypeStruct((S, GDN_W), F32), jax.ShapeDtypeStruct((N, GDN_W, GDN_DH), F32)],
        scratch_shapes=[pltpu.VMEM((GDN_W, GDN_DH), F32)],
        compiler_params=_cp("arbitrary"),
    )(w, u, qd, kd, attn, gc)


def _gdn_scan_bwd(w, u, qd, kd, attn, gc, states, o, proj, dmixed, nw, name):
    S = w.shape[0]
    N = S // GDN_C

    def body(w_ref, u_ref, qd_ref, kd_ref, at_ref, gc_ref, st_ref, o_ref, z_ref, dm_ref, nw_ref,
             dw_ref, du_ref, dqd_ref, dkd_ref, dat_ref, dgl_ref, dz_ref, gnw_ref, ds_ref):
        @pl.when(pl.program_id(0) == 0)
        def _():
            ds_ref[...] = jnp.zeros_like(ds_ref)
            gnw_ref[...] = jnp.zeros_like(gnw_ref)

        nw = nw_ref[...]
        _, vjp_n = jax.vjp(lambda o, z, w_: _each(lambda a, b: _gated_norm(a, b, w_), o, z), _heads(o_ref), _heads(z_ref), nw)
        do, dz, dnw = vjp_n(_heads(dm_ref))
        state = tuple(st_ref[_hsl(h), :] for h in range(GDN_H))
        egl = _each(jnp.exp, _cols(gc_ref[GDN_C - 1:GDN_C, :], 0))
        _, vjp_s = jax.vjp(_gdn_step, state, _heads(w_ref), _heads(u_ref), _heads(qd_ref), _heads(kd_ref),
                           _heads(at_ref, width=GDN_C), egl)
        ds, dw, du, dqd, dkd, dat, degl = vjp_s((do, tuple(ds_ref[_hsl(h), :] for h in range(GDN_H))))
        dgl = jnp.zeros((8, 128), F32)
        for h in range(GDN_H):
            dz_ref[:, _hsl(h)] = dz[h].astype(dz_ref.dtype)
            ds_ref[_hsl(h), :] = ds[h]
            dw_ref[:, _hsl(h)] = dw[h]
            du_ref[:, _hsl(h)] = du[h]
            dqd_ref[:, _hsl(h)] = dqd[h]
            dkd_ref[:, _hsl(h)] = dkd[h]
            dat_ref[:, _hsl(h)] = _pad_lanes(dat[h])
            dgl = jnp.where(_lanes(dgl.shape) == h, degl[h] * egl[h], dgl)
        dgl_ref[...] = dgl
        gnw_ref[...] += dnw

    rev = lambda i: (N - 1 - i, 0)
    hsp = pl.BlockSpec((GDN_C, GDN_W), rev)
    gsp = pl.BlockSpec((GDN_C, 128), rev)
    vec = pl.BlockSpec((1, GDN_DH), lambda i: (0, 0))
    hshape = jax.ShapeDtypeStruct((S, GDN_W), F32)
    return pl.pallas_call(
        body, name=name, grid=(N,),
        in_specs=[hsp] * 5 + [gsp, pl.BlockSpec((None, GDN_W, GDN_DH), lambda i: (N - 1 - i, 0, 0)), hsp,
                              pl.BlockSpec((GDN_C, GDN_W), lambda i: (N - 1 - i, PZ // GDN_W)), hsp, vec],
        out_specs=[hsp] * 5 + [pl.BlockSpec((8, 128), rev), hsp, vec],
        out_shape=[hshape] * 5 + [jax.ShapeDtypeStruct((N * 8, 128), F32), jax.ShapeDtypeStruct((S, GDN_W), BF16),
                                  jax.ShapeDtypeStruct((1, GDN_DH), F32)],
        scratch_shapes=[pltpu.VMEM((GDN_W, GDN_DH), F32)],
        compiler_params=_cp("arbitrary"),
    )(w, u, qd, kd, attn, gc, states, o, proj, dmixed, nw)


def _gdn_prep_bwd(qkv, proj, alog, dtb, tinv, dw, du, dqd, dkd, dat, dgl, name):
    S = qkv.shape[0]
    N = S // GDN_C

    def body(qkv_ref, ab_ref, al_ref, dt_ref, ti_ref, dw_ref, du_ref, dqd_ref, dkd_ref, dat_ref, dgl_ref,
             dqkv_ref, dab_ref, gal_ref, gdt_ref):
        @pl.when(pl.program_id(0) == 0)
        def _():
            gal_ref[...] = jnp.zeros_like(gal_ref)
            gdt_ref[...] = jnp.zeros_like(gdt_ref)

        ab = ab_ref[...]
        glog, vjp_g = jax.vjp(lambda a, al, dt: _each(_gdn_glog, a, al, dt), _cols(ab, 0), _cols(al_ref[...], 0),
                              _cols(dt_ref[...], 0))
        dec = _each(_gdn_decay, glog)
        gcol, dmat = _each(lambda d: d[0], dec), _each(lambda d: d[1], dec)
        _, vjp_c, _ = jax.vjp(functools.partial(_gdn_chunk, t_saved=_heads(ti_ref, width=GDN_C)), _heads(qkv_ref),
                              _heads(qkv_ref, GDN_H), _heads(qkv_ref, 2 * GDN_H), _cols(ab, GDN_H), gcol, dmat, has_aux=True)
        dqa, dka, dva, dbt, dgcol, dd = vjp_c((_heads(dw_ref), _heads(du_ref), _heads(dqd_ref), _heads(dkd_ref),
                                               _heads(dat_ref, width=GDN_C)))
        last = _rows(dgcol[0].shape) == GDN_C - 1
        dgcol = _each(lambda d, g: d + jnp.where(last, g, 0.0), dgcol, _cols(dgl_ref[0:1, :], 0))
        da_col, dal, ddt = vjp_g(_each(_gdn_decay_bwd, dgcol, dd))
        dab = jnp.zeros((GDN_C, 128), F32)
        gal = jnp.zeros((1, 128), F32)
        gdt = jnp.zeros((1, 128), F32)
        for h in range(GDN_H):
            dqkv_ref[:, _hsl(h)] = dqa[h]
            dqkv_ref[:, _hsl(GDN_H + h)] = dka[h]
            dqkv_ref[:, _hsl(2 * GDN_H + h)] = dva[h]
            ln = _lanes(dab.shape)
            dab = dab + jnp.where(ln == h, da_col[h], 0.0) + jnp.where(ln == GDN_H + h, dbt[h], 0.0)
            l1 = _lanes(gal.shape)
            gal = gal + jnp.where(l1 == h, dal[h], 0.0)
            gdt = gdt + jnp.where(l1 == h, ddt[h], 0.0)
        dab_ref[...] = dab.astype(dab_ref.dtype)
        gal_ref[...] += gal
        gdt_ref[...] += gdt

    vec = pl.BlockSpec((1, 128), lambda i: (0, 0))
    hsp = pl.BlockSpec((GDN_C, GDN_W), lambda i: (i, 0))
    qsp = pl.BlockSpec((GDN_C, 3 * GDN_W), lambda i: (i, 0))
    return pl.pallas_call(
        body, name=name, grid=(N,),
        in_specs=[qsp, pl.BlockSpec((GDN_C, 128), lambda i: (i, PAB // 128)), vec, vec] + [hsp] * 6
        + [pl.BlockSpec((8, 128), lambda i: (i, 0))],
        out_specs=[qsp, pl.BlockSpec((GDN_C, 128), lambda i: (i, 0)), vec, vec],
        out_shape=[jax.ShapeDtypeStruct((S, 3 * GDN_W), F32), jax.ShapeDtypeStruct((S, 128), BF16),
                   jax.ShapeDtypeStruct((1, 128), F32), jax.ShapeDtypeStruct((1, 128), F32)],
        compiler_params=_cp("arbitrary"),
    )(qkv, proj, alog, dtb, tinv, dw, du, dqd, dkd, dat, dgl)


@jax.custom_vjp
def _expm1(x):
    u = jnp.exp(x)
    lu = jnp.log(u)
    small = (u - 1.0) * x / jnp.where(u == 1.0, 1.0, lu)
    small = jnp.where(u == 1.0, x, small)
    return jnp.where(jnp.abs(x) < 0.5, small, u - 1.0)


def _expm1_fwd(x):
    return _expm1(x), jnp.exp(x)


def _expm1_bwd(ex, g):
    return (g * ex,)


_expm1.defvjp(_expm1_fwd, _expm1_bwd)


def _lru_gates(xc, wa, ba, wx, bx, lam, first):
    r = jax.nn.sigmoid(_mm(xc, wa) + ba)
    i = jax.nn.sigmoid(_mm(xc, wx) + bx)
    log_a = -LRU_C * r * jax.nn.softplus(-lam)
    mult = jnp.sqrt(-_expm1(2.0 * log_a))
    mult = jnp.where(first, 1.0, mult)
    return jnp.exp(log_a), mult * i * xc


def _scan_fwd(a, b):
    T = a.shape[0]
    rows = _rows(a.shape)
    s = 1
    while s < T:
        ok = rows >= s
        b = a * jnp.where(ok, _down(b, s), 0.0) + b
        a = a * jnp.where(ok, _down(a, s), 1.0)
        s *= 2
    return a, b


def _scan_rev(a, b):
    T = a.shape[0]
    rows = _rows(a.shape)
    s = 1
    while s < T:
        ok = rows + s < T
        b = a * jnp.where(ok, _up(b, s), 0.0) + b
        a = a * jnp.where(ok, _up(a, s), 1.0)
        s *= 2
    return b


def _bsl(j):
    return slice(j * LRU_BD, (j + 1) * LRU_BD)


def _lru_tile(S):
    return _row_tile(S, 256)


def _lru_fwd(proj, conv_w, conv_b, wa, ba, wx, bx, lam, name):
    S = proj.shape[0]
    T = _lru_tile(S)
    taps = conv_w.shape[0]
    r = T // HALO

    def body(x_ref, xp_ref, cw_ref, cb_ref, wa_ref, ba_ref, wx_ref, bx_ref, lam_ref, h_ref, carry_ref):
        i = pl.program_id(0)

        @pl.when(i == 0)
        def _():
            carry_ref[...] = jnp.zeros_like(carry_ref)

        ext = jnp.concatenate([jnp.where(i > 0, xp_ref[...], 0.0), x_ref[...]], axis=0)
        xc = _conv_rows(ext, cw_ref, taps)[HALO:] + cb_ref[...]
        first = (i * T + _rows((T, LRU_BD))) == 0
        for j in range(LRU_NB):
            a, b = _lru_gates(xc[:, _bsl(j)], wa_ref[j], ba_ref[:, _bsl(j)], wx_ref[j], bx_ref[:, _bsl(j)],
                              lam_ref[:, _bsl(j)], first=first)
            pa, hb = _scan_fwd(a, b)
            h_ref[:, _bsl(j)] = pa * carry_ref[0:1, _bsl(j)] + hb
            carry_ref[0:1, _bsl(j)] = h_ref[T - 1:T, _bsl(j)]

    vec = pl.BlockSpec((1, LRU_W), lambda i: (0, 0))
    wsp = pl.BlockSpec((LRU_NB, LRU_BD, LRU_BD), lambda i: (0, 0, 0))
    return pl.pallas_call(
        body, name=name, grid=(S // T,),
        in_specs=[pl.BlockSpec((T, LRU_W), lambda i: (i, PXR // LRU_W)),
                  pl.BlockSpec((HALO, LRU_W), lambda i: (jnp.maximum(i * r - 1, 0), PXR // LRU_W)),
                  pl.BlockSpec((taps, LRU_W), lambda i: (0, 0)), vec, wsp, vec, wsp, vec, vec],
        out_specs=pl.BlockSpec((T, LRU_W), lambda i: (i, 0)), out_shape=jax.ShapeDtypeStruct((S, LRU_W), F32),
        scratch_shapes=[pltpu.VMEM((8, LRU_W), F32)],
        compiler_params=_cp("arbitrary"),
    )(proj, proj, conv_w, conv_b.reshape(1, LRU_W), wa, ba.reshape(1, LRU_W), wx, bx.reshape(1, LRU_W), lam.reshape(1, LRU_W))


def _lru_bwd(proj, hl, dmixed, conv_w, conv_b, wa, ba, wx, bx, lam, name):
    S = proj.shape[0]
    T = _lru_tile(S)
    n = S // T
    taps = conv_w.shape[0]
    r = T // HALO
    mb = 768 // LRU_W

    def body(x_ref, xp_ref, g_ref, h_ref, hp_ref, dy_ref, cw_ref, cb_ref, wa_ref, ba_ref, wx_ref, bx_ref, lam_ref,
             dx_ref, dg_ref, gcw_ref, gcb_ref, gwa_ref, gba_ref, gwx_ref, gbx_ref, glam_ref, carry_ref, dxc_ref, nxt_ref):
        s = pl.program_id(0)
        i = n - 1 - s

        @pl.when(s == 0)
        def _():
            carry_ref[...] = jnp.zeros_like(carry_ref)
            nxt_ref[...] = jnp.zeros_like(nxt_ref)
            for ref in (gcw_ref, gcb_ref, gwa_ref, gba_ref, gwx_ref, gbx_ref, glam_ref):
                ref[...] = jnp.zeros_like(ref)

        ext = jnp.concatenate([jnp.where(i > 0, xp_ref[...], 0.0), x_ref[...]], axis=0)
        xc = _conv_rows(ext, cw_ref, taps)[HALO:] + cb_ref[...]
        rows = _rows((T, LRU_BD))
        first = (i * T + rows) == 0
        h_before = jnp.where(i > 0, hp_ref[HALO - 1:HALO, :], 0.0)
        for j in range(LRU_NB):
            sl = _bsl(j)
            (a, _), vjp_g = jax.vjp(functools.partial(_lru_gates, first=first), xc[:, sl], wa_ref[j], ba_ref[:, sl],
                                    wx_ref[j], bx_ref[:, sl], lam_ref[:, sl])
            gelu_g, vjp_a = jax.vjp(jax.nn.gelu, g_ref[:, sl])
            h = h_ref[:, sl]
            dy = dy_ref[:, sl]
            dg_ref[:, sl] = vjp_a(dy * h)[0].astype(dg_ref.dtype)
            b_rev = dy * gelu_g + jnp.where(rows == T - 1, carry_ref[0:1, sl], 0.0)
            a_rev = jnp.where(rows == T - 1, 0.0, _up(a, 1))
            dh = _scan_rev(a_rev, b_rev)
            carry_ref[:, sl] = (a * dh)[:HALO]
            h_prev = jnp.where(rows == 0, h_before[:, sl], _down(h, 1))
            dxc, dwa, dba, dwx, dbx, dlam = vjp_g((dh * h_prev, dh))
            dxc_ref[:, sl] = dxc
            gwa_ref[j] += dwa
            gwx_ref[j] += dwx
            gba_ref[:, sl] += dba
            gbx_ref[:, sl] += dbx
            glam_ref[:, sl] += dlam
        dxc = dxc_ref[...]
        d_ext = jnp.concatenate([dxc, nxt_ref[...]], axis=0)
        dx_ref[...] = _conv_t_rows(d_ext, cw_ref, taps)[:T].astype(dx_ref.dtype)
        nxt_ref[...] = dxc[:HALO]
        gcb_ref[...] += jnp.sum(dxc, axis=0, keepdims=True)
        for k in range(taps):
            gcw_ref[taps - 1 - k:taps - k, :] += jnp.sum(dxc * _down(ext, k)[HALO:], axis=0, keepdims=True)

    vec = pl.BlockSpec((1, LRU_W), lambda s: (0, 0))
    wsp = pl.BlockSpec((LRU_NB, LRU_BD, LRU_BD), lambda s: (0, 0, 0))
    cwsp = pl.BlockSpec((taps, LRU_W), lambda s: (0, 0))

    def main(cb):
        return pl.BlockSpec((T, LRU_W), lambda s: (n - 1 - s, cb))

    def prev(cb):
        return pl.BlockSpec((HALO, LRU_W), lambda s: (jnp.maximum((n - 1 - s) * r - 1, 0), cb))

    vshape = jax.ShapeDtypeStruct((1, LRU_W), F32)
    wshape = jax.ShapeDtypeStruct((LRU_NB, LRU_BD, LRU_BD), F32)
    return pl.pallas_call(
        body, name=name, grid=(n,),
        in_specs=[main(PXR // LRU_W), prev(PXR // LRU_W), main(PGR // LRU_W), main(0), prev(0), main(mb),
                  cwsp, vec, wsp, vec, wsp, vec, vec],
        out_specs=[main(0), main(0), cwsp, vec, wsp, vec, wsp, vec, vec],
        out_shape=[jax.ShapeDtypeStruct((S, LRU_W), BF16), jax.ShapeDtypeStruct((S, LRU_W), BF16),
                   jax.ShapeDtypeStruct((taps, LRU_W), F32), vshape, wshape, vshape, wshape, vshape, vshape],
        scratch_shapes=[pltpu.VMEM((8, LRU_W), F32), pltpu.VMEM((T, LRU_W), F32), pltpu.VMEM((HALO, LRU_W), F32)],
        compiler_params=_cp("arbitrary"),
    )(proj, proj, proj, hl, hl, dmixed, conv_w, conv_b.reshape(1, LRU_W), wa, ba.reshape(1, LRU_W), wx,
      bx.reshape(1, LRU_W), lam.reshape(1, LRU_W))


def _mix_out(o, proj, hl, y_pool, nw, name):
    S = o.shape[0]
    T = _row_tile(S)

    def body(o_ref, z_ref, h_ref, g_ref, p_ref, nw_ref, m_ref):
        for h in range(GDN_H):
            m_ref[:, _hsl(h)] = _gated_norm(o_ref[:, _hsl(h)], z_ref[:, _hsl(h)], nw_ref[...]).astype(m_ref.dtype)
        m_ref[:, GDN_W:GDN_W + LRU_W] = (h_ref[...] * jax.nn.gelu(g_ref[...])).astype(m_ref.dtype)
        m_ref[:, GDN_W + LRU_W:] = p_ref[...].astype(m_ref.dtype)

    row = pl.BlockSpec((T, GDN_W), lambda i: (i, 0))
    return pl.pallas_call(
        body, name=name, grid=(S // T,),
        in_specs=[row, pl.BlockSpec((T, GDN_W), lambda i: (i, PZ // GDN_W)), row,
                  pl.BlockSpec((T, LRU_W), lambda i: (i, PGR // LRU_W)), pl.BlockSpec((T, POOL_W), lambda i: (i, 0)),
                  pl.BlockSpec((1, GDN_DH), lambda i: (0, 0))],
        out_specs=pl.BlockSpec((T, D_MODEL), lambda i: (i, 0)), out_shape=jax.ShapeDtypeStruct((S, D_MODEL), BF16),
        compiler_params=_cp("parallel"),
    )(o, proj, hl, proj, y_pool, nw)


def _as2d(a):
    return a.reshape(-1, a.shape[-1])


def _ew_rows(rows, cols):
    t = rows
    while t * cols * 4 > (2 << 20) and t % 16 == 0:
        t //= 2
    return t


def _rs_rows(rows, cols, budget=2 << 20):
    t = rows
    while t * cols * 4 > budget and t % 32 == 0:
        t //= 2
    return t


def _adamw(w, g, m, v, name):
    shape = w.shape
    w2, g2, m2, v2 = _as2d(w), _as2d(g), _as2d(m), _as2d(v)
    rows, cols = w2.shape
    t = _ew_rows(rows, cols)

    def body(w_ref, g_ref, m_ref, v_ref, d_ref, nm_ref, nv_ref):
        gr = g_ref[...]
        nm = ADAM_B1 * m_ref[...] + (1.0 - ADAM_B1) * gr
        nv = ADAM_B2 * v_ref[...] + (1.0 - ADAM_B2) * (gr * gr)
        m_hat = nm / (1.0 - ADAM_B1 ** ADAM_STEP)
        v_hat = nv / (1.0 - ADAM_B2 ** ADAM_STEP)
        d_ref[...] = -ADAM_LR * (m_hat / (jnp.sqrt(v_hat) + ADAM_EPS) + ADAM_WD * w_ref[...])
        nm_ref[...] = nm
        nv_ref[...] = nv

    sp = pl.BlockSpec((t, cols), lambda i: (i, 0))
    sh = jax.ShapeDtypeStruct((rows, cols), F32)
    d, nm, nv = pl.pallas_call(body, name=name, grid=(rows // t,), in_specs=[sp] * 4, out_specs=[sp] * 3,
                               out_shape=[sh] * 3, compiler_params=_cp("parallel"))(w2, g2, m2, v2)
    return d.reshape(shape), nm.reshape(shape), nv.reshape(shape)


def _place():
    return lax.axis_index("x"), lax.axis_index("y"), lax.axis_index("c")


def _gather_weights(arrs, name):
    n = len(arrs)

    def body(*refs):
        outs = refs[n:2 * n]
        send, recv = refs[2 * n:]
        x, y, c = _place()
        s_me, s_x, s_y, s_d = 2 * x + y, 2 * (1 - x) + y, 2 * x + (1 - y), 2 * (1 - x) + (1 - y)
        xpeer, ypeer, sib = (1 - x, y, c), (x, 1 - y, c), (x, y, 1 - c)

        def rc(k, t, src, dst, to):
            return pltpu.make_async_remote_copy(src_ref=src, dst_ref=dst, send_sem=send.at[k, t], recv_sem=recv.at[k, t],
                                                device_id=to, device_id_type=MESH)

        def piece(k, s, top):
            rq = outs[k].shape[2] // 2
            return outs[k].at[s, c, pl.ds(0 if top else rq, rq)]

        sent = []

        def start(k, t, ref, to):
            cp = rc(k, t, ref, ref, to)
            cp.start()
            sent.append(cp)

        for k in range(n):
            start(k, 0, outs[k].at[s_me, c], xpeer)
            start(k, 1, outs[k].at[s_me, c], ypeer)
        for k in range(n):
            got = outs[k].at[s_x, c]
            rc(k, 0, got, got, xpeer).wait_recv()
            start(k, 2, piece(k, s_x, True), ypeer)
            start(k, 3, got, sib)
        for k in range(n):
            got = outs[k].at[s_y, c]
            rc(k, 1, got, got, ypeer).wait_recv()
            start(k, 6, piece(k, s_y, False), xpeer)
            start(k, 4, got, sib)
        for k in range(n):
            top, bottom = piece(k, s_d, True), piece(k, s_d, False)
            rc(k, 2, top, top, ypeer).wait_recv()
            rc(k, 6, bottom, bottom, xpeer).wait_recv()
            start(k, 5, outs[k].at[s_d, c], sib)
        for k in range(n):
            for t, s in ((3, s_x), (4, s_y), (5, s_d)):
                got = outs[k].at[s, 1 - c]
                rc(k, t, got, got, sib).wait_recv()
        for cp in sent:
            cp.wait_send()

    return pl.pallas_call(
        body, name=name, in_specs=[HBM] * n, out_specs=[HBM] * n,
        out_shape=[jax.ShapeDtypeStruct(a.shape, a.dtype) for a in arrs],
        input_output_aliases={k: k for k in range(n)},
        scratch_shapes=[pltpu.SemaphoreType.DMA((n, 7)), pltpu.SemaphoreType.DMA((n, 7))],
    )(*arrs)


def _share_halves(arrs, name):
    n = len(arrs)

    def body(*refs):
        outs = refs[n:2 * n]
        send, recv = refs[2 * n:]
        x, y, c = _place()
        cps = []
        for k in range(n):
            mine = outs[k].at[:, c]
            cp = pltpu.make_async_remote_copy(src_ref=mine, dst_ref=mine, send_sem=send.at[k], recv_sem=recv.at[k],
                                              device_id=(x, y, 1 - c), device_id_type=MESH)
            cp.start()
            cps.append(cp)
        for k in range(n):
            got = outs[k].at[:, 1 - c]
            pltpu.make_async_remote_copy(src_ref=got, dst_ref=got, send_sem=send.at[k], recv_sem=recv.at[k],
                                         device_id=(x, y, 1 - c), device_id_type=MESH).wait_recv()
        for cp in cps:
            cp.wait_send()

    return pl.pallas_call(
        body, name=name, in_specs=[HBM] * n, out_specs=[HBM] * n,
        out_shape=[jax.ShapeDtypeStruct(a.shape, a.dtype) for a in arrs],
        input_output_aliases={k: k for k in range(n)},
        scratch_shapes=[pltpu.SemaphoreType.DMA((n,)), pltpu.SemaphoreType.DMA((n,))],
    )(*arrs)


_REL = tuple((dx, dy, dc) for dx in (0, 1) for dy in (0, 1) for dc in (0, 1))[1:]
SEM = pl.BlockSpec(memory_space=pltpu.SEMAPHORE)
DATAFLOW = pltpu.SideEffectType.DATAFLOW_SIDE_EFFECTING


def _flip(v, d):
    return 1 - v if d else v


def _rs_direct_copies(srcs, land, send, recv):
    x, y, c = _place()
    cps = []
    for k in range(len(srcs)):
        for r, (dx, dy, dc) in enumerate(_REL):
            px, py, pc = _flip(x, dx), _flip(y, dy), _flip(c, dc)
            cps.append(pltpu.make_async_remote_copy(
                src_ref=srcs[k].at[2 * px + py, pc], dst_ref=land[k].at[r], send_sem=send.at[k * len(_REL) + r],
                recv_sem=recv.at[k * len(_REL) + r], device_id=(px, py, pc), device_id_type=MESH))
    return cps


def _rs_direct_start(grads, thru, name):
    n = len(grads)
    lands = [pltpu.with_memory_space_constraint(lax.empty((len(_REL),) + g.shape[2:], g.dtype), pltpu.HBM) for g in grads]

    def body(*refs):
        for cp in _rs_direct_copies(refs[:n], refs[n + 1:2 * n + 1], refs[2 * n + 1], refs[2 * n + 2]):
            cp.start()

    sems = pltpu.SemaphoreType.DMA((n * len(_REL),))
    keep = [pltpu.HBM(a.shape, a.dtype) for a in (*grads, thru, *lands)]
    out = pl.pallas_call(
        body, name=name, in_specs=[HBM] * (2 * n + 1), out_specs=(SEM, SEM) + (HBM,) * (2 * n + 1),
        out_shape=(sems, sems, *keep), input_output_aliases={i: 2 + i for i in range(2 * n + 1)},
        compiler_params=pltpu.CompilerParams(has_side_effects=DATAFLOW),
    )(*[pltpu.with_memory_space_constraint(a, pltpu.HBM) for a in (*grads, thru)], *lands)
    return out[0], out[1], out[2:2 + n], out[2 + n], out[3 + n:]


def _rs_direct_wait(send, recv, grads, lands, after, name):
    n = len(grads)
    after = list(after) if isinstance(after, (list, tuple)) else [after]

    def body(*refs):
        for cp in _rs_direct_copies(refs[:n], refs[n:2 * n], refs[2 * n], refs[2 * n + 1]):
            cp.wait_send()
            cp.wait_recv()

    keep = [pltpu.HBM(a.shape, a.dtype) for a in (*grads, *lands)]
    out = pl.pallas_call(
        body, name=name, in_specs=[HBM] * (2 * n) + [SEM, SEM] + [pl.BlockSpec(memory_space=pl.ANY)] * len(after),
        out_specs=(HBM,) * (2 * n), out_shape=tuple(keep), input_output_aliases={i: i for i in range(2 * n)},
        compiler_params=pltpu.CompilerParams(has_side_effects=DATAFLOW),
    )(*grads, *lands, send, recv, *after)
    return out[:n], out[n:]


_CHIPS = ((1, 0), (0, 1), (1, 1))


def _gather_ici_copies(bufs, send, recv):
    x, y, c = _place()
    cps = []
    for k in range(len(bufs)):
        mine = bufs[k].at[2 * x + y, c]
        for j, (dx, dy) in enumerate(_CHIPS):
            cps.append(pltpu.make_async_remote_copy(
                src_ref=mine, dst_ref=mine, send_sem=send.at[k * len(_CHIPS) + j], recv_sem=recv.at[k * len(_CHIPS) + j],
                device_id=(_flip(x, dx), _flip(y, dy), c), device_id_type=MESH))
    return cps


def _gather_d2d_copies(bufs, send, recv):
    x, y, c = _place()
    cps = []
    for k in range(len(bufs)):
        for j, (dx, dy) in enumerate(_CHIPS):
            got = bufs[k].at[2 * _flip(x, dx) + _flip(y, dy), c]
            cps.append(pltpu.make_async_remote_copy(
                src_ref=got, dst_ref=got, send_sem=send.at[k * len(_CHIPS) + j], recv_sem=recv.at[k * len(_CHIPS) + j],
                device_id=(x, y, 1 - c), device_id_type=MESH))
    return cps


def _copies_start(bufs, thru, copies, name):
    n = len(bufs)

    def body(*refs):
        for cp in copies(refs[:n], refs[n + 1], refs[n + 2]):
            cp.start()

    sems = pltpu.SemaphoreType.DMA((n * len(_CHIPS),))
    out = pl.pallas_call(
        body, name=name, in_specs=[HBM] * (n + 1), out_specs=(SEM, SEM) + (HBM,) * (n + 1),
        out_shape=(sems, sems, *[pltpu.HBM(a.shape, a.dtype) for a in (*bufs, thru)]),
        input_output_aliases={i: 2 + i for i in range(n + 1)},
        compiler_params=pltpu.CompilerParams(has_side_effects=DATAFLOW),
    )(*[pltpu.with_memory_space_constraint(a, pltpu.HBM) for a in (*bufs, thru)])
    return out[0], out[1], out[2:2 + n], out[2 + n]


def _copies_wait(send, recv, bufs, after, copies, name):
    n = len(bufs)

    def body(*refs):
        for cp in copies(refs[:n], refs[n], refs[n + 1]):
            cp.wait_send()
            cp.wait_recv()

    return pl.pallas_call(
        body, name=name, in_specs=[HBM] * n + [SEM, SEM, pl.BlockSpec(memory_space=pl.ANY)], out_specs=(HBM,) * n,
        out_shape=tuple(pltpu.HBM(a.shape, a.dtype) for a in bufs), input_output_aliases={i: i for i in range(n)},
        compiler_params=pltpu.CompilerParams(has_side_effects=DATAFLOW),
    )(*bufs, send, recv, after)


def _rs_direct_sum(grad, land, layer, name, into=None):
    _, _, rows, cols = grad.shape
    t = _rs_rows(rows, cols, 6 << 20)
    npieces = len(_REL) + 1

    def body(g_ref, l_ref, *rest):
        o_ref, acc_ref = rest[-2], rest[-1]
        j = pl.program_id(1)

        @pl.when(j == 0)
        def _():
            acc_ref[...] = g_ref[...].astype(F32)

        @pl.when(j > 0)
        def _():
            acc_ref[...] += l_ref[...].astype(F32)

        @pl.when(j == npieces - 1)
        def _():
            o_ref[...] = acc_ref[...]

    def mine(i, j):
        x, y, c = _place()
        return (2 * x + y, c, i, 0)

    in_specs = [pl.BlockSpec((None, None, t, cols), mine),
                pl.BlockSpec((None, t, cols), lambda i, j: (jnp.maximum(j - 1, 0), i, 0))]
    args = [grad, land]
    if into is not None:
        in_specs.append(pl.BlockSpec(memory_space=pl.ANY))
        args.append(into)
    return pl.pallas_call(
        body, name=name, grid=(rows // t, npieces), in_specs=in_specs,
        out_specs=pl.BlockSpec((None, None, t, cols), lambda i, j: (layer, lax.axis_index("c"), i, 0)),
        scratch_shapes=[pltpu.VMEM((t, cols), F32)],
        out_shape=jax.ShapeDtypeStruct((2, 2, rows, cols), F32), input_output_aliases={} if into is None else {2: 0},
        compiler_params=_cp("parallel", "arbitrary"),
    )(*args)


def _ar_copies(buf, land, send, recv):
    x, y, c = _place()
    return [pltpu.make_async_remote_copy(src_ref=buf, dst_ref=land.at[r], send_sem=send.at[r], recv_sem=recv.at[r],
                                         device_id=(_flip(x, dx), _flip(y, dy), _flip(c, dc)), device_id_type=MESH)
            for r, (dx, dy, dc) in enumerate(_REL)]


def _ar_start(buf, name, thru=None):
    land = pltpu.with_memory_space_constraint(lax.empty((len(_REL),) + buf.shape, buf.dtype), pltpu.HBM)
    extra = [] if thru is None else [thru]

    def body(buf_ref, land_ref, *rest):
        send, recv = rest[len(extra)], rest[len(extra) + 1]
        for cp in _ar_copies(buf_ref, land_ref, send, recv):
            cp.start()

    sems = pltpu.SemaphoreType.DMA((len(_REL),))
    return pl.pallas_call(
        body, name=name, in_specs=[HBM] * (2 + len(extra)), out_specs=(SEM, SEM) + (HBM,) * (2 + len(extra)),
        out_shape=(sems, sems, *[pltpu.HBM(a.shape, a.dtype) for a in (buf, land, *extra)]),
        input_output_aliases={i: 2 + i for i in range(2 + len(extra))},
        compiler_params=pltpu.CompilerParams(has_side_effects=DATAFLOW),
    )(pltpu.with_memory_space_constraint(buf, pltpu.HBM), land, *[pltpu.with_memory_space_constraint(a, pltpu.HBM) for a in extra])


def _ar_wait(send, recv, buf, land, after, name):
    def body(buf_ref, land_ref, send_ref, recv_ref, *_):
        for cp in _ar_copies(buf_ref, land_ref, send_ref, recv_ref):
            cp.wait_send()
            cp.wait_recv()

    return pl.pallas_call(
        body, name=name, in_specs=[HBM, HBM, SEM, SEM, pl.BlockSpec(memory_space=pl.ANY)], out_specs=(HBM, HBM),
        out_shape=(pltpu.HBM(buf.shape, buf.dtype), pltpu.HBM(land.shape, land.dtype)), input_output_aliases={0: 0, 1: 1},
        compiler_params=pltpu.CompilerParams(has_side_effects=DATAFLOW),
    )(buf, land, send, recv, after)


def _ar_sum(buf, land, name):
    rows, cols = buf.shape
    t = _rs_rows(rows, cols)

    def slot(i, j):
        x, y, c = _place()
        xd, yd, cd = j // 4, (j // 2) % 2, j % 2
        rel = 4 * (x + xd - 2 * x * xd) + 2 * (y + yd - 2 * y * yd) + (c + cd - 2 * c * cd)
        return (jnp.maximum(rel - 1, 0), i, 0)

    def body(b_ref, l_ref, o_ref, acc_ref):
        j = pl.program_id(1)
        x, y, c = _place()
        val = jnp.where(j == 4 * x + 2 * y + c, b_ref[...], l_ref[...])

        @pl.when(j == 0)
        def _():
            acc_ref[...] = val

        @pl.when(j > 0)
        def _():
            acc_ref[...] += val

        @pl.when(j == len(_REL))
        def _():
            o_ref[...] = acc_ref[...]

    sp = pl.BlockSpec((t, cols), lambda i, j: (i, 0))
    return pl.pallas_call(
        body, name=name, grid=(rows // t, len(_REL) + 1), in_specs=[sp, pl.BlockSpec((None, t, cols), slot)], out_specs=sp,
        out_shape=jax.ShapeDtypeStruct((rows, cols), F32), scratch_shapes=[pltpu.VMEM((t, cols), F32)],
        compiler_params=_cp("parallel", "arbitrary"),
    )(buf, land)


def _pad128(v):
    return jnp.zeros((1, 128), F32).at[0, :v.shape[0]].set(v)


def _layer_fwd(l, x, p, hooks=None):
    hooks = hooks or {}
    h1 = _rms_fwd(x, p["norm1_w"], f"rms1_{l}")
    proj = _matmul(h1, p["w_in"], "nn", name=f"mm_in_{l}", tn=768)
    y_pool = _pool_fwd(proj, p["pool_w"], p["pool_b"], p["pool_scale"], f"pool_{l}")
    qkv = _gdn_conv_fwd(proj, p["gdn_conv_w"], f"gconv_{l}")
    alog, dtb = _pad128(p["gdn_a_log"]), _pad128(p["gdn_dt_bias"])
    gw, gu, gqd, gkd, gat, tinv, gc = _gdn_prep(qkv, proj, alog, dtb, f"gprep_{l}")
    o, states = _gdn_scan(gw, gu, gqd, gkd, gat, gc, f"gscan_{l}")
    if "mid" in hooks:
        o = hooks["mid"](o)
    hl = _lru_fwd(proj, p["lru_conv_w"], p["lru_conv_b"], p["lru_wa"], p["lru_ba"], p["lru_wx"], p["lru_bx"],
                  p["lru_lambda"], f"lru_{l}")
    mixed = _mix_out(o, proj, hl, y_pool, p["gdn_norm_w"].reshape(1, GDN_DH), f"mix_{l}")
    x2 = _matmul(mixed, p["w_out"], "nn", name=f"mm_out_{l}", res=x)
    h2 = _rms_fwd(x2, p["norm2_w"], f"rms2_{l}")
    if "ffn" in hooks:
        h2 = hooks["ffn"](h2)
    up = _matmul(h2, p["ffn_up"], "nn", name=f"mm_up_{l}", b_split=True)
    act = _ffn_act_fwd(up, p["ffn_conv_w"], f"ffn_{l}")
    if "down" in hooks:
        act = hooks["down"](act)
    x3 = _matmul(act, p["ffn_down"], "nn", name=f"mm_down_{l}", res=x2)
    saved = dict(x=x, h1=h1, proj=proj, qkv=qkv, gdn=(gw, gu, gqd, gkd, gat, gc), tinv=tinv, states=states, o=o, hl=hl, mixed=mixed,
                 x2=x2, h2=h2, up=up, act=act, alog=alog, dtb=dtb)
    return x3, saved


def _layer_bwd(l, dx3, p, s, after_ffn=None, after_mix=None):
    g = {}
    dact = _matmul(dx3, p["ffn_down"], "nt", name=f"mm_ddown_{l}")
    g["ffn_down"] = _matmul(s["act"], dx3, "tn", name=f"mm_gdown_{l}", out_dtype=BF16)
    dup, g["ffn_conv_w"] = _ffn_act_bwd(s["up"], dact, p["ffn_conv_w"], f"ffn_b_{l}")
    dh2 = _matmul(dup, p["ffn_up"], "nt", name=f"mm_dup_{l}", b_split=True, tk=3072)
    g["ffn_up"] = _matmul(s["h2"], dup, "tn", name=f"mm_gup_{l}", b_split=True, o_split=4, tk=4096, out_dtype=BF16)
    dx2, g["norm2_w"] = _rms_bwd(s["x2"], p["norm2_w"], dh2, dx3, f"rms2_b_{l}")
    g["w_out"] = _matmul(s["mixed"], dx2, "tn", name=f"mm_gout_{l}", out_dtype=BF16)
    if after_ffn is not None:
        dx2 = after_ffn(dx2, g)
    dmixed = _matmul(dx2, p["w_out"], "nt", name=f"mm_dout_{l}")
    proj = s["proj"]
    du_pool, g["pool_w"], g["pool_b"], g["pool_scale"] = _pool_bwd(proj, dmixed, p["pool_w"], p["pool_b"], p["pool_scale"], f"pool_b_{l}")
    gw, gu, gqd, gkd, gat, gc = s["gdn"]
    dw, du, dqd, dkd, dat, dgl, dz, g["gdn_norm_w"] = _gdn_scan_bwd(
        gw, gu, gqd, gkd, gat, gc, s["states"], s["o"], proj, dmixed, p["gdn_norm_w"].reshape(1, GDN_DH), f"gscan_b_{l}")
    dqkv, dab, gal, gdt = _gdn_prep_bwd(s["qkv"], proj, s["alog"], s["dtb"], s["tinv"], dw, du, dqd, dkd, dat, dgl, f"gprep_b_{l}")
    g["gdn_a_log"], g["gdn_dt_bias"] = gal[0, :GDN_H], gdt[0, :GDN_H]
    dpre, g["gdn_conv_w"] = _gdn_conv_bwd(proj, dqkv, p["gdn_conv_w"], f"gconv_b_{l}")
    (dxr, dgr, g["lru_conv_w"], g["lru_conv_b"], g["lru_wa"], g["lru_ba"], g["lru_wx"], g["lru_bx"], g["lru_lambda"]) = _lru_bwd(
        proj, s["hl"], dmixed, p["lru_conv_w"], p["lru_conv_b"], p["lru_wa"], p["lru_ba"], p["lru_wx"], p["lru_bx"],
        p["lru_lambda"], f"lru_b_{l}")
    S = proj.shape[0]
    dproj = jnp.concatenate([dpre, dz, dxr, dgr, du_pool, dab, jnp.zeros((S, PCOLS - PAB - 128), BF16)], axis=1)
    g["w_in"] = _matmul(s["h1"], dproj, "tn", name=f"mm_gin_{l}", tn=768, tk=4096, out_dtype=BF16)
    if after_mix is not None:
        dproj = after_mix(dproj, g)
    dh1 = _matmul(dproj, p["w_in"], "nt", name=f"mm_din_{l}", tk=1792)
    dx, g["norm1_w"] = _rms_bwd(s["x"], p["norm1_w"], dh1, dx2, f"rms1_b_{l}")
    return dx, g


_IN_PERM = ((512, 3584), (3596, 5132), (0, 512), (3584, 3596))


def _w_in_to_proj(w):
    parts = [w[:, a:b] for a, b in _IN_PERM]
    return jnp.concatenate(parts + [jnp.zeros((w.shape[0], PCOLS - IN_COLS), w.dtype)], axis=1)


def _proj_to_w_in(g):
    return jnp.concatenate([g[:, PPOOL:PPOOL + 512], g[:, 0:3072], g[:, PAB:PAB + 12], g[:, 3072:PPOOL]], axis=1)


def _rows_to_mixed(w):
    return jnp.concatenate([w[512:], w[:512]], axis=0)


def _mixed_to_rows(g):
    return jnp.concatenate([g[1536:], g[:1536]], axis=0)


SMALL_SHARDED = ("gdn_conv_w", "lru_conv_w", "ffn_conv_w")
BIG = ("w_in", "w_out", "ffn_up", "ffn_down")
SMALL_REPLICATED = ("norm1_w", "pool_w", "pool_b", "pool_scale", "gdn_a_log", "gdn_dt_bias", "gdn_norm_w", "lru_conv_b",
                    "lru_wa", "lru_ba", "lru_wx", "lru_bx", "lru_lambda", "norm2_w")
WEIGHTS = ("norm1_w", "w_in", "pool_w", "pool_b", "pool_scale", "gdn_conv_w", "gdn_a_log", "gdn_dt_bias", "gdn_norm_w",
           "lru_conv_w", "lru_conv_b", "lru_wa", "lru_ba", "lru_wx", "lru_bx", "lru_lambda", "w_out", "norm2_w", "ffn_up",
           "ffn_conv_w", "ffn_down", "final_norm_w")
FLAT_COLS = 1024


def _pack(arrs):
    flat = jnp.concatenate([a.reshape(-1) for a in arrs])
    rows = -(-flat.shape[0] // (8 * FLAT_COLS)) * 8
    return jnp.pad(flat, (0, rows * FLAT_COLS - flat.shape[0])).reshape(rows, FLAT_COLS)


def _unpack(buf, like):
    flat = buf.reshape(-1)
    out, off = [], 0
    for a in like:
        size = 1
        for d in a.shape:
            size *= d
        out.append(flat[off:off + size].reshape(a.shape))
        off += size
    return out


def kernel(x, norm1_w, w_in, pool_w, pool_b, pool_scale, gdn_conv_w, gdn_a_log, gdn_dt_bias, gdn_norm_w, lru_conv_w, lru_conv_b, lru_wa, lru_ba, lru_wx, lru_bx, lru_lambda, w_out, norm2_w, ffn_up, ffn_conv_w, ffn_down, final_norm_w, loss_target, m_norm1_w, m_w_in, m_pool_w, m_pool_b, m_pool_scale, m_gdn_conv_w, m_gdn_a_log, m_gdn_dt_bias, m_gdn_norm_w, m_lru_conv_w, m_lru_conv_b, m_lru_wa, m_lru_ba, m_lru_wx, m_lru_bx, m_lru_lambda, m_w_out, m_norm2_w, m_ffn_up, m_ffn_conv_w, m_ffn_down, m_final_norm_w, v_norm1_w, v_w_in, v_pool_w, v_pool_b, v_pool_scale, v_gdn_conv_w, v_gdn_a_log, v_gdn_dt_bias, v_gdn_norm_w, v_lru_conv_w, v_lru_conv_b, v_lru_wa, v_lru_ba, v_lru_wx, v_lru_bx, v_lru_lambda, v_w_out, v_norm2_w, v_ffn_up, v_ffn_conv_w, v_ffn_down, v_final_norm_w):
    W = dict(norm1_w=norm1_w, w_in=w_in, pool_w=pool_w, pool_b=pool_b, pool_scale=pool_scale, gdn_conv_w=gdn_conv_w,
             gdn_a_log=gdn_a_log, gdn_dt_bias=gdn_dt_bias, gdn_norm_w=gdn_norm_w, lru_conv_w=lru_conv_w, lru_conv_b=lru_conv_b,
             lru_wa=lru_wa, lru_ba=lru_ba, lru_wx=lru_wx, lru_bx=lru_bx, lru_lambda=lru_lambda, w_out=w_out, norm2_w=norm2_w,
             ffn_up=ffn_up, ffn_conv_w=ffn_conv_w, ffn_down=ffn_down, final_norm_w=final_norm_w)
    M = dict(norm1_w=m_norm1_w, w_in=m_w_in, pool_w=m_pool_w, pool_b=m_pool_b, pool_scale=m_pool_scale, gdn_conv_w=m_gdn_conv_w,
             gdn_a_log=m_gdn_a_log, gdn_dt_bias=m_gdn_dt_bias, gdn_norm_w=m_gdn_norm_w, lru_conv_w=m_lru_conv_w,
             lru_conv_b=m_lru_conv_b, lru_wa=m_lru_wa, lru_ba=m_lru_ba, lru_wx=m_lru_wx, lru_bx=m_lru_bx, lru_lambda=m_lru_lambda,
             w_out=m_w_out, norm2_w=m_norm2_w, ffn_up=m_ffn_up, ffn_conv_w=m_ffn_conv_w, ffn_down=m_ffn_down,
             final_norm_w=m_final_norm_w)
    V = dict(norm1_w=v_norm1_w, w_in=v_w_in, pool_w=v_pool_w, pool_b=v_pool_b, pool_scale=v_pool_scale, gdn_conv_w=v_gdn_conv_w,
             gdn_a_log=v_gdn_a_log, gdn_dt_bias=v_gdn_dt_bias, gdn_norm_w=v_gdn_norm_w, lru_conv_w=v_lru_conv_w,
             lru_conv_b=v_lru_conv_b, lru_wa=v_lru_wa, lru_ba=v_lru_ba, lru_wx=v_lru_wx, lru_bx=v_lru_bx, lru_lambda=v_lru_lambda,
             w_out=v_w_out, norm2_w=v_norm2_w, ffn_up=v_ffn_up, ffn_conv_w=v_ffn_conv_w, ffn_down=v_ffn_down,
             final_norm_w=v_final_norm_w)
    S = x.shape[1]
    xs = x.reshape(S, D_MODEL)
    tgt = loss_target.reshape(S, D_MODEL)
    mx, my, mc = _place()
    shard = 2 * mx + my

    small_sh = jnp.concatenate([W[k].reshape(N_LAYERS, -1) for k in SMALL_SHARDED], axis=1)
    n_small = small_sh.shape[1]
    pad = -n_small % 1024
    small_sh = jnp.pad(small_sh, ((0, 0), (0, pad))).reshape(N_LAYERS, -1, 1024)

    def own_slots(l):
        out = []
        for w in [W[k][l].astype(BF16) for k in BIG] + [small_sh[l]]:
            buf = lax.dynamic_update_slice(lax.empty((4,) + w.shape, w.dtype), w[None], (shard,) + (0,) * w.ndim)
            out.append(buf.reshape(4, 2, w.shape[0] // 2, w.shape[1]))
        return out

    def whole(g):
        return g.reshape(4, 2 * g.shape[2], g.shape[3])

    def mixer_params(l, g_in, g_out, g_small):
        p = {k: W[k][l] for k in SMALL_REPLICATED}
        g_in = whole(g_in)
        p["w_in"] = _w_in_to_proj(jnp.transpose(g_in, (1, 0, 2)).reshape(g_in.shape[1], IN_COLS))
        p["w_out"] = _rows_to_mixed(whole(g_out).reshape(D_MODEL, D_MODEL))
        g_small = whole(g_small).reshape(4, -1)[:, :n_small]
        off = 0
        for k in SMALL_SHARDED:
            taps, width = W[k].shape[1], W[k].shape[2]
            piece = g_small[:, off:off + taps * width].reshape(4, taps, width)
            p[k] = jnp.transpose(piece, (1, 0, 2)).reshape(taps, 4 * width)
            off += taps * width
        return p

    def ffn_params(g_up, g_down):
        return dict(ffn_up=whole(g_up), ffn_down=whole(g_down).reshape(D_FF, D_MODEL))

    layers, saved = [None] * N_LAYERS, [None] * N_LAYERS
    s0 = own_slots(0)
    g_in0, g_out0, g_small0 = _gather_weights([s0[0], s0[1], s0[4]], "gather_weights")
    f_send, f_recv, ffn0, g_in0 = _copies_start(s0[2:4], g_in0, _gather_ici_copies, "gather_ffn0_ici_start")
    l_send, l_recv, bufs1, g_in0 = _copies_start(own_slots(1), g_in0, _gather_ici_copies, "gather_l1_ici_start")
    layers[0] = mixer_params(0, g_in0, g_out0, g_small0)
    stage = {}

    def mid(o):
        bufs = _copies_wait(f_send, f_recv, ffn0, o, _gather_ici_copies, "gather_ffn0_ici_wait")
        stage["ffn0"] = _copies_start(bufs, o, _gather_d2d_copies, "gather_ffn0_d2d_start")
        return stage["ffn0"][3]

    def ffn(h2):
        send, recv, bufs, _ = stage["ffn0"]
        layers[0].update(ffn_params(*_copies_wait(send, recv, bufs, h2, _gather_d2d_copies, "gather_ffn0_d2d_wait")))
        return h2

    def down(act):
        bufs = _copies_wait(l_send, l_recv, bufs1, act, _gather_ici_copies, "gather_l1_ici_wait")
        stage["l1"] = _copies_start(bufs, act, _gather_d2d_copies, "gather_l1_d2d_start")
        return stage["l1"][3]

    h, saved[0] = _layer_fwd(0, xs, layers[0], dict(mid=mid, ffn=ffn, down=down))
    send, recv, bufs, _ = stage["l1"]
    g1 = _copies_wait(send, recv, bufs, h, _gather_d2d_copies, "gather_l1_d2d_wait")
    layers[1] = {**mixer_params(1, g1[0], g1[1], g1[4]), **ffn_params(g1[2], g1[3])}
    h, saved[1] = _layer_fwd(1, h, layers[1])
    loss_part, dh, g_final = _loss_head(h, final_norm_w, tgt, "loss_head")

    def big_partials(g_layer, names=BIG):
        out = []
        for k in names:
            g = g_layer[k]
            if k == "w_in":
                g = _proj_to_w_in(g)
                g = jnp.transpose(g.reshape(g.shape[0], 4, IN_COLS // 4), (1, 0, 2))
            elif k == "w_out":
                g = _mixed_to_rows(g).reshape(4, D_MODEL // 4, D_MODEL)
            elif k == "ffn_down":
                g = g.reshape(4, D_FF // 4, D_MODEL)
            out.append(g.reshape(4, 2, g.shape[1] // 2, g.shape[2]))
        return out

    FFN, MIX = ("ffn_up", "ffn_down", "w_out"), ("w_in",)
    grads = [None] * N_LAYERS
    dh, grads[1] = _layer_bwd(1, dh, layers[1], saved[1])
    send1, recv1, part1, dh, lands1 = _rs_direct_start(big_partials(grads[1]), dh, "rs_direct_start_1")
    small_names = SMALL_REPLICATED + SMALL_SHARDED

    def small_of(l):
        return [grads[l][k].reshape(W[k].shape[1:]) if k in SMALL_REPLICATED else grads[l][k] for k in small_names]

    small_list1 = small_of(1) + [g_final.reshape(D_MODEL), loss_part[0, 0:1]]
    ar1 = _ar_start(_pack(small_list1), "ar_start_1")
    sent0 = []

    def after_ffn(dx2, g):
        send0, recv0, part0, dx2, lands0 = _rs_direct_start(big_partials(g, FFN), dx2, "rs_direct_start_0")
        sent0.extend([send0, recv0, part0, lands0])
        return dx2

    sentm = []

    def after_mix(dproj, g):
        sendm, recvm, partm, dproj, landsm = _rs_direct_start(big_partials(g, MIX), dproj, "rs_direct_start_0m")
        sentm.extend([sendm, recvm, partm, landsm])
        early = [g[k].reshape(W[k].shape[1:]) if k in SMALL_REPLICATED else g[k] for k in small_early]
        *ar, dproj = _ar_start(_pack(early), "ar_start_0a", thru=dproj)
        sentm.extend([early, ar])
        return dproj

    small_early = tuple(k for k in small_names if k != "norm1_w")
    dh, grads[0] = _layer_bwd(0, dh, layers[0], saved[0], after_ffn, after_mix)
    sendm, recvm, partm, landsm, small_list0a, ar0a = sentm
    small_list0b = [grads[0]["norm1_w"].reshape(D_MODEL)]
    ar0b = _ar_start(_pack(small_list0b), "ar_start_0b")

    part1, lands1 = _rs_direct_wait(send1, recv1, part1, lands1, dh, "rs_direct_wait_1")
    part0, lands0 = _rs_direct_wait(*sent0, dh, "rs_direct_wait_0")
    red = {k: _rs_direct_sum(g, ld, 1, f"rs_sum_1_{k}") for k, g, ld in zip(BIG, part1, lands1)}
    for k, g, ld in zip(FFN, part0, lands0):
        red[k] = _rs_direct_sum(g, ld, 0, f"rs_sum_0_{k}", into=red[k])
    G, DELTA, NM, NV = {}, {}, {}, {}

    def update_big(names, shared):
        for k, r in zip(names, shared):
            G[k] = r.reshape(N_LAYERS, 2 * r.shape[2], r.shape[3])
            DELTA[k], NM[k], NV[k] = _adamw(W[k], G[k], M[k], V[k], f"adam_{k}")

    update_big(FFN, _share_halves([red[k] for k in FFN], "rs_share_ffn"))
    partm, landsm = _rs_direct_wait(sendm, recvm, partm, landsm, DELTA[FFN[-1]], "rs_direct_wait_0m")
    for k, g, ld in zip(MIX, partm, landsm):
        red[k] = _rs_direct_sum(g, ld, 0, f"rs_sum_0_{k}", into=red[k])
    update_big(MIX, _share_halves([red[k] for k in MIX], "rs_share_mix"))
    grad_x = dh.reshape(x.shape)

    red1 = _unpack(_ar_sum(*_ar_wait(*ar1, G[MIX[-1]], "ar_wait_1"), "ar_sum_1"), small_list1)
    red0 = dict(zip(small_early, _unpack(_ar_sum(*_ar_wait(*ar0a, G[MIX[-1]], "ar_wait_0a"), "ar_sum_0a"), small_list0a)))
    red0["norm1_w"] = _unpack(_ar_sum(*_ar_wait(*ar0b, G[MIX[-1]], "ar_wait_0b"), "ar_sum_0b"), small_list0b)[0]
    small_g = {k: jnp.stack([red0[k], r1]) for k, r1 in zip(small_names, red1)}
    small_g["final_norm_w"] = red1[-2]
    loss = red1[-1][0]
    for k in SMALL_SHARDED:
        width = W[k].shape[2]
        small_g[k] = lax.dynamic_slice_in_dim(small_g[k], shard * width, width, axis=2)

    small_all = small_names + ("final_norm_w",)
    dl, nm, nv = _adamw(_pack([W[k] for k in small_all]), _pack([small_g[k] for k in small_all]),
                        _pack([M[k] for k in small_all]), _pack([V[k] for k in small_all]), "adam_small")
    like = [W[k] for k in small_all]
    for k, d_, m_, v_ in zip(small_all, _unpack(dl, like), _unpack(nm, like), _unpack(nv, like)):
        G[k], DELTA[k], NM[k], NV[k] = small_g[k], d_, m_, v_

    return (loss, grad_x, *[G[k] for k in WEIGHTS], *[DELTA[k] for k in WEIGHTS], *[NM[k] for k in WEIGHTS],
            *[NV[k] for k in WEIGHTS])
```

```python
import functools

import jax
import jax.numpy as jnp
from jax import lax
from jax.experimental import pallas as pl
from jax.experimental.pallas import tpu as pltpu

F32 = jnp.float32
BF16 = jnp.bfloat16
_MXU = jnp.bfloat16

D_MODEL = 2048
N_LAYERS = 2
POOL_W = 512
POOL_G = 4
POOL_GD = 128
POOL_WINDOWS = (2, 4, 8, 16)
POOL_HALO = 16
GDN_W = 768
GDN_H = 6
GDN_DH = 128
GDN_C = 64
LRU_W = 768
LRU_NB = 6
LRU_BD = 128
LRU_C = 8.0
D_FF = 6144
EPS = 1e-6
IN_COLS = 5132
HALO = 8

PQ, PK, PV, PZ, PXR, PGR, PPOOL, PAB, PCOLS = 0, 768, 1536, 2304, 3072, 3840, 4608, 5120, 5376
CB = 768

ADAM_LR = 0.001
ADAM_B1 = 0.9
ADAM_B2 = 0.999
ADAM_EPS = 1e-08
ADAM_WD = 0.01
ADAM_STEP = 10

VMEM_LIMIT = 56 * 1024 * 1024
MESH = pl.DeviceIdType.MESH
HBM = pl.BlockSpec(memory_space=pltpu.HBM)


def _cp(*sem):
    return pltpu.CompilerParams(dimension_semantics=sem, vmem_limit_bytes=VMEM_LIMIT)


def _dg(a, b, ta, tb):
    dims = (((0 if ta else 1,), (1 if tb else 0,)), ((), ()))
    return lax.dot_general(a, b, dims, preferred_element_type=F32)


def _split2(a):
    hi = a.astype(BF16)
    lo = (a - hi.astype(F32)).astype(BF16)
    return hi, lo


def _mm_raw(a, b, ta, tb, hi):
    if _MXU == F32:
        return _dg(a, b, ta, tb)
    if not hi:
        return _dg(a.astype(_MXU), b.astype(_MXU), ta, tb)
    a1, a2 = _split2(a)
    b1, b2 = _split2(b)
    return _dg(a1, b1, ta, tb) + (_dg(a1, b2, ta, tb) + _dg(a2, b1, ta, tb))


@functools.partial(jax.custom_vjp, nondiff_argnums=(2, 3, 4))
def _mm(a, b, ta=False, tb=False, hi=False):
    return _mm_raw(a, b, ta, tb, hi)


def _mm_fwd(a, b, ta, tb, hi):
    return _mm_raw(a, b, ta, tb, hi), (a, b)


def _mm_bwd(ta, tb, hi, res, dc):
    a, b = res
    da = _mm(b, dc, tb, True, hi) if ta else _mm(dc, b, False, not tb, hi)
    db = _mm(dc, a, True, ta, hi) if tb else _mm(a, dc, not ta, False, hi)
    return da, db


_mm.defvjp(_mm_fwd, _mm_bwd)


def _mm01(m01, x):
    if _MXU == F32:
        return _dg(m01, x, False, False)
    m = m01.astype(BF16)
    x1 = x.astype(BF16)
    r = x - x1.astype(F32)
    x2 = r.astype(BF16)
    x3 = (r - x2.astype(F32)).astype(BF16)
    return _dg(m, x1, False, False) + (_dg(m, x2, False, False) + _dg(m, x3, False, False))


def _down(x, k):
    return x if k == 0 else pltpu.roll(x, k, 0)


def _up(x, k):
    return x if k == 0 else pltpu.roll(x, x.shape[0] - k, 0)


def _rows(shape):
    return lax.broadcasted_iota(jnp.int32, shape, 0)


def _lanes(shape):
    return lax.broadcasted_iota(jnp.int32, shape, 1)


def _matmul(a, b, mode, *, name, res=None, tm=1024, tn=1024, tk=2048, b_split=False, o_split=0, out_dtype=F32):
    ta, tb = mode == "tn", mode == "nt"
    a_split = a.ndim == 3
    if a_split:
        assert not ta
        M, K = a.shape[1], a.shape[0] * a.shape[2]
        tk = min(tk, a.shape[2])
    elif ta:
        K, M = a.shape
    else:
        M, K = a.shape
    if b_split:
        ns = b.shape[0]
        N = b.shape[1] if tb else ns * b.shape[2]
    else:
        N = b.shape[0] if tb else b.shape[1]
    tm, tn, tk = min(tm, M), min(tn, N), min(tk, K)
    if b_split:
        per = b.shape[2]
        if tb:
            tk = min(tk, per)
        else:
            tn = min(tn, per)
    if o_split:
        tn = min(tn, N // o_split)
    assert M % tm == 0 and N % tn == 0 and K % tk == 0, (name, M, N, K, tm, tn, tk)
    nk = K // tk
    if a_split:
        ka = a.shape[2] // tk
        a_spec = pl.BlockSpec((None, tm, tk), lambda i, j, k: (k // ka, i, k % ka))
    else:
        a_spec = pl.BlockSpec((tk, tm), lambda i, j, k: (k, i)) if ta else pl.BlockSpec((tm, tk), lambda i, j, k: (i, k))
    if not b_split:
        b_spec = pl.BlockSpec((tn, tk), lambda i, j, k: (j, k)) if tb else pl.BlockSpec((tk, tn), lambda i, j, k: (k, j))
    elif tb:
        kb = per // tk
        b_spec = pl.BlockSpec((None, tn, tk), lambda i, j, k: (k // kb, j, k % kb))
    else:
        nb = per // tn
        b_spec = pl.BlockSpec((None, tk, tn), lambda i, j, k: (j // nb, k, j % nb))
    if o_split:
        ob = (N // o_split) // tn
        out_shape = jax.ShapeDtypeStruct((o_split, M, N // o_split), out_dtype)
        o_spec = pl.BlockSpec((None, tm, tn), lambda i, j, k: (j // ob, i, j % ob))
    else:
        out_shape = jax.ShapeDtypeStruct((M, N), out_dtype)
        o_spec = pl.BlockSpec((tm, tn), lambda i, j, k: (i, j))
    in_specs = [a_spec, b_spec]
    args = [a, b]
    if res is not None:
        in_specs.append(pl.BlockSpec((tm, tn), lambda i, j, k: (i, j)))
        args.append(res)
    use_acc = nk > 1 and out_dtype != F32

    def body(*refs):
        a_ref, b_ref = refs[0], refs[1]
        o_ref = refs[2 + (res is not None)]
        acc_ref = refs[-1] if use_acc else o_ref
        p = _dg(a_ref[...].astype(_MXU), b_ref[...].astype(_MXU), ta, tb)
        first = p + refs[2][...] if res is not None else p
        if nk == 1:
            o_ref[...] = first.astype(o_ref.dtype)
        else:
            k = pl.program_id(2)

            @pl.when(k == 0)
            def _():
                acc_ref[...] = first

            @pl.when(k > 0)
            def _():
                acc_ref[...] += p

            if use_acc:
                @pl.when(k == nk - 1)
                def _():
                    o_ref[...] = acc_ref[...].astype(o_ref.dtype)

    return pl.pallas_call(
        body, name=name, grid=(M // tm, N // tn, nk), in_specs=in_specs, out_specs=o_spec, out_shape=out_shape,
        scratch_shapes=[pltpu.VMEM((tm, tn), F32)] if use_acc else [],
        compiler_params=_cp("parallel", "parallel", "arbitrary"),
    )(*args)


def _rms(x, w):
    return x * lax.rsqrt(jnp.mean(x * x, axis=-1, keepdims=True) + EPS) * w


def _row_tile(S, t=512):
    t = min(t, S)
    assert S % t == 0
    return t


def _rms_fwd(x, w, name):
    S, D = x.shape
    T = _row_tile(S)

    def body(x_ref, w_ref, o_ref):
        o_ref[...] = _rms(x_ref[...], w_ref[...]).astype(o_ref.dtype)

    return pl.pallas_call(
        body, name=name, grid=(S // T,),
        in_specs=[pl.BlockSpec((T, D), lambda i: (i, 0)), pl.BlockSpec((1, D), lambda i: (0, 0))],
        out_specs=pl.BlockSpec((T, D), lambda i: (i, 0)), out_shape=jax.ShapeDtypeStruct((S, D), BF16),
        compiler_params=_cp("parallel"),
    )(x, w.reshape(1, D))


def _rms_bwd(x, w, dh, dres, name):
    S, D = x.shape
    T = _row_tile(S)

    def body(x_ref, w_ref, dh_ref, dr_ref, dx_ref, gw_ref):
        _, vjp = jax.vjp(_rms, x_ref[...], w_ref[...])
        dx, dw = vjp(dh_ref[...])
        dx_ref[...] = dr_ref[...] + dx

        @pl.when(pl.program_id(0) == 0)
        def _():
            gw_ref[...] = jnp.zeros_like(gw_ref)

        gw_ref[...] += dw

    row = pl.BlockSpec((T, D), lambda i: (i, 0))
    vec = pl.BlockSpec((1, D), lambda i: (0, 0))
    return pl.pallas_call(
        body, name=name, grid=(S // T,), in_specs=[row, vec, row, row], out_specs=[row, vec],
        out_shape=[jax.ShapeDtypeStruct((S, D), F32), jax.ShapeDtypeStruct((1, D), F32)],
        compiler_params=_cp("arbitrary"),
    )(x, w.reshape(1, D), dh, dres)


def _loss_head(x, w, tgt, name):
    S, D = x.shape
    T = _row_tile(S)

    def body(x_ref, w_ref, t_ref, l_ref, dx_ref, gw_ref):
        y, vjp = jax.vjp(_rms, x_ref[...], w_ref[...])
        err = y - t_ref[...]
        part = 0.5 * jnp.sum(jnp.mean(err * err, axis=-1, keepdims=True), axis=0, keepdims=True)
        dx, dw = vjp(err * (1.0 / D))
        dx_ref[...] = dx

        @pl.when(pl.program_id(0) == 0)
        def _():
            gw_ref[...] = jnp.zeros_like(gw_ref)
            l_ref[...] = jnp.zeros_like(l_ref)

        gw_ref[...] += dw
        l_ref[...] += jnp.broadcast_to(part, l_ref.shape)

    row = pl.BlockSpec((T, D), lambda i: (i, 0))
    vec = pl.BlockSpec((1, D), lambda i: (0, 0))
    return pl.pallas_call(
        body, name=name, grid=(S // T,), in_specs=[row, vec, row],
        out_specs=[pl.BlockSpec((8, 128), lambda i: (0, 0)), row, vec],
        out_shape=[jax.ShapeDtypeStruct((8, 128), F32), jax.ShapeDtypeStruct((S, D), F32), jax.ShapeDtypeStruct((1, D), F32)],
        compiler_params=_cp("arbitrary"),
    )(x, w.reshape(1, D), tgt)


def _by_group(shape, vals):
    g = _lanes(shape) // POOL_GD
    out = vals[-1]
    for k in range(len(vals) - 2, -1, -1):
        out = jnp.where(g == k, vals[k], out)
    return out


def _pool_d(prev, u, t0):
    ext = jnp.concatenate([prev, u], axis=0)
    s2 = ext + _down(ext, 1)
    s4 = s2 + _down(s2, 2)
    s8 = s4 + _down(s4, 4)
    s16 = s8 + _down(s8, 8)
    ssel = _by_group(ext.shape, [s2, s4, s8, s16])[POOL_HALO:]
    win = _by_group(u.shape, [jnp.int32(w) for w in POOL_WINDOWS])
    cnt = jnp.minimum(t0 + _rows(u.shape) + 1, win).astype(F32)
    return ssel / cnt - u


def _pool_lin(d, w_ref, b):
    ys = [_mm(d[:, g * POOL_GD:(g + 1) * POOL_GD], w_ref[g]) for g in range(POOL_G)]
    return jnp.concatenate(ys, axis=1) + b


def _pool_fwd(proj, w, b, scale, name):
    S = proj.shape[0]
    T = _row_tile(S)
    r = T // POOL_HALO
    cb = PPOOL // POOL_W

    def body(u_ref, up_ref, w_ref, b_ref, sc_ref, y_ref):
        i = pl.program_id(0)
        prev = jnp.where(i > 0, up_ref[...], 0.0)
        d = _pool_d(prev, u_ref[...], i * T)
        y_ref[...] = _pool_lin(d, w_ref, b_ref[...]) * sc_ref[...]

    vec = pl.BlockSpec((1, POOL_W), lambda i: (0, 0))
    return pl.pallas_call(
        body, name=name, grid=(S // T,),
        in_specs=[pl.BlockSpec((T, POOL_W), lambda i: (i, cb)),
                  pl.BlockSpec((POOL_HALO, POOL_W), lambda i: (jnp.maximum(i * r - 1, 0), cb)),
                  pl.BlockSpec((POOL_G, POOL_GD, POOL_GD), lambda i: (0, 0, 0)), vec, vec],
        out_specs=pl.BlockSpec((T, POOL_W), lambda i: (i, 0)), out_shape=jax.ShapeDtypeStruct((S, POOL_W), F32),
        compiler_params=_cp("parallel"),
    )(proj, proj, w, b.reshape(1, POOL_W), scale.reshape(1, POOL_W))


def _pool_bwd(proj, dmixed, w, b, scale, name):
    S = proj.shape[0]
    T = _row_tile(S)
    n = S // T
    r = T // POOL_HALO
    cb = PPOOL // POOL_W
    mb = 1536 // POOL_W

    def body(u_ref, up_ref, dy_ref, dyn_ref, w_ref, b_ref, sc_ref, du_ref, gw_ref, gb_ref, gs_ref):
        i = pl.program_id(0)
        sc = sc_ref[...]
        dy = dy_ref[...]
        dy_ext = jnp.concatenate([dy, jnp.where(i < n - 1, dyn_ref[...], 0.0)], axis=0)
        dyl = dy_ext * sc
        dd = jnp.concatenate(
            [_mm(dyl[:, g * POOL_GD:(g + 1) * POOL_GD], w_ref[g], False, True) for g in range(POOL_G)], axis=1)
        t_ext = i * T + _rows(dd.shape)
        win = _by_group(dd.shape, [jnp.int32(v) for v in POOL_WINDOWS])
        cnt = jnp.minimum(t_ext + 1, win).astype(F32)
        e = jnp.where(t_ext < S, dd / cnt, 0.0)
        f2 = e + _up(e, 1)
        f4 = f2 + _up(f2, 2)
        f8 = f4 + _up(f4, 4)
        f16 = f8 + _up(f8, 8)
        du = (_by_group(dd.shape, [f2, f4, f8, f16]) - dd)[:T]
        du_ref[...] = du.astype(du_ref.dtype)

        prev = jnp.where(i > 0, up_ref[...], 0.0)
        d = _pool_d(prev, u_ref[...], i * T)
        ylin = _pool_lin(d, w_ref, b_ref[...])
        dyl_m = dy * sc

        @pl.when(i == 0)
        def _():
            gw_ref[...] = jnp.zeros_like(gw_ref)
            gb_ref[...] = jnp.zeros_like(gb_ref)
            gs_ref[...] = jnp.zeros_like(gs_ref)

        gs_ref[...] += jnp.sum(dy * ylin, axis=0, keepdims=True)
        gb_ref[...] += jnp.sum(dyl_m, axis=0, keepdims=True)
        for g in range(POOL_G):
            sl = slice(g * POOL_GD, (g + 1) * POOL_GD)
            gw_ref[g] += _mm(d[:, sl], dyl_m[:, sl], True, False)

    vec = pl.BlockSpec((1, POOL_W), lambda i: (0, 0))
    wsp = pl.BlockSpec((POOL_G, POOL_GD, POOL_GD), lambda i: (0, 0, 0))
    nh = S // POOL_HALO
    return pl.pallas_call(
        body, name=name, grid=(n,),
        in_specs=[pl.BlockSpec((T, POOL_W), lambda i: (i, cb)),
                  pl.BlockSpec((POOL_HALO, POOL_W), lambda i: (jnp.maximum(i * r - 1, 0), cb)),
                  pl.BlockSpec((T, POOL_W), lambda i: (i, mb)),
                  pl.BlockSpec((POOL_HALO, POOL_W), lambda i: (jnp.minimum((i + 1) * r, nh - 1), mb)),
                  wsp, vec, vec],
        out_specs=[pl.BlockSpec((T, POOL_W), lambda i: (i, 0)), wsp, vec, vec],
        out_shape=[jax.ShapeDtypeStruct((S, POOL_W), BF16), jax.ShapeDtypeStruct((POOL_G, POOL_GD, POOL_GD), F32),
                   jax.ShapeDtypeStruct((1, POOL_W), F32), jax.ShapeDtypeStruct((1, POOL_W), F32)],
        compiler_params=_cp("arbitrary"),
    )(proj, proj, dmixed, dmixed, w, b.reshape(1, POOL_W), scale.reshape(1, POOL_W))


def _conv_rows(ext, w_ref, taps):
    acc = w_ref[taps - 1:taps, :] * ext
    for k in range(1, taps):
        acc = acc + w_ref[taps - 1 - k:taps - k, :] * _down(ext, k)
    return acc


def _conv_t_rows(dc, w_ref, taps):
    acc = w_ref[taps - 1:taps, :] * dc
    for k in range(1, taps):
        acc = acc + w_ref[taps - 1 - k:taps - k, :] * _up(dc, k)
    return acc


def _conv_specs(T, S, ncb0, with_next):
    r = T // HALO
    nh = S // HALO
    main = pl.BlockSpec((T, CB), lambda j, i: (i, j + ncb0))
    prev = pl.BlockSpec((HALO, CB), lambda j, i: (jnp.maximum(i * r - 1, 0), j + ncb0))
    nxt = pl.BlockSpec((HALO, CB), lambda j, i: (jnp.minimum((i + 1) * r, nh - 1), j + ncb0))
    return (main, prev, nxt) if with_next else (main, prev)


def _gdn_conv_fwd(proj, w, name):
    S = proj.shape[0]
    T = _row_tile(S)
    taps = w.shape[0]
    ncb = 3 * GDN_W // CB

    def body(x_ref, xp_ref, w_ref, o_ref):
        i = pl.program_id(1)
        ext = jnp.concatenate([jnp.where(i > 0, xp_ref[...], 0.0), x_ref[...]], axis=0)
        o_ref[...] = jax.nn.silu(_conv_rows(ext, w_ref, taps)[HALO:])

    main, prev = _conv_specs(T, S, PQ // CB, False)
    return pl.pallas_call(
        body, name=name, grid=(ncb, S // T),
        in_specs=[main, prev, pl.BlockSpec((taps, CB), lambda j, i: (0, j))],
        out_specs=pl.BlockSpec((T, CB), lambda j, i: (i, j)), out_shape=jax.ShapeDtypeStruct((S, 3 * GDN_W), F32),
        compiler_params=_cp("parallel", "parallel"),
    )(proj, proj, w)


def _gdn_conv_bwd(proj, dact, w, name):
    S = proj.shape[0]
    T = _row_tile(S)
    n = S // T
    taps = w.shape[0]
    ncb = 3 * GDN_W // CB

    def body(x_ref, xp_ref, xn_ref, d_ref, dn_ref, w_ref, dx_ref, gw_ref):
        i = pl.program_id(1)
        last = i == n - 1
        ext = jnp.concatenate([jnp.where(i > 0, xp_ref[...], 0.0), x_ref[...], jnp.where(last, 0.0, xn_ref[...])], axis=0)
        c = _conv_rows(ext, w_ref, taps)[HALO:]
        d_ext = jnp.concatenate([d_ref[...], jnp.where(last, 0.0, dn_ref[...])], axis=0)
        _, vjp = jax.vjp(jax.nn.silu, c)
        dc = vjp(d_ext)[0]
        dx_ref[...] = _conv_t_rows(dc, w_ref, taps)[:T].astype(dx_ref.dtype)

        @pl.when(i == 0)
        def _():
            gw_ref[...] = jnp.zeros_like(gw_ref)

        dcm = dc[:T]
        for k in range(taps):
            gw_ref[taps - 1 - k:taps - k, :] += jnp.sum(dcm * _down(ext, k)[HALO:HALO + T], axis=0, keepdims=True)

    main, prev, nxt = _conv_specs(T, S, PQ // CB, True)
    dmain, _, dnxt = _conv_specs(T, S, 0, True)
    wsp = pl.BlockSpec((taps, CB), lambda j, i: (0, j))
    return pl.pallas_call(
        body, name=name, grid=(ncb, n), in_specs=[main, prev, nxt, dmain, dnxt, wsp],
        out_specs=[pl.BlockSpec((T, CB), lambda j, i: (i, j)), wsp],
        out_shape=[jax.ShapeDtypeStruct((S, 3 * GDN_W), BF16), jax.ShapeDtypeStruct((taps, 3 * GDN_W), F32)],
        compiler_params=_cp("parallel", "arbitrary"),
    )(proj, proj, proj, dact, dact, w)


def _ffn_act_fwd(up, w, name):
    S = up.shape[0]
    T = _row_tile(S)
    taps = w.shape[0]
    ncb = D_FF // CB

    def body(g_ref, gp_ref, v_ref, w_ref, o_ref):
        i = pl.program_id(1)
        ext = jnp.concatenate([jnp.where(i > 0, gp_ref[...], 0.0), g_ref[...]], axis=0)
        c = _conv_rows(ext, w_ref, taps)[HALO:]
        o_ref[...] = (jax.nn.gelu(c) * v_ref[...]).astype(o_ref.dtype)

    main, prev = _conv_specs(T, S, 0, False)
    val = pl.BlockSpec((T, CB), lambda j, i: (i, j + ncb))
    return pl.pallas_call(
        body, name=name, grid=(ncb, S // T),
        in_specs=[main, prev, val, pl.BlockSpec((taps, CB), lambda j, i: (0, j))],
        out_specs=pl.BlockSpec((T, CB), lambda j, i: (i, j)), out_shape=jax.ShapeDtypeStruct((S, D_FF), BF16),
        compiler_params=_cp("parallel", "parallel"),
    )(up, up, up, w)


def _ffn_act_bwd(up, dact, w, name):
    S = up.shape[0]
    T = _row_tile(S)
    n = S // T
    taps = w.shape[0]
    ncb = D_FF // CB

    def body(g_ref, gp_ref, gn_ref, v_ref, vn_ref, d_ref, dn_ref, w_ref, dup_ref, gw_ref):
        i = pl.program_id(1)
        last = i == n - 1
        ext = jnp.concatenate([jnp.where(i > 0, gp_ref[...], 0.0), g_ref[...], jnp.where(last, 0.0, gn_ref[...])], axis=0)
        c = _conv_rows(ext, w_ref, taps)[HALO:]
        v_ext = jnp.concatenate([v_ref[...], jnp.where(last, 0.0, vn_ref[...])], axis=0)
        d_ext = jnp.concatenate([d_ref[...], jnp.where(last, 0.0, dn_ref[...])], axis=0)
        gl, vjp = jax.vjp(jax.nn.gelu, c)
        dup_ref[1] = (d_ext * gl)[:T].astype(dup_ref.dtype)
        dc = vjp(d_ext * v_ext)[0]
        dup_ref[0] = _conv_t_rows(dc, w_ref, taps)[:T].astype(dup_ref.dtype)

        @pl.when(i == 0)
        def _():
            gw_ref[...] = jnp.zeros_like(gw_ref)

        dcm = dc[:T]
        for k in range(taps):
            gw_ref[taps - 1 - k:taps - k, :] += jnp.sum(dcm * _down(ext, k)[HALO:HALO + T], axis=0, keepdims=True)

    main, prev, nxt = _conv_specs(T, S, 0, True)
    vmain, _, vnxt = _conv_specs(T, S, ncb, True)
    wsp = pl.BlockSpec((taps, CB), lambda j, i: (0, j))
    osp = pl.BlockSpec((2, T, CB), lambda j, i: (0, i, j))
    return pl.pallas_call(
        body, name=name, grid=(ncb, n), in_specs=[main, prev, nxt, vmain, vnxt, main, nxt, wsp],
        out_specs=[osp, wsp],
        out_shape=[jax.ShapeDtypeStruct((2, S, D_FF), BF16), jax.ShapeDtypeStruct((taps, D_FF), F32)],
        compiler_params=_cp("parallel", "arbitrary"),
    )(up, up, up, up, up, dact, dact, w)


def _tri_masks():
    r = _rows((GDN_C, GDN_C))
    c = _lanes((GDN_C, GDN_C))
    return r >= c, r > c


def _each(fn, *cols):
    return tuple(fn(*args) for args in zip(*cols))


def _tri_inv_raw(lows):
    r = _rows(lows[0].shape)
    c = _lanes(lows[0].shape)
    eye = jnp.where(r == c, 1.0, 0.0)
    ps = _each(lambda low: eye - low, lows)
    lps = lows
    for _ in range(5):
        lps = _each(lambda lp: _mm(lp, lp, False, False, True), lps)
        ps = _each(lambda p, lp: p + _mm(p, lp, False, False, True), ps, lps)
    return ps


@jax.custom_vjp
def _tri_inv(lows):
    return _tri_inv_raw(lows)


def _tri_inv_fwd(lows):
    ts = _tri_inv_raw(lows)
    return ts, ts


def _tri_inv_bwd(ts, dts):
    inner = _each(lambda t, dt: _mm(t, dt, True, False, True), ts, dts)
    return (_each(lambda m, t: -_mm(m, t, False, True, True), inner, ts),)


_tri_inv.defvjp(_tri_inv_fwd, _tri_inv_bwd)


@jax.custom_vjp
def _tri_inv_given(lows, ts):
    return ts


def _tri_inv_given_fwd(lows, ts):
    return ts, ts


def _tri_inv_given_bwd(ts, dts):
    return _tri_inv_bwd(ts, dts)[0], _each(jnp.zeros_like, ts)


_tri_inv_given.defvjp(_tri_inv_given_fwd, _tri_inv_given_bwd)


def _gdn_glog(a_col, alog, dtb):
    return -jnp.exp(alog) * jax.nn.softplus(a_col + dtb)


def _decay_operand():
    r = _rows((GDN_C, 2 * GDN_C))
    c = _lanes((GDN_C, 2 * GDN_C))
    return jnp.where((c >= GDN_C) | (r > c), 1.0, 0.0)


def _gdn_decay(glog):
    causal, _ = _tri_masks()
    res = _mm01(jnp.where(causal, 1.0, 0.0), glog * _decay_operand())
    return res[:, GDN_C:GDN_C + 1], res[:, :GDN_C]


def _gdn_decay_bwd(dgcol, dd):
    r = _rows((GDN_C, GDN_C))
    c = _lanes((GDN_C, GDN_C))
    dres = jnp.concatenate([dd, jnp.where(c == 0, dgcol, 0.0)], axis=1)
    dx = _mm01(jnp.where(r <= c, 1.0, 0.0), dres)
    return jnp.sum(dx * _decay_operand(), axis=1, keepdims=True)


def _gdn_chunk(qa, ka, va, bt_col, gcol, dmat, t_saved=None):
    causal, strict = _tri_masks()
    qn = _each(lambda q: q * lax.rsqrt(jnp.sum(q * q, axis=-1, keepdims=True) + EPS) * (GDN_DH ** -0.5), qa)
    kn = _each(lambda k: k * lax.rsqrt(jnp.sum(k * k, axis=-1, keepdims=True) + EPS), ka)
    beta = _each(jax.nn.sigmoid, bt_col)
    eg = _each(jnp.exp, gcol)
    decay = _each(lambda d: jnp.where(causal, jnp.exp(d), 0.0), dmat)
    kk = _each(lambda k: _mm(k, k, False, True), kn)
    low = _each(lambda b, m, d: jnp.where(strict, b * m * d, 0.0), beta, kk, decay)
    t = _tri_inv(low) if t_saved is None else _tri_inv_given(low, t_saved)
    w = _each(lambda t_, k, b, e: _mm(t_, k * (b * e), False, False, True), t, kn, beta, eg)
    u = _each(lambda t_, v, b: _mm(t_, v * b, False, False, True), t, va, beta)
    attn = _each(lambda q, k, d: _mm(q, k, False, True) * d, qn, kn, decay)
    last = _rows(gcol[0].shape) == GDN_C - 1
    g_last = _each(lambda g: jnp.sum(jnp.where(last, g, 0.0), axis=0, keepdims=True), gcol)
    qd = _each(lambda q, e: q * e, qn, eg)
    kd = _each(lambda k, gl, g: k * jnp.exp(gl - g), kn, g_last, gcol)
    return (w, u, qd, kd, attn), t


def _gdn_step(state, w, u, qd, kd, attn, egl):
    v_new = _each(lambda u_, w_, s: u_ - _mm(w_, s), u, w, state)
    o_state = _each(_mm, qd, state)
    o = _each(lambda os, a, v: os + _mm(a, v), o_state, attn, v_new)
    new = _each(lambda s, e, k, v: s * e + _mm(k, v, True, False), state, egl, kd, v_new)
    return o, new


def _heads(ref, base=0, width=GDN_DH):
    return tuple(ref[:, (base + h) * GDN_DH:(base + h) * GDN_DH + width] for h in range(GDN_H))


def _cols(a, base):
    return tuple(a[:, base + h:base + h + 1] for h in range(GDN_H))


def _gated_norm(o, z, nw):
    return o * lax.rsqrt(jnp.mean(o * o, axis=-1, keepdims=True) + EPS) * nw * jax.nn.silu(z)


def _hsl(h):
    return slice(h * GDN_DH, (h + 1) * GDN_DH)


def _pad_lanes(a, width=GDN_DH):
    return jnp.concatenate([a, jnp.zeros((a.shape[0], width - a.shape[1]), a.dtype)], axis=1)


def _gdn_prep(qkv, proj, alog, dtb, name):
    S = qkv.shape[0]
    N = S // GDN_C

    def body(qkv_ref, ab_ref, al_ref, dt_ref, w_ref, u_ref, qd_ref, kd_ref, at_ref, ti_ref, gc_ref):
        ab = ab_ref[...]
        glog = _each(_gdn_glog, _cols(ab, 0), _cols(al_ref[...], 0), _cols(dt_ref[...], 0))
        dec = _each(_gdn_decay, glog)
        gcol, dmat = _each(lambda d: d[0], dec), _each(lambda d: d[1], dec)
        (w, u, qd, kd, attn), tinv = _gdn_chunk(_heads(qkv_ref), _heads(qkv_ref, GDN_H), _heads(qkv_ref, 2 * GDN_H),
                                                _cols(ab, GDN_H), gcol, dmat)
        gc = jnp.zeros((GDN_C, 128), F32)
        for h in range(GDN_H):
            w_ref[:, _hsl(h)] = w[h]
            u_ref[:, _hsl(h)] = u[h]
            qd_ref[:, _hsl(h)] = qd[h]
            kd_ref[:, _hsl(h)] = kd[h]
            at_ref[:, _hsl(h)] = _pad_lanes(attn[h])
            ti_ref[:, _hsl(h)] = _pad_lanes(tinv[h])
            gc = jnp.where(_lanes(gc.shape) == h, gcol[h], gc)
        gc_ref[...] = gc

    vec = pl.BlockSpec((1, 128), lambda i: (0, 0))
    hsp = pl.BlockSpec((GDN_C, GDN_W), lambda i: (i, 0))
    hshape = jax.ShapeDtypeStruct((S, GDN_W), F32)
    return pl.pallas_call(
        body, name=name, grid=(N,),
        in_specs=[pl.BlockSpec((GDN_C, 3 * GDN_W), lambda i: (i, 0)), pl.BlockSpec((GDN_C, 128), lambda i: (i, PAB // 128)), vec, vec],
        out_specs=[hsp] * 6 + [pl.BlockSpec((GDN_C, 128), lambda i: (i, 0))],
        out_shape=[hshape] * 6 + [jax.ShapeDtypeStruct((S, 128), F32)],
        compiler_params=_cp("parallel"),
    )(qkv, proj, alog, dtb)


def _gdn_scan(w, u, qd, kd, attn, gc, name):
    S = w.shape[0]
    N = S // GDN_C

    def body(w_ref, u_ref, qd_ref, kd_ref, at_ref, gc_ref, o_ref, st_ref, s_ref):
        @pl.when(pl.program_id(0) == 0)
        def _():
            s_ref[...] = jnp.zeros_like(s_ref)

        state = tuple(s_ref[_hsl(h), :] for h in range(GDN_H))
        egl = _each(jnp.exp, _cols(gc_ref[GDN_C - 1:GDN_C, :], 0))
        o, new = _gdn_step(state, _heads(w_ref), _heads(u_ref), _heads(qd_ref), _heads(kd_ref),
                           _heads(at_ref, width=GDN_C), egl)
        for h in range(GDN_H):
            st_ref[_hsl(h), :] = state[h]
            o_ref[:, _hsl(h)] = o[h]
            s_ref[_hsl(h), :] = new[h]

    hsp = pl.BlockSpec((GDN_C, GDN_W), lambda i: (i, 0))
    return pl.pallas_call(
        body, name=name, grid=(N,),
        in_specs=[hsp] * 5 + [pl.BlockSpec((GDN_C, 128), lambda i: (i, 0))],
        out_specs=[hsp, pl.BlockSpec((None, GDN_W, GDN_DH), lambda i: (i, 0, 0))],
        out_shape=[jax.ShapeDtypeStruct((S, GDN_W), F32), jax.ShapeDtypeStruct((N, GDN_W, GDN_DH), F32)],
        scratch_shapes=[pltpu.VMEM((GDN_W, GDN_DH), F32)],
        compiler_params=_cp("arbitrary"),
    )(w, u, qd, kd, attn, gc)


def _gdn_scan_bwd(w, u, qd, kd, attn, gc, states, o, proj, dmixed, nw, name):
    S = w.shape[0]
    N = S // GDN_C

    def body(w_ref, u_ref, qd_ref, kd_ref, at_ref, gc_ref, st_ref, o_ref, z_ref, dm_ref, nw_ref,
             dw_ref, du_ref, dqd_ref, dkd_ref, dat_ref, dgl_ref, dz_ref, gnw_ref, ds_ref):
        @pl.when(pl.program_id(0) == 0)
        def _():
            ds_ref[...] = jnp.zeros_like(ds_ref)
            gnw_ref[...] = jnp.zeros_like(gnw_ref)

        nw = nw_ref[...]
        _, vjp_n = jax.vjp(lambda o, z, w_: _each(lambda a, b: _gated_norm(a, b, w_), o, z), _heads(o_ref), _heads(z_ref), nw)
        do, dz, dnw = vjp_n(_heads(dm_ref))
        state = tuple(st_ref[_hsl(h), :] for h in range(GDN_H))
        egl = _each(jnp.exp, _cols(gc_ref[GDN_C - 1:GDN_C, :], 0))
        _, vjp_s = jax.vjp(_gdn_step, state, _heads(w_ref), _heads(u_ref), _heads(qd_ref), _heads(kd_ref),
                           _heads(at_ref, width=GDN_C), egl)
        ds, dw, du, dqd, dkd, dat, degl = vjp_s((do, tuple(ds_ref[_hsl(h), :] for h in range(GDN_H))))
        dgl = jnp.zeros((8, 128), F32)
        for h in range(GDN_H):
            dz_ref[:, _hsl(h)] = dz[h].astype(dz_ref.dtype)
            ds_ref[_hsl(h), :] = ds[h]
            dw_ref[:, _hsl(h)] = dw[h]
            du_ref[:, _hsl(h)] = du[h]
            dqd_ref[:, _hsl(h)] = dqd[h]
            dkd_ref[:, _hsl(h)] = dkd[h]
            dat_ref[:, _hsl(h)] = _pad_lanes(dat[h])
            dgl = jnp.where(_lanes(dgl.shape) == h, degl[h] * egl[h], dgl)
        dgl_ref[...] = dgl
        gnw_ref[...] += dnw

    rev = lambda i: (N - 1 - i, 0)
    hsp = pl.BlockSpec((GDN_C, GDN_W), rev)
    gsp = pl.BlockSpec((GDN_C, 128), rev)
    vec = pl.BlockSpec((1, GDN_DH), lambda i: (0, 0))
    hshape = jax.ShapeDtypeStruct((S, GDN_W), F32)
    return pl.pallas_call(
        body, name=name, grid=(N,),
        in_specs=[hsp] * 5 + [gsp, pl.BlockSpec((None, GDN_W, GDN_DH), lambda i: (N - 1 - i, 0, 0)), hsp,
                              pl.BlockSpec((GDN_C, GDN_W), lambda i: (N - 1 - i, PZ // GDN_W)), hsp, vec],
        out_specs=[hsp] * 5 + [pl.BlockSpec((8, 128), rev), hsp, vec],
        out_shape=[hshape] * 5 + [jax.ShapeDtypeStruct((N * 8, 128), F32), jax.ShapeDtypeStruct((S, GDN_W), BF16),
                                  jax.ShapeDtypeStruct((1, GDN_DH), F32)],
        scratch_shapes=[pltpu.VMEM((GDN_W, GDN_DH), F32)],
        compiler_params=_cp("arbitrary"),
    )(w, u, qd, kd, attn, gc, states, o, proj, dmixed, nw)


def _gdn_prep_bwd(qkv, proj, alog, dtb, tinv, dw, du, dqd, dkd, dat, dgl, name):
    S = qkv.shape[0]
    N = S // GDN_C

    def body(qkv_ref, ab_ref, al_ref, dt_ref, ti_ref, dw_ref, du_ref, dqd_ref, dkd_ref, dat_ref, dgl_ref,
             dqkv_ref, dab_ref, gal_ref, gdt_ref):
        @pl.when(pl.program_id(0) == 0)
        def _():
            gal_ref[...] = jnp.zeros_like(gal_ref)
            gdt_ref[...] = jnp.zeros_like(gdt_ref)

        ab = ab_ref[...]
        glog, vjp_g = jax.vjp(lambda a, al, dt: _each(_gdn_glog, a, al, dt), _cols(ab, 0), _cols(al_ref[...], 0),
                              _cols(dt_ref[...], 0))
        dec = _each(_gdn_decay, glog)
        gcol, dmat = _each(lambda d: d[0], dec), _each(lambda d: d[1], dec)
        _, vjp_c, _ = jax.vjp(functools.partial(_gdn_chunk, t_saved=_heads(ti_ref, width=GDN_C)), _heads(qkv_ref),
                              _heads(qkv_ref, GDN_H), _heads(qkv_ref, 2 * GDN_H), _cols(ab, GDN_H), gcol, dmat, has_aux=True)
        dqa, dka, dva, dbt, dgcol, dd = vjp_c((_heads(dw_ref), _heads(du_ref), _heads(dqd_ref), _heads(dkd_ref),
                                               _heads(dat_ref, width=GDN_C)))
        last = _rows(dgcol[0].shape) == GDN_C - 1
        dgcol = _each(lambda d, g: d + jnp.where(last, g, 0.0), dgcol, _cols(dgl_ref[0:1, :], 0))
        da_col, dal, ddt = vjp_g(_each(_gdn_decay_bwd, dgcol, dd))
        dab = jnp.zeros((GDN_C, 128), F32)
        gal = jnp.zeros((1, 128), F32)
        gdt = jnp.zeros((1, 128), F32)
        for h in range(GDN_H):
            dqkv_ref[:, _hsl(h)] = dqa[h]
            dqkv_ref[:, _hsl(GDN_H + h)] = dka[h]
            dqkv_ref[:, _hsl(2 * GDN_H + h)] = dva[h]
            ln = _lanes(dab.shape)
            dab = dab + jnp.where(ln == h, da_col[h], 0.0) + jnp.where(ln == GDN_H + h, dbt[h], 0.0)
            l1 = _lanes(gal.shape)
            gal = gal + jnp.where(l1 == h, dal[h], 0.0)
            gdt = gdt + jnp.where(l1 == h, ddt[h], 0.0)
        dab_ref[...] = dab.astype(dab_ref.dtype)
        gal_ref[...] += gal
        gdt_ref[...] += gdt

    vec = pl.BlockSpec((1, 128), lambda i: (0, 0))
    hsp = pl.BlockSpec((GDN_C, GDN_W), lambda i: (i, 0))
    qsp = pl.BlockSpec((GDN_C, 3 * GDN_W), lambda i: (i, 0))
    return pl.pallas_call(
        body, name=name, grid=(N,),
        in_specs=[qsp, pl.BlockSpec((GDN_C, 128), lambda i: (i, PAB // 128)), vec, vec] + [hsp] * 6
        + [pl.BlockSpec((8, 128), lambda i: (i, 0))],
        out_specs=[qsp, pl.BlockSpec((GDN_C, 128), lambda i: (i, 0)), vec, vec],
        out_shape=[jax.ShapeDtypeStruct((S, 3 * GDN_W), F32), jax.ShapeDtypeStruct((S, 128), BF16),
                   jax.ShapeDtypeStruct((1, 128), F32), jax.ShapeDtypeStruct((1, 128), F32)],
        compiler_params=_cp("arbitrary"),
    )(qkv, proj, alog, dtb, tinv, dw, du, dqd, dkd, dat, dgl)


@jax.custom_vjp
def _expm1(x):
    u = jnp.exp(x)
    lu = jnp.log(u)
    small = (u - 1.0) * x / jnp.where(u == 1.0, 1.0, lu)
    small = jnp.where(u == 1.0, x, small)
    return jnp.where(jnp.abs(x) < 0.5, small, u - 1.0)


def _expm1_fwd(x):
    return _expm1(x), jnp.exp(x)


def _expm1_bwd(ex, g):
    return (g * ex,)


_expm1.defvjp(_expm1_fwd, _expm1_bwd)


def _lru_gates(xc, wa, ba, wx, bx, lam, first):
    r = jax.nn.sigmoid(_mm(xc, wa) + ba)
    i = jax.nn.sigmoid(_mm(xc, wx) + bx)
    log_a = -LRU_C * r * jax.nn.softplus(-lam)
    mult = jnp.sqrt(-_expm1(2.0 * log_a))
    mult = jnp.where(first, 1.0, mult)
    return jnp.exp(log_a), mult * i * xc


def _scan_fwd(a, b):
    T = a.shape[0]
    rows = _rows(a.shape)
    s = 1
    while s < T:
        ok = rows >= s
        b = a * jnp.where(ok, _down(b, s), 0.0) + b
        a = a * jnp.where(ok, _down(a, s), 1.0)
        s *= 2
    return a, b


def _scan_rev(a, b):
    T = a.shape[0]
    rows = _rows(a.shape)
    s = 1
    while s < T:
        ok = rows + s < T
        b = a * jnp.where(ok, _up(b, s), 0.0) + b
        a = a * jnp.where(ok, _up(a, s), 1.0)
        s *= 2
    return b


def _bsl(j):
    return slice(j * LRU_BD, (j + 1) * LRU_BD)


def _lru_tile(S):
    return _row_tile(S, 256)


def _lru_fwd(proj, conv_w, conv_b, wa, ba, wx, bx, lam, name):
    S = proj.shape[0]
    T = _lru_tile(S)
    taps = conv_w.shape[0]
    r = T // HALO

    def body(x_ref, xp_ref, cw_ref, cb_ref, wa_ref, ba_ref, wx_ref, bx_ref, lam_ref, h_ref, carry_ref):
        i = pl.program_id(0)

        @pl.when(i == 0)
        def _():
            carry_ref[...] = jnp.zeros_like(carry_ref)

        ext = jnp.concatenate([jnp.where(i > 0, xp_ref[...], 0.0), x_ref[...]], axis=0)
        xc = _conv_rows(ext, cw_ref, taps)[HALO:] + cb_ref[...]
        first = (i * T + _rows((T, LRU_BD))) == 0
        for j in range(LRU_NB):
            a, b = _lru_gates(xc[:, _bsl(j)], wa_ref[j], ba_ref[:, _bsl(j)], wx_ref[j], bx_ref[:, _bsl(j)],
                              lam_ref[:, _bsl(j)], first=first)
            pa, hb = _scan_fwd(a, b)
            h_ref[:, _bsl(j)] = pa * carry_ref[0:1, _bsl(j)] + hb
            carry_ref[0:1, _bsl(j)] = h_ref[T - 1:T, _bsl(j)]

    vec = pl.BlockSpec((1, LRU_W), lambda i: (0, 0))
    wsp = pl.BlockSpec((LRU_NB, LRU_BD, LRU_BD), lambda i: (0, 0, 0))
    return pl.pallas_call(
        body, name=name, grid=(S // T,),
        in_specs=[pl.BlockSpec((T, LRU_W), lambda i: (i, PXR // LRU_W)),
                  pl.BlockSpec((HALO, LRU_W), lambda i: (jnp.maximum(i * r - 1, 0), PXR // LRU_W)),
                  pl.BlockSpec((taps, LRU_W), lambda i: (0, 0)), vec, wsp, vec, wsp, vec, vec],
        out_specs=pl.BlockSpec((T, LRU_W), lambda i: (i, 0)), out_shape=jax.ShapeDtypeStruct((S, LRU_W), F32),
        scratch_shapes=[pltpu.VMEM((8, LRU_W), F32)],
        compiler_params=_cp("arbitrary"),
    )(proj, proj, conv_w, conv_b.reshape(1, LRU_W), wa, ba.reshape(1, LRU_W), wx, bx.reshape(1, LRU_W), lam.reshape(1, LRU_W))


def _lru_bwd(proj, hl, dmixed, conv_w, conv_b, wa, ba, wx, bx, lam, name):
    S = proj.shape[0]
    T = _lru_tile(S)
    n = S // T
    taps = conv_w.shape[0]
    r = T // HALO
    mb = 768 // LRU_W

    def body(x_ref, xp_ref, g_ref, h_ref, hp_ref, dy_ref, cw_ref, cb_ref, wa_ref, ba_ref, wx_ref, bx_ref, lam_ref,
             dx_ref, dg_ref, gcw_ref, gcb_ref, gwa_ref, gba_ref, gwx_ref, gbx_ref, glam_ref, carry_ref, dxc_ref, nxt_ref):
        s = pl.program_id(0)
        i = n - 1 - s

        @pl.when(s == 0)
        def _():
            carry_ref[...] = jnp.zeros_like(carry_ref)
            nxt_ref[...] = jnp.zeros_like(nxt_ref)
            for ref in (gcw_ref, gcb_ref, gwa_ref, gba_ref, gwx_ref, gbx_ref, glam_ref):
                ref[...] = jnp.zeros_like(ref)

        ext = jnp.concatenate([jnp.where(i > 0, xp_ref[...], 0.0), x_ref[...]], axis=0)
        xc = _conv_rows(ext, cw_ref, taps)[HALO:] + cb_ref[...]
        rows = _rows((T, LRU_BD))
        first = (i * T + rows) == 0
        h_before = jnp.where(i > 0, hp_ref[HALO - 1:HALO, :], 0.0)
        for j in range(LRU_NB):
            sl = _bsl(j)
            (a, _), vjp_g = jax.vjp(functools.partial(_lru_gates, first=first), xc[:, sl], wa_ref[j], ba_ref[:, sl],
                                    wx_ref[j], bx_ref[:, sl], lam_ref[:, sl])
            gelu_g, vjp_a = jax.vjp(jax.nn.gelu, g_ref[:, sl])
            h = h_ref[:, sl]
            dy = dy_ref[:, sl]
            dg_ref[:, sl] = vjp_a(dy * h)[0].astype(dg_ref.dtype)
            b_rev = dy * gelu_g + jnp.where(rows == T - 1, carry_ref[0:1, sl], 0.0)
            a_rev = jnp.where(rows == T - 1, 0.0, _up(a, 1))
            dh = _scan_rev(a_rev, b_rev)
            carry_ref[:, sl] = (a * dh)[:HALO]
            h_prev = jnp.where(rows == 0, h_before[:, sl], _down(h, 1))
            dxc, dwa, dba, dwx, dbx, dlam = vjp_g((dh * h_prev, dh))
            dxc_ref[:, sl] = dxc
            gwa_ref[j] += dwa
            gwx_ref[j] += dwx
            gba_ref[:, sl] += dba
            gbx_ref[:, sl] += dbx
            glam_ref[:, sl] += dlam
        dxc = dxc_ref[...]
        d_ext = jnp.concatenate([dxc, nxt_ref[...]], axis=0)
        dx_ref[...] = _conv_t_rows(d_ext, cw_ref, taps)[:T].astype(dx_ref.dtype)
        nxt_ref[...] = dxc[:HALO]
        gcb_ref[...] += jnp.sum(dxc, axis=0, keepdims=True)
        for k in range(taps):
            gcw_ref[taps - 1 - k:taps - k, :] += jnp.sum(dxc * _down(ext, k)[HALO:], axis=0, keepdims=True)

    vec = pl.BlockSpec((1, LRU_W), lambda s: (0, 0))
    wsp = pl.BlockSpec((LRU_NB, LRU_BD, LRU_BD), lambda s: (0, 0, 0))
    cwsp = pl.BlockSpec((taps, LRU_W), lambda s: (0, 0))

    def main(cb):
        return pl.BlockSpec((T, LRU_W), lambda s: (n - 1 - s, cb))

    def prev(cb):
        return pl.BlockSpec((HALO, LRU_W), lambda s: (jnp.maximum((n - 1 - s) * r - 1, 0), cb))

    vshape = jax.ShapeDtypeStruct((1, LRU_W), F32)
    wshape = jax.ShapeDtypeStruct((LRU_NB, LRU_BD, LRU_BD), F32)
    return pl.pallas_call(
        body, name=name, grid=(n,),
        in_specs=[main(PXR // LRU_W), prev(PXR // LRU_W), main(PGR // LRU_W), main(0), prev(0), main(mb),
                  cwsp, vec, wsp, vec, wsp, vec, vec],
        out_specs=[main(0), main(0), cwsp, vec, wsp, vec, wsp, vec, vec],
        out_shape=[jax.ShapeDtypeStruct((S, LRU_W), BF16), jax.ShapeDtypeStruct((S, LRU_W), BF16),
                   jax.ShapeDtypeStruct((taps, LRU_W), F32), vshape, wshape, vshape, wshape, vshape, vshape],
        scratch_shapes=[pltpu.VMEM((8, LRU_W), F32), pltpu.VMEM((T, LRU_W), F32), pltpu.VMEM((HALO, LRU_W), F32)],
        compiler_params=_cp("arbitrary"),
    )(proj, proj, proj, hl, hl, dmixed, conv_w, conv_b.reshape(1, LRU_W), wa, ba.reshape(1, LRU_W), wx,
      bx.reshape(1, LRU_W), lam.reshape(1, LRU_W))


def _mix_out(o, proj, hl, y_pool, nw, name):
    S = o.shape[0]
    T = _row_tile(S)

    def body(o_ref, z_ref, h_ref, g_ref, p_ref, nw_ref, m_ref):
        for h in range(GDN_H):
            m_ref[:, _hsl(h)] = _gated_norm(o_ref[:, _hsl(h)], z_ref[:, _hsl(h)], nw_ref[...]).astype(m_ref.dtype)
        m_ref[:, GDN_W:GDN_W + LRU_W] = (h_ref[...] * jax.nn.gelu(g_ref[...])).astype(m_ref.dtype)
        m_ref[:, GDN_W + LRU_W:] = p_ref[...].astype(m_ref.dtype)

    row = pl.BlockSpec((T, GDN_W), lambda i: (i, 0))
    return pl.pallas_call(
        body, name=name, grid=(S // T,),
        in_specs=[row, pl.BlockSpec((T, GDN_W), lambda i: (i, PZ // GDN_W)), row,
                  pl.BlockSpec((T, LRU_W), lambda i: (i, PGR // LRU_W)), pl.BlockSpec((T, POOL_W), lambda i: (i, 0)),
                  pl.BlockSpec((1, GDN_DH), lambda i: (0, 0))],
        out_specs=pl.BlockSpec((T, D_MODEL), lambda i: (i, 0)), out_shape=jax.ShapeDtypeStruct((S, D_MODEL), BF16),
        compiler_params=_cp("parallel"),
    )(o, proj, hl, proj, y_pool, nw)


def _as2d(a):
    return a.reshape(-1, a.shape[-1])


def _ew_rows(rows, cols):
    t = rows
    while t * cols * 4 > (2 << 20) and t % 16 == 0:
        t //= 2
    return t


def _rs_rows(rows, cols, budget=2 << 20):
    t = rows
    while t * cols * 4 > budget and t % 32 == 0:
        t //= 2
    return t


def _adamw(w, g, m, v, name):
    shape = w.shape
    w2, g2, m2, v2 = _as2d(w), _as2d(g), _as2d(m), _as2d(v)
    rows, cols = w2.shape
    t = _ew_rows(rows, cols)

    def body(w_ref, g_ref, m_ref, v_ref, d_ref, nm_ref, nv_ref):
        gr = g_ref[...]
        nm = ADAM_B1 * m_ref[...] + (1.0 - ADAM_B1) * gr
        nv = ADAM_B2 * v_ref[...] + (1.0 - ADAM_B2) * (gr * gr)
        m_hat = nm / (1.0 - ADAM_B1 ** ADAM_STEP)
        v_hat = nv / (1.0 - ADAM_B2 ** ADAM_STEP)
        d_ref[...] = -ADAM_LR * (m_hat / (jnp.sqrt(v_hat) + ADAM_EPS) + ADAM_WD * w_ref[...])
        nm_ref[...] = nm
        nv_ref[...] = nv

    sp = pl.BlockSpec((t, cols), lambda i: (i, 0))
    sh = jax.ShapeDtypeStruct((rows, cols), F32)
    d, nm, nv = pl.pallas_call(body, name=name, grid=(rows // t,), in_specs=[sp] * 4, out_specs=[sp] * 3,
                               out_shape=[sh] * 3, compiler_params=_cp("parallel"))(w2, g2, m2, v2)
    return d.reshape(shape), nm.reshape(shape), nv.reshape(shape)


def _place():
    return lax.axis_index("x"), lax.axis_index("y"), lax.axis_index("c")


def _gather_weights(arrs, name):
    n = len(arrs)

    def body(*refs):
        outs = refs[n:2 * n]
        send, recv = refs[2 * n:]
        x, y, c = _place()
        s_me, s_x, s_y, s_d = 2 * x + y, 2 * (1 - x) + y, 2 * x + (1 - y), 2 * (1 - x) + (1 - y)
        xpeer, ypeer, sib = (1 - x, y, c), (x, 1 - y, c), (x, y, 1 - c)

        def rc(k, t, src, dst, to):
            return pltpu.make_async_remote_copy(src_ref=src, dst_ref=dst, send_sem=send.at[k, t], recv_sem=recv.at[k, t],
                                                device_id=to, device_id_type=MESH)

        def piece(k, s, top):
            rq = outs[k].shape[2] // 2
            return outs[k].at[s, c, pl.ds(0 if top else rq, rq)]

        sent = []

        def start(k, t, ref, to):
            cp = rc(k, t, ref, ref, to)
            cp.start()
            sent.append(cp)

        for k in range(n):
            start(k, 0, outs[k].at[s_me, c], xpeer)
            start(k, 1, outs[k].at[s_me, c], ypeer)
        for k in range(n):
            got = outs[k].at[s_x, c]
            rc(k, 0, got, got, xpeer).wait_recv()
            start(k, 2, piece(k, s_x, True), ypeer)
            start(k, 3, got, sib)
        for k in range(n):
            got = outs[k].at[s_y, c]
            rc(k, 1, got, got, ypeer).wait_recv()
            start(k, 6, piece(k, s_y, False), xpeer)
            start(k, 4, got, sib)
        for k in range(n):
            top, bottom = piece(k, s_d, True), piece(k, s_d, False)
            rc(k, 2, top, top, ypeer).wait_recv()
            rc(k, 6, bottom, bottom, xpeer).wait_recv()
            start(k, 5, outs[k].at[s_d, c], sib)
        for k in range(n):
            for t, s in ((3, s_x), (4, s_y), (5, s_d)):
                got = outs[k].at[s, 1 - c]
                rc(k, t, got, got, sib).wait_recv()
        for cp in sent:
            cp.wait_send()

    return pl.pallas_call(
        body, name=name, in_specs=[HBM] * n, out_specs=[HBM] * n,
        out_shape=[jax.ShapeDtypeStruct(a.shape, a.dtype) for a in arrs],
        input_output_aliases={k: k for k in range(n)},
        scratch_shapes=[pltpu.SemaphoreType.DMA((n, 7)), pltpu.SemaphoreType.DMA((n, 7))],
    )(*arrs)


def _share_halves(arrs, name):
    n = len(arrs)

    def body(*refs):
        outs = refs[n:2 * n]
        send, recv = refs[2 * n:]
        x, y, c = _place()
        cps = []
        for k in range(n):
            mine = outs[k].at[:, c]
            cp = pltpu.make_async_remote_copy(src_ref=mine, dst_ref=mine, send_sem=send.at[k], recv_sem=recv.at[k],
                                              device_id=(x, y, 1 - c), device_id_type=MESH)
            cp.start()
            cps.append(cp)
        for k in range(n):
            got = outs[k].at[:, 1 - c]
            pltpu.make_async_remote_copy(src_ref=got, dst_ref=got, send_sem=send.at[k], recv_sem=recv.at[k],
                                         device_id=(x, y, 1 - c), device_id_type=MESH).wait_recv()
        for cp in cps:
            cp.wait_send()

    return pl.pallas_call(
        body, name=name, in_specs=[HBM] * n, out_specs=[HBM] * n,
        out_shape=[jax.ShapeDtypeStruct(a.shape, a.dtype) for a in arrs],
        input_output_aliases={k: k for k in range(n)},
        scratch_shapes=[pltpu.SemaphoreType.DMA((n,)), pltpu.SemaphoreType.DMA((n,))],
    )(*arrs)


_REL = tuple((dx, dy, dc) for dx in (0, 1) for dy in (0, 1) for dc in (0, 1))[1:]
SEM = pl.BlockSpec(memory_space=pltpu.SEMAPHORE)
DATAFLOW = pltpu.SideEffectType.DATAFLOW_SIDE_EFFECTING


def _flip(v, d):
    return 1 - v if d else v


def _rs_direct_copies(srcs, land, send, recv):
    x, y, c = _place()
    cps = []
    for k in range(len(srcs)):
        for r, (dx, dy, dc) in enumerate(_REL):
            px, py, pc = _flip(x, dx), _flip(y, dy), _flip(c, dc)
            cps.append(pltpu.make_async_remote_copy(
                src_ref=srcs[k].at[2 * px + py, pc], dst_ref=land[k].at[r], send_sem=send.at[k * len(_REL) + r],
                recv_sem=recv.at[k * len(_REL) + r], device_id=(px, py, pc), device_id_type=MESH))
    return cps


def _rs_direct_start(grads, thru, name):
    n = len(grads)
    lands = [pltpu.with_memory_space_constraint(lax.empty((len(_REL),) + g.shape[2:], g.dtype), pltpu.HBM) for g in grads]

    def body(*refs):
        for cp in _rs_direct_copies(refs[:n], refs[n + 1:2 * n + 1], refs[2 * n + 1], refs[2 * n + 2]):
            cp.start()

    sems = pltpu.SemaphoreType.DMA((n * len(_REL),))
    keep = [pltpu.HBM(a.shape, a.dtype) for a in (*grads, thru, *lands)]
    out = pl.pallas_call(
        body, name=name, in_specs=[HBM] * (2 * n + 1), out_specs=(SEM, SEM) + (HBM,) * (2 * n + 1),
        out_shape=(sems, sems, *keep), input_output_aliases={i: 2 + i for i in range(2 * n + 1)},
        compiler_params=pltpu.CompilerParams(has_side_effects=DATAFLOW),
    )(*[pltpu.with_memory_space_constraint(a, pltpu.HBM) for a in (*grads, thru)], *lands)
    return out[0], out[1], out[2:2 + n], out[2 + n], out[3 + n:]


def _rs_direct_wait(send, recv, grads, lands, after, name):
    n = len(grads)
    after = list(after) if isinstance(after, (list, tuple)) else [after]

    def body(*refs):
        for cp in _rs_direct_copies(refs[:n], refs[n:2 * n], refs[2 * n], refs[2 * n + 1]):
            cp.wait_send()
            cp.wait_recv()

    keep = [pltpu.HBM(a.shape, a.dtype) for a in (*grads, *lands)]
    out = pl.pallas_call(
        body, name=name, in_specs=[HBM] * (2 * n) + [SEM, SEM] + [pl.BlockSpec(memory_space=pl.ANY)] * len(after),
        out_specs=(HBM,) * (2 * n), out_shape=tuple(keep), input_output_aliases={i: i for i in range(2 * n)},
        compiler_params=pltpu.CompilerParams(has_side_effects=DATAFLOW),
    )(*grads, *lands, send, recv, *after)
    return out[:n], out[n:]


_CHIPS = ((1, 0), (0, 1), (1, 1))


def _gather_ici_copies(bufs, send, recv):
    x, y, c = _place()
    cps = []
    for k in range(len(bufs)):
        mine = bufs[k].at[2 * x + y, c]
        for j, (dx, dy) in enumerate(_CHIPS):
            cps.append(pltpu.make_async_remote_copy(
                src_ref=mine, dst_ref=mine, send_sem=send.at[k * len(_CHIPS) + j], recv_sem=recv.at[k * len(_CHIPS) + j],
                device_id=(_flip(x, dx), _flip(y, dy), c), device_id_type=MESH))
    return cps


def _gather_d2d_copies(bufs, send, recv):
    x, y, c = _place()
    cps = []
    for k in range(len(bufs)):
        for j, (dx, dy) in enumerate(_CHIPS):
            got = bufs[k].at[2 * _flip(x, dx) + _flip(y, dy), c]
            cps.append(pltpu.make_async_remote_copy(
                src_ref=got, dst_ref=got, send_sem=send.at[k * len(_CHIPS) + j], recv_sem=recv.at[k * len(_CHIPS) + j],
                device_id=(x, y, 1 - c), device_id_type=MESH))
    return cps


def _copies_start(bufs, thru, copies, name):
    n = len(bufs)

    def body(*refs):
        for cp in copies(refs[:n], refs[n + 1], refs[n + 2]):
            cp.start()

    sems = pltpu.SemaphoreType.DMA((n * len(_CHIPS),))
    out = pl.pallas_call(
        body, name=name, in_specs=[HBM] * (n + 1), out_specs=(SEM, SEM) + (HBM,) * (n + 1),
        out_shape=(sems, sems, *[pltpu.HBM(a.shape, a.dtype) for a in (*bufs, thru)]),
        input_output_aliases={i: 2 + i for i in range(n + 1)},
        compiler_params=pltpu.CompilerParams(has_side_effects=DATAFLOW),
    )(*[pltpu.with_memory_space_constraint(a, pltpu.HBM) for a in (*bufs, thru)])
    return out[0], out[1], out[2:2 + n], out[2 + n]


def _copies_wait(send, recv, bufs, after, copies, name):
    n = len(bufs)

    def body(*refs):
        for cp in copies(refs[:n], refs[n], refs[n + 1]):
            cp.wait_send()
            cp.wait_recv()

    return pl.pallas_call(
        body, name=name, in_specs=[HBM] * n + [SEM, SEM, pl.BlockSpec(memory_space=pl.ANY)], out_specs=(HBM,) * n,
        out_shape=tuple(pltpu.HBM(a.shape, a.dtype) for a in bufs), input_output_aliases={i: i for i in range(n)},
        compiler_params=pltpu.CompilerParams(has_side_effects=DATAFLOW),
    )(*bufs, send, recv, after)


def _rs_direct_sum(grad, land, layer, name, into=None):
    _, _, rows, cols = grad.shape
    t = _rs_rows(rows, cols, 6 << 20)
    npieces = len(_REL) + 1

    def body(g_ref, l_ref, *rest):
        o_ref, acc_ref = rest[-2], rest[-1]
        j = pl.program_id(1)

        @pl.when(j == 0)
        def _():
            acc_ref[...] = g_ref[...].astype(F32)

        @pl.when(j > 0)
        def _():
            acc_ref[...] += l_ref[...].astype(F32)

        @pl.when(j == npieces - 1)
        def _():
            o_ref[...] = acc_ref[...]

    def mine(i, j):
        x, y, c = _place()
        return (2 * x + y, c, i, 0)

    in_specs = [pl.BlockSpec((None, None, t, cols), mine),
                pl.BlockSpec((None, t, cols), lambda i, j: (jnp.maximum(j - 1, 0), i, 0))]
    args = [grad, land]
    if into is not None:
        in_specs.append(pl.BlockSpec(memory_space=pl.ANY))
        args.append(into)
    return pl.pallas_call(
        body, name=name, grid=(rows // t, npieces), in_specs=in_specs,
        out_specs=pl.BlockSpec((None, None, t, cols), lambda i, j: (layer, lax.axis_index("c"), i, 0)),
        scratch_shapes=[pltpu.VMEM((t, cols), F32)],
        out_shape=jax.ShapeDtypeStruct((2, 2, rows, cols), F32), input_output_aliases={} if into is None else {2: 0},
        compiler_params=_cp("parallel", "arbitrary"),
    )(*args)


def _ar_copies(buf, land, send, recv):
    x, y, c = _place()
    return [pltpu.make_async_remote_copy(src_ref=buf, dst_ref=land.at[r], send_sem=send.at[r], recv_sem=recv.at[r],
                                         device_id=(_flip(x, dx), _flip(y, dy), _flip(c, dc)), device_id_type=MESH)
            for r, (dx, dy, dc) in enumerate(_REL)]


def _ar_start(buf, name, thru=None):
    land = pltpu.with_memory_space_constraint(lax.empty((len(_REL),) + buf.shape, buf.dtype), pltpu.HBM)
    extra = [] if thru is None else [thru]

    def body(buf_ref, land_ref, *rest):
        send, recv = rest[len(extra)], rest[len(extra) + 1]
        for cp in _ar_copies(buf_ref, land_ref, send, recv):
            cp.start()

    sems = pltpu.SemaphoreType.DMA((len(_REL),))
    return pl.pallas_call(
        body, name=name, in_specs=[HBM] * (2 + len(extra)), out_specs=(SEM, SEM) + (HBM,) * (2 + len(extra)),
        out_shape=(sems, sems, *[pltpu.HBM(a.shape, a.dtype) for a in (buf, land, *extra)]),
        input_output_aliases={i: 2 + i for i in range(2 + len(extra))},
        compiler_params=pltpu.CompilerParams(has_side_effects=DATAFLOW),
    )(pltpu.with_memory_space_constraint(buf, pltpu.HBM), land, *[pltpu.with_memory_space_constraint(a, pltpu.HBM) for a in extra])


def _ar_wait(send, recv, buf, land, after, name):
    def body(buf_ref, land_ref, send_ref, recv_ref, *_):
        for cp in _ar_copies(buf_ref, land_ref, send_ref, recv_ref):
            cp.wait_send()
            cp.wait_recv()

    return pl.pallas_call(
        body, name=name, in_specs=[HBM, HBM, SEM, SEM, pl.BlockSpec(memory_space=pl.ANY)], out_specs=(HBM, HBM),
        out_shape=(pltpu.HBM(buf.shape, buf.dtype), pltpu.HBM(land.shape, land.dtype)), input_output_aliases={0: 0, 1: 1},
        compiler_params=pltpu.CompilerParams(has_side_effects=DATAFLOW),
    )(buf, land, send, recv, after)


def _ar_sum(buf, land, name):
    rows, cols = buf.shape
    t = _rs_rows(rows, cols)

    def slot(i, j):
        x, y, c = _place()
        xd, yd, cd = j // 4, (j // 2) % 2, j % 2
        rel = 4 * (x + xd - 2 * x * xd) + 2 * (y + yd - 2 * y * yd) + (c + cd - 2 * c * cd)
        return (jnp.maximum(rel - 1, 0), i, 0)

    def body(b_ref, l_ref, o_ref, acc_ref):
        j = pl.program_id(1)
        x, y, c = _place()
        val = jnp.where(j == 4 * x + 2 * y + c, b_ref[...], l_ref[...])

        @pl.when(j == 0)
        def _():
            acc_ref[...] = val

        @pl.when(j > 0)
        def _():
            acc_ref[...] += val

        @pl.when(j == len(_REL))
        def _():
            o_ref[...] = acc_ref[...]

    sp = pl.BlockSpec((t, cols), lambda i, j: (i, 0))
    return pl.pallas_call(
        body, name=name, grid=(rows // t, len(_REL) + 1), in_specs=[sp, pl.BlockSpec((None, t, cols), slot)], out_specs=sp,
        out_shape=jax.ShapeDtypeStruct((rows, cols), F32), scratch_shapes=[pltpu.VMEM((t, cols), F32)],
        compiler_params=_cp("parallel", "arbitrary"),
    )(buf, land)


def _pad128(v):
    return jnp.zeros((1, 128), F32).at[0, :v.shape[0]].set(v)


def _layer_fwd(l, x, p, hooks=None):
    hooks = hooks or {}
    h1 = _rms_fwd(x, p["norm1_w"], f"rms1_{l}")
    proj = _matmul(h1, p["w_in"], "nn", name=f"mm_in_{l}", tn=768)
    y_pool = _pool_fwd(proj, p["pool_w"], p["pool_b"], p["pool_scale"], f"pool_{l}")
    qkv = _gdn_conv_fwd(proj, p["gdn_conv_w"], f"gconv_{l}")
    alog, dtb = _pad128(p["gdn_a_log"]), _pad128(p["gdn_dt_bias"])
    gw, gu, gqd, gkd, gat, tinv, gc = _gdn_prep(qkv, proj, alog, dtb, f"gprep_{l}")
    o, states = _gdn_scan(gw, gu, gqd, gkd, gat, gc, f"gscan_{l}")
    if "mid" in hooks:
        o = hooks["mid"](o)
    hl = _lru_fwd(proj, p["lru_conv_w"], p["lru_conv_b"], p["lru_wa"], p["lru_ba"], p["lru_wx"], p["lru_bx"],
                  p["lru_lambda"], f"lru_{l}")
    mixed = _mix_out(o, proj, hl, y_pool, p["gdn_norm_w"].reshape(1, GDN_DH), f"mix_{l}")
    x2 = _matmul(mixed, p["w_out"], "nn", name=f"mm_out_{l}", res=x)
    h2 = _rms_fwd(x2, p["norm2_w"], f"rms2_{l}")
    if "ffn" in hooks:
        h2 = hooks["ffn"](h2)
    up = _matmul(h2, p["ffn_up"], "nn", name=f"mm_up_{l}", b_split=True)
    act = _ffn_act_fwd(up, p["ffn_conv_w"], f"ffn_{l}")
    if "down" in hooks:
        act = hooks["down"](act)
    x3 = _matmul(act, p["ffn_down"], "nn", name=f"mm_down_{l}", res=x2, tk=3072)
    saved = dict(x=x, h1=h1, proj=proj, qkv=qkv, gdn=(gw, gu, gqd, gkd, gat, gc), tinv=tinv, states=states, o=o, hl=hl, mixed=mixed,
                 x2=x2, h2=h2, up=up, act=act, alog=alog, dtb=dtb)
    return x3, saved


def _layer_bwd(l, dx3, p, s, after_ffn=None, after_mix=None):
    g = {}
    dact = _matmul(dx3, p["ffn_down"], "nt", name=f"mm_ddown_{l}")
    g["ffn_down"] = _matmul(s["act"], dx3, "tn", name=f"mm_gdown_{l}", out_dtype=BF16)
    dup, g["ffn_conv_w"] = _ffn_act_bwd(s["up"], dact, p["ffn_conv_w"], f"ffn_b_{l}")
    dh2 = _matmul(dup, p["ffn_up"], "nt", name=f"mm_dup_{l}", b_split=True, tk=3072)
    g["ffn_up"] = _matmul(s["h2"], dup, "tn", name=f"mm_gup_{l}", b_split=True, o_split=4, tk=4096, out_dtype=BF16)
    dx2, g["norm2_w"] = _rms_bwd(s["x2"], p["norm2_w"], dh2, dx3, f"rms2_b_{l}")
    g["w_out"] = _matmul(s["mixed"], dx2, "tn", name=f"mm_gout_{l}", out_dtype=BF16)
    if after_ffn is not None:
        dx2 = after_ffn(dx2, g)
    dmixed = _matmul(dx2, p["w_out"], "nt", name=f"mm_dout_{l}")
    proj = s["proj"]
    du_pool, g["pool_w"], g["pool_b"], g["pool_scale"] = _pool_bwd(proj, dmixed, p["pool_w"], p["pool_b"], p["pool_scale"], f"pool_b_{l}")
    gw, gu, gqd, gkd, gat, gc = s["gdn"]
    dw, du, dqd, dkd, dat, dgl, dz, g["gdn_norm_w"] = _gdn_scan_bwd(
        gw, gu, gqd, gkd, gat, gc, s["states"], s["o"], proj, dmixed, p["gdn_norm_w"].reshape(1, GDN_DH), f"gscan_b_{l}")
    dqkv, dab, gal, gdt = _gdn_prep_bwd(s["qkv"], proj, s["alog"], s["dtb"], s["tinv"], dw, du, dqd, dkd, dat, dgl, f"gprep_b_{l}")
    g["gdn_a_log"], g["gdn_dt_bias"] = gal[0, :GDN_H], gdt[0, :GDN_H]
    dpre, g["gdn_conv_w"] = _gdn_conv_bwd(proj, dqkv, p["gdn_conv_w"], f"gconv_b_{l}")
    (dxr, dgr, g["lru_conv_w"], g["lru_conv_b"], g["lru_wa"], g["lru_ba"], g["lru_wx"], g["lru_bx"], g["lru_lambda"]) = _lru_bwd(
        proj, s["hl"], dmixed, p["lru_conv_w"], p["lru_conv_b"], p["lru_wa"], p["lru_ba"], p["lru_wx"], p["lru_bx"],
        p["lru_lambda"], f"lru_b_{l}")
    S = proj.shape[0]
    dproj = jnp.concatenate([dpre, dz, dxr, dgr, du_pool, dab, jnp.zeros((S, PCOLS - PAB - 128), BF16)], axis=1)
    g["w_in"] = _matmul(s["h1"], dproj, "tn", name=f"mm_gin_{l}", tn=768, tk=4096, out_dtype=BF16)
    if after_mix is not None:
        dproj = after_mix(dproj, g)
    dh1 = _matmul(dproj, p["w_in"], "nt", name=f"mm_din_{l}", tk=2688)
    dx, g["norm1_w"] = _rms_bwd(s["x"], p["norm1_w"], dh1, dx2, f"rms1_b_{l}")
    return dx, g


_IN_PERM = ((512, 3584), (3596, 5132), (0, 512), (3584, 3596))


def _w_in_to_proj(w):
    parts = [w[:, a:b] for a, b in _IN_PERM]
    return jnp.concatenate(parts + [jnp.zeros((w.shape[0], PCOLS - IN_COLS), w.dtype)], axis=1)


def _proj_to_w_in(g):
    return jnp.concatenate([g[:, PPOOL:PPOOL + 512], g[:, 0:3072], g[:, PAB:PAB + 12], g[:, 3072:PPOOL]], axis=1)


def _rows_to_mixed(w):
    return jnp.concatenate([w[512:], w[:512]], axis=0)


def _mixed_to_rows(g):
    return jnp.concatenate([g[1536:], g[:1536]], axis=0)


SMALL_SHARDED = ("gdn_conv_w", "lru_conv_w", "ffn_conv_w")
BIG = ("w_in", "w_out", "ffn_up", "ffn_down")
SMALL_REPLICATED = ("norm1_w", "pool_w", "pool_b", "pool_scale", "gdn_a_log", "gdn_dt_bias", "gdn_norm_w", "lru_conv_b",
                    "lru_wa", "lru_ba", "lru_wx", "lru_bx", "lru_lambda", "norm2_w")
WEIGHTS = ("norm1_w", "w_in", "pool_w", "pool_b", "pool_scale", "gdn_conv_w", "gdn_a_log", "gdn_dt_bias", "gdn_norm_w",
           "lru_conv_w", "lru_conv_b", "lru_wa", "lru_ba", "lru_wx", "lru_bx", "lru_lambda", "w_out", "norm2_w", "ffn_up",
           "ffn_conv_w", "ffn_down", "final_norm_w")
FLAT_COLS = 1024


def _pack(arrs):
    flat = jnp.concatenate([a.reshape(-1) for a in arrs])
    rows = -(-flat.shape[0] // (8 * FLAT_COLS)) * 8
    return jnp.pad(flat, (0, rows * FLAT_COLS - flat.shape[0])).reshape(rows, FLAT_COLS)


def _unpack(buf, like):
    flat = buf.reshape(-1)
    out, off = [], 0
    for a in like:
        size = 1
        for d in a.shape:
            size *= d
        out.append(flat[off:off + size].reshape(a.shape))
        off += size
    return out


def kernel(x, norm1_w, w_in, pool_w, pool_b, pool_scale, gdn_conv_w, gdn_a_log, gdn_dt_bias, gdn_norm_w, lru_conv_w, lru_conv_b, lru_wa, lru_ba, lru_wx, lru_bx, lru_lambda, w_out, norm2_w, ffn_up, ffn_conv_w, ffn_down, final_norm_w, loss_target, m_norm1_w, m_w_in, m_pool_w, m_pool_b, m_pool_scale, m_gdn_conv_w, m_gdn_a_log, m_gdn_dt_bias, m_gdn_norm_w, m_lru_conv_w, m_lru_conv_b, m_lru_wa, m_lru_ba, m_lru_wx, m_lru_bx, m_lru_lambda, m_w_out, m_norm2_w, m_ffn_up, m_ffn_conv_w, m_ffn_down, m_final_norm_w, v_norm1_w, v_w_in, v_pool_w, v_pool_b, v_pool_scale, v_gdn_conv_w, v_gdn_a_log, v_gdn_dt_bias, v_gdn_norm_w, v_lru_conv_w, v_lru_conv_b, v_lru_wa, v_lru_ba, v_lru_wx, v_lru_bx, v_lru_lambda, v_w_out, v_norm2_w, v_ffn_up, v_ffn_conv_w, v_ffn_down, v_final_norm_w):
    W = dict(norm1_w=norm1_w, w_in=w_in, pool_w=pool_w, pool_b=pool_b, pool_scale=pool_scale, gdn_conv_w=gdn_conv_w,
             gdn_a_log=gdn_a_log, gdn_dt_bias=gdn_dt_bias, gdn_norm_w=gdn_norm_w, lru_conv_w=lru_conv_w, lru_conv_b=lru_conv_b,
             lru_wa=lru_wa, lru_ba=lru_ba, lru_wx=lru_wx, lru_bx=lru_bx, lru_lambda=lru_lambda, w_out=w_out, norm2_w=norm2_w,
             ffn_up=ffn_up, ffn_conv_w=ffn_conv_w, ffn_down=ffn_down, final_norm_w=final_norm_w)
    M = dict(norm1_w=m_norm1_w, w_in=m_w_in, pool_w=m_pool_w, pool_b=m_pool_b, pool_scale=m_pool_scale, gdn_conv_w=m_gdn_conv_w,
             gdn_a_log=m_gdn_a_log, gdn_dt_bias=m_gdn_dt_bias, gdn_norm_w=m_gdn_norm_w, lru_conv_w=m_lru_conv_w,
             lru_conv_b=m_lru_conv_b, lru_wa=m_lru_wa, lru_ba=m_lru_ba, lru_wx=m_lru_wx, lru_bx=m_lru_bx, lru_lambda=m_lru_lambda,
             w_out=m_w_out, norm2_w=m_norm2_w, ffn_up=m_ffn_up, ffn_conv_w=m_ffn_conv_w, ffn_down=m_ffn_down,
             final_norm_w=m_final_norm_w)
    V = dict(norm1_w=v_norm1_w, w_in=v_w_in, pool_w=v_pool_w, pool_b=v_pool_b, pool_scale=v_pool_scale, gdn_conv_w=v_gdn_conv_w,
             gdn_a_log=v_gdn_a_log, gdn_dt_bias=v_gdn_dt_bias, gdn_norm_w=v_gdn_norm_w, lru_conv_w=v_lru_conv_w,
             lru_conv_b=v_lru_conv_b, lru_wa=v_lru_wa, lru_ba=v_lru_ba, lru_wx=v_lru_wx, lru_bx=v_lru_bx, lru_lambda=v_lru_lambda,
             w_out=v_w_out, norm2_w=v_norm2_w, ffn_up=v_ffn_up, ffn_conv_w=v_ffn_conv_w, ffn_down=v_ffn_down,
             final_norm_w=v_final_norm_w)
    S = x.shape[1]
    xs = x.reshape(S, D_MODEL)
    tgt = loss_target.reshape(S, D_MODEL)
    mx, my, mc = _place()
    shard = 2 * mx + my

    small_sh = jnp.concatenate([W[k].reshape(N_LAYERS, -1) for k in SMALL_SHARDED], axis=1)
    n_small = small_sh.shape[1]
    pad = -n_small % 1024
    small_sh = jnp.pad(small_sh, ((0, 0), (0, pad))).reshape(N_LAYERS, -1, 1024)

    def own_slots(l):
        out = []
        for w in [W[k][l].astype(BF16) for k in BIG] + [small_sh[l]]:
            buf = lax.dynamic_update_slice(lax.empty((4,) + w.shape, w.dtype), w[None], (shard,) + (0,) * w.ndim)
            out.append(buf.reshape(4, 2, w.shape[0] // 2, w.shape[1]))
        return out

    def whole(g):
        return g.reshape(4, 2 * g.shape[2], g.shape[3])

    def mixer_params(l, g_in, g_out, g_small):
        p = {k: W[k][l] for k in SMALL_REPLICATED}
        g_in = whole(g_in)
        p["w_in"] = _w_in_to_proj(jnp.transpose(g_in, (1, 0, 2)).reshape(g_in.shape[1], IN_COLS))
        p["w_out"] = _rows_to_mixed(whole(g_out).reshape(D_MODEL, D_MODEL))
        g_small = whole(g_small).reshape(4, -1)[:, :n_small]
        off = 0
        for k in SMALL_SHARDED:
            taps, width = W[k].shape[1], W[k].shape[2]
            piece = g_small[:, off:off + taps * width].reshape(4, taps, width)
            p[k] = jnp.transpose(piece, (1, 0, 2)).reshape(taps, 4 * width)
            off += taps * width
        return p

    def ffn_params(g_up, g_down):
        return dict(ffn_up=whole(g_up), ffn_down=whole(g_down).reshape(D_FF, D_MODEL))

    layers, saved = [None] * N_LAYERS, [None] * N_LAYERS
    s0 = own_slots(0)
    g_in0, g_out0, g_small0 = _gather_weights([s0[0], s0[1], s0[4]], "gather_weights")
    f_send, f_recv, ffn0, g_in0 = _copies_start(s0[2:4], g_in0, _gather_ici_copies, "gather_ffn0_ici_start")
    l_send, l_recv, bufs1, g_in0 = _copies_start(own_slots(1), g_in0, _gather_ici_copies, "gather_l1_ici_start")
    layers[0] = mixer_params(0, g_in0, g_out0, g_small0)
    stage = {}

    def mid(o):
        bufs = _copies_wait(f_send, f_recv, ffn0, o, _gather_ici_copies, "gather_ffn0_ici_wait")
        stage["ffn0"] = _copies_start(bufs, o, _gather_d2d_copies, "gather_ffn0_d2d_start")
        return stage["ffn0"][3]

    def ffn(h2):
        send, recv, bufs, _ = stage["ffn0"]
        layers[0].update(ffn_params(*_copies_wait(send, recv, bufs, h2, _gather_d2d_copies, "gather_ffn0_d2d_wait")))
        return h2

    def down(act):
        bufs = _copies_wait(l_send, l_recv, bufs1, act, _gather_ici_copies, "gather_l1_ici_wait")
        stage["l1"] = _copies_start(bufs, act, _gather_d2d_copies, "gather_l1_d2d_start")
        return stage["l1"][3]

    h, saved[0] = _layer_fwd(0, xs, layers[0], dict(mid=mid, ffn=ffn, down=down))
    send, recv, bufs, _ = stage["l1"]
    g1 = _copies_wait(send, recv, bufs, h, _gather_d2d_copies, "gather_l1_d2d_wait")
    layers[1] = {**mixer_params(1, g1[0], g1[1], g1[4]), **ffn_params(g1[2], g1[3])}
    h, saved[1] = _layer_fwd(1, h, layers[1])
    loss_part, dh, g_final = _loss_head(h, final_norm_w, tgt, "loss_head")

    def big_partials(g_layer, names=BIG):
        out = []
        for k in names:
            g = g_layer[k]
            if k == "w_in":
                g = _proj_to_w_in(g)
                g = jnp.transpose(g.reshape(g.shape[0], 4, IN_COLS // 4), (1, 0, 2))
            elif k == "w_out":
                g = _mixed_to_rows(g).reshape(4, D_MODEL // 4, D_MODEL)
            elif k == "ffn_down":
                g = g.reshape(4, D_FF // 4, D_MODEL)
            out.append(g.reshape(4, 2, g.shape[1] // 2, g.shape[2]))
        return out

    FFN, MIX = ("ffn_up", "ffn_down", "w_out"), ("w_in",)
    grads = [None] * N_LAYERS
    dh, grads[1] = _layer_bwd(1, dh, layers[1], saved[1])
    send1, recv1, part1, dh, lands1 = _rs_direct_start(big_partials(grads[1]), dh, "rs_direct_start_1")
    small_names = SMALL_REPLICATED + SMALL_SHARDED

    def small_of(l):
        return [grads[l][k].reshape(W[k].shape[1:]) if k in SMALL_REPLICATED else grads[l][k] for k in small_names]

    small_list1 = small_of(1) + [g_final.reshape(D_MODEL), loss_part[0, 0:1]]
    ar1 = _ar_start(_pack(small_list1), "ar_start_1")
    sent0 = []

    def after_ffn(dx2, g):
        send0, recv0, part0, dx2, lands0 = _rs_direct_start(big_partials(g, FFN), dx2, "rs_direct_start_0")
        sent0.extend([send0, recv0, part0, lands0])
        return dx2

    sentm = []

    def after_mix(dproj, g):
        sendm, recvm, partm, dproj, landsm = _rs_direct_start(big_partials(g, MIX), dproj, "rs_direct_start_0m")
        sentm.extend([sendm, recvm, partm, landsm])
        early = [g[k].reshape(W[k].shape[1:]) if k in SMALL_REPLICATED else g[k] for k in small_early]
        *ar, dproj = _ar_start(_pack(early), "ar_start_0a", thru=dproj)
        sentm.extend([early, ar])
        return dproj

    small_early = tuple(k for k in small_names if k != "norm1_w")
    dh, grads[0] = _layer_bwd(0, dh, layers[0], saved[0], after_ffn, after_mix)
    sendm, recvm, partm, landsm, small_list0a, ar0a = sentm
    small_list0b = [grads[0]["norm1_w"].reshape(D_MODEL)]
    ar0b = _ar_start(_pack(small_list0b), "ar_start_0b")

    part1, lands1 = _rs_direct_wait(send1, recv1, part1, lands1, dh, "rs_direct_wait_1")
    part0, lands0 = _rs_direct_wait(*sent0, dh, "rs_direct_wait_0")
    red = {k: _rs_direct_sum(g, ld, 1, f"rs_sum_1_{k}") for k, g, ld in zip(BIG, part1, lands1)}
    for k, g, ld in zip(FFN, part0, lands0):
        red[k] = _rs_direct_sum(g, ld, 0, f"rs_sum_0_{k}", into=red[k])
    G, DELTA, NM, NV = {}, {}, {}, {}

    def update_big(names, shared):
        for k, r in zip(names, shared):
            G[k] = r.reshape(N_LAYERS, 2 * r.shape[2], r.shape[3])
            DELTA[k], NM[k], NV[k] = _adamw(W[k], G[k], M[k], V[k], f"adam_{k}")

    update_big(FFN, _share_halves([red[k] for k in FFN], "rs_share_ffn"))
    partm, landsm = _rs_direct_wait(sendm, recvm, partm, landsm, DELTA[FFN[-1]], "rs_direct_wait_0m")
    for k, g, ld in zip(MIX, partm, landsm):
        red[k] = _rs_direct_sum(g, ld, 0, f"rs_sum_0_{k}", into=red[k])
    update_big(MIX, _share_halves([red[k] for k in MIX], "rs_share_mix"))
    grad_x = dh.reshape(x.shape)

    red1 = _unpack(_ar_sum(*_ar_wait(*ar1, G[MIX[-1]], "ar_wait_1"), "ar_sum_1"), small_list1)
    red0 = dict(zip(small_early, _unpack(_ar_sum(*_ar_wait(*ar0a, G[MIX[-1]], "ar_wait_0a"), "ar_sum_0a"), small_list0a)))
    red0["norm1_w"] = _unpack(_ar_sum(*_ar_wait(*ar0b, G[MIX[-1]], "ar_wait_0b"), "ar_sum_0b"), small_list0b)[0]
    small_g = {k: jnp.stack([red0[k], r1]) for k, r1 in zip(small_names, red1)}
    small_g["final_norm_w"] = red1[-2]
    loss = red1[-1][0]
    for k in SMALL_SHARDED:
        width = W[k].shape[2]
        small_g[k] = lax.dynamic_slice_in_dim(small_g[k], shard * width, width, axis=2)

    small_all = small_names + ("final_norm_w",)
    dl, nm, nv = _adamw(_pack([W[k] for k in small_all]), _pack([small_g[k] for k in small_all]),
                        _pack([M[k] for k in small_all]), _pack([V[k] for k in small_all]), "adam_small")
    like = [W[k] for k in small_all]
    for k, d_, m_, v_ in zip(small_all, _unpack(dl, like), _unpack(nm, like), _unpack(nv, like)):
        G[k], DELTA[k], NM[k], NV[k] = small_g[k], d_, m_, v_

    return (loss, grad_x, *[G[k] for k in WEIGHTS], *[DELTA[k] for k in WEIGHTS], *[NM[k] for k in WEIGHTS],
            *[NV[k] for k in WEIGHTS])
```
